```python
import math
import jax, jax.numpy as jnp
from jax import lax
import numpy as np

D_MODEL = 1024
BATCH = 8
SEQ = 8192
DEPTH = 2

CHUNK = 64
N_MEM = 256
SB_HEADS = 8
SB_HEAD_DIM = 64
SB_WIDTH = SB_HEADS * SB_HEAD_DIM
SB_BLOCK = 128
CONV_CH = 256
CONV_WIDTH = 31
SSM_CH = 256
SSM_GROUP = 16
SSM_GROUPS = SSM_CH // SSM_GROUP
SSM_STATE = 64
MIX_WIDTH = SB_WIDTH + CONV_CH + SSM_CH
IN_PROJ = 3 * SB_WIDTH + 2 * CONV_CH + SSM_CH
XA_HEADS = 4
XA_HEAD_DIM = D_MODEL // XA_HEADS
XA_WIDTH = XA_HEADS * XA_HEAD_DIM
FFN_HIDDEN = ((int(math.ceil(8 * D_MODEL / 3)) + 255) // 256) * 256
EPS = 1e-6

kernel_name = "hybrid_sb_conformer_s5_encoder"


def rms_norm(x, g):
    xf = x.astype(jnp.float32)
    y = xf * lax.rsqrt(jnp.mean(xf * xf, axis=-1, keepdims=True) + EPS)
    return (y * g.astype(jnp.float32)).astype(x.dtype)


def layer_norm(x, g, b):
    xf = x.astype(jnp.float32)
    mu = jnp.mean(xf, axis=-1, keepdims=True)
    var = jnp.mean(jnp.square(xf - mu), axis=-1, keepdims=True)
    y = (xf - mu) * lax.rsqrt(var + EPS)
    return (y * g.astype(jnp.float32) + b.astype(jnp.float32)).astype(x.dtype)


def stick_breaking_attention(q, k, v):
    bsz, L, H, hd = q.shape
    nb = L // SB_BLOCK
    kh = k.transpose(0, 2, 1, 3)
    vh = v.transpose(0, 2, 1, 3)
    q_blocks = q.transpose(0, 2, 1, 3).reshape(bsz, H, nb, SB_BLOCK, hd).transpose(2, 0, 1, 3, 4)
    key_pos = jnp.arange(L)
    scale = hd ** -0.5

    def one_block(args):
        qb, blk = args
        z = jnp.einsum('bhqd,bhkd->bhqk', qb, kh).astype(jnp.float32) * scale
        q_pos = blk * SB_BLOCK + jnp.arange(SB_BLOCK)
        mask = key_pos[None, :] < q_pos[:, None]
        log_1mb = jnp.where(mask, jax.nn.log_sigmoid(-z), 0.0)
        later = lax.cumsum(log_1mb, axis=3, reverse=True) - log_1mb
        w = jnp.where(mask, jnp.exp(jax.nn.log_sigmoid(z) + later), 0.0)
        return jnp.einsum('bhqk,bhkd->bhqd', w.astype(vh.dtype), vh)

    out = lax.map(one_block, (q_blocks, jnp.arange(nb)))
    return out.transpose(1, 0, 3, 2, 4).reshape(bsz, L, H * hd)


def conformer_conv(u2, dw_w, dw_b, ln_g, ln_b, pw2_w):
    a, b = jnp.split(u2, 2, axis=-1)
    h = a * jax.nn.sigmoid(b)
    h = lax.conv_general_dilated(
        h, dw_w[:, None, :].astype(h.dtype), window_strides=(1,),
        padding=((CONV_WIDTH - 1, 0),), dimension_numbers=('NWC', 'WIO', 'NWC'),
        feature_group_count=CONV_CH) + dw_b
    h = jax.nn.silu(layer_norm(h, ln_g, ln_b))
    return h @ pw2_w


def s5_ssm(u, lam_re, lam_im, log_dt, b_re, b_im, c_re, c_im, d, glu_w):
    bsz, L, _ = u.shape
    f32 = jnp.float32
    uf = u.astype(f32).reshape(bsz, L, SSM_GROUPS, SSM_GROUP)
    lr, li = lam_re.astype(f32), lam_im.astype(f32)
    dt = jnp.exp(log_dt.astype(f32))[:, None]
    mag = jnp.exp(lr * dt)
    ar, ai = mag * jnp.cos(li * dt), mag * jnp.sin(li * dt)
    den = lr * lr + li * li
    fr = ((ar - 1.0) * lr + ai * li) / den
    fi = (ai * lr - (ar - 1.0) * li) / den
    br, bi = b_re.astype(f32), b_im.astype(f32)
    bbr = fr[..., None] * br - fi[..., None] * bi
    bbi = fr[..., None] * bi + fi[..., None] * br
    bu_r = jnp.einsum('blgh,gph->blgp', uf, bbr)
    bu_i = jnp.einsum('blgh,gph->blgp', uf, bbi)
    shape = bu_r.shape
    a_r = jnp.broadcast_to(ar, shape)
    a_i = jnp.broadcast_to(ai, shape)

    def combine(e1, e2):
        a1r, a1i, b1r, b1i = e1
        a2r, a2i, b2r, b2i = e2
        return (a2r * a1r - a2i * a1i,
                a2r * a1i + a2i * a1r,
                a2r * b1r - a2i * b1i + b2r,
                a2r * b1i + a2i * b1r + b2i)

    _, _, xr, xi = lax.associative_scan(combine, (a_r, a_i, bu_r, bu_i), axis=1)
    y = (jnp.einsum('blgp,ghp->blgh', xr, c_re.astype(f32))
         - jnp.einsum('blgp,ghp->blgh', xi, c_im.astype(f32)))
    y = y.reshape(bsz, L, SSM_CH) + d.astype(f32) * uf.reshape(bsz, L, SSM_CH)
    y = y.astype(u.dtype)
    ya, yb = jnp.split(y @ glu_w, 2, axis=-1)
    return ya * jax.nn.sigmoid(yb)


def memory_cross_attention(h, m, wq, wk, wv, q_g, k_g, wo):
    bsz, L, _ = h.shape
    q = rms_norm((h @ wq).reshape(bsz, L, XA_HEADS, XA_HEAD_DIM), q_g)
    k = rms_norm((m @ wk).reshape(bsz, N_MEM, XA_HEADS, XA_HEAD_DIM), k_g)
    v = (m @ wv).reshape(bsz, N_MEM, XA_HEADS, XA_HEAD_DIM)
    s = jnp.einsum('blhd,bmhd->bhlm', q, k).astype(jnp.float32) * (XA_HEAD_DIM ** -0.5)
    p = jax.nn.softmax(s, axis=-1).astype(v.dtype)
    o = jnp.einsum('bhlm,bmhd->blhd', p, v).reshape(bsz, L, XA_WIDTH)
    return o @ wo


def _fwd_setup_inputs(seed: int = 0) -> dict:
    key = jax.random.key(seed)
    ks = iter(jax.random.split(key, 40))
    f32 = jnp.float32

    def nrm(shape, scale):
        return jax.random.normal(next(ks), shape, f32) * scale

    def gain(shape):
        return 1.0 + 0.02 * jax.random.normal(next(ks), shape, f32)

    n_idx = jnp.arange(SSM_STATE, dtype=f32)
    return {
        "x": nrm((BATCH, SEQ, D_MODEL), 1.0),
        "mem": nrm((BATCH, N_MEM, D_MODEL), 1.0),
        "norm_mix_g": gain((DEPTH, D_MODEL)),
        "w_in": nrm((DEPTH, D_MODEL, IN_PROJ), D_MODEL ** -0.5),
        "sb_q_norm_g": gain((DEPTH, SB_HEAD_DIM)),
        "sb_k_norm_g": gain((DEPTH, SB_HEAD_DIM)),
        "conv_dw_w": nrm((DEPTH, CONV_WIDTH, CONV_CH), CONV_WIDTH ** -0.5),
        "conv_dw_b": nrm((DEPTH, CONV_CH), 0.02),
        "conv_ln_g": gain((DEPTH, CONV_CH)),
        "conv_ln_b": nrm((DEPTH, CONV_CH), 0.02),
        "conv_pw2_w": nrm((DEPTH, CONV_CH, CONV_CH), CONV_CH ** -0.5),
        "ssm_lam_re": -0.5 * jnp.exp(nrm((DEPTH, SSM_GROUPS, SSM_STATE), 0.05)),
        "ssm_lam_im": jnp.pi * n_idx * jnp.exp(nrm((DEPTH, SSM_GROUPS, SSM_STATE), 0.01)),
        "ssm_log_dt": jax.random.uniform(next(ks), (DEPTH, SSM_GROUPS), f32,
                                         math.log(1e-3), math.log(1e-1)),
        "ssm_b_re": nrm((DEPTH, SSM_GROUPS, SSM_STATE, SSM_GROUP), (2 * SSM_GROUP) ** -0.5),
        "ssm_b_im": nrm((DEPTH, SSM_GROUPS, SSM_STATE, SSM_GROUP), (2 * SSM_GROUP) ** -0.5),
        "ssm_c_re": nrm((DEPTH, SSM_GROUPS, SSM_GROUP, SSM_STATE), (2 * SSM_STATE) ** -0.5),
        "ssm_c_im": nrm((DEPTH, SSM_GROUPS, SSM_GROUP, SSM_STATE), (2 * SSM_STATE) ** -0.5),
        "ssm_d": nrm((DEPTH, SSM_CH), 1.0),
        "ssm_glu_w": nrm((DEPTH, SSM_CH, 2 * SSM_CH), SSM_CH ** -0.5),
        "branch_norm_g": gain((DEPTH, MIX_WIDTH)),
        "w_out": nrm((DEPTH, MIX_WIDTH, D_MODEL), MIX_WIDTH ** -0.5),
        "norm_xa_g": gain((DEPTH, D_MODEL)),
        "norm_mem_g": gain((DEPTH, D_MODEL)),
        "xa_wq": nrm((DEPTH, D_MODEL, XA_WIDTH), D_MODEL ** -0.5),
        "xa_wk": nrm((DEPTH, D_MODEL, XA_WIDTH), D_MODEL ** -0.5),
        "xa_wv": nrm((DEPTH, D_MODEL, XA_WIDTH), D_MODEL ** -0.5),
        "xa_q_norm_g": gain((DEPTH, XA_HEAD_DIM)),
        "xa_k_norm_g": gain((DEPTH, XA_HEAD_DIM)),
        "xa_wo": nrm((DEPTH, XA_WIDTH, D_MODEL), XA_WIDTH ** -0.5),
        "norm_ffn_g": gain((DEPTH, D_MODEL)),
        "ffn_w_in": nrm((DEPTH, D_MODEL, 2 * FFN_HIDDEN), D_MODEL ** -0.5),
        "ffn_w_out": nrm((DEPTH, FFN_HIDDEN, D_MODEL), FFN_HIDDEN ** -0.5),
    }


def _fwd_reference(x, mem, norm_mix_g, w_in, sb_q_norm_g, sb_k_norm_g, conv_dw_w, conv_dw_b,
              conv_ln_g, conv_ln_b, conv_pw2_w, ssm_lam_re, ssm_lam_im, ssm_log_dt,
              ssm_b_re, ssm_b_im, ssm_c_re, ssm_c_im, ssm_d, ssm_glu_w, branch_norm_g,
              w_out, norm_xa_g, norm_mem_g, xa_wq, xa_wk, xa_wv, xa_q_norm_g, xa_k_norm_g,
              xa_wo, norm_ffn_g, ffn_w_in, ffn_w_out):
    bsz, L, _ = x.shape
    s1 = SB_WIDTH
    s2 = 2 * SB_WIDTH
    s3 = 3 * SB_WIDTH
    s4 = s3 + 2 * CONV_CH
    for l in range(DEPTH):
        h = rms_norm(x, norm_mix_g[l])
        p = h @ w_in[l]
        q = rms_norm(p[..., :s1].reshape(bsz, L, SB_HEADS, SB_HEAD_DIM), sb_q_norm_g[l])
        k = rms_norm(p[..., s1:s2].reshape(bsz, L, SB_HEADS, SB_HEAD_DIM), sb_k_norm_g[l])
        v = p[..., s2:s3].reshape(bsz, L, SB_HEADS, SB_HEAD_DIM)
        o_sb = stick_breaking_attention(q, k, v)
        o_conv = conformer_conv(p[..., s3:s4], conv_dw_w[l], conv_dw_b[l],
                                conv_ln_g[l], conv_ln_b[l], conv_pw2_w[l])
        o_ssm = s5_ssm(p[..., s4:], ssm_lam_re[l], ssm_lam_im[l], ssm_log_dt[l],
                       ssm_b_re[l], ssm_b_im[l], ssm_c_re[l], ssm_c_im[l],
                       ssm_d[l], ssm_glu_w[l])
        g = branch_norm_g[l]
        mixed = jnp.concatenate([
            rms_norm(o_sb, g[:SB_WIDTH]),
            rms_norm(o_conv, g[SB_WIDTH:SB_WIDTH + CONV_CH]),
            rms_norm(o_ssm, g[SB_WIDTH + CONV_CH:]),
        ], axis=-1)
        x = x + mixed @ w_out[l]
        hx = rms_norm(x, norm_xa_g[l])
        hm = rms_norm(mem, norm_mem_g[l])
        x = x + memory_cross_attention(hx, hm, xa_wq[l], xa_wk[l], xa_wv[l],
                                       xa_q_norm_g[l], xa_k_norm_g[l], xa_wo[l])
        hf = rms_norm(x, norm_ffn_g[l])
        gate, up = jnp.split(hf @ ffn_w_in[l], 2, axis=-1)
        x = x + (jax.nn.silu(gate) * up) @ ffn_w_out[l]
    return x


import jax as _jax
import jax.numpy as _jnp

TWIN_FORMAT = 'train_step'
FWD_PARAMS = ['x', 'mem', 'norm_mix_g', 'w_in', 'sb_q_norm_g', 'sb_k_norm_g', 'conv_dw_w', 'conv_dw_b', 'conv_ln_g', 'conv_ln_b', 'conv_pw2_w', 'ssm_lam_re', 'ssm_lam_im', 'ssm_log_dt', 'ssm_b_re', 'ssm_b_im', 'ssm_c_re', 'ssm_c_im', 'ssm_d', 'ssm_glu_w', 'branch_norm_g', 'w_out', 'norm_xa_g', 'norm_mem_g', 'xa_wq', 'xa_wk', 'xa_wv', 'xa_q_norm_g', 'xa_k_norm_g', 'xa_wo', 'norm_ffn_g', 'ffn_w_in', 'ffn_w_out']
TWIN_WEIGHTS = ['norm_mix_g', 'w_in', 'sb_q_norm_g', 'sb_k_norm_g', 'conv_dw_w', 'conv_dw_b', 'conv_ln_g', 'conv_ln_b', 'conv_pw2_w', 'ssm_lam_re', 'ssm_lam_im', 'ssm_log_dt', 'ssm_b_re', 'ssm_b_im', 'ssm_c_re', 'ssm_c_im', 'ssm_d', 'ssm_glu_w', 'branch_norm_g', 'w_out', 'norm_xa_g', 'norm_mem_g', 'xa_wq', 'xa_wk', 'xa_wv', 'xa_q_norm_g', 'xa_k_norm_g', 'xa_wo', 'norm_ffn_g', 'ffn_w_in', 'ffn_w_out']
TWIN_DIFF_INPUT = 'x'
TWIN_INPUTS = ['x', 'mem', 'norm_mix_g', 'w_in', 'sb_q_norm_g', 'sb_k_norm_g', 'conv_dw_w', 'conv_dw_b', 'conv_ln_g', 'conv_ln_b', 'conv_pw2_w', 'ssm_lam_re', 'ssm_lam_im', 'ssm_log_dt', 'ssm_b_re', 'ssm_b_im', 'ssm_c_re', 'ssm_c_im', 'ssm_d', 'ssm_glu_w', 'branch_norm_g', 'w_out', 'norm_xa_g', 'norm_mem_g', 'xa_wq', 'xa_wk', 'xa_wv', 'xa_q_norm_g', 'xa_k_norm_g', 'xa_wo', 'norm_ffn_g', 'ffn_w_in', 'ffn_w_out', 'loss_target', 'm_norm_mix_g', 'm_w_in', 'm_sb_q_norm_g', 'm_sb_k_norm_g', 'm_conv_dw_w', 'm_conv_dw_b', 'm_conv_ln_g', 'm_conv_ln_b', 'm_conv_pw2_w', 'm_ssm_lam_re', 'm_ssm_lam_im', 'm_ssm_log_dt', 'm_ssm_b_re', 'm_ssm_b_im', 'm_ssm_c_re', 'm_ssm_c_im', 'm_ssm_d', 'm_ssm_glu_w', 'm_branch_norm_g', 'm_w_out', 'm_norm_xa_g', 'm_norm_mem_g', 'm_xa_wq', 'm_xa_wk', 'm_xa_wv', 'm_xa_q_norm_g', 'm_xa_k_norm_g', 'm_xa_wo', 'm_norm_ffn_g', 'm_ffn_w_in', 'm_ffn_w_out', 'v_norm_mix_g', 'v_w_in', 'v_sb_q_norm_g', 'v_sb_k_norm_g', 'v_conv_dw_w', 'v_conv_dw_b', 'v_conv_ln_g', 'v_conv_ln_b', 'v_conv_pw2_w', 'v_ssm_lam_re', 'v_ssm_lam_im', 'v_ssm_log_dt', 'v_ssm_b_re', 'v_ssm_b_im', 'v_ssm_c_re', 'v_ssm_c_im', 'v_ssm_d', 'v_ssm_glu_w', 'v_branch_norm_g', 'v_w_out', 'v_norm_xa_g', 'v_norm_mem_g', 'v_xa_wq', 'v_xa_wk', 'v_xa_wv', 'v_xa_q_norm_g', 'v_xa_k_norm_g', 'v_xa_wo', 'v_norm_ffn_g', 'v_ffn_w_in', 'v_ffn_w_out']
TWIN_OUTPUTS = ['loss', 'grad_x', 'grad_norm_mix_g', 'grad_w_in', 'grad_sb_q_norm_g', 'grad_sb_k_norm_g', 'grad_conv_dw_w', 'grad_conv_dw_b', 'grad_conv_ln_g', 'grad_conv_ln_b', 'grad_conv_pw2_w', 'grad_ssm_lam_re', 'grad_ssm_lam_im', 'grad_ssm_log_dt', 'grad_ssm_b_re', 'grad_ssm_b_im', 'grad_ssm_c_re', 'grad_ssm_c_im', 'grad_ssm_d', 'grad_ssm_glu_w', 'grad_branch_norm_g', 'grad_w_out', 'grad_norm_xa_g', 'grad_norm_mem_g', 'grad_xa_wq', 'grad_xa_wk', 'grad_xa_wv', 'grad_xa_q_norm_g', 'grad_xa_k_norm_g', 'grad_xa_wo', 'grad_norm_ffn_g', 'grad_ffn_w_in', 'grad_ffn_w_out', 'delta_norm_mix_g', 'delta_w_in', 'delta_sb_q_norm_g', 'delta_sb_k_norm_g', 'delta_conv_dw_w', 'delta_conv_dw_b', 'delta_conv_ln_g', 'delta_conv_ln_b', 'delta_conv_pw2_w', 'delta_ssm_lam_re', 'delta_ssm_lam_im', 'delta_ssm_log_dt', 'delta_ssm_b_re', 'delta_ssm_b_im', 'delta_ssm_c_re', 'delta_ssm_c_im', 'delta_ssm_d', 'delta_ssm_glu_w', 'delta_branch_norm_g', 'delta_w_out', 'delta_norm_xa_g', 'delta_norm_mem_g', 'delta_xa_wq', 'delta_xa_wk', 'delta_xa_wv', 'delta_xa_q_norm_g', 'delta_xa_k_norm_g', 'delta_xa_wo', 'delta_norm_ffn_g', 'delta_ffn_w_in', 'delta_ffn_w_out', 'new_m_norm_mix_g', 'new_m_w_in', 'new_m_sb_q_norm_g', 'new_m_sb_k_norm_g', 'new_m_conv_dw_w', 'new_m_conv_dw_b', 'new_m_conv_ln_g', 'new_m_conv_ln_b', 'new_m_conv_pw2_w', 'new_m_ssm_lam_re', 'new_m_ssm_lam_im', 'new_m_ssm_log_dt', 'new_m_ssm_b_re', 'new_m_ssm_b_im', 'new_m_ssm_c_re', 'new_m_ssm_c_im', 'new_m_ssm_d', 'new_m_ssm_glu_w', 'new_m_branch_norm_g', 'new_m_w_out', 'new_m_norm_xa_g', 'new_m_norm_mem_g', 'new_m_xa_wq', 'new_m_xa_wk', 'new_m_xa_wv', 'new_m_xa_q_norm_g', 'new_m_xa_k_norm_g', 'new_m_xa_wo', 'new_m_norm_ffn_g', 'new_m_ffn_w_in', 'new_m_ffn_w_out', 'new_v_norm_mix_g', 'new_v_w_in', 'new_v_sb_q_norm_g', 'new_v_sb_k_norm_g', 'new_v_conv_dw_w', 'new_v_conv_dw_b', 'new_v_conv_ln_g', 'new_v_conv_ln_b', 'new_v_conv_pw2_w', 'new_v_ssm_lam_re', 'new_v_ssm_lam_im', 'new_v_ssm_log_dt', 'new_v_ssm_b_re', 'new_v_ssm_b_im', 'new_v_ssm_c_re', 'new_v_ssm_c_im', 'new_v_ssm_d', 'new_v_ssm_glu_w', 'new_v_branch_norm_g', 'new_v_w_out', 'new_v_norm_xa_g', 'new_v_norm_mem_g', 'new_v_xa_wq', 'new_v_xa_wk', 'new_v_xa_wv', 'new_v_xa_q_norm_g', 'new_v_xa_k_norm_g', 'new_v_xa_wo', 'new_v_norm_ffn_g', 'new_v_ffn_w_in', 'new_v_ffn_w_out']
TWIN_LEAF_KINDS = {'loss': 'loss', 'grad_x': 'grad_x', 'grad_norm_mix_g': 'grad_w', 'grad_w_in': 'grad_w', 'grad_sb_q_norm_g': 'grad_w', 'grad_sb_k_norm_g': 'grad_w', 'grad_conv_dw_w': 'grad_w', 'grad_conv_dw_b': 'grad_w', 'grad_conv_ln_g': 'grad_w', 'grad_conv_ln_b': 'grad_w', 'grad_conv_pw2_w': 'grad_w', 'grad_ssm_lam_re': 'grad_w', 'grad_ssm_lam_im': 'grad_w', 'grad_ssm_log_dt': 'grad_w', 'grad_ssm_b_re': 'grad_w', 'grad_ssm_b_im': 'grad_w', 'grad_ssm_c_re': 'grad_w', 'grad_ssm_c_im': 'grad_w', 'grad_ssm_d': 'grad_w', 'grad_ssm_glu_w': 'grad_w', 'grad_branch_norm_g': 'grad_w', 'grad_w_out': 'grad_w', 'grad_norm_xa_g': 'grad_w', 'grad_norm_mem_g': 'grad_w', 'grad_xa_wq': 'grad_w', 'grad_xa_wk': 'grad_w', 'grad_xa_wv': 'grad_w', 'grad_xa_q_norm_g': 'grad_w', 'grad_xa_k_norm_g': 'grad_w', 'grad_xa_wo': 'grad_w', 'grad_norm_ffn_g': 'grad_w', 'grad_ffn_w_in': 'grad_w', 'grad_ffn_w_out': 'grad_w', 'delta_norm_mix_g': 'delta_w', 'delta_w_in': 'delta_w', 'delta_sb_q_norm_g': 'delta_w', 'delta_sb_k_norm_g': 'delta_w', 'delta_conv_dw_w': 'delta_w', 'delta_conv_dw_b': 'delta_w', 'delta_conv_ln_g': 'delta_w', 'delta_conv_ln_b': 'delta_w', 'delta_conv_pw2_w': 'delta_w', 'delta_ssm_lam_re': 'delta_w', 'delta_ssm_lam_im': 'delta_w', 'delta_ssm_log_dt': 'delta_w', 'delta_ssm_b_re': 'delta_w', 'delta_ssm_b_im': 'delta_w', 'delta_ssm_c_re': 'delta_w', 'delta_ssm_c_im': 'delta_w', 'delta_ssm_d': 'delta_w', 'delta_ssm_glu_w': 'delta_w', 'delta_branch_norm_g': 'delta_w', 'delta_w_out': 'delta_w', 'delta_norm_xa_g': 'delta_w', 'delta_norm_mem_g': 'delta_w', 'delta_xa_wq': 'delta_w', 'delta_xa_wk': 'delta_w', 'delta_xa_wv': 'delta_w', 'delta_xa_q_norm_g': 'delta_w', 'delta_xa_k_norm_g': 'delta_w', 'delta_xa_wo': 'delta_w', 'delta_norm_ffn_g': 'delta_w', 'delta_ffn_w_in': 'delta_w', 'delta_ffn_w_out': 'delta_w', 'new_m_norm_mix_g': 'new_m', 'new_m_w_in': 'new_m', 'new_m_sb_q_norm_g': 'new_m', 'new_m_sb_k_norm_g': 'new_m', 'new_m_conv_dw_w': 'new_m', 'new_m_conv_dw_b': 'new_m', 'new_m_conv_ln_g': 'new_m', 'new_m_conv_ln_b': 'new_m', 'new_m_conv_pw2_w': 'new_m', 'new_m_ssm_lam_re': 'new_m', 'new_m_ssm_lam_im': 'new_m', 'new_m_ssm_log_dt': 'new_m', 'new_m_ssm_b_re': 'new_m', 'new_m_ssm_b_im': 'new_m', 'new_m_ssm_c_re': 'new_m', 'new_m_ssm_c_im': 'new_m', 'new_m_ssm_d': 'new_m', 'new_m_ssm_glu_w': 'new_m', 'new_m_branch_norm_g': 'new_m', 'new_m_w_out': 'new_m', 'new_m_norm_xa_g': 'new_m', 'new_m_norm_mem_g': 'new_m', 'new_m_xa_wq': 'new_m', 'new_m_xa_wk': 'new_m', 'new_m_xa_wv': 'new_m', 'new_m_xa_q_norm_g': 'new_m', 'new_m_xa_k_norm_g': 'new_m', 'new_m_xa_wo': 'new_m', 'new_m_norm_ffn_g': 'new_m', 'new_m_ffn_w_in': 'new_m', 'new_m_ffn_w_out': 'new_m', 'new_v_norm_mix_g': 'new_v', 'new_v_w_in': 'new_v', 'new_v_sb_q_norm_g': 'new_v', 'new_v_sb_k_norm_g': 'new_v', 'new_v_conv_dw_w': 'new_v', 'new_v_conv_dw_b': 'new_v', 'new_v_conv_ln_g': 'new_v', 'new_v_conv_ln_b': 'new_v', 'new_v_conv_pw2_w': 'new_v', 'new_v_ssm_lam_re': 'new_v', 'new_v_ssm_lam_im': 'new_v', 'new_v_ssm_log_dt': 'new_v', 'new_v_ssm_b_re': 'new_v', 'new_v_ssm_b_im': 'new_v', 'new_v_ssm_c_re': 'new_v', 'new_v_ssm_c_im': 'new_v', 'new_v_ssm_d': 'new_v', 'new_v_ssm_glu_w': 'new_v', 'new_v_branch_norm_g': 'new_v', 'new_v_w_out': 'new_v', 'new_v_norm_xa_g': 'new_v', 'new_v_norm_mem_g': 'new_v', 'new_v_xa_wq': 'new_v', 'new_v_xa_wk': 'new_v', 'new_v_xa_wv': 'new_v', 'new_v_xa_q_norm_g': 'new_v', 'new_v_xa_k_norm_g': 'new_v', 'new_v_xa_wo': 'new_v', 'new_v_norm_ffn_g': 'new_v', 'new_v_ffn_w_in': 'new_v', 'new_v_ffn_w_out': 'new_v'}


def _forward(args):
    return _fwd_reference(*[args[k] for k in FWD_PARAMS])


def _output_shape():
    def fwd():
        inp = _fwd_setup_inputs(0)
        return _fwd_reference(*[inp[k] for k in FWD_PARAMS])
    out = _jax.eval_shape(fwd)
    return out.shape, out.dtype

N_MICROBATCH = 1
ADAM_LR = 0.001
ADAM_B1 = 0.9
ADAM_B2 = 0.999
ADAM_EPS = 1e-08
ADAM_WD = 0.01
ADAM_STEP = 10
PER_EXAMPLE_BATCH_AXIS = {'x': 0, 'mem': 0, 'loss_target': 0}
SHARED_INPUTS = []
_WEIGHT_DTYPES = {'norm_mix_g': _jnp.float32, 'w_in': _jnp.float32, 'sb_q_norm_g': _jnp.float32, 'sb_k_norm_g': _jnp.float32, 'conv_dw_w': _jnp.float32, 'conv_dw_b': _jnp.float32, 'conv_ln_g': _jnp.float32, 'conv_ln_b': _jnp.float32, 'conv_pw2_w': _jnp.float32, 'ssm_lam_re': _jnp.float32, 'ssm_lam_im': _jnp.float32, 'ssm_log_dt': _jnp.float32, 'ssm_b_re': _jnp.float32, 'ssm_b_im': _jnp.float32, 'ssm_c_re': _jnp.float32, 'ssm_c_im': _jnp.float32, 'ssm_d': _jnp.float32, 'ssm_glu_w': _jnp.float32, 'branch_norm_g': _jnp.float32, 'w_out': _jnp.float32, 'norm_xa_g': _jnp.float32, 'norm_mem_g': _jnp.float32, 'xa_wq': _jnp.float32, 'xa_wk': _jnp.float32, 'xa_wv': _jnp.float32, 'xa_q_norm_g': _jnp.float32, 'xa_k_norm_g': _jnp.float32, 'xa_wo': _jnp.float32, 'norm_ffn_g': _jnp.float32, 'ffn_w_in': _jnp.float32, 'ffn_w_out': _jnp.float32}
MOMENT_SCALE = {'norm_mix_g': 2.173492e+00, 'w_in': 1.396477e+00, 'sb_q_norm_g': 1.238739e+00, 'sb_k_norm_g': 1.223705e+00, 'conv_dw_w': 1.987418e+00, 'conv_dw_b': 3.274414e+01, 'conv_ln_g': 1.311839e+01, 'conv_ln_b': 1.916287e+01, 'conv_pw2_w': 8.238201e+00, 'ssm_lam_re': 2.292637e-01, 'ssm_lam_im': 2.430656e-01, 'ssm_log_dt': 5.387536e+01, 'ssm_b_re': 1.082142e-01, 'ssm_b_im': 1.033858e-01, 'ssm_c_re': 1.775095e-01, 'ssm_c_im': 2.060802e-01, 'ssm_d': 3.876765e+00, 'ssm_glu_w': 3.004414e+00, 'branch_norm_g': 6.488291e+01, 'w_out': 4.955316e+00, 'norm_xa_g': 1.591430e-01, 'norm_mem_g': 8.069625e-01, 'xa_wq': 1.544984e-01, 'xa_wk': 1.529816e-01, 'xa_wv': 4.854269e-01, 'xa_q_norm_g': 2.443468e+00, 'xa_k_norm_g': 2.442616e+00, 'xa_wo': 4.685426e-01, 'norm_ffn_g': 4.934843e+01, 'ffn_w_in': 7.265847e-01, 'ffn_w_out': 1.006776e+00}


def _to_microbatches(a, axis):
    t = _jnp.moveaxis(a, axis, 0)
    t = t.reshape((N_MICROBATCH, t.shape[0] // N_MICROBATCH) + t.shape[1:])
    return _jnp.moveaxis(t, 1, axis + 1)


def setup_inputs(seed: int = 0) -> dict:
    inp = _fwd_setup_inputs(seed)
    key = _jax.random.fold_in(_jax.random.key(seed), 7919)
    shape, _ = _output_shape()
    out = dict(inp)
    out["loss_target"] = _jax.random.normal(_jax.random.fold_in(key, 0), shape, _jnp.float32)
    for i, name in enumerate(TWIN_WEIGHTS):
        w = inp[name].astype(_jnp.float32)
        if MOMENT_SCALE is None:
            s = _jnp.sqrt(_jnp.mean(_jnp.square(w)) + 1e-30)
        else:
            s = MOMENT_SCALE[name]
        km, kv = _jax.random.split(_jax.random.fold_in(key, i + 1))
        out[name] = w
        out["m_" + name] = s * _jax.random.normal(km, w.shape, _jnp.float32)
        out["v_" + name] = (s * s) * _jax.random.uniform(kv, w.shape, _jnp.float32, 0.5, 1.5)
    if N_MICROBATCH > 1:
        for name, axis in PER_EXAMPLE_BATCH_AXIS.items():
            out[name] = _to_microbatches(out[name], axis)
    return {'x': out['x'], 'mem': out['mem'], 'norm_mix_g': out['norm_mix_g'], 'w_in': out['w_in'], 'sb_q_norm_g': out['sb_q_norm_g'], 'sb_k_norm_g': out['sb_k_norm_g'], 'conv_dw_w': out['conv_dw_w'], 'conv_dw_b': out['conv_dw_b'], 'conv_ln_g': out['conv_ln_g'], 'conv_ln_b': out['conv_ln_b'], 'conv_pw2_w': out['conv_pw2_w'], 'ssm_lam_re': out['ssm_lam_re'], 'ssm_lam_im': out['ssm_lam_im'], 'ssm_log_dt': out['ssm_log_dt'], 'ssm_b_re': out['ssm_b_re'], 'ssm_b_im': out['ssm_b_im'], 'ssm_c_re': out['ssm_c_re'], 'ssm_c_im': out['ssm_c_im'], 'ssm_d': out['ssm_d'], 'ssm_glu_w': out['ssm_glu_w'], 'branch_norm_g': out['branch_norm_g'], 'w_out': out['w_out'], 'norm_xa_g': out['norm_xa_g'], 'norm_mem_g': out['norm_mem_g'], 'xa_wq': out['xa_wq'], 'xa_wk': out['xa_wk'], 'xa_wv': out['xa_wv'], 'xa_q_norm_g': out['xa_q_norm_g'], 'xa_k_norm_g': out['xa_k_norm_g'], 'xa_wo': out['xa_wo'], 'norm_ffn_g': out['norm_ffn_g'], 'ffn_w_in': out['ffn_w_in'], 'ffn_w_out': out['ffn_w_out'], 'loss_target': out['loss_target'], 'm_norm_mix_g': out['m_norm_mix_g'], 'm_w_in': out['m_w_in'], 'm_sb_q_norm_g': out['m_sb_q_norm_g'], 'm_sb_k_norm_g': out['m_sb_k_norm_g'], 'm_conv_dw_w': out['m_conv_dw_w'], 'm_conv_dw_b': out['m_conv_dw_b'], 'm_conv_ln_g': out['m_conv_ln_g'], 'm_conv_ln_b': out['m_conv_ln_b'], 'm_conv_pw2_w': out['m_conv_pw2_w'], 'm_ssm_lam_re': out['m_ssm_lam_re'], 'm_ssm_lam_im': out['m_ssm_lam_im'], 'm_ssm_log_dt': out['m_ssm_log_dt'], 'm_ssm_b_re': out['m_ssm_b_re'], 'm_ssm_b_im': out['m_ssm_b_im'], 'm_ssm_c_re': out['m_ssm_c_re'], 'm_ssm_c_im': out['m_ssm_c_im'], 'm_ssm_d': out['m_ssm_d'], 'm_ssm_glu_w': out['m_ssm_glu_w'], 'm_branch_norm_g': out['m_branch_norm_g'], 'm_w_out': out['m_w_out'], 'm_norm_xa_g': out['m_norm_xa_g'], 'm_norm_mem_g': out['m_norm_mem_g'], 'm_xa_wq': out['m_xa_wq'], 'm_xa_wk': out['m_xa_wk'], 'm_xa_wv': out['m_xa_wv'], 'm_xa_q_norm_g': out['m_xa_q_norm_g'], 'm_xa_k_norm_g': out['m_xa_k_norm_g'], 'm_xa_wo': out['m_xa_wo'], 'm_norm_ffn_g': out['m_norm_ffn_g'], 'm_ffn_w_in': out['m_ffn_w_in'], 'm_ffn_w_out': out['m_ffn_w_out'], 'v_norm_mix_g': out['v_norm_mix_g'], 'v_w_in': out['v_w_in'], 'v_sb_q_norm_g': out['v_sb_q_norm_g'], 'v_sb_k_norm_g': out['v_sb_k_norm_g'], 'v_conv_dw_w': out['v_conv_dw_w'], 'v_conv_dw_b': out['v_conv_dw_b'], 'v_conv_ln_g': out['v_conv_ln_g'], 'v_conv_ln_b': out['v_conv_ln_b'], 'v_conv_pw2_w': out['v_conv_pw2_w'], 'v_ssm_lam_re': out['v_ssm_lam_re'], 'v_ssm_lam_im': out['v_ssm_lam_im'], 'v_ssm_log_dt': out['v_ssm_log_dt'], 'v_ssm_b_re': out['v_ssm_b_re'], 'v_ssm_b_im': out['v_ssm_b_im'], 'v_ssm_c_re': out['v_ssm_c_re'], 'v_ssm_c_im': out['v_ssm_c_im'], 'v_ssm_d': out['v_ssm_d'], 'v_ssm_glu_w': out['v_ssm_glu_w'], 'v_branch_norm_g': out['v_branch_norm_g'], 'v_w_out': out['v_w_out'], 'v_norm_xa_g': out['v_norm_xa_g'], 'v_norm_mem_g': out['v_norm_mem_g'], 'v_xa_wq': out['v_xa_wq'], 'v_xa_wk': out['v_xa_wk'], 'v_xa_wv': out['v_xa_wv'], 'v_xa_q_norm_g': out['v_xa_q_norm_g'], 'v_xa_k_norm_g': out['v_xa_k_norm_g'], 'v_xa_wo': out['v_xa_wo'], 'v_norm_ffn_g': out['v_norm_ffn_g'], 'v_ffn_w_in': out['v_ffn_w_in'], 'v_ffn_w_out': out['v_ffn_w_out']}


def _loss(weights, diff, rest, loss_target):
    with _jax.named_scope("forward"):
        args = {**rest, TWIN_DIFF_INPUT: diff, **{k: w.astype(_WEIGHT_DTYPES[k]) for k, w in weights.items()}}
        y = _forward(args)
    with _jax.named_scope("loss_head"):
        err = _jnp.square(y.astype(_jnp.float32) - loss_target)
        return 0.5 * _jnp.sum(_jnp.mean(err, axis=-1)) if err.ndim else 0.5 * err


def _adamw(w, g, m, v):
    m = ADAM_B1 * m + (1.0 - ADAM_B1) * g
    v = ADAM_B2 * v + (1.0 - ADAM_B2) * _jnp.square(g)
    m_hat = m / (1.0 - ADAM_B1 ** ADAM_STEP)
    v_hat = v / (1.0 - ADAM_B2 ** ADAM_STEP)
    delta = -ADAM_LR * (m_hat / (_jnp.sqrt(v_hat) + ADAM_EPS) + ADAM_WD * w)
    return delta, m, v


def reference(x, mem, norm_mix_g, w_in, sb_q_norm_g, sb_k_norm_g, conv_dw_w, conv_dw_b, conv_ln_g, conv_ln_b, conv_pw2_w, ssm_lam_re, ssm_lam_im, ssm_log_dt, ssm_b_re, ssm_b_im, ssm_c_re, ssm_c_im, ssm_d, ssm_glu_w, branch_norm_g, w_out, norm_xa_g, norm_mem_g, xa_wq, xa_wk, xa_wv, xa_q_norm_g, xa_k_norm_g, xa_wo, norm_ffn_g, ffn_w_in, ffn_w_out, loss_target, m_norm_mix_g, m_w_in, m_sb_q_norm_g, m_sb_k_norm_g, m_conv_dw_w, m_conv_dw_b, m_conv_ln_g, m_conv_ln_b, m_conv_pw2_w, m_ssm_lam_re, m_ssm_lam_im, m_ssm_log_dt, m_ssm_b_re, m_ssm_b_im, m_ssm_c_re, m_ssm_c_im, m_ssm_d, m_ssm_glu_w, m_branch_norm_g, m_w_out, m_norm_xa_g, m_norm_mem_g, m_xa_wq, m_xa_wk, m_xa_wv, m_xa_q_norm_g, m_xa_k_norm_g, m_xa_wo, m_norm_ffn_g, m_ffn_w_in, m_ffn_w_out, v_norm_mix_g, v_w_in, v_sb_q_norm_g, v_sb_k_norm_g, v_conv_dw_w, v_conv_dw_b, v_conv_ln_g, v_conv_ln_b, v_conv_pw2_w, v_ssm_lam_re, v_ssm_lam_im, v_ssm_log_dt, v_ssm_b_re, v_ssm_b_im, v_ssm_c_re, v_ssm_c_im, v_ssm_d, v_ssm_glu_w, v_branch_norm_g, v_w_out, v_norm_xa_g, v_norm_mem_g, v_xa_wq, v_xa_wk, v_xa_wv, v_xa_q_norm_g, v_xa_k_norm_g, v_xa_wo, v_norm_ffn_g, v_ffn_w_in, v_ffn_w_out):
    given = dict(x=x, mem=mem, norm_mix_g=norm_mix_g, w_in=w_in, sb_q_norm_g=sb_q_norm_g, sb_k_norm_g=sb_k_norm_g, conv_dw_w=conv_dw_w, conv_dw_b=conv_dw_b, conv_ln_g=conv_ln_g, conv_ln_b=conv_ln_b, conv_pw2_w=conv_pw2_w, ssm_lam_re=ssm_lam_re, ssm_lam_im=ssm_lam_im, ssm_log_dt=ssm_log_dt, ssm_b_re=ssm_b_re, ssm_b_im=ssm_b_im, ssm_c_re=ssm_c_re, ssm_c_im=ssm_c_im, ssm_d=ssm_d, ssm_glu_w=ssm_glu_w, branch_norm_g=branch_norm_g, w_out=w_out, norm_xa_g=norm_xa_g, norm_mem_g=norm_mem_g, xa_wq=xa_wq, xa_wk=xa_wk, xa_wv=xa_wv, xa_q_norm_g=xa_q_norm_g, xa_k_norm_g=xa_k_norm_g, xa_wo=xa_wo, norm_ffn_g=norm_ffn_g, ffn_w_in=ffn_w_in, ffn_w_out=ffn_w_out, loss_target=loss_target, m_norm_mix_g=m_norm_mix_g, m_w_in=m_w_in, m_sb_q_norm_g=m_sb_q_norm_g, m_sb_k_norm_g=m_sb_k_norm_g, m_conv_dw_w=m_conv_dw_w, m_conv_dw_b=m_conv_dw_b, m_conv_ln_g=m_conv_ln_g, m_conv_ln_b=m_conv_ln_b, m_conv_pw2_w=m_conv_pw2_w, m_ssm_lam_re=m_ssm_lam_re, m_ssm_lam_im=m_ssm_lam_im, m_ssm_log_dt=m_ssm_log_dt, m_ssm_b_re=m_ssm_b_re, m_ssm_b_im=m_ssm_b_im, m_ssm_c_re=m_ssm_c_re, m_ssm_c_im=m_ssm_c_im, m_ssm_d=m_ssm_d, m_ssm_glu_w=m_ssm_glu_w, m_branch_norm_g=m_branch_norm_g, m_w_out=m_w_out, m_norm_xa_g=m_norm_xa_g, m_norm_mem_g=m_norm_mem_g, m_xa_wq=m_xa_wq, m_xa_wk=m_xa_wk, m_xa_wv=m_xa_wv, m_xa_q_norm_g=m_xa_q_norm_g, m_xa_k_norm_g=m_xa_k_norm_g, m_xa_wo=m_xa_wo, m_norm_ffn_g=m_norm_ffn_g, m_ffn_w_in=m_ffn_w_in, m_ffn_w_out=m_ffn_w_out, v_norm_mix_g=v_norm_mix_g, v_w_in=v_w_in, v_sb_q_norm_g=v_sb_q_norm_g, v_sb_k_norm_g=v_sb_k_norm_g, v_conv_dw_w=v_conv_dw_w, v_conv_dw_b=v_conv_dw_b, v_conv_ln_g=v_conv_ln_g, v_conv_ln_b=v_conv_ln_b, v_conv_pw2_w=v_conv_pw2_w, v_ssm_lam_re=v_ssm_lam_re, v_ssm_lam_im=v_ssm_lam_im, v_ssm_log_dt=v_ssm_log_dt, v_ssm_b_re=v_ssm_b_re, v_ssm_b_im=v_ssm_b_im, v_ssm_c_re=v_ssm_c_re, v_ssm_c_im=v_ssm_c_im, v_ssm_d=v_ssm_d, v_ssm_glu_w=v_ssm_glu_w, v_branch_norm_g=v_branch_norm_g, v_w_out=v_w_out, v_norm_xa_g=v_norm_xa_g, v_norm_mem_g=v_norm_mem_g, v_xa_wq=v_xa_wq, v_xa_wk=v_xa_wk, v_xa_wv=v_xa_wv, v_xa_q_norm_g=v_xa_q_norm_g, v_xa_k_norm_g=v_xa_k_norm_g, v_xa_wo=v_xa_wo, v_norm_ffn_g=v_norm_ffn_g, v_ffn_w_in=v_ffn_w_in, v_ffn_w_out=v_ffn_w_out)
    weights = {n: given[n] for n in TWIN_WEIGHTS}
    shared = {n: given[n] for n in SHARED_INPUTS}
    per_example = {n: given[n] for n in ['x', 'mem']}
    grad_fn = _jax.value_and_grad(_loss, argnums=(0, 1))

    def one_microbatch(ex, loss_target):
        ex = dict(ex)
        diff = ex.pop(TWIN_DIFF_INPUT)
        return grad_fn(weights, diff, {**shared, **ex}, loss_target)

    if N_MICROBATCH == 1:
        loss, (grad_w, grad_x) = one_microbatch(per_example, given["loss_target"])
    else:
        def body(carry, xs):
            loss_sum, grad_sum = carry
            l_k, (gw_k, gx_k) = one_microbatch(xs[0], xs[1])
            with _jax.named_scope("update"):
                return (loss_sum + l_k, _jax.tree.map(_jnp.add, grad_sum, gw_k)), gx_k

        init = (_jnp.zeros((), _jnp.float32), _jax.tree.map(_jnp.zeros_like, weights))
        (loss, grad_w), grad_x = _jax.lax.scan(body, init, (per_example, given["loss_target"]))
    with _jax.named_scope("update"):
        delta_w, new_m, new_v = {}, {}, {}
        for n in TWIN_WEIGHTS:
            delta_w[n], new_m[n], new_v[n] = _adamw(weights[n], grad_w[n], given["m_" + n], given["v_" + n])
    return (loss, grad_x, *[grad_w[n] for n in TWIN_WEIGHTS], *[delta_w[n] for n in TWIN_WEIGHTS],
            *[new_m[n] for n in TWIN_WEIGHTS], *[new_v[n] for n in TWIN_WEIGHTS])
```

```python
import functools
import math

import jax
import jax.numpy as jnp
from jax import lax
from jax.experimental import pallas as pl
from jax.experimental.pallas import tpu as pltpu

F32 = jnp.float32
BF16 = jnp.bfloat16
MESH = pl.DeviceIdType.MESH
HIGHEST = lax.Precision.HIGHEST

EPS = 1e-6
LANES = 128
SUBLANES = 8
VMEM_LIMIT = 56 * 1024 * 1024

SB_HEAD_DIM = 64
SB_WIDTH = 512
CONV_CH = 256
CONV_WIDTH = 31
CONV_HALO = 32
SSM_CH = 256
SSM_GROUPS = 16
SSM_GROUP = 16
SSM_STATE = 64
SSM_LANES = SSM_GROUPS * SSM_STATE
XA_HEADS = 4
XA_HEAD_DIM = 256
SB_CUT = 110.0

ADAM_LR = 0.001
ADAM_B1 = 0.9
ADAM_B2 = 0.999
ADAM_EPS = 1e-08
ADAM_WD = 0.01
ADAM_STEP = 10

WEIGHTS = ['norm_mix_g', 'w_in', 'sb_q_norm_g', 'sb_k_norm_g', 'conv_dw_w', 'conv_dw_b', 'conv_ln_g',
           'conv_ln_b', 'conv_pw2_w', 'ssm_lam_re', 'ssm_lam_im', 'ssm_log_dt', 'ssm_b_re', 'ssm_b_im',
           'ssm_c_re', 'ssm_c_im', 'ssm_d', 'ssm_glu_w', 'branch_norm_g', 'w_out', 'norm_xa_g',
           'norm_mem_g', 'xa_wq', 'xa_wk', 'xa_wv', 'xa_q_norm_g', 'xa_k_norm_g', 'xa_wo', 'norm_ffn_g',
           'ffn_w_in', 'ffn_w_out']
SHARD_AXIS = {'w_in': 2, 'conv_dw_w': 2, 'conv_pw2_w': 1, 'ssm_glu_w': 2, 'w_out': 1, 'xa_wq': 1,
              'xa_wk': 1, 'xa_wv': 1, 'xa_wo': 1, 'ffn_w_in': 2, 'ffn_w_out': 1}
SHARDED = [n for n in WEIGHTS if n in SHARD_AXIS]
REPLICATED = [n for n in WEIGHTS if n not in SHARD_AXIS]
N_CHIPS = 4
N_DEV = 8


def _cparams(sem=None, **kw):
    if sem is not None:
        kw['dimension_semantics'] = sem
    return pltpu.CompilerParams(vmem_limit_bytes=VMEM_LIMIT, **kw)


def _pick(n, target):
    best = None
    d = LANES
    while d <= min(n, target):
        if n % d == 0:
            best = d
        d += LANES
    return best if best is not None else n


def _rows_for(n_rows, width):
    t = 512
    while t > 8 and t * width > 768 * 1024:
        t //= 2
    return min(t, n_rows)


def _dg(a, b, ca, cb):
    return lax.dot_general(a.astype(BF16), b.astype(BF16), (((ca,), (cb,)), ((), ())),
                           preferred_element_type=F32)


@jax.custom_vjp
def bdot_nn(a, b):
    return _dg(a, b, 1, 0)


def _bdot_nn_fwd(a, b):
    return _dg(a, b, 1, 0), (a, b)


def _bdot_nn_bwd(res, g):
    a, b = res
    return _dg(g, b, 1, 1), _dg(a, g, 0, 0)


bdot_nn.defvjp(_bdot_nn_fwd, _bdot_nn_bwd)


@jax.custom_vjp
def bdot_nt(a, b):
    return _dg(a, b, 1, 1)


def _bdot_nt_fwd(a, b):
    return _dg(a, b, 1, 1), (a, b)


def _bdot_nt_bwd(res, g):
    a, b = res
    return _dg(g, b, 1, 0), _dg(g, a, 0, 0)


bdot_nt.defvjp(_bdot_nt_fwd, _bdot_nt_bwd)


def mm(a, b, mode, name):
    if mode == 'nn':
        M, K = a.shape
        N = b.shape[1]
    elif mode == 'nt':
        M, K = a.shape
        N = b.shape[0]
    else:
        K, M = a.shape
        N = b.shape[1]
    tm, tn, tk = _pick(M, 1024), _pick(N, 1024), _pick(K, 512)
    nk = K // tk
    ca, cb = {'nn': (1, 0), 'nt': (1, 1), 'tn': (0, 0)}[mode]

    def body(a_ref, b_ref, o_ref, acc_ref):
        k = pl.program_id(2)

        @pl.when(k == 0)
        def _():
            acc_ref[...] = jnp.zeros_like(acc_ref)

        acc_ref[...] += _dg(a_ref[...], b_ref[...], ca, cb)

        @pl.when(k == nk - 1)
        def _():
            o_ref[...] = acc_ref[...]

    if mode == 'nn':
        a_spec = pl.BlockSpec((tm, tk), lambda i, j, k: (i, k))
        b_spec = pl.BlockSpec((tk, tn), lambda i, j, k: (k, j))
    elif mode == 'nt':
        a_spec = pl.BlockSpec((tm, tk), lambda i, j, k: (i, k))
        b_spec = pl.BlockSpec((tn, tk), lambda i, j, k: (j, k))
    else:
        a_spec = pl.BlockSpec((tk, tm), lambda i, j, k: (k, i))
        b_spec = pl.BlockSpec((tk, tn), lambda i, j, k: (k, j))
    return pl.pallas_call(
        body, name=name, grid=(M // tm, N // tn, nk),
        in_specs=[a_spec, b_spec],
        out_specs=pl.BlockSpec((tm, tn), lambda i, j, k: (i, j)),
        out_shape=jax.ShapeDtypeStruct((M, N), F32),
        scratch_shapes=[pltpu.VMEM((tm, tn), F32)],
        compiler_params=_cparams(("parallel", "parallel", "arbitrary")),
    )(a, b)


def linear(x, w, name):
    @jax.custom_vjp
    def op(x, w):
        return mm(x, w, 'nn', name + '_fwd')

    def op_fwd(x, w):
        return mm(x, w, 'nn', name + '_fwd'), (x, w)

    def op_bwd(res, g):
        x, w = res
        return mm(g, w, 'nt', name + '_dx'), mm(x, g, 'tn', name + '_dw')

    op.defvjp(op_fwd, op_bwd)
    return op(x, w)


def rowwise(f, rows, params, consts, out_widths, name, need_row_grad=None, block_rows=None):
    nr, npar, nc, nout = len(rows), len(params), len(consts), len(out_widths)
    L = rows[0].shape[0]
    widths = [r.shape[1] for r in rows]
    T = block_rows or _rows_for(L, max(widths + list(out_widths)))
    n = L // T
    need = list(need_row_grad) if need_row_grad is not None else [True] * nr
    pshapes = [p.shape for p in params]
    cshapes = [c.shape for c in consts]

    row_specs = [pl.BlockSpec((T, w), lambda i: (i, 0)) for w in widths]
    par_specs = [pl.BlockSpec(s, lambda i: (0, 0)) for s in pshapes]
    con_specs = [pl.BlockSpec(s, lambda i: (0, 0)) for s in cshapes]
    out_specs = [pl.BlockSpec((T, w), lambda i: (i, 0)) for w in out_widths]

    def fwd_call(rows, params, consts):
        def body(*refs):
            ins = [r[...] for r in refs[:nr + npar + nc]]
            outs = f(*ins)
            for o_ref, val in zip(refs[nr + npar + nc:], outs):
                o_ref[...] = val

        return pl.pallas_call(
            body, name=name + '_fwd', grid=(n,),
            in_specs=row_specs + par_specs + con_specs, out_specs=out_specs,
            out_shape=[jax.ShapeDtypeStruct((L, w), F32) for w in out_widths],
            compiler_params=_cparams(("parallel",)),
        )(*rows, *params, *consts)

    def bwd_call(rows, params, consts, cts):
        grad_rows = [k for k in range(nr) if need[k]]

        def body(*refs):
            i = pl.program_id(0)
            rv = [r[...] for r in refs[:nr]]
            pv = [r[...] for r in refs[nr:nr + npar]]
            cv = [r[...] for r in refs[nr + npar:nr + npar + nc]]
            ctv = tuple(r[...] for r in refs[nr + npar + nc:nr + npar + nc + nout])
            orefs = refs[nr + npar + nc + nout:]
            _, vjp = jax.vjp(lambda *rp: tuple(f(*rp, *cv)), *rv, *pv)
            g = vjp(ctv)
            for slot, k in enumerate(grad_rows):
                orefs[slot][...] = g[k]

            @pl.when(i == 0)
            def _():
                for k in range(npar):
                    orefs[len(grad_rows) + k][...] = jnp.zeros(pshapes[k], F32)

            for k in range(npar):
                orefs[len(grad_rows) + k][...] += g[nr + k]

        outs = pl.pallas_call(
            body, name=name + '_bwd', grid=(n,),
            in_specs=row_specs + par_specs + con_specs + out_specs,
            out_specs=[row_specs[k] for k in grad_rows] + par_specs,
            out_shape=[jax.ShapeDtypeStruct((L, widths[k]), F32) for k in grad_rows]
            + [jax.ShapeDtypeStruct(s, F32) for s in pshapes],
            compiler_params=_cparams(("arbitrary",)),
        )(*rows, *params, *consts, *cts)
        drows = []
        slot = 0
        for k in range(nr):
            if need[k]:
                drows.append(outs[slot])
                slot += 1
            else:
                drows.append(jnp.zeros_like(rows[k]))
        return tuple(drows), tuple(outs[len(grad_rows):])

    @jax.custom_vjp
    def op(rows, params, consts):
        return tuple(fwd_call(rows, params, consts))

    def op_fwd(rows, params, consts):
        return tuple(fwd_call(rows, params, consts)), (rows, params, consts)

    def op_bwd(res, cts):
        rows, params, consts = res
        drows, dparams = bwd_call(rows, params, consts, cts)
        return drows, dparams, tuple(jnp.zeros_like(c) for c in consts)

    op.defvjp(op_fwd, op_bwd)
    return op(tuple(rows), tuple(params), tuple(consts))


def _rms(x, g):
    return x * lax.rsqrt(jnp.mean(x * x, axis=-1, keepdims=True) + EPS) * g


def rmsnorm(x, g, name):
    return rowwise(lambda x, g: (_rms(x, g),), [x], [g.reshape(1, -1)], [], [x.shape[1]], name)[0]


def _group_matrices(width, group):
    gid = jnp.arange(width) // group
    sel = (gid[:, None] == jnp.arange(LANES)[None, :]).astype(F32)
    return sel / group, sel.T


def groupnorm(x, g, group, name):
    width = x.shape[1]
    avg, spread = _group_matrices(width, group)
    g_full = jnp.tile(g.reshape(1, group), (1, width // group))

    def f(x, g_full, avg, spread):
        ms = jnp.dot(x * x, avg, precision=HIGHEST, preferred_element_type=F32)
        inv = jnp.dot(lax.rsqrt(ms + EPS), spread, precision=HIGHEST, preferred_element_type=F32)
        return (x * inv * g_full,)

    return rowwise(f, [x], [g_full], [avg, spread], [width], name)[0]


def glu(x, name):
    half = x.shape[1] // 2

    def f(x):
        return (x[:, :half] * jax.nn.sigmoid(x[:, half:]),)

    return rowwise(f, [x], [], [], [half], name)[0]


def swiglu(x, name):
    half = x.shape[1] // 2

    def f(x):
        gate = x[:, :half]
        return (gate * jax.nn.sigmoid(gate) * x[:, half:],)

    return rowwise(f, [x], [], [], [half], name)[0]


def ln_silu(x, g, b, name):
    def f(x, g, b):
        mu = jnp.mean(x, axis=-1, keepdims=True)
        xc = x - mu
        var = jnp.mean(xc * xc, axis=-1, keepdims=True)
        y = xc * lax.rsqrt(var + EPS) * g + b
        return (y * jax.nn.sigmoid(y),)

    return rowwise(f, [x], [g.reshape(1, -1), b.reshape(1, -1)], [], [x.shape[1]], name)[0]


def branch_norms(o_sb, o_conv, o_ssm, g, name):
    w1, w2 = o_sb.shape[1], o_conv.shape[1]

    def f(a, b, c, g):
        return (jnp.concatenate([_rms(a, g[:, :w1]), _rms(b, g[:, w1:w1 + w2]), _rms(c, g[:, w1 + w2:])],
                                axis=-1),)

    return rowwise(f, [o_sb, o_conv, o_ssm], [g.reshape(1, -1)], [], [g.shape[-1]], name)[0]


def xa_core(q, k, v, name):
    scale = XA_HEAD_DIM ** -0.5

    def f(q, k, v):
        outs = []
        for h in range(XA_HEADS):
            sl = slice(h * XA_HEAD_DIM, (h + 1) * XA_HEAD_DIM)
            s = bdot_nt(q[:, sl], k[:, sl]) * scale
            m = lax.stop_gradient(jnp.max(s, axis=-1, keepdims=True))
            e = jnp.exp(s - m)
            p = e / jnp.sum(e, axis=-1, keepdims=True)
            outs.append(bdot_nn(p, v[:, sl]))
        return (jnp.concatenate(outs, axis=-1),)

    return rowwise(f, [q], [k, v], [], [q.shape[1]], name, block_rows=min(256, q.shape[0]))[0]


def loss_rows(y, target, name):
    def f(y, t):
        d = y - t
        return (0.5 * jnp.mean(d * d, axis=-1, keepdims=True),)

    return rowwise(f, [y, target], [], [], [1], name, need_row_grad=[True, False])[0]


def _hilo(x, ones_bf16):
    hi = x.astype(BF16)
    lo = (x - hi.astype(F32)).astype(BF16)
    return _dg(hi, ones_bf16, 1, 0) + _dg(lo, ones_bf16, 1, 0)


def _sb_block(qh, kb, c, valid, strict_upper):
    z = _dg(qh, kb, 1, 1)
    a = jnp.minimum(z, 0.0) - jnp.log(1.0 + jnp.exp(-jnp.abs(z)))
    b = jnp.where(valid, a - z, 0.0)
    s = _hilo(b, strict_upper) + c
    w = jnp.where(valid, jnp.exp(a + s), 0.0)
    return a, b, w


def _sb_masks(T):
    row = lax.broadcasted_iota(jnp.int32, (T, T), 0)
    col = lax.broadcasted_iota(jnp.int32, (T, T), 1)
    return col < row, (row > col).astype(BF16), (row >= col).astype(BF16)


def _head_mask(h):
    lane = lax.broadcasted_iota(jnp.int32, (1, LANES), 1)
    return (lane // SB_HEAD_DIM == h).astype(F32)


def _sb_fwd_call(q, k, v, T, name):
    L, W = q.shape
    scale = SB_HEAD_DIM ** -0.5

    def body(q_ref, k_ref, v_ref, o_ref):
        i = pl.program_id(1)
        causal, strict_upper, _ = _sb_masks(T)
        q2 = q_ref[...] * scale
        acc = jnp.zeros((T, LANES), F32)
        for h in range(LANES // SB_HEAD_DIM):
            hm = _head_mask(h)
            qh = (q2 * hm).astype(BF16)

            def cond(state):
                j, c, _ = state
                return jnp.logical_and(j >= 0, jnp.max(c) > -SB_CUT)

            def step(state):
                j, c, acc = state
                r0 = pl.multiple_of(j * T, T)
                valid = jnp.logical_or(causal, j != i)
                _, b, w = _sb_block(qh, k_ref[pl.ds(r0, T), :], c, valid, strict_upper)
                acc = acc + _dg(w, v_ref[pl.ds(r0, T), :] * hm, 1, 0)
                return j - 1, c + jnp.sum(b, axis=1, keepdims=True), acc

            _, _, acc = lax.while_loop(cond, step, (i, jnp.zeros((T, 1), F32), acc))
        o_ref[...] = acc

    return pl.pallas_call(
        body, name=name, grid=(W // LANES, L // T),
        in_specs=[pl.BlockSpec((T, LANES), lambda p, i: (i, p)),
                  pl.BlockSpec((L, LANES), lambda p, i: (0, p)),
                  pl.BlockSpec((L, LANES), lambda p, i: (0, p))],
        out_specs=pl.BlockSpec((T, LANES), lambda p, i: (i, p)),
        out_shape=jax.ShapeDtypeStruct((L, W), F32),
        compiler_params=_cparams(("parallel", "parallel")),
    )(q, k, v)


def _sb_bwd_call(q, k, v, do, T, name):
    L, W = q.shape
    scale = SB_HEAD_DIM ** -0.5

    def body(q_ref, k_ref, v_ref, do_ref, dq_ref, dk_ref, dv_ref):
        i = pl.program_id(1)

        @pl.when(i == 0)
        def _():
            dk_ref[...] = jnp.zeros_like(dk_ref)
            dv_ref[...] = jnp.zeros_like(dv_ref)

        causal, strict_upper, upper = _sb_masks(T)
        q2 = q_ref[...] * scale
        do2 = do_ref[...]
        dq = jnp.zeros((T, LANES), F32)
        zero = jnp.zeros((T, 1), F32)
        for h in range(LANES // SB_HEAD_DIM):
            hm = _head_mask(h)
            qh = (q2 * hm).astype(BF16)
            doh = (do2 * hm).astype(BF16)

            def block(j, c):
                r0 = pl.multiple_of(j * T, T)
                valid = jnp.logical_or(causal, j != i)
                kb = k_ref[pl.ds(r0, T), :]
                a, b, w = _sb_block(qh, kb, c, valid, strict_upper)
                e = _dg(doh, v_ref[pl.ds(r0, T), :], 1, 1) * w
                return r0, valid, kb, a, b, w, e

            def live(j, c):
                return jnp.logical_and(j >= 0, jnp.max(c) > -SB_CUT)

            def step1(state):
                j, c, r = state
                _, _, _, _, b, _, e = block(j, c)
                return (j - 1, c + jnp.sum(b, axis=1, keepdims=True),
                        r + jnp.sum(e, axis=1, keepdims=True))

            _, _, total = lax.while_loop(lambda s: live(s[0], s[1]), step1, (i, zero, zero))

            def step2(state):
                j, c, r, dq = state
                r0, valid, kb, a, b, w, e = block(j, c)
                before = total - (_hilo(e, upper) + r)
                dz = jnp.where(valid, e * jnp.exp(b) - before * jnp.exp(a), 0.0).astype(BF16)
                dq = dq + _dg(dz, kb * hm, 1, 0)
                dk_ref[pl.ds(r0, T), :] += _dg(dz, qh, 0, 0)
                dv_ref[pl.ds(r0, T), :] += _dg(w, doh, 0, 0)
                return (j - 1, c + jnp.sum(b, axis=1, keepdims=True),
                        r + jnp.sum(e, axis=1, keepdims=True), dq)

            _, _, _, dq = lax.while_loop(lambda s: live(s[0], s[1]), step2, (i, zero, zero, dq))
        dq_ref[...] = dq * scale

    blk = pl.BlockSpec((T, LANES), lambda p, i: (i, p))
    full = pl.BlockSpec((L, LANES), lambda p, i: (0, p))
    return pl.pallas_call(
        body, name=name, grid=(W // LANES, L // T),
        in_specs=[blk, full, full, blk],
        out_specs=[blk, full, full],
        out_shape=[jax.ShapeDtypeStruct((L, W), F32)] * 3,
        compiler_params=_cparams(("parallel", "arbitrary")),
    )(q, k, v, do)


def sb_attention(q, k, v, name):
    T = min(128, q.shape[0])

    @jax.custom_vjp
    def op(q, k, v):
        return _sb_fwd_call(q, k, v, T, name + '_fwd')

    def op_fwd(q, k, v):
        return _sb_fwd_call(q, k, v, T, name + '_fwd'), (q, k, v)

    def op_bwd(res, do):
        q, k, v = res
        return tuple(_sb_bwd_call(q, k, v, do, T, name + '_bwd'))

    op.defvjp(op_fwd, op_bwd)
    return op(q, k, v)


def _dwconv_fwd_call(x, w, b, T, name):
    L, C = x.shape
    per = T // CONV_HALO
    lead = CONV_HALO - (CONV_WIDTH - 1)

    def body(x_ref, halo_ref, w_ref, b_ref, o_ref, buf):
        i = pl.program_id(0)
        buf[0:CONV_HALO, :] = jnp.where(i > 0, halo_ref[...], 0.0)
        buf[CONV_HALO:CONV_HALO + T, :] = x_ref[...]
        acc = jnp.zeros((T, C), F32) + b_ref[...]
        for j in range(CONV_WIDTH):
            acc = acc + w_ref[j:j + 1, :] * buf[lead + j:lead + j + T, :]
        o_ref[...] = acc

    return pl.pallas_call(
        body, name=name, grid=(L // T,),
        in_specs=[pl.BlockSpec((T, C), lambda i: (i, 0)),
                  pl.BlockSpec((CONV_HALO, C), lambda i: (jnp.maximum(i * per - 1, 0), 0)),
                  pl.BlockSpec(w.shape, lambda i: (0, 0)),
                  pl.BlockSpec(b.shape, lambda i: (0, 0))],
        out_specs=pl.BlockSpec((T, C), lambda i: (i, 0)),
        out_shape=jax.ShapeDtypeStruct((L, C), F32),
        scratch_shapes=[pltpu.VMEM((T + CONV_HALO, C), F32)],
        compiler_params=_cparams(("parallel",)),
    )(x, x, w, b)


def _dwconv_bwd_call(x, w, g, T, name):
    L, C = x.shape
    per = T // CONV_HALO
    n = L // T
    last_halo = L // CONV_HALO - 1
    lead = CONV_HALO - (CONV_WIDTH - 1)

    def body(x_ref, xh_ref, g_ref, gh_ref, w_ref, dx_ref, dw_ref, db_ref, bufx, bufg):
        i = pl.program_id(0)
        bufx[0:CONV_HALO, :] = jnp.where(i > 0, xh_ref[...], 0.0)
        bufx[CONV_HALO:CONV_HALO + T, :] = x_ref[...]
        gm = g_ref[...]
        bufg[0:T, :] = gm
        bufg[T:T + CONV_HALO, :] = jnp.where(i < n - 1, gh_ref[...], 0.0)
        acc = jnp.zeros((T, C), F32)
        for j in range(CONV_WIDTH):
            off = CONV_WIDTH - 1 - j
            acc = acc + w_ref[j:j + 1, :] * bufg[off:off + T, :]
        dx_ref[...] = acc

        @pl.when(i == 0)
        def _():
            dw_ref[...] = jnp.zeros_like(dw_ref)
            db_ref[...] = jnp.zeros_like(db_ref)

        for j in range(CONV_WIDTH):
            dw_ref[j:j + 1, :] += jnp.sum(gm * bufx[lead + j:lead + j + T, :], axis=0, keepdims=True)
        db_ref[...] += jnp.sum(gm, axis=0, keepdims=True)

    return pl.pallas_call(
        body, name=name, grid=(n,),
        in_specs=[pl.BlockSpec((T, C), lambda i: (i, 0)),
                  pl.BlockSpec((CONV_HALO, C), lambda i: (jnp.maximum(i * per - 1, 0), 0)),
                  pl.BlockSpec((T, C), lambda i: (i, 0)),
                  pl.BlockSpec((CONV_HALO, C), lambda i: (jnp.minimum((i + 1) * per, last_halo), 0)),
                  pl.BlockSpec(w.shape, lambda i: (0, 0))],
        out_specs=[pl.BlockSpec((T, C), lambda i: (i, 0)),
                   pl.BlockSpec(w.shape, lambda i: (0, 0)),
                   pl.BlockSpec((1, C), lambda i: (0, 0))],
        out_shape=[jax.ShapeDtypeStruct((L, C), F32), jax.ShapeDtypeStruct(w.shape, F32),
                   jax.ShapeDtypeStruct((1, C), F32)],
        scratch_shapes=[pltpu.VMEM((T + CONV_HALO, C), F32), pltpu.VMEM((T + CONV_HALO, C), F32)],
        compiler_params=_cparams(("arbitrary",)),
    )(x, x, g, g, w)


def dwconv(x, w, b, name):
    T = min(512, x.shape[0])

    @jax.custom_vjp
    def op(x, w, b):
        return _dwconv_fwd_call(x, w, b, T, name + '_fwd')

    def op_fwd(x, w, b):
        return _dwconv_fwd_call(x, w, b, T, name + '_fwd'), (x, w)

    def op_bwd(res, g):
        x, w = res
        return tuple(_dwconv_bwd_call(x, w, g, T, name + '_bwd'))

    op.defvjp(op_fwd, op_bwd)
    return op(x, w, b)


def _ssm_fwd_call(u, ar, ai, bbr, bbi, cr, ci, d, T, name):
    L, C = u.shape
    S = SSM_LANES

    def body(u_ref, ar_ref, ai_ref, bbr_ref, bbi_ref, cr_ref, ci_ref, d_ref,
             y_ref, xr_ref, xi_ref, st_r, st_i, in_r, in_i, out_r, out_i):
        i = pl.program_id(0)

        @pl.when(i == 0)
        def _():
            st_r[...] = jnp.zeros_like(st_r)
            st_i[...] = jnp.zeros_like(st_i)

        u_blk = u_ref[...]
        xr_ref[...] = _dg(u_blk, bbr_ref[...], 1, 0)
        xi_ref[...] = _dg(u_blk, bbi_ref[...], 1, 0)
        a_r, a_i = ar_ref[...], ai_ref[...]

        def tile(t, carry):
            sr, si = carry
            r0 = pl.multiple_of(t * SUBLANES, SUBLANES)
            in_r[...] = xr_ref[pl.ds(r0, SUBLANES), :]
            in_i[...] = xi_ref[pl.ds(r0, SUBLANES), :]
            for r in range(SUBLANES):
                nr = a_r * sr - a_i * si + in_r[r:r + 1, :]
                ni = a_r * si + a_i * sr + in_i[r:r + 1, :]
                sr, si = nr, ni
                out_r[r:r + 1, :] = sr
                out_i[r:r + 1, :] = si
            xr_ref[pl.ds(r0, SUBLANES), :] = out_r[...]
            xi_ref[pl.ds(r0, SUBLANES), :] = out_i[...]
            return sr, si

        sr, si = lax.fori_loop(0, T // SUBLANES, tile, (st_r[0:1, :], st_i[0:1, :]))
        st_r[0:1, :] = sr
        st_i[0:1, :] = si
        y_ref[...] = (_dg(xr_ref[...], cr_ref[...], 1, 0) - _dg(xi_ref[...], ci_ref[...], 1, 0)
                      + d_ref[...] * u_blk)

    full = lambda a: pl.BlockSpec(a.shape, lambda i: (0, 0))
    return pl.pallas_call(
        body, name=name, grid=(L // T,),
        in_specs=[pl.BlockSpec((T, C), lambda i: (i, 0))] + [full(a) for a in (ar, ai, bbr, bbi, cr, ci, d)],
        out_specs=[pl.BlockSpec((T, C), lambda i: (i, 0)), pl.BlockSpec((T, S), lambda i: (i, 0)),
                   pl.BlockSpec((T, S), lambda i: (i, 0))],
        out_shape=[jax.ShapeDtypeStruct((L, C), F32), jax.ShapeDtypeStruct((L, S), F32),
                   jax.ShapeDtypeStruct((L, S), F32)],
        scratch_shapes=[pltpu.VMEM((SUBLANES, S), F32)] * 6,
        compiler_params=_cparams(("arbitrary",)),
    )(u, ar, ai, bbr, bbi, cr, ci, d)


def _ssm_bwd_call(u, xr, xi, dy, ar, ai, bbr, bbi, cr, ci, d, T, name):
    L, C = u.shape
    S = SSM_LANES
    n = L // T
    per = T // SUBLANES

    def body(u_ref, xr_ref, xi_ref, hr_ref, hi_ref, dy_ref, ar_ref, ai_ref, bbr_ref, bbi_ref, cr_ref, ci_ref,
             d_ref, du_ref, dar_ref, dai_ref, dbr_ref, dbi_ref, dcr_ref, dci_ref, dd_ref,
             lam_r, lam_i, prev_r, prev_i, st_r, st_i, in_r, in_i, out_r, out_i):
        i = pl.program_id(0)
        chunk = n - 1 - i

        @pl.when(i == 0)
        def _():
            st_r[...] = jnp.zeros_like(st_r)
            st_i[...] = jnp.zeros_like(st_i)
            for ref in (dar_ref, dai_ref, dbr_ref, dbi_ref, dcr_ref, dci_ref, dd_ref):
                ref[...] = jnp.zeros_like(ref)

        dy_blk = dy_ref[...]
        u_blk = u_ref[...]
        lam_r[...] = _dg(dy_blk, cr_ref[...], 1, 1)
        lam_i[...] = -_dg(dy_blk, ci_ref[...], 1, 1)
        dcr_ref[...] += _dg(xr_ref[...], dy_blk, 0, 0)
        dci_ref[...] -= _dg(xi_ref[...], dy_blk, 0, 0)
        a_r, a_i = ar_ref[...], ai_ref[...]

        def tile(k, carry):
            lr, li = carry
            r0 = pl.multiple_of((per - 1 - k) * SUBLANES, SUBLANES)
            in_r[...] = lam_r[pl.ds(r0, SUBLANES), :]
            in_i[...] = lam_i[pl.ds(r0, SUBLANES), :]
            for r in range(SUBLANES - 1, -1, -1):
                nr = in_r[r:r + 1, :] + a_r * lr + a_i * li
                ni = in_i[r:r + 1, :] + a_r * li - a_i * lr
                lr, li = nr, ni
                out_r[r:r + 1, :] = lr
                out_i[r:r + 1, :] = li
            lam_r[pl.ds(r0, SUBLANES), :] = out_r[...]
            lam_i[pl.ds(r0, SUBLANES), :] = out_i[...]
            return lr, li

        lr, li = lax.fori_loop(0, per, tile, (st_r[0:1, :], st_i[0:1, :]))
        st_r[0:1, :] = lr
        st_i[0:1, :] = li

        l_r, l_i = lam_r[...], lam_i[...]
        du_ref[...] = _dg(l_r, bbr_ref[...], 1, 1) + _dg(l_i, bbi_ref[...], 1, 1) + d_ref[...] * dy_blk
        dbr_ref[...] += _dg(u_blk, l_r, 0, 0)
        dbi_ref[...] += _dg(u_blk, l_i, 0, 0)
        dd_ref[...] += jnp.sum(dy_blk * u_blk, axis=0, keepdims=True)

        prev_r[0:SUBLANES, :] = jnp.where(chunk > 0, hr_ref[...], 0.0)
        prev_i[0:SUBLANES, :] = jnp.where(chunk > 0, hi_ref[...], 0.0)
        prev_r[SUBLANES:SUBLANES + T, :] = xr_ref[...]
        prev_i[SUBLANES:SUBLANES + T, :] = xi_ref[...]
        p_r = prev_r[SUBLANES - 1:SUBLANES - 1 + T, :]
        p_i = prev_i[SUBLANES - 1:SUBLANES - 1 + T, :]
        dar_ref[...] += jnp.sum(l_r * p_r + l_i * p_i, axis=0, keepdims=True)
        dai_ref[...] += jnp.sum(l_i * p_r - l_r * p_i, axis=0, keepdims=True)

    rev = lambda w: pl.BlockSpec((T, w), lambda i: (n - 1 - i, 0))
    halo = pl.BlockSpec((SUBLANES, S), lambda i: (jnp.maximum((n - 1 - i) * per - 1, 0), 0))
    full = lambda a: pl.BlockSpec(a.shape, lambda i: (0, 0))
    params = (ar, ai, bbr, bbi, cr, ci, d)
    return pl.pallas_call(
        body, name=name, grid=(n,),
        in_specs=[rev(C), rev(S), rev(S), halo, halo, rev(C)] + [full(a) for a in params],
        out_specs=[rev(C)] + [full(a) for a in params],
        out_shape=[jax.ShapeDtypeStruct((L, C), F32)] + [jax.ShapeDtypeStruct(a.shape, F32) for a in params],
        scratch_shapes=[pltpu.VMEM((T, S), F32), pltpu.VMEM((T, S), F32),
                        pltpu.VMEM((T + SUBLANES, S), F32), pltpu.VMEM((T + SUBLANES, S), F32)]
        + [pltpu.VMEM((SUBLANES, S), F32)] * 6,
        compiler_params=_cparams(("arbitrary",)),
    )(u, xr, xi, xr, xi, dy, *params)


def ssm_core(u, ar, ai, bbr, bbi, cr, ci, d, name):
    T = min(256, u.shape[0])

    @jax.custom_vjp
    def op(u, ar, ai, bbr, bbi, cr, ci, d):
        return _ssm_fwd_call(u, ar, ai, bbr, bbi, cr, ci, d, T, name + '_fwd')[0]

    def op_fwd(u, ar, ai, bbr, bbi, cr, ci, d):
        y, xr, xi = _ssm_fwd_call(u, ar, ai, bbr, bbi, cr, ci, d, T, name + '_fwd')
        return y, (u, xr, xi, ar, ai, bbr, bbi, cr, ci, d)

    def op_bwd(res, dy):
        u, xr, xi, ar, ai, bbr, bbi, cr, ci, d = res
        return tuple(_ssm_bwd_call(u, xr, xi, dy, ar, ai, bbr, bbi, cr, ci, d, T, name + '_bwd'))

    op.defvjp(op_fwd, op_bwd)
    return op(u, ar, ai, bbr, bbi, cr, ci, d)


@jax.custom_vjp
def _block_diag(blocks):
    G, R, Cc = blocks.shape
    eye = jnp.eye(G, dtype=blocks.dtype)
    return (blocks[:, :, None, :] * eye[:, None, :, None]).reshape(G * R, G * Cc)


def _block_diag_fwd(blocks):
    return _block_diag(blocks), blocks.shape


def _block_diag_bwd(shape, g):
    G, R, Cc = shape
    return (jnp.stack([g[k * R:(k + 1) * R, k * Cc:(k + 1) * Cc] for k in range(G)]),)


_block_diag.defvjp(_block_diag_fwd, _block_diag_bwd)


def ssm_discretise(lam_re, lam_im, log_dt, b_re, b_im, c_re, c_im):
    dt = jnp.exp(log_dt)[:, None]
    mag = jnp.exp(lam_re * dt)
    ar, ai = mag * jnp.cos(lam_im * dt), mag * jnp.sin(lam_im * dt)
    den = lam_re * lam_re + lam_im * lam_im
    fr = ((ar - 1.0) * lam_re + ai * lam_im) / den
    fi = (ai * lam_re - (ar - 1.0) * lam_im) / den
    bbr = fr[..., None] * b_re - fi[..., None] * b_im
    bbi = fr[..., None] * b_im + fi[..., None] * b_re
    return (ar.reshape(1, SSM_LANES), ai.reshape(1, SSM_LANES),
            _block_diag(bbr.transpose(0, 2, 1)), _block_diag(bbi.transpose(0, 2, 1)),
            _block_diag(c_re.transpose(0, 2, 1)), _block_diag(c_im.transpose(0, 2, 1)))


def split_columns(p, bounds):
    @jax.custom_vjp
    def op(p):
        return tuple(p[:, lo:hi] for lo, hi in zip(bounds[:-1], bounds[1:]))

    def op_fwd(p):
        return op(p), None

    def op_bwd(_, gs):
        return (jnp.concatenate(gs, axis=1),)

    op.defvjp(op_fwd, op_bwd)
    return op(p)


def local_loss(w, x, mem, target):
    depth = w['w_in'].shape[0]
    s1, s2, s3 = SB_WIDTH, 2 * SB_WIDTH, 3 * SB_WIDTH
    s4 = s3 + 2 * CONV_CH
    for l in range(depth):
        tag = 'l%d_' % l
        h = rmsnorm(x, w['norm_mix_g'][l], tag + 'norm_mix')
        p = linear(h, w['w_in'][l], tag + 'w_in')
        q, k, v, u_conv, u_ssm = split_columns(p, (0, s1, s2, s3, s4, p.shape[1]))
        q = groupnorm(q, w['sb_q_norm_g'][l], SB_HEAD_DIM, tag + 'q_norm')
        k = groupnorm(k, w['sb_k_norm_g'][l], SB_HEAD_DIM, tag + 'k_norm')
        o_sb = sb_attention(q, k, v, tag + 'sb')

        dw_w = jnp.pad(w['conv_dw_w'][l], ((0, CONV_HALO - CONV_WIDTH), (0, 0)))
        hc = dwconv(glu(u_conv, tag + 'conv_glu'), dw_w, w['conv_dw_b'][l].reshape(1, -1), tag + 'dwconv')
        hc = ln_silu(hc, w['conv_ln_g'][l], w['conv_ln_b'][l], tag + 'conv_ln')
        o_conv = linear(hc, w['conv_pw2_w'][l], tag + 'pw2')

        ar, ai, bbr, bbi, cr, ci = ssm_discretise(
            w['ssm_lam_re'][l], w['ssm_lam_im'][l], w['ssm_log_dt'][l], w['ssm_b_re'][l], w['ssm_b_im'][l],
            w['ssm_c_re'][l], w['ssm_c_im'][l])
        y = ssm_core(u_ssm, ar, ai, bbr, bbi, cr, ci, w['ssm_d'][l].reshape(1, -1), tag + 'ssm')
        o_ssm = glu(linear(y, w['ssm_glu_w'][l], tag + 'ssm_glu_w'), tag + 'ssm_glu')

        mixed = branch_norms(o_sb, o_conv, o_ssm, w['branch_norm_g'][l], tag + 'branch_norm')
        x = x + linear(mixed, w['w_out'][l], tag + 'w_out')

        hx = rmsnorm(x, w['norm_xa_g'][l], tag + 'norm_xa')
        hm = rmsnorm(mem, w['norm_mem_g'][l], tag + 'norm_mem')
        qx = groupnorm(linear(hx, w['xa_wq'][l], tag + 'xa_wq'), w['xa_q_norm_g'][l], XA_HEAD_DIM, tag + 'xa_qn')
        kx = groupnorm(linear(hm, w['xa_wk'][l], tag + 'xa_wk'), w['xa_k_norm_g'][l], XA_HEAD_DIM, tag + 'xa_kn')
        vx = linear(hm, w['xa_wv'][l], tag + 'xa_wv')
        x = x + linear(xa_core(qx, kx, vx, tag + 'xa_core'), w['xa_wo'][l], tag + 'xa_wo')

        hf = rmsnorm(x, w['norm_ffn_g'][l], tag + 'norm_ffn')
        act = swiglu(linear(hf, w['ffn_w_in'][l], tag + 'ffn_in'), tag + 'swiglu')
        x = x + linear(act, w['ffn_w_out'][l], tag + 'ffn_out')
    return jnp.sum(loss_rows(x, target, 'loss'))


def local_step(w, x, mem, target):
    loss, (gw, gx) = jax.value_and_grad(local_loss, argnums=(0, 1))(w, x, mem, target)
    return loss, gx, gw


PACK_ROWS = 2048


def pack(arrays, dtype):
    flat = jnp.concatenate([a.reshape(-1).astype(dtype) for a in arrays])
    unit = PACK_ROWS * LANES
    total = -(-flat.shape[0] // unit) * unit
    return jnp.pad(flat, (0, total - flat.shape[0])).reshape(-1, LANES)


def unpack(packed, shapes):
    flat = packed.reshape(-1)
    out, off = [], 0
    for s in shapes:
        size = math.prod(s)
        out.append(flat[off:off + size].reshape(s))
        off += size
    return out


def _mesh_pos():
    return lax.axis_index("x"), lax.axis_index("y"), lax.axis_index("c")


def exchange_xy(buf, per_peer, name):
    R = buf.shape[-2]

    def body(in_ref, out_ref, send_sems, recv_sems, local_sem):
        x, y, c = _mesh_pos()
        me = 2 * x + y
        peers = [(1 - x, y), (x, 1 - y), (1 - x, 1 - y)]

        def src(chip):
            return in_ref.at[chip] if per_peer else in_ref

        local = pltpu.make_async_copy(src(me), out_ref.at[me], local_sem)
        local.start()
        sends = []
        for k, (px, py) in enumerate(peers):
            cp = pltpu.make_async_remote_copy(
                src_ref=src(2 * px + py), dst_ref=out_ref.at[me], send_sem=send_sems.at[k],
                recv_sem=recv_sems.at[k], device_id=(px, py, c), device_id_type=MESH)
            cp.start()
            sends.append(cp)
        for k, (px, py) in enumerate(peers):
            pltpu.make_async_remote_copy(
                src_ref=src(me), dst_ref=out_ref.at[2 * px + py], send_sem=send_sems.at[k],
                recv_sem=recv_sems.at[k], device_id=(px, py, c), device_id_type=MESH).wait_recv()
        for cp in sends:
            cp.wait_send()
        local.wait()

    return pl.pallas_call(
        body, name=name,
        in_specs=[pl.BlockSpec(memory_space=pl.ANY)],
        out_specs=pl.BlockSpec(memory_space=pl.ANY),
        out_shape=jax.ShapeDtypeStruct((N_CHIPS, R, LANES), buf.dtype),
        scratch_shapes=[pltpu.SemaphoreType.DMA((3,)), pltpu.SemaphoreType.DMA((3,)), pltpu.SemaphoreType.DMA(())],
        compiler_params=pltpu.CompilerParams(has_side_effects=True),
    )(buf)


def exchange_c(buf, name):
    def body(in_ref, out_ref, send_sem, recv_sem):
        x, y, c = _mesh_pos()
        cp = pltpu.make_async_remote_copy(
            src_ref=in_ref, dst_ref=out_ref, send_sem=send_sem, recv_sem=recv_sem,
            device_id=(x, y, 1 - c), device_id_type=MESH)
        cp.start()
        cp.wait()

    return pl.pallas_call(
        body, name=name,
        in_specs=[pl.BlockSpec(memory_space=pl.ANY)],
        out_specs=pl.BlockSpec(memory_space=pl.ANY),
        out_shape=jax.ShapeDtypeStruct(buf.shape, buf.dtype),
        scratch_shapes=[pltpu.SemaphoreType.DMA(()), pltpu.SemaphoreType.DMA(())],
        compiler_params=pltpu.CompilerParams(has_side_effects=True),
    )(buf)


def allreduce_small(buf, name):
    R = buf.shape[0]

    def body(in_ref, sum_ref, all_ref, send_sems, recv_sems):
        x, y, c = _mesh_pos()
        me = 4 * x + 2 * y + c
        all_ref[me] = in_ref[...]
        flips = [(fx, fy, fc) for fx in (0, 1) for fy in (0, 1) for fc in (0, 1)][1:]
        sends = []
        for k, (fx, fy, fc) in enumerate(flips):
            cp = pltpu.make_async_remote_copy(
                src_ref=in_ref, dst_ref=all_ref.at[me], send_sem=send_sems.at[k], recv_sem=recv_sems.at[k],
                device_id=(x ^ fx, y ^ fy, c ^ fc), device_id_type=MESH)
            cp.start()
            sends.append(cp)
        for k, (fx, fy, fc) in enumerate(flips):
            peer = 4 * (x ^ fx) + 2 * (y ^ fy) + (c ^ fc)
            pltpu.make_async_remote_copy(
                src_ref=in_ref, dst_ref=all_ref.at[peer], send_sem=send_sems.at[k], recv_sem=recv_sems.at[k],
                device_id=(x ^ fx, y ^ fy, c ^ fc), device_id_type=MESH).wait_recv()
        for cp in sends:
            cp.wait_send()
        acc = all_ref[0]
        for k in range(1, N_DEV):
            acc = acc + all_ref[k]
        sum_ref[...] = acc

    return pl.pallas_call(
        body, name=name,
        in_specs=[pl.BlockSpec(memory_space=pltpu.VMEM)],
        out_specs=[pl.BlockSpec(memory_space=pltpu.VMEM), pl.BlockSpec(memory_space=pltpu.VMEM)],
        out_shape=[jax.ShapeDtypeStruct((R, LANES), F32), jax.ShapeDtypeStruct((N_DEV, R, LANES), F32)],
        scratch_shapes=[pltpu.SemaphoreType.DMA((N_DEV - 1,)), pltpu.SemaphoreType.DMA((N_DEV - 1,))],
        compiler_params=pltpu.CompilerParams(has_side_effects=True, vmem_limit_bytes=VMEM_LIMIT),
    )(buf)[0]


def sum_slots(buf, name):
    _, R, _ = buf.shape

    def body(in_ref, o_ref):
        acc = in_ref[0].astype(F32)
        for k in range(1, N_CHIPS):
            acc = acc + in_ref[k].astype(F32)
        o_ref[...] = acc

    return pl.pallas_call(
        body, name=name, grid=(R // PACK_ROWS,),
        in_specs=[pl.BlockSpec((N_CHIPS, PACK_ROWS, LANES), lambda i: (0, i, 0))],
        out_specs=pl.BlockSpec((PACK_ROWS, LANES), lambda i: (i, 0)),
        out_shape=jax.ShapeDtypeStruct((R, LANES), F32),
        compiler_params=_cparams(("parallel",)),
    )(buf)


def adamw(g_parts, w, m, v, name):
    R = w.shape[0]
    npart = len(g_parts)

    def body(*refs):
        g = refs[0][...]
        for r in refs[1:npart]:
            g = g + r[...]
        w_ref, m_ref, v_ref, g_out, d_out, m_out, v_out = refs[npart:]
        m2 = ADAM_B1 * m_ref[...] + (1.0 - ADAM_B1) * g
        v2 = ADAM_B2 * v_ref[...] + (1.0 - ADAM_B2) * (g * g)
        m_hat = m2 / (1.0 - ADAM_B1 ** ADAM_STEP)
        v_hat = v2 / (1.0 - ADAM_B2 ** ADAM_STEP)
        g_out[...] = g
        d_out[...] = -ADAM_LR * (m_hat / (jnp.sqrt(v_hat) + ADAM_EPS) + ADAM_WD * w_ref[...])
        m_out[...] = m2
        v_out[...] = v2

    spec = pl.BlockSpec((PACK_ROWS, LANES), lambda i: (i, 0))
    return pl.pallas_call(
        body, name=name, grid=(R // PACK_ROWS,),
        in_specs=[spec] * (npart + 3), out_specs=[spec] * 4,
        out_shape=[jax.ShapeDtypeStruct((R, LANES), F32)] * 4,
        compiler_params=_cparams(("parallel",)),
    )(*g_parts, w, m, v)


def _shard_of(full, axis, chip):
    size = full.shape[axis] // N_CHIPS
    return lax.slice_in_dim(full, chip * size, (chip + 1) * size, axis=axis)


def kernel(x, mem, norm_mix_g, w_in, sb_q_norm_g, sb_k_norm_g, conv_dw_w, conv_dw_b, conv_ln_g, conv_ln_b, conv_pw2_w, ssm_lam_re, ssm_lam_im, ssm_log_dt, ssm_b_re, ssm_b_im, ssm_c_re, ssm_c_im, ssm_d, ssm_glu_w, branch_norm_g, w_out, norm_xa_g, norm_mem_g, xa_wq, xa_wk, xa_wv, xa_q_norm_g, xa_k_norm_g, xa_wo, norm_ffn_g, ffn_w_in, ffn_w_out, loss_target, m_norm_mix_g, m_w_in, m_sb_q_norm_g, m_sb_k_norm_g, m_conv_dw_w, m_conv_dw_b, m_conv_ln_g, m_conv_ln_b, m_conv_pw2_w, m_ssm_lam_re, m_ssm_lam_im, m_ssm_log_dt, m_ssm_b_re, m_ssm_b_im, m_ssm_c_re, m_ssm_c_im, m_ssm_d, m_ssm_glu_w, m_branch_norm_g, m_w_out, m_norm_xa_g, m_norm_mem_g, m_xa_wq, m_xa_wk, m_xa_wv, m_xa_q_norm_g, m_xa_k_norm_g, m_xa_wo, m_norm_ffn_g, m_ffn_w_in, m_ffn_w_out, v_norm_mix_g, v_w_in, v_sb_q_norm_g, v_sb_k_norm_g, v_conv_dw_w, v_conv_dw_b, v_conv_ln_g, v_conv_ln_b, v_conv_pw2_w, v_ssm_lam_re, v_ssm_lam_im, v_ssm_log_dt, v_ssm_b_re, v_ssm_b_im, v_ssm_c_re, v_ssm_c_im, v_ssm_d, v_ssm_glu_w, v_branch_norm_g, v_w_out, v_norm_xa_g, v_norm_mem_g, v_xa_wq, v_xa_wk, v_xa_wv, v_xa_q_norm_g, v_xa_k_norm_g, v_xa_wo, v_norm_ffn_g, v_ffn_w_in, v_ffn_w_out):
    given = dict(locals())
    w = {n: given[n] for n in WEIGHTS}
    m = {n: given['m_' + n] for n in WEIGHTS}
    v = {n: given['v_' + n] for n in WEIGHTS}

    shard_shapes = [w[n].shape for n in SHARDED]
    gathered = exchange_xy(pack([w[n] for n in SHARDED], BF16), False, 'gather_weights')
    per_chip = [unpack(gathered[chip], shard_shapes) for chip in range(N_CHIPS)]
    full = dict(w)
    for k, n in enumerate(SHARDED):
        full[n] = jnp.concatenate([per_chip[chip][k] for chip in range(N_CHIPS)],
                                  axis=SHARD_AXIS[n]).astype(F32)

    loss, gx, gw = local_step(full, x[0], mem[0], loss_target[0])

    to_chips = jnp.stack([pack([_shard_of(gw[n], SHARD_AXIS[n], chip) for n in SHARDED], BF16)
                          for chip in range(N_CHIPS)])
    mine = sum_slots(exchange_xy(to_chips, True, 'scatter_grads'), 'sum_grads')
    other = exchange_c(mine, 'swap_cores')
    g_s, d_s, m_s, v_s = adamw([mine, other], pack([w[n] for n in SHARDED], F32),
                               pack([m[n] for n in SHARDED], F32), pack([v[n] for n in SHARDED], F32),
                               'adamw_sharded')

    small_shapes = [w[n].shape for n in REPLICATED] + [(1,)]
    small = allreduce_small(pack([gw[n] for n in REPLICATED] + [loss.reshape(1)], F32), 'allreduce_small')
    g_r, d_r, m_r, v_r = adamw([small], pack([w[n] for n in REPLICATED] + [jnp.zeros((1,), F32)], F32),
                               pack([m[n] for n in REPLICATED] + [jnp.zeros((1,), F32)], F32),
                               pack([v[n] for n in REPLICATED] + [jnp.zeros((1,), F32)], F32),
                               'adamw_replicated')

    outs = {}
    for kind, ps, pr in (('grad', g_s, g_r), ('delta', d_s, d_r), ('new_m', m_s, m_r), ('new_v', v_s, v_r)):
        us = unpack(ps, shard_shapes)
        ur = unpack(pr, small_shapes)
        for k, n in enumerate(SHARDED):
            outs[kind + '_' + n] = us[k]
        for k, n in enumerate(REPLICATED):
            outs[kind + '_' + n] = ur[k]
        if kind == 'grad':
            total_loss = ur[-1].reshape(())
    return (total_loss, gx[None], *[outs['grad_' + n] for n in WEIGHTS], *[outs['delta_' + n] for n in WEIGHTS],
            *[outs['new_m_' + n] for n in WEIGHTS], *[outs['new_v_' + n] for n in WEIGHTS])
```

```python
import functools
import math

import jax
import jax.numpy as jnp
from jax import lax
from jax.experimental import pallas as pl
from jax.experimental.pallas import tpu as pltpu

F32 = jnp.float32
BF16 = jnp.bfloat16
MESH = pl.DeviceIdType.MESH
HIGHEST = lax.Precision.HIGHEST

EPS = 1e-6
LANES = 128
SUBLANES = 8
VMEM_LIMIT = 56 * 1024 * 1024

SB_HEAD_DIM = 64
SB_WIDTH = 512
CONV_CH = 256
CONV_WIDTH = 31
CONV_HALO = 32
SSM_CH = 256
SSM_GROUPS = 16
SSM_GROUP = 16
SSM_STATE = 64
SSM_LANES = SSM_GROUPS * SSM_STATE
XA_HEADS = 4
XA_HEAD_DIM = 256
SB_CUT = 110.0

ADAM_LR = 0.001
ADAM_B1 = 0.9
ADAM_B2 = 0.999
ADAM_EPS = 1e-08
ADAM_WD = 0.01
ADAM_STEP = 10

WEIGHTS = ['norm_mix_g', 'w_in', 'sb_q_norm_g', 'sb_k_norm_g', 'conv_dw_w', 'conv_dw_b', 'conv_ln_g',
           'conv_ln_b', 'conv_pw2_w', 'ssm_lam_re', 'ssm_lam_im', 'ssm_log_dt', 'ssm_b_re', 'ssm_b_im',
           'ssm_c_re', 'ssm_c_im', 'ssm_d', 'ssm_glu_w', 'branch_norm_g', 'w_out', 'norm_xa_g',
           'norm_mem_g', 'xa_wq', 'xa_wk', 'xa_wv', 'xa_q_norm_g', 'xa_k_norm_g', 'xa_wo', 'norm_ffn_g',
           'ffn_w_in', 'ffn_w_out']
SHARD_AXIS = {'w_in': 2, 'conv_dw_w': 2, 'conv_pw2_w': 1, 'ssm_glu_w': 2, 'w_out': 1, 'xa_wq': 1,
              'xa_wk': 1, 'xa_wv': 1, 'xa_wo': 1, 'ffn_w_in': 2, 'ffn_w_out': 1}
MATRICES = [n for n in WEIGHTS if n in SHARD_AXIS and n != 'conv_dw_w']
REPLICATED = [n for n in WEIGHTS if n not in SHARD_AXIS]
N_CHIPS = 4
N_DEV = 8


def _cparams(sem=None, **kw):
    if sem is not None:
        kw['dimension_semantics'] = sem
    return pltpu.CompilerParams(vmem_limit_bytes=VMEM_LIMIT, **kw)


def _pick(n, target):
    best = None
    d = LANES
    while d <= min(n, target):
        if n % d == 0:
            best = d
        d += LANES
    return best if best is not None else n


def _rows_for(n_rows, width):
    t = 512
    while t > 8 and t * width > 768 * 1024:
        t //= 2
    return min(t, n_rows)


def _dg(a, b, ca, cb):
    return lax.dot_general(a.astype(BF16), b.astype(BF16), (((ca,), (cb,)), ((), ())),
                           preferred_element_type=F32)


@jax.custom_vjp
def bdot_nn(a, b):
    return _dg(a, b, 1, 0)


def _bdot_nn_fwd(a, b):
    return _dg(a, b, 1, 0), (a, b)


def _bdot_nn_bwd(res, g):
    a, b = res
    return _dg(g, b, 1, 1), _dg(a, g, 0, 0)


bdot_nn.defvjp(_bdot_nn_fwd, _bdot_nn_bwd)


@jax.custom_vjp
def bdot_nt(a, b):
    return _dg(a, b, 1, 1)


def _bdot_nt_fwd(a, b):
    return _dg(a, b, 1, 1), (a, b)


def _bdot_nt_bwd(res, g):
    a, b = res
    return _dg(g, b, 1, 0), _dg(g, a, 0, 0)


bdot_nt.defvjp(_bdot_nt_fwd, _bdot_nt_bwd)


def mm(a, b, mode, name, out_dtype=F32):
    if mode == 'nn':
        M, K = a.shape
        N = b.shape[1]
    elif mode == 'nt':
        M, K = a.shape
        N = b.shape[0]
    else:
        K, M = a.shape
        N = b.shape[1]
    if mode == 'tn':
        tm, tn, tk = _pick(M, 1536), _pick(N, 1536), _pick(K, 512)
    else:
        tm, tn = _pick(M, 512), _pick(N, 1536)
        tk = K if K <= 2816 else _pick(K, 1536)
    nk = K // tk
    ca, cb = {'nn': (1, 0), 'nt': (1, 1), 'tn': (0, 0)}[mode]

    def body(a_ref, b_ref, o_ref, *scratch):
        if nk == 1:
            o_ref[...] = _dg(a_ref[...], b_ref[...], ca, cb).astype(out_dtype)
            return
        acc_ref, = scratch
        k = pl.program_id(2)

        @pl.when(k == 0)
        def _():
            acc_ref[...] = jnp.zeros_like(acc_ref)

        acc_ref[...] += _dg(a_ref[...], b_ref[...], ca, cb)

        @pl.when(k == nk - 1)
        def _():
            o_ref[...] = acc_ref[...].astype(out_dtype)

    if mode == 'nn':
        a_spec = pl.BlockSpec((tm, tk), lambda i, j, k: (i, k))
        b_spec = pl.BlockSpec((tk, tn), lambda i, j, k: (k, j))
    elif mode == 'nt':
        a_spec = pl.BlockSpec((tm, tk), lambda i, j, k: (i, k))
        b_spec = pl.BlockSpec((tn, tk), lambda i, j, k: (j, k))
    else:
        a_spec = pl.BlockSpec((tk, tm), lambda i, j, k: (k, i))
        b_spec = pl.BlockSpec((tk, tn), lambda i, j, k: (k, j))
    return pl.pallas_call(
        body, name=name, grid=(M // tm, N // tn, nk),
        in_specs=[a_spec, b_spec],
        out_specs=pl.BlockSpec((tm, tn), lambda i, j, k: (i, j)),
        out_shape=jax.ShapeDtypeStruct((M, N), out_dtype),
        scratch_shapes=[pltpu.VMEM((tm, tn), F32)] if nk > 1 else [],
        compiler_params=_cparams(("parallel", "parallel", "arbitrary")),
    )(a, b)


def linear(x, w, slot, name):
    @jax.custom_vjp
    def op(x, w, slot):
        return mm(x, w, 'nn', name + '_fwd')

    def op_fwd(x, w, slot):
        return mm(x, w, 'nn', name + '_fwd'), (x, w)

    def op_bwd(res, g):
        x, w = res
        return mm(g, w, 'nt', name + '_dx'), jnp.zeros_like(w), mm(x, g, 'tn', name + '_dw', BF16)

    op.defvjp(op_fwd, op_bwd)
    return op(x, w, slot)


def rowwise(f, rows, params, consts, out_widths, name, need_row_grad=None, block_rows=None):
    nr, npar, nc, nout = len(rows), len(params), len(consts), len(out_widths)
    L = rows[0].shape[0]
    widths = [r.shape[1] for r in rows]
    T = block_rows or _rows_for(L, max(widths + list(out_widths)))
    n = L // T
    need = list(need_row_grad) if need_row_grad is not None else [True] * nr
    pshapes = [p.shape for p in params]
    cshapes = [c.shape for c in consts]

    row_specs = [pl.BlockSpec((T, w), lambda i: (i, 0)) for w in widths]
    par_specs = [pl.BlockSpec(s, lambda i: (0, 0)) for s in pshapes]
    con_specs = [pl.BlockSpec(s, lambda i: (0, 0)) for s in cshapes]
    out_specs = [pl.BlockSpec((T, w), lambda i: (i, 0)) for w in out_widths]

    def fwd_call(rows, params, consts):
        def body(*refs):
            ins = [r[...] for r in refs[:nr + npar + nc]]
            outs = f(*ins)
            for o_ref, val in zip(refs[nr + npar + nc:], outs):
                o_ref[...] = val

        return pl.pallas_call(
            body, name=name + '_fwd', grid=(n,),
            in_specs=row_specs + par_specs + con_specs, out_specs=out_specs,
            out_shape=[jax.ShapeDtypeStruct((L, w), F32) for w in out_widths],
            compiler_params=_cparams(("parallel",)),
        )(*rows, *params, *consts)

    def bwd_call(rows, params, consts, cts):
        grad_rows = [k for k in range(nr) if need[k]]

        def body(*refs):
            i = pl.program_id(0)
            rv = [r[...] for r in refs[:nr]]
            pv = [r[...] for r in refs[nr:nr + npar]]
            cv = [r[...] for r in refs[nr + npar:nr + npar + nc]]
            ctv = tuple(r[...] for r in refs[nr + npar + nc:nr + npar + nc + nout])
            orefs = refs[nr + npar + nc + nout:]
            _, vjp = jax.vjp(lambda *rp: tuple(f(*rp, *cv)), *rv, *pv)
            g = vjp(ctv)
            for slot, k in enumerate(grad_rows):
                orefs[slot][...] = g[k]

            @pl.when(i == 0)
            def _():
                for k in range(npar):
                    orefs[len(grad_rows) + k][...] = jnp.zeros(pshapes[k], F32)

            for k in range(npar):
                orefs[len(grad_rows) + k][...] += g[nr + k]

        outs = pl.pallas_call(
            body, name=name + '_bwd', grid=(n,),
            in_specs=row_specs + par_specs + con_specs + out_specs,
            out_specs=[row_specs[k] for k in grad_rows] + par_specs,
            out_shape=[jax.ShapeDtypeStruct((L, widths[k]), F32) for k in grad_rows]
            + [jax.ShapeDtypeStruct(s, F32) for s in pshapes],
            compiler_params=_cparams(("arbitrary",)),
        )(*rows, *params, *consts, *cts)
        drows = []
        slot = 0
        for k in range(nr):
            if need[k]:
                drows.append(outs[slot])
                slot += 1
            else:
                drows.append(jnp.zeros_like(rows[k]))
        return tuple(drows), tuple(outs[len(grad_rows):])

    @jax.custom_vjp
    def op(rows, params, consts):
        return tuple(fwd_call(rows, params, consts))

    def op_fwd(rows, params, consts):
        return tuple(fwd_call(rows, params, consts)), (rows, params, consts)

    def op_bwd(res, cts):
        rows, params, consts = res
        drows, dparams = bwd_call(rows, params, consts, cts)
        return drows, dparams, tuple(jnp.zeros_like(c) for c in consts)

    op.defvjp(op_fwd, op_bwd)
    return op(tuple(rows), tuple(params), tuple(consts))


def _rms(x, g):
    return x * lax.rsqrt(jnp.mean(x * x, axis=-1, keepdims=True) + EPS) * g


def rmsnorm(x, g, name):
    return rowwise(lambda x, g: (_rms(x, g),), [x], [g.reshape(1, -1)], [], [x.shape[1]], name)[0]


def _split3(x):
    hi = x.astype(BF16)
    r = x - hi.astype(F32)
    mid = r.astype(BF16)
    return hi, mid, (r - mid.astype(F32)).astype(BF16)


@jax.custom_vjp
def select_mm(x, sel):
    return sum(_dg(t, sel, 1, 0) for t in _split3(x))


def _select_mm_fwd(x, sel):
    return select_mm(x, sel), sel


def _select_mm_bwd(sel, g):
    return sum(_dg(t, sel, 1, 1) for t in _split3(g)), jnp.zeros_like(sel)


select_mm.defvjp(_select_mm_fwd, _select_mm_bwd)


def groupnorm(x, g, group, name):
    width = x.shape[1]
    g_full = jnp.tile(g.reshape(1, group), (1, width // group))
    if group % LANES == 0:
        def f(x, g_full):
            outs = []
            for lo in range(0, width, group):
                xs = x[:, lo:lo + group]
                outs.append(_rms(xs, g_full[:, lo:lo + group]))
            return (jnp.concatenate(outs, axis=-1),)

        return rowwise(f, [x], [g_full], [], [width], name)[0]

    gid = jnp.arange(width) // group
    sel = (gid[:, None] == jnp.arange(LANES)[None, :]).astype(BF16)

    def f(x, g_full, sel, sel_t):
        ms = select_mm(x * x, sel) * (1.0 / group)
        inv = select_mm(lax.rsqrt(ms + EPS), sel_t)
        return (x * inv * g_full,)

    return rowwise(f, [x], [g_full], [sel, sel.T], [width], name)[0]


def glu(x, name):
    half = x.shape[1] // 2

    def f(x):
        return (x[:, :half] * jax.nn.sigmoid(x[:, half:]),)

    return rowwise(f, [x], [], [], [half], name)[0]


def swiglu(x, name):
    half = x.shape[1] // 2

    def f(x):
        gate = x[:, :half]
        return (gate * jax.nn.sigmoid(gate) * x[:, half:],)

    return rowwise(f, [x], [], [], [half], name)[0]


def ln_silu(x, g, b, name):
    def f(x, g, b):
        mu = jnp.mean(x, axis=-1, keepdims=True)
        xc = x - mu
        var = jnp.mean(xc * xc, axis=-1, keepdims=True)
        y = xc * lax.rsqrt(var + EPS) * g + b
        return (y * jax.nn.sigmoid(y),)

    return rowwise(f, [x], [g.reshape(1, -1), b.reshape(1, -1)], [], [x.shape[1]], name)[0]


def branch_norms(o_sb, o_conv, o_ssm, g, name):
    w1, w2 = o_sb.shape[1], o_conv.shape[1]

    def f(a, b, c, g):
        return (jnp.concatenate([_rms(a, g[:, :w1]), _rms(b, g[:, w1:w1 + w2]), _rms(c, g[:, w1 + w2:])],
                                axis=-1),)

    return rowwise(f, [o_sb, o_conv, o_ssm], [g.reshape(1, -1)], [], [g.shape[-1]], name)[0]


def xa_core(q, k, v, name):
    scale = XA_HEAD_DIM ** -0.5

    def f(q, k, v):
        outs = []
        for h in range(XA_HEADS):
            sl = slice(h * XA_HEAD_DIM, (h + 1) * XA_HEAD_DIM)
            s = bdot_nt(q[:, sl], k[:, sl]) * scale
            m = lax.stop_gradient(jnp.max(s, axis=-1, keepdims=True))
            e = jnp.exp(s - m)
            p = e / jnp.sum(e, axis=-1, keepdims=True)
            outs.append(bdot_nn(p, v[:, sl]))
        return (jnp.concatenate(outs, axis=-1),)

    return rowwise(f, [q], [k, v], [], [q.shape[1]], name, block_rows=min(256, q.shape[0]))[0]


def loss_rows(y, target, name):
    def f(y, t):
        d = y - t
        return (0.5 * jnp.mean(d * d, axis=-1, keepdims=True),)

    return rowwise(f, [y, target], [], [], [1], name, need_row_grad=[True, False])[0]


def _hilo(x, ones_bf16):
    hi = x.astype(BF16)
    lo = (x - hi.astype(F32)).astype(BF16)
    return _dg(hi, ones_bf16, 1, 0) + _dg(lo, ones_bf16, 1, 0)


def _sb_block(qh, kb, c, valid, strict_upper):
    z = _dg(qh, kb, 1, 1)
    a = jnp.minimum(z, 0.0) - jnp.log(1.0 + jnp.exp(-jnp.abs(z)))
    b = jnp.where(valid, a - z, 0.0)
    s = _hilo(b, strict_upper) + c
    w = jnp.where(valid, jnp.exp(a + s), 0.0)
    return a, b, w


def _sb_masks(T):
    row = lax.broadcasted_iota(jnp.int32, (T, T), 0)
    col = lax.broadcasted_iota(jnp.int32, (T, T), 1)
    return col < row, (row > col).astype(BF16), (row >= col).astype(BF16)


HEADS_PER_BLOCK = LANES // SB_HEAD_DIM


def _head_mask(h):
    lane = lax.broadcasted_iota(jnp.int32, (1, LANES), 1)
    return (lane // SB_HEAD_DIM == h).astype(F32)


def _max_all(columns):
    m = columns[0]
    for c in columns[1:]:
        m = jnp.maximum(m, c)
    return jnp.max(m)


def _sb_fwd_call(q, k, v, T, name):
    L, W = q.shape
    scale = SB_HEAD_DIM ** -0.5

    def body(q_ref, k_ref, v_ref, o_ref):
        i = pl.program_id(1)
        causal, strict_upper, _ = _sb_masks(T)
        q2 = q_ref[...] * scale
        masks = [_head_mask(h) for h in range(HEADS_PER_BLOCK)]
        qs = [(q2 * hm).astype(BF16) for hm in masks]
        zero = jnp.zeros((T, 1), F32)

        def cond(state):
            j, cs, _ = state
            return jnp.logical_and(j >= 0, _max_all(cs) > -SB_CUT)

        def step(state):
            j, cs, acc = state
            r0 = pl.multiple_of(j * T, T)
            valid = jnp.logical_or(causal, j != i)
            kb = k_ref[pl.ds(r0, T), :].astype(BF16)
            vb = v_ref[pl.ds(r0, T), :]
            new_cs = []
            for hm, qh, c in zip(masks, qs, cs):
                _, b, w = _sb_block(qh, kb, c, valid, strict_upper)
                vh = (vb * hm).astype(BF16)
                w_hi = w.astype(BF16)
                w_lo = (w - w_hi.astype(F32)).astype(BF16)
                acc = acc + _dg(w_hi, vh, 1, 0) + _dg(w_lo, vh, 1, 0)
                new_cs.append(c + jnp.sum(b, axis=1, keepdims=True))
            return j - 1, tuple(new_cs), acc

        _, _, acc = lax.while_loop(cond, step, (i, (zero,) * HEADS_PER_BLOCK, jnp.zeros((T, LANES), F32)))
        o_ref[...] = acc

    return pl.pallas_call(
        body, name=name, grid=(W // LANES, L // T),
        in_specs=[pl.BlockSpec((T, LANES), lambda p, i: (i, p)),
                  pl.BlockSpec((L, LANES), lambda p, i: (0, p)),
                  pl.BlockSpec((L, LANES), lambda p, i: (0, p))],
        out_specs=pl.BlockSpec((T, LANES), lambda p, i: (i, p)),
        out_shape=jax.ShapeDtypeStruct((L, W), F32),
        compiler_params=_cparams(("parallel", "parallel")),
    )(q, k, v)


def _sb_bwd_call(q, k, v, o, do, T, name):
    L, W = q.shape
    scale = SB_HEAD_DIM ** -0.5

    def body(q_ref, k_ref, v_ref, o_ref, do_ref, dq_ref, dk_ref, dv_ref):
        i = pl.program_id(1)

        @pl.when(i == 0)
        def _():
            dk_ref[...] = jnp.zeros_like(dk_ref)
            dv_ref[...] = jnp.zeros_like(dv_ref)

        causal, strict_upper, upper = _sb_masks(T)
        q2 = q_ref[...] * scale
        do2 = do_ref[...]
        o2 = o_ref[...]
        masks = [_head_mask(h) for h in range(HEADS_PER_BLOCK)]
        qs = [(q2 * hm).astype(BF16) for hm in masks]
        dos = [(do2 * hm).astype(BF16) for hm in masks]
        totals = [jnp.sum(doh.astype(F32) * o2, axis=1, keepdims=True) for doh in dos]
        zero = jnp.zeros((T, 1), F32)

        def cond(state):
            j, cs, _, _ = state
            return jnp.logical_and(j >= 0, _max_all(cs) > -SB_CUT)

        def step(state):
            j, cs, rs, dq = state
            r0 = pl.multiple_of(j * T, T)
            valid = jnp.logical_or(causal, j != i)
            kf = k_ref[pl.ds(r0, T), :]
            kb = kf.astype(BF16)
            vb = v_ref[pl.ds(r0, T), :].astype(BF16)
            new_cs, new_rs = [], []
            dk = jnp.zeros((T, LANES), F32)
            dv = jnp.zeros((T, LANES), F32)
            for hm, qh, doh, total, c, r in zip(masks, qs, dos, totals, cs, rs):
                a, b, w = _sb_block(qh, kb, c, valid, strict_upper)
                e = _dg(doh, vb, 1, 1) * w
                before = total - (_hilo(e, upper) + r)
                dz = jnp.where(valid, e * jnp.exp(b) - before * jnp.exp(a), 0.0).astype(BF16)
                dq = dq + _dg(dz, kf * hm, 1, 0)
                dk = dk + _dg(dz, qh, 0, 0)
                dv = dv + _dg(w, doh, 0, 0)
                new_cs.append(c + jnp.sum(b, axis=1, keepdims=True))
                new_rs.append(r + jnp.sum(e, axis=1, keepdims=True))
            dk_ref[pl.ds(r0, T), :] += dk
            dv_ref[pl.ds(r0, T), :] += dv
            return j - 1, tuple(new_cs), tuple(new_rs), dq

        init = (i, (zero,) * HEADS_PER_BLOCK, (zero,) * HEADS_PER_BLOCK, jnp.zeros((T, LANES), F32))
        dq = lax.while_loop(cond, step, init)[3]
        dq_ref[...] = dq * scale

    blk = pl.BlockSpec((T, LANES), lambda p, i: (i, p))
    full = pl.BlockSpec((L, LANES), lambda p, i: (0, p))
    return pl.pallas_call(
        body, name=name, grid=(W // LANES, L // T),
        in_specs=[blk, full, full, blk, blk],
        out_specs=[blk, full, full],
        out_shape=[jax.ShapeDtypeStruct((L, W), F32)] * 3,
        compiler_params=_cparams(("parallel", "arbitrary")),
    )(q, k, v, o, do)


def sb_attention(q, k, v, name):
    T = min(256, q.shape[0])

    @jax.custom_vjp
    def op(q, k, v):
        return _sb_fwd_call(q, k, v, T, name + '_fwd')

    def op_fwd(q, k, v):
        o = _sb_fwd_call(q, k, v, T, name + '_fwd')
        return o, (q, k, v, o)

    def op_bwd(res, do):
        q, k, v, o = res
        return tuple(_sb_bwd_call(q, k, v, o, do, T, name + '_bwd'))

    op.defvjp(op_fwd, op_bwd)
    return op(q, k, v)


def _dwconv_fwd_call(x, w, b, T, name):
    L, C = x.shape
    per = T // CONV_HALO
    lead = CONV_HALO - (CONV_WIDTH - 1)

    def body(x_ref, halo_ref, w_ref, b_ref, o_ref, buf):
        i = pl.program_id(0)
        buf[0:CONV_HALO, :] = jnp.where(i > 0, halo_ref[...], 0.0)
        buf[CONV_HALO:CONV_HALO + T, :] = x_ref[...]
        acc = jnp.zeros((T, C), F32) + b_ref[...]
        for j in range(CONV_WIDTH):
            acc = acc + w_ref[j:j + 1, :] * buf[lead + j:lead + j + T, :]
        o_ref[...] = acc

    return pl.pallas_call(
        body, name=name, grid=(L // T,),
        in_specs=[pl.BlockSpec((T, C), lambda i: (i, 0)),
                  pl.BlockSpec((CONV_HALO, C), lambda i: (jnp.maximum(i * per - 1, 0), 0)),
                  pl.BlockSpec(w.shape, lambda i: (0, 0)),
                  pl.BlockSpec(b.shape, lambda i: (0, 0))],
        out_specs=pl.BlockSpec((T, C), lambda i: (i, 0)),
        out_shape=jax.ShapeDtypeStruct((L, C), F32),
        scratch_shapes=[pltpu.VMEM((T + CONV_HALO, C), F32)],
        compiler_params=_cparams(("parallel",)),
    )(x, x, w, b)


def _dwconv_bwd_call(x, w, g, T, name):
    L, C = x.shape
    per = T // CONV_HALO
    n = L // T
    last_halo = L // CONV_HALO - 1
    lead = CONV_HALO - (CONV_WIDTH - 1)

    def body(x_ref, xh_ref, g_ref, gh_ref, w_ref, dx_ref, dw_ref, db_ref, bufx, bufg):
        i = pl.program_id(0)
        bufx[0:CONV_HALO, :] = jnp.where(i > 0, xh_ref[...], 0.0)
        bufx[CONV_HALO:CONV_HALO + T, :] = x_ref[...]
        gm = g_ref[...]
        bufg[0:T, :] = gm
        bufg[T:T + CONV_HALO, :] = jnp.where(i < n - 1, gh_ref[...], 0.0)
        acc = jnp.zeros((T, C), F32)
        for j in range(CONV_WIDTH):
            off = CONV_WIDTH - 1 - j
            acc = acc + w_ref[j:j + 1, :] * bufg[off:off + T, :]
        dx_ref[...] = acc

        @pl.when(i == 0)
        def _():
            dw_ref[...] = jnp.zeros_like(dw_ref)
            db_ref[...] = jnp.zeros_like(db_ref)

        for j in range(CONV_WIDTH):
            dw_ref[j:j + 1, :] += jnp.sum(gm * bufx[lead + j:lead + j + T, :], axis=0, keepdims=True)
        db_ref[...] += jnp.sum(gm, axis=0, keepdims=True)

    return pl.pallas_call(
        body, name=name, grid=(n,),
        in_specs=[pl.BlockSpec((T, C), lambda i: (i, 0)),
                  pl.BlockSpec((CONV_HALO, C), lambda i: (jnp.maximum(i * per - 1, 0), 0)),
                  pl.BlockSpec((T, C), lambda i: (i, 0)),
                  pl.BlockSpec((CONV_HALO, C), lambda i: (jnp.minimum((i + 1) * per, last_halo), 0)),
                  pl.BlockSpec(w.shape, lambda i: (0, 0))],
        out_specs=[pl.BlockSpec((T, C), lambda i: (i, 0)),
                   pl.BlockSpec(w.shape, lambda i: (0, 0)),
                   pl.BlockSpec((1, C), lambda i: (0, 0))],
        out_shape=[jax.ShapeDtypeStruct((L, C), F32), jax.ShapeDtypeStruct(w.shape, F32),
                   jax.ShapeDtypeStruct((1, C), F32)],
        scratch_shapes=[pltpu.VMEM((T + CONV_HALO, C), F32), pltpu.VMEM((T + CONV_HALO, C), F32)],
        compiler_params=_cparams(("arbitrary",)),
    )(x, x, g, g, w)


def dwconv(x, w, b, name):
    T = min(512, x.shape[0])

    @jax.custom_vjp
    def op(x, w, b):
        return _dwconv_fwd_call(x, w, b, T, name + '_fwd')

    def op_fwd(x, w, b):
        return _dwconv_fwd_call(x, w, b, T, name + '_fwd'), (x, w)

    def op_bwd(res, g):
        x, w = res
        return tuple(_dwconv_bwd_call(x, w, g, T, name + '_bwd'))

    op.defvjp(op_fwd, op_bwd)
    return op(x, w, b)


def _ssm_fwd_call(u, ar, ai, bbr, bbi, cr, ci, d, T, name):
    L, C = u.shape
    S = SSM_LANES

    def body(u_ref, ar_ref, ai_ref, bbr_ref, bbi_ref, cr_ref, ci_ref, d_ref,
             y_ref, xr_ref, xi_ref, st_r, st_i, in_r, in_i, out_r, out_i):
        i = pl.program_id(0)

        @pl.when(i == 0)
        def _():
            st_r[...] = jnp.zeros_like(st_r)
            st_i[...] = jnp.zeros_like(st_i)

        u_blk = u_ref[...]
        xr_ref[...] = _dg(u_blk, bbr_ref[...], 1, 0)
        xi_ref[...] = _dg(u_blk, bbi_ref[...], 1, 0)
        a_r, a_i = ar_ref[...], ai_ref[...]

        def tile(t, carry):
            sr, si = carry
            r0 = pl.multiple_of(t * SUBLANES, SUBLANES)
            in_r[...] = xr_ref[pl.ds(r0, SUBLANES), :]
            in_i[...] = xi_ref[pl.ds(r0, SUBLANES), :]
            for r in range(SUBLANES):
                nr = a_r * sr - a_i * si + in_r[r:r + 1, :]
                ni = a_r * si + a_i * sr + in_i[r:r + 1, :]
                sr, si = nr, ni
                out_r[r:r + 1, :] = sr
                out_i[r:r + 1, :] = si
            xr_ref[pl.ds(r0, SUBLANES), :] = out_r[...]
            xi_ref[pl.ds(r0, SUBLANES), :] = out_i[...]
            return sr, si

        sr, si = lax.fori_loop(0, T // SUBLANES, tile, (st_r[0:1, :], st_i[0:1, :]))
        st_r[0:1, :] = sr
        st_i[0:1, :] = si
        y_ref[...] = (_dg(xr_ref[...], cr_ref[...], 1, 0) - _dg(xi_ref[...], ci_ref[...], 1, 0)
                      + d_ref[...] * u_blk)

    full = lambda a: pl.BlockSpec(a.shape, lambda i: (0, 0))
    return pl.pallas_call(
        body, name=name, grid=(L // T,),
        in_specs=[pl.BlockSpec((T, C), lambda i: (i, 0))] + [full(a) for a in (ar, ai, bbr, bbi, cr, ci, d)],
        out_specs=[pl.BlockSpec((T, C), lambda i: (i, 0)), pl.BlockSpec((T, S), lambda i: (i, 0)),
                   pl.BlockSpec((T, S), lambda i: (i, 0))],
        out_shape=[jax.ShapeDtypeStruct((L, C), F32), jax.ShapeDtypeStruct((L, S), F32),
                   jax.ShapeDtypeStruct((L, S), F32)],
        scratch_shapes=[pltpu.VMEM((SUBLANES, S), F32)] * 6,
        compiler_params=_cparams(("arbitrary",)),
    )(u, ar, ai, bbr, bbi, cr, ci, d)


def _ssm_bwd_call(u, xr, xi, dy, ar, ai, bbr, bbi, cr, ci, d, T, name):
    L, C = u.shape
    S = SSM_LANES
    n = L // T
    per = T // SUBLANES

    def body(u_ref, xr_ref, xi_ref, hr_ref, hi_ref, dy_ref, ar_ref, ai_ref, bbr_ref, bbi_ref, cr_ref, ci_ref,
             d_ref, du_ref, dar_ref, dai_ref, dbr_ref, dbi_ref, dcr_ref, dci_ref, dd_ref,
             lam_r, lam_i, prev_r, prev_i, st_r, st_i, in_r, in_i, out_r, out_i):
        i = pl.program_id(0)
        chunk = n - 1 - i

        @pl.when(i == 0)
        def _():
            st_r[...] = jnp.zeros_like(st_r)
            st_i[...] = jnp.zeros_like(st_i)
            for ref in (dar_ref, dai_ref, dbr_ref, dbi_ref, dcr_ref, dci_ref, dd_ref):
                ref[...] = jnp.zeros_like(ref)

        dy_blk = dy_ref[...]
        u_blk = u_ref[...]
        lam_r[...] = _dg(dy_blk, cr_ref[...], 1, 1)
        lam_i[...] = -_dg(dy_blk, ci_ref[...], 1, 1)
        dcr_ref[...] += _dg(xr_ref[...], dy_blk, 0, 0)
        dci_ref[...] -= _dg(xi_ref[...], dy_blk, 0, 0)
        a_r, a_i = ar_ref[...], ai_ref[...]

        def tile(k, carry):
            lr, li = carry
            r0 = pl.multiple_of((per - 1 - k) * SUBLANES, SUBLANES)
            in_r[...] = lam_r[pl.ds(r0, SUBLANES), :]
            in_i[...] = lam_i[pl.ds(r0, SUBLANES), :]
            for r in range(SUBLANES - 1, -1, -1):
                nr = in_r[r:r + 1, :] + a_r * lr + a_i * li
                ni = in_i[r:r + 1, :] + a_r * li - a_i * lr
                lr, li = nr, ni
                out_r[r:r + 1, :] = lr
                out_i[r:r + 1, :] = li
            lam_r[pl.ds(r0, SUBLANES), :] = out_r[...]
            lam_i[pl.ds(r0, SUBLANES), :] = out_i[...]
            return lr, li

        lr, li = lax.fori_loop(0, per, tile, (st_r[0:1, :], st_i[0:1, :]))
        st_r[0:1, :] = lr
        st_i[0:1, :] = li

        l_r, l_i = lam_r[...], lam_i[...]
        du_ref[...] = _dg(l_r, bbr_ref[...], 1, 1) + _dg(l_i, bbi_ref[...], 1, 1) + d_ref[...] * dy_blk
        dbr_ref[...] += _dg(u_blk, l_r, 0, 0)
        dbi_ref[...] += _dg(u_blk, l_i, 0, 0)
        dd_ref[...] += jnp.sum(dy_blk * u_blk, axis=0, keepdims=True)

        prev_r[0:SUBLANES, :] = jnp.where(chunk > 0, hr_ref[...], 0.0)
        prev_i[0:SUBLANES, :] = jnp.where(chunk > 0, hi_ref[...], 0.0)
        prev_r[SUBLANES:SUBLANES + T, :] = xr_ref[...]
        prev_i[SUBLANES:SUBLANES + T, :] = xi_ref[...]
        p_r = prev_r[SUBLANES - 1:SUBLANES - 1 + T, :]
        p_i = prev_i[SUBLANES - 1:SUBLANES - 1 + T, :]
        dar_ref[...] += jnp.sum(l_r * p_r + l_i * p_i, axis=0, keepdims=True)
        dai_ref[...] += jnp.sum(l_i * p_r - l_r * p_i, axis=0, keepdims=True)

    rev = lambda w: pl.BlockSpec((T, w), lambda i: (n - 1 - i, 0))
    halo = pl.BlockSpec((SUBLANES, S), lambda i: (jnp.maximum((n - 1 - i) * per - 1, 0), 0))
    full = lambda a: pl.BlockSpec(a.shape, lambda i: (0, 0))
    params = (ar, ai, bbr, bbi, cr, ci, d)
    return pl.pallas_call(
        body, name=name, grid=(n,),
        in_specs=[rev(C), rev(S), rev(S), halo, halo, rev(C)] + [full(a) for a in params],
        out_specs=[rev(C)] + [full(a) for a in params],
        out_shape=[jax.ShapeDtypeStruct((L, C), F32)] + [jax.ShapeDtypeStruct(a.shape, F32) for a in params],
        scratch_shapes=[pltpu.VMEM((T, S), F32), pltpu.VMEM((T, S), F32),
                        pltpu.VMEM((T + SUBLANES, S), F32), pltpu.VMEM((T + SUBLANES, S), F32)]
        + [pltpu.VMEM((SUBLANES, S), F32)] * 6,
        compiler_params=_cparams(("arbitrary",)),
    )(u, xr, xi, xr, xi, dy, *params)


def ssm_core(u, ar, ai, bbr, bbi, cr, ci, d, name):
    T = min(256, u.shape[0])

    @jax.custom_vjp
    def op(u, ar, ai, bbr, bbi, cr, ci, d):
        return _ssm_fwd_call(u, ar, ai, bbr, bbi, cr, ci, d, T, name + '_fwd')[0]

    def op_fwd(u, ar, ai, bbr, bbi, cr, ci, d):
        y, xr, xi = _ssm_fwd_call(u, ar, ai, bbr, bbi, cr, ci, d, T, name + '_fwd')
        return y, (u, xr, xi, ar, ai, bbr, bbi, cr, ci, d)

    def op_bwd(res, dy):
        u, xr, xi, ar, ai, bbr, bbi, cr, ci, d = res
        return tuple(_ssm_bwd_call(u, xr, xi, dy, ar, ai, bbr, bbi, cr, ci, d, T, name + '_bwd'))

    op.defvjp(op_fwd, op_bwd)
    return op(u, ar, ai, bbr, bbi, cr, ci, d)


@jax.custom_vjp
def _block_diag(blocks):
    G, R, Cc = blocks.shape
    eye = jnp.eye(G, dtype=blocks.dtype)
    return (blocks[:, :, None, :] * eye[:, None, :, None]).reshape(G * R, G * Cc)


def _block_diag_fwd(blocks):
    return _block_diag(blocks), blocks.shape


def _block_diag_bwd(shape, g):
    G, R, Cc = shape
    return (jnp.stack([g[k * R:(k + 1) * R, k * Cc:(k + 1) * Cc] for k in range(G)]),)


_block_diag.defvjp(_block_diag_fwd, _block_diag_bwd)


def ssm_discretise(lam_re, lam_im, log_dt, b_re, b_im, c_re, c_im):
    dt = jnp.exp(log_dt)[:, None]
    mag = jnp.exp(lam_re * dt)
    ar, ai = mag * jnp.cos(lam_im * dt), mag * jnp.sin(lam_im * dt)
    den = lam_re * lam_re + lam_im * lam_im
    fr = ((ar - 1.0) * lam_re + ai * lam_im) / den
    fi = (ai * lam_re - (ar - 1.0) * lam_im) / den
    bbr = fr[..., None] * b_re - fi[..., None] * b_im
    bbi = fr[..., None] * b_im + fi[..., None] * b_re
    return (ar.reshape(1, SSM_LANES), ai.reshape(1, SSM_LANES),
            _block_diag(bbr.transpose(0, 2, 1)), _block_diag(bbi.transpose(0, 2, 1)),
            _block_diag(c_re.transpose(0, 2, 1)), _block_diag(c_im.transpose(0, 2, 1)))


def split_columns(p, bounds):
    @jax.custom_vjp
    def op(p):
        return tuple(p[:, lo:hi] for lo, hi in zip(bounds[:-1], bounds[1:]))

    def op_fwd(p):
        return op(p), None

    def op_bwd(_, gs):
        return (jnp.concatenate(gs, axis=1),)

    op.defvjp(op_fwd, op_bwd)
    return op(p)


def local_loss(slots, w, mats, x, mem, target):
    depth = len(mats['w_in'])
    s1, s2, s3 = SB_WIDTH, 2 * SB_WIDTH, 3 * SB_WIDTH
    s4 = s3 + 2 * CONV_CH

    def linear_(x, n, l, name):
        return linear(x, mats[n][l], slots[n][l], name)

    for l in range(depth):
        tag = 'l%d_' % l
        h = rmsnorm(x, w['norm_mix_g'][l], tag + 'norm_mix')
        p = linear_(h, 'w_in', l, tag + 'w_in')
        q, k, v, u_conv, u_ssm = split_columns(p, (0, s1, s2, s3, s4, p.shape[1]))
        q = groupnorm(q, w['sb_q_norm_g'][l], SB_HEAD_DIM, tag + 'q_norm')
        k = groupnorm(k, w['sb_k_norm_g'][l], SB_HEAD_DIM, tag + 'k_norm')
        o_sb = sb_attention(q, k, v, tag + 'sb')

        dw_w = jnp.pad(w['conv_dw_w'][l], ((0, CONV_HALO - CONV_WIDTH), (0, 0)))
        hc = dwconv(glu(u_conv, tag + 'conv_glu'), dw_w, w['conv_dw_b'][l].reshape(1, -1), tag + 'dwconv')
        hc = ln_silu(hc, w['conv_ln_g'][l], w['conv_ln_b'][l], tag + 'conv_ln')
        o_conv = linear_(hc, 'conv_pw2_w', l, tag + 'pw2')

        ar, ai, bbr, bbi, cr, ci = ssm_discretise(
            w['ssm_lam_re'][l], w['ssm_lam_im'][l], w['ssm_log_dt'][l], w['ssm_b_re'][l], w['ssm_b_im'][l],
            w['ssm_c_re'][l], w['ssm_c_im'][l])
        y = ssm_core(u_ssm, ar, ai, bbr, bbi, cr, ci, w['ssm_d'][l].reshape(1, -1), tag + 'ssm')
        o_ssm = glu(linear_(y, 'ssm_glu_w', l, tag + 'ssm_glu_w'), tag + 'ssm_glu')

        mixed = branch_norms(o_sb, o_conv, o_ssm, w['branch_norm_g'][l], tag + 'branch_norm')
        x = x + linear_(mixed, 'w_out', l, tag + 'w_out')

        hx = rmsnorm(x, w['norm_xa_g'][l], tag + 'norm_xa')
        hm = rmsnorm(mem, w['norm_mem_g'][l], tag + 'norm_mem')
        qx = groupnorm(linear_(hx, 'xa_wq', l, tag + 'xa_wq'), w['xa_q_norm_g'][l], XA_HEAD_DIM, tag + 'xa_qn')
        kx = groupnorm(linear_(hm, 'xa_wk', l, tag + 'xa_wk'), w['xa_k_norm_g'][l], XA_HEAD_DIM, tag + 'xa_kn')
        vx = linear_(hm, 'xa_wv', l, tag + 'xa_wv')
        x = x + linear_(xa_core(qx, kx, vx, tag + 'xa_core'), 'xa_wo', l, tag + 'xa_wo')

        hf = rmsnorm(x, w['norm_ffn_g'][l], tag + 'norm_ffn')
        act = swiglu(linear_(hf, 'ffn_w_in', l, tag + 'ffn_in'), tag + 'swiglu')
        x = x + linear_(act, 'ffn_w_out', l, tag + 'ffn_out')
    return jnp.sum(loss_rows(x, target, 'loss'))


def local_step(w, mats, x, mem, target):
    slots = {n: [jnp.zeros(m.shape, BF16) for m in ms] for n, ms in mats.items()}
    loss, (g_mats, g_w, gx) = jax.value_and_grad(local_loss, argnums=(0, 1, 3))(slots, w, mats, x, mem, target)
    return loss, gx, g_w, g_mats


PACK_ROWS = 2048


PIECE_ROWS = 16


def _piece_rows(size):
    rows = -(-size // LANES)
    return rows, -(-rows // PIECE_ROWS) * PIECE_ROWS


def pack(arrays, dtype):
    parts, total = [], 0
    for a in arrays:
        rows, padded = _piece_rows(a.size)
        a = a.astype(dtype)
        if a.size % LANES:
            a = jnp.pad(a.reshape(-1), (0, rows * LANES - a.size))
        a = a.reshape(rows, LANES)
        if padded != rows:
            a = jnp.pad(a, ((0, padded - rows), (0, 0)))
        parts.append(a)
        total += padded
    tail = -total % PACK_ROWS
    if tail:
        parts.append(jnp.zeros((tail, LANES), dtype))
    return jnp.concatenate(parts, axis=0)


def unpack(packed, shapes):
    out, off = [], 0
    for s in shapes:
        size = math.prod(s)
        rows, padded = _piece_rows(size)
        piece = packed[off:off + rows]
        if size % LANES:
            piece = piece.reshape(-1)[:size]
        out.append(piece.reshape(s))
        off += padded
    return out


def _mesh_pos():
    return lax.axis_index("x"), lax.axis_index("y"), lax.axis_index("c")


def exchange_xy(buf, per_peer, name):
    R = buf.shape[-2]

    def body(in_ref, out_ref, send_sems, recv_sems, local_sem):
        x, y, c = _mesh_pos()
        me = 2 * x + y
        peers = [(1 - x, y), (x, 1 - y), (1 - x, 1 - y)]

        def src(chip):
            return in_ref.at[chip] if per_peer else in_ref

        local = pltpu.make_async_copy(src(me), out_ref.at[me], local_sem)
        local.start()
        sends = []
        for k, (px, py) in enumerate(peers):
            cp = pltpu.make_async_remote_copy(
                src_ref=src(2 * px + py), dst_ref=out_ref.at[me], send_sem=send_sems.at[k],
                recv_sem=recv_sems.at[k], device_id=(px, py, c), device_id_type=MESH)
            cp.start()
            sends.append(cp)
        for k, (px, py) in enumerate(peers):
            pltpu.make_async_remote_copy(
                src_ref=src(me), dst_ref=out_ref.at[2 * px + py], send_sem=send_sems.at[k],
                recv_sem=recv_sems.at[k], device_id=(px, py, c), device_id_type=MESH).wait_recv()
        for cp in sends:
            cp.wait_send()
        local.wait()

    return pl.pallas_call(
        body, name=name,
        in_specs=[pl.BlockSpec(memory_space=pl.ANY)],
        out_specs=pl.BlockSpec(memory_space=pl.ANY),
        out_shape=jax.ShapeDtypeStruct((N_CHIPS, R, LANES), buf.dtype),
        scratch_shapes=[pltpu.SemaphoreType.DMA((3,)), pltpu.SemaphoreType.DMA((3,)), pltpu.SemaphoreType.DMA(())],
        compiler_params=pltpu.CompilerParams(has_side_effects=True),
    )(buf)


def exchange_c(buf, name):
    def body(in_ref, out_ref, send_sem, recv_sem):
        x, y, c = _mesh_pos()
        cp = pltpu.make_async_remote_copy(
            src_ref=in_ref, dst_ref=out_ref, send_sem=send_sem, recv_sem=recv_sem,
            device_id=(x, y, 1 - c), device_id_type=MESH)
        cp.start()
        cp.wait()

    return pl.pallas_call(
        body, name=name,
        in_specs=[pl.BlockSpec(memory_space=pl.ANY)],
        out_specs=pl.BlockSpec(memory_space=pl.ANY),
        out_shape=jax.ShapeDtypeStruct(buf.shape, buf.dtype),
        scratch_shapes=[pltpu.SemaphoreType.DMA(()), pltpu.SemaphoreType.DMA(())],
        compiler_params=pltpu.CompilerParams(has_side_effects=True),
    )(buf)


def allreduce_small(buf, name):
    R = buf.shape[0]

    def body(in_ref, sum_ref, all_ref, send_sems, recv_sems):
        x, y, c = _mesh_pos()
        me = 4 * x + 2 * y + c
        all_ref[me] = in_ref[...]
        flips = [(fx, fy, fc) for fx in (0, 1) for fy in (0, 1) for fc in (0, 1)][1:]
        sends = []
        for k, (fx, fy, fc) in enumerate(flips):
            cp = pltpu.make_async_remote_copy(
                src_ref=in_ref, dst_ref=all_ref.at[me], send_sem=send_sems.at[k], recv_sem=recv_sems.at[k],
                device_id=(x ^ fx, y ^ fy, c ^ fc), device_id_type=MESH)
            cp.start()
            sends.append(cp)
        for k, (fx, fy, fc) in enumerate(flips):
            peer = 4 * (x ^ fx) + 2 * (y ^ fy) + (c ^ fc)
            pltpu.make_async_remote_copy(
                src_ref=in_ref, dst_ref=all_ref.at[peer], send_sem=send_sems.at[k], recv_sem=recv_sems.at[k],
                device_id=(x ^ fx, y ^ fy, c ^ fc), device_id_type=MESH).wait_recv()
        for cp in sends:
            cp.wait_send()
        acc = all_ref[0]
        for k in range(1, N_DEV):
            acc = acc + all_ref[k]
        sum_ref[...] = acc

    return pl.pallas_call(
        body, name=name,
        in_specs=[pl.BlockSpec(memory_space=pltpu.VMEM)],
        out_specs=[pl.BlockSpec(memory_space=pltpu.VMEM), pl.BlockSpec(memory_space=pltpu.VMEM)],
        out_shape=[jax.ShapeDtypeStruct((R, LANES), F32), jax.ShapeDtypeStruct((N_DEV, R, LANES), F32)],
        scratch_shapes=[pltpu.SemaphoreType.DMA((N_DEV - 1,)), pltpu.SemaphoreType.DMA((N_DEV - 1,))],
        compiler_params=pltpu.CompilerParams(has_side_effects=True, vmem_limit_bytes=VMEM_LIMIT),
    )(buf)[0]


def sum_slots(buf, name):
    _, R, _ = buf.shape

    def body(in_ref, o_ref):
        acc = in_ref[0].astype(F32)
        for k in range(1, N_CHIPS):
            acc = acc + in_ref[k].astype(F32)
        o_ref[...] = acc

    return pl.pallas_call(
        body, name=name, grid=(R // PACK_ROWS,),
        in_specs=[pl.BlockSpec((N_CHIPS, PACK_ROWS, LANES), lambda i: (0, i, 0))],
        out_specs=pl.BlockSpec((PACK_ROWS, LANES), lambda i: (i, 0)),
        out_shape=jax.ShapeDtypeStruct((R, LANES), F32),
        compiler_params=_cparams(("parallel",)),
    )(buf)


def adamw(g_parts, w, m, v, name):
    R = w.shape[0]
    npart = len(g_parts)

    def body(*refs):
        g = refs[0][...]
        for r in refs[1:npart]:
            g = g + r[...]
        w_ref, m_ref, v_ref, g_out, d_out, m_out, v_out = refs[npart:]
        m2 = ADAM_B1 * m_ref[...] + (1.0 - ADAM_B1) * g
        v2 = ADAM_B2 * v_ref[...] + (1.0 - ADAM_B2) * (g * g)
        m_hat = m2 / (1.0 - ADAM_B1 ** ADAM_STEP)
        v_hat = v2 / (1.0 - ADAM_B2 ** ADAM_STEP)
        g_out[...] = g
        d_out[...] = -ADAM_LR * (m_hat / (jnp.sqrt(v_hat) + ADAM_EPS) + ADAM_WD * w_ref[...])
        m_out[...] = m2
        v_out[...] = v2

    spec = pl.BlockSpec((PACK_ROWS, LANES), lambda i: (i, 0))
    return pl.pallas_call(
        body, name=name, grid=(R // PACK_ROWS,),
        in_specs=[spec] * (npart + 3), out_specs=[spec] * 4,
        out_shape=[jax.ShapeDtypeStruct((R, LANES), F32)] * 4,
        compiler_params=_cparams(("parallel",)),
    )(*g_parts, w, m, v)


def _shard_of(full, axis, chip):
    size = full.shape[axis] // N_CHIPS
    return lax.slice_in_dim(full, chip * size, (chip + 1) * size, axis=axis)


def kernel(x, mem, norm_mix_g, w_in, sb_q_norm_g, sb_k_norm_g, conv_dw_w, conv_dw_b, conv_ln_g, conv_ln_b, conv_pw2_w, ssm_lam_re, ssm_lam_im, ssm_log_dt, ssm_b_re, ssm_b_im, ssm_c_re, ssm_c_im, ssm_d, ssm_glu_w, branch_norm_g, w_out, norm_xa_g, norm_mem_g, xa_wq, xa_wk, xa_wv, xa_q_norm_g, xa_k_norm_g, xa_wo, norm_ffn_g, ffn_w_in, ffn_w_out, loss_target, m_norm_mix_g, m_w_in, m_sb_q_norm_g, m_sb_k_norm_g, m_conv_dw_w, m_conv_dw_b, m_conv_ln_g, m_conv_ln_b, m_conv_pw2_w, m_ssm_lam_re, m_ssm_lam_im, m_ssm_log_dt, m_ssm_b_re, m_ssm_b_im, m_ssm_c_re, m_ssm_c_im, m_ssm_d, m_ssm_glu_w, m_branch_norm_g, m_w_out, m_norm_xa_g, m_norm_mem_g, m_xa_wq, m_xa_wk, m_xa_wv, m_xa_q_norm_g, m_xa_k_norm_g, m_xa_wo, m_norm_ffn_g, m_ffn_w_in, m_ffn_w_out, v_norm_mix_g, v_w_in, v_sb_q_norm_g, v_sb_k_norm_g, v_conv_dw_w, v_conv_dw_b, v_conv_ln_g, v_conv_ln_b, v_conv_pw2_w, v_ssm_lam_re, v_ssm_lam_im, v_ssm_log_dt, v_ssm_b_re, v_ssm_b_im, v_ssm_c_re, v_ssm_c_im, v_ssm_d, v_ssm_glu_w, v_branch_norm_g, v_w_out, v_norm_xa_g, v_norm_mem_g, v_xa_wq, v_xa_wk, v_xa_wv, v_xa_q_norm_g, v_xa_k_norm_g, v_xa_wo, v_norm_ffn_g, v_ffn_w_in, v_ffn_w_out):
    given = dict(locals())
    w = {n: given[n] for n in WEIGHTS}
    m = {n: given['m_' + n] for n in WEIGHTS}
    v = {n: given['v_' + n] for n in WEIGHTS}

    depth = w_in.shape[0]
    chip = 2 * lax.axis_index("x") + lax.axis_index("y")

    taps_bits = lax.bitcast_convert_type(conv_dw_w, BF16)
    wire_shapes = [w[n].shape for n in MATRICES] + [taps_bits.shape]
    gathered = exchange_xy(pack([w[n] for n in MATRICES] + [taps_bits], BF16), False, 'gather_weights')
    per_chip = [unpack(gathered[k], wire_shapes) for k in range(N_CHIPS)]
    mats = {}
    for k, n in enumerate(MATRICES):
        whole = jnp.concatenate([per_chip[j][k] for j in range(N_CHIPS)], axis=SHARD_AXIS[n])
        mats[n] = [whole[l] for l in range(depth)]
    taps = jnp.concatenate([lax.bitcast_convert_type(per_chip[j][-1], F32) for j in range(N_CHIPS)], axis=2)
    local_w = {n: w[n] for n in REPLICATED}
    local_w['conv_dw_w'] = taps

    loss, gx, g_w, g_mats = local_step(local_w, mats, x[0], mem[0], loss_target[0])

    to_chips = jnp.stack([pack([_shard_of(g_mats[n][l], SHARD_AXIS[n] - 1, j)
                                for n in MATRICES for l in range(depth)], BF16) for j in range(N_CHIPS)])
    mine = sum_slots(exchange_xy(to_chips, True, 'scatter_grads'), 'sum_grads')
    other = exchange_c(mine, 'swap_cores')
    mat_shapes = [w[n].shape for n in MATRICES]
    g_s, d_s, m_s, v_s = adamw([mine, other], pack([w[n] for n in MATRICES], F32),
                               pack([m[n] for n in MATRICES], F32), pack([v[n] for n in MATRICES], F32),
                               'adamw_matrices')

    reduced = allreduce_small(pack([g_w[n] for n in REPLICATED] + [g_w['conv_dw_w'], loss.reshape(1)], F32),
                              'allreduce_small')
    reduced = unpack(reduced, [w[n].shape for n in REPLICATED] + [taps.shape, (1,)])
    tap_cols = conv_dw_w.shape[2]
    reduced[-2] = lax.dynamic_slice_in_dim(reduced[-2], chip * tap_cols, tap_cols, axis=2)
    small_names = REPLICATED + ['conv_dw_w']
    small_shapes = [w[n].shape for n in small_names] + [(1,)]
    none = [jnp.zeros((1,), F32)]
    g_r, d_r, m_r, v_r = adamw([pack(reduced, F32)], pack([w[n] for n in small_names] + none, F32),
                               pack([m[n] for n in small_names] + none, F32),
                               pack([v[n] for n in small_names] + none, F32), 'adamw_small')

    outs = {}
    for kind, ps, pr in (('grad', g_s, g_r), ('delta', d_s, d_r), ('new_m', m_s, m_r), ('new_v', v_s, v_r)):
        us = unpack(ps, mat_shapes)
        ur = unpack(pr, small_shapes)
        for k, n in enumerate(MATRICES):
            outs[kind + '_' + n] = us[k]
        for k, n in enumerate(small_names):
            outs[kind + '_' + n] = ur[k]
        if kind == 'grad':
            total_loss = ur[-1].reshape(())
    return (total_loss, gx[None], *[outs['grad_' + n] for n in WEIGHTS], *[outs['delta_' + n] for n in WEIGHTS],
            *[outs['new_m_' + n] for n in WEIGHTS], *[outs['new_v_' + n] for n in WEIGHTS])
```

```python
import functools
import math

import jax
import jax.numpy as jnp
from jax import lax
from jax.experimental import pallas as pl
from jax.experimental.pallas import tpu as pltpu

F32 = jnp.float32
BF16 = jnp.bfloat16
MESH = pl.DeviceIdType.MESH
HIGHEST = lax.Precision.HIGHEST

EPS = 1e-6
LANES = 128
SUBLANES = 8
VMEM_LIMIT = 56 * 1024 * 1024

SB_HEAD_DIM = 64
SB_WIDTH = 512
CONV_CH = 256
CONV_WIDTH = 31
CONV_HALO = 32
SSM_CH = 256
SSM_GROUPS = 16
SSM_GROUP = 16
SSM_STATE = 64
SSM_LANES = SSM_GROUPS * SSM_STATE
XA_HEADS = 4
XA_HEAD_DIM = 256
SB_CUT = 110.0

ADAM_LR = 0.001
ADAM_B1 = 0.9
ADAM_B2 = 0.999
ADAM_EPS = 1e-08
ADAM_WD = 0.01
ADAM_STEP = 10

WEIGHTS = ['norm_mix_g', 'w_in', 'sb_q_norm_g', 'sb_k_norm_g', 'conv_dw_w', 'conv_dw_b', 'conv_ln_g',
           'conv_ln_b', 'conv_pw2_w', 'ssm_lam_re', 'ssm_lam_im', 'ssm_log_dt', 'ssm_b_re', 'ssm_b_im',
           'ssm_c_re', 'ssm_c_im', 'ssm_d', 'ssm_glu_w', 'branch_norm_g', 'w_out', 'norm_xa_g',
           'norm_mem_g', 'xa_wq', 'xa_wk', 'xa_wv', 'xa_q_norm_g', 'xa_k_norm_g', 'xa_wo', 'norm_ffn_g',
           'ffn_w_in', 'ffn_w_out']
SHARD_AXIS = {'w_in': 2, 'conv_dw_w': 2, 'conv_pw2_w': 1, 'ssm_glu_w': 2, 'w_out': 1, 'xa_wq': 1,
              'xa_wk': 1, 'xa_wv': 1, 'xa_wo': 1, 'ffn_w_in': 2, 'ffn_w_out': 1}
MATRICES = [n for n in WEIGHTS if n in SHARD_AXIS and n != 'conv_dw_w']
REPLICATED = [n for n in WEIGHTS if n not in SHARD_AXIS]
N_CHIPS = 4
N_DEV = 8


def _cparams(sem=None, **kw):
    if sem is not None:
        kw['dimension_semantics'] = sem
    return pltpu.CompilerParams(vmem_limit_bytes=VMEM_LIMIT, **kw)


def _pick(n, target):
    best = None
    d = LANES
    while d <= min(n, target):
        if n % d == 0:
            best = d
        d += LANES
    return best if best is not None else n


def _rows_for(n_rows, width):
    t = 512
    while t > 8 and t * width > 768 * 1024:
        t //= 2
    return min(t, n_rows)


def _dg(a, b, ca, cb):
    return lax.dot_general(a.astype(BF16), b.astype(BF16), (((ca,), (cb,)), ((), ())),
                           preferred_element_type=F32)


@jax.custom_vjp
def bdot_nn(a, b):
    return _dg(a, b, 1, 0)


def _bdot_nn_fwd(a, b):
    return _dg(a, b, 1, 0), (a, b)


def _bdot_nn_bwd(res, g):
    a, b = res
    return _dg(g, b, 1, 1), _dg(a, g, 0, 0)


bdot_nn.defvjp(_bdot_nn_fwd, _bdot_nn_bwd)


@jax.custom_vjp
def bdot_nt(a, b):
    return _dg(a, b, 1, 1)


def _bdot_nt_fwd(a, b):
    return _dg(a, b, 1, 1), (a, b)


def _bdot_nt_bwd(res, g):
    a, b = res
    return _dg(g, b, 1, 0), _dg(g, a, 0, 0)


bdot_nt.defvjp(_bdot_nt_fwd, _bdot_nt_bwd)


def mm(a, b, mode, name, out_dtype=F32, add=None):
    if mode == 'nn':
        M, K = a.shape
        N = b.shape[1]
    elif mode == 'nt':
        M, K = a.shape
        N = b.shape[0]
    else:
        K, M = a.shape
        N = b.shape[1]
    if mode == 'tn':
        tm, tn, tk = _pick(M, 1536), _pick(N, 1536), _pick(K, 512)
    else:
        tm, tn = _pick(M, 512), _pick(N, 1536)
        tk = K if K <= 2816 else _pick(K, 1536)
    nk = K // tk
    ca, cb = {'nn': (1, 0), 'nt': (1, 1), 'tn': (0, 0)}[mode]

    def body(a_ref, b_ref, *rest):
        add_ref = rest[0] if add is not None else None
        o_ref = rest[1 if add is not None else 0]

        def finish(acc):
            if add_ref is not None:
                acc = acc + add_ref[...]
            o_ref[...] = acc.astype(out_dtype)

        if nk == 1:
            finish(_dg(a_ref[...], b_ref[...], ca, cb))
            return
        acc_ref = rest[-1]
        k = pl.program_id(2)

        @pl.when(k == 0)
        def _():
            acc_ref[...] = jnp.zeros_like(acc_ref)

        acc_ref[...] += _dg(a_ref[...], b_ref[...], ca, cb)

        @pl.when(k == nk - 1)
        def _():
            finish(acc_ref[...])

    if mode == 'nn':
        a_spec = pl.BlockSpec((tm, tk), lambda i, j, k: (i, k))
        b_spec = pl.BlockSpec((tk, tn), lambda i, j, k: (k, j))
    elif mode == 'nt':
        a_spec = pl.BlockSpec((tm, tk), lambda i, j, k: (i, k))
        b_spec = pl.BlockSpec((tn, tk), lambda i, j, k: (j, k))
    else:
        a_spec = pl.BlockSpec((tk, tm), lambda i, j, k: (k, i))
        b_spec = pl.BlockSpec((tk, tn), lambda i, j, k: (k, j))
    out_spec = pl.BlockSpec((tm, tn), lambda i, j, k: (i, j))
    return pl.pallas_call(
        body, name=name, grid=(M // tm, N // tn, nk),
        in_specs=[a_spec, b_spec] + ([out_spec] if add is not None else []),
        out_specs=out_spec,
        out_shape=jax.ShapeDtypeStruct((M, N), out_dtype),
        scratch_shapes=[pltpu.VMEM((tm, tn), F32)] if nk > 1 else [],
        compiler_params=_cparams(("parallel", "parallel", "arbitrary")),
    )(*((a, b) if add is None else (a, b, add)))


def linear(x, w, slot, name, residual=None):
    @jax.custom_vjp
    def op(x, w, slot, residual):
        return mm(x, w, 'nn', name + '_fwd', add=residual)

    def op_fwd(x, w, slot, residual):
        return op(x, w, slot, residual), (x, w)

    def op_bwd(res, g):
        x, w = res
        return (mm(g, w, 'nt', name + '_dx'), jnp.zeros_like(w), mm(x, g, 'tn', name + '_dw', BF16),
                None if residual is None else g)

    op.defvjp(op_fwd, op_bwd)
    return op(x, w, slot, residual)


def rowwise(f, rows, params, consts, out_widths, name, need_row_grad=None, block_rows=None, carry=()):
    nr, npar, nc, nout = len(rows), len(params), len(consts), len(out_widths)
    carry = tuple(carry)
    L = rows[0].shape[0]
    widths = [r.shape[1] for r in rows]
    T = block_rows or _rows_for(L, max(widths + list(out_widths)))
    n = L // T
    need = list(need_row_grad) if need_row_grad is not None else [True] * nr
    pshapes = [p.shape for p in params]
    cshapes = [c.shape for c in consts]

    row_specs = [pl.BlockSpec((T, w), lambda i: (i, 0)) for w in widths]
    par_specs = [pl.BlockSpec(s, lambda i: (0, 0)) for s in pshapes]
    con_specs = [pl.BlockSpec(s, lambda i: (0, 0)) for s in cshapes]
    out_specs = [pl.BlockSpec((T, w), lambda i: (i, 0)) for w in out_widths]

    def fwd_call(rows, params, consts):
        def body(*refs):
            ins = [r[...] for r in refs[:nr + npar + nc]]
            outs = f(*ins)
            for o_ref, val in zip(refs[nr + npar + nc:], outs):
                o_ref[...] = val

        return pl.pallas_call(
            body, name=name + '_fwd', grid=(n,),
            in_specs=row_specs + par_specs + con_specs, out_specs=out_specs,
            out_shape=[jax.ShapeDtypeStruct((L, w), F32) for w in out_widths],
            compiler_params=_cparams(("parallel",)),
        )(*rows, *params, *consts)

    def bwd_call(rows, params, consts, cts, carried):
        grad_rows = [k for k in range(nr) if need[k]]
        n_in = nr + npar + nc + nout

        def body(*refs):
            i = pl.program_id(0)
            rv = [r[...] for r in refs[:nr]]
            pv = [r[...] for r in refs[nr:nr + npar]]
            cv = [r[...] for r in refs[nr + npar:nr + npar + nc]]
            ctv = tuple(r[...] for r in refs[nr + npar + nc:n_in])
            carried_refs = dict(zip(carry, refs[n_in:n_in + len(carry)]))
            orefs = refs[n_in + len(carry):]
            _, vjp = jax.vjp(lambda *rp: tuple(f(*rp, *cv)), *rv, *pv)
            g = vjp(ctv)
            for slot, k in enumerate(grad_rows):
                orefs[slot][...] = g[k] + carried_refs[k][...] if k in carried_refs else g[k]

            @pl.when(i == 0)
            def _():
                for k in range(npar):
                    orefs[len(grad_rows) + k][...] = jnp.zeros(pshapes[k], F32)

            for k in range(npar):
                orefs[len(grad_rows) + k][...] += g[nr + k]

        outs = pl.pallas_call(
            body, name=name + '_bwd', grid=(n,),
            in_specs=row_specs + par_specs + con_specs + out_specs + [row_specs[k] for k in carry],
            out_specs=[row_specs[k] for k in grad_rows] + par_specs,
            out_shape=[jax.ShapeDtypeStruct((L, widths[k]), F32) for k in grad_rows]
            + [jax.ShapeDtypeStruct(s, F32) for s in pshapes],
            compiler_params=_cparams(("arbitrary",)),
        )(*rows, *params, *consts, *cts, *carried)
        drows = []
        slot = 0
        for k in range(nr):
            if need[k]:
                drows.append(outs[slot])
                slot += 1
            else:
                drows.append(jnp.zeros_like(rows[k]))
        return tuple(drows), tuple(outs[len(grad_rows):])

    @jax.custom_vjp
    def op(rows, params, consts):
        return tuple(fwd_call(rows, params, consts)) + tuple(rows[k] for k in carry)

    def op_fwd(rows, params, consts):
        return op(rows, params, consts), (rows, params, consts)

    def op_bwd(res, cts):
        rows, params, consts = res
        drows, dparams = bwd_call(rows, params, consts, cts[:nout], cts[nout:])
        return drows, dparams, tuple(jnp.zeros_like(c) for c in consts)

    op.defvjp(op_fwd, op_bwd)
    return op(tuple(rows), tuple(params), tuple(consts))


def _rms(x, g):
    return x * lax.rsqrt(jnp.mean(x * x, axis=-1, keepdims=True) + EPS) * g


def rmsnorm(x, g, name, carry=False):
    out = rowwise(lambda x, g: (_rms(x, g),), [x], [g.reshape(1, -1)], [], [x.shape[1]], name,
                  carry=(0,) if carry else ())
    return out if carry else out[0]


def _split3(x):
    hi = x.astype(BF16)
    r = x - hi.astype(F32)
    mid = r.astype(BF16)
    return hi, mid, (r - mid.astype(F32)).astype(BF16)


@jax.custom_vjp
def select_mm(x, sel):
    return sum(_dg(t, sel, 1, 0) for t in _split3(x))


def _select_mm_fwd(x, sel):
    return select_mm(x, sel), sel


def _select_mm_bwd(sel, g):
    return sum(_dg(t, sel, 1, 1) for t in _split3(g)), jnp.zeros_like(sel)


select_mm.defvjp(_select_mm_fwd, _select_mm_bwd)


def groupnorm(x, g, group, name):
    width = x.shape[1]
    g_full = jnp.tile(g.reshape(1, group), (1, width // group))
    if group % LANES == 0:
        def f(x, g_full):
            outs = []
            for lo in range(0, width, group):
                xs = x[:, lo:lo + group]
                outs.append(_rms(xs, g_full[:, lo:lo + group]))
            return (jnp.concatenate(outs, axis=-1),)

        return rowwise(f, [x], [g_full], [], [width], name)[0]

    gid = jnp.arange(width) // group
    sel = (gid[:, None] == jnp.arange(LANES)[None, :]).astype(BF16)

    def f(x, g_full, sel, sel_t):
        ms = select_mm(x * x, sel) * (1.0 / group)
        inv = select_mm(lax.rsqrt(ms + EPS), sel_t)
        return (x * inv * g_full,)

    return rowwise(f, [x], [g_full], [sel, sel.T], [width], name)[0]


def glu(x, name):
    half = x.shape[1] // 2

    def f(x):
        return (x[:, :half] * jax.nn.sigmoid(x[:, half:]),)

    return rowwise(f, [x], [], [], [half], name)[0]


def swiglu(x, name):
    half = x.shape[1] // 2

    def f(x):
        gate = x[:, :half]
        return (gate * jax.nn.sigmoid(gate) * x[:, half:],)

    return rowwise(f, [x], [], [], [half], name)[0]


def ln_silu(x, g, b, name):
    def f(x, g, b):
        mu = jnp.mean(x, axis=-1, keepdims=True)
        xc = x - mu
        var = jnp.mean(xc * xc, axis=-1, keepdims=True)
        y = xc * lax.rsqrt(var + EPS) * g + b
        return (y * jax.nn.sigmoid(y),)

    return rowwise(f, [x], [g.reshape(1, -1), b.reshape(1, -1)], [], [x.shape[1]], name)[0]


def branch_norms(o_sb, o_conv, o_ssm, g, name):
    w1, w2 = o_sb.shape[1], o_conv.shape[1]

    def f(a, b, c, g):
        return (jnp.concatenate([_rms(a, g[:, :w1]), _rms(b, g[:, w1:w1 + w2]), _rms(c, g[:, w1 + w2:])],
                                axis=-1),)

    return rowwise(f, [o_sb, o_conv, o_ssm], [g.reshape(1, -1)], [], [g.shape[-1]], name)[0]


def xa_core(q, k, v, name):
    scale = XA_HEAD_DIM ** -0.5

    def f(q, k, v):
        outs = []
        for h in range(XA_HEADS):
            sl = slice(h * XA_HEAD_DIM, (h + 1) * XA_HEAD_DIM)
            s = bdot_nt(q[:, sl], k[:, sl]) * scale
            m = lax.stop_gradient(jnp.max(s, axis=-1, keepdims=True))
            e = jnp.exp(s - m)
            p = e / jnp.sum(e, axis=-1, keepdims=True)
            outs.append(bdot_nn(p, v[:, sl]))
        return (jnp.concatenate(outs, axis=-1),)

    return rowwise(f, [q], [k, v], [], [q.shape[1]], name, block_rows=min(256, q.shape[0]))[0]


def loss_rows(y, target, name):
    def f(y, t):
        d = y - t
        return (0.5 * jnp.mean(d * d, axis=-1, keepdims=True),)

    return rowwise(f, [y, target], [], [], [1], name, need_row_grad=[True, False])[0]


def _hilo(x, ones_bf16):
    hi = x.astype(BF16)
    lo = (x - hi.astype(F32)).astype(BF16)
    return _dg(hi, ones_bf16, 1, 0) + _dg(lo, ones_bf16, 1, 0)


def _sb_block(qh, kb, c, valid, strict_upper):
    z = _dg(qh, kb, 1, 1)
    a = jnp.minimum(z, 0.0) - jnp.log(1.0 + jnp.exp(-jnp.abs(z)))
    b = jnp.where(valid, a - z, 0.0)
    s = _hilo(b, strict_upper) + c
    w = jnp.where(valid, jnp.exp(a + s), 0.0)
    return a, b, w


def _sb_masks(T):
    row = lax.broadcasted_iota(jnp.int32, (T, T), 0)
    col = lax.broadcasted_iota(jnp.int32, (T, T), 1)
    return col < row, (row > col).astype(BF16), (row >= col).astype(BF16)


HEADS_PER_BLOCK = LANES // SB_HEAD_DIM


def _head_mask(h):
    lane = lax.broadcasted_iota(jnp.int32, (1, LANES), 1)
    return (lane // SB_HEAD_DIM == h).astype(F32)


def _max_all(columns):
    m = columns[0]
    for c in columns[1:]:
        m = jnp.maximum(m, c)
    return jnp.max(m)


def _sb_fwd_call(q, k, v, T, name):
    L, W = q.shape
    scale = SB_HEAD_DIM ** -0.5

    def body(q_ref, k_ref, v_ref, o_ref):
        i = pl.program_id(1)
        causal, strict_upper, _ = _sb_masks(T)
        q2 = q_ref[...] * scale
        masks = [_head_mask(h) for h in range(HEADS_PER_BLOCK)]
        qs = [(q2 * hm).astype(BF16) for hm in masks]
        zero = jnp.zeros((T, 1), F32)

        def cond(state):
            j, cs, _ = state
            return jnp.logical_and(j >= 0, _max_all(cs) > -SB_CUT)

        def step(state):
            j, cs, acc = state
            r0 = pl.multiple_of(j * T, T)
            valid = jnp.logical_or(causal, j != i)
            kb = k_ref[pl.ds(r0, T), :].astype(BF16)
            vb = v_ref[pl.ds(r0, T), :]
            new_cs = []
            for hm, qh, c in zip(masks, qs, cs):
                _, b, w = _sb_block(qh, kb, c, valid, strict_upper)
                vh = (vb * hm).astype(BF16)
                w_hi = w.astype(BF16)
                w_lo = (w - w_hi.astype(F32)).astype(BF16)
                acc = acc + _dg(w_hi, vh, 1, 0) + _dg(w_lo, vh, 1, 0)
                new_cs.append(c + jnp.sum(b, axis=1, keepdims=True))
            return j - 1, tuple(new_cs), acc

        _, _, acc = lax.while_loop(cond, step, (i, (zero,) * HEADS_PER_BLOCK, jnp.zeros((T, LANES), F32)))
        o_ref[...] = acc

    return pl.pallas_call(
        body, name=name, grid=(W // LANES, L // T),
        in_specs=[pl.BlockSpec((T, LANES), lambda p, i: (i, p)),
                  pl.BlockSpec((L, LANES), lambda p, i: (0, p)),
                  pl.BlockSpec((L, LANES), lambda p, i: (0, p))],
        out_specs=pl.BlockSpec((T, LANES), lambda p, i: (i, p)),
        out_shape=jax.ShapeDtypeStruct((L, W), F32),
        compiler_params=_cparams(("parallel", "parallel")),
    )(q, k, v)


def _sb_bwd_call(q, k, v, o, do, T, name):
    L, W = q.shape
    scale = SB_HEAD_DIM ** -0.5

    def body(q_ref, k_ref, v_ref, o_ref, do_ref, dq_ref, dk_ref, dv_ref):
        i = pl.program_id(1)

        @pl.when(i == 0)
        def _():
            dk_ref[...] = jnp.zeros_like(dk_ref)
            dv_ref[...] = jnp.zeros_like(dv_ref)

        causal, strict_upper, upper = _sb_masks(T)
        q2 = q_ref[...] * scale
        do2 = do_ref[...]
        o2 = o_ref[...]
        masks = [_head_mask(h) for h in range(HEADS_PER_BLOCK)]
        qs = [(q2 * hm).astype(BF16) for hm in masks]
        dos = [(do2 * hm).astype(BF16) for hm in masks]
        totals = [jnp.sum(doh.astype(F32) * o2, axis=1, keepdims=True) for doh in dos]
        zero = jnp.zeros((T, 1), F32)

        def cond(state):
            j, cs, _, _ = state
            return jnp.logical_and(j >= 0, _max_all(cs) > -SB_CUT)

        def step(state):
            j, cs, rs, dq = state
            r0 = pl.multiple_of(j * T, T)
            valid = jnp.logical_or(causal, j != i)
            kf = k_ref[pl.ds(r0, T), :]
            kb = kf.astype(BF16)
            vb = v_ref[pl.ds(r0, T), :].astype(BF16)
            new_cs, new_rs = [], []
            dk = jnp.zeros((T, LANES), F32)
            dv = jnp.zeros((T, LANES), F32)
            for hm, qh, doh, total, c, r in zip(masks, qs, dos, totals, cs, rs):
                a, b, w = _sb_block(qh, kb, c, valid, strict_upper)
                e = _dg(doh, vb, 1, 1) * w
                before = total - (_hilo(e, upper) + r)
                dz = jnp.where(valid, e * jnp.exp(b) - before * jnp.exp(a), 0.0).astype(BF16)
                dq = dq + _dg(dz, kf * hm, 1, 0)
                dk = dk + _dg(dz, qh, 0, 0)
                dv = dv + _dg(w, doh, 0, 0)
                new_cs.append(c + jnp.sum(b, axis=1, keepdims=True))
                new_rs.append(r + jnp.sum(e, axis=1, keepdims=True))
            dk_ref[pl.ds(r0, T), :] += dk
            dv_ref[pl.ds(r0, T), :] += dv
            return j - 1, tuple(new_cs), tuple(new_rs), dq

        init = (i, (zero,) * HEADS_PER_BLOCK, (zero,) * HEADS_PER_BLOCK, jnp.zeros((T, LANES), F32))
        dq = lax.while_loop(cond, step, init)[3]
        dq_ref[...] = dq * scale

    blk = pl.BlockSpec((T, LANES), lambda p, i: (i, p))
    full = pl.BlockSpec((L, LANES), lambda p, i: (0, p))
    return pl.pallas_call(
        body, name=name, grid=(W // LANES, L // T),
        in_specs=[blk, full, full, blk, blk],
        out_specs=[blk, full, full],
        out_shape=[jax.ShapeDtypeStruct((L, W), F32)] * 3,
        compiler_params=_cparams(("parallel", "arbitrary")),
    )(q, k, v, o, do)


def sb_attention(q, k, v, name):
    T = min(256, q.shape[0])

    @jax.custom_vjp
    def op(q, k, v):
        return _sb_fwd_call(q, k, v, T, name + '_fwd')

    def op_fwd(q, k, v):
        o = _sb_fwd_call(q, k, v, T, name + '_fwd')
        return o, (q, k, v, o)

    def op_bwd(res, do):
        q, k, v, o = res
        return tuple(_sb_bwd_call(q, k, v, o, do, T, name + '_bwd'))

    op.defvjp(op_fwd, op_bwd)
    return op(q, k, v)


def _dwconv_fwd_call(x, w, b, T, name):
    L, C = x.shape
    per = T // CONV_HALO
    lead = CONV_HALO - (CONV_WIDTH - 1)

    def body(x_ref, halo_ref, w_ref, b_ref, o_ref, buf):
        i = pl.program_id(0)
        buf[0:CONV_HALO, :] = jnp.where(i > 0, halo_ref[...], 0.0)
        buf[CONV_HALO:CONV_HALO + T, :] = x_ref[...]
        acc = jnp.zeros((T, C), F32) + b_ref[...]
        for j in range(CONV_WIDTH):
            acc = acc + w_ref[j:j + 1, :] * buf[lead + j:lead + j + T, :]
        o_ref[...] = acc

    return pl.pallas_call(
        body, name=name, grid=(L // T,),
        in_specs=[pl.BlockSpec((T, C), lambda i: (i, 0)),
                  pl.BlockSpec((CONV_HALO, C), lambda i: (jnp.maximum(i * per - 1, 0), 0)),
                  pl.BlockSpec(w.shape, lambda i: (0, 0)),
                  pl.BlockSpec(b.shape, lambda i: (0, 0))],
        out_specs=pl.BlockSpec((T, C), lambda i: (i, 0)),
        out_shape=jax.ShapeDtypeStruct((L, C), F32),
        scratch_shapes=[pltpu.VMEM((T + CONV_HALO, C), F32)],
        compiler_params=_cparams(("parallel",)),
    )(x, x, w, b)


def _dwconv_bwd_call(x, w, g, T, name):
    L, C = x.shape
    per = T // CONV_HALO
    n = L // T
    last_halo = L // CONV_HALO - 1
    lead = CONV_HALO - (CONV_WIDTH - 1)

    def body(x_ref, xh_ref, g_ref, gh_ref, w_ref, dx_ref, dw_ref, db_ref, bufx, bufg):
        i = pl.program_id(0)
        bufx[0:CONV_HALO, :] = jnp.where(i > 0, xh_ref[...], 0.0)
        bufx[CONV_HALO:CONV_HALO + T, :] = x_ref[...]
        gm = g_ref[...]
        bufg[0:T, :] = gm
        bufg[T:T + CONV_HALO, :] = jnp.where(i < n - 1, gh_ref[...], 0.0)
        acc = jnp.zeros((T, C), F32)
        for j in range(CONV_WIDTH):
            off = CONV_WIDTH - 1 - j
            acc = acc + w_ref[j:j + 1, :] * bufg[off:off + T, :]
        dx_ref[...] = acc

        @pl.when(i == 0)
        def _():
            dw_ref[...] = jnp.zeros_like(dw_ref)
            db_ref[...] = jnp.zeros_like(db_ref)

        for j in range(CONV_WIDTH):
            dw_ref[j:j + 1, :] += jnp.sum(gm * bufx[lead + j:lead + j + T, :], axis=0, keepdims=True)
        db_ref[...] += jnp.sum(gm, axis=0, keepdims=True)

    return pl.pallas_call(
        body, name=name, grid=(n,),
        in_specs=[pl.BlockSpec((T, C), lambda i: (i, 0)),
                  pl.BlockSpec((CONV_HALO, C), lambda i: (jnp.maximum(i * per - 1, 0), 0)),
                  pl.BlockSpec((T, C), lambda i: (i, 0)),
                  pl.BlockSpec((CONV_HALO, C), lambda i: (jnp.minimum((i + 1) * per, last_halo), 0)),
                  pl.BlockSpec(w.shape, lambda i: (0, 0))],
        out_specs=[pl.BlockSpec((T, C), lambda i: (i, 0)),
                   pl.BlockSpec(w.shape, lambda i: (0, 0)),
                   pl.BlockSpec((1, C), lambda i: (0, 0))],
        out_shape=[jax.ShapeDtypeStruct((L, C), F32), jax.ShapeDtypeStruct(w.shape, F32),
                   jax.ShapeDtypeStruct((1, C), F32)],
        scratch_shapes=[pltpu.VMEM((T + CONV_HALO, C), F32), pltpu.VMEM((T + CONV_HALO, C), F32)],
        compiler_params=_cparams(("arbitrary",)),
    )(x, x, g, g, w)


def dwconv(x, w, b, name):
    T = min(512, x.shape[0])

    @jax.custom_vjp
    def op(x, w, b):
        return _dwconv_fwd_call(x, w, b, T, name + '_fwd')

    def op_fwd(x, w, b):
        return _dwconv_fwd_call(x, w, b, T, name + '_fwd'), (x, w)

    def op_bwd(res, g):
        x, w = res
        return tuple(_dwconv_bwd_call(x, w, g, T, name + '_bwd'))

    op.defvjp(op_fwd, op_bwd)
    return op(x, w, b)


def _ssm_fwd_call(u, ar, ai, bbr, bbi, cr, ci, d, T, name):
    L, C = u.shape
    S = SSM_LANES

    def body(u_ref, ar_ref, ai_ref, bbr_ref, bbi_ref, cr_ref, ci_ref, d_ref,
             y_ref, xr_ref, xi_ref, st_r, st_i, in_r, in_i, out_r, out_i):
        i = pl.program_id(0)

        @pl.when(i == 0)
        def _():
            st_r[...] = jnp.zeros_like(st_r)
            st_i[...] = jnp.zeros_like(st_i)

        u_blk = u_ref[...]
        xr_ref[...] = _dg(u_blk, bbr_ref[...], 1, 0)
        xi_ref[...] = _dg(u_blk, bbi_ref[...], 1, 0)
        a_r, a_i = ar_ref[...], ai_ref[...]

        def tile(t, carry):
            sr, si = carry
            r0 = pl.multiple_of(t * SUBLANES, SUBLANES)
            in_r[...] = xr_ref[pl.ds(r0, SUBLANES), :]
            in_i[...] = xi_ref[pl.ds(r0, SUBLANES), :]
            for r in range(SUBLANES):
                nr = a_r * sr - a_i * si + in_r[r:r + 1, :]
                ni = a_r * si + a_i * sr + in_i[r:r + 1, :]
                sr, si = nr, ni
                out_r[r:r + 1, :] = sr
                out_i[r:r + 1, :] = si
            xr_ref[pl.ds(r0, SUBLANES), :] = out_r[...]
            xi_ref[pl.ds(r0, SUBLANES), :] = out_i[...]
            return sr, si

        sr, si = lax.fori_loop(0, T // SUBLANES, tile, (st_r[0:1, :], st_i[0:1, :]))
        st_r[0:1, :] = sr
        st_i[0:1, :] = si
        y_ref[...] = (_dg(xr_ref[...], cr_ref[...], 1, 0) - _dg(xi_ref[...], ci_ref[...], 1, 0)
                      + d_ref[...] * u_blk)

    full = lambda a: pl.BlockSpec(a.shape, lambda i: (0, 0))
    return pl.pallas_call(
        body, name=name, grid=(L // T,),
        in_specs=[pl.BlockSpec((T, C), lambda i: (i, 0))] + [full(a) for a in (ar, ai, bbr, bbi, cr, ci, d)],
        out_specs=[pl.BlockSpec((T, C), lambda i: (i, 0)), pl.BlockSpec((T, S), lambda i: (i, 0)),
                   pl.BlockSpec((T, S), lambda i: (i, 0))],
        out_shape=[jax.ShapeDtypeStruct((L, C), F32), jax.ShapeDtypeStruct((L, S), F32),
                   jax.ShapeDtypeStruct((L, S), F32)],
        scratch_shapes=[pltpu.VMEM((SUBLANES, S), F32)] * 6,
        compiler_params=_cparams(("arbitrary",)),
    )(u, ar, ai, bbr, bbi, cr, ci, d)


def _ssm_bwd_call(u, xr, xi, dy, ar, ai, bbr, bbi, cr, ci, d, T, name):
    L, C = u.shape
    S = SSM_LANES
    n = L // T
    per = T // SUBLANES

    def body(u_ref, xr_ref, xi_ref, hr_ref, hi_ref, dy_ref, ar_ref, ai_ref, bbr_ref, bbi_ref, cr_ref, ci_ref,
             d_ref, du_ref, dar_ref, dai_ref, dbr_ref, dbi_ref, dcr_ref, dci_ref, dd_ref,
             lam_r, lam_i, prev_r, prev_i, st_r, st_i, in_r, in_i, out_r, out_i):
        i = pl.program_id(0)
        chunk = n - 1 - i

        @pl.when(i == 0)
        def _():
            st_r[...] = jnp.zeros_like(st_r)
            st_i[...] = jnp.zeros_like(st_i)
            for ref in (dar_ref, dai_ref, dbr_ref, dbi_ref, dcr_ref, dci_ref, dd_ref):
                ref[...] = jnp.zeros_like(ref)

        dy_blk = dy_ref[...]
        u_blk = u_ref[...]
        lam_r[...] = _dg(dy_blk, cr_ref[...], 1, 1)
        lam_i[...] = -_dg(dy_blk, ci_ref[...], 1, 1)
        dcr_ref[...] += _dg(xr_ref[...], dy_blk, 0, 0)
        dci_ref[...] -= _dg(xi_ref[...], dy_blk, 0, 0)
        a_r, a_i = ar_ref[...], ai_ref[...]

        def tile(k, carry):
            lr, li = carry
            r0 = pl.multiple_of((per - 1 - k) * SUBLANES, SUBLANES)
            in_r[...] = lam_r[pl.ds(r0, SUBLANES), :]
            in_i[...] = lam_i[pl.ds(r0, SUBLANES), :]
            for r in range(SUBLANES - 1, -1, -1):
                nr = in_r[r:r + 1, :] + a_r * lr + a_i * li
                ni = in_i[r:r + 1, :] + a_r * li - a_i * lr
                lr, li = nr, ni
                out_r[r:r + 1, :] = lr
                out_i[r:r + 1, :] = li
            lam_r[pl.ds(r0, SUBLANES), :] = out_r[...]
            lam_i[pl.ds(r0, SUBLANES), :] = out_i[...]
            return lr, li

        lr, li = lax.fori_loop(0, per, tile, (st_r[0:1, :], st_i[0:1, :]))
        st_r[0:1, :] = lr
        st_i[0:1, :] = li

        l_r, l_i = lam_r[...], lam_i[...]
        du_ref[...] = _dg(l_r, bbr_ref[...], 1, 1) + _dg(l_i, bbi_ref[...], 1, 1) + d_ref[...] * dy_blk
        dbr_ref[...] += _dg(u_blk, l_r, 0, 0)
        dbi_ref[...] += _dg(u_blk, l_i, 0, 0)
        dd_ref[...] += jnp.sum(dy_blk * u_blk, axis=0, keepdims=True)

        prev_r[0:SUBLANES, :] = jnp.where(chunk > 0, hr_ref[...], 0.0)
        prev_i[0:SUBLANES, :] = jnp.where(chunk > 0, hi_ref[...], 0.0)
        prev_r[SUBLANES:SUBLANES + T, :] = xr_ref[...]
        prev_i[SUBLANES:SUBLANES + T, :] = xi_ref[...]
        p_r = prev_r[SUBLANES - 1:SUBLANES - 1 + T, :]
        p_i = prev_i[SUBLANES - 1:SUBLANES - 1 + T, :]
        dar_ref[...] += jnp.sum(l_r * p_r + l_i * p_i, axis=0, keepdims=True)
        dai_ref[...] += jnp.sum(l_i * p_r - l_r * p_i, axis=0, keepdims=True)

    rev = lambda w: pl.BlockSpec((T, w), lambda i: (n - 1 - i, 0))
    halo = pl.BlockSpec((SUBLANES, S), lambda i: (jnp.maximum((n - 1 - i) * per - 1, 0), 0))
    full = lambda a: pl.BlockSpec(a.shape, lambda i: (0, 0))
    params = (ar, ai, bbr, bbi, cr, ci, d)
    return pl.pallas_call(
        body, name=name, grid=(n,),
        in_specs=[rev(C), rev(S), rev(S), halo, halo, rev(C)] + [full(a) for a in params],
        out_specs=[rev(C)] + [full(a) for a in params],
        out_shape=[jax.ShapeDtypeStruct((L, C), F32)] + [jax.ShapeDtypeStruct(a.shape, F32) for a in params],
        scratch_shapes=[pltpu.VMEM((T, S), F32), pltpu.VMEM((T, S), F32),
                        pltpu.VMEM((T + SUBLANES, S), F32), pltpu.VMEM((T + SUBLANES, S), F32)]
        + [pltpu.VMEM((SUBLANES, S), F32)] * 6,
        compiler_params=_cparams(("arbitrary",)),
    )(u, xr, xi, xr, xi, dy, *params)


def ssm_core(u, ar, ai, bbr, bbi, cr, ci, d, name):
    T = min(256, u.shape[0])

    @jax.custom_vjp
    def op(u, ar, ai, bbr, bbi, cr, ci, d):
        return _ssm_fwd_call(u, ar, ai, bbr, bbi, cr, ci, d, T, name + '_fwd')[0]

    def op_fwd(u, ar, ai, bbr, bbi, cr, ci, d):
        y, xr, xi = _ssm_fwd_call(u, ar, ai, bbr, bbi, cr, ci, d, T, name + '_fwd')
        return y, (u, xr, xi, ar, ai, bbr, bbi, cr, ci, d)

    def op_bwd(res, dy):
        u, xr, xi, ar, ai, bbr, bbi, cr, ci, d = res
        return tuple(_ssm_bwd_call(u, xr, xi, dy, ar, ai, bbr, bbi, cr, ci, d, T, name + '_bwd'))

    op.defvjp(op_fwd, op_bwd)
    return op(u, ar, ai, bbr, bbi, cr, ci, d)


@jax.custom_vjp
def _block_diag(blocks):
    G, R, Cc = blocks.shape
    eye = jnp.eye(G, dtype=blocks.dtype)
    return (blocks[:, :, None, :] * eye[:, None, :, None]).reshape(G * R, G * Cc)


def _block_diag_fwd(blocks):
    return _block_diag(blocks), blocks.shape


def _block_diag_bwd(shape, g):
    G, R, Cc = shape
    return (jnp.stack([g[k * R:(k + 1) * R, k * Cc:(k + 1) * Cc] for k in range(G)]),)


_block_diag.defvjp(_block_diag_fwd, _block_diag_bwd)


def ssm_discretise(lam_re, lam_im, log_dt, b_re, b_im, c_re, c_im):
    dt = jnp.exp(log_dt)[:, None]
    mag = jnp.exp(lam_re * dt)
    ar, ai = mag * jnp.cos(lam_im * dt), mag * jnp.sin(lam_im * dt)
    den = lam_re * lam_re + lam_im * lam_im
    fr = ((ar - 1.0) * lam_re + ai * lam_im) / den
    fi = (ai * lam_re - (ar - 1.0) * lam_im) / den
    bbr = fr[..., None] * b_re - fi[..., None] * b_im
    bbi = fr[..., None] * b_im + fi[..., None] * b_re
    return (ar.reshape(1, SSM_LANES), ai.reshape(1, SSM_LANES),
            _block_diag(bbr.transpose(0, 2, 1)), _block_diag(bbi.transpose(0, 2, 1)),
            _block_diag(c_re.transpose(0, 2, 1)), _block_diag(c_im.transpose(0, 2, 1)))


def split_columns(p, bounds):
    @jax.custom_vjp
    def op(p):
        return tuple(p[:, lo:hi] for lo, hi in zip(bounds[:-1], bounds[1:]))

    def op_fwd(p):
        return op(p), None

    def op_bwd(_, gs):
        return (jnp.concatenate(gs, axis=1),)

    op.defvjp(op_fwd, op_bwd)
    return op(p)


def local_loss(slots, w, mats, x, mem, target):
    depth = len(mats['w_in'])
    s1, s2, s3 = SB_WIDTH, 2 * SB_WIDTH, 3 * SB_WIDTH
    s4 = s3 + 2 * CONV_CH

    def linear_(x, n, l, name, residual=None):
        return linear(x, mats[n][l], slots[n][l], name, residual)

    for l in range(depth):
        tag = 'l%d_' % l
        h, x = rmsnorm(x, w['norm_mix_g'][l], tag + 'norm_mix', carry=True)
        p = linear_(h, 'w_in', l, tag + 'w_in')
        q, k, v, u_conv, u_ssm = split_columns(p, (0, s1, s2, s3, s4, p.shape[1]))
        q = groupnorm(q, w['sb_q_norm_g'][l], SB_HEAD_DIM, tag + 'q_norm')
        k = groupnorm(k, w['sb_k_norm_g'][l], SB_HEAD_DIM, tag + 'k_norm')
        o_sb = sb_attention(q, k, v, tag + 'sb')

        dw_w = jnp.pad(w['conv_dw_w'][l], ((0, CONV_HALO - CONV_WIDTH), (0, 0)))
        hc = dwconv(glu(u_conv, tag + 'conv_glu'), dw_w, w['conv_dw_b'][l].reshape(1, -1), tag + 'dwconv')
        hc = ln_silu(hc, w['conv_ln_g'][l], w['conv_ln_b'][l], tag + 'conv_ln')
        o_conv = linear_(hc, 'conv_pw2_w', l, tag + 'pw2')

        ar, ai, bbr, bbi, cr, ci = ssm_discretise(
            w['ssm_lam_re'][l], w['ssm_lam_im'][l], w['ssm_log_dt'][l], w['ssm_b_re'][l], w['ssm_b_im'][l],
            w['ssm_c_re'][l], w['ssm_c_im'][l])
        y = ssm_core(u_ssm, ar, ai, bbr, bbi, cr, ci, w['ssm_d'][l].reshape(1, -1), tag + 'ssm')
        o_ssm = glu(linear_(y, 'ssm_glu_w', l, tag + 'ssm_glu_w'), tag + 'ssm_glu')

        mixed = branch_norms(o_sb, o_conv, o_ssm, w['branch_norm_g'][l], tag + 'branch_norm')
        x = linear_(mixed, 'w_out', l, tag + 'w_out', residual=x)

        hx, x = rmsnorm(x, w['norm_xa_g'][l], tag + 'norm_xa', carry=True)
        hm = rmsnorm(mem, w['norm_mem_g'][l], tag + 'norm_mem')
        qx = groupnorm(linear_(hx, 'xa_wq', l, tag + 'xa_wq'), w['xa_q_norm_g'][l], XA_HEAD_DIM, tag + 'xa_qn')
        kx = groupnorm(linear_(hm, 'xa_wk', l, tag + 'xa_wk'), w['xa_k_norm_g'][l], XA_HEAD_DIM, tag + 'xa_kn')
        vx = linear_(hm, 'xa_wv', l, tag + 'xa_wv')
        x = linear_(xa_core(qx, kx, vx, tag + 'xa_core'), 'xa_wo', l, tag + 'xa_wo', residual=x)

        hf, x = rmsnorm(x, w['norm_ffn_g'][l], tag + 'norm_ffn', carry=True)
        act = swiglu(linear_(hf, 'ffn_w_in', l, tag + 'ffn_in'), tag + 'swiglu')
        x = linear_(act, 'ffn_w_out', l, tag + 'ffn_out', residual=x)
    return jnp.sum(loss_rows(x, target, 'loss'))


def local_step(w, mats, x, mem, target):
    slots = {n: [jnp.zeros(m.shape, BF16) for m in ms] for n, ms in mats.items()}
    loss, (g_mats, g_w, gx) = jax.value_and_grad(local_loss, argnums=(0, 1, 3))(slots, w, mats, x, mem, target)
    return loss, gx, g_w, g_mats


PACK_ROWS = 2048


PIECE_ROWS = 16


def _piece_rows(size):
    rows = -(-size // LANES)
    return rows, -(-rows // PIECE_ROWS) * PIECE_ROWS


def pack(arrays, dtype):
    parts, total = [], 0
    for a in arrays:
        rows, padded = _piece_rows(a.size)
        a = a.astype(dtype)
        if a.size % LANES:
            a = jnp.pad(a.reshape(-1), (0, rows * LANES - a.size))
        a = a.reshape(rows, LANES)
        if padded != rows:
            a = jnp.pad(a, ((0, padded - rows), (0, 0)))
        parts.append(a)
        total += padded
    tail = -total % PACK_ROWS
    if tail:
        parts.append(jnp.zeros((tail, LANES), dtype))
    return jnp.concatenate(parts, axis=0)


def unpack(packed, shapes):
    out, off = [], 0
    for s in shapes:
        size = math.prod(s)
        rows, padded = _piece_rows(size)
        piece = packed[off:off + rows]
        if size % LANES:
            piece = piece.reshape(-1)[:size]
        out.append(piece.reshape(s))
        off += padded
    return out


def _mesh_pos():
    return lax.axis_index("x"), lax.axis_index("y"), lax.axis_index("c")


def _exchange_xy(n_arrays, src_of, dst_of, sems):
    send_sems, recv_sems, local_sems = sems
    x, y, c = _mesh_pos()
    me = 2 * x + y
    peers = [(1 - x, y), (x, 1 - y), (1 - x, 1 - y)]
    started = []
    for k in range(n_arrays):
        own = pltpu.make_async_copy(src_of(k, me), dst_of(k, me), local_sems.at[k])
        own.start()
        started.append(own)
    sends = []
    for k in range(n_arrays):
        for p, (px, py) in enumerate(peers):
            cp = pltpu.make_async_remote_copy(
                src_ref=src_of(k, 2 * px + py), dst_ref=dst_of(k, me), send_sem=send_sems.at[3 * k + p],
                recv_sem=recv_sems.at[3 * k + p], device_id=(px, py, c), device_id_type=MESH)
            cp.start()
            sends.append(cp)
    for k in range(n_arrays):
        for p, (px, py) in enumerate(peers):
            pltpu.make_async_remote_copy(
                src_ref=src_of(k, me), dst_ref=dst_of(k, 2 * px + py), send_sem=send_sems.at[3 * k + p],
                recv_sem=recv_sems.at[3 * k + p], device_id=(px, py, c), device_id_type=MESH).wait_recv()
    for cp in sends:
        cp.wait_send()
    for own in started:
        own.wait()


def _exchange_call(body, arrays, out_shapes, n_copies, name):
    return pl.pallas_call(
        body, name=name,
        in_specs=[pl.BlockSpec(memory_space=pl.ANY)] * len(arrays),
        out_specs=[pl.BlockSpec(memory_space=pl.ANY)] * len(out_shapes),
        out_shape=[jax.ShapeDtypeStruct(s, BF16) for s in out_shapes],
        scratch_shapes=[pltpu.SemaphoreType.DMA((3 * n_copies,)), pltpu.SemaphoreType.DMA((3 * n_copies,)),
                        pltpu.SemaphoreType.DMA((n_copies,))],
        compiler_params=pltpu.CompilerParams(has_side_effects=True),
    )(*arrays)


SLOTTED = ('w_in', 'taps')


def _part(ref, axis, chip, size):
    start = pl.multiple_of(chip * size, size)
    index = [slice(None)] * len(ref.shape)
    index[axis] = pl.ds(start, size)
    return ref.at[tuple(index)]


def gather_weights(names, shards, name):
    na = len(shards)
    axes = [SHARD_AXIS.get(n, 0) for n in names]

    def out_shape(k):
        if names[k] in SLOTTED:
            return (N_CHIPS,) + shards[k].shape
        s = list(shards[k].shape)
        s[axes[k]] *= N_CHIPS
        return tuple(s)

    def body(*refs):
        ins, outs = refs[:na], refs[na:2 * na]

        def dst_of(k, chip):
            if names[k] in SLOTTED:
                return outs[k].at[chip]
            return _part(outs[k], axes[k], chip, shards[k].shape[axes[k]])

        _exchange_xy(na, lambda k, chip: ins[k], dst_of, refs[2 * na:])

    return _exchange_call(body, shards, [out_shape(k) for k in range(na)], na, name)


def scatter_grads(names, grads, depth, name):
    na = len(grads)

    def shard_shape(k):
        g = grads[k * depth]
        if names[k] in SLOTTED:
            return g.shape[1:]
        s = list(g.shape)
        s[SHARD_AXIS[names[k]] - 1] //= N_CHIPS
        return tuple(s)

    def body(*refs):
        ins, outs = refs[:na], refs[na:na + len(names)]

        def src_of(i, chip):
            k = i // depth
            if names[k] in SLOTTED:
                return ins[i].at[chip]
            axis = SHARD_AXIS[names[k]] - 1
            return _part(ins[i], axis, chip, shard_shape(k)[axis])

        _exchange_xy(na, src_of, lambda i, chip: outs[i // depth].at[chip, i % depth], refs[na + len(names):])

    return _exchange_call(body, grads, [(N_CHIPS, depth) + shard_shape(k) for k in range(len(names))], na, name)


def swap_cores(arrays, name):
    na = len(arrays)

    def body(*refs):
        ins, outs, send_sems, recv_sems = refs[:na], refs[na:2 * na], refs[2 * na], refs[2 * na + 1]
        x, y, c = _mesh_pos()
        copies = [pltpu.make_async_remote_copy(
            src_ref=ins[k], dst_ref=outs[k], send_sem=send_sems.at[k], recv_sem=recv_sems.at[k],
            device_id=(x, y, 1 - c), device_id_type=MESH) for k in range(na)]
        for cp in copies:
            cp.start()
        for cp in copies:
            cp.wait()

    return pl.pallas_call(
        body, name=name,
        in_specs=[pl.BlockSpec(memory_space=pl.ANY)] * na,
        out_specs=[pl.BlockSpec(memory_space=pl.ANY)] * na,
        out_shape=[jax.ShapeDtypeStruct(a.shape, a.dtype) for a in arrays],
        scratch_shapes=[pltpu.SemaphoreType.DMA((na,)), pltpu.SemaphoreType.DMA((na,))],
        compiler_params=pltpu.CompilerParams(has_side_effects=True),
    )(*arrays)


def allreduce_small(buf, name):
    R = buf.shape[0]

    def body(in_ref, sum_ref, all_ref, send_sems, recv_sems):
        x, y, c = _mesh_pos()
        me = 4 * x + 2 * y + c
        all_ref[me] = in_ref[...]
        flips = [(fx, fy, fc) for fx in (0, 1) for fy in (0, 1) for fc in (0, 1)][1:]
        sends = []
        for k, (fx, fy, fc) in enumerate(flips):
            cp = pltpu.make_async_remote_copy(
                src_ref=in_ref, dst_ref=all_ref.at[me], send_sem=send_sems.at[k], recv_sem=recv_sems.at[k],
                device_id=(x ^ fx, y ^ fy, c ^ fc), device_id_type=MESH)
            cp.start()
            sends.append(cp)
        for k, (fx, fy, fc) in enumerate(flips):
            peer = 4 * (x ^ fx) + 2 * (y ^ fy) + (c ^ fc)
            pltpu.make_async_remote_copy(
                src_ref=in_ref, dst_ref=all_ref.at[peer], send_sem=send_sems.at[k], recv_sem=recv_sems.at[k],
                device_id=(x ^ fx, y ^ fy, c ^ fc), device_id_type=MESH).wait_recv()
        for cp in sends:
            cp.wait_send()
        acc = all_ref[0]
        for k in range(1, N_DEV):
            acc = acc + all_ref[k]
        sum_ref[...] = acc

    return pl.pallas_call(
        body, name=name,
        in_specs=[pl.BlockSpec(memory_space=pltpu.VMEM)],
        out_specs=[pl.BlockSpec(memory_space=pltpu.VMEM), pl.BlockSpec(memory_space=pltpu.VMEM)],
        out_shape=[jax.ShapeDtypeStruct((R, LANES), F32), jax.ShapeDtypeStruct((N_DEV, R, LANES), F32)],
        scratch_shapes=[pltpu.SemaphoreType.DMA((N_DEV - 1,)), pltpu.SemaphoreType.DMA((N_DEV - 1,))],
        compiler_params=pltpu.CompilerParams(has_side_effects=True, vmem_limit_bytes=VMEM_LIMIT),
    )(buf)[0]


def _adamw_update(g, w, m, v):
    m2 = ADAM_B1 * m + (1.0 - ADAM_B1) * g
    v2 = ADAM_B2 * v + (1.0 - ADAM_B2) * (g * g)
    m_hat = m2 / (1.0 - ADAM_B1 ** ADAM_STEP)
    v_hat = v2 / (1.0 - ADAM_B2 ** ADAM_STEP)
    return -ADAM_LR * (m_hat / (jnp.sqrt(v_hat) + ADAM_EPS) + ADAM_WD * w), m2, v2


def adamw_matrix(mine, other, w, m, v, name):
    shape = w.shape
    rows, cols = shape[0] * shape[1], shape[2]
    T = 16
    while rows % (2 * T) == 0 and 2 * T * cols <= 128 * 1024:
        T *= 2

    def body(mine_ref, other_ref, w_ref, m_ref, v_ref, g_out, d_out, m_out, v_out):
        def total(ref):
            acc = ref[0].astype(F32)
            for k in range(1, N_CHIPS):
                acc = acc + ref[k].astype(F32)
            return acc

        g = total(mine_ref) + total(other_ref)
        g_out[...] = g
        d_out[...], m_out[...], v_out[...] = _adamw_update(g, w_ref[...], m_ref[...], v_ref[...])

    slots = pl.BlockSpec((N_CHIPS, T, cols), lambda i: (0, i, 0))
    spec = pl.BlockSpec((T, cols), lambda i: (i, 0))
    outs = pl.pallas_call(
        body, name=name, grid=(rows // T,),
        in_specs=[slots, slots, spec, spec, spec], out_specs=[spec] * 4,
        out_shape=[jax.ShapeDtypeStruct((rows, cols), F32)] * 4,
        compiler_params=_cparams(("parallel",)),
    )(mine.reshape(N_CHIPS, rows, cols), other.reshape(N_CHIPS, rows, cols),
      w.reshape(rows, cols), m.reshape(rows, cols), v.reshape(rows, cols))
    return [o.reshape(shape) for o in outs]


def adamw_small(gs, ws, ms, vs, name):
    n = len(gs)

    def body(*refs):
        for k in range(n):
            g, w, m, v = (refs[j * n + k][...] for j in range(4))
            d_out, m_out, v_out = (refs[(4 + j) * n + k] for j in range(3))
            d_out[...], m_out[...], v_out[...] = _adamw_update(g, w, m, v)

    vmem = pl.BlockSpec(memory_space=pltpu.VMEM)
    outs = pl.pallas_call(
        body, name=name,
        in_specs=[vmem] * (4 * n), out_specs=[vmem] * (3 * n),
        out_shape=[jax.ShapeDtypeStruct(w.shape, F32) for w in ws] * 3,
        compiler_params=_cparams(),
    )(*gs, *ws, *ms, *vs)
    return outs[:n], outs[n:2 * n], outs[2 * n:]


def _shard_of(full, axis, chip):
    size = full.shape[axis] // N_CHIPS
    return lax.slice_in_dim(full, chip * size, (chip + 1) * size, axis=axis)


def kernel(x, mem, norm_mix_g, w_in, sb_q_norm_g, sb_k_norm_g, conv_dw_w, conv_dw_b, conv_ln_g, conv_ln_b, conv_pw2_w, ssm_lam_re, ssm_lam_im, ssm_log_dt, ssm_b_re, ssm_b_im, ssm_c_re, ssm_c_im, ssm_d, ssm_glu_w, branch_norm_g, w_out, norm_xa_g, norm_mem_g, xa_wq, xa_wk, xa_wv, xa_q_norm_g, xa_k_norm_g, xa_wo, norm_ffn_g, ffn_w_in, ffn_w_out, loss_target, m_norm_mix_g, m_w_in, m_sb_q_norm_g, m_sb_k_norm_g, m_conv_dw_w, m_conv_dw_b, m_conv_ln_g, m_conv_ln_b, m_conv_pw2_w, m_ssm_lam_re, m_ssm_lam_im, m_ssm_log_dt, m_ssm_b_re, m_ssm_b_im, m_ssm_c_re, m_ssm_c_im, m_ssm_d, m_ssm_glu_w, m_branch_norm_g, m_w_out, m_norm_xa_g, m_norm_mem_g, m_xa_wq, m_xa_wk, m_xa_wv, m_xa_q_norm_g, m_xa_k_norm_g, m_xa_wo, m_norm_ffn_g, m_ffn_w_in, m_ffn_w_out, v_norm_mix_g, v_w_in, v_sb_q_norm_g, v_sb_k_norm_g, v_conv_dw_w, v_conv_dw_b, v_conv_ln_g, v_conv_ln_b, v_conv_pw2_w, v_ssm_lam_re, v_ssm_lam_im, v_ssm_log_dt, v_ssm_b_re, v_ssm_b_im, v_ssm_c_re, v_ssm_c_im, v_ssm_d, v_ssm_glu_w, v_branch_norm_g, v_w_out, v_norm_xa_g, v_norm_mem_g, v_xa_wq, v_xa_wk, v_xa_wv, v_xa_q_norm_g, v_xa_k_norm_g, v_xa_wo, v_norm_ffn_g, v_ffn_w_in, v_ffn_w_out):
    given = dict(locals())
    w = {n: given[n] for n in WEIGHTS}
    m = {n: given['m_' + n] for n in WEIGHTS}
    v = {n: given['v_' + n] for n in WEIGHTS}

    depth = w_in.shape[0]
    chip = 2 * lax.axis_index("x") + lax.axis_index("y")

    taps_bits = lax.bitcast_convert_type(conv_dw_w, BF16)
    gathered = gather_weights(MATRICES + ['taps'], [w[n].astype(BF16) for n in MATRICES] + [taps_bits],
                              'gather_weights')
    mats = {}
    for k, n in enumerate(MATRICES):
        whole = gathered[k]
        if n in SLOTTED:
            whole = jnp.concatenate([whole[j] for j in range(N_CHIPS)], axis=SHARD_AXIS[n])
        mats[n] = [whole[l] for l in range(depth)]
    taps = jnp.concatenate([lax.bitcast_convert_type(gathered[-1][j], F32) for j in range(N_CHIPS)], axis=2)
    local_w = {n: w[n] for n in REPLICATED}
    local_w['conv_dw_w'] = taps

    loss, gx, g_w, g_mats = local_step(local_w, mats, x[0], mem[0], loss_target[0])

    outs = {}
    to_send = []
    for n in MATRICES:
        for l in range(depth):
            g = g_mats[n][l]
            if n in SLOTTED:
                g = jnp.stack([_shard_of(g, SHARD_AXIS[n] - 1, j) for j in range(N_CHIPS)])
            to_send.append(g)
    mine = scatter_grads(MATRICES, to_send, depth, 'scatter_grads')
    other = swap_cores(mine, 'swap_cores')
    for k, n in enumerate(MATRICES):
        (outs['grad_' + n], outs['delta_' + n], outs['new_m_' + n],
         outs['new_v_' + n]) = adamw_matrix(mine[k], other[k], w[n], m[n], v[n], 'adamw_' + n)

    reduced = allreduce_small(pack([g_w[n] for n in REPLICATED] + [g_w['conv_dw_w'], loss.reshape(1)], F32),
                              'allreduce_small')
    reduced = unpack(reduced, [w[n].shape for n in REPLICATED] + [taps.shape, (1,)])
    total_loss = reduced[-1].reshape(())
    tap_cols = conv_dw_w.shape[2]
    reduced[-2] = lax.dynamic_slice_in_dim(reduced[-2], chip * tap_cols, tap_cols, axis=2)
    small_names = REPLICATED + ['conv_dw_w']
    deltas, new_ms, new_vs = adamw_small(reduced[:-1], [w[n] for n in small_names], [m[n] for n in small_names],
                                         [v[n] for n in small_names], 'adamw_small')
    for k, n in enumerate(small_names):
        outs['grad_' + n], outs['delta_' + n] = reduced[k], deltas[k]
        outs['new_m_' + n], outs['new_v_' + n] = new_ms[k], new_vs[k]
    return (total_loss, gx[None], *[outs['grad_' + n] for n in WEIGHTS], *[outs['delta_' + n] for n in WEIGHTS],
            *[outs['new_m_' + n] for n in WEIGHTS], *[outs['new_v_' + n] for n in WEIGHTS])
```

```python
import functools
import math

import jax
import jax.numpy as jnp
from jax import lax
from jax.experimental import pallas as pl
from jax.experimental.pallas import tpu as pltpu

F32 = jnp.float32
BF16 = jnp.bfloat16
MESH = pl.DeviceIdType.MESH
HIGHEST = lax.Precision.HIGHEST

EPS = 1e-6
LANES = 128
SUBLANES = 8
VMEM_LIMIT = 56 * 1024 * 1024

SB_HEAD_DIM = 64
SB_WIDTH = 512
CONV_CH = 256
CONV_WIDTH = 31
CONV_HALO = 32
SSM_CH = 256
SSM_GROUPS = 16
SSM_GROUP = 16
SSM_STATE = 64
SSM_LANES = SSM_GROUPS * SSM_STATE
XA_HEADS = 4
XA_HEAD_DIM = 256
SB_CUT = 110.0

ADAM_LR = 0.001
ADAM_B1 = 0.9
ADAM_B2 = 0.999
ADAM_EPS = 1e-08
ADAM_WD = 0.01
ADAM_STEP = 10

WEIGHTS = ['norm_mix_g', 'w_in', 'sb_q_norm_g', 'sb_k_norm_g', 'conv_dw_w', 'conv_dw_b', 'conv_ln_g',
           'conv_ln_b', 'conv_pw2_w', 'ssm_lam_re', 'ssm_lam_im', 'ssm_log_dt', 'ssm_b_re', 'ssm_b_im',
           'ssm_c_re', 'ssm_c_im', 'ssm_d', 'ssm_glu_w', 'branch_norm_g', 'w_out', 'norm_xa_g',
           'norm_mem_g', 'xa_wq', 'xa_wk', 'xa_wv', 'xa_q_norm_g', 'xa_k_norm_g', 'xa_wo', 'norm_ffn_g',
           'ffn_w_in', 'ffn_w_out']
SHARD_AXIS = {'w_in': 2, 'conv_dw_w': 2, 'conv_pw2_w': 1, 'ssm_glu_w': 2, 'w_out': 1, 'xa_wq': 1,
              'xa_wk': 1, 'xa_wv': 1, 'xa_wo': 1, 'ffn_w_in': 2, 'ffn_w_out': 1}
MATRICES = [n for n in WEIGHTS if n in SHARD_AXIS and n != 'conv_dw_w']
REPLICATED = [n for n in WEIGHTS if n not in SHARD_AXIS]
N_CHIPS = 4
N_DEV = 8


def _cparams(sem=None, **kw):
    if sem is not None:
        kw['dimension_semantics'] = sem
    return pltpu.CompilerParams(vmem_limit_bytes=VMEM_LIMIT, **kw)


def _pick(n, target):
    best = None
    d = LANES
    while d <= min(n, target):
        if n % d == 0:
            best = d
        d += LANES
    return best if best is not None else n


def _rows_for(n_rows, width):
    t = 512
    while t > 8 and t * width > 768 * 1024:
        t //= 2
    return min(t, n_rows)


def _dg(a, b, ca, cb):
    return lax.dot_general(a.astype(BF16), b.astype(BF16), (((ca,), (cb,)), ((), ())),
                           preferred_element_type=F32)


@jax.custom_vjp
def bdot_nn(a, b):
    return _dg(a, b, 1, 0)


def _bdot_nn_fwd(a, b):
    return _dg(a, b, 1, 0), (a, b)


def _bdot_nn_bwd(res, g):
    a, b = res
    return _dg(g, b, 1, 1), _dg(a, g, 0, 0)


bdot_nn.defvjp(_bdot_nn_fwd, _bdot_nn_bwd)


@jax.custom_vjp
def bdot_nt(a, b):
    return _dg(a, b, 1, 1)


def _bdot_nt_fwd(a, b):
    return _dg(a, b, 1, 1), (a, b)


def _bdot_nt_bwd(res, g):
    a, b = res
    return _dg(g, b, 1, 0), _dg(g, a, 0, 0)


bdot_nt.defvjp(_bdot_nt_fwd, _bdot_nt_bwd)


def mm(a, b, mode, name, out_dtype=F32, add=None, ride=None):
    if mode == 'nn':
        M, K = a.shape
        N = b.shape[1]
    elif mode == 'nt':
        M, K = a.shape
        N = b.shape[0]
    else:
        K, M = a.shape
        N = b.shape[1]
    if mode == 'tn':
        tm, tn, tk = _pick(M, 1536), _pick(N, 1536), _pick(K, 512)
    else:
        tm, tn = _pick(M, 512), _pick(N, 1536)
        tk = K if K <= 2816 else _pick(K, 1536)
    nk = K // tk
    ca, cb = {'nn': (1, 0), 'nt': (1, 1), 'tn': (0, 0)}[mode]

    grid = (M // tm, N // tn, nk)

    def body(*refs):
        ins, (o_ref,), scratch, riding = _split_refs(ride, 3 if add is not None else 2, 1, refs)
        a_ref, b_ref = ins[:2]
        add_ref = ins[2] if add is not None else None
        i, j, k = pl.program_id(0), pl.program_id(1), pl.program_id(2)
        ride_done = None
        if riding is not None:
            first = jnp.logical_and(i == 0, jnp.logical_and(j == 0, k == 0))
            last = jnp.logical_and(i == grid[0] - 1, jnp.logical_and(j == grid[1] - 1, k == nk - 1))
            ride_done = ride.at_ends(riding, first, last)

        def finish(acc):
            if add_ref is not None:
                acc = acc + add_ref[...]
            o_ref[...] = acc.astype(out_dtype)

        if nk == 1:
            finish(_dg(a_ref[...], b_ref[...], ca, cb))
        else:
            acc_ref, = scratch

            @pl.when(k == 0)
            def _():
                acc_ref[...] = jnp.zeros_like(acc_ref)

            acc_ref[...] += _dg(a_ref[...], b_ref[...], ca, cb)

            @pl.when(k == nk - 1)
            def _():
                finish(acc_ref[...])

        if ride_done is not None:
            ride_done()

    if mode == 'nn':
        a_spec = pl.BlockSpec((tm, tk), lambda i, j, k: (i, k))
        b_spec = pl.BlockSpec((tk, tn), lambda i, j, k: (k, j))
    elif mode == 'nt':
        a_spec = pl.BlockSpec((tm, tk), lambda i, j, k: (i, k))
        b_spec = pl.BlockSpec((tn, tk), lambda i, j, k: (j, k))
    else:
        a_spec = pl.BlockSpec((tk, tm), lambda i, j, k: (k, i))
        b_spec = pl.BlockSpec((tk, tn), lambda i, j, k: (k, j))
    out_spec = pl.BlockSpec((tm, tn), lambda i, j, k: (i, j))
    own_in = [a_spec, b_spec] + ([out_spec] if add is not None else [])
    operands = (a, b) if add is None else (a, b, add)
    scratch = [pltpu.VMEM((tm, tn), F32)] if nk > 1 else []
    if ride is None:
        return pl.pallas_call(
            body, name=name, grid=grid, in_specs=own_in, out_specs=out_spec,
            out_shape=jax.ShapeDtypeStruct((M, N), out_dtype), scratch_shapes=scratch,
            compiler_params=_cparams(("parallel", "parallel", "arbitrary")),
        )(*operands)
    outs = pl.pallas_call(
        body, name=name, grid=grid, in_specs=own_in + ride.in_specs, out_specs=[out_spec] + ride.out_specs,
        out_shape=[jax.ShapeDtypeStruct((M, N), out_dtype)] + ride.out_shape,
        scratch_shapes=scratch + ride.scratch,
        compiler_params=_cparams(("arbitrary", "arbitrary", "arbitrary"), has_side_effects=True),
    )(*operands, *ride.arrays)
    return outs[0], outs[1:]


class Riders:
    def __init__(self, gather=(), scatter=()):
        self.gather_names = [n for n, _ in gather]
        self.shards = tuple(s for _, s in gather)
        self.scatter_names = [n for n, _ in scatter]
        self.slots = tuple(s for _, s in scatter)

    def gather_ride(self, shards):
        return gather_ride(self.gather_names, list(shards)) if shards else None

    def scatter_ride(self, grads):
        if not grads:
            return None
        return scatter_ride(self.scatter_names, [_cut_for_chips(n, g) for n, g in zip(self.scatter_names, grads)])

    def whole_slots(self, slots):
        return tuple(jnp.zeros(_whole_shape(n, s.shape[1:]), BF16) for n, s in zip(self.scatter_names, slots))


NO_RIDERS = Riders()


def linear(x, w, slot, name, residual=None, riders=NO_RIDERS):
    @jax.custom_vjp
    def op(x, w, slot, residual, shards, slots):
        ride = riders.gather_ride(shards)
        y = mm(x, w, 'nn', name + '_fwd', add=residual, ride=ride)
        y, gathered = y if ride is not None else (y, ())
        return y, tuple(gathered), riders.whole_slots(slots)

    def op_fwd(x, w, slot, residual, shards, slots):
        return op(x, w, slot, residual, shards, slots), (x, w, shards)

    def op_bwd(res, cts):
        x, w, shards = res
        g, _, slot_grads = cts
        ride = riders.scatter_ride(slot_grads)
        dx = mm(g, w, 'nt', name + '_dx', ride=ride)
        dx, received = dx if ride is not None else (dx, ())
        return (dx, jnp.zeros_like(w), mm(x, g, 'tn', name + '_dw', BF16), None if residual is None else g,
                tuple(jnp.zeros_like(s) for s in shards), tuple(received))

    op.defvjp(op_fwd, op_bwd)
    return op(x, w, slot, residual, riders.shards, riders.slots)


def rowwise(f, rows, params, consts, out_widths, name, need_row_grad=None, block_rows=None, carry=()):
    nr, npar, nc, nout = len(rows), len(params), len(consts), len(out_widths)
    carry = tuple(carry)
    L = rows[0].shape[0]
    widths = [r.shape[1] for r in rows]
    T = block_rows or _rows_for(L, max(widths + list(out_widths)))
    n = L // T
    need = list(need_row_grad) if need_row_grad is not None else [True] * nr
    pshapes = [p.shape for p in params]
    cshapes = [c.shape for c in consts]

    row_specs = [pl.BlockSpec((T, w), lambda i: (i, 0)) for w in widths]
    par_specs = [pl.BlockSpec(s, lambda i: (0, 0)) for s in pshapes]
    con_specs = [pl.BlockSpec(s, lambda i: (0, 0)) for s in cshapes]
    out_specs = [pl.BlockSpec((T, w), lambda i: (i, 0)) for w in out_widths]

    def fwd_call(rows, params, consts):
        def body(*refs):
            ins = [r[...] for r in refs[:nr + npar + nc]]
            outs = f(*ins)
            for o_ref, val in zip(refs[nr + npar + nc:], outs):
                o_ref[...] = val

        return pl.pallas_call(
            body, name=name + '_fwd', grid=(n,),
            in_specs=row_specs + par_specs + con_specs, out_specs=out_specs,
            out_shape=[jax.ShapeDtypeStruct((L, w), F32) for w in out_widths],
            compiler_params=_cparams(("parallel",)),
        )(*rows, *params, *consts)

    def bwd_call(rows, params, consts, cts, carried):
        grad_rows = [k for k in range(nr) if need[k]]
        n_in = nr + npar + nc + nout

        def body(*refs):
            i = pl.program_id(0)
            rv = [r[...] for r in refs[:nr]]
            pv = [r[...] for r in refs[nr:nr + npar]]
            cv = [r[...] for r in refs[nr + npar:nr + npar + nc]]
            ctv = tuple(r[...] for r in refs[nr + npar + nc:n_in])
            carried_refs = dict(zip(carry, refs[n_in:n_in + len(carry)]))
            orefs = refs[n_in + len(carry):]
            _, vjp = jax.vjp(lambda *rp: tuple(f(*rp, *cv)), *rv, *pv)
            g = vjp(ctv)
            for slot, k in enumerate(grad_rows):
                orefs[slot][...] = g[k] + carried_refs[k][...] if k in carried_refs else g[k]

            @pl.when(i == 0)
            def _():
                for k in range(npar):
                    orefs[len(grad_rows) + k][...] = jnp.zeros(pshapes[k], F32)

            for k in range(npar):
                orefs[len(grad_rows) + k][...] += g[nr + k]

        outs = pl.pallas_call(
            body, name=name + '_bwd', grid=(n,),
            in_specs=row_specs + par_specs + con_specs + out_specs + [row_specs[k] for k in carry],
            out_specs=[row_specs[k] for k in grad_rows] + par_specs,
            out_shape=[jax.ShapeDtypeStruct((L, widths[k]), F32) for k in grad_rows]
            + [jax.ShapeDtypeStruct(s, F32) for s in pshapes],
            compiler_params=_cparams(("arbitrary",)),
        )(*rows, *params, *consts, *cts, *carried)
        drows = []
        slot = 0
        for k in range(nr):
            if need[k]:
                drows.append(outs[slot])
                slot += 1
            else:
                drows.append(jnp.zeros_like(rows[k]))
        return tuple(drows), tuple(outs[len(grad_rows):])

    @jax.custom_vjp
    def op(rows, params, consts):
        return tuple(fwd_call(rows, params, consts)) + tuple(rows[k] for k in carry)

    def op_fwd(rows, params, consts):
        return op(rows, params, consts), (rows, params, consts)

    def op_bwd(res, cts):
        rows, params, consts = res
        drows, dparams = bwd_call(rows, params, consts, cts[:nout], cts[nout:])
        return drows, dparams, tuple(jnp.zeros_like(c) for c in consts)

    op.defvjp(op_fwd, op_bwd)
    return op(tuple(rows), tuple(params), tuple(consts))


def _rms(x, g):
    return x * lax.rsqrt(jnp.mean(x * x, axis=-1, keepdims=True) + EPS) * g


def rmsnorm(x, g, name, carry=False):
    out = rowwise(lambda x, g: (_rms(x, g),), [x], [g.reshape(1, -1)], [], [x.shape[1]], name,
                  carry=(0,) if carry else ())
    return out if carry else out[0]


def _split3(x):
    hi = x.astype(BF16)
    r = x - hi.astype(F32)
    mid = r.astype(BF16)
    return hi, mid, (r - mid.astype(F32)).astype(BF16)


@jax.custom_vjp
def select_mm(x, sel):
    return sum(_dg(t, sel, 1, 0) for t in _split3(x))


def _select_mm_fwd(x, sel):
    return select_mm(x, sel), sel


def _select_mm_bwd(sel, g):
    return sum(_dg(t, sel, 1, 1) for t in _split3(g)), jnp.zeros_like(sel)


select_mm.defvjp(_select_mm_fwd, _select_mm_bwd)


def groupnorm(x, g, group, name):
    width = x.shape[1]
    g_full = jnp.tile(g.reshape(1, group), (1, width // group))
    if group % LANES == 0:
        def f(x, g_full):
            outs = []
            for lo in range(0, width, group):
                xs = x[:, lo:lo + group]
                outs.append(_rms(xs, g_full[:, lo:lo + group]))
            return (jnp.concatenate(outs, axis=-1),)

        return rowwise(f, [x], [g_full], [], [width], name)[0]

    gid = jnp.arange(width) // group
    sel = (gid[:, None] == jnp.arange(LANES)[None, :]).astype(BF16)

    def f(x, g_full, sel, sel_t):
        ms = select_mm(x * x, sel) * (1.0 / group)
        inv = select_mm(lax.rsqrt(ms + EPS), sel_t)
        return (x * inv * g_full,)

    return rowwise(f, [x], [g_full], [sel, sel.T], [width], name)[0]


def glu(x, name):
    half = x.shape[1] // 2

    def f(x):
        return (x[:, :half] * jax.nn.sigmoid(x[:, half:]),)

    return rowwise(f, [x], [], [], [half], name)[0]


def swiglu(x, name):
    half = x.shape[1] // 2

    def f(x):
        gate = x[:, :half]
        return (gate * jax.nn.sigmoid(gate) * x[:, half:],)

    return rowwise(f, [x], [], [], [half], name)[0]


def ln_silu(x, g, b, name):
    def f(x, g, b):
        mu = jnp.mean(x, axis=-1, keepdims=True)
        xc = x - mu
        var = jnp.mean(xc * xc, axis=-1, keepdims=True)
        y = xc * lax.rsqrt(var + EPS) * g + b
        return (y * jax.nn.sigmoid(y),)

    return rowwise(f, [x], [g.reshape(1, -1), b.reshape(1, -1)], [], [x.shape[1]], name)[0]


def branch_norms(o_sb, o_conv, o_ssm, g, name):
    w1, w2 = o_sb.shape[1], o_conv.shape[1]

    def f(a, b, c, g):
        return (jnp.concatenate([_rms(a, g[:, :w1]), _rms(b, g[:, w1:w1 + w2]), _rms(c, g[:, w1 + w2:])],
                                axis=-1),)

    return rowwise(f, [o_sb, o_conv, o_ssm], [g.reshape(1, -1)], [], [g.shape[-1]], name)[0]


def xa_core(q, k, v, name):
    scale = XA_HEAD_DIM ** -0.5

    def f(q, k, v):
        outs = []
        for h in range(XA_HEADS):
            sl = slice(h * XA_HEAD_DIM, (h + 1) * XA_HEAD_DIM)
            s = bdot_nt(q[:, sl], k[:, sl]) * scale
            m = lax.stop_gradient(jnp.max(s, axis=-1, keepdims=True))
            e = jnp.exp(s - m)
            p = e / jnp.sum(e, axis=-1, keepdims=True)
            outs.append(bdot_nn(p, v[:, sl]))
        return (jnp.concatenate(outs, axis=-1),)

    return rowwise(f, [q], [k, v], [], [q.shape[1]], name, block_rows=min(256, q.shape[0]))[0]


def loss_rows(y, target, name):
    def f(y, t):
        d = y - t
        return (0.5 * jnp.mean(d * d, axis=-1, keepdims=True),)

    return rowwise(f, [y, target], [], [], [1], name, need_row_grad=[True, False])[0]


def _hilo(x, ones_bf16):
    hi = x.astype(BF16)
    lo = (x - hi.astype(F32)).astype(BF16)
    return _dg(hi, ones_bf16, 1, 0) + _dg(lo, ones_bf16, 1, 0)


def _sb_block(qh, kb, c, valid, strict_upper):
    z = _dg(qh, kb, 1, 1)
    a = jnp.minimum(z, 0.0) - jnp.log(1.0 + jnp.exp(-jnp.abs(z)))
    b = jnp.where(valid, a - z, 0.0)
    s = _hilo(b, strict_upper) + c
    w = jnp.where(valid, jnp.exp(a + s), 0.0)
    return a, b, w


def _sb_masks(T):
    row = lax.broadcasted_iota(jnp.int32, (T, T), 0)
    col = lax.broadcasted_iota(jnp.int32, (T, T), 1)
    return col < row, (row > col).astype(BF16), (row >= col).astype(BF16)


HEADS_PER_BLOCK = LANES // SB_HEAD_DIM


def _head_mask(h):
    lane = lax.broadcasted_iota(jnp.int32, (1, LANES), 1)
    return (lane // SB_HEAD_DIM == h).astype(F32)


def _max_all(columns):
    m = columns[0]
    for c in columns[1:]:
        m = jnp.maximum(m, c)
    return jnp.max(m)


def _ride_call(body, name, grid, in_specs, out_specs, out_shape, operands, ride, semantics):
    if ride is None:
        outs = pl.pallas_call(body, name=name, grid=grid, in_specs=in_specs, out_specs=out_specs,
                              out_shape=out_shape, compiler_params=_cparams(semantics))(*operands)
        return outs, ()
    outs = pl.pallas_call(
        body, name=name, grid=grid, in_specs=in_specs + ride.in_specs, out_specs=out_specs + ride.out_specs,
        out_shape=out_shape + ride.out_shape, scratch_shapes=ride.scratch,
        compiler_params=_cparams(("arbitrary",) * len(grid), has_side_effects=True),
    )(*operands, *ride.arrays)
    return outs[:len(out_shape)], outs[len(out_shape):]


def _ride_ends(ride, riding, grid):
    if riding is None:
        return lambda: None
    first, last = None, None
    for axis, size in enumerate(grid):
        at0, at1 = pl.program_id(axis) == 0, pl.program_id(axis) == size - 1
        first = at0 if first is None else jnp.logical_and(first, at0)
        last = at1 if last is None else jnp.logical_and(last, at1)
    return ride.at_ends(riding, first, last)


def _sb_fwd_call(q, k, v, T, name, ride=None):
    L, W = q.shape
    scale = SB_HEAD_DIM ** -0.5
    grid = (W // LANES, L // T)

    def body(*refs):
        (q_ref, k_ref, v_ref), (o_ref,), _, riding = _split_refs(ride, 3, 1, refs)
        ride_done = _ride_ends(ride, riding, grid)
        i = pl.program_id(1)
        causal, strict_upper, _ = _sb_masks(T)
        q2 = q_ref[...] * scale
        masks = [_head_mask(h) for h in range(HEADS_PER_BLOCK)]
        qs = [(q2 * hm).astype(BF16) for hm in masks]
        zero = jnp.zeros((T, 1), F32)

        def cond(state):
            j, cs, _ = state
            return jnp.logical_and(j >= 0, _max_all(cs) > -SB_CUT)

        def step(state):
            j, cs, acc = state
            r0 = pl.multiple_of(j * T, T)
            valid = jnp.logical_or(causal, j != i)
            kb = k_ref[pl.ds(r0, T), :].astype(BF16)
            vb = v_ref[pl.ds(r0, T), :]
            new_cs = []
            for hm, qh, c in zip(masks, qs, cs):
                _, b, w = _sb_block(qh, kb, c, valid, strict_upper)
                vh = (vb * hm).astype(BF16)
                w_hi = w.astype(BF16)
                w_lo = (w - w_hi.astype(F32)).astype(BF16)
                acc = acc + _dg(w_hi, vh, 1, 0) + _dg(w_lo, vh, 1, 0)
                new_cs.append(c + jnp.sum(b, axis=1, keepdims=True))
            return j - 1, tuple(new_cs), acc

        _, _, acc = lax.while_loop(cond, step, (i, (zero,) * HEADS_PER_BLOCK, jnp.zeros((T, LANES), F32)))
        o_ref[...] = acc
        ride_done()

    (o,), rode = _ride_call(
        body, name, grid,
        [pl.BlockSpec((T, LANES), lambda p, i: (i, p)),
         pl.BlockSpec((L, LANES), lambda p, i: (0, p)),
         pl.BlockSpec((L, LANES), lambda p, i: (0, p))],
        [pl.BlockSpec((T, LANES), lambda p, i: (i, p))], [jax.ShapeDtypeStruct((L, W), F32)],
        (q, k, v), ride, ("parallel", "parallel"))
    return o, rode


def _sb_bwd_call(q, k, v, o, do, T, name, ride=None):
    L, W = q.shape
    scale = SB_HEAD_DIM ** -0.5
    grid = (W // LANES, L // T)

    def body(*refs):
        (q_ref, k_ref, v_ref, o_ref, do_ref), (dq_ref, dk_ref, dv_ref), _, riding = _split_refs(ride, 5, 3, refs)
        ride_done = _ride_ends(ride, riding, grid)
        i = pl.program_id(1)

        @pl.when(i == 0)
        def _():
            dk_ref[...] = jnp.zeros_like(dk_ref)
            dv_ref[...] = jnp.zeros_like(dv_ref)

        causal, strict_upper, upper = _sb_masks(T)
        q2 = q_ref[...] * scale
        do2 = do_ref[...]
        o2 = o_ref[...]
        masks = [_head_mask(h) for h in range(HEADS_PER_BLOCK)]
        qs = [(q2 * hm).astype(BF16) for hm in masks]
        dos = [(do2 * hm).astype(BF16) for hm in masks]
        totals = [jnp.sum(doh.astype(F32) * o2, axis=1, keepdims=True) for doh in dos]
        zero = jnp.zeros((T, 1), F32)

        def cond(state):
            j, cs, _, _ = state
            return jnp.logical_and(j >= 0, _max_all(cs) > -SB_CUT)

        def step(state):
            j, cs, rs, dq = state
            r0 = pl.multiple_of(j * T, T)
            valid = jnp.logical_or(causal, j != i)
            kf = k_ref[pl.ds(r0, T), :]
            kb = kf.astype(BF16)
            vb = v_ref[pl.ds(r0, T), :].astype(BF16)
            new_cs, new_rs = [], []
            dk = jnp.zeros((T, LANES), F32)
            dv = jnp.zeros((T, LANES), F32)
            for hm, qh, doh, total, c, r in zip(masks, qs, dos, totals, cs, rs):
                a, b, w = _sb_block(qh, kb, c, valid, strict_upper)
                e = _dg(doh, vb, 1, 1) * w
                before = total - (_hilo(e, upper) + r)
                dz = jnp.where(valid, e * jnp.exp(b) - before * jnp.exp(a), 0.0).astype(BF16)
                dq = dq + _dg(dz, kf * hm, 1, 0)
                dk = dk + _dg(dz, qh, 0, 0)
                dv = dv + _dg(w, doh, 0, 0)
                new_cs.append(c + jnp.sum(b, axis=1, keepdims=True))
                new_rs.append(r + jnp.sum(e, axis=1, keepdims=True))
            dk_ref[pl.ds(r0, T), :] += dk
            dv_ref[pl.ds(r0, T), :] += dv
            return j - 1, tuple(new_cs), tuple(new_rs), dq

        init = (i, (zero,) * HEADS_PER_BLOCK, (zero,) * HEADS_PER_BLOCK, jnp.zeros((T, LANES), F32))
        dq = lax.while_loop(cond, step, init)[3]
        dq_ref[...] = dq * scale
        ride_done()

    blk = pl.BlockSpec((T, LANES), lambda p, i: (i, p))
    full = pl.BlockSpec((L, LANES), lambda p, i: (0, p))
    return _ride_call(body, name, grid, [blk, full, full, blk, blk], [blk, full, full],
                      [jax.ShapeDtypeStruct((L, W), F32)] * 3, (q, k, v, o, do), ride, ("parallel", "arbitrary"))


def sb_attention(q, k, v, name, riders=NO_RIDERS):
    T = min(256, q.shape[0])

    @jax.custom_vjp
    def op(q, k, v, shards, slots):
        o, gathered = _sb_fwd_call(q, k, v, T, name + '_fwd', riders.gather_ride(shards))
        return o, tuple(gathered), riders.whole_slots(slots)

    def op_fwd(q, k, v, shards, slots):
        out = op(q, k, v, shards, slots)
        return out, (q, k, v, out[0], shards)

    def op_bwd(res, cts):
        q, k, v, o, shards = res
        do, _, slot_grads = cts
        grads, received = _sb_bwd_call(q, k, v, o, do, T, name + '_bwd', riders.scatter_ride(slot_grads))
        return (*grads, tuple(jnp.zeros_like(s) for s in shards), tuple(received))

    op.defvjp(op_fwd, op_bwd)
    return op(q, k, v, riders.shards, riders.slots)


def _dwconv_fwd_call(x, w, b, T, name):
    L, C = x.shape
    per = T // CONV_HALO
    lead = CONV_HALO - (CONV_WIDTH - 1)

    def body(x_ref, halo_ref, w_ref, b_ref, o_ref, buf):
        i = pl.program_id(0)
        buf[0:CONV_HALO, :] = jnp.where(i > 0, halo_ref[...], 0.0)
        buf[CONV_HALO:CONV_HALO + T, :] = x_ref[...]
        acc = jnp.zeros((T, C), F32) + b_ref[...]
        for j in range(CONV_WIDTH):
            acc = acc + w_ref[j:j + 1, :] * buf[lead + j:lead + j + T, :]
        o_ref[...] = acc

    return pl.pallas_call(
        body, name=name, grid=(L // T,),
        in_specs=[pl.BlockSpec((T, C), lambda i: (i, 0)),
                  pl.BlockSpec((CONV_HALO, C), lambda i: (jnp.maximum(i * per - 1, 0), 0)),
                  pl.BlockSpec(w.shape, lambda i: (0, 0)),
                  pl.BlockSpec(b.shape, lambda i: (0, 0))],
        out_specs=pl.BlockSpec((T, C), lambda i: (i, 0)),
        out_shape=jax.ShapeDtypeStruct((L, C), F32),
        scratch_shapes=[pltpu.VMEM((T + CONV_HALO, C), F32)],
        compiler_params=_cparams(("parallel",)),
    )(x, x, w, b)


def _dwconv_bwd_call(x, w, g, T, name):
    L, C = x.shape
    per = T // CONV_HALO
    n = L // T
    last_halo = L // CONV_HALO - 1
    lead = CONV_HALO - (CONV_WIDTH - 1)

    def body(x_ref, xh_ref, g_ref, gh_ref, w_ref, dx_ref, dw_ref, db_ref, bufx, bufg):
        i = pl.program_id(0)
        bufx[0:CONV_HALO, :] = jnp.where(i > 0, xh_ref[...], 0.0)
        bufx[CONV_HALO:CONV_HALO + T, :] = x_ref[...]
        gm = g_ref[...]
        bufg[0:T, :] = gm
        bufg[T:T + CONV_HALO, :] = jnp.where(i < n - 1, gh_ref[...], 0.0)
        acc = jnp.zeros((T, C), F32)
        for j in range(CONV_WIDTH):
            off = CONV_WIDTH - 1 - j
            acc = acc + w_ref[j:j + 1, :] * bufg[off:off + T, :]
        dx_ref[...] = acc

        @pl.when(i == 0)
        def _():
            dw_ref[...] = jnp.zeros_like(dw_ref)
            db_ref[...] = jnp.zeros_like(db_ref)

        for j in range(CONV_WIDTH):
            dw_ref[j:j + 1, :] += jnp.sum(gm * bufx[lead + j:lead + j + T, :], axis=0, keepdims=True)
        db_ref[...] += jnp.sum(gm, axis=0, keepdims=True)

    return pl.pallas_call(
        body, name=name, grid=(n,),
        in_specs=[pl.BlockSpec((T, C), lambda i: (i, 0)),
                  pl.BlockSpec((CONV_HALO, C), lambda i: (jnp.maximum(i * per - 1, 0), 0)),
                  pl.BlockSpec((T, C), lambda i: (i, 0)),
                  pl.BlockSpec((CONV_HALO, C), lambda i: (jnp.minimum((i + 1) * per, last_halo), 0)),
                  pl.BlockSpec(w.shape, lambda i: (0, 0))],
        out_specs=[pl.BlockSpec((T, C), lambda i: (i, 0)),
                   pl.BlockSpec(w.shape, lambda i: (0, 0)),
                   pl.BlockSpec((1, C), lambda i: (0, 0))],
        out_shape=[jax.ShapeDtypeStruct((L, C), F32), jax.ShapeDtypeStruct(w.shape, F32),
                   jax.ShapeDtypeStruct((1, C), F32)],
        scratch_shapes=[pltpu.VMEM((T + CONV_HALO, C), F32), pltpu.VMEM((T + CONV_HALO, C), F32)],
        compiler_params=_cparams(("arbitrary",)),
    )(x, x, g, g, w)


def dwconv(x, w, b, name):
    T = min(512, x.shape[0])

    @jax.custom_vjp
    def op(x, w, b):
        return _dwconv_fwd_call(x, w, b, T, name + '_fwd')

    def op_fwd(x, w, b):
        return _dwconv_fwd_call(x, w, b, T, name + '_fwd'), (x, w)

    def op_bwd(res, g):
        x, w = res
        return tuple(_dwconv_bwd_call(x, w, g, T, name + '_bwd'))

    op.defvjp(op_fwd, op_bwd)
    return op(x, w, b)


def _ssm_fwd_call(u, ar, ai, bbr, bbi, cr, ci, d, T, name):
    L, C = u.shape
    S = SSM_LANES

    def body(u_ref, ar_ref, ai_ref, bbr_ref, bbi_ref, cr_ref, ci_ref, d_ref,
             y_ref, xr_ref, xi_ref, st_r, st_i, in_r, in_i, out_r, out_i):
        i = pl.program_id(0)

        @pl.when(i == 0)
        def _():
            st_r[...] = jnp.zeros_like(st_r)
            st_i[...] = jnp.zeros_like(st_i)

        u_blk = u_ref[...]
        xr_ref[...] = _dg(u_blk, bbr_ref[...], 1, 0)
        xi_ref[...] = _dg(u_blk, bbi_ref[...], 1, 0)
        a_r, a_i = ar_ref[...], ai_ref[...]

        def tile(t, carry):
            sr, si = carry
            r0 = pl.multiple_of(t * SUBLANES, SUBLANES)
            in_r[...] = xr_ref[pl.ds(r0, SUBLANES), :]
            in_i[...] = xi_ref[pl.ds(r0, SUBLANES), :]
            for r in range(SUBLANES):
                nr = a_r * sr - a_i * si + in_r[r:r + 1, :]
                ni = a_r * si + a_i * sr + in_i[r:r + 1, :]
                sr, si = nr, ni
                out_r[r:r + 1, :] = sr
                out_i[r:r + 1, :] = si
            xr_ref[pl.ds(r0, SUBLANES), :] = out_r[...]
            xi_ref[pl.ds(r0, SUBLANES), :] = out_i[...]
            return sr, si

        sr, si = lax.fori_loop(0, T // SUBLANES, tile, (st_r[0:1, :], st_i[0:1, :]))
        st_r[0:1, :] = sr
        st_i[0:1, :] = si
        y_ref[...] = (_dg(xr_ref[...], cr_ref[...], 1, 0) - _dg(xi_ref[...], ci_ref[...], 1, 0)
                      + d_ref[...] * u_blk)

    full = lambda a: pl.BlockSpec(a.shape, lambda i: (0, 0))
    return pl.pallas_call(
        body, name=name, grid=(L // T,),
        in_specs=[pl.BlockSpec((T, C), lambda i: (i, 0))] + [full(a) for a in (ar, ai, bbr, bbi, cr, ci, d)],
        out_specs=[pl.BlockSpec((T, C), lambda i: (i, 0)), pl.BlockSpec((T, S), lambda i: (i, 0)),
                   pl.BlockSpec((T, S), lambda i: (i, 0))],
        out_shape=[jax.ShapeDtypeStruct((L, C), F32), jax.ShapeDtypeStruct((L, S), F32),
                   jax.ShapeDtypeStruct((L, S), F32)],
        scratch_shapes=[pltpu.VMEM((SUBLANES, S), F32)] * 6,
        compiler_params=_cparams(("arbitrary",)),
    )(u, ar, ai, bbr, bbi, cr, ci, d)


def _ssm_bwd_call(u, xr, xi, dy, ar, ai, bbr, bbi, cr, ci, d, T, name):
    L, C = u.shape
    S = SSM_LANES
    n = L // T
    per = T // SUBLANES

    def body(u_ref, xr_ref, xi_ref, hr_ref, hi_ref, dy_ref, ar_ref, ai_ref, bbr_ref, bbi_ref, cr_ref, ci_ref,
             d_ref, du_ref, dar_ref, dai_ref, dbr_ref, dbi_ref, dcr_ref, dci_ref, dd_ref,
             lam_r, lam_i, prev_r, prev_i, st_r, st_i, in_r, in_i, out_r, out_i):
        i = pl.program_id(0)
        chunk = n - 1 - i

        @pl.when(i == 0)
        def _():
            st_r[...] = jnp.zeros_like(st_r)
            st_i[...] = jnp.zeros_like(st_i)
            for ref in (dar_ref, dai_ref, dbr_ref, dbi_ref, dcr_ref, dci_ref, dd_ref):
                ref[...] = jnp.zeros_like(ref)

        dy_blk = dy_ref[...]
        u_blk = u_ref[...]
        lam_r[...] = _dg(dy_blk, cr_ref[...], 1, 1)
        lam_i[...] = -_dg(dy_blk, ci_ref[...], 1, 1)
        dcr_ref[...] += _dg(xr_ref[...], dy_blk, 0, 0)
        dci_ref[...] -= _dg(xi_ref[...], dy_blk, 0, 0)
        a_r, a_i = ar_ref[...], ai_ref[...]

        def tile(k, carry):
            lr, li = carry
            r0 = pl.multiple_of((per - 1 - k) * SUBLANES, SUBLANES)
            in_r[...] = lam_r[pl.ds(r0, SUBLANES), :]
            in_i[...] = lam_i[pl.ds(r0, SUBLANES), :]
            for r in range(SUBLANES - 1, -1, -1):
                nr = in_r[r:r + 1, :] + a_r * lr + a_i * li
                ni = in_i[r:r + 1, :] + a_r * li - a_i * lr
                lr, li = nr, ni
                out_r[r:r + 1, :] = lr
                out_i[r:r + 1, :] = li
            lam_r[pl.ds(r0, SUBLANES), :] = out_r[...]
            lam_i[pl.ds(r0, SUBLANES), :] = out_i[...]
            return lr, li

        lr, li = lax.fori_loop(0, per, tile, (st_r[0:1, :], st_i[0:1, :]))
        st_r[0:1, :] = lr
        st_i[0:1, :] = li

        l_r, l_i = lam_r[...], lam_i[...]
        du_ref[...] = _dg(l_r, bbr_ref[...], 1, 1) + _dg(l_i, bbi_ref[...], 1, 1) + d_ref[...] * dy_blk
        dbr_ref[...] += _dg(u_blk, l_r, 0, 0)
        dbi_ref[...] += _dg(u_blk, l_i, 0, 0)
        dd_ref[...] += jnp.sum(dy_blk * u_blk, axis=0, keepdims=True)

        prev_r[0:SUBLANES, :] = jnp.where(chunk > 0, hr_ref[...], 0.0)
        prev_i[0:SUBLANES, :] = jnp.where(chunk > 0, hi_ref[...], 0.0)
        prev_r[SUBLANES:SUBLANES + T, :] = xr_ref[...]
        prev_i[SUBLANES:SUBLANES + T, :] = xi_ref[...]
        p_r = prev_r[SUBLANES - 1:SUBLANES - 1 + T, :]
        p_i = prev_i[SUBLANES - 1:SUBLANES - 1 + T, :]
        dar_ref[...] += jnp.sum(l_r * p_r + l_i * p_i, axis=0, keepdims=True)
        dai_ref[...] += jnp.sum(l_i * p_r - l_r * p_i, axis=0, keepdims=True)

    rev = lambda w: pl.BlockSpec((T, w), lambda i: (n - 1 - i, 0))
    halo = pl.BlockSpec((SUBLANES, S), lambda i: (jnp.maximum((n - 1 - i) * per - 1, 0), 0))
    full = lambda a: pl.BlockSpec(a.shape, lambda i: (0, 0))
    params = (ar, ai, bbr, bbi, cr, ci, d)
    return pl.pallas_call(
        body, name=name, grid=(n,),
        in_specs=[rev(C), rev(S), rev(S), halo, halo, rev(C)] + [full(a) for a in params],
        out_specs=[rev(C)] + [full(a) for a in params],
        out_shape=[jax.ShapeDtypeStruct((L, C), F32)] + [jax.ShapeDtypeStruct(a.shape, F32) for a in params],
        scratch_shapes=[pltpu.VMEM((T, S), F32), pltpu.VMEM((T, S), F32),
                        pltpu.VMEM((T + SUBLANES, S), F32), pltpu.VMEM((T + SUBLANES, S), F32)]
        + [pltpu.VMEM((SUBLANES, S), F32)] * 6,
        compiler_params=_cparams(("arbitrary",)),
    )(u, xr, xi, xr, xi, dy, *params)


def ssm_core(u, ar, ai, bbr, bbi, cr, ci, d, name):
    T = min(256, u.shape[0])

    @jax.custom_vjp
    def op(u, ar, ai, bbr, bbi, cr, ci, d):
        return _ssm_fwd_call(u, ar, ai, bbr, bbi, cr, ci, d, T, name + '_fwd')[0]

    def op_fwd(u, ar, ai, bbr, bbi, cr, ci, d):
        y, xr, xi = _ssm_fwd_call(u, ar, ai, bbr, bbi, cr, ci, d, T, name + '_fwd')
        return y, (u, xr, xi, ar, ai, bbr, bbi, cr, ci, d)

    def op_bwd(res, dy):
        u, xr, xi, ar, ai, bbr, bbi, cr, ci, d = res
        return tuple(_ssm_bwd_call(u, xr, xi, dy, ar, ai, bbr, bbi, cr, ci, d, T, name + '_bwd'))

    op.defvjp(op_fwd, op_bwd)
    return op(u, ar, ai, bbr, bbi, cr, ci, d)


@jax.custom_vjp
def _block_diag(blocks):
    G, R, Cc = blocks.shape
    eye = jnp.eye(G, dtype=blocks.dtype)
    return (blocks[:, :, None, :] * eye[:, None, :, None]).reshape(G * R, G * Cc)


def _block_diag_fwd(blocks):
    return _block_diag(blocks), blocks.shape


def _block_diag_bwd(shape, g):
    G, R, Cc = shape
    return (jnp.stack([g[k * R:(k + 1) * R, k * Cc:(k + 1) * Cc] for k in range(G)]),)


_block_diag.defvjp(_block_diag_fwd, _block_diag_bwd)


def ssm_discretise(lam_re, lam_im, log_dt, b_re, b_im, c_re, c_im):
    dt = jnp.exp(log_dt)[:, None]
    mag = jnp.exp(lam_re * dt)
    ar, ai = mag * jnp.cos(lam_im * dt), mag * jnp.sin(lam_im * dt)
    den = lam_re * lam_re + lam_im * lam_im
    fr = ((ar - 1.0) * lam_re + ai * lam_im) / den
    fi = (ai * lam_re - (ar - 1.0) * lam_im) / den
    bbr = fr[..., None] * b_re - fi[..., None] * b_im
    bbi = fr[..., None] * b_im + fi[..., None] * b_re
    return (ar.reshape(1, SSM_LANES), ai.reshape(1, SSM_LANES),
            _block_diag(bbr.transpose(0, 2, 1)), _block_diag(bbi.transpose(0, 2, 1)),
            _block_diag(c_re.transpose(0, 2, 1)), _block_diag(c_im.transpose(0, 2, 1)))


def split_columns(p, bounds):
    @jax.custom_vjp
    def op(p):
        return tuple(p[:, lo:hi] for lo, hi in zip(bounds[:-1], bounds[1:]))

    def op_fwd(p):
        return op(p), None

    def op_bwd(_, gs):
        return (jnp.concatenate(gs, axis=1),)

    op.defvjp(op_fwd, op_bwd)
    return op(p)


DEPTH = 2
EARLY = ['w_in', 'conv_pw2_w', 'ssm_glu_w']
LATE = [n for n in MATRICES if n not in EARLY]
FFN = ['ffn_w_in', 'ffn_w_out']
GATHER_AT = {
    'start': [('w_in', 0)],
    'sb0': [(n, 0) for n in MATRICES if n != 'w_in'],
    'ffn0': [(n, 1) for n in MATRICES if n not in FFN],
    'sb1': [(n, 1) for n in FFN],
}
SCATTER_AT = {
    'sb1': [(n, 1) for n in LATE],
    'ffn0': [(n, 1) for n in EARLY],
    'sb0': [(n, 0) for n in LATE],
    'end': [(n, 0) for n in EARLY],
}


def _assemble(name, gathered):
    if name not in SLOTTED:
        return gathered
    return jnp.concatenate([gathered[j] for j in range(N_CHIPS)], axis=SHARD_AXIS[name] - 1)


def local_loss(slots, w, mats, shards, x, mem, target):
    mats = dict(mats)
    slot = {key: slots[key] for key in SCATTER_AT['end']}
    s1, s2, s3 = SB_WIDTH, 2 * SB_WIDTH, 3 * SB_WIDTH
    s4 = s3 + 2 * CONV_CH

    def riders_at(host):
        return Riders(gather=[(n, shards[(n, l)]) for n, l in GATHER_AT[host]],
                      scatter=[(n, slots[(n, l)]) for n, l in SCATTER_AT[host]])

    def take(host, gathered, handed):
        for (n, l), g in zip(GATHER_AT[host], gathered):
            mats[(n, l)] = _assemble(n, g)
        for key, s in zip(SCATTER_AT[host], handed):
            slot[key] = s

    def linear_(x, n, l, name, residual=None, host=None):
        y, gathered, handed = linear(x, mats[(n, l)], slot[(n, l)], name, residual,
                                     riders_at(host) if host else NO_RIDERS)
        if host:
            take(host, gathered, handed)
        return y

    for l in range(DEPTH):
        tag = 'l%d_' % l
        h, x = rmsnorm(x, w['norm_mix_g'][l], tag + 'norm_mix', carry=True)
        p = linear_(h, 'w_in', l, tag + 'w_in')
        q, k, v, u_conv, u_ssm = split_columns(p, (0, s1, s2, s3, s4, p.shape[1]))
        q = groupnorm(q, w['sb_q_norm_g'][l], SB_HEAD_DIM, tag + 'q_norm')
        k = groupnorm(k, w['sb_k_norm_g'][l], SB_HEAD_DIM, tag + 'k_norm')
        o_sb, gathered, handed = sb_attention(q, k, v, tag + 'sb', riders_at('sb%d' % l))
        take('sb%d' % l, gathered, handed)

        dw_w = jnp.pad(w['conv_dw_w'][l], ((0, CONV_HALO - CONV_WIDTH), (0, 0)))
        hc = dwconv(glu(u_conv, tag + 'conv_glu'), dw_w, w['conv_dw_b'][l].reshape(1, -1), tag + 'dwconv')
        hc = ln_silu(hc, w['conv_ln_g'][l], w['conv_ln_b'][l], tag + 'conv_ln')
        o_conv = linear_(hc, 'conv_pw2_w', l, tag + 'pw2')

        ar, ai, bbr, bbi, cr, ci = ssm_discretise(
            w['ssm_lam_re'][l], w['ssm_lam_im'][l], w['ssm_log_dt'][l], w['ssm_b_re'][l], w['ssm_b_im'][l],
            w['ssm_c_re'][l], w['ssm_c_im'][l])
        y = ssm_core(u_ssm, ar, ai, bbr, bbi, cr, ci, w['ssm_d'][l].reshape(1, -1), tag + 'ssm')
        o_ssm = glu(linear_(y, 'ssm_glu_w', l, tag + 'ssm_glu_w'), tag + 'ssm_glu')

        mixed = branch_norms(o_sb, o_conv, o_ssm, w['branch_norm_g'][l], tag + 'branch_norm')
        x = linear_(mixed, 'w_out', l, tag + 'w_out', residual=x)

        hx, x = rmsnorm(x, w['norm_xa_g'][l], tag + 'norm_xa', carry=True)
        hm = rmsnorm(mem, w['norm_mem_g'][l], tag + 'norm_mem')
        qx = groupnorm(linear_(hx, 'xa_wq', l, tag + 'xa_wq'), w['xa_q_norm_g'][l], XA_HEAD_DIM, tag + 'xa_qn')
        kx = groupnorm(linear_(hm, 'xa_wk', l, tag + 'xa_wk'), w['xa_k_norm_g'][l], XA_HEAD_DIM, tag + 'xa_kn')
        vx = linear_(hm, 'xa_wv', l, tag + 'xa_wv')
        x = linear_(xa_core(qx, kx, vx, tag + 'xa_core'), 'xa_wo', l, tag + 'xa_wo', residual=x)

        hf, x = rmsnorm(x, w['norm_ffn_g'][l], tag + 'norm_ffn', carry=True)
        act = swiglu(linear_(hf, 'ffn_w_in', l, tag + 'ffn_in', host='ffn0' if l == 0 else None),
                     tag + 'swiglu')
        x = linear_(act, 'ffn_w_out', l, tag + 'ffn_out', residual=x)
    return jnp.sum(loss_rows(x, target, 'loss'))


def local_step(w, mats, shards, x, mem, target):
    def shard_shape(key):
        return shards[key].shape if key in shards else _shard_shape(key[0], mats[key].shape)

    slots = {}
    for host, keys in SCATTER_AT.items():
        for key in keys:
            shape = _whole_shape(key[0], shard_shape(key)) if host == 'end' else (N_CHIPS,) + shard_shape(key)
            slots[key] = jnp.zeros(shape, BF16)
    loss, (g_mats, g_w, gx) = jax.value_and_grad(local_loss, argnums=(0, 1, 4))(
        slots, w, mats, shards, x, mem, target)
    return loss, gx, g_w, g_mats


PACK_ROWS = 2048


PIECE_ROWS = 16


def _piece_rows(size):
    rows = -(-size // LANES)
    return rows, -(-rows // PIECE_ROWS) * PIECE_ROWS


def pack(arrays, dtype):
    parts, total = [], 0
    for a in arrays:
        rows, padded = _piece_rows(a.size)
        a = a.astype(dtype)
        if a.size % LANES:
            a = jnp.pad(a.reshape(-1), (0, rows * LANES - a.size))
        a = a.reshape(rows, LANES)
        if padded != rows:
            a = jnp.pad(a, ((0, padded - rows), (0, 0)))
        parts.append(a)
        total += padded
    tail = -total % PACK_ROWS
    if tail:
        parts.append(jnp.zeros((tail, LANES), dtype))
    return jnp.concatenate(parts, axis=0)


def unpack(packed, shapes):
    out, off = [], 0
    for s in shapes:
        size = math.prod(s)
        rows, padded = _piece_rows(size)
        piece = packed[off:off + rows]
        if size % LANES:
            piece = piece.reshape(-1)[:size]
        out.append(piece.reshape(s))
        off += padded
    return out


def _mesh_pos():
    return lax.axis_index("x"), lax.axis_index("y"), lax.axis_index("c")


def _exchange_xy(n_arrays, src_of, dst_of, sems, wait):
    send_sems, recv_sems, local_sems = sems
    x, y, c = _mesh_pos()
    me = 2 * x + y
    peers = [(1 - x, y), (x, 1 - y), (1 - x, 1 - y)]
    for k in range(n_arrays):
        own = pltpu.make_async_copy(src_of(k, me), dst_of(k, me), local_sems.at[k])
        if wait:
            own.wait()
        else:
            own.start()
        for p, (px, py) in enumerate(peers):
            out = pltpu.make_async_remote_copy(
                src_ref=src_of(k, 2 * px + py), dst_ref=dst_of(k, me), send_sem=send_sems.at[3 * k + p],
                recv_sem=recv_sems.at[3 * k + p], device_id=(px, py, c), device_id_type=MESH)
            if wait:
                pltpu.make_async_remote_copy(
                    src_ref=src_of(k, me), dst_ref=dst_of(k, 2 * px + py), send_sem=send_sems.at[3 * k + p],
                    recv_sem=recv_sems.at[3 * k + p], device_id=(px, py, c), device_id_type=MESH).wait_recv()
                out.wait_send()
            else:
                out.start()


class Ride:
    def __init__(self, arrays, out_shapes, src_of, dst_of):
        self.arrays, self.out_shapes = list(arrays), list(out_shapes)
        self._src_of, self._dst_of = src_of, dst_of
        n = len(self.arrays)
        self.in_specs = [pl.BlockSpec(memory_space=pl.ANY)] * n
        self.out_specs = [pl.BlockSpec(memory_space=pl.ANY)] * len(self.out_shapes)
        self.out_shape = [jax.ShapeDtypeStruct(s, BF16) for s in self.out_shapes]
        self.scratch = [pltpu.SemaphoreType.DMA((3 * n,)), pltpu.SemaphoreType.DMA((3 * n,)),
                        pltpu.SemaphoreType.DMA((n,))]

    def run(self, parts, wait):
        ins, outs, sems = parts
        _exchange_xy(len(self.arrays), lambda k, chip: self._src_of(ins, k, chip),
                     lambda k, chip: self._dst_of(outs, k, chip), sems, wait)

    def at_ends(self, parts, first, last):
        pl.when(first)(lambda: self.run(parts, False))

        def finish():
            pl.when(last)(lambda: self.run(parts, True))
        return finish


def _split_refs(ride, n_in, n_out, refs):
    if ride is None:
        return refs[:n_in], refs[n_in:n_in + n_out], refs[n_in + n_out:], None
    ni, no = len(ride.arrays), len(ride.out_shapes)
    b = n_in + ni
    c = b + n_out
    d = c + no
    return refs[:n_in], refs[b:c], refs[d:len(refs) - 3], (refs[n_in:b], refs[c:d], refs[len(refs) - 3:])


def run_ride(ride, name):
    def body(*refs):
        parts = _split_refs(ride, 0, 0, refs)[3]
        ride.run(parts, False)
        ride.run(parts, True)

    return pl.pallas_call(
        body, name=name, in_specs=ride.in_specs, out_specs=ride.out_specs, out_shape=ride.out_shape,
        scratch_shapes=ride.scratch, compiler_params=pltpu.CompilerParams(has_side_effects=True),
    )(*ride.arrays)


SLOTTED = ('w_in', 'taps')


def _part(ref, axis, chip, size):
    start = pl.multiple_of(chip * size, size)
    index = [slice(None)] * len(ref.shape)
    index[axis] = pl.ds(start, size)
    return ref.at[tuple(index)]


def _whole_shape(name, shard_shape):
    s = list(shard_shape)
    s[SHARD_AXIS[name] - 1] *= N_CHIPS
    return tuple(s)


def _shard_shape(name, whole_shape):
    s = list(whole_shape)
    s[SHARD_AXIS[name] - 1] //= N_CHIPS
    return tuple(s)


def gather_ride(names, shards):
    def out_shape(k):
        return (N_CHIPS,) + shards[k].shape if names[k] in SLOTTED else _whole_shape(names[k], shards[k].shape)

    def dst_of(outs, k, chip):
        if names[k] in SLOTTED:
            return outs[k].at[chip]
        axis = SHARD_AXIS[names[k]] - 1
        return _part(outs[k], axis, chip, shards[k].shape[axis])

    return Ride(shards, [out_shape(k) for k in range(len(shards))], lambda ins, k, chip: ins[k], dst_of)


def scatter_ride(names, grads):
    def shard_shape(k):
        return grads[k].shape[1:] if names[k] in SLOTTED else _shard_shape(names[k], grads[k].shape)

    def src_of(ins, k, chip):
        if names[k] in SLOTTED:
            return ins[k].at[chip]
        axis = SHARD_AXIS[names[k]] - 1
        return _part(ins[k], axis, chip, shard_shape(k)[axis])

    return Ride(grads, [(N_CHIPS,) + shard_shape(k) for k in range(len(grads))], src_of,
                lambda outs, k, chip: outs[k].at[chip])


def _cut_for_chips(name, g):
    if name not in SLOTTED:
        return g
    return jnp.stack([_shard_of(g, SHARD_AXIS[name] - 1, j) for j in range(N_CHIPS)])


def swap_cores(arrays, name):
    na = len(arrays)

    def body(*refs):
        ins, outs, send_sems, recv_sems = refs[:na], refs[na:2 * na], refs[2 * na], refs[2 * na + 1]
        x, y, c = _mesh_pos()
        copies = [pltpu.make_async_remote_copy(
            src_ref=ins[k], dst_ref=outs[k], send_sem=send_sems.at[k], recv_sem=recv_sems.at[k],
            device_id=(x, y, 1 - c), device_id_type=MESH) for k in range(na)]
        for cp in copies:
            cp.start()
        for cp in copies:
            cp.wait()

    return pl.pallas_call(
        body, name=name,
        in_specs=[pl.BlockSpec(memory_space=pl.ANY)] * na,
        out_specs=[pl.BlockSpec(memory_space=pl.ANY)] * na,
        out_shape=[jax.ShapeDtypeStruct(a.shape, a.dtype) for a in arrays],
        scratch_shapes=[pltpu.SemaphoreType.DMA((na,)), pltpu.SemaphoreType.DMA((na,))],
        compiler_params=pltpu.CompilerParams(has_side_effects=True),
    )(*arrays)


def allreduce_small(buf, name):
    R = buf.shape[0]

    def body(in_ref, sum_ref, all_ref, send_sems, recv_sems):
        x, y, c = _mesh_pos()
        me = 4 * x + 2 * y + c
        all_ref[me] = in_ref[...]
        flips = [(fx, fy, fc) for fx in (0, 1) for fy in (0, 1) for fc in (0, 1)][1:]
        sends = []
        for k, (fx, fy, fc) in enumerate(flips):
            cp = pltpu.make_async_remote_copy(
                src_ref=in_ref, dst_ref=all_ref.at[me], send_sem=send_sems.at[k], recv_sem=recv_sems.at[k],
                device_id=(x ^ fx, y ^ fy, c ^ fc), device_id_type=MESH)
            cp.start()
            sends.append(cp)
        for k, (fx, fy, fc) in enumerate(flips):
            peer = 4 * (x ^ fx) + 2 * (y ^ fy) + (c ^ fc)
            pltpu.make_async_remote_copy(
                src_ref=in_ref, dst_ref=all_ref.at[peer], send_sem=send_sems.at[k], recv_sem=recv_sems.at[k],
                device_id=(x ^ fx, y ^ fy, c ^ fc), device_id_type=MESH).wait_recv()
        for cp in sends:
            cp.wait_send()
        acc = all_ref[0]
        for k in range(1, N_DEV):
            acc = acc + all_ref[k]
        sum_ref[...] = acc

    return pl.pallas_call(
        body, name=name,
        in_specs=[pl.BlockSpec(memory_space=pltpu.VMEM)],
        out_specs=[pl.BlockSpec(memory_space=pltpu.VMEM), pl.BlockSpec(memory_space=pltpu.VMEM)],
        out_shape=[jax.ShapeDtypeStruct((R, LANES), F32), jax.ShapeDtypeStruct((N_DEV, R, LANES), F32)],
        scratch_shapes=[pltpu.SemaphoreType.DMA((N_DEV - 1,)), pltpu.SemaphoreType.DMA((N_DEV - 1,))],
        compiler_params=pltpu.CompilerParams(has_side_effects=True, vmem_limit_bytes=VMEM_LIMIT),
    )(buf)[0]


def _adamw_update(g, w, m, v):
    m2 = ADAM_B1 * m + (1.0 - ADAM_B1) * g
    v2 = ADAM_B2 * v + (1.0 - ADAM_B2) * (g * g)
    m_hat = m2 / (1.0 - ADAM_B1 ** ADAM_STEP)
    v_hat = v2 / (1.0 - ADAM_B2 ** ADAM_STEP)
    return -ADAM_LR * (m_hat / (jnp.sqrt(v_hat) + ADAM_EPS) + ADAM_WD * w), m2, v2


def adamw_matrix(layer, mine, other, w, m, v, so_far, name):
    _, rows, cols = w.shape
    T = 16
    while rows % (2 * T) == 0 and 2 * T * cols <= 128 * 1024:
        T *= 2

    def body(mine_ref, other_ref, w_ref, m_ref, v_ref, *rest):
        g_out, d_out, m_out, v_out = rest[-4:]

        def total(ref):
            acc = ref[0].astype(F32)
            for k in range(1, N_CHIPS):
                acc = acc + ref[k].astype(F32)
            return acc

        g = total(mine_ref) + total(other_ref)
        g_out[...] = g
        d_out[...], m_out[...], v_out[...] = _adamw_update(g, w_ref[...], m_ref[...], v_ref[...])

    slots = pl.BlockSpec((N_CHIPS, T, cols), lambda i: (0, i, 0))
    spec = pl.BlockSpec((None, T, cols), lambda i: (layer, i, 0))
    filled = [] if so_far is None else list(so_far)
    return pl.pallas_call(
        body, name=name, grid=(rows // T,),
        in_specs=[slots, slots, spec, spec, spec] + [pl.BlockSpec(memory_space=pl.ANY)] * len(filled),
        out_specs=[spec] * 4,
        out_shape=[jax.ShapeDtypeStruct(w.shape, F32)] * 4,
        input_output_aliases={5 + j: j for j in range(len(filled))},
        compiler_params=_cparams(("parallel",)),
    )(mine, other, w, m, v, *filled)


def adamw_small(gs, ws, ms, vs, name):
    n = len(gs)

    def body(*refs):
        for k in range(n):
            g, w, m, v = (refs[j * n + k][...] for j in range(4))
            d_out, m_out, v_out = (refs[(4 + j) * n + k] for j in range(3))
            d_out[...], m_out[...], v_out[...] = _adamw_update(g, w, m, v)

    vmem = pl.BlockSpec(memory_space=pltpu.VMEM)
    outs = pl.pallas_call(
        body, name=name,
        in_specs=[vmem] * (4 * n), out_specs=[vmem] * (3 * n),
        out_shape=[jax.ShapeDtypeStruct(w.shape, F32) for w in ws] * 3,
        compiler_params=_cparams(),
    )(*gs, *ws, *ms, *vs)
    return outs[:n], outs[n:2 * n], outs[2 * n:]


def _shard_of(full, axis, chip):
    size = full.shape[axis] // N_CHIPS
    return lax.slice_in_dim(full, chip * size, (chip + 1) * size, axis=axis)


def kernel(x, mem, norm_mix_g, w_in, sb_q_norm_g, sb_k_norm_g, conv_dw_w, conv_dw_b, conv_ln_g, conv_ln_b, conv_pw2_w, ssm_lam_re, ssm_lam_im, ssm_log_dt, ssm_b_re, ssm_b_im, ssm_c_re, ssm_c_im, ssm_d, ssm_glu_w, branch_norm_g, w_out, norm_xa_g, norm_mem_g, xa_wq, xa_wk, xa_wv, xa_q_norm_g, xa_k_norm_g, xa_wo, norm_ffn_g, ffn_w_in, ffn_w_out, loss_target, m_norm_mix_g, m_w_in, m_sb_q_norm_g, m_sb_k_norm_g, m_conv_dw_w, m_conv_dw_b, m_conv_ln_g, m_conv_ln_b, m_conv_pw2_w, m_ssm_lam_re, m_ssm_lam_im, m_ssm_log_dt, m_ssm_b_re, m_ssm_b_im, m_ssm_c_re, m_ssm_c_im, m_ssm_d, m_ssm_glu_w, m_branch_norm_g, m_w_out, m_norm_xa_g, m_norm_mem_g, m_xa_wq, m_xa_wk, m_xa_wv, m_xa_q_norm_g, m_xa_k_norm_g, m_xa_wo, m_norm_ffn_g, m_ffn_w_in, m_ffn_w_out, v_norm_mix_g, v_w_in, v_sb_q_norm_g, v_sb_k_norm_g, v_conv_dw_w, v_conv_dw_b, v_conv_ln_g, v_conv_ln_b, v_conv_pw2_w, v_ssm_lam_re, v_ssm_lam_im, v_ssm_log_dt, v_ssm_b_re, v_ssm_b_im, v_ssm_c_re, v_ssm_c_im, v_ssm_d, v_ssm_glu_w, v_branch_norm_g, v_w_out, v_norm_xa_g, v_norm_mem_g, v_xa_wq, v_xa_wk, v_xa_wv, v_xa_q_norm_g, v_xa_k_norm_g, v_xa_wo, v_norm_ffn_g, v_ffn_w_in, v_ffn_w_out):
    given = dict(locals())
    w = {n: given[n] for n in WEIGHTS}
    m = {n: given['m_' + n] for n in WEIGHTS}
    v = {n: given['v_' + n] for n in WEIGHTS}

    depth = w_in.shape[0]
    chip = 2 * lax.axis_index("x") + lax.axis_index("y")

    assert depth == DEPTH
    taps_bits = lax.bitcast_convert_type(conv_dw_w, BF16)
    shards = {(n, l): w[n][l].astype(BF16) for n in MATRICES for l in range(depth)}
    first = [shards.pop(key) for key in GATHER_AT['start']]
    gathered = run_ride(gather_ride([n for n, _ in GATHER_AT['start']] + ['taps'], first + [taps_bits]),
                        'gather_first')
    mats = {key: _assemble(key[0], g) for key, g in zip(GATHER_AT['start'], gathered)}
    taps = jnp.concatenate([lax.bitcast_convert_type(gathered[-1][j], F32) for j in range(N_CHIPS)], axis=2)
    local_w = {n: w[n] for n in REPLICATED}
    local_w['conv_dw_w'] = taps

    loss, gx, g_w, mine = local_step(local_w, mats, shards, x[0], mem[0], loss_target[0])
    last = SCATTER_AT['end']
    received = run_ride(scatter_ride([n for n, _ in last], [_cut_for_chips(n, mine[(n, l)]) for n, l in last]),
                        'scatter_last')
    mine.update(zip(last, received))

    keys = [(n, l) for n in MATRICES for l in range(depth)]
    other = dict(zip(keys, swap_cores([mine[key] for key in keys], 'swap_cores')))
    outs = {}
    for n in MATRICES:
        res = None
        for l in range(depth):
            res = adamw_matrix(l, mine[(n, l)], other[(n, l)], w[n], m[n], v[n], res, 'adamw_%s_%d' % (n, l))
        outs['grad_' + n], outs['delta_' + n], outs['new_m_' + n], outs['new_v_' + n] = res

    reduced = allreduce_small(pack([g_w[n] for n in REPLICATED] + [g_w['conv_dw_w'], loss.reshape(1)], F32),
                              'allreduce_small')
    reduced = unpack(reduced, [w[n].shape for n in REPLICATED] + [taps.shape, (1,)])
    total_loss = reduced[-1].reshape(())
    tap_cols = conv_dw_w.shape[2]
    reduced[-2] = lax.dynamic_slice_in_dim(reduced[-2], chip * tap_cols, tap_cols, axis=2)
    small_names = REPLICATED + ['conv_dw_w']
    deltas, new_ms, new_vs = adamw_small(reduced[:-1], [w[n] for n in small_names], [m[n] for n in small_names],
                                         [v[n] for n in small_names], 'adamw_small')
    for k, n in enumerate(small_names):
        outs['grad_' + n], outs['delta_' + n] = reduced[k], deltas[k]
        outs['new_m_' + n], outs['new_v_' + n] = new_ms[k], new_vs[k]
    return (total_loss, gx[None], *[outs['grad_' + n] for n in WEIGHTS], *[outs['delta_' + n] for n in WEIGHTS],
            *[outs['new_m_' + n] for n in WEIGHTS], *[outs['new_v_' + n] for n in WEIGHTS])
```

```python
import functools
import math

import jax
import jax.numpy as jnp
from jax import lax
from jax.experimental import pallas as pl
from jax.experimental.pallas import tpu as pltpu

F32 = jnp.float32
BF16 = jnp.bfloat16
MESH = pl.DeviceIdType.MESH
HIGHEST = lax.Precision.HIGHEST

EPS = 1e-6
LANES = 128
SUBLANES = 8
VMEM_LIMIT = 56 * 1024 * 1024

SB_HEAD_DIM = 64
SB_WIDTH = 512
CONV_CH = 256
CONV_WIDTH = 31
CONV_HALO = 32
SSM_CH = 256
SSM_GROUPS = 16
SSM_GROUP = 16
SSM_STATE = 64
SSM_LANES = SSM_GROUPS * SSM_STATE
XA_HEADS = 4
XA_HEAD_DIM = 256
SB_CUT = 110.0

ADAM_LR = 0.001
ADAM_B1 = 0.9
ADAM_B2 = 0.999
ADAM_EPS = 1e-08
ADAM_WD = 0.01
ADAM_STEP = 10

WEIGHTS = ['norm_mix_g', 'w_in', 'sb_q_norm_g', 'sb_k_norm_g', 'conv_dw_w', 'conv_dw_b', 'conv_ln_g',
           'conv_ln_b', 'conv_pw2_w', 'ssm_lam_re', 'ssm_lam_im', 'ssm_log_dt', 'ssm_b_re', 'ssm_b_im',
           'ssm_c_re', 'ssm_c_im', 'ssm_d', 'ssm_glu_w', 'branch_norm_g', 'w_out', 'norm_xa_g',
           'norm_mem_g', 'xa_wq', 'xa_wk', 'xa_wv', 'xa_q_norm_g', 'xa_k_norm_g', 'xa_wo', 'norm_ffn_g',
           'ffn_w_in', 'ffn_w_out']
SHARD_AXIS = {'w_in': 2, 'conv_dw_w': 2, 'conv_pw2_w': 1, 'ssm_glu_w': 2, 'w_out': 1, 'xa_wq': 1,
              'xa_wk': 1, 'xa_wv': 1, 'xa_wo': 1, 'ffn_w_in': 2, 'ffn_w_out': 1}
MATRICES = [n for n in WEIGHTS if n in SHARD_AXIS and n != 'conv_dw_w']
REPLICATED = [n for n in WEIGHTS if n not in SHARD_AXIS]
N_CHIPS = 4
N_DEV = 8


def _cparams(sem=None, **kw):
    if sem is not None:
        kw['dimension_semantics'] = sem
    return pltpu.CompilerParams(vmem_limit_bytes=VMEM_LIMIT, **kw)


def _pick(n, target):
    best = None
    d = LANES
    while d <= min(n, target):
        if n % d == 0:
            best = d
        d += LANES
    return best if best is not None else n


def _rows_for(n_rows, width):
    t = 512
    while t > 8 and t * width > 768 * 1024:
        t //= 2
    return min(t, n_rows)


def _dg(a, b, ca, cb):
    return lax.dot_general(a.astype(BF16), b.astype(BF16), (((ca,), (cb,)), ((), ())),
                           preferred_element_type=F32)


@jax.custom_vjp
def bdot_nn(a, b):
    return _dg(a, b, 1, 0)


def _bdot_nn_fwd(a, b):
    return _dg(a, b, 1, 0), (a, b)


def _bdot_nn_bwd(res, g):
    a, b = res
    return _dg(g, b, 1, 1), _dg(a, g, 0, 0)


bdot_nn.defvjp(_bdot_nn_fwd, _bdot_nn_bwd)


@jax.custom_vjp
def bdot_nt(a, b):
    return _dg(a, b, 1, 1)


def _bdot_nt_fwd(a, b):
    return _dg(a, b, 1, 1), (a, b)


def _bdot_nt_bwd(res, g):
    a, b = res
    return _dg(g, b, 1, 0), _dg(g, a, 0, 0)


bdot_nt.defvjp(_bdot_nt_fwd, _bdot_nt_bwd)


def mm(a, b, mode, name, out_dtype=F32, add=None, ride=None):
    if mode == 'nn':
        M, K = a.shape
        N = b.shape[1]
    elif mode == 'nt':
        M, K = a.shape
        N = b.shape[0]
    else:
        K, M = a.shape
        N = b.shape[1]
    if mode == 'tn':
        tm, tn, tk = _pick(M, 1536), _pick(N, 2816), _pick(K, 512)
    else:
        tm, tn = _pick(M, 1024), _pick(N, 1536)
        tk = K if K <= 2816 else _pick(K, 1536)
    nk = K // tk
    ca, cb = {'nn': (1, 0), 'nt': (1, 1), 'tn': (0, 0)}[mode]

    grid = (M // tm, N // tn, nk)

    def body(*refs):
        ins, (o_ref,), scratch, riding = _split_refs(ride, 3 if add is not None else 2, 1, refs)
        a_ref, b_ref = ins[:2]
        add_ref = ins[2] if add is not None else None
        i, j, k = pl.program_id(0), pl.program_id(1), pl.program_id(2)
        ride_done = None
        if riding is not None:
            first = jnp.logical_and(i == 0, jnp.logical_and(j == 0, k == 0))
            last = jnp.logical_and(i == grid[0] - 1, jnp.logical_and(j == grid[1] - 1, k == nk - 1))
            ride_done = ride.at_ends(riding, first, last)

        def finish(acc):
            if add_ref is not None:
                acc = acc + add_ref[...]
            o_ref[...] = acc.astype(out_dtype)

        if nk == 1:
            finish(_dg(a_ref[...], b_ref[...], ca, cb))
        else:
            acc_ref, = scratch

            @pl.when(k == 0)
            def _():
                acc_ref[...] = jnp.zeros_like(acc_ref)

            acc_ref[...] += _dg(a_ref[...], b_ref[...], ca, cb)

            @pl.when(k == nk - 1)
            def _():
                finish(acc_ref[...])

        if ride_done is not None:
            ride_done()

    if mode == 'nn':
        a_spec = pl.BlockSpec((tm, tk), lambda i, j, k: (i, k))
        b_spec = pl.BlockSpec((tk, tn), lambda i, j, k: (k, j))
    elif mode == 'nt':
        a_spec = pl.BlockSpec((tm, tk), lambda i, j, k: (i, k))
        b_spec = pl.BlockSpec((tn, tk), lambda i, j, k: (j, k))
    else:
        a_spec = pl.BlockSpec((tk, tm), lambda i, j, k: (k, i))
        b_spec = pl.BlockSpec((tk, tn), lambda i, j, k: (k, j))
    out_spec = pl.BlockSpec((tm, tn), lambda i, j, k: (i, j))
    own_in = [a_spec, b_spec] + ([out_spec] if add is not None else [])
    operands = (a, b) if add is None else (a, b, add)
    scratch = [pltpu.VMEM((tm, tn), F32)] if nk > 1 else []
    if ride is None:
        return pl.pallas_call(
            body, name=name, grid=grid, in_specs=own_in, out_specs=out_spec,
            out_shape=jax.ShapeDtypeStruct((M, N), out_dtype), scratch_shapes=scratch,
            compiler_params=_cparams(("parallel", "parallel", "arbitrary")),
        )(*operands)
    outs = pl.pallas_call(
        body, name=name, grid=grid, in_specs=own_in + ride.in_specs, out_specs=[out_spec] + ride.out_specs,
        out_shape=[jax.ShapeDtypeStruct((M, N), out_dtype)] + ride.out_shape,
        scratch_shapes=scratch + ride.scratch,
        compiler_params=_cparams(("arbitrary", "arbitrary", "arbitrary"), has_side_effects=True),
    )(*operands, *ride.arrays)
    return outs[0], outs[1:]


class Riders:
    def __init__(self, gather=(), scatter=()):
        self.gather_names = [n for n, _ in gather]
        self.shards = tuple(s for _, s in gather)
        self.scatter_names = [n for n, _ in scatter]
        self.slots = tuple(s for _, s in scatter)

    def gather_ride(self, shards):
        return gather_ride(self.gather_names, list(shards)) if shards else None

    def scatter_ride(self, grads):
        if not grads:
            return None
        return scatter_ride(self.scatter_names, [_cut_for_chips(n, g) for n, g in zip(self.scatter_names, grads)])

    def whole_slots(self, slots):
        return tuple(jnp.zeros(_whole_shape(n, s.shape[1:]), BF16) for n, s in zip(self.scatter_names, slots))


NO_RIDERS = Riders()


def linear(x, w, slot, name, residual=None, riders=NO_RIDERS):
    @jax.custom_vjp
    def op(x, w, slot, residual, shards, slots):
        ride = riders.gather_ride(shards)
        y = mm(x, w, 'nn', name + '_fwd', add=residual, ride=ride)
        y, gathered = y if ride is not None else (y, ())
        return y, tuple(gathered), riders.whole_slots(slots)

    def op_fwd(x, w, slot, residual, shards, slots):
        return op(x, w, slot, residual, shards, slots), (x, w, shards)

    def op_bwd(res, cts):
        x, w, shards = res
        g, _, slot_grads = cts
        ride = riders.scatter_ride(slot_grads)
        dx = mm(g, w, 'nt', name + '_dx', ride=ride)
        dx, received = dx if ride is not None else (dx, ())
        return (dx, jnp.zeros_like(w), mm(x, g, 'tn', name + '_dw', BF16), None if residual is None else g,
                tuple(jnp.zeros_like(s) for s in shards), tuple(received))

    op.defvjp(op_fwd, op_bwd)
    return op(x, w, slot, residual, riders.shards, riders.slots)


def _rowwise_calls(f, rows, params, consts, out_widths, name, need_row_grad=None, block_rows=None, carry=(),
                   out_dtype=F32):
    nr, npar, nc, nout = len(rows), len(params), len(consts), len(out_widths)
    carry = tuple(carry)
    L = rows[0].shape[0]
    widths = [r.shape[1] for r in rows]
    T = block_rows or _rows_for(L, max(widths + list(out_widths)))
    n = L // T
    need = list(need_row_grad) if need_row_grad is not None else [True] * nr
    pshapes = [p.shape for p in params]
    cshapes = [c.shape for c in consts]

    row_specs = [pl.BlockSpec((T, w), lambda i: (i, 0)) for w in widths]
    par_specs = [pl.BlockSpec(s, lambda i: (0, 0)) for s in pshapes]
    con_specs = [pl.BlockSpec(s, lambda i: (0, 0)) for s in cshapes]
    out_specs = [pl.BlockSpec((T, w), lambda i: (i, 0)) for w in out_widths]

    def fwd_call(rows, params, consts):
        def body(*refs):
            ins = [r[...] for r in refs[:nr + npar + nc]]
            outs = f(*ins)
            for o_ref, val in zip(refs[nr + npar + nc:], outs):
                o_ref[...] = val.astype(out_dtype)

        return pl.pallas_call(
            body, name=name + '_fwd', grid=(n,),
            in_specs=row_specs + par_specs + con_specs, out_specs=out_specs,
            out_shape=[jax.ShapeDtypeStruct((L, w), out_dtype) for w in out_widths],
            compiler_params=_cparams(("parallel",)),
        )(*rows, *params, *consts)

    def bwd_call(rows, params, consts, cts, carried):
        grad_rows = [k for k in range(nr) if need[k]]
        n_in = nr + npar + nc + nout

        def body(*refs):
            i = pl.program_id(0)
            rv = [r[...] for r in refs[:nr]]
            pv = [r[...] for r in refs[nr:nr + npar]]
            cv = [r[...] for r in refs[nr + npar:nr + npar + nc]]
            ctv = tuple(r[...] for r in refs[nr + npar + nc:n_in])
            carried_refs = dict(zip(carry, refs[n_in:n_in + len(carry)]))
            orefs = refs[n_in + len(carry):]
            _, vjp = jax.vjp(lambda *rp: tuple(f(*rp, *cv)), *rv, *pv)
            g = vjp(ctv)
            for slot, k in enumerate(grad_rows):
                orefs[slot][...] = g[k] + carried_refs[k][...] if k in carried_refs else g[k]

            @pl.when(i == 0)
            def _():
                for k in range(npar):
                    orefs[len(grad_rows) + k][...] = jnp.zeros(pshapes[k], F32)

            for k in range(npar):
                orefs[len(grad_rows) + k][...] += g[nr + k]

        outs = pl.pallas_call(
            body, name=name + '_bwd', grid=(n,),
            in_specs=row_specs + par_specs + con_specs + out_specs + [row_specs[k] for k in carry],
            out_specs=[row_specs[k] for k in grad_rows] + par_specs,
            out_shape=[jax.ShapeDtypeStruct((L, widths[k]), F32) for k in grad_rows]
            + [jax.ShapeDtypeStruct(s, F32) for s in pshapes],
            compiler_params=_cparams(("arbitrary",)),
        )(*rows, *params, *consts, *cts, *carried)
        drows = []
        slot = 0
        for k in range(nr):
            if need[k]:
                drows.append(outs[slot])
                slot += 1
            else:
                drows.append(jnp.zeros_like(rows[k]))
        return tuple(drows), tuple(outs[len(grad_rows):])

    return fwd_call, bwd_call


def rowwise(f, rows, params, consts, out_widths, name, need_row_grad=None, block_rows=None, carry=()):
    fwd_call, bwd_call = _rowwise_calls(f, rows, params, consts, out_widths, name, need_row_grad, block_rows, carry)
    nout = len(out_widths)

    @jax.custom_vjp
    def op(rows, params, consts):
        return tuple(fwd_call(rows, params, consts)) + tuple(rows[k] for k in carry)

    def op_fwd(rows, params, consts):
        return op(rows, params, consts), (rows, params, consts)

    def op_bwd(res, cts):
        rows, params, consts = res
        drows, dparams = bwd_call(rows, params, consts, cts[:nout], cts[nout:])
        return drows, dparams, tuple(jnp.zeros_like(c) for c in consts)

    op.defvjp(op_fwd, op_bwd)
    return op(tuple(rows), tuple(params), tuple(consts))


def rowwise_linear(f, rows, params, consts, w, slot, name, residual=None, riders=NO_RIDERS, carry=(),
                   block_rows=None):
    width = w.shape[0]
    fwd_call, bwd_call = _rowwise_calls(lambda *a: (f(*a),), rows, params, consts, [width], name, None,
                                        block_rows, carry, BF16)

    @jax.custom_vjp
    def op(rows, params, consts, w, slot, residual, shards, slots):
        h, = fwd_call(rows, params, consts)
        ride = riders.gather_ride(shards)
        y = mm(h, w, 'nn', name + '_mm', add=residual, ride=ride)
        y, gathered = y if ride is not None else (y, ())
        return y, tuple(rows[k] for k in carry), tuple(gathered), riders.whole_slots(slots)

    def op_fwd(rows, params, consts, w, slot, residual, shards, slots):
        h, = fwd_call(rows, params, consts)
        ride = riders.gather_ride(shards)
        y = mm(h, w, 'nn', name + '_mm', add=residual, ride=ride)
        y, gathered = y if ride is not None else (y, ())
        out = (y, tuple(rows[k] for k in carry), tuple(gathered), riders.whole_slots(slots))
        return out, (rows, params, consts, h, w, shards)

    def op_bwd(res, cts):
        rows, params, consts, h, w, shards = res
        g, carried, _, slot_grads = cts
        ride = riders.scatter_ride(slot_grads)
        dh = mm(g, w, 'nt', name + '_dx', ride=ride)
        dh, received = dh if ride is not None else (dh, ())
        drows, dparams = bwd_call(rows, params, consts, (dh,), carried)
        return (drows, dparams, tuple(jnp.zeros_like(c) for c in consts), jnp.zeros_like(w),
                mm(h, g, 'tn', name + '_dw', BF16), None if residual is None else g,
                tuple(jnp.zeros_like(s) for s in shards), tuple(received))

    op.defvjp(op_fwd, op_bwd)
    return op(tuple(rows), tuple(params), tuple(consts), w, slot, residual, riders.shards, riders.slots)


def _rms(x, g):
    return x * lax.rsqrt(jnp.mean(x * x, axis=-1, keepdims=True) + EPS) * g


def rmsnorm(x, g, name, carry=False):
    out = rowwise(lambda x, g: (_rms(x, g),), [x], [g.reshape(1, -1)], [], [x.shape[1]], name,
                  carry=(0,) if carry else ())
    return out if carry else out[0]


def _split3(x):
    hi = x.astype(BF16)
    r = x - hi.astype(F32)
    mid = r.astype(BF16)
    return hi, mid, (r - mid.astype(F32)).astype(BF16)


@jax.custom_vjp
def select_mm(x, sel):
    return sum(_dg(t, sel, 1, 0) for t in _split3(x))


def _select_mm_fwd(x, sel):
    return select_mm(x, sel), sel


def _select_mm_bwd(sel, g):
    return sum(_dg(t, sel, 1, 1) for t in _split3(g)), jnp.zeros_like(sel)


select_mm.defvjp(_select_mm_fwd, _select_mm_bwd)


def groupnorm(x, g, group, name):
    width = x.shape[1]
    g_full = jnp.tile(g.reshape(1, group), (1, width // group))
    if group % LANES == 0:
        def f(x, g_full):
            outs = []
            for lo in range(0, width, group):
                xs = x[:, lo:lo + group]
                outs.append(_rms(xs, g_full[:, lo:lo + group]))
            return (jnp.concatenate(outs, axis=-1),)

        return rowwise(f, [x], [g_full], [], [width], name)[0]

    gid = jnp.arange(width) // group
    sel = (gid[:, None] == jnp.arange(LANES)[None, :]).astype(BF16)

    def f(x, g_full, sel, sel_t):
        ms = select_mm(x * x, sel) * (1.0 / group)
        inv = select_mm(lax.rsqrt(ms + EPS), sel_t)
        return (x * inv * g_full,)

    return rowwise(f, [x], [g_full], [sel, sel.T], [width], name)[0]


def glu(x, name):
    half = x.shape[1] // 2

    def f(x):
        return (x[:, :half] * jax.nn.sigmoid(x[:, half:]),)

    return rowwise(f, [x], [], [], [half], name)[0]


def swiglu_block(x):
    half = x.shape[1] // 2
    gate = x[:, :half]
    return gate * jax.nn.sigmoid(gate) * x[:, half:]


def ln_silu_block(x, g, b):
    mu = jnp.mean(x, axis=-1, keepdims=True)
    xc = x - mu
    var = jnp.mean(xc * xc, axis=-1, keepdims=True)
    y = xc * lax.rsqrt(var + EPS) * g + b
    return y * jax.nn.sigmoid(y)


def branch_norms_block(a, b, c, g):
    w1, w2 = a.shape[1], b.shape[1]
    return jnp.concatenate([_rms(a, g[:, :w1]), _rms(b, g[:, w1:w1 + w2]), _rms(c, g[:, w1 + w2:])], axis=-1)


def xa_core_block(q, k, v):
    scale = XA_HEAD_DIM ** -0.5
    outs = []
    for h in range(XA_HEADS):
        sl = slice(h * XA_HEAD_DIM, (h + 1) * XA_HEAD_DIM)
        s = bdot_nt(q[:, sl], k[:, sl]) * scale
        m = lax.stop_gradient(jnp.max(s, axis=-1, keepdims=True))
        e = jnp.exp(s - m)
        p = e / jnp.sum(e, axis=-1, keepdims=True)
        outs.append(bdot_nn(p, v[:, sl]))
    return jnp.concatenate(outs, axis=-1)


def loss_rows(y, target, name):
    def f(y, t):
        d = y - t
        return (0.5 * jnp.mean(d * d, axis=-1, keepdims=True),)

    return rowwise(f, [y, target], [], [], [1], name, need_row_grad=[True, False])[0]


def _hilo(x, ones_bf16):
    hi = x.astype(BF16)
    lo = (x - hi.astype(F32)).astype(BF16)
    return _dg(hi, ones_bf16, 1, 0) + _dg(lo, ones_bf16, 1, 0)


def _sb_block(qh, kb, c, valid, strict_upper):
    z = _dg(qh, kb, 1, 1)
    a = jnp.minimum(z, 0.0) - jnp.log(1.0 + jnp.exp(-jnp.abs(z)))
    b = jnp.where(valid, a - z, 0.0)
    s = _hilo(b, strict_upper) + c
    w = jnp.where(valid, jnp.exp(a + s), 0.0)
    return a, b, w


def _sb_masks(T):
    row = lax.broadcasted_iota(jnp.int32, (T, T), 0)
    col = lax.broadcasted_iota(jnp.int32, (T, T), 1)
    return col < row, (row > col).astype(BF16), (row >= col).astype(BF16)


HEADS_PER_BLOCK = LANES // SB_HEAD_DIM


def _head_mask(h):
    lane = lax.broadcasted_iota(jnp.int32, (1, LANES), 1)
    return (lane // SB_HEAD_DIM == h).astype(F32)


def _max_all(columns):
    m = columns[0]
    for c in columns[1:]:
        m = jnp.maximum(m, c)
    return jnp.max(m)


def _ride_call(body, name, grid, in_specs, out_specs, out_shape, operands, ride, semantics):
    if ride is None:
        outs = pl.pallas_call(body, name=name, grid=grid, in_specs=in_specs, out_specs=out_specs,
                              out_shape=out_shape, compiler_params=_cparams(semantics))(*operands)
        return outs, ()
    outs = pl.pallas_call(
        body, name=name, grid=grid, in_specs=in_specs + ride.in_specs, out_specs=out_specs + ride.out_specs,
        out_shape=out_shape + ride.out_shape, scratch_shapes=ride.scratch,
        compiler_params=_cparams(("arbitrary",) * len(grid), has_side_effects=True),
    )(*operands, *ride.arrays)
    return outs[:len(out_shape)], outs[len(out_shape):]


def _ride_ends(ride, riding, grid):
    if riding is None:
        return lambda: None
    first, last = None, None
    for axis, size in enumerate(grid):
        at0, at1 = pl.program_id(axis) == 0, pl.program_id(axis) == size - 1
        first = at0 if first is None else jnp.logical_and(first, at0)
        last = at1 if last is None else jnp.logical_and(last, at1)
    return ride.at_ends(riding, first, last)


def _sb_fwd_call(q, k, v, T, name, ride=None):
    L, W = q.shape
    scale = SB_HEAD_DIM ** -0.5
    grid = (W // LANES, L // T)

    def body(*refs):
        (q_ref, k_ref, v_ref), (o_ref,), _, riding = _split_refs(ride, 3, 1, refs)
        ride_done = _ride_ends(ride, riding, grid)
        i = pl.program_id(1)
        causal, strict_upper, _ = _sb_masks(T)
        q2 = q_ref[...] * scale
        masks = [_head_mask(h) for h in range(HEADS_PER_BLOCK)]
        qs = [(q2 * hm).astype(BF16) for hm in masks]
        zero = jnp.zeros((T, 1), F32)

        def cond(state):
            j, cs, _ = state
            return jnp.logical_and(j >= 0, _max_all(cs) > -SB_CUT)

        def step(state):
            j, cs, acc = state
            r0 = pl.multiple_of(j * T, T)
            valid = jnp.logical_or(causal, j != i)
            kb = k_ref[pl.ds(r0, T), :].astype(BF16)
            vb = v_ref[pl.ds(r0, T), :]
            new_cs = []
            for hm, qh, c in zip(masks, qs, cs):
                _, b, w = _sb_block(qh, kb, c, valid, strict_upper)
                vh = (vb * hm).astype(BF16)
                w_hi = w.astype(BF16)
                w_lo = (w - w_hi.astype(F32)).astype(BF16)
                acc = acc + _dg(w_hi, vh, 1, 0) + _dg(w_lo, vh, 1, 0)
                new_cs.append(c + jnp.sum(b, axis=1, keepdims=True))
            return j - 1, tuple(new_cs), acc

        _, _, acc = lax.while_loop(cond, step, (i, (zero,) * HEADS_PER_BLOCK, jnp.zeros((T, LANES), F32)))
        o_ref[...] = acc
        ride_done()

    (o,), rode = _ride_call(
        body, name, grid,
        [pl.BlockSpec((T, LANES), lambda p, i: (i, p)),
         pl.BlockSpec((L, LANES), lambda p, i: (0, p)),
         pl.BlockSpec((L, LANES), lambda p, i: (0, p))],
        [pl.BlockSpec((T, LANES), lambda p, i: (i, p))], [jax.ShapeDtypeStruct((L, W), F32)],
        (q, k, v), ride, ("parallel", "parallel"))
    return o, rode


def _sb_bwd_call(q, k, v, o, do, T, name, ride=None):
    L, W = q.shape
    scale = SB_HEAD_DIM ** -0.5
    grid = (W // LANES, L // T)

    def body(*refs):
        (q_ref, k_ref, v_ref, o_ref, do_ref), (dq_ref, dk_ref, dv_ref), _, riding = _split_refs(ride, 5, 3, refs)
        ride_done = _ride_ends(ride, riding, grid)
        i = pl.program_id(1)

        @pl.when(i == 0)
        def _():
            dk_ref[...] = jnp.zeros_like(dk_ref)
            dv_ref[...] = jnp.zeros_like(dv_ref)

        causal, strict_upper, upper = _sb_masks(T)
        q2 = q_ref[...] * scale
        do2 = do_ref[...]
        o2 = o_ref[...]
        masks = [_head_mask(h) for h in range(HEADS_PER_BLOCK)]
        qs = [(q2 * hm).astype(BF16) for hm in masks]
        dos = [(do2 * hm).astype(BF16) for hm in masks]
        totals = [jnp.sum(doh.astype(F32) * o2, axis=1, keepdims=True) for doh in dos]
        zero = jnp.zeros((T, 1), F32)

        def cond(state):
            j, cs, _, _ = state
            return jnp.logical_and(j >= 0, _max_all(cs) > -SB_CUT)

        def step(state):
            j, cs, rs, dq = state
            r0 = pl.multiple_of(j * T, T)
            valid = jnp.logical_or(causal, j != i)
            kf = k_ref[pl.ds(r0, T), :]
            kb = kf.astype(BF16)
            vb = v_ref[pl.ds(r0, T), :].astype(BF16)
            new_cs, new_rs = [], []
            dk = jnp.zeros((T, LANES), F32)
            dv = jnp.zeros((T, LANES), F32)
            for hm, qh, doh, total, c, r in zip(masks, qs, dos, totals, cs, rs):
                a, b, w = _sb_block(qh, kb, c, valid, strict_upper)
                e = _dg(doh, vb, 1, 1) * w
                before = total - (_hilo(e, upper) + r)
                dz = jnp.where(valid, e * jnp.exp(b) - before * jnp.exp(a), 0.0).astype(BF16)
                dq = dq + _dg(dz, kf * hm, 1, 0)
                dk = dk + _dg(dz, qh, 0, 0)
                dv = dv + _dg(w, doh, 0, 0)
                new_cs.append(c + jnp.sum(b, axis=1, keepdims=True))
                new_rs.append(r + jnp.sum(e, axis=1, keepdims=True))
            dk_ref[pl.ds(r0, T), :] += dk
            dv_ref[pl.ds(r0, T), :] += dv
            return j - 1, tuple(new_cs), tuple(new_rs), dq

        init = (i, (zero,) * HEADS_PER_BLOCK, (zero,) * HEADS_PER_BLOCK, jnp.zeros((T, LANES), F32))
        dq = lax.while_loop(cond, step, init)[3]
        dq_ref[...] = dq * scale
        ride_done()

    blk = pl.BlockSpec((T, LANES), lambda p, i: (i, p))
    full = pl.BlockSpec((L, LANES), lambda p, i: (0, p))
    return _ride_call(body, name, grid, [blk, full, full, blk, blk], [blk, full, full],
                      [jax.ShapeDtypeStruct((L, W), F32)] * 3, (q, k, v, o, do), ride, ("parallel", "arbitrary"))


def sb_attention(q, k, v, name, riders=NO_RIDERS):
    T = min(256, q.shape[0])

    @jax.custom_vjp
    def op(q, k, v, shards, slots):
        o, gathered = _sb_fwd_call(q, k, v, T, name + '_fwd', riders.gather_ride(shards))
        return o, tuple(gathered), riders.whole_slots(slots)

    def op_fwd(q, k, v, shards, slots):
        out = op(q, k, v, shards, slots)
        return out, (q, k, v, out[0], shards)

    def op_bwd(res, cts):
        q, k, v, o, shards = res
        do, _, slot_grads = cts
        grads, received = _sb_bwd_call(q, k, v, o, do, T, name + '_bwd', riders.scatter_ride(slot_grads))
        return (*grads, tuple(jnp.zeros_like(s) for s in shards), tuple(received))

    op.defvjp(op_fwd, op_bwd)
    return op(q, k, v, riders.shards, riders.slots)


def _dwconv_fwd_call(x, w, b, T, name):
    L, C = x.shape
    per = T // CONV_HALO
    lead = CONV_HALO - (CONV_WIDTH - 1)

    def body(x_ref, halo_ref, w_ref, b_ref, o_ref, buf):
        i = pl.program_id(0)
        buf[0:CONV_HALO, :] = jnp.where(i > 0, halo_ref[...], 0.0)
        buf[CONV_HALO:CONV_HALO + T, :] = x_ref[...]
        acc = jnp.zeros((T, C), F32) + b_ref[...]
        for j in range(CONV_WIDTH):
            acc = acc + w_ref[j:j + 1, :] * buf[lead + j:lead + j + T, :]
        o_ref[...] = acc

    return pl.pallas_call(
        body, name=name, grid=(L // T,),
        in_specs=[pl.BlockSpec((T, C), lambda i: (i, 0)),
                  pl.BlockSpec((CONV_HALO, C), lambda i: (jnp.maximum(i * per - 1, 0), 0)),
                  pl.BlockSpec(w.shape, lambda i: (0, 0)),
                  pl.BlockSpec(b.shape, lambda i: (0, 0))],
        out_specs=pl.BlockSpec((T, C), lambda i: (i, 0)),
        out_shape=jax.ShapeDtypeStruct((L, C), F32),
        scratch_shapes=[pltpu.VMEM((T + CONV_HALO, C), F32)],
        compiler_params=_cparams(("parallel",)),
    )(x, x, w, b)


def _dwconv_bwd_call(x, w, g, T, name):
    L, C = x.shape
    per = T // CONV_HALO
    n = L // T
    last_halo = L // CONV_HALO - 1
    lead = CONV_HALO - (CONV_WIDTH - 1)

    def body(x_ref, xh_ref, g_ref, gh_ref, w_ref, dx_ref, dw_ref, db_ref, bufx, bufg):
        i = pl.program_id(0)
        bufx[0:CONV_HALO, :] = jnp.where(i > 0, xh_ref[...], 0.0)
        bufx[CONV_HALO:CONV_HALO + T, :] = x_ref[...]
        gm = g_ref[...]
        bufg[0:T, :] = gm
        bufg[T:T + CONV_HALO, :] = jnp.where(i < n - 1, gh_ref[...], 0.0)
        acc = jnp.zeros((T, C), F32)
        for j in range(CONV_WIDTH):
            off = CONV_WIDTH - 1 - j
            acc = acc + w_ref[j:j + 1, :] * bufg[off:off + T, :]
        dx_ref[...] = acc

        @pl.when(i == 0)
        def _():
            dw_ref[...] = jnp.zeros_like(dw_ref)
            db_ref[...] = jnp.zeros_like(db_ref)

        for j in range(CONV_WIDTH):
            dw_ref[j:j + 1, :] += jnp.sum(gm * bufx[lead + j:lead + j + T, :], axis=0, keepdims=True)
        db_ref[...] += jnp.sum(gm, axis=0, keepdims=True)

    return pl.pallas_call(
        body, name=name, grid=(n,),
        in_specs=[pl.BlockSpec((T, C), lambda i: (i, 0)),
                  pl.BlockSpec((CONV_HALO, C), lambda i: (jnp.maximum(i * per - 1, 0), 0)),
                  pl.BlockSpec((T, C), lambda i: (i, 0)),
                  pl.BlockSpec((CONV_HALO, C), lambda i: (jnp.minimum((i + 1) * per, last_halo), 0)),
                  pl.BlockSpec(w.shape, lambda i: (0, 0))],
        out_specs=[pl.BlockSpec((T, C), lambda i: (i, 0)),
                   pl.BlockSpec(w.shape, lambda i: (0, 0)),
                   pl.BlockSpec((1, C), lambda i: (0, 0))],
        out_shape=[jax.ShapeDtypeStruct((L, C), F32), jax.ShapeDtypeStruct(w.shape, F32),
                   jax.ShapeDtypeStruct((1, C), F32)],
        scratch_shapes=[pltpu.VMEM((T + CONV_HALO, C), F32), pltpu.VMEM((T + CONV_HALO, C), F32)],
        compiler_params=_cparams(("arbitrary",)),
    )(x, x, g, g, w)


def dwconv(x, w, b, name):
    T = min(512, x.shape[0])

    @jax.custom_vjp
    def op(x, w, b):
        return _dwconv_fwd_call(x, w, b, T, name + '_fwd')

    def op_fwd(x, w, b):
        return _dwconv_fwd_call(x, w, b, T, name + '_fwd'), (x, w)

    def op_bwd(res, g):
        x, w = res
        return tuple(_dwconv_bwd_call(x, w, g, T, name + '_bwd'))

    op.defvjp(op_fwd, op_bwd)
    return op(x, w, b)


def _ssm_fwd_call(u, ar, ai, bbr, bbi, cr, ci, d, T, name):
    L, C = u.shape
    S = SSM_LANES

    def body(u_ref, ar_ref, ai_ref, bbr_ref, bbi_ref, cr_ref, ci_ref, d_ref,
             y_ref, xr_ref, xi_ref, st_r, st_i, in_r, in_i, out_r, out_i):
        i = pl.program_id(0)

        @pl.when(i == 0)
        def _():
            st_r[...] = jnp.zeros_like(st_r)
            st_i[...] = jnp.zeros_like(st_i)

        u_blk = u_ref[...]
        xr_ref[...] = _dg(u_blk, bbr_ref[...], 1, 0)
        xi_ref[...] = _dg(u_blk, bbi_ref[...], 1, 0)
        a_r, a_i = ar_ref[...], ai_ref[...]

        def tile(t, carry):
            sr, si = carry
            r0 = pl.multiple_of(t * SUBLANES, SUBLANES)
            in_r[...] = xr_ref[pl.ds(r0, SUBLANES), :]
            in_i[...] = xi_ref[pl.ds(r0, SUBLANES), :]
            for r in range(SUBLANES):
                nr = a_r * sr - a_i * si + in_r[r:r + 1, :]
                ni = a_r * si + a_i * sr + in_i[r:r + 1, :]
                sr, si = nr, ni
                out_r[r:r + 1, :] = sr
                out_i[r:r + 1, :] = si
            xr_ref[pl.ds(r0, SUBLANES), :] = out_r[...]
            xi_ref[pl.ds(r0, SUBLANES), :] = out_i[...]
            return sr, si

        sr, si = lax.fori_loop(0, T // SUBLANES, tile, (st_r[0:1, :], st_i[0:1, :]))
        st_r[0:1, :] = sr
        st_i[0:1, :] = si
        y_ref[...] = (_dg(xr_ref[...], cr_ref[...], 1, 0) - _dg(xi_ref[...], ci_ref[...], 1, 0)
                      + d_ref[...] * u_blk)

    full = lambda a: pl.BlockSpec(a.shape, lambda i: (0, 0))
    return pl.pallas_call(
        body, name=name, grid=(L // T,),
        in_specs=[pl.BlockSpec((T, C), lambda i: (i, 0))] + [full(a) for a in (ar, ai, bbr, bbi, cr, ci, d)],
        out_specs=[pl.BlockSpec((T, C), lambda i: (i, 0)), pl.BlockSpec((T, S), lambda i: (i, 0)),
                   pl.BlockSpec((T, S), lambda i: (i, 0))],
        out_shape=[jax.ShapeDtypeStruct((L, C), F32), jax.ShapeDtypeStruct((L, S), F32),
                   jax.ShapeDtypeStruct((L, S), F32)],
        scratch_shapes=[pltpu.VMEM((SUBLANES, S), F32)] * 6,
        compiler_params=_cparams(("arbitrary",)),
    )(u, ar, ai, bbr, bbi, cr, ci, d)


def _ssm_bwd_call(u, xr, xi, dy, ar, ai, bbr, bbi, cr, ci, d, T, name):
    L, C = u.shape
    S = SSM_LANES
    n = L // T
    per = T // SUBLANES

    def body(u_ref, xr_ref, xi_ref, hr_ref, hi_ref, dy_ref, ar_ref, ai_ref, bbr_ref, bbi_ref, cr_ref, ci_ref,
             d_ref, du_ref, dar_ref, dai_ref, dbr_ref, dbi_ref, dcr_ref, dci_ref, dd_ref,
             lam_r, lam_i, prev_r, prev_i, st_r, st_i, in_r, in_i, out_r, out_i):
        i = pl.program_id(0)
        chunk = n - 1 - i

        @pl.when(i == 0)
        def _():
            st_r[...] = jnp.zeros_like(st_r)
            st_i[...] = jnp.zeros_like(st_i)
            for ref in (dar_ref, dai_ref, dbr_ref, dbi_ref, dcr_ref, dci_ref, dd_ref):
                ref[...] = jnp.zeros_like(ref)

        dy_blk = dy_ref[...]
        u_blk = u_ref[...]
        lam_r[...] = _dg(dy_blk, cr_ref[...], 1, 1)
        lam_i[...] = -_dg(dy_blk, ci_ref[...], 1, 1)
        dcr_ref[...] += _dg(xr_ref[...], dy_blk, 0, 0)
        dci_ref[...] -= _dg(xi_ref[...], dy_blk, 0, 0)
        a_r, a_i = ar_ref[...], ai_ref[...]

        def tile(k, carry):
            lr, li = carry
            r0 = pl.multiple_of((per - 1 - k) * SUBLANES, SUBLANES)
            in_r[...] = lam_r[pl.ds(r0, SUBLANES), :]
            in_i[...] = lam_i[pl.ds(r0, SUBLANES), :]
            for r in range(SUBLANES - 1, -1, -1):
                nr = in_r[r:r + 1, :] + a_r * lr + a_i * li
                ni = in_i[r:r + 1, :] + a_r * li - a_i * lr
                lr, li = nr, ni
                out_r[r:r + 1, :] = lr
                out_i[r:r + 1, :] = li
            lam_r[pl.ds(r0, SUBLANES), :] = out_r[...]
            lam_i[pl.ds(r0, SUBLANES), :] = out_i[...]
            return lr, li

        lr, li = lax.fori_loop(0, per, tile, (st_r[0:1, :], st_i[0:1, :]))
        st_r[0:1, :] = lr
        st_i[0:1, :] = li

        l_r, l_i = lam_r[...], lam_i[...]
        du_ref[...] = _dg(l_r, bbr_ref[...], 1, 1) + _dg(l_i, bbi_ref[...], 1, 1) + d_ref[...] * dy_blk
        dbr_ref[...] += _dg(u_blk, l_r, 0, 0)
        dbi_ref[...] += _dg(u_blk, l_i, 0, 0)
        dd_ref[...] += jnp.sum(dy_blk * u_blk, axis=0, keepdims=True)

        prev_r[0:SUBLANES, :] = jnp.where(chunk > 0, hr_ref[...], 0.0)
        prev_i[0:SUBLANES, :] = jnp.where(chunk > 0, hi_ref[...], 0.0)
        prev_r[SUBLANES:SUBLANES + T, :] = xr_ref[...]
        prev_i[SUBLANES:SUBLANES + T, :] = xi_ref[...]
        p_r = prev_r[SUBLANES - 1:SUBLANES - 1 + T, :]
        p_i = prev_i[SUBLANES - 1:SUBLANES - 1 + T, :]
        dar_ref[...] += jnp.sum(l_r * p_r + l_i * p_i, axis=0, keepdims=True)
        dai_ref[...] += jnp.sum(l_i * p_r - l_r * p_i, axis=0, keepdims=True)

    rev = lambda w: pl.BlockSpec((T, w), lambda i: (n - 1 - i, 0))
    halo = pl.BlockSpec((SUBLANES, S), lambda i: (jnp.maximum((n - 1 - i) * per - 1, 0), 0))
    full = lambda a: pl.BlockSpec(a.shape, lambda i: (0, 0))
    params = (ar, ai, bbr, bbi, cr, ci, d)
    return pl.pallas_call(
        body, name=name, grid=(n,),
        in_specs=[rev(C), rev(S), rev(S), halo, halo, rev(C)] + [full(a) for a in params],
        out_specs=[rev(C)] + [full(a) for a in params],
        out_shape=[jax.ShapeDtypeStruct((L, C), F32)] + [jax.ShapeDtypeStruct(a.shape, F32) for a in params],
        scratch_shapes=[pltpu.VMEM((T, S), F32), pltpu.VMEM((T, S), F32),
                        pltpu.VMEM((T + SUBLANES, S), F32), pltpu.VMEM((T + SUBLANES, S), F32)]
        + [pltpu.VMEM((SUBLANES, S), F32)] * 6,
        compiler_params=_cparams(("arbitrary",)),
    )(u, xr, xi, xr, xi, dy, *params)


def ssm_core(u, ar, ai, bbr, bbi, cr, ci, d, name):
    T = min(256, u.shape[0])

    @jax.custom_vjp
    def op(u, ar, ai, bbr, bbi, cr, ci, d):
        return _ssm_fwd_call(u, ar, ai, bbr, bbi, cr, ci, d, T, name + '_fwd')[0]

    def op_fwd(u, ar, ai, bbr, bbi, cr, ci, d):
        y, xr, xi = _ssm_fwd_call(u, ar, ai, bbr, bbi, cr, ci, d, T, name + '_fwd')
        return y, (u, xr, xi, ar, ai, bbr, bbi, cr, ci, d)

    def op_bwd(res, dy):
        u, xr, xi, ar, ai, bbr, bbi, cr, ci, d = res
        return tuple(_ssm_bwd_call(u, xr, xi, dy, ar, ai, bbr, bbi, cr, ci, d, T, name + '_bwd'))

    op.defvjp(op_fwd, op_bwd)
    return op(u, ar, ai, bbr, bbi, cr, ci, d)


@jax.custom_vjp
def _block_diag(blocks):
    G, R, Cc = blocks.shape
    eye = jnp.eye(G, dtype=blocks.dtype)
    return (blocks[:, :, None, :] * eye[:, None, :, None]).reshape(G * R, G * Cc)


def _block_diag_fwd(blocks):
    return _block_diag(blocks), blocks.shape


def _block_diag_bwd(shape, g):
    G, R, Cc = shape
    return (jnp.stack([g[k * R:(k + 1) * R, k * Cc:(k + 1) * Cc] for k in range(G)]),)


_block_diag.defvjp(_block_diag_fwd, _block_diag_bwd)


def ssm_discretise(lam_re, lam_im, log_dt, b_re, b_im, c_re, c_im):
    dt = jnp.exp(log_dt)[:, None]
    mag = jnp.exp(lam_re * dt)
    ar, ai = mag * jnp.cos(lam_im * dt), mag * jnp.sin(lam_im * dt)
    den = lam_re * lam_re + lam_im * lam_im
    fr = ((ar - 1.0) * lam_re + ai * lam_im) / den
    fi = (ai * lam_re - (ar - 1.0) * lam_im) / den
    bbr = fr[..., None] * b_re - fi[..., None] * b_im
    bbi = fr[..., None] * b_im + fi[..., None] * b_re
    return (ar.reshape(1, SSM_LANES), ai.reshape(1, SSM_LANES),
            _block_diag(bbr.transpose(0, 2, 1)), _block_diag(bbi.transpose(0, 2, 1)),
            _block_diag(c_re.transpose(0, 2, 1)), _block_diag(c_im.transpose(0, 2, 1)))


def split_columns(p, bounds):
    @jax.custom_vjp
    def op(p):
        return tuple(p[:, lo:hi] for lo, hi in zip(bounds[:-1], bounds[1:]))

    def op_fwd(p):
        return op(p), None

    def op_bwd(_, gs):
        return (jnp.concatenate(gs, axis=1),)

    op.defvjp(op_fwd, op_bwd)
    return op(p)


DEPTH = 2
EARLY = ['w_in', 'conv_pw2_w', 'ssm_glu_w']
LATE = [n for n in MATRICES if n not in EARLY]
FFN = ['ffn_w_in', 'ffn_w_out']
GATHER_AT = {
    'start': [('w_in', 0)],
    'sb0': [(n, 0) for n in MATRICES if n != 'w_in'],
    'ffn0': [(n, 1) for n in MATRICES if n not in FFN],
    'sb1': [(n, 1) for n in FFN],
}
SCATTER_AT = {
    'sb1': [(n, 1) for n in LATE],
    'ffn0': [(n, 1) for n in EARLY],
    'sb0': [(n, 0) for n in LATE],
    'end': [(n, 0) for n in EARLY],
}


def _assemble(name, gathered):
    if name not in SLOTTED:
        return gathered
    return jnp.concatenate([gathered[j] for j in range(N_CHIPS)], axis=SHARD_AXIS[name] - 1)


def local_loss(slots, w, mats, shards, x, mem, target):
    mats = dict(mats)
    slot = {key: slots[key] for key in SCATTER_AT['end']}
    s1, s2, s3 = SB_WIDTH, 2 * SB_WIDTH, 3 * SB_WIDTH
    s4 = s3 + 2 * CONV_CH

    def riders_at(host):
        return Riders(gather=[(n, shards[(n, l)]) for n, l in GATHER_AT[host]],
                      scatter=[(n, slots[(n, l)]) for n, l in SCATTER_AT[host]])

    def take(host, gathered, handed):
        for (n, l), g in zip(GATHER_AT[host], gathered):
            mats[(n, l)] = _assemble(n, g)
        for key, s in zip(SCATTER_AT[host], handed):
            slot[key] = s

    def linear_(x, n, l, name, residual=None, host=None):
        y, gathered, handed = linear(x, mats[(n, l)], slot[(n, l)], name, residual,
                                     riders_at(host) if host else NO_RIDERS)
        if host:
            take(host, gathered, handed)
        return y

    def fused_(f, rows, params, n, l, name, residual=None, host=None, carry=(), block_rows=None):
        y, carried, gathered, handed = rowwise_linear(
            f, rows, params, [], mats[(n, l)], slot[(n, l)], name, residual,
            riders_at(host) if host else NO_RIDERS, carry, block_rows)
        if host:
            take(host, gathered, handed)
        return (y,) + tuple(carried)

    def gain(n, l):
        return w[n][l].reshape(1, -1)

    for l in range(DEPTH):
        tag = 'l%d_' % l
        p, x = fused_(_rms, [x], [gain('norm_mix_g', l)], 'w_in', l, tag + 'w_in', carry=(0,))
        q, k, v, u_conv, u_ssm = split_columns(p, (0, s1, s2, s3, s4, p.shape[1]))
        q = groupnorm(q, w['sb_q_norm_g'][l], SB_HEAD_DIM, tag + 'q_norm')
        k = groupnorm(k, w['sb_k_norm_g'][l], SB_HEAD_DIM, tag + 'k_norm')
        o_sb, gathered, handed = sb_attention(q, k, v, tag + 'sb', riders_at('sb%d' % l))
        take('sb%d' % l, gathered, handed)

        dw_w = jnp.pad(w['conv_dw_w'][l], ((0, CONV_HALO - CONV_WIDTH), (0, 0)))
        hc = dwconv(glu(u_conv, tag + 'conv_glu'), dw_w, w['conv_dw_b'][l].reshape(1, -1), tag + 'dwconv')
        o_conv, = fused_(ln_silu_block, [hc], [gain('conv_ln_g', l), gain('conv_ln_b', l)], 'conv_pw2_w', l,
                         tag + 'pw2')

        ar, ai, bbr, bbi, cr, ci = ssm_discretise(
            w['ssm_lam_re'][l], w['ssm_lam_im'][l], w['ssm_log_dt'][l], w['ssm_b_re'][l], w['ssm_b_im'][l],
            w['ssm_c_re'][l], w['ssm_c_im'][l])
        y = ssm_core(u_ssm, ar, ai, bbr, bbi, cr, ci, w['ssm_d'][l].reshape(1, -1), tag + 'ssm')
        o_ssm = glu(linear_(y, 'ssm_glu_w', l, tag + 'ssm_glu_w'), tag + 'ssm_glu')

        x, = fused_(branch_norms_block, [o_sb, o_conv, o_ssm], [gain('branch_norm_g', l)], 'w_out', l,
                    tag + 'w_out', residual=x)

        q_raw, x = fused_(_rms, [x], [gain('norm_xa_g', l)], 'xa_wq', l, tag + 'xa_wq', carry=(0,))
        hm = rmsnorm(mem, w['norm_mem_g'][l], tag + 'norm_mem')
        qx = groupnorm(q_raw, w['xa_q_norm_g'][l], XA_HEAD_DIM, tag + 'xa_qn')
        kx = groupnorm(linear_(hm, 'xa_wk', l, tag + 'xa_wk'), w['xa_k_norm_g'][l], XA_HEAD_DIM, tag + 'xa_kn')
        vx = linear_(hm, 'xa_wv', l, tag + 'xa_wv')
        x, = fused_(xa_core_block, [qx], [kx, vx], 'xa_wo', l, tag + 'xa_wo', residual=x,
                    block_rows=min(256, qx.shape[0]))

        gu, x = fused_(_rms, [x], [gain('norm_ffn_g', l)], 'ffn_w_in', l, tag + 'ffn_in',
                       host='ffn0' if l == 0 else None, carry=(0,))
        x, = fused_(swiglu_block, [gu], [], 'ffn_w_out', l, tag + 'ffn_out', residual=x)
    return jnp.sum(loss_rows(x, target, 'loss'))


def local_step(w, mats, shards, x, mem, target):
    def shard_shape(key):
        return shards[key].shape if key in shards else _shard_shape(key[0], mats[key].shape)

    slots = {}
    for host, keys in SCATTER_AT.items():
        for key in keys:
            shape = _whole_shape(key[0], shard_shape(key)) if host == 'end' else (N_CHIPS,) + shard_shape(key)
            slots[key] = jnp.zeros(shape, BF16)
    loss, (g_mats, g_w, gx) = jax.value_and_grad(local_loss, argnums=(0, 1, 4))(
        slots, w, mats, shards, x, mem, target)
    return loss, gx, g_w, g_mats


PACK_ROWS = 2048


PIECE_ROWS = 16


def _piece_rows(size):
    rows = -(-size // LANES)
    return rows, -(-rows // PIECE_ROWS) * PIECE_ROWS


def pack(arrays, dtype):
    parts, total = [], 0
    for a in arrays:
        rows, padded = _piece_rows(a.size)
        a = a.astype(dtype)
        if a.size % LANES:
            a = jnp.pad(a.reshape(-1), (0, rows * LANES - a.size))
        a = a.reshape(rows, LANES)
        if padded != rows:
            a = jnp.pad(a, ((0, padded - rows), (0, 0)))
        parts.append(a)
        total += padded
    tail = -total % PACK_ROWS
    if tail:
        parts.append(jnp.zeros((tail, LANES), dtype))
    return jnp.concatenate(parts, axis=0)


def unpack(packed, shapes):
    out, off = [], 0
    for s in shapes:
        size = math.prod(s)
        rows, padded = _piece_rows(size)
        piece = packed[off:off + rows]
        if size % LANES:
            piece = piece.reshape(-1)[:size]
        out.append(piece.reshape(s))
        off += padded
    return out


def _mesh_pos():
    return lax.axis_index("x"), lax.axis_index("y"), lax.axis_index("c")


def _exchange_xy(n_arrays, src_of, dst_of, sems, wait):
    send_sems, recv_sems, local_sems = sems
    x, y, c = _mesh_pos()
    me = 2 * x + y
    peers = [(1 - x, y), (x, 1 - y), (1 - x, 1 - y)]
    for k in range(n_arrays):
        own = pltpu.make_async_copy(src_of(k, me), dst_of(k, me), local_sems.at[k])
        if wait:
            own.wait()
        else:
            own.start()
        for p, (px, py) in enumerate(peers):
            out = pltpu.make_async_remote_copy(
                src_ref=src_of(k, 2 * px + py), dst_ref=dst_of(k, me), send_sem=send_sems.at[3 * k + p],
                recv_sem=recv_sems.at[3 * k + p], device_id=(px, py, c), device_id_type=MESH)
            if wait:
                pltpu.make_async_remote_copy(
                    src_ref=src_of(k, me), dst_ref=dst_of(k, 2 * px + py), send_sem=send_sems.at[3 * k + p],
                    recv_sem=recv_sems.at[3 * k + p], device_id=(px, py, c), device_id_type=MESH).wait_recv()
                out.wait_send()
            else:
                out.start()


class Ride:
    def __init__(self, arrays, out_shapes, src_of, dst_of):
        self.arrays, self.out_shapes = list(arrays), list(out_shapes)
        self._src_of, self._dst_of = src_of, dst_of
        n = len(self.arrays)
        self.in_specs = [pl.BlockSpec(memory_space=pl.ANY)] * n
        self.out_specs = [pl.BlockSpec(memory_space=pl.ANY)] * len(self.out_shapes)
        self.out_shape = [jax.ShapeDtypeStruct(s, BF16) for s in self.out_shapes]
        self.scratch = [pltpu.SemaphoreType.DMA((3 * n,)), pltpu.SemaphoreType.DMA((3 * n,)),
                        pltpu.SemaphoreType.DMA((n,))]

    def run(self, parts, wait):
        ins, outs, sems = parts
        _exchange_xy(len(self.arrays), lambda k, chip: self._src_of(ins, k, chip),
                     lambda k, chip: self._dst_of(outs, k, chip), sems, wait)

    def at_ends(self, parts, first, last):
        pl.when(first)(lambda: self.run(parts, False))

        def finish():
            pl.when(last)(lambda: self.run(parts, True))
        return finish


def _split_refs(ride, n_in, n_out, refs):
    if ride is None:
        return refs[:n_in], refs[n_in:n_in + n_out], refs[n_in + n_out:], None
    ni, no = len(ride.arrays), len(ride.out_shapes)
    b = n_in + ni
    c = b + n_out
    d = c + no
    return refs[:n_in], refs[b:c], refs[d:len(refs) - 3], (refs[n_in:b], refs[c:d], refs[len(refs) - 3:])


def run_ride(ride, name):
    def body(*refs):
        parts = _split_refs(ride, 0, 0, refs)[3]
        ride.run(parts, False)
        ride.run(parts, True)

    return pl.pallas_call(
        body, name=name, in_specs=ride.in_specs, out_specs=ride.out_specs, out_shape=ride.out_shape,
        scratch_shapes=ride.scratch, compiler_params=pltpu.CompilerParams(has_side_effects=True),
    )(*ride.arrays)


SLOTTED = ('w_in', 'taps')


def _part(ref, axis, chip, size):
    start = pl.multiple_of(chip * size, size)
    index = [slice(None)] * len(ref.shape)
    index[axis] = pl.ds(start, size)
    return ref.at[tuple(index)]


def _whole_shape(name, shard_shape):
    s = list(shard_shape)
    s[SHARD_AXIS[name] - 1] *= N_CHIPS
    return tuple(s)


def _shard_shape(name, whole_shape):
    s = list(whole_shape)
    s[SHARD_AXIS[name] - 1] //= N_CHIPS
    return tuple(s)


def gather_ride(names, shards):
    def out_shape(k):
        return (N_CHIPS,) + shards[k].shape if names[k] in SLOTTED else _whole_shape(names[k], shards[k].shape)

    def dst_of(outs, k, chip):
        if names[k] in SLOTTED:
            return outs[k].at[chip]
        axis = SHARD_AXIS[names[k]] - 1
        return _part(outs[k], axis, chip, shards[k].shape[axis])

    return Ride(shards, [out_shape(k) for k in range(len(shards))], lambda ins, k, chip: ins[k], dst_of)


def scatter_ride(names, grads):
    def shard_shape(k):
        return grads[k].shape[1:] if names[k] in SLOTTED else _shard_shape(names[k], grads[k].shape)

    def src_of(ins, k, chip):
        if names[k] in SLOTTED:
            return ins[k].at[chip]
        axis = SHARD_AXIS[names[k]] - 1
        return _part(ins[k], axis, chip, shard_shape(k)[axis])

    return Ride(grads, [(N_CHIPS,) + shard_shape(k) for k in range(len(grads))], src_of,
                lambda outs, k, chip: outs[k].at[chip])


def _cut_for_chips(name, g):
    if name not in SLOTTED:
        return g
    return jnp.stack([_shard_of(g, SHARD_AXIS[name] - 1, j) for j in range(N_CHIPS)])


def swap_cores(arrays, name):
    na = len(arrays)

    def body(*refs):
        ins, outs, send_sems, recv_sems = refs[:na], refs[na:2 * na], refs[2 * na], refs[2 * na + 1]
        x, y, c = _mesh_pos()
        copies = [pltpu.make_async_remote_copy(
            src_ref=ins[k], dst_ref=outs[k], send_sem=send_sems.at[k], recv_sem=recv_sems.at[k],
            device_id=(x, y, 1 - c), device_id_type=MESH) for k in range(na)]
        for cp in copies:
            cp.start()
        for cp in copies:
            cp.wait()

    return pl.pallas_call(
        body, name=name,
        in_specs=[pl.BlockSpec(memory_space=pl.ANY)] * na,
        out_specs=[pl.BlockSpec(memory_space=pl.ANY)] * na,
        out_shape=[jax.ShapeDtypeStruct(a.shape, a.dtype) for a in arrays],
        scratch_shapes=[pltpu.SemaphoreType.DMA((na,)), pltpu.SemaphoreType.DMA((na,))],
        compiler_params=pltpu.CompilerParams(has_side_effects=True),
    )(*arrays)


def allreduce_small(buf, name):
    R = buf.shape[0]

    def body(in_ref, sum_ref, all_ref, send_sems, recv_sems):
        x, y, c = _mesh_pos()
        me = 4 * x + 2 * y + c
        all_ref[me] = in_ref[...]
        flips = [(fx, fy, fc) for fx in (0, 1) for fy in (0, 1) for fc in (0, 1)][1:]
        sends = []
        for k, (fx, fy, fc) in enumerate(flips):
            cp = pltpu.make_async_remote_copy(
                src_ref=in_ref, dst_ref=all_ref.at[me], send_sem=send_sems.at[k], recv_sem=recv_sems.at[k],
                device_id=(x ^ fx, y ^ fy, c ^ fc), device_id_type=MESH)
            cp.start()
            sends.append(cp)
        for k, (fx, fy, fc) in enumerate(flips):
            peer = 4 * (x ^ fx) + 2 * (y ^ fy) + (c ^ fc)
            pltpu.make_async_remote_copy(
                src_ref=in_ref, dst_ref=all_ref.at[peer], send_sem=send_sems.at[k], recv_sem=recv_sems.at[k],
                device_id=(x ^ fx, y ^ fy, c ^ fc), device_id_type=MESH).wait_recv()
        for cp in sends:
            cp.wait_send()
        acc = all_ref[0]
        for k in range(1, N_DEV):
            acc = acc + all_ref[k]
        sum_ref[...] = acc

    return pl.pallas_call(
        body, name=name,
        in_specs=[pl.BlockSpec(memory_space=pltpu.VMEM)],
        out_specs=[pl.BlockSpec(memory_space=pltpu.VMEM), pl.BlockSpec(memory_space=pltpu.VMEM)],
        out_shape=[jax.ShapeDtypeStruct((R, LANES), F32), jax.ShapeDtypeStruct((N_DEV, R, LANES), F32)],
        scratch_shapes=[pltpu.SemaphoreType.DMA((N_DEV - 1,)), pltpu.SemaphoreType.DMA((N_DEV - 1,))],
        compiler_params=pltpu.CompilerParams(has_side_effects=True, vmem_limit_bytes=VMEM_LIMIT),
    )(buf)[0]


def _adamw_update(g, w, m, v):
    m2 = ADAM_B1 * m + (1.0 - ADAM_B1) * g
    v2 = ADAM_B2 * v + (1.0 - ADAM_B2) * (g * g)
    m_hat = m2 / (1.0 - ADAM_B1 ** ADAM_STEP)
    v_hat = v2 / (1.0 - ADAM_B2 ** ADAM_STEP)
    return -ADAM_LR * (m_hat / (jnp.sqrt(v_hat) + ADAM_EPS) + ADAM_WD * w), m2, v2


def adamw_matrix(layer, mine, other, w, m, v, so_far, name):
    _, rows, cols = w.shape
    T = 16
    while rows % (2 * T) == 0 and 2 * T * cols <= 128 * 1024:
        T *= 2

    def body(mine_ref, other_ref, w_ref, m_ref, v_ref, *rest):
        g_out, d_out, m_out, v_out = rest[-4:]

        def total(ref):
            acc = ref[0].astype(F32)
            for k in range(1, N_CHIPS):
                acc = acc + ref[k].astype(F32)
            return acc

        g = total(mine_ref) + total(other_ref)
        g_out[...] = g
        d_out[...], m_out[...], v_out[...] = _adamw_update(g, w_ref[...], m_ref[...], v_ref[...])

    slots = pl.BlockSpec((N_CHIPS, T, cols), lambda i: (0, i, 0))
    spec = pl.BlockSpec((None, T, cols), lambda i: (layer, i, 0))
    filled = [] if so_far is None else list(so_far)
    return pl.pallas_call(
        body, name=name, grid=(rows // T,),
        in_specs=[slots, slots, spec, spec, spec] + [pl.BlockSpec(memory_space=pl.ANY)] * len(filled),
        out_specs=[spec] * 4,
        out_shape=[jax.ShapeDtypeStruct(w.shape, F32)] * 4,
        input_output_aliases={5 + j: j for j in range(len(filled))},
        compiler_params=_cparams(("parallel",)),
    )(mine, other, w, m, v, *filled)


def adamw_small(gs, ws, ms, vs, name):
    n = len(gs)

    def body(*refs):
        for k in range(n):
            g, w, m, v = (refs[j * n + k][...] for j in range(4))
            d_out, m_out, v_out = (refs[(4 + j) * n + k] for j in range(3))
            d_out[...], m_out[...], v_out[...] = _adamw_update(g, w, m, v)

    vmem = pl.BlockSpec(memory_space=pltpu.VMEM)
    outs = pl.pallas_call(
        body, name=name,
        in_specs=[vmem] * (4 * n), out_specs=[vmem] * (3 * n),
        out_shape=[jax.ShapeDtypeStruct(w.shape, F32) for w in ws] * 3,
        compiler_params=_cparams(),
    )(*gs, *ws, *ms, *vs)
    return outs[:n], outs[n:2 * n], outs[2 * n:]


def _shard_of(full, axis, chip):
    size = full.shape[axis] // N_CHIPS
    return lax.slice_in_dim(full, chip * size, (chip + 1) * size, axis=axis)


def kernel(x, mem, norm_mix_g, w_in, sb_q_norm_g, sb_k_norm_g, conv_dw_w, conv_dw_b, conv_ln_g, conv_ln_b, conv_pw2_w, ssm_lam_re, ssm_lam_im, ssm_log_dt, ssm_b_re, ssm_b_im, ssm_c_re, ssm_c_im, ssm_d, ssm_glu_w, branch_norm_g, w_out, norm_xa_g, norm_mem_g, xa_wq, xa_wk, xa_wv, xa_q_norm_g, xa_k_norm_g, xa_wo, norm_ffn_g, ffn_w_in, ffn_w_out, loss_target, m_norm_mix_g, m_w_in, m_sb_q_norm_g, m_sb_k_norm_g, m_conv_dw_w, m_conv_dw_b, m_conv_ln_g, m_conv_ln_b, m_conv_pw2_w, m_ssm_lam_re, m_ssm_lam_im, m_ssm_log_dt, m_ssm_b_re, m_ssm_b_im, m_ssm_c_re, m_ssm_c_im, m_ssm_d, m_ssm_glu_w, m_branch_norm_g, m_w_out, m_norm_xa_g, m_norm_mem_g, m_xa_wq, m_xa_wk, m_xa_wv, m_xa_q_norm_g, m_xa_k_norm_g, m_xa_wo, m_norm_ffn_g, m_ffn_w_in, m_ffn_w_out, v_norm_mix_g, v_w_in, v_sb_q_norm_g, v_sb_k_norm_g, v_conv_dw_w, v_conv_dw_b, v_conv_ln_g, v_conv_ln_b, v_conv_pw2_w, v_ssm_lam_re, v_ssm_lam_im, v_ssm_log_dt, v_ssm_b_re, v_ssm_b_im, v_ssm_c_re, v_ssm_c_im, v_ssm_d, v_ssm_glu_w, v_branch_norm_g, v_w_out, v_norm_xa_g, v_norm_mem_g, v_xa_wq, v_xa_wk, v_xa_wv, v_xa_q_norm_g, v_xa_k_norm_g, v_xa_wo, v_norm_ffn_g, v_ffn_w_in, v_ffn_w_out):
    given = dict(locals())
    w = {n: given[n] for n in WEIGHTS}
    m = {n: given['m_' + n] for n in WEIGHTS}
    v = {n: given['v_' + n] for n in WEIGHTS}

    depth = w_in.shape[0]
    chip = 2 * lax.axis_index("x") + lax.axis_index("y")

    assert depth == DEPTH
    taps_bits = lax.bitcast_convert_type(conv_dw_w, BF16)
    shards = {(n, l): w[n][l].astype(BF16) for n in MATRICES for l in range(depth)}
    first = [shards.pop(key) for key in GATHER_AT['start']]
    gathered = run_ride(gather_ride([n for n, _ in GATHER_AT['start']] + ['taps'], first + [taps_bits]),
                        'gather_first')
    mats = {key: _assemble(key[0], g) for key, g in zip(GATHER_AT['start'], gathered)}
    taps = jnp.concatenate([lax.bitcast_convert_type(gathered[-1][j], F32) for j in range(N_CHIPS)], axis=2)
    local_w = {n: w[n] for n in REPLICATED}
    local_w['conv_dw_w'] = taps

    loss, gx, g_w, mine = local_step(local_w, mats, shards, x[0], mem[0], loss_target[0])
    last = SCATTER_AT['end']
    received = run_ride(scatter_ride([n for n, _ in last], [_cut_for_chips(n, mine[(n, l)]) for n, l in last]),
                        'scatter_last')
    mine.update(zip(last, received))

    keys = [(n, l) for n in MATRICES for l in range(depth)]
    other = dict(zip(keys, swap_cores([mine[key] for key in keys], 'swap_cores')))
    outs = {}
    for n in MATRICES:
        res = None
        for l in range(depth):
            res = adamw_matrix(l, mine[(n, l)], other[(n, l)], w[n], m[n], v[n], res, 'adamw_%s_%d' % (n, l))
        outs['grad_' + n], outs['delta_' + n], outs['new_m_' + n], outs['new_v_' + n] = res

    reduced = allreduce_small(pack([g_w[n] for n in REPLICATED] + [g_w['conv_dw_w'], loss.reshape(1)], F32),
                              'allreduce_small')
    reduced = unpack(reduced, [w[n].shape for n in REPLICATED] + [taps.shape, (1,)])
    total_loss = reduced[-1].reshape(())
    tap_cols = conv_dw_w.shape[2]
    reduced[-2] = lax.dynamic_slice_in_dim(reduced[-2], chip * tap_cols, tap_cols, axis=2)
    small_names = REPLICATED + ['conv_dw_w']
    deltas, new_ms, new_vs = adamw_small(reduced[:-1], [w[n] for n in small_names], [m[n] for n in small_names],
                                         [v[n] for n in small_names], 'adamw_small')
    for k, n in enumerate(small_names):
        outs['grad_' + n], outs['delta_' + n] = reduced[k], deltas[k]
        outs['new_m_' + n], outs['new_v_' + n] = new_ms[k], new_vs[k]
    return (total_loss, gx[None], *[outs['grad_' + n] for n in WEIGHTS], *[outs['delta_' + n] for n in WEIGHTS],
            *[outs['new_m_' + n] for n in WEIGHTS], *[outs['new_v_' + n] for n in WEIGHTS])
```

```python
import functools
import math

import jax
import jax.numpy as jnp
from jax import lax
from jax.experimental import pallas as pl
from jax.experimental.pallas import tpu as pltpu

F32 = jnp.float32
BF16 = jnp.bfloat16
MESH = pl.DeviceIdType.MESH
HIGHEST = lax.Precision.HIGHEST

EPS = 1e-6
LANES = 128
SUBLANES = 8
VMEM_LIMIT = 56 * 1024 * 1024

SB_HEAD_DIM = 64
SB_WIDTH = 512
CONV_CH = 256
CONV_WIDTH = 31
CONV_HALO = 32
SSM_CH = 256
SSM_GROUPS = 16
SSM_GROUP = 16
SSM_STATE = 64
SSM_LANES = SSM_GROUPS * SSM_STATE
XA_HEADS = 4
XA_HEAD_DIM = 256
SB_CUT = 110.0

ADAM_LR = 0.001
ADAM_B1 = 0.9
ADAM_B2 = 0.999
ADAM_EPS = 1e-08
ADAM_WD = 0.01
ADAM_STEP = 10

WEIGHTS = ['norm_mix_g', 'w_in', 'sb_q_norm_g', 'sb_k_norm_g', 'conv_dw_w', 'conv_dw_b', 'conv_ln_g',
           'conv_ln_b', 'conv_pw2_w', 'ssm_lam_re', 'ssm_lam_im', 'ssm_log_dt', 'ssm_b_re', 'ssm_b_im',
           'ssm_c_re', 'ssm_c_im', 'ssm_d', 'ssm_glu_w', 'branch_norm_g', 'w_out', 'norm_xa_g',
           'norm_mem_g', 'xa_wq', 'xa_wk', 'xa_wv', 'xa_q_norm_g', 'xa_k_norm_g', 'xa_wo', 'norm_ffn_g',
           'ffn_w_in', 'ffn_w_out']
SHARD_AXIS = {'w_in': 2, 'conv_dw_w': 2, 'conv_pw2_w': 1, 'ssm_glu_w': 2, 'w_out': 1, 'xa_wq': 1,
              'xa_wk': 1, 'xa_wv': 1, 'xa_wo': 1, 'ffn_w_in': 2, 'ffn_w_out': 1}
MATRICES = [n for n in WEIGHTS if n in SHARD_AXIS and n != 'conv_dw_w']
REPLICATED = [n for n in WEIGHTS if n not in SHARD_AXIS]
N_CHIPS = 4
N_DEV = 8


def _cparams(sem=None, **kw):
    if sem is not None:
        kw['dimension_semantics'] = sem
    return pltpu.CompilerParams(vmem_limit_bytes=VMEM_LIMIT, **kw)


def _pick(n, target):
    best = None
    d = LANES
    while d <= min(n, target):
        if n % d == 0:
            best = d
        d += LANES
    return best if best is not None else n


def _rows_for(n_rows, width):
    t = 512
    while t > 8 and t * width > 768 * 1024:
        t //= 2
    return min(t, n_rows)


def _dg(a, b, ca, cb):
    return lax.dot_general(a.astype(BF16), b.astype(BF16), (((ca,), (cb,)), ((), ())),
                           preferred_element_type=F32)


@jax.custom_vjp
def bdot_nn(a, b):
    return _dg(a, b, 1, 0)


def _bdot_nn_fwd(a, b):
    return _dg(a, b, 1, 0), (a, b)


def _bdot_nn_bwd(res, g):
    a, b = res
    return _dg(g, b, 1, 1), _dg(a, g, 0, 0)


bdot_nn.defvjp(_bdot_nn_fwd, _bdot_nn_bwd)


@jax.custom_vjp
def bdot_nt(a, b):
    return _dg(a, b, 1, 1)


def _bdot_nt_fwd(a, b):
    return _dg(a, b, 1, 1), (a, b)


def _bdot_nt_bwd(res, g):
    a, b = res
    return _dg(g, b, 1, 0), _dg(g, a, 0, 0)


bdot_nt.defvjp(_bdot_nt_fwd, _bdot_nt_bwd)


def mm(a, b, mode, name, out_dtype=F32, add=None, ride=None):
    if mode == 'nn':
        M, K = a.shape
        N = b.shape[1]
    elif mode == 'nt':
        M, K = a.shape
        N = b.shape[0]
    else:
        K, M = a.shape
        N = b.shape[1]
    if mode == 'tn':
        tm, tn, tk = _pick(M, 1536), _pick(N, 2816), _pick(K, 512)
    else:
        tm, tn = _pick(M, 1024), _pick(N, 1536)
        tk = K if K <= 2816 else _pick(K, 1536)
    nk = K // tk
    ca, cb = {'nn': (1, 0), 'nt': (1, 1), 'tn': (0, 0)}[mode]

    grid = (M // tm, N // tn, nk)

    def body(*refs):
        ins, (o_ref,), scratch, riding = _split_refs(ride, 3 if add is not None else 2, 1, refs)
        a_ref, b_ref = ins[:2]
        add_ref = ins[2] if add is not None else None
        i, j, k = pl.program_id(0), pl.program_id(1), pl.program_id(2)
        ride_done = None
        if riding is not None:
            first = jnp.logical_and(i == 0, jnp.logical_and(j == 0, k == 0))
            last = jnp.logical_and(i == grid[0] - 1, jnp.logical_and(j == grid[1] - 1, k == nk - 1))
            ride_done = ride.at_ends(riding, first, last)

        def finish(acc):
            if add_ref is not None:
                acc = acc + add_ref[...]
            o_ref[...] = acc.astype(out_dtype)

        if nk == 1:
            finish(_dg(a_ref[...], b_ref[...], ca, cb))
        else:
            acc_ref, = scratch

            @pl.when(k == 0)
            def _():
                acc_ref[...] = jnp.zeros_like(acc_ref)

            acc_ref[...] += _dg(a_ref[...], b_ref[...], ca, cb)

            @pl.when(k == nk - 1)
            def _():
                finish(acc_ref[...])

        if ride_done is not None:
            ride_done()

    if mode == 'nn':
        a_spec = pl.BlockSpec((tm, tk), lambda i, j, k: (i, k))
        b_spec = pl.BlockSpec((tk, tn), lambda i, j, k: (k, j))
    elif mode == 'nt':
        a_spec = pl.BlockSpec((tm, tk), lambda i, j, k: (i, k))
        b_spec = pl.BlockSpec((tn, tk), lambda i, j, k: (j, k))
    else:
        a_spec = pl.BlockSpec((tk, tm), lambda i, j, k: (k, i))
        b_spec = pl.BlockSpec((tk, tn), lambda i, j, k: (k, j))
    out_spec = pl.BlockSpec((tm, tn), lambda i, j, k: (i, j))
    own_in = [a_spec, b_spec] + ([out_spec] if add is not None else [])
    operands = (a, b) if add is None else (a, b, add)
    scratch = [pltpu.VMEM((tm, tn), F32)] if nk > 1 else []
    if ride is None:
        return pl.pallas_call(
            body, name=name, grid=grid, in_specs=own_in, out_specs=out_spec,
            out_shape=jax.ShapeDtypeStruct((M, N), out_dtype), scratch_shapes=scratch,
            compiler_params=_cparams(("parallel", "parallel", "arbitrary")),
        )(*operands)
    outs = pl.pallas_call(
        body, name=name, grid=grid, in_specs=own_in + ride.in_specs, out_specs=[out_spec] + ride.out_specs,
        out_shape=[jax.ShapeDtypeStruct((M, N), out_dtype)] + ride.out_shape,
        scratch_shapes=scratch + ride.scratch,
        compiler_params=_cparams(("arbitrary", "arbitrary", "arbitrary"), has_side_effects=True),
    )(*operands, *ride.arrays)
    return outs[0], outs[1:]


class Riders:
    def __init__(self, gather=(), scatter=()):
        self.gather_names = [n for n, _ in gather]
        self.shards = tuple(s for _, s in gather)
        self.scatter_names = [n for n, _ in scatter]
        self.slots = tuple(s for _, s in scatter)

    def gather_ride(self, shards):
        return gather_ride(self.gather_names, list(shards)) if shards else None

    def scatter_ride(self, grads):
        if not grads:
            return None
        return scatter_ride(self.scatter_names, [_cut_for_chips(n, g) for n, g in zip(self.scatter_names, grads)])

    def whole_slots(self, slots):
        return tuple(jnp.zeros(_whole_shape(n, s.shape[1:]), BF16) for n, s in zip(self.scatter_names, slots))


NO_RIDERS = Riders()


def linear(x, w, slot, name, residual=None, riders=NO_RIDERS):
    @jax.custom_vjp
    def op(x, w, slot, residual, shards, slots):
        ride = riders.gather_ride(shards)
        y = mm(x, w, 'nn', name + '_fwd', add=residual, ride=ride)
        y, gathered = y if ride is not None else (y, ())
        return y, tuple(gathered), riders.whole_slots(slots)

    def op_fwd(x, w, slot, residual, shards, slots):
        return op(x, w, slot, residual, shards, slots), (x, w, shards)

    def op_bwd(res, cts):
        x, w, shards = res
        g, _, slot_grads = cts
        ride = riders.scatter_ride(slot_grads)
        dx = mm(g, w, 'nt', name + '_dx', ride=ride)
        dx, received = dx if ride is not None else (dx, ())
        return (dx, jnp.zeros_like(w), mm(x, g, 'tn', name + '_dw', BF16), None if residual is None else g,
                tuple(jnp.zeros_like(s) for s in shards), tuple(received))

    op.defvjp(op_fwd, op_bwd)
    return op(x, w, slot, residual, riders.shards, riders.slots)


def _rowwise_calls(f, rows, params, consts, out_widths, name, need_row_grad=None, block_rows=None, carry=(),
                   out_dtype=F32):
    nr, npar, nc, nout = len(rows), len(params), len(consts), len(out_widths)
    carry = tuple(carry)
    L = rows[0].shape[0]
    widths = [r.shape[1] for r in rows]
    T = block_rows or _rows_for(L, max(widths + list(out_widths)))
    n = L // T
    need = list(need_row_grad) if need_row_grad is not None else [True] * nr
    pshapes = [p.shape for p in params]
    cshapes = [c.shape for c in consts]

    row_specs = [pl.BlockSpec((T, w), lambda i: (i, 0)) for w in widths]
    par_specs = [pl.BlockSpec(s, lambda i: (0, 0)) for s in pshapes]
    con_specs = [pl.BlockSpec(s, lambda i: (0, 0)) for s in cshapes]
    out_specs = [pl.BlockSpec((T, w), lambda i: (i, 0)) for w in out_widths]

    def fwd_call(rows, params, consts):
        def body(*refs):
            ins = [r[...] for r in refs[:nr + npar + nc]]
            outs = f(*ins)
            for o_ref, val in zip(refs[nr + npar + nc:], outs):
                o_ref[...] = val.astype(out_dtype)

        return pl.pallas_call(
            body, name=name + '_fwd', grid=(n,),
            in_specs=row_specs + par_specs + con_specs, out_specs=out_specs,
            out_shape=[jax.ShapeDtypeStruct((L, w), out_dtype) for w in out_widths],
            compiler_params=_cparams(("parallel",)),
        )(*rows, *params, *consts)

    def bwd_call(rows, params, consts, cts, carried):
        grad_rows = [k for k in range(nr) if need[k]]
        n_in = nr + npar + nc + nout

        def body(*refs):
            i = pl.program_id(0)
            rv = [r[...] for r in refs[:nr]]
            pv = [r[...] for r in refs[nr:nr + npar]]
            cv = [r[...] for r in refs[nr + npar:nr + npar + nc]]
            ctv = tuple(r[...] for r in refs[nr + npar + nc:n_in])
            carried_refs = dict(zip(carry, refs[n_in:n_in + len(carry)]))
            orefs = refs[n_in + len(carry):]
            _, vjp = jax.vjp(lambda *rp: tuple(f(*rp, *cv)), *rv, *pv)
            g = vjp(ctv)
            for slot, k in enumerate(grad_rows):
                orefs[slot][...] = g[k] + carried_refs[k][...] if k in carried_refs else g[k]

            @pl.when(i == 0)
            def _():
                for k in range(npar):
                    orefs[len(grad_rows) + k][...] = jnp.zeros(pshapes[k], F32)

            for k in range(npar):
                orefs[len(grad_rows) + k][...] += g[nr + k]

        outs = pl.pallas_call(
            body, name=name + '_bwd', grid=(n,),
            in_specs=row_specs + par_specs + con_specs + out_specs + [row_specs[k] for k in carry],
            out_specs=[row_specs[k] for k in grad_rows] + par_specs,
            out_shape=[jax.ShapeDtypeStruct((L, widths[k]), F32) for k in grad_rows]
            + [jax.ShapeDtypeStruct(s, F32) for s in pshapes],
            compiler_params=_cparams(("arbitrary",)),
        )(*rows, *params, *consts, *cts, *carried)
        drows = []
        slot = 0
        for k in range(nr):
            if need[k]:
                drows.append(outs[slot])
                slot += 1
            else:
                drows.append(jnp.zeros_like(rows[k]))
        return tuple(drows), tuple(outs[len(grad_rows):])

    return fwd_call, bwd_call


def rowwise(f, rows, params, consts, out_widths, name, need_row_grad=None, block_rows=None, carry=()):
    fwd_call, bwd_call = _rowwise_calls(f, rows, params, consts, out_widths, name, need_row_grad, block_rows, carry)
    nout = len(out_widths)

    @jax.custom_vjp
    def op(rows, params, consts):
        return tuple(fwd_call(rows, params, consts)) + tuple(rows[k] for k in carry)

    def op_fwd(rows, params, consts):
        return op(rows, params, consts), (rows, params, consts)

    def op_bwd(res, cts):
        rows, params, consts = res
        drows, dparams = bwd_call(rows, params, consts, cts[:nout], cts[nout:])
        return drows, dparams, tuple(jnp.zeros_like(c) for c in consts)

    op.defvjp(op_fwd, op_bwd)
    return op(tuple(rows), tuple(params), tuple(consts))


def rowwise_linear(f, rows, params, consts, w, slot, name, residual=None, riders=NO_RIDERS, carry=(),
                   block_rows=None):
    width = w.shape[0]
    fwd_call, bwd_call = _rowwise_calls(lambda *a: (f(*a),), rows, params, consts, [width], name, None,
                                        block_rows, carry, BF16)

    @jax.custom_vjp
    def op(rows, params, consts, w, slot, residual, shards, slots):
        h, = fwd_call(rows, params, consts)
        ride = riders.gather_ride(shards)
        y = mm(h, w, 'nn', name + '_mm', add=residual, ride=ride)
        y, gathered = y if ride is not None else (y, ())
        return y, tuple(rows[k] for k in carry), tuple(gathered), riders.whole_slots(slots)

    def op_fwd(rows, params, consts, w, slot, residual, shards, slots):
        h, = fwd_call(rows, params, consts)
        ride = riders.gather_ride(shards)
        y = mm(h, w, 'nn', name + '_mm', add=residual, ride=ride)
        y, gathered = y if ride is not None else (y, ())
        out = (y, tuple(rows[k] for k in carry), tuple(gathered), riders.whole_slots(slots))
        return out, (rows, params, consts, h, w, shards)

    def op_bwd(res, cts):
        rows, params, consts, h, w, shards = res
        g, carried, _, slot_grads = cts
        ride = riders.scatter_ride(slot_grads)
        dh = mm(g, w, 'nt', name + '_dx', ride=ride)
        dh, received = dh if ride is not None else (dh, ())
        drows, dparams = bwd_call(rows, params, consts, (dh,), carried)
        return (drows, dparams, tuple(jnp.zeros_like(c) for c in consts), jnp.zeros_like(w),
                mm(h, g, 'tn', name + '_dw', BF16), None if residual is None else g,
                tuple(jnp.zeros_like(s) for s in shards), tuple(received))

    op.defvjp(op_fwd, op_bwd)
    return op(tuple(rows), tuple(params), tuple(consts), w, slot, residual, riders.shards, riders.slots)


def _rms(x, g):
    return x * lax.rsqrt(jnp.mean(x * x, axis=-1, keepdims=True) + EPS) * g


def rmsnorm(x, g, name, carry=False):
    out = rowwise(lambda x, g: (_rms(x, g),), [x], [g.reshape(1, -1)], [], [x.shape[1]], name,
                  carry=(0,) if carry else ())
    return out if carry else out[0]


def _split2(x):
    hi = x.astype(BF16)
    return hi, (x - hi.astype(F32)).astype(BF16)


@jax.custom_vjp
def select_mm(x, sel):
    return sum(_dg(t, sel, 1, 0) for t in _split2(x))


def _select_mm_fwd(x, sel):
    return select_mm(x, sel), sel


def _select_mm_bwd(sel, g):
    return sum(_dg(t, sel, 1, 1) for t in _split2(g)), jnp.zeros_like(sel)


select_mm.defvjp(_select_mm_fwd, _select_mm_bwd)


def groupnorm(x, g, group, name):
    width = x.shape[1]
    g_full = jnp.tile(g.reshape(1, group), (1, width // group))
    if group % LANES == 0:
        def f(x, g_full):
            outs = []
            for lo in range(0, width, group):
                xs = x[:, lo:lo + group]
                outs.append(_rms(xs, g_full[:, lo:lo + group]))
            return (jnp.concatenate(outs, axis=-1),)

        return rowwise(f, [x], [g_full], [], [width], name)[0]

    gid = jnp.arange(width) // group
    sel = (gid[:, None] == jnp.arange(LANES)[None, :]).astype(BF16)

    def f(x, g_full, sel, sel_t):
        ms = select_mm(x * x, sel) * (1.0 / group)
        inv = select_mm(lax.rsqrt(ms + EPS), sel_t)
        return (x * inv * g_full,)

    return rowwise(f, [x], [g_full], [sel, sel.T], [width], name)[0]


def glu(x, name):
    half = x.shape[1] // 2

    def f(x):
        return (x[:, :half] * jax.nn.sigmoid(x[:, half:]),)

    return rowwise(f, [x], [], [], [half], name)[0]


def swiglu_block(x):
    half = x.shape[1] // 2
    gate = x[:, :half]
    return gate * jax.nn.sigmoid(gate) * x[:, half:]


def ln_silu_block(x, g, b):
    mu = jnp.mean(x, axis=-1, keepdims=True)
    xc = x - mu
    var = jnp.mean(xc * xc, axis=-1, keepdims=True)
    y = xc * lax.rsqrt(var + EPS) * g + b
    return y * jax.nn.sigmoid(y)


def branch_norms_block(a, b, c, g):
    w1, w2 = a.shape[1], b.shape[1]
    return jnp.concatenate([_rms(a, g[:, :w1]), _rms(b, g[:, w1:w1 + w2]), _rms(c, g[:, w1 + w2:])], axis=-1)


def xa_core_block(q, k, v):
    scale = XA_HEAD_DIM ** -0.5
    outs = []
    for h in range(XA_HEADS):
        sl = slice(h * XA_HEAD_DIM, (h + 1) * XA_HEAD_DIM)
        s = bdot_nt(q[:, sl], k[:, sl]) * scale
        m = lax.stop_gradient(jnp.max(s, axis=-1, keepdims=True))
        e = jnp.exp(s - m)
        p = e / jnp.sum(e, axis=-1, keepdims=True)
        outs.append(bdot_nn(p, v[:, sl]))
    return jnp.concatenate(outs, axis=-1)


def loss_rows(y, target, name):
    def f(y, t):
        d = y - t
        return (0.5 * jnp.mean(d * d, axis=-1, keepdims=True),)

    return rowwise(f, [y, target], [], [], [1], name, need_row_grad=[True, False])[0]


def _hilo(x, ones_bf16):
    hi = x.astype(BF16)
    lo = (x - hi.astype(F32)).astype(BF16)
    return _dg(hi, ones_bf16, 1, 0) + _dg(lo, ones_bf16, 1, 0)


def _sb_block(qh, kb, c, valid, strict_upper):
    z = _dg(qh, kb, 1, 1)
    a = jnp.minimum(z, 0.0) - jnp.log(1.0 + jnp.exp(-jnp.abs(z)))
    b = jnp.where(valid, a - z, 0.0)
    s = _hilo(b, strict_upper) + c
    w = jnp.where(valid, jnp.exp(a + s), 0.0)
    return a, b, w


def _sb_masks(T):
    row = lax.broadcasted_iota(jnp.int32, (T, T), 0)
    col = lax.broadcasted_iota(jnp.int32, (T, T), 1)
    return col < row, (row > col).astype(BF16), (row >= col).astype(BF16)


def _sb_key_blocks(i, j, T, causal):
    second = jnp.maximum(j - 1, 0)
    return [(pl.multiple_of(j * T, T), jnp.logical_or(causal, j != i)),
            (pl.multiple_of(second * T, T), jnp.logical_and(jnp.logical_or(causal, True), j >= 1))]


HEADS_PER_BLOCK = LANES // SB_HEAD_DIM


def _head_mask(h):
    lane = lax.broadcasted_iota(jnp.int32, (1, LANES), 1)
    return (lane // SB_HEAD_DIM == h).astype(F32)


def _max_all(columns):
    m = columns[0]
    for c in columns[1:]:
        m = jnp.maximum(m, c)
    return jnp.max(m)


def _ride_call(body, name, grid, in_specs, out_specs, out_shape, operands, ride, semantics):
    if ride is None:
        outs = pl.pallas_call(body, name=name, grid=grid, in_specs=in_specs, out_specs=out_specs,
                              out_shape=out_shape, compiler_params=_cparams(semantics))(*operands)
        return outs, ()
    outs = pl.pallas_call(
        body, name=name, grid=grid, in_specs=in_specs + ride.in_specs, out_specs=out_specs + ride.out_specs,
        out_shape=out_shape + ride.out_shape, scratch_shapes=ride.scratch,
        compiler_params=_cparams(("arbitrary",) * len(grid), has_side_effects=True),
    )(*operands, *ride.arrays)
    return outs[:len(out_shape)], outs[len(out_shape):]


def _ride_ends(ride, riding, grid):
    if riding is None:
        return lambda: None
    first, last = None, None
    for axis, size in enumerate(grid):
        at0, at1 = pl.program_id(axis) == 0, pl.program_id(axis) == size - 1
        first = at0 if first is None else jnp.logical_and(first, at0)
        last = at1 if last is None else jnp.logical_and(last, at1)
    return ride.at_ends(riding, first, last)


def _sb_fwd_call(q, k, v, T, name, ride=None):
    L, W = q.shape
    scale = SB_HEAD_DIM ** -0.5
    grid = (W // LANES, L // T)

    def body(*refs):
        (q_ref, k_ref, v_ref), (o_ref,), _, riding = _split_refs(ride, 3, 1, refs)
        ride_done = _ride_ends(ride, riding, grid)
        i = pl.program_id(1)
        causal, strict_upper, _ = _sb_masks(T)
        q2 = q_ref[...] * scale
        masks = [_head_mask(h) for h in range(HEADS_PER_BLOCK)]
        qs = [(q2 * hm).astype(BF16) for hm in masks]
        zero = jnp.zeros((T, 1), F32)

        def cond(state):
            j, cs, _ = state
            return jnp.logical_and(j >= 0, _max_all(cs) > -SB_CUT)

        def step(state):
            j, cs, acc = state
            blocks = _sb_key_blocks(i, j, T, causal)
            ks = [k_ref[pl.ds(r0, T), :].astype(BF16) for r0, _ in blocks]
            vs = [v_ref[pl.ds(r0, T), :] for r0, _ in blocks]
            new_cs = []
            for hm, qh, c in zip(masks, qs, cs):
                for (_, valid), kb, vb in zip(blocks, ks, vs):
                    _, b, w = _sb_block(qh, kb, c, valid, strict_upper)
                    vh = (vb * hm).astype(BF16)
                    w_hi = w.astype(BF16)
                    w_lo = (w - w_hi.astype(F32)).astype(BF16)
                    acc = acc + _dg(w_hi, vh, 1, 0) + _dg(w_lo, vh, 1, 0)
                    c = c + jnp.sum(b, axis=1, keepdims=True)
                new_cs.append(c)
            return j - len(blocks), tuple(new_cs), acc

        _, _, acc = lax.while_loop(cond, step, (i, (zero,) * HEADS_PER_BLOCK, jnp.zeros((T, LANES), F32)))
        o_ref[...] = acc
        ride_done()

    (o,), rode = _ride_call(
        body, name, grid,
        [pl.BlockSpec((T, LANES), lambda p, i: (i, p)),
         pl.BlockSpec((L, LANES), lambda p, i: (0, p)),
         pl.BlockSpec((L, LANES), lambda p, i: (0, p))],
        [pl.BlockSpec((T, LANES), lambda p, i: (i, p))], [jax.ShapeDtypeStruct((L, W), F32)],
        (q, k, v), ride, ("parallel", "parallel"))
    return o, rode


def _sb_bwd_call(q, k, v, o, do, T, name, ride=None):
    L, W = q.shape
    scale = SB_HEAD_DIM ** -0.5
    grid = (W // LANES, L // T)

    def body(*refs):
        (q_ref, k_ref, v_ref, o_ref, do_ref), (dq_ref, dk_ref, dv_ref), _, riding = _split_refs(ride, 5, 3, refs)
        ride_done = _ride_ends(ride, riding, grid)
        i = pl.program_id(1)

        @pl.when(i == 0)
        def _():
            dk_ref[...] = jnp.zeros_like(dk_ref)
            dv_ref[...] = jnp.zeros_like(dv_ref)

        causal, strict_upper, upper = _sb_masks(T)
        q2 = q_ref[...] * scale
        do2 = do_ref[...]
        o2 = o_ref[...]
        masks = [_head_mask(h) for h in range(HEADS_PER_BLOCK)]
        qs = [(q2 * hm).astype(BF16) for hm in masks]
        dos = [(do2 * hm).astype(BF16) for hm in masks]
        totals = [jnp.sum(doh.astype(F32) * o2, axis=1, keepdims=True) for doh in dos]
        zero = jnp.zeros((T, 1), F32)

        def cond(state):
            j, cs, _, _ = state
            return jnp.logical_and(j >= 0, _max_all(cs) > -SB_CUT)

        def step(state):
            j, cs, rs, dq = state
            blocks = _sb_key_blocks(i, j, T, causal)
            kfs = [k_ref[pl.ds(r0, T), :] for r0, _ in blocks]
            ks = [kf.astype(BF16) for kf in kfs]
            vs = [v_ref[pl.ds(r0, T), :].astype(BF16) for r0, _ in blocks]
            dks = [jnp.zeros((T, LANES), F32) for _ in blocks]
            dvs = [jnp.zeros((T, LANES), F32) for _ in blocks]
            new_cs, new_rs = [], []
            for hm, qh, doh, total, c, r in zip(masks, qs, dos, totals, cs, rs):
                for n, ((_, valid), kf, kb, vb) in enumerate(zip(blocks, kfs, ks, vs)):
                    a, b, w = _sb_block(qh, kb, c, valid, strict_upper)
                    e = _dg(doh, vb, 1, 1) * w
                    before = total - (_hilo(e, upper) + r)
                    dz = jnp.where(valid, e * jnp.exp(b) - before * jnp.exp(a), 0.0).astype(BF16)
                    dq = dq + _dg(dz, kf * hm, 1, 0)
                    dks[n] = dks[n] + _dg(dz, qh, 0, 0)
                    dvs[n] = dvs[n] + _dg(w, doh, 0, 0)
                    c = c + jnp.sum(b, axis=1, keepdims=True)
                    r = r + jnp.sum(e, axis=1, keepdims=True)
                new_cs.append(c)
                new_rs.append(r)
            for (r0, _), dk, dv in zip(blocks, dks, dvs):
                dk_ref[pl.ds(r0, T), :] += dk
                dv_ref[pl.ds(r0, T), :] += dv
            return j - len(blocks), tuple(new_cs), tuple(new_rs), dq

        init = (i, (zero,) * HEADS_PER_BLOCK, (zero,) * HEADS_PER_BLOCK, jnp.zeros((T, LANES), F32))
        dq = lax.while_loop(cond, step, init)[3]
        dq_ref[...] = dq * scale
        ride_done()

    blk = pl.BlockSpec((T, LANES), lambda p, i: (i, p))
    full = pl.BlockSpec((L, LANES), lambda p, i: (0, p))
    return _ride_call(body, name, grid, [blk, full, full, blk, blk], [blk, full, full],
                      [jax.ShapeDtypeStruct((L, W), F32)] * 3, (q, k, v, o, do), ride, ("parallel", "arbitrary"))


def sb_attention(q, k, v, name, riders=NO_RIDERS):
    T = min(256, q.shape[0])

    @jax.custom_vjp
    def op(q, k, v, shards, slots):
        o, gathered = _sb_fwd_call(q, k, v, T, name + '_fwd', riders.gather_ride(shards))
        return o, tuple(gathered), riders.whole_slots(slots)

    def op_fwd(q, k, v, shards, slots):
        out = op(q, k, v, shards, slots)
        return out, (q, k, v, out[0], shards)

    def op_bwd(res, cts):
        q, k, v, o, shards = res
        do, _, slot_grads = cts
        grads, received = _sb_bwd_call(q, k, v, o, do, T, name + '_bwd', riders.scatter_ride(slot_grads))
        return (*grads, tuple(jnp.zeros_like(s) for s in shards), tuple(received))

    op.defvjp(op_fwd, op_bwd)
    return op(q, k, v, riders.shards, riders.slots)


def _dwconv_fwd_call(x, w, b, T, name):
    L, C = x.shape
    per = T // CONV_HALO
    lead = CONV_HALO - (CONV_WIDTH - 1)

    def body(x_ref, halo_ref, w_ref, b_ref, o_ref, buf):
        i = pl.program_id(0)
        buf[0:CONV_HALO, :] = jnp.where(i > 0, halo_ref[...], 0.0)
        buf[CONV_HALO:CONV_HALO + T, :] = x_ref[...]
        acc = jnp.zeros((T, C), F32) + b_ref[...]
        for j in range(CONV_WIDTH):
            acc = acc + w_ref[j:j + 1, :] * buf[lead + j:lead + j + T, :]
        o_ref[...] = acc

    return pl.pallas_call(
        body, name=name, grid=(L // T,),
        in_specs=[pl.BlockSpec((T, C), lambda i: (i, 0)),
                  pl.BlockSpec((CONV_HALO, C), lambda i: (jnp.maximum(i * per - 1, 0), 0)),
                  pl.BlockSpec(w.shape, lambda i: (0, 0)),
                  pl.BlockSpec(b.shape, lambda i: (0, 0))],
        out_specs=pl.BlockSpec((T, C), lambda i: (i, 0)),
        out_shape=jax.ShapeDtypeStruct((L, C), F32),
        scratch_shapes=[pltpu.VMEM((T + CONV_HALO, C), F32)],
        compiler_params=_cparams(("parallel",)),
    )(x, x, w, b)


def _dwconv_bwd_call(x, w, g, T, name):
    L, C = x.shape
    per = T // CONV_HALO
    n = L // T
    last_halo = L // CONV_HALO - 1
    lead = CONV_HALO - (CONV_WIDTH - 1)

    def body(x_ref, xh_ref, g_ref, gh_ref, w_ref, dx_ref, dw_ref, db_ref, bufx, bufg):
        i = pl.program_id(0)
        bufx[0:CONV_HALO, :] = jnp.where(i > 0, xh_ref[...], 0.0)
        bufx[CONV_HALO:CONV_HALO + T, :] = x_ref[...]
        gm = g_ref[...]
        bufg[0:T, :] = gm
        bufg[T:T + CONV_HALO, :] = jnp.where(i < n - 1, gh_ref[...], 0.0)
        acc = jnp.zeros((T, C), F32)
        for j in range(CONV_WIDTH):
            off = CONV_WIDTH - 1 - j
            acc = acc + w_ref[j:j + 1, :] * bufg[off:off + T, :]
        dx_ref[...] = acc

        @pl.when(i == 0)
        def _():
            dw_ref[...] = jnp.zeros_like(dw_ref)
            db_ref[...] = jnp.zeros_like(db_ref)

        for j in range(CONV_WIDTH):
            dw_ref[j:j + 1, :] += jnp.sum(gm * bufx[lead + j:lead + j + T, :], axis=0, keepdims=True)
        db_ref[...] += jnp.sum(gm, axis=0, keepdims=True)

    return pl.pallas_call(
        body, name=name, grid=(n,),
        in_specs=[pl.BlockSpec((T, C), lambda i: (i, 0)),
                  pl.BlockSpec((CONV_HALO, C), lambda i: (jnp.maximum(i * per - 1, 0), 0)),
                  pl.BlockSpec((T, C), lambda i: (i, 0)),
                  pl.BlockSpec((CONV_HALO, C), lambda i: (jnp.minimum((i + 1) * per, last_halo), 0)),
                  pl.BlockSpec(w.shape, lambda i: (0, 0))],
        out_specs=[pl.BlockSpec((T, C), lambda i: (i, 0)),
                   pl.BlockSpec(w.shape, lambda i: (0, 0)),
                   pl.BlockSpec((1, C), lambda i: (0, 0))],
        out_shape=[jax.ShapeDtypeStruct((L, C), F32), jax.ShapeDtypeStruct(w.shape, F32),
                   jax.ShapeDtypeStruct((1, C), F32)],
        scratch_shapes=[pltpu.VMEM((T + CONV_HALO, C), F32), pltpu.VMEM((T + CONV_HALO, C), F32)],
        compiler_params=_cparams(("arbitrary",)),
    )(x, x, g, g, w)


def dwconv(x, w, b, name):
    T = min(512, x.shape[0])

    @jax.custom_vjp
    def op(x, w, b):
        return _dwconv_fwd_call(x, w, b, T, name + '_fwd')

    def op_fwd(x, w, b):
        return _dwconv_fwd_call(x, w, b, T, name + '_fwd'), (x, w)

    def op_bwd(res, g):
        x, w = res
        return tuple(_dwconv_bwd_call(x, w, g, T, name + '_bwd'))

    op.defvjp(op_fwd, op_bwd)
    return op(x, w, b)


def _ssm_fwd_call(u, ar, ai, bbr, bbi, cr, ci, d, T, name):
    L, C = u.shape
    S = SSM_LANES

    def body(u_ref, ar_ref, ai_ref, bbr_ref, bbi_ref, cr_ref, ci_ref, d_ref,
             y_ref, xr_ref, xi_ref, st_r, st_i, in_r, in_i, out_r, out_i):
        i = pl.program_id(0)

        @pl.when(i == 0)
        def _():
            st_r[...] = jnp.zeros_like(st_r)
            st_i[...] = jnp.zeros_like(st_i)

        u_blk = u_ref[...]
        xr_ref[...] = _dg(u_blk, bbr_ref[...], 1, 0)
        xi_ref[...] = _dg(u_blk, bbi_ref[...], 1, 0)
        a_r, a_i = ar_ref[...], ai_ref[...]

        def tile(t, carry):
            sr, si = carry
            r0 = pl.multiple_of(t * SUBLANES, SUBLANES)
            in_r[...] = xr_ref[pl.ds(r0, SUBLANES), :]
            in_i[...] = xi_ref[pl.ds(r0, SUBLANES), :]
            for r in range(SUBLANES):
                nr = a_r * sr - a_i * si + in_r[r:r + 1, :]
                ni = a_r * si + a_i * sr + in_i[r:r + 1, :]
                sr, si = nr, ni
                out_r[r:r + 1, :] = sr
                out_i[r:r + 1, :] = si
            xr_ref[pl.ds(r0, SUBLANES), :] = out_r[...]
            xi_ref[pl.ds(r0, SUBLANES), :] = out_i[...]
            return sr, si

        sr, si = lax.fori_loop(0, T // SUBLANES, tile, (st_r[0:1, :], st_i[0:1, :]))
        st_r[0:1, :] = sr
        st_i[0:1, :] = si
        y_ref[...] = (_dg(xr_ref[...], cr_ref[...], 1, 0) - _dg(xi_ref[...], ci_ref[...], 1, 0)
                      + d_ref[...] * u_blk)

    full = lambda a: pl.BlockSpec(a.shape, lambda i: (0, 0))
    return pl.pallas_call(
        body, name=name, grid=(L // T,),
        in_specs=[pl.BlockSpec((T, C), lambda i: (i, 0))] + [full(a) for a in (ar, ai, bbr, bbi, cr, ci, d)],
        out_specs=[pl.BlockSpec((T, C), lambda i: (i, 0)), pl.BlockSpec((T, S), lambda i: (i, 0)),
                   pl.BlockSpec((T, S), lambda i: (i, 0))],
        out_shape=[jax.ShapeDtypeStruct((L, C), F32), jax.ShapeDtypeStruct((L, S), F32),
                   jax.ShapeDtypeStruct((L, S), F32)],
        scratch_shapes=[pltpu.VMEM((SUBLANES, S), F32)] * 6,
        compiler_params=_cparams(("arbitrary",)),
    )(u, ar, ai, bbr, bbi, cr, ci, d)


def _ssm_bwd_call(u, xr, xi, dy, ar, ai, bbr, bbi, cr, ci, d, T, name):
    L, C = u.shape
    S = SSM_LANES
    n = L // T
    per = T // SUBLANES

    def body(u_ref, xr_ref, xi_ref, hr_ref, hi_ref, dy_ref, ar_ref, ai_ref, bbr_ref, bbi_ref, cr_ref, ci_ref,
             d_ref, du_ref, dar_ref, dai_ref, dbr_ref, dbi_ref, dcr_ref, dci_ref, dd_ref,
             lam_r, lam_i, prev_r, prev_i, st_r, st_i, in_r, in_i, out_r, out_i):
        i = pl.program_id(0)
        chunk = n - 1 - i

        @pl.when(i == 0)
        def _():
            st_r[...] = jnp.zeros_like(st_r)
            st_i[...] = jnp.zeros_like(st_i)
            for ref in (dar_ref, dai_ref, dbr_ref, dbi_ref, dcr_ref, dci_ref, dd_ref):
                ref[...] = jnp.zeros_like(ref)

        dy_blk = dy_ref[...]
        u_blk = u_ref[...]
        lam_r[...] = _dg(dy_blk, cr_ref[...], 1, 1)
        lam_i[...] = -_dg(dy_blk, ci_ref[...], 1, 1)
        dcr_ref[...] += _dg(xr_ref[...], dy_blk, 0, 0)
        dci_ref[...] -= _dg(xi_ref[...], dy_blk, 0, 0)
        a_r, a_i = ar_ref[...], ai_ref[...]

        def tile(k, carry):
            lr, li = carry
            r0 = pl.multiple_of((per - 1 - k) * SUBLANES, SUBLANES)
            in_r[...] = lam_r[pl.ds(r0, SUBLANES), :]
            in_i[...] = lam_i[pl.ds(r0, SUBLANES), :]
            for r in range(SUBLANES - 1, -1, -1):
                nr = in_r[r:r + 1, :] + a_r * lr + a_i * li
                ni = in_i[r:r + 1, :] + a_r * li - a_i * lr
                lr, li = nr, ni
                out_r[r:r + 1, :] = lr
                out_i[r:r + 1, :] = li
            lam_r[pl.ds(r0, SUBLANES), :] = out_r[...]
            lam_i[pl.ds(r0, SUBLANES), :] = out_i[...]
            return lr, li

        lr, li = lax.fori_loop(0, per, tile, (st_r[0:1, :], st_i[0:1, :]))
        st_r[0:1, :] = lr
        st_i[0:1, :] = li

        l_r, l_i = lam_r[...], lam_i[...]
        du_ref[...] = _dg(l_r, bbr_ref[...], 1, 1) + _dg(l_i, bbi_ref[...], 1, 1) + d_ref[...] * dy_blk
        dbr_ref[...] += _dg(u_blk, l_r, 0, 0)
        dbi_ref[...] += _dg(u_blk, l_i, 0, 0)
        dd_ref[...] += jnp.sum(dy_blk * u_blk, axis=0, keepdims=True)

        prev_r[0:SUBLANES, :] = jnp.where(chunk > 0, hr_ref[...], 0.0)
        prev_i[0:SUBLANES, :] = jnp.where(chunk > 0, hi_ref[...], 0.0)
        prev_r[SUBLANES:SUBLANES + T, :] = xr_ref[...]
        prev_i[SUBLANES:SUBLANES + T, :] = xi_ref[...]
        p_r = prev_r[SUBLANES - 1:SUBLANES - 1 + T, :]
        p_i = prev_i[SUBLANES - 1:SUBLANES - 1 + T, :]
        dar_ref[...] += jnp.sum(l_r * p_r + l_i * p_i, axis=0, keepdims=True)
        dai_ref[...] += jnp.sum(l_i * p_r - l_r * p_i, axis=0, keepdims=True)

    rev = lambda w: pl.BlockSpec((T, w), lambda i: (n - 1 - i, 0))
    halo = pl.BlockSpec((SUBLANES, S), lambda i: (jnp.maximum((n - 1 - i) * per - 1, 0), 0))
    full = lambda a: pl.BlockSpec(a.shape, lambda i: (0, 0))
    params = (ar, ai, bbr, bbi, cr, ci, d)
    return pl.pallas_call(
        body, name=name, grid=(n,),
        in_specs=[rev(C), rev(S), rev(S), halo, halo, rev(C)] + [full(a) for a in params],
        out_specs=[rev(C)] + [full(a) for a in params],
        out_shape=[jax.ShapeDtypeStruct((L, C), F32)] + [jax.ShapeDtypeStruct(a.shape, F32) for a in params],
        scratch_shapes=[pltpu.VMEM((T, S), F32), pltpu.VMEM((T, S), F32),
                        pltpu.VMEM((T + SUBLANES, S), F32), pltpu.VMEM((T + SUBLANES, S), F32)]
        + [pltpu.VMEM((SUBLANES, S), F32)] * 6,
        compiler_params=_cparams(("arbitrary",)),
    )(u, xr, xi, xr, xi, dy, *params)


def ssm_core(u, ar, ai, bbr, bbi, cr, ci, d, name):
    T = min(256, u.shape[0])

    @jax.custom_vjp
    def op(u, ar, ai, bbr, bbi, cr, ci, d):
        return _ssm_fwd_call(u, ar, ai, bbr, bbi, cr, ci, d, T, name + '_fwd')[0]

    def op_fwd(u, ar, ai, bbr, bbi, cr, ci, d):
        y, xr, xi = _ssm_fwd_call(u, ar, ai, bbr, bbi, cr, ci, d, T, name + '_fwd')
        return y, (u, xr, xi, ar, ai, bbr, bbi, cr, ci, d)

    def op_bwd(res, dy):
        u, xr, xi, ar, ai, bbr, bbi, cr, ci, d = res
        return tuple(_ssm_bwd_call(u, xr, xi, dy, ar, ai, bbr, bbi, cr, ci, d, T, name + '_bwd'))

    op.defvjp(op_fwd, op_bwd)
    return op(u, ar, ai, bbr, bbi, cr, ci, d)


@jax.custom_vjp
def _block_diag(blocks):
    G, R, Cc = blocks.shape
    eye = jnp.eye(G, dtype=blocks.dtype)
    return (blocks[:, :, None, :] * eye[:, None, :, None]).reshape(G * R, G * Cc)


def _block_diag_fwd(blocks):
    return _block_diag(blocks), blocks.shape


def _block_diag_bwd(shape, g):
    G, R, Cc = shape
    return (jnp.stack([g[k * R:(k + 1) * R, k * Cc:(k + 1) * Cc] for k in range(G)]),)


_block_diag.defvjp(_block_diag_fwd, _block_diag_bwd)


def ssm_discretise(lam_re, lam_im, log_dt, b_re, b_im, c_re, c_im):
    dt = jnp.exp(log_dt)[:, None]
    mag = jnp.exp(lam_re * dt)
    ar, ai = mag * jnp.cos(lam_im * dt), mag * jnp.sin(lam_im * dt)
    den = lam_re * lam_re + lam_im * lam_im
    fr = ((ar - 1.0) * lam_re + ai * lam_im) / den
    fi = (ai * lam_re - (ar - 1.0) * lam_im) / den
    bbr = fr[..., None] * b_re - fi[..., None] * b_im
    bbi = fr[..., None] * b_im + fi[..., None] * b_re
    return (ar.reshape(1, SSM_LANES), ai.reshape(1, SSM_LANES),
            _block_diag(bbr.transpose(0, 2, 1)), _block_diag(bbi.transpose(0, 2, 1)),
            _block_diag(c_re.transpose(0, 2, 1)), _block_diag(c_im.transpose(0, 2, 1)))


def split_columns(p, bounds):
    @jax.custom_vjp
    def op(p):
        return tuple(p[:, lo:hi] for lo, hi in zip(bounds[:-1], bounds[1:]))

    def op_fwd(p):
        return op(p), None

    def op_bwd(_, gs):
        return (jnp.concatenate(gs, axis=1),)

    op.defvjp(op_fwd, op_bwd)
    return op(p)


DEPTH = 2
EARLY = ['w_in', 'conv_pw2_w', 'ssm_glu_w']
LATE = [n for n in MATRICES if n not in EARLY]
FFN = ['ffn_w_in', 'ffn_w_out']
GATHER_AT = {
    'start': [('w_in', 0)],
    'sb0': [(n, 0) for n in MATRICES if n != 'w_in'],
    'ffn0': [(n, 1) for n in MATRICES if n not in FFN],
    'sb1': [(n, 1) for n in FFN],
}
SCATTER_AT = {
    'sb1': [(n, 1) for n in LATE],
    'ffn0': [(n, 1) for n in EARLY],
    'sb0': [(n, 0) for n in LATE],
    'end': [(n, 0) for n in EARLY],
}


def _assemble(name, gathered):
    if name not in SLOTTED:
        return gathered
    return jnp.concatenate([gathered[j] for j in range(N_CHIPS)], axis=SHARD_AXIS[name] - 1)


def local_loss(slots, w, mats, shards, x, mem, target):
    mats = dict(mats)
    slot = {key: slots[key] for key in SCATTER_AT['end']}
    s1, s2, s3 = SB_WIDTH, 2 * SB_WIDTH, 3 * SB_WIDTH
    s4 = s3 + 2 * CONV_CH

    def riders_at(host):
        return Riders(gather=[(n, shards[(n, l)]) for n, l in GATHER_AT[host]],
                      scatter=[(n, slots[(n, l)]) for n, l in SCATTER_AT[host]])

    def take(host, gathered, handed):
        for (n, l), g in zip(GATHER_AT[host], gathered):
            mats[(n, l)] = _assemble(n, g)
        for key, s in zip(SCATTER_AT[host], handed):
            slot[key] = s

    def linear_(x, n, l, name, residual=None, host=None):
        y, gathered, handed = linear(x, mats[(n, l)], slot[(n, l)], name, residual,
                                     riders_at(host) if host else NO_RIDERS)
        if host:
            take(host, gathered, handed)
        return y

    def fused_(f, rows, params, n, l, name, residual=None, host=None, carry=(), block_rows=None):
        y, carried, gathered, handed = rowwise_linear(
            f, rows, params, [], mats[(n, l)], slot[(n, l)], name, residual,
            riders_at(host) if host else NO_RIDERS, carry, block_rows)
        if host:
            take(host, gathered, handed)
        return (y,) + tuple(carried)

    def gain(n, l):
        return w[n][l].reshape(1, -1)

    for l in range(DEPTH):
        tag = 'l%d_' % l
        p, x = fused_(_rms, [x], [gain('norm_mix_g', l)], 'w_in', l, tag + 'w_in', carry=(0,))
        q, k, v, u_conv, u_ssm = split_columns(p, (0, s1, s2, s3, s4, p.shape[1]))
        q = groupnorm(q, w['sb_q_norm_g'][l], SB_HEAD_DIM, tag + 'q_norm')
        k = groupnorm(k, w['sb_k_norm_g'][l], SB_HEAD_DIM, tag + 'k_norm')
        o_sb, gathered, handed = sb_attention(q, k, v, tag + 'sb', riders_at('sb%d' % l))
        take('sb%d' % l, gathered, handed)

        dw_w = jnp.pad(w['conv_dw_w'][l], ((0, CONV_HALO - CONV_WIDTH), (0, 0)))
        hc = dwconv(glu(u_conv, tag + 'conv_glu'), dw_w, w['conv_dw_b'][l].reshape(1, -1), tag + 'dwconv')
        o_conv, = fused_(ln_silu_block, [hc], [gain('conv_ln_g', l), gain('conv_ln_b', l)], 'conv_pw2_w', l,
                         tag + 'pw2')

        ar, ai, bbr, bbi, cr, ci = ssm_discretise(
            w['ssm_lam_re'][l], w['ssm_lam_im'][l], w['ssm_log_dt'][l], w['ssm_b_re'][l], w['ssm_b_im'][l],
            w['ssm_c_re'][l], w['ssm_c_im'][l])
        y = ssm_core(u_ssm, ar, ai, bbr, bbi, cr, ci, w['ssm_d'][l].reshape(1, -1), tag + 'ssm')
        o_ssm = glu(linear_(y, 'ssm_glu_w', l, tag + 'ssm_glu_w'), tag + 'ssm_glu')

        x, = fused_(branch_norms_block, [o_sb, o_conv, o_ssm], [gain('branch_norm_g', l)], 'w_out', l,
                    tag + 'w_out', residual=x)

        q_raw, x = fused_(_rms, [x], [gain('norm_xa_g', l)], 'xa_wq', l, tag + 'xa_wq', carry=(0,))
        hm = rmsnorm(mem, w['norm_mem_g'][l], tag + 'norm_mem')
        qx = groupnorm(q_raw, w['xa_q_norm_g'][l], XA_HEAD_DIM, tag + 'xa_qn')
        kx = groupnorm(linear_(hm, 'xa_wk', l, tag + 'xa_wk'), w['xa_k_norm_g'][l], XA_HEAD_DIM, tag + 'xa_kn')
        vx = linear_(hm, 'xa_wv', l, tag + 'xa_wv')
        x, = fused_(xa_core_block, [qx], [kx, vx], 'xa_wo', l, tag + 'xa_wo', residual=x,
                    block_rows=min(256, qx.shape[0]))

        gu, x = fused_(_rms, [x], [gain('norm_ffn_g', l)], 'ffn_w_in', l, tag + 'ffn_in',
                       host='ffn0' if l == 0 else None, carry=(0,))
        x, = fused_(swiglu_block, [gu], [], 'ffn_w_out', l, tag + 'ffn_out', residual=x)
    return jnp.sum(loss_rows(x, target, 'loss'))


def local_step(w, mats, shards, x, mem, target):
    def shard_shape(key):
        return shards[key].shape if key in shards else _shard_shape(key[0], mats[key].shape)

    slots = {}
    for host, keys in SCATTER_AT.items():
        for key in keys:
            shape = _whole_shape(key[0], shard_shape(key)) if host == 'end' else (N_CHIPS,) + shard_shape(key)
            slots[key] = jnp.zeros(shape, BF16)
    loss, (g_mats, g_w, gx) = jax.value_and_grad(local_loss, argnums=(0, 1, 4))(
        slots, w, mats, shards, x, mem, target)
    return loss, gx, g_w, g_mats


PACK_ROWS = 2048


PIECE_ROWS = 16


def _piece_rows(size):
    rows = -(-size // LANES)
    return rows, -(-rows // PIECE_ROWS) * PIECE_ROWS


def pack(arrays, dtype):
    parts, total = [], 0
    for a in arrays:
        rows, padded = _piece_rows(a.size)
        a = a.astype(dtype)
        if a.size % LANES:
            a = jnp.pad(a.reshape(-1), (0, rows * LANES - a.size))
        a = a.reshape(rows, LANES)
        if padded != rows:
            a = jnp.pad(a, ((0, padded - rows), (0, 0)))
        parts.append(a)
        total += padded
    tail = -total % PACK_ROWS
    if tail:
        parts.append(jnp.zeros((tail, LANES), dtype))
    return jnp.concatenate(parts, axis=0)


def unpack(packed, shapes):
    out, off = [], 0
    for s in shapes:
        size = math.prod(s)
        rows, padded = _piece_rows(size)
        piece = packed[off:off + rows]
        if size % LANES:
            piece = piece.reshape(-1)[:size]
        out.append(piece.reshape(s))
        off += padded
    return out


def _mesh_pos():
    return lax.axis_index("x"), lax.axis_index("y"), lax.axis_index("c")


def _exchange_xy(n_arrays, src_of, dst_of, sems, wait):
    send_sems, recv_sems, local_sems = sems
    x, y, c = _mesh_pos()
    me = 2 * x + y
    peers = [(1 - x, y), (x, 1 - y), (1 - x, 1 - y)]
    for k in range(n_arrays):
        own = pltpu.make_async_copy(src_of(k, me), dst_of(k, me), local_sems.at[k])
        if wait:
            own.wait()
        else:
            own.start()
        for p, (px, py) in enumerate(peers):
            out = pltpu.make_async_remote_copy(
                src_ref=src_of(k, 2 * px + py), dst_ref=dst_of(k, me), send_sem=send_sems.at[3 * k + p],
                recv_sem=recv_sems.at[3 * k + p], device_id=(px, py, c), device_id_type=MESH)
            if wait:
                pltpu.make_async_remote_copy(
                    src_ref=src_of(k, me), dst_ref=dst_of(k, 2 * px + py), send_sem=send_sems.at[3 * k + p],
                    recv_sem=recv_sems.at[3 * k + p], device_id=(px, py, c), device_id_type=MESH).wait_recv()
                out.wait_send()
            else:
                out.start()


class Ride:
    def __init__(self, arrays, out_shapes, src_of, dst_of):
        self.arrays, self.out_shapes = list(arrays), list(out_shapes)
        self._src_of, self._dst_of = src_of, dst_of
        n = len(self.arrays)
        self.in_specs = [pl.BlockSpec(memory_space=pl.ANY)] * n
        self.out_specs = [pl.BlockSpec(memory_space=pl.ANY)] * len(self.out_shapes)
        self.out_shape = [jax.ShapeDtypeStruct(s, BF16) for s in self.out_shapes]
        self.scratch = [pltpu.SemaphoreType.DMA((3 * n,)), pltpu.SemaphoreType.DMA((3 * n,)),
                        pltpu.SemaphoreType.DMA((n,))]

    def run(self, parts, wait):
        ins, outs, sems = parts
        _exchange_xy(len(self.arrays), lambda k, chip: self._src_of(ins, k, chip),
                     lambda k, chip: self._dst_of(outs, k, chip), sems, wait)

    def at_ends(self, parts, first, last):
        pl.when(first)(lambda: self.run(parts, False))

        def finish():
            pl.when(last)(lambda: self.run(parts, True))
        return finish


def _split_refs(ride, n_in, n_out, refs):
    if ride is None:
        return refs[:n_in], refs[n_in:n_in + n_out], refs[n_in + n_out:], None
    ni, no = len(ride.arrays), len(ride.out_shapes)
    b = n_in + ni
    c = b + n_out
    d = c + no
    return refs[:n_in], refs[b:c], refs[d:len(refs) - 3], (refs[n_in:b], refs[c:d], refs[len(refs) - 3:])


def run_ride(ride, name):
    def body(*refs):
        parts = _split_refs(ride, 0, 0, refs)[3]
        ride.run(parts, False)
        ride.run(parts, True)

    return pl.pallas_call(
        body, name=name, in_specs=ride.in_specs, out_specs=ride.out_specs, out_shape=ride.out_shape,
        scratch_shapes=ride.scratch, compiler_params=pltpu.CompilerParams(has_side_effects=True),
    )(*ride.arrays)


SLOTTED = ('w_in', 'taps')


def _part(ref, axis, chip, size):
    start = pl.multiple_of(chip * size, size)
    index = [slice(None)] * len(ref.shape)
    index[axis] = pl.ds(start, size)
    return ref.at[tuple(index)]


def _whole_shape(name, shard_shape):
    s = list(shard_shape)
    s[SHARD_AXIS[name] - 1] *= N_CHIPS
    return tuple(s)


def _shard_shape(name, whole_shape):
    s = list(whole_shape)
    s[SHARD_AXIS[name] - 1] //= N_CHIPS
    return tuple(s)


def gather_ride(names, shards):
    def out_shape(k):
        return (N_CHIPS,) + shards[k].shape if names[k] in SLOTTED else _whole_shape(names[k], shards[k].shape)

    def dst_of(outs, k, chip):
        if names[k] in SLOTTED:
            return outs[k].at[chip]
        axis = SHARD_AXIS[names[k]] - 1
        return _part(outs[k], axis, chip, shards[k].shape[axis])

    return Ride(shards, [out_shape(k) for k in range(len(shards))], lambda ins, k, chip: ins[k], dst_of)


def scatter_ride(names, grads):
    def shard_shape(k):
        return grads[k].shape[1:] if names[k] in SLOTTED else _shard_shape(names[k], grads[k].shape)

    def src_of(ins, k, chip):
        if names[k] in SLOTTED:
            return ins[k].at[chip]
        axis = SHARD_AXIS[names[k]] - 1
        return _part(ins[k], axis, chip, shard_shape(k)[axis])

    return Ride(grads, [(N_CHIPS,) + shard_shape(k) for k in range(len(grads))], src_of,
                lambda outs, k, chip: outs[k].at[chip])


def _cut_for_chips(name, g):
    if name not in SLOTTED:
        return g
    return jnp.stack([_shard_of(g, SHARD_AXIS[name] - 1, j) for j in range(N_CHIPS)])


def swap_cores(arrays, name):
    na = len(arrays)

    def body(*refs):
        ins, outs, send_sems, recv_sems = refs[:na], refs[na:2 * na], refs[2 * na], refs[2 * na + 1]
        x, y, c = _mesh_pos()
        copies = [pltpu.make_async_remote_copy(
            src_ref=ins[k], dst_ref=outs[k], send_sem=send_sems.at[k], recv_sem=recv_sems.at[k],
            device_id=(x, y, 1 - c), device_id_type=MESH) for k in range(na)]
        for cp in copies:
            cp.start()
        for cp in copies:
            cp.wait()

    return pl.pallas_call(
        body, name=name,
        in_specs=[pl.BlockSpec(memory_space=pl.ANY)] * na,
        out_specs=[pl.BlockSpec(memory_space=pl.ANY)] * na,
        out_shape=[jax.ShapeDtypeStruct(a.shape, a.dtype) for a in arrays],
        scratch_shapes=[pltpu.SemaphoreType.DMA((na,)), pltpu.SemaphoreType.DMA((na,))],
        compiler_params=pltpu.CompilerParams(has_side_effects=True),
    )(*arrays)


def allreduce_small(buf, name):
    R = buf.shape[0]

    def body(in_ref, sum_ref, all_ref, send_sems, recv_sems):
        x, y, c = _mesh_pos()
        me = 4 * x + 2 * y + c
        all_ref[me] = in_ref[...]
        flips = [(fx, fy, fc) for fx in (0, 1) for fy in (0, 1) for fc in (0, 1)][1:]
        sends = []
        for k, (fx, fy, fc) in enumerate(flips):
            cp = pltpu.make_async_remote_copy(
                src_ref=in_ref, dst_ref=all_ref.at[me], send_sem=send_sems.at[k], recv_sem=recv_sems.at[k],
                device_id=(x ^ fx, y ^ fy, c ^ fc), device_id_type=MESH)
            cp.start()
            sends.append(cp)
        for k, (fx, fy, fc) in enumerate(flips):
            peer = 4 * (x ^ fx) + 2 * (y ^ fy) + (c ^ fc)
            pltpu.make_async_remote_copy(
                src_ref=in_ref, dst_ref=all_ref.at[peer], send_sem=send_sems.at[k], recv_sem=recv_sems.at[k],
                device_id=(x ^ fx, y ^ fy, c ^ fc), device_id_type=MESH).wait_recv()
        for cp in sends:
            cp.wait_send()
        acc = all_ref[0]
        for k in range(1, N_DEV):
            acc = acc + all_ref[k]
        sum_ref[...] = acc

    return pl.pallas_call(
        body, name=name,
        in_specs=[pl.BlockSpec(memory_space=pltpu.VMEM)],
        out_specs=[pl.BlockSpec(memory_space=pltpu.VMEM), pl.BlockSpec(memory_space=pltpu.VMEM)],
        out_shape=[jax.ShapeDtypeStruct((R, LANES), F32), jax.ShapeDtypeStruct((N_DEV, R, LANES), F32)],
        scratch_shapes=[pltpu.SemaphoreType.DMA((N_DEV - 1,)), pltpu.SemaphoreType.DMA((N_DEV - 1,))],
        compiler_params=pltpu.CompilerParams(has_side_effects=True, vmem_limit_bytes=VMEM_LIMIT),
    )(buf)[0]


def _adamw_update(g, w, m, v):
    m2 = ADAM_B1 * m + (1.0 - ADAM_B1) * g
    v2 = ADAM_B2 * v + (1.0 - ADAM_B2) * (g * g)
    m_hat = m2 / (1.0 - ADAM_B1 ** ADAM_STEP)
    v_hat = v2 / (1.0 - ADAM_B2 ** ADAM_STEP)
    return -ADAM_LR * (m_hat / (jnp.sqrt(v_hat) + ADAM_EPS) + ADAM_WD * w), m2, v2


def adamw_matrix(layer, mine, other, w, m, v, so_far, name):
    _, rows, cols = w.shape
    T = 16
    while rows % (2 * T) == 0 and 2 * T * cols <= 128 * 1024:
        T *= 2

    def body(mine_ref, other_ref, w_ref, m_ref, v_ref, *rest):
        g_out, d_out, m_out, v_out = rest[-4:]

        def total(ref):
            acc = ref[0].astype(F32)
            for k in range(1, N_CHIPS):
                acc = acc + ref[k].astype(F32)
            return acc

        g = total(mine_ref) + total(other_ref)
        g_out[...] = g
        d_out[...], m_out[...], v_out[...] = _adamw_update(g, w_ref[...], m_ref[...], v_ref[...])

    slots = pl.BlockSpec((N_CHIPS, T, cols), lambda i: (0, i, 0))
    spec = pl.BlockSpec((None, T, cols), lambda i: (layer, i, 0))
    filled = [] if so_far is None else list(so_far)
    return pl.pallas_call(
        body, name=name, grid=(rows // T,),
        in_specs=[slots, slots, spec, spec, spec] + [pl.BlockSpec(memory_space=pl.ANY)] * len(filled),
        out_specs=[spec] * 4,
        out_shape=[jax.ShapeDtypeStruct(w.shape, F32)] * 4,
        input_output_aliases={5 + j: j for j in range(len(filled))},
        compiler_params=_cparams(("parallel",)),
    )(mine, other, w, m, v, *filled)


def adamw_small(gs, ws, ms, vs, name):
    n = len(gs)

    def body(*refs):
        for k in range(n):
            g, w, m, v = (refs[j * n + k][...] for j in range(4))
            d_out, m_out, v_out = (refs[(4 + j) * n + k] for j in range(3))
            d_out[...], m_out[...], v_out[...] = _adamw_update(g, w, m, v)

    vmem = pl.BlockSpec(memory_space=pltpu.VMEM)
    outs = pl.pallas_call(
        body, name=name,
        in_specs=[vmem] * (4 * n), out_specs=[vmem] * (3 * n),
        out_shape=[jax.ShapeDtypeStruct(w.shape, F32) for w in ws] * 3,
        compiler_params=_cparams(),
    )(*gs, *ws, *ms, *vs)
    return outs[:n], outs[n:2 * n], outs[2 * n:]


def _shard_of(full, axis, chip):
    size = full.shape[axis] // N_CHIPS
    return lax.slice_in_dim(full, chip * size, (chip + 1) * size, axis=axis)


def kernel(x, mem, norm_mix_g, w_in, sb_q_norm_g, sb_k_norm_g, conv_dw_w, conv_dw_b, conv_ln_g, conv_ln_b, conv_pw2_w, ssm_lam_re, ssm_lam_im, ssm_log_dt, ssm_b_re, ssm_b_im, ssm_c_re, ssm_c_im, ssm_d, ssm_glu_w, branch_norm_g, w_out, norm_xa_g, norm_mem_g, xa_wq, xa_wk, xa_wv, xa_q_norm_g, xa_k_norm_g, xa_wo, norm_ffn_g, ffn_w_in, ffn_w_out, loss_target, m_norm_mix_g, m_w_in, m_sb_q_norm_g, m_sb_k_norm_g, m_conv_dw_w, m_conv_dw_b, m_conv_ln_g, m_conv_ln_b, m_conv_pw2_w, m_ssm_lam_re, m_ssm_lam_im, m_ssm_log_dt, m_ssm_b_re, m_ssm_b_im, m_ssm_c_re, m_ssm_c_im, m_ssm_d, m_ssm_glu_w, m_branch_norm_g, m_w_out, m_norm_xa_g, m_norm_mem_g, m_xa_wq, m_xa_wk, m_xa_wv, m_xa_q_norm_g, m_xa_k_norm_g, m_xa_wo, m_norm_ffn_g, m_ffn_w_in, m_ffn_w_out, v_norm_mix_g, v_w_in, v_sb_q_norm_g, v_sb_k_norm_g, v_conv_dw_w, v_conv_dw_b, v_conv_ln_g, v_conv_ln_b, v_conv_pw2_w, v_ssm_lam_re, v_ssm_lam_im, v_ssm_log_dt, v_ssm_b_re, v_ssm_b_im, v_ssm_c_re, v_ssm_c_im, v_ssm_d, v_ssm_glu_w, v_branch_norm_g, v_w_out, v_norm_xa_g, v_norm_mem_g, v_xa_wq, v_xa_wk, v_xa_wv, v_xa_q_norm_g, v_xa_k_norm_g, v_xa_wo, v_norm_ffn_g, v_ffn_w_in, v_ffn_w_out):
    given = dict(locals())
    w = {n: given[n] for n in WEIGHTS}
    m = {n: given['m_' + n] for n in WEIGHTS}
    v = {n: given['v_' + n] for n in WEIGHTS}

    depth = w_in.shape[0]
    chip = 2 * lax.axis_index("x") + lax.axis_index("y")

    assert depth == DEPTH
    taps_bits = lax.bitcast_convert_type(conv_dw_w, BF16)
    shards = {(n, l): w[n][l].astype(BF16) for n in MATRICES for l in range(depth)}
    first = [shards.pop(key) for key in GATHER_AT['start']]
    gathered = run_ride(gather_ride([n for n, _ in GATHER_AT['start']] + ['taps'], first + [taps_bits]),
                        'gather_first')
    mats = {key: _assemble(key[0], g) for key, g in zip(GATHER_AT['start'], gathered)}
    taps = jnp.concatenate([lax.bitcast_convert_type(gathered[-1][j], F32) for j in range(N_CHIPS)], axis=2)
    local_w = {n: w[n] for n in REPLICATED}
    local_w['conv_dw_w'] = taps

    loss, gx, g_w, mine = local_step(local_w, mats, shards, x[0], mem[0], loss_target[0])
    last = SCATTER_AT['end']
    received = run_ride(scatter_ride([n for n, _ in last], [_cut_for_chips(n, mine[(n, l)]) for n, l in last]),
                        'scatter_last')
    mine.update(zip(last, received))

    keys = [(n, l) for n in MATRICES for l in range(depth)]
    other = dict(zip(keys, swap_cores([mine[key] for key in keys], 'swap_cores')))
    outs = {}
    for n in MATRICES:
        res = None
        for l in range(depth):
            res = adamw_matrix(l, mine[(n, l)], other[(n, l)], w[n], m[n], v[n], res, 'adamw_%s_%d' % (n, l))
        outs['grad_' + n], outs['delta_' + n], outs['new_m_' + n], outs['new_v_' + n] = res

    reduced = allreduce_small(pack([g_w[n] for n in REPLICATED] + [g_w['conv_dw_w'], loss.reshape(1)], F32),
                              'allreduce_small')
    reduced = unpack(reduced, [w[n].shape for n in REPLICATED] + [taps.shape, (1,)])
    total_loss = reduced[-1].reshape(())
    tap_cols = conv_dw_w.shape[2]
    reduced[-2] = lax.dynamic_slice_in_dim(reduced[-2], chip * tap_cols, tap_cols, axis=2)
    small_names = REPLICATED + ['conv_dw_w']
    deltas, new_ms, new_vs = adamw_small(reduced[:-1], [w[n] for n in small_names], [m[n] for n in small_names],
                                         [v[n] for n in small_names], 'adamw_small')
    for k, n in enumerate(small_names):
        outs['grad_' + n], outs['delta_' + n] = reduced[k], deltas[k]
        outs['new_m_' + n], outs['new_v_' + n] = new_ms[k], new_vs[k]
    return (total_loss, gx[None], *[outs['grad_' + n] for n in WEIGHTS], *[outs['delta_' + n] for n in WEIGHTS],
            *[outs['new_m_' + n] for n in WEIGHTS], *[outs['new_v_' + n] for n in WEIGHTS])
```

```python
import functools
import math

import jax
import jax.numpy as jnp
from jax import lax
from jax.experimental import pallas as pl
from jax.experimental.pallas import tpu as pltpu

F32 = jnp.float32
BF16 = jnp.bfloat16
MESH = pl.DeviceIdType.MESH
HIGHEST = lax.Precision.HIGHEST

EPS = 1e-6
LANES = 128
SUBLANES = 8
VMEM_LIMIT = 56 * 1024 * 1024

SB_HEAD_DIM = 64
SB_WIDTH = 512
CONV_CH = 256
CONV_WIDTH = 31
CONV_HALO = 32
SSM_CH = 256
SSM_GROUPS = 16
SSM_GROUP = 16
SSM_STATE = 64
SSM_LANES = SSM_GROUPS * SSM_STATE
XA_HEADS = 4
XA_HEAD_DIM = 256
SB_CUT = 110.0

ADAM_LR = 0.001
ADAM_B1 = 0.9
ADAM_B2 = 0.999
ADAM_EPS = 1e-08
ADAM_WD = 0.01
ADAM_STEP = 10

WEIGHTS = ['norm_mix_g', 'w_in', 'sb_q_norm_g', 'sb_k_norm_g', 'conv_dw_w', 'conv_dw_b', 'conv_ln_g',
           'conv_ln_b', 'conv_pw2_w', 'ssm_lam_re', 'ssm_lam_im', 'ssm_log_dt', 'ssm_b_re', 'ssm_b_im',
           'ssm_c_re', 'ssm_c_im', 'ssm_d', 'ssm_glu_w', 'branch_norm_g', 'w_out', 'norm_xa_g',
           'norm_mem_g', 'xa_wq', 'xa_wk', 'xa_wv', 'xa_q_norm_g', 'xa_k_norm_g', 'xa_wo', 'norm_ffn_g',
           'ffn_w_in', 'ffn_w_out']
SHARD_AXIS = {'w_in': 2, 'conv_dw_w': 2, 'conv_pw2_w': 1, 'ssm_glu_w': 2, 'w_out': 1, 'xa_wq': 1,
              'xa_wk': 1, 'xa_wv': 1, 'xa_wo': 1, 'ffn_w_in': 2, 'ffn_w_out': 1}
MATRICES = [n for n in WEIGHTS if n in SHARD_AXIS and n != 'conv_dw_w']
REPLICATED = [n for n in WEIGHTS if n not in SHARD_AXIS]
N_CHIPS = 4
N_DEV = 8


def _cparams(sem=None, **kw):
    if sem is not None:
        kw['dimension_semantics'] = sem
    return pltpu.CompilerParams(vmem_limit_bytes=VMEM_LIMIT, **kw)


def _pick(n, target):
    best = None
    d = LANES
    while d <= min(n, target):
        if n % d == 0:
            best = d
        d += LANES
    return best if best is not None else n


def _rows_for(n_rows, width):
    t = 512
    while t > 8 and t * width > 768 * 1024:
        t //= 2
    return min(t, n_rows)


def _dg(a, b, ca, cb):
    return lax.dot_general(a.astype(BF16), b.astype(BF16), (((ca,), (cb,)), ((), ())),
                           preferred_element_type=F32)


@jax.custom_vjp
def bdot_nn(a, b):
    return _dg(a, b, 1, 0)


def _bdot_nn_fwd(a, b):
    return _dg(a, b, 1, 0), (a, b)


def _bdot_nn_bwd(res, g):
    a, b = res
    return _dg(g, b, 1, 1), _dg(a, g, 0, 0)


bdot_nn.defvjp(_bdot_nn_fwd, _bdot_nn_bwd)


@jax.custom_vjp
def bdot_nt(a, b):
    return _dg(a, b, 1, 1)


def _bdot_nt_fwd(a, b):
    return _dg(a, b, 1, 1), (a, b)


def _bdot_nt_bwd(res, g):
    a, b = res
    return _dg(g, b, 1, 0), _dg(g, a, 0, 0)


bdot_nt.defvjp(_bdot_nt_fwd, _bdot_nt_bwd)


def mm(a, b, mode, name, out_dtype=F32, add=None, ride=None):
    if mode == 'nn':
        M, K = a.shape
        N = b.shape[1]
    elif mode == 'nt':
        M, K = a.shape
        N = b.shape[0]
    else:
        K, M = a.shape
        N = b.shape[1]
    if mode == 'tn':
        tm, tn, tk = _pick(M, 1536), _pick(N, 2816), _pick(K, 512)
    else:
        tm, tn = _pick(M, 1024), _pick(N, 1536)
        tk = K if K <= 2816 else _pick(K, 1536)
    nk = K // tk
    ca, cb = {'nn': (1, 0), 'nt': (1, 1), 'tn': (0, 0)}[mode]

    grid = (M // tm, N // tn, nk)

    def body(*refs):
        ins, (o_ref,), scratch, riding = _split_refs(ride, 3 if add is not None else 2, 1, refs)
        a_ref, b_ref = ins[:2]
        add_ref = ins[2] if add is not None else None
        i, j, k = pl.program_id(0), pl.program_id(1), pl.program_id(2)
        ride_done = None
        if riding is not None:
            first = jnp.logical_and(i == 0, jnp.logical_and(j == 0, k == 0))
            last = jnp.logical_and(i == grid[0] - 1, jnp.logical_and(j == grid[1] - 1, k == nk - 1))
            ride_done = ride.at_ends(riding, first, last)

        def finish(acc):
            if add_ref is not None:
                acc = acc + add_ref[...]
            o_ref[...] = acc.astype(out_dtype)

        if nk == 1:
            finish(_dg(a_ref[...], b_ref[...], ca, cb))
        else:
            acc_ref, = scratch

            @pl.when(k == 0)
            def _():
                acc_ref[...] = jnp.zeros_like(acc_ref)

            acc_ref[...] += _dg(a_ref[...], b_ref[...], ca, cb)

            @pl.when(k == nk - 1)
            def _():
                finish(acc_ref[...])

        if ride_done is not None:
            ride_done()

    if mode == 'nn':
        a_spec = pl.BlockSpec((tm, tk), lambda i, j, k: (i, k))
        b_spec = pl.BlockSpec((tk, tn), lambda i, j, k: (k, j))
    elif mode == 'nt':
        a_spec = pl.BlockSpec((tm, tk), lambda i, j, k: (i, k))
        b_spec = pl.BlockSpec((tn, tk), lambda i, j, k: (j, k))
    else:
        a_spec = pl.BlockSpec((tk, tm), lambda i, j, k: (k, i))
        b_spec = pl.BlockSpec((tk, tn), lambda i, j, k: (k, j))
    out_spec = pl.BlockSpec((tm, tn), lambda i, j, k: (i, j))
    own_in = [a_spec, b_spec] + ([out_spec] if add is not None else [])
    operands = (a, b) if add is None else (a, b, add)
    scratch = [pltpu.VMEM((tm, tn), F32)] if nk > 1 else []
    if ride is None:
        return pl.pallas_call(
            body, name=name, grid=grid, in_specs=own_in, out_specs=out_spec,
            out_shape=jax.ShapeDtypeStruct((M, N), out_dtype), scratch_shapes=scratch,
            compiler_params=_cparams(("parallel", "parallel", "arbitrary")),
        )(*operands)
    outs = pl.pallas_call(
        body, name=name, grid=grid, in_specs=own_in + ride.in_specs, out_specs=[out_spec] + ride.out_specs,
        out_shape=[jax.ShapeDtypeStruct((M, N), out_dtype)] + ride.out_shape,
        scratch_shapes=scratch + ride.scratch,
        compiler_params=_cparams(("arbitrary", "arbitrary", "arbitrary"), has_side_effects=True),
    )(*operands, *ride.arrays)
    return outs[0], outs[1:]


class Riders:
    def __init__(self, gather=(), scatter=()):
        self.gather_names = [n for n, _ in gather]
        self.shards = tuple(s for _, s in gather)
        self.scatter_names = [n for n, _ in scatter]
        self.slots = tuple(s for _, s in scatter)

    def gather_ride(self, shards):
        return gather_ride(self.gather_names, list(shards)) if shards else None

    def scatter_ride(self, grads):
        if not grads:
            return None
        return scatter_ride(self.scatter_names, [_cut_for_chips(n, g) for n, g in zip(self.scatter_names, grads)])

    def whole_slots(self, slots):
        return tuple(jnp.zeros(_whole_shape(n, s.shape[1:]), BF16) for n, s in zip(self.scatter_names, slots))


NO_RIDERS = Riders()


def linear(x, w, slot, name, residual=None, riders=NO_RIDERS):
    @jax.custom_vjp
    def op(x, w, slot, residual, shards, slots):
        ride = riders.gather_ride(shards)
        y = mm(x, w, 'nn', name + '_fwd', add=residual, ride=ride)
        y, gathered = y if ride is not None else (y, ())
        return y, tuple(gathered), riders.whole_slots(slots)

    def op_fwd(x, w, slot, residual, shards, slots):
        return op(x, w, slot, residual, shards, slots), (x, w, shards)

    def op_bwd(res, cts):
        x, w, shards = res
        g, _, slot_grads = cts
        ride = riders.scatter_ride(slot_grads)
        dx = mm(g, w, 'nt', name + '_dx', ride=ride)
        dx, received = dx if ride is not None else (dx, ())
        return (dx, jnp.zeros_like(w), mm(x, g, 'tn', name + '_dw', BF16), None if residual is None else g,
                tuple(jnp.zeros_like(s) for s in shards), tuple(received))

    op.defvjp(op_fwd, op_bwd)
    return op(x, w, slot, residual, riders.shards, riders.slots)


def _rowwise_calls(f, rows, params, consts, out_widths, name, need_row_grad=None, block_rows=None, carry=(),
                   out_dtype=F32):
    nr, npar, nc, nout = len(rows), len(params), len(consts), len(out_widths)
    carry = tuple(carry)
    L = rows[0].shape[0]
    widths = [r.shape[1] for r in rows]
    T = block_rows or _rows_for(L, max(widths + list(out_widths)))
    n = L // T
    need = list(need_row_grad) if need_row_grad is not None else [True] * nr
    pshapes = [p.shape for p in params]
    cshapes = [c.shape for c in consts]

    row_specs = [pl.BlockSpec((T, w), lambda i: (i, 0)) for w in widths]
    par_specs = [pl.BlockSpec(s, lambda i: (0, 0)) for s in pshapes]
    con_specs = [pl.BlockSpec(s, lambda i: (0, 0)) for s in cshapes]
    out_specs = [pl.BlockSpec((T, w), lambda i: (i, 0)) for w in out_widths]

    def fwd_call(rows, params, consts):
        def body(*refs):
            ins = [r[...] for r in refs[:nr + npar + nc]]
            outs = f(*ins)
            for o_ref, val in zip(refs[nr + npar + nc:], outs):
                o_ref[...] = val.astype(out_dtype)

        return pl.pallas_call(
            body, name=name + '_fwd', grid=(n,),
            in_specs=row_specs + par_specs + con_specs, out_specs=out_specs,
            out_shape=[jax.ShapeDtypeStruct((L, w), out_dtype) for w in out_widths],
            compiler_params=_cparams(("parallel",)),
        )(*rows, *params, *consts)

    def bwd_call(rows, params, consts, cts, carried):
        grad_rows = [k for k in range(nr) if need[k]]
        n_in = nr + npar + nc + nout

        def body(*refs):
            i = pl.program_id(0)
            rv = [r[...] for r in refs[:nr]]
            pv = [r[...] for r in refs[nr:nr + npar]]
            cv = [r[...] for r in refs[nr + npar:nr + npar + nc]]
            ctv = tuple(r[...] for r in refs[nr + npar + nc:n_in])
            carried_refs = dict(zip(carry, refs[n_in:n_in + len(carry)]))
            orefs = refs[n_in + len(carry):]
            _, vjp = jax.vjp(lambda *rp: tuple(f(*rp, *cv)), *rv, *pv)
            g = vjp(ctv)
            for slot, k in enumerate(grad_rows):
                orefs[slot][...] = g[k] + carried_refs[k][...] if k in carried_refs else g[k]

            @pl.when(i == 0)
            def _():
                for k in range(npar):
                    orefs[len(grad_rows) + k][...] = jnp.zeros(pshapes[k], F32)

            for k in range(npar):
                orefs[len(grad_rows) + k][...] += g[nr + k]

        outs = pl.pallas_call(
            body, name=name + '_bwd', grid=(n,),
            in_specs=row_specs + par_specs + con_specs + out_specs + [row_specs[k] for k in carry],
            out_specs=[row_specs[k] for k in grad_rows] + par_specs,
            out_shape=[jax.ShapeDtypeStruct((L, widths[k]), F32) for k in grad_rows]
            + [jax.ShapeDtypeStruct(s, F32) for s in pshapes],
            compiler_params=_cparams(("arbitrary",)),
        )(*rows, *params, *consts, *cts, *carried)
        drows = []
        slot = 0
        for k in range(nr):
            if need[k]:
                drows.append(outs[slot])
                slot += 1
            else:
                drows.append(jnp.zeros_like(rows[k]))
        return tuple(drows), tuple(outs[len(grad_rows):])

    return fwd_call, bwd_call


def rowwise(f, rows, params, consts, out_widths, name, need_row_grad=None, block_rows=None, carry=()):
    fwd_call, bwd_call = _rowwise_calls(f, rows, params, consts, out_widths, name, need_row_grad, block_rows, carry)
    nout = len(out_widths)

    @jax.custom_vjp
    def op(rows, params, consts):
        return tuple(fwd_call(rows, params, consts)) + tuple(rows[k] for k in carry)

    def op_fwd(rows, params, consts):
        return op(rows, params, consts), (rows, params, consts)

    def op_bwd(res, cts):
        rows, params, consts = res
        drows, dparams = bwd_call(rows, params, consts, cts[:nout], cts[nout:])
        return drows, dparams, tuple(jnp.zeros_like(c) for c in consts)

    op.defvjp(op_fwd, op_bwd)
    return op(tuple(rows), tuple(params), tuple(consts))


def rowwise_linear(f, rows, params, consts, w, slot, name, residual=None, riders=NO_RIDERS, carry=(),
                   block_rows=None):
    width = w.shape[0]
    fwd_call, bwd_call = _rowwise_calls(lambda *a: (f(*a),), rows, params, consts, [width], name, None,
                                        block_rows, carry, BF16)

    @jax.custom_vjp
    def op(rows, params, consts, w, slot, residual, shards, slots):
        h, = fwd_call(rows, params, consts)
        ride = riders.gather_ride(shards)
        y = mm(h, w, 'nn', name + '_mm', add=residual, ride=ride)
        y, gathered = y if ride is not None else (y, ())
        return y, tuple(rows[k] for k in carry), tuple(gathered), riders.whole_slots(slots)

    def op_fwd(rows, params, consts, w, slot, residual, shards, slots):
        h, = fwd_call(rows, params, consts)
        ride = riders.gather_ride(shards)
        y = mm(h, w, 'nn', name + '_mm', add=residual, ride=ride)
        y, gathered = y if ride is not None else (y, ())
        out = (y, tuple(rows[k] for k in carry), tuple(gathered), riders.whole_slots(slots))
        return out, (rows, params, consts, h, w, shards)

    def op_bwd(res, cts):
        rows, params, consts, h, w, shards = res
        g, carried, _, slot_grads = cts
        ride = riders.scatter_ride(slot_grads)
        dh = mm(g, w, 'nt', name + '_dx', ride=ride)
        dh, received = dh if ride is not None else (dh, ())
        drows, dparams = bwd_call(rows, params, consts, (dh,), carried)
        return (drows, dparams, tuple(jnp.zeros_like(c) for c in consts), jnp.zeros_like(w),
                mm(h, g, 'tn', name + '_dw', BF16), None if residual is None else g,
                tuple(jnp.zeros_like(s) for s in shards), tuple(received))

    op.defvjp(op_fwd, op_bwd)
    return op(tuple(rows), tuple(params), tuple(consts), w, slot, residual, riders.shards, riders.slots)


def _rms(x, g):
    return x * lax.rsqrt(jnp.mean(x * x, axis=-1, keepdims=True) + EPS) * g


def rmsnorm(x, g, name, carry=False):
    out = rowwise(lambda x, g: (_rms(x, g),), [x], [g.reshape(1, -1)], [], [x.shape[1]], name,
                  carry=(0,) if carry else ())
    return out if carry else out[0]


def _split2(x):
    hi = x.astype(BF16)
    return hi, (x - hi.astype(F32)).astype(BF16)


@jax.custom_vjp
def select_mm(x, sel):
    return sum(_dg(t, sel, 1, 0) for t in _split2(x))


def _select_mm_fwd(x, sel):
    return select_mm(x, sel), sel


def _select_mm_bwd(sel, g):
    return sum(_dg(t, sel, 1, 1) for t in _split2(g)), jnp.zeros_like(sel)


select_mm.defvjp(_select_mm_fwd, _select_mm_bwd)


def groupnorm(x, g, group, name):
    width = x.shape[1]
    g_full = jnp.tile(g.reshape(1, group), (1, width // group))
    if group % LANES == 0:
        def f(x, g_full):
            outs = []
            for lo in range(0, width, group):
                xs = x[:, lo:lo + group]
                outs.append(_rms(xs, g_full[:, lo:lo + group]))
            return (jnp.concatenate(outs, axis=-1),)

        return rowwise(f, [x], [g_full], [], [width], name)[0]

    gid = jnp.arange(width) // group
    sel = (gid[:, None] == jnp.arange(LANES)[None, :]).astype(BF16)

    def f(x, g_full, sel, sel_t):
        ms = select_mm(x * x, sel) * (1.0 / group)
        inv = select_mm(lax.rsqrt(ms + EPS), sel_t)
        return (x * inv * g_full,)

    return rowwise(f, [x], [g_full], [sel, sel.T], [width], name)[0]


def glu(x, name):
    half = x.shape[1] // 2

    def f(x):
        return (x[:, :half] * jax.nn.sigmoid(x[:, half:]),)

    return rowwise(f, [x], [], [], [half], name)[0]


def swiglu_block(x):
    half = x.shape[1] // 2
    gate = x[:, :half]
    return gate * jax.nn.sigmoid(gate) * x[:, half:]


def ln_silu_block(x, g, b):
    mu = jnp.mean(x, axis=-1, keepdims=True)
    xc = x - mu
    var = jnp.mean(xc * xc, axis=-1, keepdims=True)
    y = xc * lax.rsqrt(var + EPS) * g + b
    return y * jax.nn.sigmoid(y)


def branch_norms_block(a, b, c, g):
    w1, w2 = a.shape[1], b.shape[1]
    return jnp.concatenate([_rms(a, g[:, :w1]), _rms(b, g[:, w1:w1 + w2]), _rms(c, g[:, w1 + w2:])], axis=-1)


def xa_core_block(q, k, v):
    scale = XA_HEAD_DIM ** -0.5
    outs = []
    for h in range(XA_HEADS):
        sl = slice(h * XA_HEAD_DIM, (h + 1) * XA_HEAD_DIM)
        s = bdot_nt(q[:, sl], k[:, sl]) * scale
        m = lax.stop_gradient(jnp.max(s, axis=-1, keepdims=True))
        e = jnp.exp(s - m)
        p = e / jnp.sum(e, axis=-1, keepdims=True)
        outs.append(bdot_nn(p, v[:, sl]))
    return jnp.concatenate(outs, axis=-1)


def loss_rows(y, target, name):
    def f(y, t):
        d = y - t
        return (0.5 * jnp.mean(d * d, axis=-1, keepdims=True),)

    return rowwise(f, [y, target], [], [], [1], name, need_row_grad=[True, False])[0]


def _swiglu(gate, up):
    return gate * jax.nn.sigmoid(gate) * up


def mm_swiglu(h, w, name, ride=None):
    M, K = h.shape
    H = w.shape[1] // 2
    tm, tn = _pick(M, 1024), _pick(H, 1536)
    nj = H // tn
    grid = (M // tm, 2 * nj)

    def body(*refs):
        (a_ref, b_ref), (gu_ref, act_ref), (kept,), riding = _split_refs(ride, 2, 2, refs)
        ride_done = _ride_ends(ride, riding, grid)
        j = pl.program_id(1)
        prod = _dg(a_ref[...], b_ref[...], 1, 0)
        gu_ref[...] = prod

        @pl.when(j < nj)
        def _():
            kept[j] = prod

        @pl.when(j >= nj)
        def _():
            act_ref[...] = _swiglu(kept[j - nj], prod).astype(BF16)

        ride_done()

    return _ride_call(
        body, name, grid,
        [pl.BlockSpec((tm, K), lambda i, j: (i, 0)), pl.BlockSpec((K, tn), lambda i, j: (0, j))],
        [pl.BlockSpec((tm, tn), lambda i, j: (i, j)),
         pl.BlockSpec((tm, tn), lambda i, j: (i, jnp.maximum(j - nj, 0)))],
        [jax.ShapeDtypeStruct((M, 2 * H), F32), jax.ShapeDtypeStruct((M, H), BF16)],
        (h, w), ride, ("parallel", "arbitrary"), scratch=[pltpu.VMEM((nj, tm, tn), F32)])


def mm_swiglu_bwd(dy, w_out, gu, name):
    M, D = dy.shape
    H = gu.shape[1] // 2
    tm, tn = _pick(M, 512), _pick(H, 1536)
    nj = H // tn

    def body(dy_ref, w_ref, gate_ref, up_ref, o_ref, kept):
        j = pl.program_id(1)

        @pl.when(j < nj)
        def _():
            dact = _dg(dy_ref[...], w_ref[...], 1, 1)
            _, vjp = jax.vjp(_swiglu, gate_ref[...], up_ref[...])
            dgate, dup = vjp(dact)
            o_ref[...] = dgate
            kept[j] = dup

        @pl.when(j >= nj)
        def _():
            o_ref[...] = kept[j - nj]

    def tile(j):
        return jnp.minimum(j, nj - 1)

    return pl.pallas_call(
        body, name=name, grid=(M // tm, 2 * nj),
        in_specs=[pl.BlockSpec((tm, D), lambda i, j: (i, 0)),
                  pl.BlockSpec((tn, D), lambda i, j: (tile(j), 0)),
                  pl.BlockSpec((tm, tn), lambda i, j: (i, tile(j))),
                  pl.BlockSpec((tm, tn), lambda i, j: (i, nj + tile(j)))],
        out_specs=pl.BlockSpec((tm, tn), lambda i, j: (i, j)),
        out_shape=jax.ShapeDtypeStruct((M, 2 * H), F32),
        scratch_shapes=[pltpu.VMEM((nj, tm, tn), F32)],
        compiler_params=_cparams(("parallel", "arbitrary")),
    )(dy, w_out, gu, gu)


def ffn(x, gain, w_in, slot_in, w_out, slot_out, name, riders=NO_RIDERS):
    norm_fwd, norm_bwd = _rowwise_calls(lambda x, g: (_rms(x, g),), [x], [gain], [], [x.shape[1]],
                                        name + '_norm', None, None, (0,), BF16)

    def forward(x, gain, w_in, w_out, shards):
        h, = norm_fwd((x,), (gain,), ())
        (gu, act), gathered = mm_swiglu(h, w_in, name + '_in', riders.gather_ride(shards))
        return mm(act, w_out, 'nn', name + '_out', add=x), tuple(gathered), (h, gu, act)

    @jax.custom_vjp
    def op(x, gain, w_in, slot_in, w_out, slot_out, shards, slots):
        y, gathered, _ = forward(x, gain, w_in, w_out, shards)
        return y, gathered, riders.whole_slots(slots)

    def op_fwd(x, gain, w_in, slot_in, w_out, slot_out, shards, slots):
        y, gathered, (h, gu, act) = forward(x, gain, w_in, w_out, shards)
        return (y, gathered, riders.whole_slots(slots)), (x, gain, h, gu, act, w_in, w_out, shards)

    def op_bwd(res, cts):
        x, gain, h, gu, act, w_in, w_out, shards = res
        g, _, slot_grads = cts
        dgu = mm_swiglu_bwd(g, w_out, gu, name + '_dact')
        dw_out = mm(act, g, 'tn', name + '_out_dw', BF16)
        dw_in = mm(h, dgu, 'tn', name + '_in_dw', BF16)
        ride = riders.scatter_ride(slot_grads)
        dh = mm(dgu, w_in, 'nt', name + '_in_dx', ride=ride)
        dh, received = dh if ride is not None else (dh, ())
        (dx,), (dgain,) = norm_bwd((x,), (gain,), (), (dh,), (g,))
        return (dx, dgain, jnp.zeros_like(w_in), dw_in, jnp.zeros_like(w_out), dw_out,
                tuple(jnp.zeros_like(s) for s in shards), tuple(received))

    op.defvjp(op_fwd, op_bwd)
    return op(x, gain, w_in, slot_in, w_out, slot_out, riders.shards, riders.slots)


def _hilo(x, ones_bf16):
    hi = x.astype(BF16)
    lo = (x - hi.astype(F32)).astype(BF16)
    return _dg(hi, ones_bf16, 1, 0) + _dg(lo, ones_bf16, 1, 0)


def _sb_block(qh, kb, c, valid, strict_upper):
    z = _dg(qh, kb, 1, 1)
    a = jnp.minimum(z, 0.0) - jnp.log(1.0 + jnp.exp(-jnp.abs(z)))
    b = jnp.where(valid, a - z, 0.0)
    s = _hilo(b, strict_upper) + c
    w = jnp.where(valid, jnp.exp(a + s), 0.0)
    return a, b, w


def _sb_masks(T):
    row = lax.broadcasted_iota(jnp.int32, (T, T), 0)
    col = lax.broadcasted_iota(jnp.int32, (T, T), 1)
    return col < row, (row > col).astype(BF16), (row >= col).astype(BF16)


def _sb_key_blocks(i, j, T, causal):
    second = jnp.maximum(j - 1, 0)
    return [(pl.multiple_of(j * T, T), jnp.logical_or(causal, j != i)),
            (pl.multiple_of(second * T, T), jnp.logical_and(jnp.logical_or(causal, True), j >= 1))]


HEADS_PER_BLOCK = LANES // SB_HEAD_DIM


def _head_mask(h):
    lane = lax.broadcasted_iota(jnp.int32, (1, LANES), 1)
    return (lane // SB_HEAD_DIM == h).astype(F32)


def _max_all(columns):
    m = columns[0]
    for c in columns[1:]:
        m = jnp.maximum(m, c)
    return jnp.max(m)


def _ride_call(body, name, grid, in_specs, out_specs, out_shape, operands, ride, semantics, scratch=()):
    if ride is None:
        outs = pl.pallas_call(body, name=name, grid=grid, in_specs=in_specs, out_specs=out_specs,
                              out_shape=out_shape, scratch_shapes=list(scratch),
                              compiler_params=_cparams(semantics))(*operands)
        return outs, ()
    outs = pl.pallas_call(
        body, name=name, grid=grid, in_specs=in_specs + ride.in_specs, out_specs=out_specs + ride.out_specs,
        out_shape=out_shape + ride.out_shape, scratch_shapes=list(scratch) + ride.scratch,
        compiler_params=_cparams(("arbitrary",) * len(grid), has_side_effects=True),
    )(*operands, *ride.arrays)
    return outs[:len(out_shape)], outs[len(out_shape):]


def _ride_ends(ride, riding, grid):
    if riding is None:
        return lambda: None
    first, last = None, None
    for axis, size in enumerate(grid):
        at0, at1 = pl.program_id(axis) == 0, pl.program_id(axis) == size - 1
        first = at0 if first is None else jnp.logical_and(first, at0)
        last = at1 if last is None else jnp.logical_and(last, at1)
    return ride.at_ends(riding, first, last)


def _sb_fwd_call(q, k, v, T, name, ride=None):
    L, W = q.shape
    scale = SB_HEAD_DIM ** -0.5
    grid = (W // LANES, L // T)

    def body(*refs):
        (q_ref, k_ref, v_ref), (o_ref,), _, riding = _split_refs(ride, 3, 1, refs)
        ride_done = _ride_ends(ride, riding, grid)
        i = pl.program_id(1)
        causal, strict_upper, _ = _sb_masks(T)
        q2 = q_ref[...] * scale
        masks = [_head_mask(h) for h in range(HEADS_PER_BLOCK)]
        qs = [(q2 * hm).astype(BF16) for hm in masks]
        zero = jnp.zeros((T, 1), F32)

        def cond(state):
            j, cs, _ = state
            return jnp.logical_and(j >= 0, _max_all(cs) > -SB_CUT)

        def step(state):
            j, cs, acc = state
            blocks = _sb_key_blocks(i, j, T, causal)
            ks = [k_ref[pl.ds(r0, T), :].astype(BF16) for r0, _ in blocks]
            vs = [v_ref[pl.ds(r0, T), :] for r0, _ in blocks]
            new_cs = []
            for hm, qh, c in zip(masks, qs, cs):
                for (_, valid), kb, vb in zip(blocks, ks, vs):
                    _, b, w = _sb_block(qh, kb, c, valid, strict_upper)
                    vh = (vb * hm).astype(BF16)
                    w_hi = w.astype(BF16)
                    w_lo = (w - w_hi.astype(F32)).astype(BF16)
                    acc = acc + _dg(w_hi, vh, 1, 0) + _dg(w_lo, vh, 1, 0)
                    c = c + jnp.sum(b, axis=1, keepdims=True)
                new_cs.append(c)
            return j - len(blocks), tuple(new_cs), acc

        _, _, acc = lax.while_loop(cond, step, (i, (zero,) * HEADS_PER_BLOCK, jnp.zeros((T, LANES), F32)))
        o_ref[...] = acc
        ride_done()

    (o,), rode = _ride_call(
        body, name, grid,
        [pl.BlockSpec((T, LANES), lambda p, i: (i, p)),
         pl.BlockSpec((L, LANES), lambda p, i: (0, p)),
         pl.BlockSpec((L, LANES), lambda p, i: (0, p))],
        [pl.BlockSpec((T, LANES), lambda p, i: (i, p))], [jax.ShapeDtypeStruct((L, W), F32)],
        (q, k, v), ride, ("parallel", "parallel"))
    return o, rode


def _sb_bwd_call(q, k, v, o, do, T, name, ride=None):
    L, W = q.shape
    scale = SB_HEAD_DIM ** -0.5
    grid = (W // LANES, L // T)

    def body(*refs):
        (q_ref, k_ref, v_ref, o_ref, do_ref), (dq_ref, dk_ref, dv_ref), _, riding = _split_refs(ride, 5, 3, refs)
        ride_done = _ride_ends(ride, riding, grid)
        i = pl.program_id(1)

        @pl.when(i == 0)
        def _():
            dk_ref[...] = jnp.zeros_like(dk_ref)
            dv_ref[...] = jnp.zeros_like(dv_ref)

        causal, strict_upper, upper = _sb_masks(T)
        q2 = q_ref[...] * scale
        do2 = do_ref[...]
        o2 = o_ref[...]
        masks = [_head_mask(h) for h in range(HEADS_PER_BLOCK)]
        qs = [(q2 * hm).astype(BF16) for hm in masks]
        dos = [(do2 * hm).astype(BF16) for hm in masks]
        totals = [jnp.sum(doh.astype(F32) * o2, axis=1, keepdims=True) for doh in dos]
        zero = jnp.zeros((T, 1), F32)

        def cond(state):
            j, cs, _, _ = state
            return jnp.logical_and(j >= 0, _max_all(cs) > -SB_CUT)

        def step(state):
            j, cs, rs, dq = state
            blocks = _sb_key_blocks(i, j, T, causal)
            kfs = [k_ref[pl.ds(r0, T), :] for r0, _ in blocks]
            ks = [kf.astype(BF16) for kf in kfs]
            vs = [v_ref[pl.ds(r0, T), :].astype(BF16) for r0, _ in blocks]
            dks = [jnp.zeros((T, LANES), F32) for _ in blocks]
            dvs = [jnp.zeros((T, LANES), F32) for _ in blocks]
            new_cs, new_rs = [], []
            for hm, qh, doh, total, c, r in zip(masks, qs, dos, totals, cs, rs):
                for n, ((_, valid), kf, kb, vb) in enumerate(zip(blocks, kfs, ks, vs)):
                    a, b, w = _sb_block(qh, kb, c, valid, strict_upper)
                    e = _dg(doh, vb, 1, 1) * w
                    before = total - (_hilo(e, upper) + r)
                    dz = jnp.where(valid, e * jnp.exp(b) - before * jnp.exp(a), 0.0).astype(BF16)
                    dq = dq + _dg(dz, kf * hm, 1, 0)
                    dks[n] = dks[n] + _dg(dz, qh, 0, 0)
                    dvs[n] = dvs[n] + _dg(w, doh, 0, 0)
                    c = c + jnp.sum(b, axis=1, keepdims=True)
                    r = r + jnp.sum(e, axis=1, keepdims=True)
                new_cs.append(c)
                new_rs.append(r)
            for (r0, _), dk, dv in zip(blocks, dks, dvs):
                dk_ref[pl.ds(r0, T), :] += dk
                dv_ref[pl.ds(r0, T), :] += dv
            return j - len(blocks), tuple(new_cs), tuple(new_rs), dq

        init = (i, (zero,) * HEADS_PER_BLOCK, (zero,) * HEADS_PER_BLOCK, jnp.zeros((T, LANES), F32))
        dq = lax.while_loop(cond, step, init)[3]
        dq_ref[...] = dq * scale
        ride_done()

    blk = pl.BlockSpec((T, LANES), lambda p, i: (i, p))
    full = pl.BlockSpec((L, LANES), lambda p, i: (0, p))
    return _ride_call(body, name, grid, [blk, full, full, blk, blk], [blk, full, full],
                      [jax.ShapeDtypeStruct((L, W), F32)] * 3, (q, k, v, o, do), ride, ("parallel", "arbitrary"))


def sb_attention(q, k, v, name, riders=NO_RIDERS):
    T = min(256, q.shape[0])

    @jax.custom_vjp
    def op(q, k, v, shards, slots):
        o, gathered = _sb_fwd_call(q, k, v, T, name + '_fwd', riders.gather_ride(shards))
        return o, tuple(gathered), riders.whole_slots(slots)

    def op_fwd(q, k, v, shards, slots):
        out = op(q, k, v, shards, slots)
        return out, (q, k, v, out[0], shards)

    def op_bwd(res, cts):
        q, k, v, o, shards = res
        do, _, slot_grads = cts
        grads, received = _sb_bwd_call(q, k, v, o, do, T, name + '_bwd', riders.scatter_ride(slot_grads))
        return (*grads, tuple(jnp.zeros_like(s) for s in shards), tuple(received))

    op.defvjp(op_fwd, op_bwd)
    return op(q, k, v, riders.shards, riders.slots)


def _dwconv_fwd_call(x, w, b, T, name):
    L, C = x.shape
    per = T // CONV_HALO
    lead = CONV_HALO - (CONV_WIDTH - 1)

    def body(x_ref, halo_ref, w_ref, b_ref, o_ref, buf):
        i = pl.program_id(0)
        buf[0:CONV_HALO, :] = jnp.where(i > 0, halo_ref[...], 0.0)
        buf[CONV_HALO:CONV_HALO + T, :] = x_ref[...]
        acc = jnp.zeros((T, C), F32) + b_ref[...]
        for j in range(CONV_WIDTH):
            acc = acc + w_ref[j:j + 1, :] * buf[lead + j:lead + j + T, :]
        o_ref[...] = acc

    return pl.pallas_call(
        body, name=name, grid=(L // T,),
        in_specs=[pl.BlockSpec((T, C), lambda i: (i, 0)),
                  pl.BlockSpec((CONV_HALO, C), lambda i: (jnp.maximum(i * per - 1, 0), 0)),
                  pl.BlockSpec(w.shape, lambda i: (0, 0)),
                  pl.BlockSpec(b.shape, lambda i: (0, 0))],
        out_specs=pl.BlockSpec((T, C), lambda i: (i, 0)),
        out_shape=jax.ShapeDtypeStruct((L, C), F32),
        scratch_shapes=[pltpu.VMEM((T + CONV_HALO, C), F32)],
        compiler_params=_cparams(("parallel",)),
    )(x, x, w, b)


def _dwconv_bwd_call(x, w, g, T, name):
    L, C = x.shape
    per = T // CONV_HALO
    n = L // T
    last_halo = L // CONV_HALO - 1
    lead = CONV_HALO - (CONV_WIDTH - 1)

    def body(x_ref, xh_ref, g_ref, gh_ref, w_ref, dx_ref, dw_ref, db_ref, bufx, bufg):
        i = pl.program_id(0)
        bufx[0:CONV_HALO, :] = jnp.where(i > 0, xh_ref[...], 0.0)
        bufx[CONV_HALO:CONV_HALO + T, :] = x_ref[...]
        gm = g_ref[...]
        bufg[0:T, :] = gm
        bufg[T:T + CONV_HALO, :] = jnp.where(i < n - 1, gh_ref[...], 0.0)
        acc = jnp.zeros((T, C), F32)
        for j in range(CONV_WIDTH):
            off = CONV_WIDTH - 1 - j
            acc = acc + w_ref[j:j + 1, :] * bufg[off:off + T, :]
        dx_ref[...] = acc

        @pl.when(i == 0)
        def _():
            dw_ref[...] = jnp.zeros_like(dw_ref)
            db_ref[...] = jnp.zeros_like(db_ref)

        for j in range(CONV_WIDTH):
            dw_ref[j:j + 1, :] += jnp.sum(gm * bufx[lead + j:lead + j + T, :], axis=0, keepdims=True)
        db_ref[...] += jnp.sum(gm, axis=0, keepdims=True)

    return pl.pallas_call(
        body, name=name, grid=(n,),
        in_specs=[pl.BlockSpec((T, C), lambda i: (i, 0)),
                  pl.BlockSpec((CONV_HALO, C), lambda i: (jnp.maximum(i * per - 1, 0), 0)),
                  pl.BlockSpec((T, C), lambda i: (i, 0)),
                  pl.BlockSpec((CONV_HALO, C), lambda i: (jnp.minimum((i + 1) * per, last_halo), 0)),
                  pl.BlockSpec(w.shape, lambda i: (0, 0))],
        out_specs=[pl.BlockSpec((T, C), lambda i: (i, 0)),
                   pl.BlockSpec(w.shape, lambda i: (0, 0)),
                   pl.BlockSpec((1, C), lambda i: (0, 0))],
        out_shape=[jax.ShapeDtypeStruct((L, C), F32), jax.ShapeDtypeStruct(w.shape, F32),
                   jax.ShapeDtypeStruct((1, C), F32)],
        scratch_shapes=[pltpu.VMEM((T + CONV_HALO, C), F32), pltpu.VMEM((T + CONV_HALO, C), F32)],
        compiler_params=_cparams(("arbitrary",)),
    )(x, x, g, g, w)


def dwconv(x, w, b, name):
    T = min(512, x.shape[0])

    @jax.custom_vjp
    def op(x, w, b):
        return _dwconv_fwd_call(x, w, b, T, name + '_fwd')

    def op_fwd(x, w, b):
        return _dwconv_fwd_call(x, w, b, T, name + '_fwd'), (x, w)

    def op_bwd(res, g):
        x, w = res
        return tuple(_dwconv_bwd_call(x, w, g, T, name + '_bwd'))

    op.defvjp(op_fwd, op_bwd)
    return op(x, w, b)


def _ssm_fwd_call(u, ar, ai, bbr, bbi, cr, ci, d, T, name):
    L, C = u.shape
    S = SSM_LANES

    def body(u_ref, ar_ref, ai_ref, bbr_ref, bbi_ref, cr_ref, ci_ref, d_ref,
             y_ref, xr_ref, xi_ref, st_r, st_i, in_r, in_i, out_r, out_i):
        i = pl.program_id(0)

        @pl.when(i == 0)
        def _():
            st_r[...] = jnp.zeros_like(st_r)
            st_i[...] = jnp.zeros_like(st_i)

        u_blk = u_ref[...]
        xr_ref[...] = _dg(u_blk, bbr_ref[...], 1, 0)
        xi_ref[...] = _dg(u_blk, bbi_ref[...], 1, 0)
        a_r, a_i = ar_ref[...], ai_ref[...]

        def tile(t, carry):
            sr, si = carry
            r0 = pl.multiple_of(t * SUBLANES, SUBLANES)
            in_r[...] = xr_ref[pl.ds(r0, SUBLANES), :]
            in_i[...] = xi_ref[pl.ds(r0, SUBLANES), :]
            for r in range(SUBLANES):
                nr = a_r * sr - a_i * si + in_r[r:r + 1, :]
                ni = a_r * si + a_i * sr + in_i[r:r + 1, :]
                sr, si = nr, ni
                out_r[r:r + 1, :] = sr
                out_i[r:r + 1, :] = si
            xr_ref[pl.ds(r0, SUBLANES), :] = out_r[...]
            xi_ref[pl.ds(r0, SUBLANES), :] = out_i[...]
            return sr, si

        sr, si = lax.fori_loop(0, T // SUBLANES, tile, (st_r[0:1, :], st_i[0:1, :]))
        st_r[0:1, :] = sr
        st_i[0:1, :] = si
        y_ref[...] = (_dg(xr_ref[...], cr_ref[...], 1, 0) - _dg(xi_ref[...], ci_ref[...], 1, 0)
                      + d_ref[...] * u_blk)

    full = lambda a: pl.BlockSpec(a.shape, lambda i: (0, 0))
    return pl.pallas_call(
        body, name=name, grid=(L // T,),
        in_specs=[pl.BlockSpec((T, C), lambda i: (i, 0))] + [full(a) for a in (ar, ai, bbr, bbi, cr, ci, d)],
        out_specs=[pl.BlockSpec((T, C), lambda i: (i, 0)), pl.BlockSpec((T, S), lambda i: (i, 0)),
                   pl.BlockSpec((T, S), lambda i: (i, 0))],
        out_shape=[jax.ShapeDtypeStruct((L, C), F32), jax.ShapeDtypeStruct((L, S), F32),
                   jax.ShapeDtypeStruct((L, S), F32)],
        scratch_shapes=[pltpu.VMEM((SUBLANES, S), F32)] * 6,
        compiler_params=_cparams(("arbitrary",)),
    )(u, ar, ai, bbr, bbi, cr, ci, d)


def _ssm_bwd_call(u, xr, xi, dy, ar, ai, bbr, bbi, cr, ci, d, T, name):
    L, C = u.shape
    S = SSM_LANES
    n = L // T
    per = T // SUBLANES

    def body(u_ref, xr_ref, xi_ref, hr_ref, hi_ref, dy_ref, ar_ref, ai_ref, bbr_ref, bbi_ref, cr_ref, ci_ref,
             d_ref, du_ref, dar_ref, dai_ref, dbr_ref, dbi_ref, dcr_ref, dci_ref, dd_ref,
             lam_r, lam_i, prev_r, prev_i, st_r, st_i, in_r, in_i, out_r, out_i):
        i = pl.program_id(0)
        chunk = n - 1 - i

        @pl.when(i == 0)
        def _():
            st_r[...] = jnp.zeros_like(st_r)
            st_i[...] = jnp.zeros_like(st_i)
            for ref in (dar_ref, dai_ref, dbr_ref, dbi_ref, dcr_ref, dci_ref, dd_ref):
                ref[...] = jnp.zeros_like(ref)

        dy_blk = dy_ref[...]
        u_blk = u_ref[...]
        lam_r[...] = _dg(dy_blk, cr_ref[...], 1, 1)
        lam_i[...] = -_dg(dy_blk, ci_ref[...], 1, 1)
        dcr_ref[...] += _dg(xr_ref[...], dy_blk, 0, 0)
        dci_ref[...] -= _dg(xi_ref[...], dy_blk, 0, 0)
        a_r, a_i = ar_ref[...], ai_ref[...]

        def tile(k, carry):
            lr, li = carry
            r0 = pl.multiple_of((per - 1 - k) * SUBLANES, SUBLANES)
            in_r[...] = lam_r[pl.ds(r0, SUBLANES), :]
            in_i[...] = lam_i[pl.ds(r0, SUBLANES), :]
            for r in range(SUBLANES - 1, -1, -1):
                nr = in_r[r:r + 1, :] + a_r * lr + a_i * li
                ni = in_i[r:r + 1, :] + a_r * li - a_i * lr
                lr, li = nr, ni
                out_r[r:r + 1, :] = lr
                out_i[r:r + 1, :] = li
            lam_r[pl.ds(r0, SUBLANES), :] = out_r[...]
            lam_i[pl.ds(r0, SUBLANES), :] = out_i[...]
            return lr, li

        lr, li = lax.fori_loop(0, per, tile, (st_r[0:1, :], st_i[0:1, :]))
        st_r[0:1, :] = lr
        st_i[0:1, :] = li

        l_r, l_i = lam_r[...], lam_i[...]
        du_ref[...] = _dg(l_r, bbr_ref[...], 1, 1) + _dg(l_i, bbi_ref[...], 1, 1) + d_ref[...] * dy_blk
        dbr_ref[...] += _dg(u_blk, l_r, 0, 0)
        dbi_ref[...] += _dg(u_blk, l_i, 0, 0)
        dd_ref[...] += jnp.sum(dy_blk * u_blk, axis=0, keepdims=True)

        prev_r[0:SUBLANES, :] = jnp.where(chunk > 0, hr_ref[...], 0.0)
        prev_i[0:SUBLANES, :] = jnp.where(chunk > 0, hi_ref[...], 0.0)
        prev_r[SUBLANES:SUBLANES + T, :] = xr_ref[...]
        prev_i[SUBLANES:SUBLANES + T, :] = xi_ref[...]
        p_r = prev_r[SUBLANES - 1:SUBLANES - 1 + T, :]
        p_i = prev_i[SUBLANES - 1:SUBLANES - 1 + T, :]
        dar_ref[...] += jnp.sum(l_r * p_r + l_i * p_i, axis=0, keepdims=True)
        dai_ref[...] += jnp.sum(l_i * p_r - l_r * p_i, axis=0, keepdims=True)

    rev = lambda w: pl.BlockSpec((T, w), lambda i: (n - 1 - i, 0))
    halo = pl.BlockSpec((SUBLANES, S), lambda i: (jnp.maximum((n - 1 - i) * per - 1, 0), 0))
    full = lambda a: pl.BlockSpec(a.shape, lambda i: (0, 0))
    params = (ar, ai, bbr, bbi, cr, ci, d)
    return pl.pallas_call(
        body, name=name, grid=(n,),
        in_specs=[rev(C), rev(S), rev(S), halo, halo, rev(C)] + [full(a) for a in params],
        out_specs=[rev(C)] + [full(a) for a in params],
        out_shape=[jax.ShapeDtypeStruct((L, C), F32)] + [jax.ShapeDtypeStruct(a.shape, F32) for a in params],
        scratch_shapes=[pltpu.VMEM((T, S), F32), pltpu.VMEM((T, S), F32),
                        pltpu.VMEM((T + SUBLANES, S), F32), pltpu.VMEM((T + SUBLANES, S), F32)]
        + [pltpu.VMEM((SUBLANES, S), F32)] * 6,
        compiler_params=_cparams(("arbitrary",)),
    )(u, xr, xi, xr, xi, dy, *params)


def ssm_core(u, ar, ai, bbr, bbi, cr, ci, d, name):
    T = min(256, u.shape[0])

    @jax.custom_vjp
    def op(u, ar, ai, bbr, bbi, cr, ci, d):
        return _ssm_fwd_call(u, ar, ai, bbr, bbi, cr, ci, d, T, name + '_fwd')[0]

    def op_fwd(u, ar, ai, bbr, bbi, cr, ci, d):
        y, xr, xi = _ssm_fwd_call(u, ar, ai, bbr, bbi, cr, ci, d, T, name + '_fwd')
        return y, (u, xr, xi, ar, ai, bbr, bbi, cr, ci, d)

    def op_bwd(res, dy):
        u, xr, xi, ar, ai, bbr, bbi, cr, ci, d = res
        return tuple(_ssm_bwd_call(u, xr, xi, dy, ar, ai, bbr, bbi, cr, ci, d, T, name + '_bwd'))

    op.defvjp(op_fwd, op_bwd)
    return op(u, ar, ai, bbr, bbi, cr, ci, d)


@jax.custom_vjp
def _block_diag(blocks):
    G, R, Cc = blocks.shape
    eye = jnp.eye(G, dtype=blocks.dtype)
    return (blocks[:, :, None, :] * eye[:, None, :, None]).reshape(G * R, G * Cc)


def _block_diag_fwd(blocks):
    return _block_diag(blocks), blocks.shape


def _block_diag_bwd(shape, g):
    G, R, Cc = shape
    return (jnp.stack([g[k * R:(k + 1) * R, k * Cc:(k + 1) * Cc] for k in range(G)]),)


_block_diag.defvjp(_block_diag_fwd, _block_diag_bwd)


def ssm_discretise(lam_re, lam_im, log_dt, b_re, b_im, c_re, c_im):
    dt = jnp.exp(log_dt)[:, None]
    mag = jnp.exp(lam_re * dt)
    ar, ai = mag * jnp.cos(lam_im * dt), mag * jnp.sin(lam_im * dt)
    den = lam_re * lam_re + lam_im * lam_im
    fr = ((ar - 1.0) * lam_re + ai * lam_im) / den
    fi = (ai * lam_re - (ar - 1.0) * lam_im) / den
    bbr = fr[..., None] * b_re - fi[..., None] * b_im
    bbi = fr[..., None] * b_im + fi[..., None] * b_re
    return (ar.reshape(1, SSM_LANES), ai.reshape(1, SSM_LANES),
            _block_diag(bbr.transpose(0, 2, 1)), _block_diag(bbi.transpose(0, 2, 1)),
            _block_diag(c_re.transpose(0, 2, 1)), _block_diag(c_im.transpose(0, 2, 1)))


def split_columns(p, bounds):
    @jax.custom_vjp
    def op(p):
        return tuple(p[:, lo:hi] for lo, hi in zip(bounds[:-1], bounds[1:]))

    def op_fwd(p):
        return op(p), None

    def op_bwd(_, gs):
        return (jnp.concatenate(gs, axis=1),)

    op.defvjp(op_fwd, op_bwd)
    return op(p)


DEPTH = 2
EARLY = ['w_in', 'conv_pw2_w', 'ssm_glu_w']
LATE = [n for n in MATRICES if n not in EARLY]
FFN = ['ffn_w_in', 'ffn_w_out']
GATHER_AT = {
    'start': [('w_in', 0)],
    'sb0': [(n, 0) for n in MATRICES if n != 'w_in'],
    'ffn0': [(n, 1) for n in MATRICES if n not in FFN],
    'sb1': [(n, 1) for n in FFN],
}
SCATTER_AT = {
    'sb1': [(n, 1) for n in LATE],
    'ffn0': [(n, 1) for n in EARLY],
    'sb0': [(n, 0) for n in LATE],
    'end': [(n, 0) for n in EARLY],
}


def _assemble(name, gathered):
    if name not in SLOTTED:
        return gathered
    return jnp.concatenate([gathered[j] for j in range(N_CHIPS)], axis=SHARD_AXIS[name] - 1)


def local_loss(slots, w, mats, shards, x, mem, target):
    mats = dict(mats)
    slot = {key: slots[key] for key in SCATTER_AT['end']}
    s1, s2, s3 = SB_WIDTH, 2 * SB_WIDTH, 3 * SB_WIDTH
    s4 = s3 + 2 * CONV_CH

    def riders_at(host):
        return Riders(gather=[(n, shards[(n, l)]) for n, l in GATHER_AT[host]],
                      scatter=[(n, slots[(n, l)]) for n, l in SCATTER_AT[host]])

    def take(host, gathered, handed):
        for (n, l), g in zip(GATHER_AT[host], gathered):
            mats[(n, l)] = _assemble(n, g)
        for key, s in zip(SCATTER_AT[host], handed):
            slot[key] = s

    def linear_(x, n, l, name, residual=None, host=None):
        y, gathered, handed = linear(x, mats[(n, l)], slot[(n, l)], name, residual,
                                     riders_at(host) if host else NO_RIDERS)
        if host:
            take(host, gathered, handed)
        return y

    def fused_(f, rows, params, n, l, name, residual=None, host=None, carry=(), block_rows=None):
        y, carried, gathered, handed = rowwise_linear(
            f, rows, params, [], mats[(n, l)], slot[(n, l)], name, residual,
            riders_at(host) if host else NO_RIDERS, carry, block_rows)
        if host:
            take(host, gathered, handed)
        return (y,) + tuple(carried)

    def gain(n, l):
        return w[n][l].reshape(1, -1)

    for l in range(DEPTH):
        tag = 'l%d_' % l
        p, x = fused_(_rms, [x], [gain('norm_mix_g', l)], 'w_in', l, tag + 'w_in', carry=(0,))
        q, k, v, u_conv, u_ssm = split_columns(p, (0, s1, s2, s3, s4, p.shape[1]))
        q = groupnorm(q, w['sb_q_norm_g'][l], SB_HEAD_DIM, tag + 'q_norm')
        k = groupnorm(k, w['sb_k_norm_g'][l], SB_HEAD_DIM, tag + 'k_norm')
        o_sb, gathered, handed = sb_attention(q, k, v, tag + 'sb', riders_at('sb%d' % l))
        take('sb%d' % l, gathered, handed)

        dw_w = jnp.pad(w['conv_dw_w'][l], ((0, CONV_HALO - CONV_WIDTH), (0, 0)))
        hc = dwconv(glu(u_conv, tag + 'conv_glu'), dw_w, w['conv_dw_b'][l].reshape(1, -1), tag + 'dwconv')
        o_conv, = fused_(ln_silu_block, [hc], [gain('conv_ln_g', l), gain('conv_ln_b', l)], 'conv_pw2_w', l,
                         tag + 'pw2')

        ar, ai, bbr, bbi, cr, ci = ssm_discretise(
            w['ssm_lam_re'][l], w['ssm_lam_im'][l], w['ssm_log_dt'][l], w['ssm_b_re'][l], w['ssm_b_im'][l],
            w['ssm_c_re'][l], w['ssm_c_im'][l])
        y = ssm_core(u_ssm, ar, ai, bbr, bbi, cr, ci, w['ssm_d'][l].reshape(1, -1), tag + 'ssm')
        o_ssm = glu(linear_(y, 'ssm_glu_w', l, tag + 'ssm_glu_w'), tag + 'ssm_glu')

        x, = fused_(branch_norms_block, [o_sb, o_conv, o_ssm], [gain('branch_norm_g', l)], 'w_out', l,
                    tag + 'w_out', residual=x)

        q_raw, x = fused_(_rms, [x], [gain('norm_xa_g', l)], 'xa_wq', l, tag + 'xa_wq', carry=(0,))
        hm = rmsnorm(mem, w['norm_mem_g'][l], tag + 'norm_mem')
        qx = groupnorm(q_raw, w['xa_q_norm_g'][l], XA_HEAD_DIM, tag + 'xa_qn')
        kx = groupnorm(linear_(hm, 'xa_wk', l, tag + 'xa_wk'), w['xa_k_norm_g'][l], XA_HEAD_DIM, tag + 'xa_kn')
        vx = linear_(hm, 'xa_wv', l, tag + 'xa_wv')
        x, = fused_(xa_core_block, [qx], [kx, vx], 'xa_wo', l, tag + 'xa_wo', residual=x,
                    block_rows=min(256, qx.shape[0]))

        host = 'ffn0' if l == 0 else None
        x, gathered, handed = ffn(x, gain('norm_ffn_g', l), mats[('ffn_w_in', l)], slot[('ffn_w_in', l)],
                                  mats[('ffn_w_out', l)], slot[('ffn_w_out', l)], tag + 'ffn',
                                  riders_at(host) if host else NO_RIDERS)
        if host:
            take(host, gathered, handed)
    return jnp.sum(loss_rows(x, target, 'loss'))


def local_step(w, mats, shards, x, mem, target):
    def shard_shape(key):
        return shards[key].shape if key in shards else _shard_shape(key[0], mats[key].shape)

    slots = {}
    for host, keys in SCATTER_AT.items():
        for key in keys:
            shape = _whole_shape(key[0], shard_shape(key)) if host == 'end' else (N_CHIPS,) + shard_shape(key)
            slots[key] = jnp.zeros(shape, BF16)
    loss, (g_mats, g_w, gx) = jax.value_and_grad(local_loss, argnums=(0, 1, 4))(
        slots, w, mats, shards, x, mem, target)
    return loss, gx, g_w, g_mats


PACK_ROWS = 2048


PIECE_ROWS = 16


def _piece_rows(size):
    rows = -(-size // LANES)
    return rows, -(-rows // PIECE_ROWS) * PIECE_ROWS


def pack(arrays, dtype):
    parts, total = [], 0
    for a in arrays:
        rows, padded = _piece_rows(a.size)
        a = a.astype(dtype)
        if a.size % LANES:
            a = jnp.pad(a.reshape(-1), (0, rows * LANES - a.size))
        a = a.reshape(rows, LANES)
        if padded != rows:
            a = jnp.pad(a, ((0, padded - rows), (0, 0)))
        parts.append(a)
        total += padded
    tail = -total % PACK_ROWS
    if tail:
        parts.append(jnp.zeros((tail, LANES), dtype))
    return jnp.concatenate(parts, axis=0)


def unpack(packed, shapes):
    out, off = [], 0
    for s in shapes:
        size = math.prod(s)
        rows, padded = _piece_rows(size)
        piece = packed[off:off + rows]
        if size % LANES:
            piece = piece.reshape(-1)[:size]
        out.append(piece.reshape(s))
        off += padded
    return out


def _mesh_pos():
    return lax.axis_index("x"), lax.axis_index("y"), lax.axis_index("c")


def _exchange_xy(n_arrays, src_of, dst_of, sems, wait):
    send_sems, recv_sems, local_sems = sems
    x, y, c = _mesh_pos()
    me = 2 * x + y
    peers = [(1 - x, y), (x, 1 - y), (1 - x, 1 - y)]
    for k in range(n_arrays):
        own = pltpu.make_async_copy(src_of(k, me), dst_of(k, me), local_sems.at[k])
        if wait:
            own.wait()
        else:
            own.start()
        for p, (px, py) in enumerate(peers):
            out = pltpu.make_async_remote_copy(
                src_ref=src_of(k, 2 * px + py), dst_ref=dst_of(k, me), send_sem=send_sems.at[3 * k + p],
                recv_sem=recv_sems.at[3 * k + p], device_id=(px, py, c), device_id_type=MESH)
            if wait:
                pltpu.make_async_remote_copy(
                    src_ref=src_of(k, me), dst_ref=dst_of(k, 2 * px + py), send_sem=send_sems.at[3 * k + p],
                    recv_sem=recv_sems.at[3 * k + p], device_id=(px, py, c), device_id_type=MESH).wait_recv()
                out.wait_send()
            else:
                out.start()


class Ride:
    def __init__(self, arrays, out_shapes, src_of, dst_of):
        self.arrays, self.out_shapes = list(arrays), list(out_shapes)
        self._src_of, self._dst_of = src_of, dst_of
        n = len(self.arrays)
        self.in_specs = [pl.BlockSpec(memory_space=pl.ANY)] * n
        self.out_specs = [pl.BlockSpec(memory_space=pl.ANY)] * len(self.out_shapes)
        self.out_shape = [jax.ShapeDtypeStruct(s, BF16) for s in self.out_shapes]
        self.scratch = [pltpu.SemaphoreType.DMA((3 * n,)), pltpu.SemaphoreType.DMA((3 * n,)),
                        pltpu.SemaphoreType.DMA((n,))]

    def run(self, parts, wait):
        ins, outs, sems = parts
        _exchange_xy(len(self.arrays), lambda k, chip: self._src_of(ins, k, chip),
                     lambda k, chip: self._dst_of(outs, k, chip), sems, wait)

    def at_ends(self, parts, first, last):
        pl.when(first)(lambda: self.run(parts, False))

        def finish():
            pl.when(last)(lambda: self.run(parts, True))
        return finish


def _split_refs(ride, n_in, n_out, refs):
    if ride is None:
        return refs[:n_in], refs[n_in:n_in + n_out], refs[n_in + n_out:], None
    ni, no = len(ride.arrays), len(ride.out_shapes)
    b = n_in + ni
    c = b + n_out
    d = c + no
    return refs[:n_in], refs[b:c], refs[d:len(refs) - 3], (refs[n_in:b], refs[c:d], refs[len(refs) - 3:])


def run_ride(ride, name):
    def body(*refs):
        parts = _split_refs(ride, 0, 0, refs)[3]
        ride.run(parts, False)
        ride.run(parts, True)

    return pl.pallas_call(
        body, name=name, in_specs=ride.in_specs, out_specs=ride.out_specs, out_shape=ride.out_shape,
        scratch_shapes=ride.scratch, compiler_params=pltpu.CompilerParams(has_side_effects=True),
    )(*ride.arrays)


SLOTTED = ('w_in', 'taps')


def _part(ref, axis, chip, size):
    start = pl.multiple_of(chip * size, size)
    index = [slice(None)] * len(ref.shape)
    index[axis] = pl.ds(start, size)
    return ref.at[tuple(index)]


def _whole_shape(name, shard_shape):
    s = list(shard_shape)
    s[SHARD_AXIS[name] - 1] *= N_CHIPS
    return tuple(s)


def _shard_shape(name, whole_shape):
    s = list(whole_shape)
    s[SHARD_AXIS[name] - 1] //= N_CHIPS
    return tuple(s)


def gather_ride(names, shards):
    def out_shape(k):
        return (N_CHIPS,) + shards[k].shape if names[k] in SLOTTED else _whole_shape(names[k], shards[k].shape)

    def dst_of(outs, k, chip):
        if names[k] in SLOTTED:
            return outs[k].at[chip]
        axis = SHARD_AXIS[names[k]] - 1
        return _part(outs[k], axis, chip, shards[k].shape[axis])

    return Ride(shards, [out_shape(k) for k in range(len(shards))], lambda ins, k, chip: ins[k], dst_of)


def scatter_ride(names, grads):
    def shard_shape(k):
        return grads[k].shape[1:] if names[k] in SLOTTED else _shard_shape(names[k], grads[k].shape)

    def src_of(ins, k, chip):
        if names[k] in SLOTTED:
            return ins[k].at[chip]
        axis = SHARD_AXIS[names[k]] - 1
        return _part(ins[k], axis, chip, shard_shape(k)[axis])

    return Ride(grads, [(N_CHIPS,) + shard_shape(k) for k in range(len(grads))], src_of,
                lambda outs, k, chip: outs[k].at[chip])


def _cut_for_chips(name, g):
    if name not in SLOTTED:
        return g
    axis = SHARD_AXIS[name] - 1
    cut = g.shape[:axis] + (N_CHIPS, g.shape[axis] // N_CHIPS) + g.shape[axis + 1:]
    return jnp.moveaxis(g.reshape(cut), axis, 0)


def swap_cores(arrays, name):
    na = len(arrays)

    def body(*refs):
        ins, outs, send_sems, recv_sems = refs[:na], refs[na:2 * na], refs[2 * na], refs[2 * na + 1]
        x, y, c = _mesh_pos()
        copies = [pltpu.make_async_remote_copy(
            src_ref=ins[k], dst_ref=outs[k], send_sem=send_sems.at[k], recv_sem=recv_sems.at[k],
            device_id=(x, y, 1 - c), device_id_type=MESH) for k in range(na)]
        for cp in copies:
            cp.start()
        for cp in copies:
            cp.wait()

    return pl.pallas_call(
        body, name=name,
        in_specs=[pl.BlockSpec(memory_space=pl.ANY)] * na,
        out_specs=[pl.BlockSpec(memory_space=pl.ANY)] * na,
        out_shape=[jax.ShapeDtypeStruct(a.shape, a.dtype) for a in arrays],
        scratch_shapes=[pltpu.SemaphoreType.DMA((na,)), pltpu.SemaphoreType.DMA((na,))],
        compiler_params=pltpu.CompilerParams(has_side_effects=True),
    )(*arrays)


def allreduce_small(buf, name):
    R = buf.shape[0]

    def body(in_ref, sum_ref, all_ref, send_sems, recv_sems):
        x, y, c = _mesh_pos()
        me = 4 * x + 2 * y + c
        all_ref[me] = in_ref[...]
        flips = [(fx, fy, fc) for fx in (0, 1) for fy in (0, 1) for fc in (0, 1)][1:]
        sends = []
        for k, (fx, fy, fc) in enumerate(flips):
            cp = pltpu.make_async_remote_copy(
                src_ref=in_ref, dst_ref=all_ref.at[me], send_sem=send_sems.at[k], recv_sem=recv_sems.at[k],
                device_id=(x ^ fx, y ^ fy, c ^ fc), device_id_type=MESH)
            cp.start()
            sends.append(cp)
        for k, (fx, fy, fc) in enumerate(flips):
            peer = 4 * (x ^ fx) + 2 * (y ^ fy) + (c ^ fc)
            pltpu.make_async_remote_copy(
                src_ref=in_ref, dst_ref=all_ref.at[peer], send_sem=send_sems.at[k], recv_sem=recv_sems.at[k],
                device_id=(x ^ fx, y ^ fy, c ^ fc), device_id_type=MESH).wait_recv()
        for cp in sends:
            cp.wait_send()
        acc = all_ref[0]
        for k in range(1, N_DEV):
            acc = acc + all_ref[k]
        sum_ref[...] = acc

    return pl.pallas_call(
        body, name=name,
        in_specs=[pl.BlockSpec(memory_space=pltpu.VMEM)],
        out_specs=[pl.BlockSpec(memory_space=pltpu.VMEM), pl.BlockSpec(memory_space=pltpu.VMEM)],
        out_shape=[jax.ShapeDtypeStruct((R, LANES), F32), jax.ShapeDtypeStruct((N_DEV, R, LANES), F32)],
        scratch_shapes=[pltpu.SemaphoreType.DMA((N_DEV - 1,)), pltpu.SemaphoreType.DMA((N_DEV - 1,))],
        compiler_params=pltpu.CompilerParams(has_side_effects=True, vmem_limit_bytes=VMEM_LIMIT),
    )(buf)[0]


def _adamw_update(g, w, m, v):
    m2 = ADAM_B1 * m + (1.0 - ADAM_B1) * g
    v2 = ADAM_B2 * v + (1.0 - ADAM_B2) * (g * g)
    m_hat = m2 / (1.0 - ADAM_B1 ** ADAM_STEP)
    v_hat = v2 / (1.0 - ADAM_B2 ** ADAM_STEP)
    return -ADAM_LR * (m_hat / (jnp.sqrt(v_hat) + ADAM_EPS) + ADAM_WD * w), m2, v2


def adamw_matrix(layer, mine, other, w, m, v, so_far, name):
    _, rows, cols = w.shape
    T = 16
    while rows % (2 * T) == 0 and 2 * T * cols <= 128 * 1024:
        T *= 2

    def body(mine_ref, other_ref, w_ref, m_ref, v_ref, *rest):
        g_out, d_out, m_out, v_out = rest[-4:]

        def total(ref):
            acc = ref[0].astype(F32)
            for k in range(1, N_CHIPS):
                acc = acc + ref[k].astype(F32)
            return acc

        g = total(mine_ref) + total(other_ref)
        g_out[...] = g
        d_out[...], m_out[...], v_out[...] = _adamw_update(g, w_ref[...], m_ref[...], v_ref[...])

    slots = pl.BlockSpec((N_CHIPS, T, cols), lambda i: (0, i, 0))
    spec = pl.BlockSpec((None, T, cols), lambda i: (layer, i, 0))
    filled = [] if so_far is None else list(so_far)
    return pl.pallas_call(
        body, name=name, grid=(rows // T,),
        in_specs=[slots, slots, spec, spec, spec] + [pl.BlockSpec(memory_space=pl.ANY)] * len(filled),
        out_specs=[spec] * 4,
        out_shape=[jax.ShapeDtypeStruct(w.shape, F32)] * 4,
        input_output_aliases={5 + j: j for j in range(len(filled))},
        compiler_params=_cparams(("parallel",)),
    )(mine, other, w, m, v, *filled)


def adamw_small(gs, ws, ms, vs, name):
    n = len(gs)

    def body(*refs):
        for k in range(n):
            g, w, m, v = (refs[j * n + k][...] for j in range(4))
            d_out, m_out, v_out = (refs[(4 + j) * n + k] for j in range(3))
            d_out[...], m_out[...], v_out[...] = _adamw_update(g, w, m, v)

    vmem = pl.BlockSpec(memory_space=pltpu.VMEM)
    outs = pl.pallas_call(
        body, name=name,
        in_specs=[vmem] * (4 * n), out_specs=[vmem] * (3 * n),
        out_shape=[jax.ShapeDtypeStruct(w.shape, F32) for w in ws] * 3,
        compiler_params=_cparams(),
    )(*gs, *ws, *ms, *vs)
    return outs[:n], outs[n:2 * n], outs[2 * n:]


def _shard_of(full, axis, chip):
    size = full.shape[axis] // N_CHIPS
    return lax.slice_in_dim(full, chip * size, (chip + 1) * size, axis=axis)


def kernel(x, mem, norm_mix_g, w_in, sb_q_norm_g, sb_k_norm_g, conv_dw_w, conv_dw_b, conv_ln_g, conv_ln_b, conv_pw2_w, ssm_lam_re, ssm_lam_im, ssm_log_dt, ssm_b_re, ssm_b_im, ssm_c_re, ssm_c_im, ssm_d, ssm_glu_w, branch_norm_g, w_out, norm_xa_g, norm_mem_g, xa_wq, xa_wk, xa_wv, xa_q_norm_g, xa_k_norm_g, xa_wo, norm_ffn_g, ffn_w_in, ffn_w_out, loss_target, m_norm_mix_g, m_w_in, m_sb_q_norm_g, m_sb_k_norm_g, m_conv_dw_w, m_conv_dw_b, m_conv_ln_g, m_conv_ln_b, m_conv_pw2_w, m_ssm_lam_re, m_ssm_lam_im, m_ssm_log_dt, m_ssm_b_re, m_ssm_b_im, m_ssm_c_re, m_ssm_c_im, m_ssm_d, m_ssm_glu_w, m_branch_norm_g, m_w_out, m_norm_xa_g, m_norm_mem_g, m_xa_wq, m_xa_wk, m_xa_wv, m_xa_q_norm_g, m_xa_k_norm_g, m_xa_wo, m_norm_ffn_g, m_ffn_w_in, m_ffn_w_out, v_norm_mix_g, v_w_in, v_sb_q_norm_g, v_sb_k_norm_g, v_conv_dw_w, v_conv_dw_b, v_conv_ln_g, v_conv_ln_b, v_conv_pw2_w, v_ssm_lam_re, v_ssm_lam_im, v_ssm_log_dt, v_ssm_b_re, v_ssm_b_im, v_ssm_c_re, v_ssm_c_im, v_ssm_d, v_ssm_glu_w, v_branch_norm_g, v_w_out, v_norm_xa_g, v_norm_mem_g, v_xa_wq, v_xa_wk, v_xa_wv, v_xa_q_norm_g, v_xa_k_norm_g, v_xa_wo, v_norm_ffn_g, v_ffn_w_in, v_ffn_w_out):
    given = dict(locals())
    w = {n: given[n] for n in WEIGHTS}
    m = {n: given['m_' + n] for n in WEIGHTS}
    v = {n: given['v_' + n] for n in WEIGHTS}

    depth = w_in.shape[0]
    chip = 2 * lax.axis_index("x") + lax.axis_index("y")

    assert depth == DEPTH
    taps_bits = lax.bitcast_convert_type(conv_dw_w, BF16)
    shards = {(n, l): w[n][l].astype(BF16) for n in MATRICES for l in range(depth)}
    first = [shards.pop(key) for key in GATHER_AT['start']]
    gathered = run_ride(gather_ride([n for n, _ in GATHER_AT['start']] + ['taps'], first + [taps_bits]),
                        'gather_first')
    mats = {key: _assemble(key[0], g) for key, g in zip(GATHER_AT['start'], gathered)}
    taps = jnp.concatenate([lax.bitcast_convert_type(gathered[-1][j], F32) for j in range(N_CHIPS)], axis=2)
    local_w = {n: w[n] for n in REPLICATED}
    local_w['conv_dw_w'] = taps

    loss, gx, g_w, mine = local_step(local_w, mats, shards, x[0], mem[0], loss_target[0])
    last = SCATTER_AT['end']
    received = run_ride(scatter_ride([n for n, _ in last], [_cut_for_chips(n, mine[(n, l)]) for n, l in last]),
                        'scatter_last')
    mine.update(zip(last, received))

    keys = [(n, l) for n in MATRICES for l in range(depth)]
    other = dict(zip(keys, swap_cores([mine[key] for key in keys], 'swap_cores')))
    outs = {}
    for n in MATRICES:
        res = None
        for l in range(depth):
            res = adamw_matrix(l, mine[(n, l)], other[(n, l)], w[n], m[n], v[n], res, 'adamw_%s_%d' % (n, l))
        outs['grad_' + n], outs['delta_' + n], outs['new_m_' + n], outs['new_v_' + n] = res

    reduced = allreduce_small(pack([g_w[n] for n in REPLICATED] + [g_w['conv_dw_w'], loss.reshape(1)], F32),
                              'allreduce_small')
    reduced = unpack(reduced, [w[n].shape for n in REPLICATED] + [taps.shape, (1,)])
    total_loss = reduced[-1].reshape(())
    tap_cols = conv_dw_w.shape[2]
    reduced[-2] = lax.dynamic_slice_in_dim(reduced[-2], chip * tap_cols, tap_cols, axis=2)
    small_names = REPLICATED + ['conv_dw_w']
    deltas, new_ms, new_vs = adamw_small(reduced[:-1], [w[n] for n in small_names], [m[n] for n in small_names],
                                         [v[n] for n in small_names], 'adamw_small')
    for k, n in enumerate(small_names):
        outs['grad_' + n], outs['delta_' + n] = reduced[k], deltas[k]
        outs['new_m_' + n], outs['new_v_' + n] = new_ms[k], new_vs[k]
    return (total_loss, gx[None], *[outs['grad_' + n] for n in WEIGHTS], *[outs['delta_' + n] for n in WEIGHTS],
            *[outs['new_m_' + n] for n in WEIGHTS], *[outs['new_v_' + n] for n in WEIGHTS])
```

```python
import functools
import math

import jax
import jax.numpy as jnp
from jax import lax
from jax.experimental import pallas as pl
from jax.experimental.pallas import tpu as pltpu

F32 = jnp.float32
BF16 = jnp.bfloat16
MESH = pl.DeviceIdType.MESH
HIGHEST = lax.Precision.HIGHEST

EPS = 1e-6
LANES = 128
SUBLANES = 8
VMEM_LIMIT = 56 * 1024 * 1024

SB_HEAD_DIM = 64
SB_WIDTH = 512
CONV_CH = 256
CONV_WIDTH = 31
CONV_HALO = 32
SSM_CH = 256
SSM_GROUPS = 16
SSM_GROUP = 16
SSM_STATE = 64
SSM_LANES = SSM_GROUPS * SSM_STATE
XA_HEADS = 4
XA_HEAD_DIM = 256
SB_CUT = 110.0

ADAM_LR = 0.001
ADAM_B1 = 0.9
ADAM_B2 = 0.999
ADAM_EPS = 1e-08
ADAM_WD = 0.01
ADAM_STEP = 10

WEIGHTS = ['norm_mix_g', 'w_in', 'sb_q_norm_g', 'sb_k_norm_g', 'conv_dw_w', 'conv_dw_b', 'conv_ln_g',
           'conv_ln_b', 'conv_pw2_w', 'ssm_lam_re', 'ssm_lam_im', 'ssm_log_dt', 'ssm_b_re', 'ssm_b_im',
           'ssm_c_re', 'ssm_c_im', 'ssm_d', 'ssm_glu_w', 'branch_norm_g', 'w_out', 'norm_xa_g',
           'norm_mem_g', 'xa_wq', 'xa_wk', 'xa_wv', 'xa_q_norm_g', 'xa_k_norm_g', 'xa_wo', 'norm_ffn_g',
           'ffn_w_in', 'ffn_w_out']
SHARD_AXIS = {'w_in': 2, 'conv_dw_w': 2, 'conv_pw2_w': 1, 'ssm_glu_w': 2, 'w_out': 1, 'xa_wq': 1,
              'xa_wk': 1, 'xa_wv': 1, 'xa_wo': 1, 'ffn_w_in': 2, 'ffn_w_out': 1}
MATRICES = [n for n in WEIGHTS if n in SHARD_AXIS and n != 'conv_dw_w']
REPLICATED = [n for n in WEIGHTS if n not in SHARD_AXIS]
N_CHIPS = 4
N_DEV = 8


def _cparams(sem=None, **kw):
    if sem is not None:
        kw['dimension_semantics'] = sem
    return pltpu.CompilerParams(vmem_limit_bytes=VMEM_LIMIT, **kw)


def _pick(n, target):
    best = None
    d = LANES
    while d <= min(n, target):
        if n % d == 0:
            best = d
        d += LANES
    return best if best is not None else n


def _rows_for(n_rows, width):
    t = 512
    while t > 8 and t * width > 768 * 1024:
        t //= 2
    return min(t, n_rows)


def _dg(a, b, ca, cb):
    return lax.dot_general(a.astype(BF16), b.astype(BF16), (((ca,), (cb,)), ((), ())),
                           preferred_element_type=F32)


@jax.custom_vjp
def bdot_nn(a, b):
    return _dg(a, b, 1, 0)


def _bdot_nn_fwd(a, b):
    return _dg(a, b, 1, 0), (a, b)


def _bdot_nn_bwd(res, g):
    a, b = res
    return _dg(g, b, 1, 1), _dg(a, g, 0, 0)


bdot_nn.defvjp(_bdot_nn_fwd, _bdot_nn_bwd)


@jax.custom_vjp
def bdot_nt(a, b):
    return _dg(a, b, 1, 1)


def _bdot_nt_fwd(a, b):
    return _dg(a, b, 1, 1), (a, b)


def _bdot_nt_bwd(res, g):
    a, b = res
    return _dg(g, b, 1, 0), _dg(g, a, 0, 0)


bdot_nt.defvjp(_bdot_nt_fwd, _bdot_nt_bwd)


def mm(a, b, mode, name, out_dtype=F32, add=None, ride=None):
    if mode == 'nn':
        M, K = a.shape
        N = b.shape[1]
    elif mode == 'nt':
        M, K = a.shape
        N = b.shape[0]
    else:
        K, M = a.shape
        N = b.shape[1]
    if mode == 'tn':
        tm, tn, tk = _pick(M, 1536), _pick(N, 2816), _pick(K, 512)
    else:
        tm, tn = _pick(M, 1024), _pick(N, 1536)
        tk = K if K <= 2816 else _pick(K, 1536)
    nk = K // tk
    ca, cb = {'nn': (1, 0), 'nt': (1, 1), 'tn': (0, 0)}[mode]

    grid = (M // tm, N // tn, nk)

    def body(*refs):
        ins, (o_ref,), scratch, riding = _split_refs(ride, 3 if add is not None else 2, 1, refs)
        a_ref, b_ref = ins[:2]
        add_ref = ins[2] if add is not None else None
        i, j, k = pl.program_id(0), pl.program_id(1), pl.program_id(2)
        ride_done = None
        if riding is not None:
            first = jnp.logical_and(i == 0, jnp.logical_and(j == 0, k == 0))
            last = jnp.logical_and(i == grid[0] - 1, jnp.logical_and(j == grid[1] - 1, k == nk - 1))
            ride_done = ride.at_ends(riding, first, last)

        def finish(acc):
            if add_ref is not None:
                acc = acc + add_ref[...]
            o_ref[...] = acc.astype(out_dtype)

        if nk == 1:
            finish(_dg(a_ref[...], b_ref[...], ca, cb))
        else:
            acc_ref, = scratch

            @pl.when(k == 0)
            def _():
                acc_ref[...] = jnp.zeros_like(acc_ref)

            acc_ref[...] += _dg(a_ref[...], b_ref[...], ca, cb)

            @pl.when(k == nk - 1)
            def _():
                finish(acc_ref[...])

        if ride_done is not None:
            ride_done()

    if mode == 'nn':
        a_spec = pl.BlockSpec((tm, tk), lambda i, j, k: (i, k))
        b_spec = pl.BlockSpec((tk, tn), lambda i, j, k: (k, j))
    elif mode == 'nt':
        a_spec = pl.BlockSpec((tm, tk), lambda i, j, k: (i, k))
        b_spec = pl.BlockSpec((tn, tk), lambda i, j, k: (j, k))
    else:
        a_spec = pl.BlockSpec((tk, tm), lambda i, j, k: (k, i))
        b_spec = pl.BlockSpec((tk, tn), lambda i, j, k: (k, j))
    out_spec = pl.BlockSpec((tm, tn), lambda i, j, k: (i, j))
    own_in = [a_spec, b_spec] + ([out_spec] if add is not None else [])
    operands = (a, b) if add is None else (a, b, add)
    scratch = [pltpu.VMEM((tm, tn), F32)] if nk > 1 else []
    if ride is None:
        return pl.pallas_call(
            body, name=name, grid=grid, in_specs=own_in, out_specs=out_spec,
            out_shape=jax.ShapeDtypeStruct((M, N), out_dtype), scratch_shapes=scratch,
            compiler_params=_cparams(("parallel", "parallel", "arbitrary")),
        )(*operands)
    outs = pl.pallas_call(
        body, name=name, grid=grid, in_specs=own_in + ride.in_specs, out_specs=[out_spec] + ride.out_specs,
        out_shape=[jax.ShapeDtypeStruct((M, N), out_dtype)] + ride.out_shape,
        scratch_shapes=scratch + ride.scratch,
        compiler_params=_cparams(("arbitrary", "arbitrary", "arbitrary"), has_side_effects=True),
    )(*operands, *ride.arrays)
    return outs[0], outs[1:]


class Riders:
    def __init__(self, gather=(), scatter=()):
        self.gather_names = [n for n, _ in gather]
        self.shards = tuple(s for _, s in gather)
        self.scatter_names = [n for n, _ in scatter]
        self.slots = tuple(s for _, s in scatter)

    def gather_ride(self, shards):
        return gather_ride(self.gather_names, list(shards)) if shards else None

    def scatter_ride(self, grads):
        if not grads:
            return None
        return scatter_ride(self.scatter_names, [_cut_for_chips(n, g) for n, g in zip(self.scatter_names, grads)])

    def whole_slots(self, slots):
        return tuple(jnp.zeros(_whole_shape(n, s.shape[1:]), BF16) for n, s in zip(self.scatter_names, slots))


NO_RIDERS = Riders()


def linear(x, w, slot, name, residual=None, riders=NO_RIDERS):
    @jax.custom_vjp
    def op(x, w, slot, residual, shards, slots):
        ride = riders.gather_ride(shards)
        y = mm(x, w, 'nn', name + '_fwd', add=residual, ride=ride)
        y, gathered = y if ride is not None else (y, ())
        return y, tuple(gathered), riders.whole_slots(slots)

    def op_fwd(x, w, slot, residual, shards, slots):
        return op(x, w, slot, residual, shards, slots), (x, w, shards)

    def op_bwd(res, cts):
        x, w, shards = res
        g, _, slot_grads = cts
        ride = riders.scatter_ride(slot_grads)
        dx = mm(g, w, 'nt', name + '_dx', ride=ride)
        dx, received = dx if ride is not None else (dx, ())
        return (dx, jnp.zeros_like(w), mm(x, g, 'tn', name + '_dw', BF16), None if residual is None else g,
                tuple(jnp.zeros_like(s) for s in shards), tuple(received))

    op.defvjp(op_fwd, op_bwd)
    return op(x, w, slot, residual, riders.shards, riders.slots)


def _rowwise_calls(f, rows, params, consts, out_widths, name, need_row_grad=None, block_rows=None, carry=(),
                   out_dtype=F32):
    nr, npar, nc, nout = len(rows), len(params), len(consts), len(out_widths)
    carry = tuple(carry)
    L = rows[0].shape[0]
    widths = [r.shape[1] for r in rows]
    T = block_rows or _rows_for(L, max(widths + list(out_widths)))
    n = L // T
    need = list(need_row_grad) if need_row_grad is not None else [True] * nr
    pshapes = [p.shape for p in params]
    cshapes = [c.shape for c in consts]

    row_specs = [pl.BlockSpec((T, w), lambda i: (i, 0)) for w in widths]
    par_specs = [pl.BlockSpec(s, lambda i: (0, 0)) for s in pshapes]
    con_specs = [pl.BlockSpec(s, lambda i: (0, 0)) for s in cshapes]
    out_specs = [pl.BlockSpec((T, w), lambda i: (i, 0)) for w in out_widths]

    def fwd_call(rows, params, consts):
        def body(*refs):
            ins = [r[...] for r in refs[:nr + npar + nc]]
            outs = f(*ins)
            for o_ref, val in zip(refs[nr + npar + nc:], outs):
                o_ref[...] = val.astype(out_dtype)

        return pl.pallas_call(
            body, name=name + '_fwd', grid=(n,),
            in_specs=row_specs + par_specs + con_specs, out_specs=out_specs,
            out_shape=[jax.ShapeDtypeStruct((L, w), out_dtype) for w in out_widths],
            compiler_params=_cparams(("parallel",)),
        )(*rows, *params, *consts)

    def bwd_call(rows, params, consts, cts, carried):
        grad_rows = [k for k in range(nr) if need[k]]
        n_in = nr + npar + nc + nout

        def body(*refs):
            i = pl.program_id(0)
            rv = [r[...] for r in refs[:nr]]
            pv = [r[...] for r in refs[nr:nr + npar]]
            cv = [r[...] for r in refs[nr + npar:nr + npar + nc]]
            ctv = tuple(r[...] for r in refs[nr + npar + nc:n_in])
            carried_refs = dict(zip(carry, refs[n_in:n_in + len(carry)]))
            orefs = refs[n_in + len(carry):]
            _, vjp = jax.vjp(lambda *rp: tuple(f(*rp, *cv)), *rv, *pv)
            g = vjp(ctv)
            for slot, k in enumerate(grad_rows):
                orefs[slot][...] = g[k] + carried_refs[k][...] if k in carried_refs else g[k]

            @pl.when(i == 0)
            def _():
                for k in range(npar):
                    orefs[len(grad_rows) + k][...] = jnp.zeros(pshapes[k], F32)

            for k in range(npar):
                orefs[len(grad_rows) + k][...] += g[nr + k]

        outs = pl.pallas_call(
            body, name=name + '_bwd', grid=(n,),
            in_specs=row_specs + par_specs + con_specs + out_specs + [row_specs[k] for k in carry],
            out_specs=[row_specs[k] for k in grad_rows] + par_specs,
            out_shape=[jax.ShapeDtypeStruct((L, widths[k]), F32) for k in grad_rows]
            + [jax.ShapeDtypeStruct(s, F32) for s in pshapes],
            compiler_params=_cparams(("arbitrary",)),
        )(*rows, *params, *consts, *cts, *carried)
        drows = []
        slot = 0
        for k in range(nr):
            if need[k]:
                drows.append(outs[slot])
                slot += 1
            else:
                drows.append(jnp.zeros_like(rows[k]))
        return tuple(drows), tuple(outs[len(grad_rows):])

    return fwd_call, bwd_call


def rowwise(f, rows, params, consts, out_widths, name, need_row_grad=None, block_rows=None, carry=()):
    fwd_call, bwd_call = _rowwise_calls(f, rows, params, consts, out_widths, name, need_row_grad, block_rows, carry)
    nout = len(out_widths)

    @jax.custom_vjp
    def op(rows, params, consts):
        return tuple(fwd_call(rows, params, consts)) + tuple(rows[k] for k in carry)

    def op_fwd(rows, params, consts):
        return op(rows, params, consts), (rows, params, consts)

    def op_bwd(res, cts):
        rows, params, consts = res
        drows, dparams = bwd_call(rows, params, consts, cts[:nout], cts[nout:])
        return drows, dparams, tuple(jnp.zeros_like(c) for c in consts)

    op.defvjp(op_fwd, op_bwd)
    return op(tuple(rows), tuple(params), tuple(consts))


def rowwise_linear(f, rows, params, consts, w, slot, name, residual=None, riders=NO_RIDERS, carry=(),
                   block_rows=None):
    width = w.shape[0]
    fwd_call, bwd_call = _rowwise_calls(lambda *a: (f(*a),), rows, params, consts, [width], name, None,
                                        block_rows, carry, BF16)

    @jax.custom_vjp
    def op(rows, params, consts, w, slot, residual, shards, slots):
        h, = fwd_call(rows, params, consts)
        ride = riders.gather_ride(shards)
        y = mm(h, w, 'nn', name + '_mm', add=residual, ride=ride)
        y, gathered = y if ride is not None else (y, ())
        return y, tuple(rows[k] for k in carry), tuple(gathered), riders.whole_slots(slots)

    def op_fwd(rows, params, consts, w, slot, residual, shards, slots):
        h, = fwd_call(rows, params, consts)
        ride = riders.gather_ride(shards)
        y = mm(h, w, 'nn', name + '_mm', add=residual, ride=ride)
        y, gathered = y if ride is not None else (y, ())
        out = (y, tuple(rows[k] for k in carry), tuple(gathered), riders.whole_slots(slots))
        return out, (rows, params, consts, h, w, shards)

    def op_bwd(res, cts):
        rows, params, consts, h, w, shards = res
        g, carried, _, slot_grads = cts
        ride = riders.scatter_ride(slot_grads)
        dh = mm(g, w, 'nt', name + '_dx', ride=ride)
        dh, received = dh if ride is not None else (dh, ())
        drows, dparams = bwd_call(rows, params, consts, (dh,), carried)
        return (drows, dparams, tuple(jnp.zeros_like(c) for c in consts), jnp.zeros_like(w),
                mm(h, g, 'tn', name + '_dw', BF16), None if residual is None else g,
                tuple(jnp.zeros_like(s) for s in shards), tuple(received))

    op.defvjp(op_fwd, op_bwd)
    return op(tuple(rows), tuple(params), tuple(consts), w, slot, residual, riders.shards, riders.slots)


def _rms(x, g):
    return x * lax.rsqrt(jnp.mean(x * x, axis=-1, keepdims=True) + EPS) * g


def rmsnorm(x, g, name, carry=False):
    out = rowwise(lambda x, g: (_rms(x, g),), [x], [g.reshape(1, -1)], [], [x.shape[1]], name,
                  carry=(0,) if carry else ())
    return out if carry else out[0]


def _split2(x):
    hi = x.astype(BF16)
    return hi, (x - hi.astype(F32)).astype(BF16)


@jax.custom_vjp
def select_mm(x, sel):
    return sum(_dg(t, sel, 1, 0) for t in _split2(x))


def _select_mm_fwd(x, sel):
    return select_mm(x, sel), sel


def _select_mm_bwd(sel, g):
    return sum(_dg(t, sel, 1, 1) for t in _split2(g)), jnp.zeros_like(sel)


select_mm.defvjp(_select_mm_fwd, _select_mm_bwd)


def groupnorm(x, g, group, name):
    width = x.shape[1]
    g_full = jnp.tile(g.reshape(1, group), (1, width // group))
    if group % LANES == 0:
        def f(x, g_full):
            outs = []
            for lo in range(0, width, group):
                xs = x[:, lo:lo + group]
                outs.append(_rms(xs, g_full[:, lo:lo + group]))
            return (jnp.concatenate(outs, axis=-1),)

        return rowwise(f, [x], [g_full], [], [width], name)[0]

    gid = jnp.arange(width) // group
    sel = (gid[:, None] == jnp.arange(LANES)[None, :]).astype(BF16)

    def f(x, g_full, sel, sel_t):
        ms = select_mm(x * x, sel) * (1.0 / group)
        inv = select_mm(lax.rsqrt(ms + EPS), sel_t)
        return (x * inv * g_full,)

    return rowwise(f, [x], [g_full], [sel, sel.T], [width], name)[0]


def glu(x, name):
    half = x.shape[1] // 2

    def f(x):
        return (x[:, :half] * jax.nn.sigmoid(x[:, half:]),)

    return rowwise(f, [x], [], [], [half], name)[0]


def swiglu_block(x):
    half = x.shape[1] // 2
    gate = x[:, :half]
    return gate * jax.nn.sigmoid(gate) * x[:, half:]


def ln_silu_block(x, g, b):
    mu = jnp.mean(x, axis=-1, keepdims=True)
    xc = x - mu
    var = jnp.mean(xc * xc, axis=-1, keepdims=True)
    y = xc * lax.rsqrt(var + EPS) * g + b
    return y * jax.nn.sigmoid(y)


def branch_norms_block(a, b, c, g):
    w1, w2 = a.shape[1], b.shape[1]
    return jnp.concatenate([_rms(a, g[:, :w1]), _rms(b, g[:, w1:w1 + w2]), _rms(c, g[:, w1 + w2:])], axis=-1)


def xa_core_block(q, k, v):
    scale = XA_HEAD_DIM ** -0.5
    outs = []
    for h in range(XA_HEADS):
        sl = slice(h * XA_HEAD_DIM, (h + 1) * XA_HEAD_DIM)
        s = bdot_nt(q[:, sl], k[:, sl]) * scale
        m = lax.stop_gradient(jnp.max(s, axis=-1, keepdims=True))
        e = jnp.exp(s - m)
        p = e / jnp.sum(e, axis=-1, keepdims=True)
        outs.append(bdot_nn(p, v[:, sl]))
    return jnp.concatenate(outs, axis=-1)


def loss_rows(y, target, name):
    def f(y, t):
        d = y - t
        return (0.5 * jnp.mean(d * d, axis=-1, keepdims=True),)

    return rowwise(f, [y, target], [], [], [1], name, need_row_grad=[True, False])[0]


def _swiglu(gate, up):
    return gate * jax.nn.sigmoid(gate) * up


def mm_swiglu(h, w, name, ride=None):
    M, K = h.shape
    H = w.shape[1] // 2
    tm, tn = _pick(M, 1024), _pick(H, 1536)
    nj = H // tn
    grid = (M // tm, 2 * nj)

    def body(*refs):
        (a_ref, b_ref), (gu_ref, act_ref), (kept,), riding = _split_refs(ride, 2, 2, refs)
        ride_done = _ride_ends(ride, riding, grid)
        j = pl.program_id(1)
        prod = _dg(a_ref[...], b_ref[...], 1, 0)
        gu_ref[...] = prod

        @pl.when(j < nj)
        def _():
            kept[j] = prod

        @pl.when(j >= nj)
        def _():
            act_ref[...] = _swiglu(kept[j - nj], prod).astype(BF16)

        ride_done()

    return _ride_call(
        body, name, grid,
        [pl.BlockSpec((tm, K), lambda i, j: (i, 0)), pl.BlockSpec((K, tn), lambda i, j: (0, j))],
        [pl.BlockSpec((tm, tn), lambda i, j: (i, j)),
         pl.BlockSpec((tm, tn), lambda i, j: (i, jnp.maximum(j - nj, 0)))],
        [jax.ShapeDtypeStruct((M, 2 * H), F32), jax.ShapeDtypeStruct((M, H), BF16)],
        (h, w), ride, ("parallel", "arbitrary"), scratch=[pltpu.VMEM((nj, tm, tn), F32)])


def mm_swiglu_bwd(dy, w_out, gu, name):
    M, D = dy.shape
    H = gu.shape[1] // 2
    tm, tn = _pick(M, 512), _pick(H, 1536)
    nj = H // tn

    def body(dy_ref, w_ref, gate_ref, up_ref, o_ref, kept):
        j = pl.program_id(1)

        @pl.when(j < nj)
        def _():
            dact = _dg(dy_ref[...], w_ref[...], 1, 1)
            _, vjp = jax.vjp(_swiglu, gate_ref[...], up_ref[...])
            dgate, dup = vjp(dact)
            o_ref[...] = dgate.astype(BF16)
            kept[j] = dup.astype(BF16)

        @pl.when(j >= nj)
        def _():
            o_ref[...] = kept[j - nj]

    def tile(j):
        return jnp.minimum(j, nj - 1)

    return pl.pallas_call(
        body, name=name, grid=(M // tm, 2 * nj),
        in_specs=[pl.BlockSpec((tm, D), lambda i, j: (i, 0)),
                  pl.BlockSpec((tn, D), lambda i, j: (tile(j), 0)),
                  pl.BlockSpec((tm, tn), lambda i, j: (i, tile(j))),
                  pl.BlockSpec((tm, tn), lambda i, j: (i, nj + tile(j)))],
        out_specs=pl.BlockSpec((tm, tn), lambda i, j: (i, j)),
        out_shape=jax.ShapeDtypeStruct((M, 2 * H), BF16),
        scratch_shapes=[pltpu.VMEM((nj, tm, tn), BF16)],
        compiler_params=_cparams(("parallel", "arbitrary")),
    )(dy, w_out, gu, gu)


def ffn(x, gain, w_in, slot_in, w_out, slot_out, name, riders=NO_RIDERS):
    norm_fwd, norm_bwd = _rowwise_calls(lambda x, g: (_rms(x, g),), [x], [gain], [], [x.shape[1]],
                                        name + '_norm', None, None, (0,), BF16)

    def forward(x, gain, w_in, w_out, shards):
        h, = norm_fwd((x,), (gain,), ())
        (gu, act), gathered = mm_swiglu(h, w_in, name + '_in', riders.gather_ride(shards))
        return mm(act, w_out, 'nn', name + '_out', add=x), tuple(gathered), (h, gu, act)

    @jax.custom_vjp
    def op(x, gain, w_in, slot_in, w_out, slot_out, shards, slots):
        y, gathered, _ = forward(x, gain, w_in, w_out, shards)
        return y, gathered, riders.whole_slots(slots)

    def op_fwd(x, gain, w_in, slot_in, w_out, slot_out, shards, slots):
        y, gathered, (h, gu, act) = forward(x, gain, w_in, w_out, shards)
        return (y, gathered, riders.whole_slots(slots)), (x, gain, h, gu, act, w_in, w_out, shards)

    def op_bwd(res, cts):
        x, gain, h, gu, act, w_in, w_out, shards = res
        g, _, slot_grads = cts
        dgu = mm_swiglu_bwd(g, w_out, gu, name + '_dact')
        dw_out = mm(act, g, 'tn', name + '_out_dw', BF16)
        dw_in = mm(h, dgu, 'tn', name + '_in_dw', BF16)
        ride = riders.scatter_ride(slot_grads)
        dh = mm(dgu, w_in, 'nt', name + '_in_dx', ride=ride)
        dh, received = dh if ride is not None else (dh, ())
        (dx,), (dgain,) = norm_bwd((x,), (gain,), (), (dh,), (g,))
        return (dx, dgain, jnp.zeros_like(w_in), dw_in, jnp.zeros_like(w_out), dw_out,
                tuple(jnp.zeros_like(s) for s in shards), tuple(received))

    op.defvjp(op_fwd, op_bwd)
    return op(x, gain, w_in, slot_in, w_out, slot_out, riders.shards, riders.slots)


def _hilo(x, ones_bf16):
    hi = x.astype(BF16)
    lo = (x - hi.astype(F32)).astype(BF16)
    return _dg(hi, ones_bf16, 1, 0) + _dg(lo, ones_bf16, 1, 0)


def _sb_block(qh, kb, c, valid, strict_upper):
    z = _dg(qh, kb, 1, 1)
    a = jnp.minimum(z, 0.0) - jnp.log(1.0 + jnp.exp(-jnp.abs(z)))
    b = jnp.where(valid, a - z, 0.0)
    s = _hilo(b, strict_upper) + c
    w = jnp.where(valid, jnp.exp(a + s), 0.0)
    return a, b, w


def _sb_masks(T):
    row = lax.broadcasted_iota(jnp.int32, (T, T), 0)
    col = lax.broadcasted_iota(jnp.int32, (T, T), 1)
    return col < row, (row > col).astype(BF16), (row >= col).astype(BF16)


def _sb_key_blocks(i, j, T, causal):
    second = jnp.maximum(j - 1, 0)
    return [(pl.multiple_of(j * T, T), jnp.logical_or(causal, j != i)),
            (pl.multiple_of(second * T, T), jnp.logical_and(jnp.logical_or(causal, True), j >= 1))]


HEADS_PER_BLOCK = LANES // SB_HEAD_DIM


def _head_mask(h):
    lane = lax.broadcasted_iota(jnp.int32, (1, LANES), 1)
    return (lane // SB_HEAD_DIM == h).astype(F32)


def _max_all(columns):
    m = columns[0]
    for c in columns[1:]:
        m = jnp.maximum(m, c)
    return jnp.max(m)


def _ride_call(body, name, grid, in_specs, out_specs, out_shape, operands, ride, semantics, scratch=()):
    if ride is None:
        outs = pl.pallas_call(body, name=name, grid=grid, in_specs=in_specs, out_specs=out_specs,
                              out_shape=out_shape, scratch_shapes=list(scratch),
                              compiler_params=_cparams(semantics))(*operands)
        return outs, ()
    outs = pl.pallas_call(
        body, name=name, grid=grid, in_specs=in_specs + ride.in_specs, out_specs=out_specs + ride.out_specs,
        out_shape=out_shape + ride.out_shape, scratch_shapes=list(scratch) + ride.scratch,
        compiler_params=_cparams(("arbitrary",) * len(grid), has_side_effects=True),
    )(*operands, *ride.arrays)
    return outs[:len(out_shape)], outs[len(out_shape):]


def _ride_ends(ride, riding, grid):
    if riding is None:
        return lambda: None
    first, last = None, None
    for axis, size in enumerate(grid):
        at0, at1 = pl.program_id(axis) == 0, pl.program_id(axis) == size - 1
        first = at0 if first is None else jnp.logical_and(first, at0)
        last = at1 if last is None else jnp.logical_and(last, at1)
    return ride.at_ends(riding, first, last)


def _sb_fwd_call(q, k, v, T, name, ride=None):
    L, W = q.shape
    scale = SB_HEAD_DIM ** -0.5
    grid = (W // LANES, L // T)

    def body(*refs):
        (q_ref, k_ref, v_ref), (o_ref,), _, riding = _split_refs(ride, 3, 1, refs)
        ride_done = _ride_ends(ride, riding, grid)
        i = pl.program_id(1)
        causal, strict_upper, _ = _sb_masks(T)
        q2 = q_ref[...] * scale
        masks = [_head_mask(h) for h in range(HEADS_PER_BLOCK)]
        qs = [(q2 * hm).astype(BF16) for hm in masks]
        zero = jnp.zeros((T, 1), F32)

        def cond(state):
            j, cs, _ = state
            return jnp.logical_and(j >= 0, _max_all(cs) > -SB_CUT)

        def step(state):
            j, cs, acc = state
            blocks = _sb_key_blocks(i, j, T, causal)
            ks = [k_ref[pl.ds(r0, T), :].astype(BF16) for r0, _ in blocks]
            vs = [v_ref[pl.ds(r0, T), :] for r0, _ in blocks]
            new_cs = []
            for hm, qh, c in zip(masks, qs, cs):
                for (_, valid), kb, vb in zip(blocks, ks, vs):
                    _, b, w = _sb_block(qh, kb, c, valid, strict_upper)
                    vh = (vb * hm).astype(BF16)
                    w_hi = w.astype(BF16)
                    w_lo = (w - w_hi.astype(F32)).astype(BF16)
                    acc = acc + _dg(w_hi, vh, 1, 0) + _dg(w_lo, vh, 1, 0)
                    c = c + jnp.sum(b, axis=1, keepdims=True)
                new_cs.append(c)
            return j - len(blocks), tuple(new_cs), acc

        _, _, acc = lax.while_loop(cond, step, (i, (zero,) * HEADS_PER_BLOCK, jnp.zeros((T, LANES), F32)))
        o_ref[...] = acc
        ride_done()

    (o,), rode = _ride_call(
        body, name, grid,
        [pl.BlockSpec((T, LANES), lambda p, i: (i, p)),
         pl.BlockSpec((L, LANES), lambda p, i: (0, p)),
         pl.BlockSpec((L, LANES), lambda p, i: (0, p))],
        [pl.BlockSpec((T, LANES), lambda p, i: (i, p))], [jax.ShapeDtypeStruct((L, W), F32)],
        (q, k, v), ride, ("parallel", "parallel"))
    return o, rode


def _sb_bwd_call(q, k, v, o, do, T, name, ride=None):
    L, W = q.shape
    scale = SB_HEAD_DIM ** -0.5
    grid = (W // LANES, L // T)

    def body(*refs):
        (q_ref, k_ref, v_ref, o_ref, do_ref), (dq_ref, dk_ref, dv_ref), _, riding = _split_refs(ride, 5, 3, refs)
        ride_done = _ride_ends(ride, riding, grid)
        i = pl.program_id(1)

        @pl.when(i == 0)
        def _():
            dk_ref[...] = jnp.zeros_like(dk_ref)
            dv_ref[...] = jnp.zeros_like(dv_ref)

        causal, strict_upper, upper = _sb_masks(T)
        q2 = q_ref[...] * scale
        do2 = do_ref[...]
        o2 = o_ref[...]
        masks = [_head_mask(h) for h in range(HEADS_PER_BLOCK)]
        qs = [(q2 * hm).astype(BF16) for hm in masks]
        dos = [(do2 * hm).astype(BF16) for hm in masks]
        totals = [jnp.sum(doh.astype(F32) * o2, axis=1, keepdims=True) for doh in dos]
        zero = jnp.zeros((T, 1), F32)

        def cond(state):
            j, cs, _, _ = state
            return jnp.logical_and(j >= 0, _max_all(cs) > -SB_CUT)

        def step(state):
            j, cs, rs, dq = state
            blocks = _sb_key_blocks(i, j, T, causal)
            kfs = [k_ref[pl.ds(r0, T), :] for r0, _ in blocks]
            ks = [kf.astype(BF16) for kf in kfs]
            vs = [v_ref[pl.ds(r0, T), :].astype(BF16) for r0, _ in blocks]
            dks = [jnp.zeros((T, LANES), F32) for _ in blocks]
            dvs = [jnp.zeros((T, LANES), F32) for _ in blocks]
            new_cs, new_rs = [], []
            for hm, qh, doh, total, c, r in zip(masks, qs, dos, totals, cs, rs):
                for n, ((_, valid), kf, kb, vb) in enumerate(zip(blocks, kfs, ks, vs)):
                    a, b, w = _sb_block(qh, kb, c, valid, strict_upper)
                    e = _dg(doh, vb, 1, 1) * w
                    before = total - (_hilo(e, upper) + r)
                    dz = jnp.where(valid, e * jnp.exp(b) - before * jnp.exp(a), 0.0).astype(BF16)
                    dq = dq + _dg(dz, kf * hm, 1, 0)
                    dks[n] = dks[n] + _dg(dz, qh, 0, 0)
                    dvs[n] = dvs[n] + _dg(w, doh, 0, 0)
                    c = c + jnp.sum(b, axis=1, keepdims=True)
                    r = r + jnp.sum(e, axis=1, keepdims=True)
                new_cs.append(c)
                new_rs.append(r)
            for (r0, _), dk, dv in zip(blocks, dks, dvs):
                dk_ref[pl.ds(r0, T), :] += dk
                dv_ref[pl.ds(r0, T), :] += dv
            return j - len(blocks), tuple(new_cs), tuple(new_rs), dq

        init = (i, (zero,) * HEADS_PER_BLOCK, (zero,) * HEADS_PER_BLOCK, jnp.zeros((T, LANES), F32))
        dq = lax.while_loop(cond, step, init)[3]
        dq_ref[...] = dq * scale
        ride_done()

    blk = pl.BlockSpec((T, LANES), lambda p, i: (i, p))
    full = pl.BlockSpec((L, LANES), lambda p, i: (0, p))
    return _ride_call(body, name, grid, [blk, full, full, blk, blk], [blk, full, full],
                      [jax.ShapeDtypeStruct((L, W), F32)] * 3, (q, k, v, o, do), ride, ("parallel", "arbitrary"))


def sb_attention(q, k, v, name, riders=NO_RIDERS):
    T = min(256, q.shape[0])

    @jax.custom_vjp
    def op(q, k, v, shards, slots):
        o, gathered = _sb_fwd_call(q, k, v, T, name + '_fwd', riders.gather_ride(shards))
        return o, tuple(gathered), riders.whole_slots(slots)

    def op_fwd(q, k, v, shards, slots):
        out = op(q, k, v, shards, slots)
        return out, (q, k, v, out[0], shards)

    def op_bwd(res, cts):
        q, k, v, o, shards = res
        do, _, slot_grads = cts
        grads, received = _sb_bwd_call(q, k, v, o, do, T, name + '_bwd', riders.scatter_ride(slot_grads))
        return (*grads, tuple(jnp.zeros_like(s) for s in shards), tuple(received))

    op.defvjp(op_fwd, op_bwd)
    return op(q, k, v, riders.shards, riders.slots)


def _dwconv_fwd_call(x, w, b, T, name):
    L, C = x.shape
    per = T // CONV_HALO
    lead = CONV_HALO - (CONV_WIDTH - 1)

    def body(x_ref, halo_ref, w_ref, b_ref, o_ref, buf):
        i = pl.program_id(0)
        buf[0:CONV_HALO, :] = jnp.where(i > 0, halo_ref[...], 0.0)
        buf[CONV_HALO:CONV_HALO + T, :] = x_ref[...]
        acc = jnp.zeros((T, C), F32) + b_ref[...]
        for j in range(CONV_WIDTH):
            acc = acc + w_ref[j:j + 1, :] * buf[lead + j:lead + j + T, :]
        o_ref[...] = acc

    return pl.pallas_call(
        body, name=name, grid=(L // T,),
        in_specs=[pl.BlockSpec((T, C), lambda i: (i, 0)),
                  pl.BlockSpec((CONV_HALO, C), lambda i: (jnp.maximum(i * per - 1, 0), 0)),
                  pl.BlockSpec(w.shape, lambda i: (0, 0)),
                  pl.BlockSpec(b.shape, lambda i: (0, 0))],
        out_specs=pl.BlockSpec((T, C), lambda i: (i, 0)),
        out_shape=jax.ShapeDtypeStruct((L, C), F32),
        scratch_shapes=[pltpu.VMEM((T + CONV_HALO, C), F32)],
        compiler_params=_cparams(("parallel",)),
    )(x, x, w, b)


def _dwconv_bwd_call(x, w, g, T, name):
    L, C = x.shape
    per = T // CONV_HALO
    n = L // T
    last_halo = L // CONV_HALO - 1
    lead = CONV_HALO - (CONV_WIDTH - 1)

    def body(x_ref, xh_ref, g_ref, gh_ref, w_ref, dx_ref, dw_ref, db_ref, bufx, bufg):
        i = pl.program_id(0)
        bufx[0:CONV_HALO, :] = jnp.where(i > 0, xh_ref[...], 0.0)
        bufx[CONV_HALO:CONV_HALO + T, :] = x_ref[...]
        gm = g_ref[...]
        bufg[0:T, :] = gm
        bufg[T:T + CONV_HALO, :] = jnp.where(i < n - 1, gh_ref[...], 0.0)
        acc = jnp.zeros((T, C), F32)
        for j in range(CONV_WIDTH):
            off = CONV_WIDTH - 1 - j
            acc = acc + w_ref[j:j + 1, :] * bufg[off:off + T, :]
        dx_ref[...] = acc

        @pl.when(i == 0)
        def _():
            dw_ref[...] = jnp.zeros_like(dw_ref)
            db_ref[...] = jnp.zeros_like(db_ref)

        for j in range(CONV_WIDTH):
            dw_ref[j:j + 1, :] += jnp.sum(gm * bufx[lead + j:lead + j + T, :], axis=0, keepdims=True)
        db_ref[...] += jnp.sum(gm, axis=0, keepdims=True)

    return pl.pallas_call(
        body, name=name, grid=(n,),
        in_specs=[pl.BlockSpec((T, C), lambda i: (i, 0)),
                  pl.BlockSpec((CONV_HALO, C), lambda i: (jnp.maximum(i * per - 1, 0), 0)),
                  pl.BlockSpec((T, C), lambda i: (i, 0)),
                  pl.BlockSpec((CONV_HALO, C), lambda i: (jnp.minimum((i + 1) * per, last_halo), 0)),
                  pl.BlockSpec(w.shape, lambda i: (0, 0))],
        out_specs=[pl.BlockSpec((T, C), lambda i: (i, 0)),
                   pl.BlockSpec(w.shape, lambda i: (0, 0)),
                   pl.BlockSpec((1, C), lambda i: (0, 0))],
        out_shape=[jax.ShapeDtypeStruct((L, C), F32), jax.ShapeDtypeStruct(w.shape, F32),
                   jax.ShapeDtypeStruct((1, C), F32)],
        scratch_shapes=[pltpu.VMEM((T + CONV_HALO, C), F32), pltpu.VMEM((T + CONV_HALO, C), F32)],
        compiler_params=_cparams(("arbitrary",)),
    )(x, x, g, g, w)


def dwconv(x, w, b, name):
    T = min(512, x.shape[0])

    @jax.custom_vjp
    def op(x, w, b):
        return _dwconv_fwd_call(x, w, b, T, name + '_fwd')

    def op_fwd(x, w, b):
        return _dwconv_fwd_call(x, w, b, T, name + '_fwd'), (x, w)

    def op_bwd(res, g):
        x, w = res
        return tuple(_dwconv_bwd_call(x, w, g, T, name + '_bwd'))

    op.defvjp(op_fwd, op_bwd)
    return op(x, w, b)


def _ssm_fwd_call(u, ar, ai, bbr, bbi, cr, ci, d, T, name):
    L, C = u.shape
    S = SSM_LANES

    def body(u_ref, ar_ref, ai_ref, bbr_ref, bbi_ref, cr_ref, ci_ref, d_ref,
             y_ref, xr_ref, xi_ref, st_r, st_i, in_r, in_i, out_r, out_i):
        i = pl.program_id(0)

        @pl.when(i == 0)
        def _():
            st_r[...] = jnp.zeros_like(st_r)
            st_i[...] = jnp.zeros_like(st_i)

        u_blk = u_ref[...]
        xr_ref[...] = _dg(u_blk, bbr_ref[...], 1, 0)
        xi_ref[...] = _dg(u_blk, bbi_ref[...], 1, 0)
        a_r, a_i = ar_ref[...], ai_ref[...]

        def tile(t, carry):
            sr, si = carry
            r0 = pl.multiple_of(t * SUBLANES, SUBLANES)
            in_r[...] = xr_ref[pl.ds(r0, SUBLANES), :]
            in_i[...] = xi_ref[pl.ds(r0, SUBLANES), :]
            for r in range(SUBLANES):
                nr = a_r * sr - a_i * si + in_r[r:r + 1, :]
                ni = a_r * si + a_i * sr + in_i[r:r + 1, :]
                sr, si = nr, ni
                out_r[r:r + 1, :] = sr
                out_i[r:r + 1, :] = si
            xr_ref[pl.ds(r0, SUBLANES), :] = out_r[...]
            xi_ref[pl.ds(r0, SUBLANES), :] = out_i[...]
            return sr, si

        sr, si = lax.fori_loop(0, T // SUBLANES, tile, (st_r[0:1, :], st_i[0:1, :]))
        st_r[0:1, :] = sr
        st_i[0:1, :] = si
        y_ref[...] = (_dg(xr_ref[...], cr_ref[...], 1, 0) - _dg(xi_ref[...], ci_ref[...], 1, 0)
                      + d_ref[...] * u_blk)

    full = lambda a: pl.BlockSpec(a.shape, lambda i: (0, 0))
    return pl.pallas_call(
        body, name=name, grid=(L // T,),
        in_specs=[pl.BlockSpec((T, C), lambda i: (i, 0))] + [full(a) for a in (ar, ai, bbr, bbi, cr, ci, d)],
        out_specs=[pl.BlockSpec((T, C), lambda i: (i, 0)), pl.BlockSpec((T, S), lambda i: (i, 0)),
                   pl.BlockSpec((T, S), lambda i: (i, 0))],
        out_shape=[jax.ShapeDtypeStruct((L, C), F32), jax.ShapeDtypeStruct((L, S), F32),
                   jax.ShapeDtypeStruct((L, S), F32)],
        scratch_shapes=[pltpu.VMEM((SUBLANES, S), F32)] * 6,
        compiler_params=_cparams(("arbitrary",)),
    )(u, ar, ai, bbr, bbi, cr, ci, d)


def _ssm_bwd_call(u, xr, xi, dy, ar, ai, bbr, bbi, cr, ci, d, T, name):
    L, C = u.shape
    S = SSM_LANES
    n = L // T
    per = T // SUBLANES

    def body(u_ref, xr_ref, xi_ref, hr_ref, hi_ref, dy_ref, ar_ref, ai_ref, bbr_ref, bbi_ref, cr_ref, ci_ref,
             d_ref, du_ref, dar_ref, dai_ref, dbr_ref, dbi_ref, dcr_ref, dci_ref, dd_ref,
             lam_r, lam_i, prev_r, prev_i, st_r, st_i, in_r, in_i, out_r, out_i):
        i = pl.program_id(0)
        chunk = n - 1 - i

        @pl.when(i == 0)
        def _():
            st_r[...] = jnp.zeros_like(st_r)
            st_i[...] = jnp.zeros_like(st_i)
            for ref in (dar_ref, dai_ref, dbr_ref, dbi_ref, dcr_ref, dci_ref, dd_ref):
                ref[...] = jnp.zeros_like(ref)

        dy_blk = dy_ref[...]
        u_blk = u_ref[...]
        lam_r[...] = _dg(dy_blk, cr_ref[...], 1, 1)
        lam_i[...] = -_dg(dy_blk, ci_ref[...], 1, 1)
        dcr_ref[...] += _dg(xr_ref[...], dy_blk, 0, 0)
        dci_ref[...] -= _dg(xi_ref[...], dy_blk, 0, 0)
        a_r, a_i = ar_ref[...], ai_ref[...]

        def tile(k, carry):
            lr, li = carry
            r0 = pl.multiple_of((per - 1 - k) * SUBLANES, SUBLANES)
            in_r[...] = lam_r[pl.ds(r0, SUBLANES), :]
            in_i[...] = lam_i[pl.ds(r0, SUBLANES), :]
            for r in range(SUBLANES - 1, -1, -1):
                nr = in_r[r:r + 1, :] + a_r * lr + a_i * li
                ni = in_i[r:r + 1, :] + a_r * li - a_i * lr
                lr, li = nr, ni
                out_r[r:r + 1, :] = lr
                out_i[r:r + 1, :] = li
            lam_r[pl.ds(r0, SUBLANES), :] = out_r[...]
            lam_i[pl.ds(r0, SUBLANES), :] = out_i[...]
            return lr, li

        lr, li = lax.fori_loop(0, per, tile, (st_r[0:1, :], st_i[0:1, :]))
        st_r[0:1, :] = lr
        st_i[0:1, :] = li

        l_r, l_i = lam_r[...], lam_i[...]
        du_ref[...] = _dg(l_r, bbr_ref[...], 1, 1) + _dg(l_i, bbi_ref[...], 1, 1) + d_ref[...] * dy_blk
        dbr_ref[...] += _dg(u_blk, l_r, 0, 0)
        dbi_ref[...] += _dg(u_blk, l_i, 0, 0)
        dd_ref[...] += jnp.sum(dy_blk * u_blk, axis=0, keepdims=True)

        prev_r[0:SUBLANES, :] = jnp.where(chunk > 0, hr_ref[...], 0.0)
        prev_i[0:SUBLANES, :] = jnp.where(chunk > 0, hi_ref[...], 0.0)
        prev_r[SUBLANES:SUBLANES + T, :] = xr_ref[...]
        prev_i[SUBLANES:SUBLANES + T, :] = xi_ref[...]
        p_r = prev_r[SUBLANES - 1:SUBLANES - 1 + T, :]
        p_i = prev_i[SUBLANES - 1:SUBLANES - 1 + T, :]
        dar_ref[...] += jnp.sum(l_r * p_r + l_i * p_i, axis=0, keepdims=True)
        dai_ref[...] += jnp.sum(l_i * p_r - l_r * p_i, axis=0, keepdims=True)

    rev = lambda w: pl.BlockSpec((T, w), lambda i: (n - 1 - i, 0))
    halo = pl.BlockSpec((SUBLANES, S), lambda i: (jnp.maximum((n - 1 - i) * per - 1, 0), 0))
    full = lambda a: pl.BlockSpec(a.shape, lambda i: (0, 0))
    params = (ar, ai, bbr, bbi, cr, ci, d)
    return pl.pallas_call(
        body, name=name, grid=(n,),
        in_specs=[rev(C), rev(S), rev(S), halo, halo, rev(C)] + [full(a) for a in params],
        out_specs=[rev(C)] + [full(a) for a in params],
        out_shape=[jax.ShapeDtypeStruct((L, C), F32)] + [jax.ShapeDtypeStruct(a.shape, F32) for a in params],
        scratch_shapes=[pltpu.VMEM((T, S), F32), pltpu.VMEM((T, S), F32),
                        pltpu.VMEM((T + SUBLANES, S), F32), pltpu.VMEM((T + SUBLANES, S), F32)]
        + [pltpu.VMEM((SUBLANES, S), F32)] * 6,
        compiler_params=_cparams(("arbitrary",)),
    )(u, xr, xi, xr, xi, dy, *params)


def ssm_core(u, ar, ai, bbr, bbi, cr, ci, d, name):
    T = min(256, u.shape[0])

    @jax.custom_vjp
    def op(u, ar, ai, bbr, bbi, cr, ci, d):
        return _ssm_fwd_call(u, ar, ai, bbr, bbi, cr, ci, d, T, name + '_fwd')[0]

    def op_fwd(u, ar, ai, bbr, bbi, cr, ci, d):
        y, xr, xi = _ssm_fwd_call(u, ar, ai, bbr, bbi, cr, ci, d, T, name + '_fwd')
        return y, (u, xr, xi, ar, ai, bbr, bbi, cr, ci, d)

    def op_bwd(res, dy):
        u, xr, xi, ar, ai, bbr, bbi, cr, ci, d = res
        return tuple(_ssm_bwd_call(u, xr, xi, dy, ar, ai, bbr, bbi, cr, ci, d, T, name + '_bwd'))

    op.defvjp(op_fwd, op_bwd)
    return op(u, ar, ai, bbr, bbi, cr, ci, d)


@jax.custom_vjp
def _block_diag(blocks):
    G, R, Cc = blocks.shape
    eye = jnp.eye(G, dtype=blocks.dtype)
    return (blocks[:, :, None, :] * eye[:, None, :, None]).reshape(G * R, G * Cc)


def _block_diag_fwd(blocks):
    return _block_diag(blocks), blocks.shape


def _block_diag_bwd(shape, g):
    G, R, Cc = shape
    return (jnp.stack([g[k * R:(k + 1) * R, k * Cc:(k + 1) * Cc] for k in range(G)]),)


_block_diag.defvjp(_block_diag_fwd, _block_diag_bwd)


def ssm_discretise(lam_re, lam_im, log_dt, b_re, b_im, c_re, c_im):
    dt = jnp.exp(log_dt)[:, None]
    mag = jnp.exp(lam_re * dt)
    ar, ai = mag * jnp.cos(lam_im * dt), mag * jnp.sin(lam_im * dt)
    den = lam_re * lam_re + lam_im * lam_im
    fr = ((ar - 1.0) * lam_re + ai * lam_im) / den
    fi = (ai * lam_re - (ar - 1.0) * lam_im) / den
    bbr = fr[..., None] * b_re - fi[..., None] * b_im
    bbi = fr[..., None] * b_im + fi[..., None] * b_re
    return (ar.reshape(1, SSM_LANES), ai.reshape(1, SSM_LANES),
            _block_diag(bbr.transpose(0, 2, 1)), _block_diag(bbi.transpose(0, 2, 1)),
            _block_diag(c_re.transpose(0, 2, 1)), _block_diag(c_im.transpose(0, 2, 1)))


def split_columns(p, bounds):
    @jax.custom_vjp
    def op(p):
        return tuple(p[:, lo:hi] for lo, hi in zip(bounds[:-1], bounds[1:]))

    def op_fwd(p):
        return op(p), None

    def op_bwd(_, gs):
        return (jnp.concatenate(gs, axis=1),)

    op.defvjp(op_fwd, op_bwd)
    return op(p)


DEPTH = 2
EARLY = ['w_in', 'conv_pw2_w', 'ssm_glu_w']
LATE = [n for n in MATRICES if n not in EARLY]
FFN = ['ffn_w_in', 'ffn_w_out']
GATHER_AT = {
    'start': [('w_in', 0)],
    'sb0': [(n, 0) for n in MATRICES if n != 'w_in'],
    'ffn0': [(n, 1) for n in MATRICES if n not in FFN],
    'sb1': [(n, 1) for n in FFN],
}
SCATTER_AT = {
    'sb1': [(n, 1) for n in LATE],
    'ffn0': [(n, 1) for n in EARLY],
    'sb0': [(n, 0) for n in LATE],
    'end': [(n, 0) for n in EARLY],
}


def _assemble(name, gathered):
    if name not in SLOTTED:
        return gathered
    return jnp.concatenate([gathered[j] for j in range(N_CHIPS)], axis=SHARD_AXIS[name] - 1)


def local_loss(slots, w, mats, shards, x, mem, target):
    mats = dict(mats)
    slot = {key: slots[key] for key in SCATTER_AT['end']}
    s1, s2, s3 = SB_WIDTH, 2 * SB_WIDTH, 3 * SB_WIDTH
    s4 = s3 + 2 * CONV_CH

    def riders_at(host):
        return Riders(gather=[(n, shards[(n, l)]) for n, l in GATHER_AT[host]],
                      scatter=[(n, slots[(n, l)]) for n, l in SCATTER_AT[host]])

    def take(host, gathered, handed):
        for (n, l), g in zip(GATHER_AT[host], gathered):
            mats[(n, l)] = _assemble(n, g)
        for key, s in zip(SCATTER_AT[host], handed):
            slot[key] = s

    def linear_(x, n, l, name, residual=None, host=None):
        y, gathered, handed = linear(x, mats[(n, l)], slot[(n, l)], name, residual,
                                     riders_at(host) if host else NO_RIDERS)
        if host:
            take(host, gathered, handed)
        return y

    def fused_(f, rows, params, n, l, name, residual=None, host=None, carry=(), block_rows=None):
        y, carried, gathered, handed = rowwise_linear(
            f, rows, params, [], mats[(n, l)], slot[(n, l)], name, residual,
            riders_at(host) if host else NO_RIDERS, carry, block_rows)
        if host:
            take(host, gathered, handed)
        return (y,) + tuple(carried)

    def gain(n, l):
        return w[n][l].reshape(1, -1)

    for l in range(DEPTH):
        tag = 'l%d_' % l
        p, x = fused_(_rms, [x], [gain('norm_mix_g', l)], 'w_in', l, tag + 'w_in', carry=(0,))
        q, k, v, u_conv, u_ssm = split_columns(p, (0, s1, s2, s3, s4, p.shape[1]))
        q = groupnorm(q, w['sb_q_norm_g'][l], SB_HEAD_DIM, tag + 'q_norm')
        k = groupnorm(k, w['sb_k_norm_g'][l], SB_HEAD_DIM, tag + 'k_norm')
        o_sb, gathered, handed = sb_attention(q, k, v, tag + 'sb', riders_at('sb%d' % l))
        take('sb%d' % l, gathered, handed)

        dw_w = jnp.pad(w['conv_dw_w'][l], ((0, CONV_HALO - CONV_WIDTH), (0, 0)))
        hc = dwconv(glu(u_conv, tag + 'conv_glu'), dw_w, w['conv_dw_b'][l].reshape(1, -1), tag + 'dwconv')
        o_conv, = fused_(ln_silu_block, [hc], [gain('conv_ln_g', l), gain('conv_ln_b', l)], 'conv_pw2_w', l,
                         tag + 'pw2')

        ar, ai, bbr, bbi, cr, ci = ssm_discretise(
            w['ssm_lam_re'][l], w['ssm_lam_im'][l], w['ssm_log_dt'][l], w['ssm_b_re'][l], w['ssm_b_im'][l],
            w['ssm_c_re'][l], w['ssm_c_im'][l])
        y = ssm_core(u_ssm, ar, ai, bbr, bbi, cr, ci, w['ssm_d'][l].reshape(1, -1), tag + 'ssm')
        o_ssm = glu(linear_(y, 'ssm_glu_w', l, tag + 'ssm_glu_w'), tag + 'ssm_glu')

        x, = fused_(branch_norms_block, [o_sb, o_conv, o_ssm], [gain('branch_norm_g', l)], 'w_out', l,
                    tag + 'w_out', residual=x)

        q_raw, x = fused_(_rms, [x], [gain('norm_xa_g', l)], 'xa_wq', l, tag + 'xa_wq', carry=(0,))
        hm = rmsnorm(mem, w['norm_mem_g'][l], tag + 'norm_mem')
        qx = groupnorm(q_raw, w['xa_q_norm_g'][l], XA_HEAD_DIM, tag + 'xa_qn')
        kx = groupnorm(linear_(hm, 'xa_wk', l, tag + 'xa_wk'), w['xa_k_norm_g'][l], XA_HEAD_DIM, tag + 'xa_kn')
        vx = linear_(hm, 'xa_wv', l, tag + 'xa_wv')
        x, = fused_(xa_core_block, [qx], [kx, vx], 'xa_wo', l, tag + 'xa_wo', residual=x,
                    block_rows=min(256, qx.shape[0]))

        host = 'ffn0' if l == 0 else None
        x, gathered, handed = ffn(x, gain('norm_ffn_g', l), mats[('ffn_w_in', l)], slot[('ffn_w_in', l)],
                                  mats[('ffn_w_out', l)], slot[('ffn_w_out', l)], tag + 'ffn',
                                  riders_at(host) if host else NO_RIDERS)
        if host:
            take(host, gathered, handed)
    return jnp.sum(loss_rows(x, target, 'loss'))


def local_step(w, mats, shards, x, mem, target):
    def shard_shape(key):
        return shards[key].shape if key in shards else _shard_shape(key[0], mats[key].shape)

    slots = {}
    for host, keys in SCATTER_AT.items():
        for key in keys:
            shape = _whole_shape(key[0], shard_shape(key)) if host == 'end' else (N_CHIPS,) + shard_shape(key)
            slots[key] = jnp.zeros(shape, BF16)
    loss, (g_mats, g_w, gx) = jax.value_and_grad(local_loss, argnums=(0, 1, 4))(
        slots, w, mats, shards, x, mem, target)
    return loss, gx, g_w, g_mats


PACK_ROWS = 2048


PIECE_ROWS = 16


def _piece_rows(size):
    rows = -(-size // LANES)
    return rows, -(-rows // PIECE_ROWS) * PIECE_ROWS


def pack(arrays, dtype):
    parts, total = [], 0
    for a in arrays:
        rows, padded = _piece_rows(a.size)
        a = a.astype(dtype)
        if a.size % LANES:
            a = jnp.pad(a.reshape(-1), (0, rows * LANES - a.size))
        a = a.reshape(rows, LANES)
        if padded != rows:
            a = jnp.pad(a, ((0, padded - rows), (0, 0)))
        parts.append(a)
        total += padded
    tail = -total % PACK_ROWS
    if tail:
        parts.append(jnp.zeros((tail, LANES), dtype))
    return jnp.concatenate(parts, axis=0)


def unpack(packed, shapes):
    out, off = [], 0
    for s in shapes:
        size = math.prod(s)
        rows, padded = _piece_rows(size)
        piece = packed[off:off + rows]
        if size % LANES:
            piece = piece.reshape(-1)[:size]
        out.append(piece.reshape(s))
        off += padded
    return out


def _mesh_pos():
    return lax.axis_index("x"), lax.axis_index("y"), lax.axis_index("c")


def _exchange_xy(n_arrays, src_of, dst_of, sems, wait):
    send_sems, recv_sems, local_sems = sems
    x, y, c = _mesh_pos()
    me = 2 * x + y
    peers = [(1 - x, y), (x, 1 - y), (1 - x, 1 - y)]
    for k in range(n_arrays):
        own = pltpu.make_async_copy(src_of(k, me), dst_of(k, me), local_sems.at[k])
        if wait:
            own.wait()
        else:
            own.start()
        for p, (px, py) in enumerate(peers):
            out = pltpu.make_async_remote_copy(
                src_ref=src_of(k, 2 * px + py), dst_ref=dst_of(k, me), send_sem=send_sems.at[3 * k + p],
                recv_sem=recv_sems.at[3 * k + p], device_id=(px, py, c), device_id_type=MESH)
            if wait:
                pltpu.make_async_remote_copy(
                    src_ref=src_of(k, me), dst_ref=dst_of(k, 2 * px + py), send_sem=send_sems.at[3 * k + p],
                    recv_sem=recv_sems.at[3 * k + p], device_id=(px, py, c), device_id_type=MESH).wait_recv()
                out.wait_send()
            else:
                out.start()


class Ride:
    def __init__(self, arrays, out_shapes, src_of, dst_of):
        self.arrays, self.out_shapes = list(arrays), list(out_shapes)
        self._src_of, self._dst_of = src_of, dst_of
        n = len(self.arrays)
        self.in_specs = [pl.BlockSpec(memory_space=pl.ANY)] * n
        self.out_specs = [pl.BlockSpec(memory_space=pl.ANY)] * len(self.out_shapes)
        self.out_shape = [jax.ShapeDtypeStruct(s, BF16) for s in self.out_shapes]
        self.scratch = [pltpu.SemaphoreType.DMA((3 * n,)), pltpu.SemaphoreType.DMA((3 * n,)),
                        pltpu.SemaphoreType.DMA((n,))]

    def run(self, parts, wait):
        ins, outs, sems = parts
        _exchange_xy(len(self.arrays), lambda k, chip: self._src_of(ins, k, chip),
                     lambda k, chip: self._dst_of(outs, k, chip), sems, wait)

    def at_ends(self, parts, first, last):
        pl.when(first)(lambda: self.run(parts, False))

        def finish():
            pl.when(last)(lambda: self.run(parts, True))
        return finish


def _split_refs(ride, n_in, n_out, refs):
    if ride is None:
        return refs[:n_in], refs[n_in:n_in + n_out], refs[n_in + n_out:], None
    ni, no = len(ride.arrays), len(ride.out_shapes)
    b = n_in + ni
    c = b + n_out
    d = c + no
    return refs[:n_in], refs[b:c], refs[d:len(refs) - 3], (refs[n_in:b], refs[c:d], refs[len(refs) - 3:])


def run_ride(ride, name):
    def body(*refs):
        parts = _split_refs(ride, 0, 0, refs)[3]
        ride.run(parts, False)
        ride.run(parts, True)

    return pl.pallas_call(
        body, name=name, in_specs=ride.in_specs, out_specs=ride.out_specs, out_shape=ride.out_shape,
        scratch_shapes=ride.scratch, compiler_params=pltpu.CompilerParams(has_side_effects=True),
    )(*ride.arrays)


SLOTTED = ('w_in', 'taps')


def _part(ref, axis, chip, size):
    start = pl.multiple_of(chip * size, size)
    index = [slice(None)] * len(ref.shape)
    index[axis] = pl.ds(start, size)
    return ref.at[tuple(index)]


def _whole_shape(name, shard_shape):
    s = list(shard_shape)
    s[SHARD_AXIS[name] - 1] *= N_CHIPS
    return tuple(s)


def _shard_shape(name, whole_shape):
    s = list(whole_shape)
    s[SHARD_AXIS[name] - 1] //= N_CHIPS
    return tuple(s)


def gather_ride(names, shards):
    def out_shape(k):
        return (N_CHIPS,) + shards[k].shape if names[k] in SLOTTED else _whole_shape(names[k], shards[k].shape)

    def dst_of(outs, k, chip):
        if names[k] in SLOTTED:
            return outs[k].at[chip]
        axis = SHARD_AXIS[names[k]] - 1
        return _part(outs[k], axis, chip, shards[k].shape[axis])

    return Ride(shards, [out_shape(k) for k in range(len(shards))], lambda ins, k, chip: ins[k], dst_of)


def scatter_ride(names, grads):
    def shard_shape(k):
        return grads[k].shape[1:] if names[k] in SLOTTED else _shard_shape(names[k], grads[k].shape)

    def src_of(ins, k, chip):
        if names[k] in SLOTTED:
            return ins[k].at[chip]
        axis = SHARD_AXIS[names[k]] - 1
        return _part(ins[k], axis, chip, shard_shape(k)[axis])

    return Ride(grads, [(N_CHIPS,) + shard_shape(k) for k in range(len(grads))], src_of,
                lambda outs, k, chip: outs[k].at[chip])


def _cut_for_chips(name, g):
    if name not in SLOTTED:
        return g
    axis = SHARD_AXIS[name] - 1
    cut = g.shape[:axis] + (N_CHIPS, g.shape[axis] // N_CHIPS) + g.shape[axis + 1:]
    return jnp.moveaxis(g.reshape(cut), axis, 0)


def swap_cores(arrays, name):
    na = len(arrays)

    def body(*refs):
        ins, outs, send_sems, recv_sems = refs[:na], refs[na:2 * na], refs[2 * na], refs[2 * na + 1]
        x, y, c = _mesh_pos()
        copies = [pltpu.make_async_remote_copy(
            src_ref=ins[k], dst_ref=outs[k], send_sem=send_sems.at[k], recv_sem=recv_sems.at[k],
            device_id=(x, y, 1 - c), device_id_type=MESH) for k in range(na)]
        for cp in copies:
            cp.start()
        for cp in copies:
            cp.wait()

    return pl.pallas_call(
        body, name=name,
        in_specs=[pl.BlockSpec(memory_space=pl.ANY)] * na,
        out_specs=[pl.BlockSpec(memory_space=pl.ANY)] * na,
        out_shape=[jax.ShapeDtypeStruct(a.shape, a.dtype) for a in arrays],
        scratch_shapes=[pltpu.SemaphoreType.DMA((na,)), pltpu.SemaphoreType.DMA((na,))],
        compiler_params=pltpu.CompilerParams(has_side_effects=True),
    )(*arrays)


def allreduce_small(buf, name):
    R = buf.shape[0]
    half = R // 2

    def body(in_ref, sum_ref, other_ref, chip_ref, got_ref, send_sems, recv_sems):
        x, y, c = _mesh_pos()
        sibling = (x, y, 1 - c)
        peers = [(1 - x, y, c), (x, 1 - y, c), (1 - x, 1 - y, c)]

        def copy(k, src, dst, to):
            return pltpu.make_async_remote_copy(src_ref=src, dst_ref=dst, send_sem=send_sems.at[k],
                                                recv_sem=recv_sems.at[k], device_id=to, device_id_type=MESH)

        swap = copy(0, in_ref, other_ref, sibling)
        swap.start()
        swap.wait()
        chip_ref[...] = in_ref[...] + other_ref[...]

        mine = chip_ref.at[pl.ds(pl.multiple_of(c * half, SUBLANES), half), :]
        sends = [copy(1 + k, mine, got_ref.at[k], to) for k, to in enumerate(peers)]
        for cp in sends:
            cp.start()
        for cp in sends:
            cp.wait()
        total = (mine[...] + got_ref[0]) + (got_ref[1] + got_ref[2])

        done = sum_ref.at[pl.ds(pl.multiple_of(c * half, SUBLANES), half), :]
        done[...] = total
        back = copy(4, done, done, sibling)
        back.start()
        back.wait()

    vmem = pl.BlockSpec(memory_space=pltpu.VMEM)
    return pl.pallas_call(
        body, name=name, in_specs=[vmem], out_specs=vmem,
        out_shape=jax.ShapeDtypeStruct((R, LANES), F32),
        scratch_shapes=[pltpu.VMEM((R, LANES), F32), pltpu.VMEM((R, LANES), F32),
                        pltpu.VMEM((3, half, LANES), F32),
                        pltpu.SemaphoreType.DMA((5,)), pltpu.SemaphoreType.DMA((5,))],
        compiler_params=pltpu.CompilerParams(has_side_effects=True, vmem_limit_bytes=VMEM_LIMIT),
    )(buf)


def _adamw_update(g, w, m, v):
    m2 = ADAM_B1 * m + (1.0 - ADAM_B1) * g
    v2 = ADAM_B2 * v + (1.0 - ADAM_B2) * (g * g)
    m_hat = m2 / (1.0 - ADAM_B1 ** ADAM_STEP)
    v_hat = v2 / (1.0 - ADAM_B2 ** ADAM_STEP)
    return -ADAM_LR * (m_hat / (jnp.sqrt(v_hat) + ADAM_EPS) + ADAM_WD * w), m2, v2


def adamw_matrix(layer, mine, other, w, m, v, so_far, name):
    _, rows, cols = w.shape
    T = 16
    while rows % (2 * T) == 0 and 2 * T * cols <= 128 * 1024:
        T *= 2

    def body(mine_ref, other_ref, w_ref, m_ref, v_ref, *rest):
        g_out, d_out, m_out, v_out = rest[-4:]

        def total(ref):
            acc = ref[0].astype(F32)
            for k in range(1, N_CHIPS):
                acc = acc + ref[k].astype(F32)
            return acc

        g = total(mine_ref) + total(other_ref)
        g_out[...] = g
        d_out[...], m_out[...], v_out[...] = _adamw_update(g, w_ref[...], m_ref[...], v_ref[...])

    slots = pl.BlockSpec((N_CHIPS, T, cols), lambda i: (0, i, 0))
    spec = pl.BlockSpec((None, T, cols), lambda i: (layer, i, 0))
    filled = [] if so_far is None else list(so_far)
    return pl.pallas_call(
        body, name=name, grid=(rows // T,),
        in_specs=[slots, slots, spec, spec, spec] + [pl.BlockSpec(memory_space=pl.ANY)] * len(filled),
        out_specs=[spec] * 4,
        out_shape=[jax.ShapeDtypeStruct(w.shape, F32)] * 4,
        input_output_aliases={5 + j: j for j in range(len(filled))},
        compiler_params=_cparams(("parallel",)),
    )(mine, other, w, m, v, *filled)


def adamw_small(gs, ws, ms, vs, name):
    n = len(gs)

    def body(*refs):
        for k in range(n):
            g, w, m, v = (refs[j * n + k][...] for j in range(4))
            d_out, m_out, v_out = (refs[(4 + j) * n + k] for j in range(3))
            d_out[...], m_out[...], v_out[...] = _adamw_update(g, w, m, v)

    vmem = pl.BlockSpec(memory_space=pltpu.VMEM)
    outs = pl.pallas_call(
        body, name=name,
        in_specs=[vmem] * (4 * n), out_specs=[vmem] * (3 * n),
        out_shape=[jax.ShapeDtypeStruct(w.shape, F32) for w in ws] * 3,
        compiler_params=_cparams(),
    )(*gs, *ws, *ms, *vs)
    return outs[:n], outs[n:2 * n], outs[2 * n:]


def _shard_of(full, axis, chip):
    size = full.shape[axis] // N_CHIPS
    return lax.slice_in_dim(full, chip * size, (chip + 1) * size, axis=axis)


def kernel(x, mem, norm_mix_g, w_in, sb_q_norm_g, sb_k_norm_g, conv_dw_w, conv_dw_b, conv_ln_g, conv_ln_b, conv_pw2_w, ssm_lam_re, ssm_lam_im, ssm_log_dt, ssm_b_re, ssm_b_im, ssm_c_re, ssm_c_im, ssm_d, ssm_glu_w, branch_norm_g, w_out, norm_xa_g, norm_mem_g, xa_wq, xa_wk, xa_wv, xa_q_norm_g, xa_k_norm_g, xa_wo, norm_ffn_g, ffn_w_in, ffn_w_out, loss_target, m_norm_mix_g, m_w_in, m_sb_q_norm_g, m_sb_k_norm_g, m_conv_dw_w, m_conv_dw_b, m_conv_ln_g, m_conv_ln_b, m_conv_pw2_w, m_ssm_lam_re, m_ssm_lam_im, m_ssm_log_dt, m_ssm_b_re, m_ssm_b_im, m_ssm_c_re, m_ssm_c_im, m_ssm_d, m_ssm_glu_w, m_branch_norm_g, m_w_out, m_norm_xa_g, m_norm_mem_g, m_xa_wq, m_xa_wk, m_xa_wv, m_xa_q_norm_g, m_xa_k_norm_g, m_xa_wo, m_norm_ffn_g, m_ffn_w_in, m_ffn_w_out, v_norm_mix_g, v_w_in, v_sb_q_norm_g, v_sb_k_norm_g, v_conv_dw_w, v_conv_dw_b, v_conv_ln_g, v_conv_ln_b, v_conv_pw2_w, v_ssm_lam_re, v_ssm_lam_im, v_ssm_log_dt, v_ssm_b_re, v_ssm_b_im, v_ssm_c_re, v_ssm_c_im, v_ssm_d, v_ssm_glu_w, v_branch_norm_g, v_w_out, v_norm_xa_g, v_norm_mem_g, v_xa_wq, v_xa_wk, v_xa_wv, v_xa_q_norm_g, v_xa_k_norm_g, v_xa_wo, v_norm_ffn_g, v_ffn_w_in, v_ffn_w_out):
    given = dict(locals())
    w = {n: given[n] for n in WEIGHTS}
    m = {n: given['m_' + n] for n in WEIGHTS}
    v = {n: given['v_' + n] for n in WEIGHTS}

    depth = w_in.shape[0]
    chip = 2 * lax.axis_index("x") + lax.axis_index("y")

    assert depth == DEPTH
    taps_bits = lax.bitcast_convert_type(conv_dw_w, BF16)
    shards = {(n, l): w[n][l].astype(BF16) for n in MATRICES for l in range(depth)}
    first = [shards.pop(key) for key in GATHER_AT['start']]
    gathered = run_ride(gather_ride([n for n, _ in GATHER_AT['start']] + ['taps'], first + [taps_bits]),
                        'gather_first')
    mats = {key: _assemble(key[0], g) for key, g in zip(GATHER_AT['start'], gathered)}
    taps = jnp.concatenate([lax.bitcast_convert_type(gathered[-1][j], F32) for j in range(N_CHIPS)], axis=2)
    local_w = {n: w[n] for n in REPLICATED}
    local_w['conv_dw_w'] = taps

    loss, gx, g_w, mine = local_step(local_w, mats, shards, x[0], mem[0], loss_target[0])
    last = SCATTER_AT['end']
    received = run_ride(scatter_ride([n for n, _ in last], [_cut_for_chips(n, mine[(n, l)]) for n, l in last]),
                        'scatter_last')
    mine.update(zip(last, received))

    keys = [(n, l) for n in MATRICES for l in range(depth)]
    other = dict(zip(keys, swap_cores([mine[key] for key in keys], 'swap_cores')))
    outs = {}
    for n in MATRICES:
        res = None
        for l in range(depth):
            res = adamw_matrix(l, mine[(n, l)], other[(n, l)], w[n], m[n], v[n], res, 'adamw_%s_%d' % (n, l))
        outs['grad_' + n], outs['delta_' + n], outs['new_m_' + n], outs['new_v_' + n] = res

    reduced = allreduce_small(pack([g_w[n] for n in REPLICATED] + [g_w['conv_dw_w'], loss.reshape(1)], F32),
                              'allreduce_small')
    reduced = unpack(reduced, [w[n].shape for n in REPLICATED] + [taps.shape, (1,)])
    total_loss = reduced[-1].reshape(())
    tap_cols = conv_dw_w.shape[2]
    reduced[-2] = lax.dynamic_slice_in_dim(reduced[-2], chip * tap_cols, tap_cols, axis=2)
    small_names = REPLICATED + ['conv_dw_w']
    deltas, new_ms, new_vs = adamw_small(reduced[:-1], [w[n] for n in small_names], [m[n] for n in small_names],
                                         [v[n] for n in small_names], 'adamw_small')
    for k, n in enumerate(small_names):
        outs['grad_' + n], outs['delta_' + n] = reduced[k], deltas[k]
        outs['new_m_' + n], outs['new_v_' + n] = new_ms[k], new_vs[k]
    return (total_loss, gx[None], *[outs['grad_' + n] for n in WEIGHTS], *[outs['delta_' + n] for n in WEIGHTS],
            *[outs['new_m_' + n] for n in WEIGHTS], *[outs['new_v_' + n] for n in WEIGHTS])
```

```python
import functools
import math

import jax
import jax.numpy as jnp
from jax import lax
from jax.experimental import pallas as pl
from jax.experimental.pallas import tpu as pltpu

F32 = jnp.float32
BF16 = jnp.bfloat16
MESH = pl.DeviceIdType.MESH
HIGHEST = lax.Precision.HIGHEST

EPS = 1e-6
LANES = 128
SUBLANES = 8
VMEM_LIMIT = 56 * 1024 * 1024

SB_HEAD_DIM = 64
SB_WIDTH = 512
CONV_CH = 256
CONV_WIDTH = 31
CONV_HALO = 32
SSM_CH = 256
SSM_GROUPS = 16
SSM_GROUP = 16
SSM_STATE = 64
SSM_LANES = SSM_GROUPS * SSM_STATE
XA_HEADS = 4
XA_HEAD_DIM = 256
SB_CUT = 110.0

ADAM_LR = 0.001
ADAM_B1 = 0.9
ADAM_B2 = 0.999
ADAM_EPS = 1e-08
ADAM_WD = 0.01
ADAM_STEP = 10

WEIGHTS = ['norm_mix_g', 'w_in', 'sb_q_norm_g', 'sb_k_norm_g', 'conv_dw_w', 'conv_dw_b', 'conv_ln_g',
           'conv_ln_b', 'conv_pw2_w', 'ssm_lam_re', 'ssm_lam_im', 'ssm_log_dt', 'ssm_b_re', 'ssm_b_im',
           'ssm_c_re', 'ssm_c_im', 'ssm_d', 'ssm_glu_w', 'branch_norm_g', 'w_out', 'norm_xa_g',
           'norm_mem_g', 'xa_wq', 'xa_wk', 'xa_wv', 'xa_q_norm_g', 'xa_k_norm_g', 'xa_wo', 'norm_ffn_g',
           'ffn_w_in', 'ffn_w_out']
SHARD_AXIS = {'w_in': 2, 'conv_dw_w': 2, 'conv_pw2_w': 1, 'ssm_glu_w': 2, 'w_out': 1, 'xa_wq': 1,
              'xa_wk': 1, 'xa_wv': 1, 'xa_wo': 1, 'ffn_w_in': 2, 'ffn_w_out': 1}
MATRICES = [n for n in WEIGHTS if n in SHARD_AXIS and n != 'conv_dw_w']
REPLICATED = [n for n in WEIGHTS if n not in SHARD_AXIS]
N_CHIPS = 4
N_DEV = 8


def _cparams(sem=None, **kw):
    if sem is not None:
        kw['dimension_semantics'] = sem
    return pltpu.CompilerParams(vmem_limit_bytes=VMEM_LIMIT, **kw)


def _pick(n, target):
    best = None
    d = LANES
    while d <= min(n, target):
        if n % d == 0:
            best = d
        d += LANES
    return best if best is not None else n


def _rows_for(n_rows, width):
    t = 512
    while t > 8 and t * width > 768 * 1024:
        t //= 2
    return min(t, n_rows)


def _dg(a, b, ca, cb):
    return lax.dot_general(a.astype(BF16), b.astype(BF16), (((ca,), (cb,)), ((), ())),
                           preferred_element_type=F32)


@jax.custom_vjp
def bdot_nn(a, b):
    return _dg(a, b, 1, 0)


def _bdot_nn_fwd(a, b):
    return _dg(a, b, 1, 0), (a, b)


def _bdot_nn_bwd(res, g):
    a, b = res
    return _dg(g, b, 1, 1), _dg(a, g, 0, 0)


bdot_nn.defvjp(_bdot_nn_fwd, _bdot_nn_bwd)


@jax.custom_vjp
def bdot_nt(a, b):
    return _dg(a, b, 1, 1)


def _bdot_nt_fwd(a, b):
    return _dg(a, b, 1, 1), (a, b)


def _bdot_nt_bwd(res, g):
    a, b = res
    return _dg(g, b, 1, 0), _dg(g, a, 0, 0)


bdot_nt.defvjp(_bdot_nt_fwd, _bdot_nt_bwd)


def mm(a, b, mode, name, out_dtype=F32, add=None, ride=None):
    if mode == 'nn':
        M, K = a.shape
        N = b.shape[1]
    elif mode == 'nt':
        M, K = a.shape
        N = b.shape[0]
    else:
        K, M = a.shape
        N = b.shape[1]
    if mode == 'tn':
        tm, tn, tk = _pick(M, 1536), _pick(N, 2816), _pick(K, 512)
    else:
        tm, tn = _pick(M, 1024), _pick(N, 1536)
        tk = K if K <= 2816 else _pick(K, 1536)
    nk = K // tk
    ca, cb = {'nn': (1, 0), 'nt': (1, 1), 'tn': (0, 0)}[mode]

    grid = (M // tm, N // tn, nk)

    def body(*refs):
        ins, (o_ref,), scratch, riding = _split_refs(ride, 3 if add is not None else 2, 1, refs)
        a_ref, b_ref = ins[:2]
        add_ref = ins[2] if add is not None else None
        i, j, k = pl.program_id(0), pl.program_id(1), pl.program_id(2)
        ride_done = None
        if riding is not None:
            first = jnp.logical_and(i == 0, jnp.logical_and(j == 0, k == 0))
            last = jnp.logical_and(i == grid[0] - 1, jnp.logical_and(j == grid[1] - 1, k == nk - 1))
            ride_done = ride.at_ends(riding, first, last)

        def finish(acc):
            if add_ref is not None:
                acc = acc + add_ref[...]
            o_ref[...] = acc.astype(out_dtype)

        if nk == 1:
            finish(_dg(a_ref[...], b_ref[...], ca, cb))
        else:
            acc_ref, = scratch

            @pl.when(k == 0)
            def _():
                acc_ref[...] = jnp.zeros_like(acc_ref)

            acc_ref[...] += _dg(a_ref[...], b_ref[...], ca, cb)

            @pl.when(k == nk - 1)
            def _():
                finish(acc_ref[...])

        if ride_done is not None:
            ride_done()

    if mode == 'nn':
        a_spec = pl.BlockSpec((tm, tk), lambda i, j, k: (i, k))
        b_spec = pl.BlockSpec((tk, tn), lambda i, j, k: (k, j))
    elif mode == 'nt':
        a_spec = pl.BlockSpec((tm, tk), lambda i, j, k: (i, k))
        b_spec = pl.BlockSpec((tn, tk), lambda i, j, k: (j, k))
    else:
        a_spec = pl.BlockSpec((tk, tm), lambda i, j, k: (k, i))
        b_spec = pl.BlockSpec((tk, tn), lambda i, j, k: (k, j))
    out_spec = pl.BlockSpec((tm, tn), lambda i, j, k: (i, j))
    own_in = [a_spec, b_spec] + ([out_spec] if add is not None else [])
    operands = (a, b) if add is None else (a, b, add)
    scratch = [pltpu.VMEM((tm, tn), F32)] if nk > 1 else []
    if ride is None:
        return pl.pallas_call(
            body, name=name, grid=grid, in_specs=own_in, out_specs=out_spec,
            out_shape=jax.ShapeDtypeStruct((M, N), out_dtype), scratch_shapes=scratch,
            compiler_params=_cparams(("parallel", "parallel", "arbitrary")),
        )(*operands)
    outs = pl.pallas_call(
        body, name=name, grid=grid, in_specs=own_in + ride.in_specs, out_specs=[out_spec] + ride.out_specs,
        out_shape=[jax.ShapeDtypeStruct((M, N), out_dtype)] + ride.out_shape,
        scratch_shapes=scratch + ride.scratch,
        compiler_params=_cparams(("arbitrary", "arbitrary", "arbitrary"), has_side_effects=True),
    )(*operands, *ride.arrays)
    return outs[0], outs[1:]


class Riders:
    def __init__(self, gather=(), scatter=()):
        self.gather_names = [n for n, _ in gather]
        self.shards = tuple(s for _, s in gather)
        self.scatter_names = [n for n, _ in scatter]
        self.slots = tuple(s for _, s in scatter)

    def gather_ride(self, shards):
        return gather_ride(self.gather_names, list(shards)) if shards else None

    def scatter_ride(self, grads):
        if not grads:
            return None
        return scatter_ride(self.scatter_names, [_cut_for_chips(n, g) for n, g in zip(self.scatter_names, grads)])

    def whole_slots(self, slots):
        return tuple(jnp.zeros(_whole_shape(n, s.shape[1:]), BF16) for n, s in zip(self.scatter_names, slots))


NO_RIDERS = Riders()


def linear(x, w, slot, name, residual=None, riders=NO_RIDERS):
    @jax.custom_vjp
    def op(x, w, slot, residual, shards, slots):
        ride = riders.gather_ride(shards)
        y = mm(x, w, 'nn', name + '_fwd', add=residual, ride=ride)
        y, gathered = y if ride is not None else (y, ())
        return y, tuple(gathered), riders.whole_slots(slots)

    def op_fwd(x, w, slot, residual, shards, slots):
        return op(x, w, slot, residual, shards, slots), (x, w, shards)

    def op_bwd(res, cts):
        x, w, shards = res
        g, _, slot_grads = cts
        ride = riders.scatter_ride(slot_grads)
        dx = mm(g, w, 'nt', name + '_dx', ride=ride)
        dx, received = dx if ride is not None else (dx, ())
        return (dx, jnp.zeros_like(w), mm(x, g, 'tn', name + '_dw', BF16), None if residual is None else g,
                tuple(jnp.zeros_like(s) for s in shards), tuple(received))

    op.defvjp(op_fwd, op_bwd)
    return op(x, w, slot, residual, riders.shards, riders.slots)


def _rowwise_calls(f, rows, params, consts, out_widths, name, need_row_grad=None, block_rows=None, carry=(),
                   out_dtype=F32):
    nr, npar, nc, nout = len(rows), len(params), len(consts), len(out_widths)
    carry = tuple(carry)
    L = rows[0].shape[0]
    widths = [r.shape[1] for r in rows]
    T = block_rows or _rows_for(L, max(widths + list(out_widths)))
    n = L // T
    need = list(need_row_grad) if need_row_grad is not None else [True] * nr
    pshapes = [p.shape for p in params]
    cshapes = [c.shape for c in consts]

    row_specs = [pl.BlockSpec((T, w), lambda i: (i, 0)) for w in widths]
    par_specs = [pl.BlockSpec(s, lambda i: (0, 0)) for s in pshapes]
    con_specs = [pl.BlockSpec(s, lambda i: (0, 0)) for s in cshapes]
    out_specs = [pl.BlockSpec((T, w), lambda i: (i, 0)) for w in out_widths]

    def fwd_call(rows, params, consts):
        def body(*refs):
            ins = [r[...] for r in refs[:nr + npar + nc]]
            outs = f(*ins)
            for o_ref, val in zip(refs[nr + npar + nc:], outs):
                o_ref[...] = val.astype(out_dtype)

        return pl.pallas_call(
            body, name=name + '_fwd', grid=(n,),
            in_specs=row_specs + par_specs + con_specs, out_specs=out_specs,
            out_shape=[jax.ShapeDtypeStruct((L, w), out_dtype) for w in out_widths],
            compiler_params=_cparams(("parallel",)),
        )(*rows, *params, *consts)

    def bwd_call(rows, params, consts, cts, carried):
        grad_rows = [k for k in range(nr) if need[k]]
        n_in = nr + npar + nc + nout

        def body(*refs):
            i = pl.program_id(0)
            rv = [r[...] for r in refs[:nr]]
            pv = [r[...] for r in refs[nr:nr + npar]]
            cv = [r[...] for r in refs[nr + npar:nr + npar + nc]]
            ctv = tuple(r[...] for r in refs[nr + npar + nc:n_in])
            carried_refs = dict(zip(carry, refs[n_in:n_in + len(carry)]))
            orefs = refs[n_in + len(carry):]
            _, vjp = jax.vjp(lambda *rp: tuple(f(*rp, *cv)), *rv, *pv)
            g = vjp(ctv)
            for slot, k in enumerate(grad_rows):
                orefs[slot][...] = g[k] + carried_refs[k][...] if k in carried_refs else g[k]

            @pl.when(i == 0)
            def _():
                for k in range(npar):
                    orefs[len(grad_rows) + k][...] = jnp.zeros(pshapes[k], F32)

            for k in range(npar):
                orefs[len(grad_rows) + k][...] += g[nr + k]

        outs = pl.pallas_call(
            body, name=name + '_bwd', grid=(n,),
            in_specs=row_specs + par_specs + con_specs + out_specs + [row_specs[k] for k in carry],
            out_specs=[row_specs[k] for k in grad_rows] + par_specs,
            out_shape=[jax.ShapeDtypeStruct((L, widths[k]), F32) for k in grad_rows]
            + [jax.ShapeDtypeStruct(s, F32) for s in pshapes],
            compiler_params=_cparams(("arbitrary",)),
        )(*rows, *params, *consts, *cts, *carried)
        drows = []
        slot = 0
        for k in range(nr):
            if need[k]:
                drows.append(outs[slot])
                slot += 1
            else:
                drows.append(jnp.zeros_like(rows[k]))
        return tuple(drows), tuple(outs[len(grad_rows):])

    return fwd_call, bwd_call


def rowwise(f, rows, params, consts, out_widths, name, need_row_grad=None, block_rows=None, carry=()):
    fwd_call, bwd_call = _rowwise_calls(f, rows, params, consts, out_widths, name, need_row_grad, block_rows, carry)
    nout = len(out_widths)

    @jax.custom_vjp
    def op(rows, params, consts):
        return tuple(fwd_call(rows, params, consts)) + tuple(rows[k] for k in carry)

    def op_fwd(rows, params, consts):
        return op(rows, params, consts), (rows, params, consts)

    def op_bwd(res, cts):
        rows, params, consts = res
        drows, dparams = bwd_call(rows, params, consts, cts[:nout], cts[nout:])
        return drows, dparams, tuple(jnp.zeros_like(c) for c in consts)

    op.defvjp(op_fwd, op_bwd)
    return op(tuple(rows), tuple(params), tuple(consts))


def rowwise_linear(f, rows, params, consts, w, slot, name, residual=None, riders=NO_RIDERS, carry=(),
                   block_rows=None):
    width = w.shape[0]
    fwd_call, bwd_call = _rowwise_calls(lambda *a: (f(*a),), rows, params, consts, [width], name, None,
                                        block_rows, carry, BF16)

    @jax.custom_vjp
    def op(rows, params, consts, w, slot, residual, shards, slots):
        h, = fwd_call(rows, params, consts)
        ride = riders.gather_ride(shards)
        y = mm(h, w, 'nn', name + '_mm', add=residual, ride=ride)
        y, gathered = y if ride is not None else (y, ())
        return y, tuple(rows[k] for k in carry), tuple(gathered), riders.whole_slots(slots)

    def op_fwd(rows, params, consts, w, slot, residual, shards, slots):
        h, = fwd_call(rows, params, consts)
        ride = riders.gather_ride(shards)
        y = mm(h, w, 'nn', name + '_mm', add=residual, ride=ride)
        y, gathered = y if ride is not None else (y, ())
        out = (y, tuple(rows[k] for k in carry), tuple(gathered), riders.whole_slots(slots))
        return out, (rows, params, consts, h, w, shards)

    def op_bwd(res, cts):
        rows, params, consts, h, w, shards = res
        g, carried, _, slot_grads = cts
        ride = riders.scatter_ride(slot_grads)
        dh = mm(g, w, 'nt', name + '_dx', ride=ride)
        dh, received = dh if ride is not None else (dh, ())
        drows, dparams = bwd_call(rows, params, consts, (dh,), carried)
        return (drows, dparams, tuple(jnp.zeros_like(c) for c in consts), jnp.zeros_like(w),
                mm(h, g, 'tn', name + '_dw', BF16), None if residual is None else g,
                tuple(jnp.zeros_like(s) for s in shards), tuple(received))

    op.defvjp(op_fwd, op_bwd)
    return op(tuple(rows), tuple(params), tuple(consts), w, slot, residual, riders.shards, riders.slots)


def _rms(x, g):
    return x * lax.rsqrt(jnp.mean(x * x, axis=-1, keepdims=True) + EPS) * g


def rmsnorm(x, g, name, carry=False):
    out = rowwise(lambda x, g: (_rms(x, g),), [x], [g.reshape(1, -1)], [], [x.shape[1]], name,
                  carry=(0,) if carry else ())
    return out if carry else out[0]


def _split2(x):
    hi = x.astype(BF16)
    return hi, (x - hi.astype(F32)).astype(BF16)


@jax.custom_vjp
def select_mm(x, sel):
    return sum(_dg(t, sel, 1, 0) for t in _split2(x))


def _select_mm_fwd(x, sel):
    return select_mm(x, sel), sel


def _select_mm_bwd(sel, g):
    return sum(_dg(t, sel, 1, 1) for t in _split2(g)), jnp.zeros_like(sel)


select_mm.defvjp(_select_mm_fwd, _select_mm_bwd)


def groupnorm(x, g, group, name):
    width = x.shape[1]
    g_full = jnp.tile(g.reshape(1, group), (1, width // group))
    if group % LANES == 0:
        def f(x, g_full):
            outs = []
            for lo in range(0, width, group):
                xs = x[:, lo:lo + group]
                outs.append(_rms(xs, g_full[:, lo:lo + group]))
            return (jnp.concatenate(outs, axis=-1),)

        return rowwise(f, [x], [g_full], [], [width], name)[0]

    gid = jnp.arange(width) // group
    sel = (gid[:, None] == jnp.arange(LANES)[None, :]).astype(BF16)

    def f(x, g_full, sel, sel_t):
        ms = select_mm(x * x, sel) * (1.0 / group)
        inv = select_mm(lax.rsqrt(ms + EPS), sel_t)
        return (x * inv * g_full,)

    return rowwise(f, [x], [g_full], [sel, sel.T], [width], name)[0]


def glu(x, name):
    half = x.shape[1] // 2

    def f(x):
        return (x[:, :half] * jax.nn.sigmoid(x[:, half:]),)

    return rowwise(f, [x], [], [], [half], name)[0]


def swiglu_block(x):
    half = x.shape[1] // 2
    gate = x[:, :half]
    return gate * jax.nn.sigmoid(gate) * x[:, half:]


def ln_silu_block(x, g, b):
    mu = jnp.mean(x, axis=-1, keepdims=True)
    xc = x - mu
    var = jnp.mean(xc * xc, axis=-1, keepdims=True)
    y = xc * lax.rsqrt(var + EPS) * g + b
    return y * jax.nn.sigmoid(y)


def branch_norms_block(a, b, c, g):
    w1, w2 = a.shape[1], b.shape[1]
    return jnp.concatenate([_rms(a, g[:, :w1]), _rms(b, g[:, w1:w1 + w2]), _rms(c, g[:, w1 + w2:])], axis=-1)


def xa_core_block(q, k, v):
    scale = XA_HEAD_DIM ** -0.5
    outs = []
    for h in range(XA_HEADS):
        sl = slice(h * XA_HEAD_DIM, (h + 1) * XA_HEAD_DIM)
        s = bdot_nt(q[:, sl], k[:, sl]) * scale
        m = lax.stop_gradient(jnp.max(s, axis=-1, keepdims=True))
        e = jnp.exp(s - m)
        p = e / jnp.sum(e, axis=-1, keepdims=True)
        outs.append(bdot_nn(p, v[:, sl]))
    return jnp.concatenate(outs, axis=-1)


def loss_rows(y, target, name):
    def f(y, t):
        d = y - t
        return (0.5 * jnp.mean(d * d, axis=-1, keepdims=True),)

    return rowwise(f, [y, target], [], [], [1], name, need_row_grad=[True, False])[0]


def _swiglu(gate, up):
    return gate * jax.nn.sigmoid(gate) * up


def mm_swiglu(h, w, name, ride=None):
    M, K = h.shape
    H = w.shape[1] // 2
    tm, tn = _pick(M, 1024), _pick(H, 1536)
    nj = H // tn
    grid = (M // tm, 2 * nj)

    def body(*refs):
        (a_ref, b_ref), (gu_ref, act_ref), (kept,), riding = _split_refs(ride, 2, 2, refs)
        ride_done = _ride_ends(ride, riding, grid)
        j = pl.program_id(1)
        prod = _dg(a_ref[...], b_ref[...], 1, 0)
        gu_ref[...] = prod

        @pl.when(j < nj)
        def _():
            kept[j] = prod

        @pl.when(j >= nj)
        def _():
            act_ref[...] = _swiglu(kept[j - nj], prod).astype(BF16)

        ride_done()

    return _ride_call(
        body, name, grid,
        [pl.BlockSpec((tm, K), lambda i, j: (i, 0)), pl.BlockSpec((K, tn), lambda i, j: (0, j))],
        [pl.BlockSpec((tm, tn), lambda i, j: (i, j)),
         pl.BlockSpec((tm, tn), lambda i, j: (i, jnp.maximum(j - nj, 0)))],
        [jax.ShapeDtypeStruct((M, 2 * H), F32), jax.ShapeDtypeStruct((M, H), BF16)],
        (h, w), ride, ("parallel", "arbitrary"), scratch=[pltpu.VMEM((nj, tm, tn), F32)])


def mm_swiglu_bwd(dy, w_out, gu, name):
    M, D = dy.shape
    H = gu.shape[1] // 2
    tm, tn = _pick(M, 512), _pick(H, 1536)
    nj = H // tn

    def body(dy_ref, w_ref, gate_ref, up_ref, o_ref, kept):
        j = pl.program_id(1)

        @pl.when(j < nj)
        def _():
            dact = _dg(dy_ref[...], w_ref[...], 1, 1)
            _, vjp = jax.vjp(_swiglu, gate_ref[...], up_ref[...])
            dgate, dup = vjp(dact)
            o_ref[...] = dgate.astype(BF16)
            kept[j] = dup.astype(BF16)

        @pl.when(j >= nj)
        def _():
            o_ref[...] = kept[j - nj]

    def tile(j):
        return jnp.minimum(j, nj - 1)

    return pl.pallas_call(
        body, name=name, grid=(M // tm, 2 * nj),
        in_specs=[pl.BlockSpec((tm, D), lambda i, j: (i, 0)),
                  pl.BlockSpec((tn, D), lambda i, j: (tile(j), 0)),
                  pl.BlockSpec((tm, tn), lambda i, j: (i, tile(j))),
                  pl.BlockSpec((tm, tn), lambda i, j: (i, nj + tile(j)))],
        out_specs=pl.BlockSpec((tm, tn), lambda i, j: (i, j)),
        out_shape=jax.ShapeDtypeStruct((M, 2 * H), BF16),
        scratch_shapes=[pltpu.VMEM((nj, tm, tn), BF16)],
        compiler_params=_cparams(("parallel", "arbitrary")),
    )(dy, w_out, gu, gu)


def ffn(x, gain, w_in, slot_in, w_out, slot_out, name, riders=NO_RIDERS):
    norm_fwd, norm_bwd = _rowwise_calls(lambda x, g: (_rms(x, g),), [x], [gain], [], [x.shape[1]],
                                        name + '_norm', None, None, (0,), BF16)

    def forward(x, gain, w_in, w_out, shards):
        h, = norm_fwd((x,), (gain,), ())
        (gu, act), gathered = mm_swiglu(h, w_in, name + '_in', riders.gather_ride(shards))
        return mm(act, w_out, 'nn', name + '_out', add=x), tuple(gathered), (h, gu, act)

    @jax.custom_vjp
    def op(x, gain, w_in, slot_in, w_out, slot_out, shards, slots):
        y, gathered, _ = forward(x, gain, w_in, w_out, shards)
        return y, gathered, riders.whole_slots(slots)

    def op_fwd(x, gain, w_in, slot_in, w_out, slot_out, shards, slots):
        y, gathered, (h, gu, act) = forward(x, gain, w_in, w_out, shards)
        return (y, gathered, riders.whole_slots(slots)), (x, gain, h, gu, act, w_in, w_out, shards)

    def op_bwd(res, cts):
        x, gain, h, gu, act, w_in, w_out, shards = res
        g, _, slot_grads = cts
        dgu = mm_swiglu_bwd(g, w_out, gu, name + '_dact')
        dw_out = mm(act, g, 'tn', name + '_out_dw', BF16)
        dw_in = mm(h, dgu, 'tn', name + '_in_dw', BF16)
        ride = riders.scatter_ride(slot_grads)
        dh = mm(dgu, w_in, 'nt', name + '_in_dx', ride=ride)
        dh, received = dh if ride is not None else (dh, ())
        (dx,), (dgain,) = norm_bwd((x,), (gain,), (), (dh,), (g,))
        return (dx, dgain, jnp.zeros_like(w_in), dw_in, jnp.zeros_like(w_out), dw_out,
                tuple(jnp.zeros_like(s) for s in shards), tuple(received))

    op.defvjp(op_fwd, op_bwd)
    return op(x, gain, w_in, slot_in, w_out, slot_out, riders.shards, riders.slots)


def _hilo(x, ones_bf16):
    hi = x.astype(BF16)
    lo = (x - hi.astype(F32)).astype(BF16)
    return _dg(hi, ones_bf16, 1, 0) + _dg(lo, ones_bf16, 1, 0)


def _sb_block(qh, kb, c, valid, strict_upper):
    z = _dg(qh, kb, 1, 1)
    a = jnp.minimum(z, 0.0) - jnp.log(1.0 + jnp.exp(-jnp.abs(z)))
    b = jnp.where(valid, a - z, 0.0)
    s = _hilo(b, strict_upper) + c
    w = jnp.where(valid, jnp.exp(a + s), 0.0)
    return a, b, w


def _sb_masks(T):
    row = lax.broadcasted_iota(jnp.int32, (T, T), 0)
    col = lax.broadcasted_iota(jnp.int32, (T, T), 1)
    return col < row, (row > col).astype(BF16), (row >= col).astype(BF16)


def _sb_key_blocks(i, j, T, causal):
    second = jnp.maximum(j - 1, 0)
    return [(pl.multiple_of(j * T, T), jnp.logical_or(causal, j != i)),
            (pl.multiple_of(second * T, T), jnp.logical_and(jnp.logical_or(causal, True), j >= 1))]


HEADS_PER_BLOCK = LANES // SB_HEAD_DIM


def _head_mask(h):
    lane = lax.broadcasted_iota(jnp.int32, (1, LANES), 1)
    return (lane // SB_HEAD_DIM == h).astype(F32)


def _max_all(columns):
    m = columns[0]
    for c in columns[1:]:
        m = jnp.maximum(m, c)
    return jnp.max(m)


def _ride_call(body, name, grid, in_specs, out_specs, out_shape, operands, ride, semantics, scratch=(),
               aliases=None):
    if ride is None:
        outs = pl.pallas_call(body, name=name, grid=grid, in_specs=in_specs, out_specs=out_specs,
                              out_shape=out_shape, scratch_shapes=list(scratch),
                              input_output_aliases=aliases or {},
                              compiler_params=_cparams(semantics))(*operands)
        return outs, ()
    outs = pl.pallas_call(
        body, name=name, grid=grid, in_specs=in_specs + ride.in_specs, out_specs=out_specs + ride.out_specs,
        out_shape=out_shape + ride.out_shape, scratch_shapes=list(scratch) + ride.scratch,
        input_output_aliases=aliases or {},
        compiler_params=_cparams(("arbitrary",) * len(grid), has_side_effects=True),
    )(*operands, *ride.arrays)
    return outs[:len(out_shape)], outs[len(out_shape):]


def _ride_ends(ride, riding, grid):
    if riding is None:
        return lambda: None
    first, last = None, None
    for axis, size in enumerate(grid):
        at0, at1 = pl.program_id(axis) == 0, pl.program_id(axis) == size - 1
        first = at0 if first is None else jnp.logical_and(first, at0)
        last = at1 if last is None else jnp.logical_and(last, at1)
    return ride.at_ends(riding, first, last)


def _sb_fwd_call(q, k, v, T, name, ride=None):
    L, W = q.shape
    scale = SB_HEAD_DIM ** -0.5
    grid = (W // LANES, L // T)

    def body(*refs):
        (q_ref, k_ref, v_ref), (o_ref,), _, riding = _split_refs(ride, 3, 1, refs)
        ride_done = _ride_ends(ride, riding, grid)
        i = pl.program_id(1)
        causal, strict_upper, _ = _sb_masks(T)
        q2 = q_ref[...] * scale
        masks = [_head_mask(h) for h in range(HEADS_PER_BLOCK)]
        qs = [(q2 * hm).astype(BF16) for hm in masks]
        zero = jnp.zeros((T, 1), F32)

        def cond(state):
            j, cs, _ = state
            return jnp.logical_and(j >= 0, _max_all(cs) > -SB_CUT)

        def step(state):
            j, cs, acc = state
            blocks = _sb_key_blocks(i, j, T, causal)
            ks = [k_ref[pl.ds(r0, T), :].astype(BF16) for r0, _ in blocks]
            vs = [v_ref[pl.ds(r0, T), :] for r0, _ in blocks]
            new_cs = []
            for hm, qh, c in zip(masks, qs, cs):
                for (_, valid), kb, vb in zip(blocks, ks, vs):
                    _, b, w = _sb_block(qh, kb, c, valid, strict_upper)
                    vh = (vb * hm).astype(BF16)
                    w_hi = w.astype(BF16)
                    w_lo = (w - w_hi.astype(F32)).astype(BF16)
                    acc = acc + _dg(w_hi, vh, 1, 0) + _dg(w_lo, vh, 1, 0)
                    c = c + jnp.sum(b, axis=1, keepdims=True)
                new_cs.append(c)
            return j - len(blocks), tuple(new_cs), acc

        _, _, acc = lax.while_loop(cond, step, (i, (zero,) * HEADS_PER_BLOCK, jnp.zeros((T, LANES), F32)))
        o_ref[...] = acc
        ride_done()

    (o,), rode = _ride_call(
        body, name, grid,
        [pl.BlockSpec((T, LANES), lambda p, i: (i, p)),
         pl.BlockSpec((L, LANES), lambda p, i: (0, p)),
         pl.BlockSpec((L, LANES), lambda p, i: (0, p))],
        [pl.BlockSpec((T, LANES), lambda p, i: (i, p))], [jax.ShapeDtypeStruct((L, W), F32)],
        (q, k, v), ride, ("parallel", "parallel"))
    return o, rode


def _sb_bwd_call(q, k, v, o, do, T, name, ride=None):
    L, W = q.shape
    scale = SB_HEAD_DIM ** -0.5
    grid = (W // LANES, L // T)

    def body(*refs):
        (q_ref, k_ref, v_ref, o_ref, do_ref), (dq_ref, dk_ref, dv_ref), _, riding = _split_refs(ride, 5, 3, refs)
        ride_done = _ride_ends(ride, riding, grid)
        i = pl.program_id(1)

        @pl.when(i == 0)
        def _():
            dk_ref[...] = jnp.zeros_like(dk_ref)
            dv_ref[...] = jnp.zeros_like(dv_ref)

        causal, strict_upper, upper = _sb_masks(T)
        q2 = q_ref[...] * scale
        do2 = do_ref[...]
        o2 = o_ref[...]
        masks = [_head_mask(h) for h in range(HEADS_PER_BLOCK)]
        qs = [(q2 * hm).astype(BF16) for hm in masks]
        dos = [(do2 * hm).astype(BF16) for hm in masks]
        totals = [jnp.sum(doh.astype(F32) * o2, axis=1, keepdims=True) for doh in dos]
        zero = jnp.zeros((T, 1), F32)

        def cond(state):
            j, cs, _, _ = state
            return jnp.logical_and(j >= 0, _max_all(cs) > -SB_CUT)

        def step(state):
            j, cs, rs, dq = state
            blocks = _sb_key_blocks(i, j, T, causal)
            kfs = [k_ref[pl.ds(r0, T), :] for r0, _ in blocks]
            ks = [kf.astype(BF16) for kf in kfs]
            vs = [v_ref[pl.ds(r0, T), :].astype(BF16) for r0, _ in blocks]
            dks = [jnp.zeros((T, LANES), F32) for _ in blocks]
            dvs = [jnp.zeros((T, LANES), F32) for _ in blocks]
            new_cs, new_rs = [], []
            for hm, qh, doh, total, c, r in zip(masks, qs, dos, totals, cs, rs):
                for n, ((_, valid), kf, kb, vb) in enumerate(zip(blocks, kfs, ks, vs)):
                    a, b, w = _sb_block(qh, kb, c, valid, strict_upper)
                    e = _dg(doh, vb, 1, 1) * w
                    before = total - (_hilo(e, upper) + r)
                    dz = jnp.where(valid, e * jnp.exp(b) - before * jnp.exp(a), 0.0).astype(BF16)
                    dq = dq + _dg(dz, kf * hm, 1, 0)
                    dks[n] = dks[n] + _dg(dz, qh, 0, 0)
                    dvs[n] = dvs[n] + _dg(w, doh, 0, 0)
                    c = c + jnp.sum(b, axis=1, keepdims=True)
                    r = r + jnp.sum(e, axis=1, keepdims=True)
                new_cs.append(c)
                new_rs.append(r)
            for (r0, _), dk, dv in zip(blocks, dks, dvs):
                dk_ref[pl.ds(r0, T), :] += dk
                dv_ref[pl.ds(r0, T), :] += dv
            return j - len(blocks), tuple(new_cs), tuple(new_rs), dq

        init = (i, (zero,) * HEADS_PER_BLOCK, (zero,) * HEADS_PER_BLOCK, jnp.zeros((T, LANES), F32))
        dq = lax.while_loop(cond, step, init)[3]
        dq_ref[...] = dq * scale
        ride_done()

    blk = pl.BlockSpec((T, LANES), lambda p, i: (i, p))
    full = pl.BlockSpec((L, LANES), lambda p, i: (0, p))
    return _ride_call(body, name, grid, [blk, full, full, blk, blk], [blk, full, full],
                      [jax.ShapeDtypeStruct((L, W), F32)] * 3, (q, k, v, o, do), ride, ("parallel", "arbitrary"))


def sb_attention(q, k, v, name, riders=NO_RIDERS):
    T = min(256, q.shape[0])

    @jax.custom_vjp
    def op(q, k, v, shards, slots):
        o, gathered = _sb_fwd_call(q, k, v, T, name + '_fwd', riders.gather_ride(shards))
        return o, tuple(gathered), riders.whole_slots(slots)

    def op_fwd(q, k, v, shards, slots):
        out = op(q, k, v, shards, slots)
        return out, (q, k, v, out[0], shards)

    def op_bwd(res, cts):
        q, k, v, o, shards = res
        do, _, slot_grads = cts
        grads, received = _sb_bwd_call(q, k, v, o, do, T, name + '_bwd', riders.scatter_ride(slot_grads))
        return (*grads, tuple(jnp.zeros_like(s) for s in shards), tuple(received))

    op.defvjp(op_fwd, op_bwd)
    return op(q, k, v, riders.shards, riders.slots)


def _dwconv_fwd_call(x, w, b, T, name):
    L, C = x.shape
    per = T // CONV_HALO
    lead = CONV_HALO - (CONV_WIDTH - 1)

    def body(x_ref, halo_ref, w_ref, b_ref, o_ref, buf):
        i = pl.program_id(0)
        buf[0:CONV_HALO, :] = jnp.where(i > 0, halo_ref[...], 0.0)
        buf[CONV_HALO:CONV_HALO + T, :] = x_ref[...]
        acc = jnp.zeros((T, C), F32) + b_ref[...]
        for j in range(CONV_WIDTH):
            acc = acc + w_ref[j:j + 1, :] * buf[lead + j:lead + j + T, :]
        o_ref[...] = acc

    return pl.pallas_call(
        body, name=name, grid=(L // T,),
        in_specs=[pl.BlockSpec((T, C), lambda i: (i, 0)),
                  pl.BlockSpec((CONV_HALO, C), lambda i: (jnp.maximum(i * per - 1, 0), 0)),
                  pl.BlockSpec(w.shape, lambda i: (0, 0)),
                  pl.BlockSpec(b.shape, lambda i: (0, 0))],
        out_specs=pl.BlockSpec((T, C), lambda i: (i, 0)),
        out_shape=jax.ShapeDtypeStruct((L, C), F32),
        scratch_shapes=[pltpu.VMEM((T + CONV_HALO, C), F32)],
        compiler_params=_cparams(("parallel",)),
    )(x, x, w, b)


def _dwconv_bwd_call(x, w, g, T, name):
    L, C = x.shape
    per = T // CONV_HALO
    n = L // T
    last_halo = L // CONV_HALO - 1
    lead = CONV_HALO - (CONV_WIDTH - 1)

    def body(x_ref, xh_ref, g_ref, gh_ref, w_ref, dx_ref, dw_ref, db_ref, bufx, bufg):
        i = pl.program_id(0)
        bufx[0:CONV_HALO, :] = jnp.where(i > 0, xh_ref[...], 0.0)
        bufx[CONV_HALO:CONV_HALO + T, :] = x_ref[...]
        gm = g_ref[...]
        bufg[0:T, :] = gm
        bufg[T:T + CONV_HALO, :] = jnp.where(i < n - 1, gh_ref[...], 0.0)
        acc = jnp.zeros((T, C), F32)
        for j in range(CONV_WIDTH):
            off = CONV_WIDTH - 1 - j
            acc = acc + w_ref[j:j + 1, :] * bufg[off:off + T, :]
        dx_ref[...] = acc

        @pl.when(i == 0)
        def _():
            dw_ref[...] = jnp.zeros_like(dw_ref)
            db_ref[...] = jnp.zeros_like(db_ref)

        for j in range(CONV_WIDTH):
            dw_ref[j:j + 1, :] += jnp.sum(gm * bufx[lead + j:lead + j + T, :], axis=0, keepdims=True)
        db_ref[...] += jnp.sum(gm, axis=0, keepdims=True)

    return pl.pallas_call(
        body, name=name, grid=(n,),
        in_specs=[pl.BlockSpec((T, C), lambda i: (i, 0)),
                  pl.BlockSpec((CONV_HALO, C), lambda i: (jnp.maximum(i * per - 1, 0), 0)),
                  pl.BlockSpec((T, C), lambda i: (i, 0)),
                  pl.BlockSpec((CONV_HALO, C), lambda i: (jnp.minimum((i + 1) * per, last_halo), 0)),
                  pl.BlockSpec(w.shape, lambda i: (0, 0))],
        out_specs=[pl.BlockSpec((T, C), lambda i: (i, 0)),
                   pl.BlockSpec(w.shape, lambda i: (0, 0)),
                   pl.BlockSpec((1, C), lambda i: (0, 0))],
        out_shape=[jax.ShapeDtypeStruct((L, C), F32), jax.ShapeDtypeStruct(w.shape, F32),
                   jax.ShapeDtypeStruct((1, C), F32)],
        scratch_shapes=[pltpu.VMEM((T + CONV_HALO, C), F32), pltpu.VMEM((T + CONV_HALO, C), F32)],
        compiler_params=_cparams(("arbitrary",)),
    )(x, x, g, g, w)


def dwconv(x, w, b, name):
    T = min(512, x.shape[0])

    @jax.custom_vjp
    def op(x, w, b):
        return _dwconv_fwd_call(x, w, b, T, name + '_fwd')

    def op_fwd(x, w, b):
        return _dwconv_fwd_call(x, w, b, T, name + '_fwd'), (x, w)

    def op_bwd(res, g):
        x, w = res
        return tuple(_dwconv_bwd_call(x, w, g, T, name + '_bwd'))

    op.defvjp(op_fwd, op_bwd)
    return op(x, w, b)


def _ssm_fwd_call(u, ar, ai, bbr, bbi, cr, ci, d, T, name):
    L, C = u.shape
    S = SSM_LANES

    def body(u_ref, ar_ref, ai_ref, bbr_ref, bbi_ref, cr_ref, ci_ref, d_ref,
             y_ref, xr_ref, xi_ref, st_r, st_i, in_r, in_i, out_r, out_i):
        i = pl.program_id(0)

        @pl.when(i == 0)
        def _():
            st_r[...] = jnp.zeros_like(st_r)
            st_i[...] = jnp.zeros_like(st_i)

        u_blk = u_ref[...]
        xr_ref[...] = _dg(u_blk, bbr_ref[...], 1, 0)
        xi_ref[...] = _dg(u_blk, bbi_ref[...], 1, 0)
        a_r, a_i = ar_ref[...], ai_ref[...]

        def tile(t, carry):
            sr, si = carry
            r0 = pl.multiple_of(t * SUBLANES, SUBLANES)
            in_r[...] = xr_ref[pl.ds(r0, SUBLANES), :]
            in_i[...] = xi_ref[pl.ds(r0, SUBLANES), :]
            for r in range(SUBLANES):
                nr = a_r * sr - a_i * si + in_r[r:r + 1, :]
                ni = a_r * si + a_i * sr + in_i[r:r + 1, :]
                sr, si = nr, ni
                out_r[r:r + 1, :] = sr
                out_i[r:r + 1, :] = si
            xr_ref[pl.ds(r0, SUBLANES), :] = out_r[...]
            xi_ref[pl.ds(r0, SUBLANES), :] = out_i[...]
            return sr, si

        sr, si = lax.fori_loop(0, T // SUBLANES, tile, (st_r[0:1, :], st_i[0:1, :]))
        st_r[0:1, :] = sr
        st_i[0:1, :] = si
        y_ref[...] = (_dg(xr_ref[...], cr_ref[...], 1, 0) - _dg(xi_ref[...], ci_ref[...], 1, 0)
                      + d_ref[...] * u_blk)

    full = lambda a: pl.BlockSpec(a.shape, lambda i: (0, 0))
    return pl.pallas_call(
        body, name=name, grid=(L // T,),
        in_specs=[pl.BlockSpec((T, C), lambda i: (i, 0))] + [full(a) for a in (ar, ai, bbr, bbi, cr, ci, d)],
        out_specs=[pl.BlockSpec((T, C), lambda i: (i, 0)), pl.BlockSpec((T, S), lambda i: (i, 0)),
                   pl.BlockSpec((T, S), lambda i: (i, 0))],
        out_shape=[jax.ShapeDtypeStruct((L, C), F32), jax.ShapeDtypeStruct((L, S), F32),
                   jax.ShapeDtypeStruct((L, S), F32)],
        scratch_shapes=[pltpu.VMEM((SUBLANES, S), F32)] * 6,
        compiler_params=_cparams(("arbitrary",)),
    )(u, ar, ai, bbr, bbi, cr, ci, d)


def _ssm_bwd_call(u, xr, xi, dy, ar, ai, bbr, bbi, cr, ci, d, T, name):
    L, C = u.shape
    S = SSM_LANES
    n = L // T
    per = T // SUBLANES

    def body(u_ref, xr_ref, xi_ref, hr_ref, hi_ref, dy_ref, ar_ref, ai_ref, bbr_ref, bbi_ref, cr_ref, ci_ref,
             d_ref, du_ref, dar_ref, dai_ref, dbr_ref, dbi_ref, dcr_ref, dci_ref, dd_ref,
             lam_r, lam_i, prev_r, prev_i, st_r, st_i, in_r, in_i, out_r, out_i):
        i = pl.program_id(0)
        chunk = n - 1 - i

        @pl.when(i == 0)
        def _():
            st_r[...] = jnp.zeros_like(st_r)
            st_i[...] = jnp.zeros_like(st_i)
            for ref in (dar_ref, dai_ref, dbr_ref, dbi_ref, dcr_ref, dci_ref, dd_ref):
                ref[...] = jnp.zeros_like(ref)

        dy_blk = dy_ref[...]
        u_blk = u_ref[...]
        lam_r[...] = _dg(dy_blk, cr_ref[...], 1, 1)
        lam_i[...] = -_dg(dy_blk, ci_ref[...], 1, 1)
        dcr_ref[...] += _dg(xr_ref[...], dy_blk, 0, 0)
        dci_ref[...] -= _dg(xi_ref[...], dy_blk, 0, 0)
        a_r, a_i = ar_ref[...], ai_ref[...]

        def tile(k, carry):
            lr, li = carry
            r0 = pl.multiple_of((per - 1 - k) * SUBLANES, SUBLANES)
            in_r[...] = lam_r[pl.ds(r0, SUBLANES), :]
            in_i[...] = lam_i[pl.ds(r0, SUBLANES), :]
            for r in range(SUBLANES - 1, -1, -1):
                nr = in_r[r:r + 1, :] + a_r * lr + a_i * li
                ni = in_i[r:r + 1, :] + a_r * li - a_i * lr
                lr, li = nr, ni
                out_r[r:r + 1, :] = lr
                out_i[r:r + 1, :] = li
            lam_r[pl.ds(r0, SUBLANES), :] = out_r[...]
            lam_i[pl.ds(r0, SUBLANES), :] = out_i[...]
            return lr, li

        lr, li = lax.fori_loop(0, per, tile, (st_r[0:1, :], st_i[0:1, :]))
        st_r[0:1, :] = lr
        st_i[0:1, :] = li

        l_r, l_i = lam_r[...], lam_i[...]
        du_ref[...] = _dg(l_r, bbr_ref[...], 1, 1) + _dg(l_i, bbi_ref[...], 1, 1) + d_ref[...] * dy_blk
        dbr_ref[...] += _dg(u_blk, l_r, 0, 0)
        dbi_ref[...] += _dg(u_blk, l_i, 0, 0)
        dd_ref[...] += jnp.sum(dy_blk * u_blk, axis=0, keepdims=True)

        prev_r[0:SUBLANES, :] = jnp.where(chunk > 0, hr_ref[...], 0.0)
        prev_i[0:SUBLANES, :] = jnp.where(chunk > 0, hi_ref[...], 0.0)
        prev_r[SUBLANES:SUBLANES + T, :] = xr_ref[...]
        prev_i[SUBLANES:SUBLANES + T, :] = xi_ref[...]
        p_r = prev_r[SUBLANES - 1:SUBLANES - 1 + T, :]
        p_i = prev_i[SUBLANES - 1:SUBLANES - 1 + T, :]
        dar_ref[...] += jnp.sum(l_r * p_r + l_i * p_i, axis=0, keepdims=True)
        dai_ref[...] += jnp.sum(l_i * p_r - l_r * p_i, axis=0, keepdims=True)

    rev = lambda w: pl.BlockSpec((T, w), lambda i: (n - 1 - i, 0))
    halo = pl.BlockSpec((SUBLANES, S), lambda i: (jnp.maximum((n - 1 - i) * per - 1, 0), 0))
    full = lambda a: pl.BlockSpec(a.shape, lambda i: (0, 0))
    params = (ar, ai, bbr, bbi, cr, ci, d)
    return pl.pallas_call(
        body, name=name, grid=(n,),
        in_specs=[rev(C), rev(S), rev(S), halo, halo, rev(C)] + [full(a) for a in params],
        out_specs=[rev(C)] + [full(a) for a in params],
        out_shape=[jax.ShapeDtypeStruct((L, C), F32)] + [jax.ShapeDtypeStruct(a.shape, F32) for a in params],
        scratch_shapes=[pltpu.VMEM((T, S), F32), pltpu.VMEM((T, S), F32),
                        pltpu.VMEM((T + SUBLANES, S), F32), pltpu.VMEM((T + SUBLANES, S), F32)]
        + [pltpu.VMEM((SUBLANES, S), F32)] * 6,
        compiler_params=_cparams(("arbitrary",)),
    )(u, xr, xi, xr, xi, dy, *params)


def ssm_core(u, ar, ai, bbr, bbi, cr, ci, d, name):
    T = min(256, u.shape[0])

    @jax.custom_vjp
    def op(u, ar, ai, bbr, bbi, cr, ci, d):
        return _ssm_fwd_call(u, ar, ai, bbr, bbi, cr, ci, d, T, name + '_fwd')[0]

    def op_fwd(u, ar, ai, bbr, bbi, cr, ci, d):
        y, xr, xi = _ssm_fwd_call(u, ar, ai, bbr, bbi, cr, ci, d, T, name + '_fwd')
        return y, (u, xr, xi, ar, ai, bbr, bbi, cr, ci, d)

    def op_bwd(res, dy):
        u, xr, xi, ar, ai, bbr, bbi, cr, ci, d = res
        return tuple(_ssm_bwd_call(u, xr, xi, dy, ar, ai, bbr, bbi, cr, ci, d, T, name + '_bwd'))

    op.defvjp(op_fwd, op_bwd)
    return op(u, ar, ai, bbr, bbi, cr, ci, d)


@jax.custom_vjp
def _block_diag(blocks):
    G, R, Cc = blocks.shape
    eye = jnp.eye(G, dtype=blocks.dtype)
    return (blocks[:, :, None, :] * eye[:, None, :, None]).reshape(G * R, G * Cc)


def _block_diag_fwd(blocks):
    return _block_diag(blocks), blocks.shape


def _block_diag_bwd(shape, g):
    G, R, Cc = shape
    on_diagonal = jnp.eye(G, dtype=bool)[:, None, :, None]
    return (jnp.sum(jnp.where(on_diagonal, g.reshape(G, R, G, Cc), 0.0), axis=2),)


_block_diag.defvjp(_block_diag_fwd, _block_diag_bwd)


def ssm_discretise(lam_re, lam_im, log_dt, b_re, b_im, c_re, c_im):
    dt = jnp.exp(log_dt)[:, None]
    mag = jnp.exp(lam_re * dt)
    ar, ai = mag * jnp.cos(lam_im * dt), mag * jnp.sin(lam_im * dt)
    den = lam_re * lam_re + lam_im * lam_im
    fr = ((ar - 1.0) * lam_re + ai * lam_im) / den
    fi = (ai * lam_re - (ar - 1.0) * lam_im) / den
    bbr = fr[..., None] * b_re - fi[..., None] * b_im
    bbi = fr[..., None] * b_im + fi[..., None] * b_re
    return (ar.reshape(1, SSM_LANES), ai.reshape(1, SSM_LANES),
            _block_diag(bbr.transpose(0, 2, 1)), _block_diag(bbi.transpose(0, 2, 1)),
            _block_diag(c_re.transpose(0, 2, 1)), _block_diag(c_im.transpose(0, 2, 1)))


def split_columns(p, bounds):
    @jax.custom_vjp
    def op(p):
        return tuple(p[:, lo:hi] for lo, hi in zip(bounds[:-1], bounds[1:]))

    def op_fwd(p):
        return op(p), None

    def op_bwd(_, gs):
        return (jnp.concatenate(gs, axis=1),)

    op.defvjp(op_fwd, op_bwd)
    return op(p)


DEPTH = 2
EARLY = ['w_in', 'conv_pw2_w', 'ssm_glu_w']
LATE = [n for n in MATRICES if n not in EARLY]
FFN = ['ffn_w_in', 'ffn_w_out']
GATHER_AT = {
    'start': [('w_in', 0)],
    'sb0': [(n, 0) for n in MATRICES if n != 'w_in'],
    'ffn0': [(n, 1) for n in MATRICES if n not in FFN],
    'sb1': [(n, 1) for n in FFN],
}
SCATTER_AT = {
    'sb1': [(n, 1) for n in LATE],
    'ffn0': [(n, 1) for n in EARLY],
    'sb0': [(n, 0) for n in LATE],
    'end': [(n, 0) for n in EARLY],
}


def _assemble(name, gathered):
    if name not in SLOTTED:
        return gathered
    return jnp.concatenate([gathered[j] for j in range(N_CHIPS)], axis=SHARD_AXIS[name] - 1)


def local_loss(slots, w, mats, shards, x, mem, target):
    mats = dict(mats)
    slot = {key: slots[key] for key in SCATTER_AT['end']}
    s1, s2, s3 = SB_WIDTH, 2 * SB_WIDTH, 3 * SB_WIDTH
    s4 = s3 + 2 * CONV_CH

    def riders_at(host):
        return Riders(gather=[(n, shards[(n, l)]) for n, l in GATHER_AT[host]],
                      scatter=[(n, slots[(n, l)]) for n, l in SCATTER_AT[host]])

    def take(host, gathered, handed):
        for (n, l), g in zip(GATHER_AT[host], gathered):
            mats[(n, l)] = _assemble(n, g)
        for key, s in zip(SCATTER_AT[host], handed):
            slot[key] = s

    def linear_(x, n, l, name, residual=None, host=None):
        y, gathered, handed = linear(x, mats[(n, l)], slot[(n, l)], name, residual,
                                     riders_at(host) if host else NO_RIDERS)
        if host:
            take(host, gathered, handed)
        return y

    def fused_(f, rows, params, n, l, name, residual=None, host=None, carry=(), block_rows=None):
        y, carried, gathered, handed = rowwise_linear(
            f, rows, params, [], mats[(n, l)], slot[(n, l)], name, residual,
            riders_at(host) if host else NO_RIDERS, carry, block_rows)
        if host:
            take(host, gathered, handed)
        return (y,) + tuple(carried)

    def gain(n, l):
        return w[n][l].reshape(1, -1)

    for l in range(DEPTH):
        tag = 'l%d_' % l
        p, x = fused_(_rms, [x], [gain('norm_mix_g', l)], 'w_in', l, tag + 'w_in', carry=(0,))
        q, k, v, u_conv, u_ssm = split_columns(p, (0, s1, s2, s3, s4, p.shape[1]))
        q = groupnorm(q, w['sb_q_norm_g'][l], SB_HEAD_DIM, tag + 'q_norm')
        k = groupnorm(k, w['sb_k_norm_g'][l], SB_HEAD_DIM, tag + 'k_norm')
        o_sb, gathered, handed = sb_attention(q, k, v, tag + 'sb', riders_at('sb%d' % l))
        take('sb%d' % l, gathered, handed)

        dw_w = jnp.pad(w['conv_dw_w'][l], ((0, CONV_HALO - CONV_WIDTH), (0, 0)))
        hc = dwconv(glu(u_conv, tag + 'conv_glu'), dw_w, w['conv_dw_b'][l].reshape(1, -1), tag + 'dwconv')
        o_conv, = fused_(ln_silu_block, [hc], [gain('conv_ln_g', l), gain('conv_ln_b', l)], 'conv_pw2_w', l,
                         tag + 'pw2')

        ar, ai, bbr, bbi, cr, ci = ssm_discretise(
            w['ssm_lam_re'][l], w['ssm_lam_im'][l], w['ssm_log_dt'][l], w['ssm_b_re'][l], w['ssm_b_im'][l],
            w['ssm_c_re'][l], w['ssm_c_im'][l])
        y = ssm_core(u_ssm, ar, ai, bbr, bbi, cr, ci, w['ssm_d'][l].reshape(1, -1), tag + 'ssm')
        o_ssm = glu(linear_(y, 'ssm_glu_w', l, tag + 'ssm_glu_w'), tag + 'ssm_glu')

        x, = fused_(branch_norms_block, [o_sb, o_conv, o_ssm], [gain('branch_norm_g', l)], 'w_out', l,
                    tag + 'w_out', residual=x)

        q_raw, x = fused_(_rms, [x], [gain('norm_xa_g', l)], 'xa_wq', l, tag + 'xa_wq', carry=(0,))
        hm = rmsnorm(mem, w['norm_mem_g'][l], tag + 'norm_mem')
        qx = groupnorm(q_raw, w['xa_q_norm_g'][l], XA_HEAD_DIM, tag + 'xa_qn')
        kx = groupnorm(linear_(hm, 'xa_wk', l, tag + 'xa_wk'), w['xa_k_norm_g'][l], XA_HEAD_DIM, tag + 'xa_kn')
        vx = linear_(hm, 'xa_wv', l, tag + 'xa_wv')
        x, = fused_(xa_core_block, [qx], [kx, vx], 'xa_wo', l, tag + 'xa_wo', residual=x,
                    block_rows=min(256, qx.shape[0]))

        host = 'ffn0' if l == 0 else None
        x, gathered, handed = ffn(x, gain('norm_ffn_g', l), mats[('ffn_w_in', l)], slot[('ffn_w_in', l)],
                                  mats[('ffn_w_out', l)], slot[('ffn_w_out', l)], tag + 'ffn',
                                  riders_at(host) if host else NO_RIDERS)
        if host:
            take(host, gathered, handed)
    return jnp.sum(loss_rows(x, target, 'loss'))


def local_step(w, mats, shards, x, mem, target):
    def shard_shape(key):
        return shards[key].shape if key in shards else _shard_shape(key[0], mats[key].shape)

    slots = {}
    for host, keys in SCATTER_AT.items():
        for key in keys:
            shape = _whole_shape(key[0], shard_shape(key)) if host == 'end' else (N_CHIPS,) + shard_shape(key)
            slots[key] = jnp.zeros(shape, BF16)
    loss, (g_mats, g_w, gx) = jax.value_and_grad(local_loss, argnums=(0, 1, 4))(
        slots, w, mats, shards, x, mem, target)
    return loss, gx, g_w, g_mats


PACK_ROWS = 2048


PIECE_ROWS = 16


def _piece_rows(size):
    rows = -(-size // LANES)
    return rows, -(-rows // PIECE_ROWS) * PIECE_ROWS


def pack(arrays, dtype):
    parts, total = [], 0
    for a in arrays:
        rows, padded = _piece_rows(a.size)
        a = a.astype(dtype)
        if a.size % LANES:
            a = jnp.pad(a.reshape(-1), (0, rows * LANES - a.size))
        a = a.reshape(rows, LANES)
        if padded != rows:
            a = jnp.pad(a, ((0, padded - rows), (0, 0)))
        parts.append(a)
        total += padded
    tail = -total % PACK_ROWS
    if tail:
        parts.append(jnp.zeros((tail, LANES), dtype))
    return jnp.concatenate(parts, axis=0)


def unpack(packed, shapes):
    out, off = [], 0
    for s in shapes:
        size = math.prod(s)
        rows, padded = _piece_rows(size)
        piece = packed[off:off + rows]
        if size % LANES:
            piece = piece.reshape(-1)[:size]
        out.append(piece.reshape(s))
        off += padded
    return out


def _mesh_pos():
    return lax.axis_index("x"), lax.axis_index("y"), lax.axis_index("c")


def _exchange_xy(n_arrays, src_of, dst_of, sems, wait):
    send_sems, recv_sems, local_sems = sems
    x, y, c = _mesh_pos()
    me = 2 * x + y
    peers = [(1 - x, y), (x, 1 - y), (1 - x, 1 - y)]
    for k in range(n_arrays):
        own = pltpu.make_async_copy(src_of(k, me), dst_of(k, me), local_sems.at[k])
        if wait:
            own.wait()
        else:
            own.start()
        for p, (px, py) in enumerate(peers):
            out = pltpu.make_async_remote_copy(
                src_ref=src_of(k, 2 * px + py), dst_ref=dst_of(k, me), send_sem=send_sems.at[3 * k + p],
                recv_sem=recv_sems.at[3 * k + p], device_id=(px, py, c), device_id_type=MESH)
            if wait:
                pltpu.make_async_remote_copy(
                    src_ref=src_of(k, me), dst_ref=dst_of(k, 2 * px + py), send_sem=send_sems.at[3 * k + p],
                    recv_sem=recv_sems.at[3 * k + p], device_id=(px, py, c), device_id_type=MESH).wait_recv()
                out.wait_send()
            else:
                out.start()


class Ride:
    def __init__(self, arrays, out_shapes, src_of, dst_of):
        self.arrays, self.out_shapes = list(arrays), list(out_shapes)
        self._src_of, self._dst_of = src_of, dst_of
        n = len(self.arrays)
        self.in_specs = [pl.BlockSpec(memory_space=pl.ANY)] * n
        self.out_specs = [pl.BlockSpec(memory_space=pl.ANY)] * len(self.out_shapes)
        self.out_shape = [jax.ShapeDtypeStruct(s, BF16) for s in self.out_shapes]
        self.scratch = [pltpu.SemaphoreType.DMA((3 * n,)), pltpu.SemaphoreType.DMA((3 * n,)),
                        pltpu.SemaphoreType.DMA((n,))]

    def run(self, parts, wait):
        ins, outs, sems = parts
        _exchange_xy(len(self.arrays), lambda k, chip: self._src_of(ins, k, chip),
                     lambda k, chip: self._dst_of(outs, k, chip), sems, wait)

    def at_ends(self, parts, first, last):
        pl.when(first)(lambda: self.run(parts, False))

        def finish():
            pl.when(last)(lambda: self.run(parts, True))
        return finish


def _split_refs(ride, n_in, n_out, refs):
    if ride is None:
        return refs[:n_in], refs[n_in:n_in + n_out], refs[n_in + n_out:], None
    ni, no = len(ride.arrays), len(ride.out_shapes)
    b = n_in + ni
    c = b + n_out
    d = c + no
    return refs[:n_in], refs[b:c], refs[d:len(refs) - 3], (refs[n_in:b], refs[c:d], refs[len(refs) - 3:])


def run_ride(ride, name):
    def body(*refs):
        parts = _split_refs(ride, 0, 0, refs)[3]
        ride.run(parts, False)
        ride.run(parts, True)

    return pl.pallas_call(
        body, name=name, in_specs=ride.in_specs, out_specs=ride.out_specs, out_shape=ride.out_shape,
        scratch_shapes=ride.scratch, compiler_params=pltpu.CompilerParams(has_side_effects=True),
    )(*ride.arrays)


SLOTTED = ('w_in', 'taps')


def _part(ref, axis, chip, size):
    start = pl.multiple_of(chip * size, size)
    index = [slice(None)] * len(ref.shape)
    index[axis] = pl.ds(start, size)
    return ref.at[tuple(index)]


def _whole_shape(name, shard_shape):
    s = list(shard_shape)
    s[SHARD_AXIS[name] - 1] *= N_CHIPS
    return tuple(s)


def _shard_shape(name, whole_shape):
    s = list(whole_shape)
    s[SHARD_AXIS[name] - 1] //= N_CHIPS
    return tuple(s)


def gather_ride(names, shards):
    def out_shape(k):
        return (N_CHIPS,) + shards[k].shape if names[k] in SLOTTED else _whole_shape(names[k], shards[k].shape)

    def dst_of(outs, k, chip):
        if names[k] in SLOTTED:
            return outs[k].at[chip]
        axis = SHARD_AXIS[names[k]] - 1
        return _part(outs[k], axis, chip, shards[k].shape[axis])

    return Ride(shards, [out_shape(k) for k in range(len(shards))], lambda ins, k, chip: ins[k], dst_of)


def scatter_ride(names, grads):
    def shard_shape(k):
        return grads[k].shape[1:] if names[k] in SLOTTED else _shard_shape(names[k], grads[k].shape)

    def src_of(ins, k, chip):
        if names[k] in SLOTTED:
            return ins[k].at[chip]
        axis = SHARD_AXIS[names[k]] - 1
        return _part(ins[k], axis, chip, shard_shape(k)[axis])

    return Ride(grads, [(N_CHIPS,) + shard_shape(k) for k in range(len(grads))], src_of,
                lambda outs, k, chip: outs[k].at[chip])


def _cut_for_chips(name, g):
    if name not in SLOTTED:
        return g
    axis = SHARD_AXIS[name] - 1
    cut = g.shape[:axis] + (N_CHIPS, g.shape[axis] // N_CHIPS) + g.shape[axis + 1:]
    return jnp.moveaxis(g.reshape(cut), axis, 0)


class SwapRide(Ride):
    def __init__(self, arrays):
        self.arrays = list(arrays)
        self.out_shapes = [a.shape for a in self.arrays]
        n = len(self.arrays)
        self.in_specs = [pl.BlockSpec(memory_space=pl.ANY)] * n
        self.out_specs = [pl.BlockSpec(memory_space=pl.ANY)] * n
        self.out_shape = [jax.ShapeDtypeStruct(a.shape, a.dtype) for a in self.arrays]
        self.scratch = [pltpu.SemaphoreType.DMA((n,)), pltpu.SemaphoreType.DMA((n,)), pltpu.SemaphoreType.DMA((1,))]

    def run(self, parts, wait):
        ins, outs, (send_sems, recv_sems, _) = parts
        x, y, c = _mesh_pos()
        for k in range(len(self.arrays)):
            cp = pltpu.make_async_remote_copy(
                src_ref=ins[k], dst_ref=outs[k], send_sem=send_sems.at[k], recv_sem=recv_sems.at[k],
                device_id=(x, y, 1 - c), device_id_type=MESH)
            if wait:
                cp.wait()
            else:
                cp.start()


def allreduce_small(buf, name):
    R = buf.shape[0]
    half = R // 2

    def body(in_ref, sum_ref, other_ref, chip_ref, got_ref, send_sems, recv_sems):
        x, y, c = _mesh_pos()
        sibling = (x, y, 1 - c)
        peers = [(1 - x, y, c), (x, 1 - y, c), (1 - x, 1 - y, c)]

        def copy(k, src, dst, to):
            return pltpu.make_async_remote_copy(src_ref=src, dst_ref=dst, send_sem=send_sems.at[k],
                                                recv_sem=recv_sems.at[k], device_id=to, device_id_type=MESH)

        swap = copy(0, in_ref, other_ref, sibling)
        swap.start()
        swap.wait()
        chip_ref[...] = in_ref[...] + other_ref[...]

        mine = chip_ref.at[pl.ds(pl.multiple_of(c * half, SUBLANES), half), :]
        sends = [copy(1 + k, mine, got_ref.at[k], to) for k, to in enumerate(peers)]
        for cp in sends:
            cp.start()
        for cp in sends:
            cp.wait()
        total = (mine[...] + got_ref[0]) + (got_ref[1] + got_ref[2])

        done = sum_ref.at[pl.ds(pl.multiple_of(c * half, SUBLANES), half), :]
        done[...] = total
        back = copy(4, done, done, sibling)
        back.start()
        back.wait()

    vmem = pl.BlockSpec(memory_space=pltpu.VMEM)
    return pl.pallas_call(
        body, name=name, in_specs=[vmem], out_specs=vmem,
        out_shape=jax.ShapeDtypeStruct((R, LANES), F32),
        scratch_shapes=[pltpu.VMEM((R, LANES), F32), pltpu.VMEM((R, LANES), F32),
                        pltpu.VMEM((3, half, LANES), F32),
                        pltpu.SemaphoreType.DMA((5,)), pltpu.SemaphoreType.DMA((5,))],
        compiler_params=pltpu.CompilerParams(has_side_effects=True, vmem_limit_bytes=VMEM_LIMIT),
    )(buf)


def _adamw_update(g, w, m, v):
    m2 = ADAM_B1 * m + (1.0 - ADAM_B1) * g
    v2 = ADAM_B2 * v + (1.0 - ADAM_B2) * (g * g)
    m_hat = m2 / (1.0 - ADAM_B1 ** ADAM_STEP)
    v_hat = v2 / (1.0 - ADAM_B2 ** ADAM_STEP)
    return -ADAM_LR * (m_hat / (jnp.sqrt(v_hat) + ADAM_EPS) + ADAM_WD * w), m2, v2


def adamw_matrix(layer, mine, other, w, m, v, so_far, name, ride=None):
    _, rows, cols = w.shape
    T = 16
    while rows % (2 * T) == 0 and 2 * T * cols <= 128 * 1024:
        T *= 2
    grid = (rows // T,)
    filled = [] if so_far is None else list(so_far)

    def body(*refs):
        ins, (g_out, d_out, m_out, v_out), _, riding = _split_refs(ride, 5 + len(filled), 4, refs)
        mine_ref, other_ref, w_ref, m_ref, v_ref = ins[:5]
        ride_done = _ride_ends(ride, riding, grid)

        def total(ref):
            acc = ref[0].astype(F32)
            for k in range(1, N_CHIPS):
                acc = acc + ref[k].astype(F32)
            return acc

        g = total(mine_ref) + total(other_ref)
        g_out[...] = g
        d_out[...], m_out[...], v_out[...] = _adamw_update(g, w_ref[...], m_ref[...], v_ref[...])
        ride_done()

    slots = pl.BlockSpec((N_CHIPS, T, cols), lambda i: (0, i, 0))
    spec = pl.BlockSpec((None, T, cols), lambda i: (layer, i, 0))
    return _ride_call(
        body, name, grid,
        [slots, slots, spec, spec, spec] + [pl.BlockSpec(memory_space=pl.ANY)] * len(filled),
        [spec] * 4, [jax.ShapeDtypeStruct(w.shape, F32)] * 4, (mine, other, w, m, v, *filled), ride,
        ("parallel",), aliases={5 + j: j for j in range(len(filled))})


def adamw_small(gs, ws, ms, vs, name):
    n = len(gs)

    def body(*refs):
        for k in range(n):
            g, w, m, v = (refs[j * n + k][...] for j in range(4))
            d_out, m_out, v_out = (refs[(4 + j) * n + k] for j in range(3))
            d_out[...], m_out[...], v_out[...] = _adamw_update(g, w, m, v)

    vmem = pl.BlockSpec(memory_space=pltpu.VMEM)
    outs = pl.pallas_call(
        body, name=name,
        in_specs=[vmem] * (4 * n), out_specs=[vmem] * (3 * n),
        out_shape=[jax.ShapeDtypeStruct(w.shape, F32) for w in ws] * 3,
        compiler_params=_cparams(),
    )(*gs, *ws, *ms, *vs)
    return outs[:n], outs[n:2 * n], outs[2 * n:]


def _shard_of(full, axis, chip):
    size = full.shape[axis] // N_CHIPS
    return lax.slice_in_dim(full, chip * size, (chip + 1) * size, axis=axis)


def kernel(x, mem, norm_mix_g, w_in, sb_q_norm_g, sb_k_norm_g, conv_dw_w, conv_dw_b, conv_ln_g, conv_ln_b, conv_pw2_w, ssm_lam_re, ssm_lam_im, ssm_log_dt, ssm_b_re, ssm_b_im, ssm_c_re, ssm_c_im, ssm_d, ssm_glu_w, branch_norm_g, w_out, norm_xa_g, norm_mem_g, xa_wq, xa_wk, xa_wv, xa_q_norm_g, xa_k_norm_g, xa_wo, norm_ffn_g, ffn_w_in, ffn_w_out, loss_target, m_norm_mix_g, m_w_in, m_sb_q_norm_g, m_sb_k_norm_g, m_conv_dw_w, m_conv_dw_b, m_conv_ln_g, m_conv_ln_b, m_conv_pw2_w, m_ssm_lam_re, m_ssm_lam_im, m_ssm_log_dt, m_ssm_b_re, m_ssm_b_im, m_ssm_c_re, m_ssm_c_im, m_ssm_d, m_ssm_glu_w, m_branch_norm_g, m_w_out, m_norm_xa_g, m_norm_mem_g, m_xa_wq, m_xa_wk, m_xa_wv, m_xa_q_norm_g, m_xa_k_norm_g, m_xa_wo, m_norm_ffn_g, m_ffn_w_in, m_ffn_w_out, v_norm_mix_g, v_w_in, v_sb_q_norm_g, v_sb_k_norm_g, v_conv_dw_w, v_conv_dw_b, v_conv_ln_g, v_conv_ln_b, v_conv_pw2_w, v_ssm_lam_re, v_ssm_lam_im, v_ssm_log_dt, v_ssm_b_re, v_ssm_b_im, v_ssm_c_re, v_ssm_c_im, v_ssm_d, v_ssm_glu_w, v_branch_norm_g, v_w_out, v_norm_xa_g, v_norm_mem_g, v_xa_wq, v_xa_wk, v_xa_wv, v_xa_q_norm_g, v_xa_k_norm_g, v_xa_wo, v_norm_ffn_g, v_ffn_w_in, v_ffn_w_out):
    given = dict(locals())
    w = {n: given[n] for n in WEIGHTS}
    m = {n: given['m_' + n] for n in WEIGHTS}
    v = {n: given['v_' + n] for n in WEIGHTS}

    depth = w_in.shape[0]
    chip = 2 * lax.axis_index("x") + lax.axis_index("y")

    assert depth == DEPTH
    taps_bits = lax.bitcast_convert_type(conv_dw_w, BF16)
    shards = {(n, l): w[n][l].astype(BF16) for n in MATRICES for l in range(depth)}
    first = [shards.pop(key) for key in GATHER_AT['start']]
    gathered = run_ride(gather_ride([n for n, _ in GATHER_AT['start']] + ['taps'], first + [taps_bits]),
                        'gather_first')
    mats = {key: _assemble(key[0], g) for key, g in zip(GATHER_AT['start'], gathered)}
    taps = jnp.concatenate([lax.bitcast_convert_type(gathered[-1][j], F32) for j in range(N_CHIPS)], axis=2)
    local_w = {n: w[n] for n in REPLICATED}
    local_w['conv_dw_w'] = taps

    loss, gx, g_w, mine = local_step(local_w, mats, shards, x[0], mem[0], loss_target[0])
    last = SCATTER_AT['end']
    received = run_ride(scatter_ride([n for n, _ in last], [_cut_for_chips(n, mine[(n, l)]) for n, l in last]),
                        'scatter_last')
    mine.update(zip(last, received))

    keys = sorted(((n, l) for n in MATRICES for l in range(depth)), key=lambda key: mine[key].size)
    other, = run_ride(SwapRide([mine[keys[0]]]), 'swap_first')
    outs, results = {}, {}
    for k, (n, l) in enumerate(keys):
        ride = SwapRide([mine[keys[k + 1]]]) if k + 1 < len(keys) else None
        results[n], swapped = adamw_matrix(l, mine[(n, l)], other, w[n], m[n], v[n], results.get(n),
                                           'adamw_%s_%d' % (n, l), ride)
        if ride is not None:
            other, = swapped
    for n in MATRICES:
        outs['grad_' + n], outs['delta_' + n], outs['new_m_' + n], outs['new_v_' + n] = results[n]

    reduced = allreduce_small(pack([g_w[n] for n in REPLICATED] + [g_w['conv_dw_w'], loss.reshape(1)], F32),
                              'allreduce_small')
    reduced = unpack(reduced, [w[n].shape for n in REPLICATED] + [taps.shape, (1,)])
    total_loss = reduced[-1].reshape(())
    tap_cols = conv_dw_w.shape[2]
    reduced[-2] = lax.dynamic_slice_in_dim(reduced[-2], chip * tap_cols, tap_cols, axis=2)
    small_names = REPLICATED + ['conv_dw_w']
    deltas, new_ms, new_vs = adamw_small(reduced[:-1], [w[n] for n in small_names], [m[n] for n in small_names],
                                         [v[n] for n in small_names], 'adamw_small')
    for k, n in enumerate(small_names):
        outs['grad_' + n], outs['delta_' + n] = reduced[k], deltas[k]
        outs['new_m_' + n], outs['new_v_' + n] = new_ms[k], new_vs[k]
    return (total_loss, gx[None], *[outs['grad_' + n] for n in WEIGHTS], *[outs['delta_' + n] for n in WEIGHTS],
            *[outs['new_m_' + n] for n in WEIGHTS], *[outs['new_v_' + n] for n in WEIGHTS])
```

```python
import functools
import math

import jax
import jax.numpy as jnp
from jax import lax
from jax.experimental import pallas as pl
from jax.experimental.pallas import tpu as pltpu

F32 = jnp.float32
BF16 = jnp.bfloat16
MESH = pl.DeviceIdType.MESH
HIGHEST = lax.Precision.HIGHEST

EPS = 1e-6
LANES = 128
SUBLANES = 8
VMEM_LIMIT = 56 * 1024 * 1024

SB_HEAD_DIM = 64
SB_WIDTH = 512
CONV_CH = 256
CONV_WIDTH = 31
CONV_HALO = 32
SSM_CH = 256
SSM_GROUPS = 16
SSM_GROUP = 16
SSM_STATE = 64
SSM_LANES = SSM_GROUPS * SSM_STATE
XA_HEADS = 4
XA_HEAD_DIM = 256
SB_CUT = 110.0

ADAM_LR = 0.001
ADAM_B1 = 0.9
ADAM_B2 = 0.999
ADAM_EPS = 1e-08
ADAM_WD = 0.01
ADAM_STEP = 10

WEIGHTS = ['norm_mix_g', 'w_in', 'sb_q_norm_g', 'sb_k_norm_g', 'conv_dw_w', 'conv_dw_b', 'conv_ln_g',
           'conv_ln_b', 'conv_pw2_w', 'ssm_lam_re', 'ssm_lam_im', 'ssm_log_dt', 'ssm_b_re', 'ssm_b_im',
           'ssm_c_re', 'ssm_c_im', 'ssm_d', 'ssm_glu_w', 'branch_norm_g', 'w_out', 'norm_xa_g',
           'norm_mem_g', 'xa_wq', 'xa_wk', 'xa_wv', 'xa_q_norm_g', 'xa_k_norm_g', 'xa_wo', 'norm_ffn_g',
           'ffn_w_in', 'ffn_w_out']
SHARD_AXIS = {'w_in': 2, 'conv_dw_w': 2, 'conv_pw2_w': 1, 'ssm_glu_w': 2, 'w_out': 1, 'xa_wq': 1,
              'xa_wk': 1, 'xa_wv': 1, 'xa_wo': 1, 'ffn_w_in': 2, 'ffn_w_out': 1}
MATRICES = [n for n in WEIGHTS if n in SHARD_AXIS and n != 'conv_dw_w']
REPLICATED = [n for n in WEIGHTS if n not in SHARD_AXIS]
N_CHIPS = 4
N_DEV = 8


def _cparams(sem=None, **kw):
    if sem is not None:
        kw['dimension_semantics'] = sem
    return pltpu.CompilerParams(vmem_limit_bytes=VMEM_LIMIT, **kw)


def _pick(n, target):
    best = None
    d = LANES
    while d <= min(n, target):
        if n % d == 0:
            best = d
        d += LANES
    return best if best is not None else n


def _rows_for(n_rows, width):
    t = 512
    while t > 8 and t * width > 768 * 1024:
        t //= 2
    return min(t, n_rows)


def _dg(a, b, ca, cb):
    return lax.dot_general(a.astype(BF16), b.astype(BF16), (((ca,), (cb,)), ((), ())),
                           preferred_element_type=F32)


@jax.custom_vjp
def bdot_nn(a, b):
    return _dg(a, b, 1, 0)


def _bdot_nn_fwd(a, b):
    return _dg(a, b, 1, 0), (a, b)


def _bdot_nn_bwd(res, g):
    a, b = res
    return _dg(g, b, 1, 1), _dg(a, g, 0, 0)


bdot_nn.defvjp(_bdot_nn_fwd, _bdot_nn_bwd)


@jax.custom_vjp
def bdot_nt(a, b):
    return _dg(a, b, 1, 1)


def _bdot_nt_fwd(a, b):
    return _dg(a, b, 1, 1), (a, b)


def _bdot_nt_bwd(res, g):
    a, b = res
    return _dg(g, b, 1, 0), _dg(g, a, 0, 0)


bdot_nt.defvjp(_bdot_nt_fwd, _bdot_nt_bwd)


def mm(a, b, mode, name, out_dtype=F32, add=None, ride=None):
    if mode == 'nn':
        M, K = a.shape
        N = b.shape[1]
    elif mode == 'nt':
        M, K = a.shape
        N = b.shape[0]
    else:
        K, M = a.shape
        N = b.shape[1]
    if mode == 'tn':
        tm, tn, tk = _pick(M, 1536), _pick(N, 2816), _pick(K, 512)
    else:
        tm, tn = _pick(M, 1024), _pick(N, 1536)
        tk = K if K <= 2816 else _pick(K, 1536)
    nk = K // tk
    ca, cb = {'nn': (1, 0), 'nt': (1, 1), 'tn': (0, 0)}[mode]

    grid = (M // tm, N // tn, nk)

    def body(*refs):
        ins, (o_ref,), scratch, riding = _split_refs(ride, 3 if add is not None else 2, 1, refs)
        a_ref, b_ref = ins[:2]
        add_ref = ins[2] if add is not None else None
        i, j, k = pl.program_id(0), pl.program_id(1), pl.program_id(2)
        ride_done = None
        if riding is not None:
            first = jnp.logical_and(i == 0, jnp.logical_and(j == 0, k == 0))
            last = jnp.logical_and(i == grid[0] - 1, jnp.logical_and(j == grid[1] - 1, k == nk - 1))
            ride_done = ride.at_ends(riding, first, last)

        def finish(acc):
            if add_ref is not None:
                acc = acc + add_ref[...]
            o_ref[...] = acc.astype(out_dtype)

        if nk == 1:
            finish(_dg(a_ref[...], b_ref[...], ca, cb))
        else:
            acc_ref, = scratch

            @pl.when(k == 0)
            def _():
                acc_ref[...] = jnp.zeros_like(acc_ref)

            acc_ref[...] += _dg(a_ref[...], b_ref[...], ca, cb)

            @pl.when(k == nk - 1)
            def _():
                finish(acc_ref[...])

        if ride_done is not None:
            ride_done()

    if mode == 'nn':
        a_spec = pl.BlockSpec((tm, tk), lambda i, j, k: (i, k))
        b_spec = pl.BlockSpec((tk, tn), lambda i, j, k: (k, j))
    elif mode == 'nt':
        a_spec = pl.BlockSpec((tm, tk), lambda i, j, k: (i, k))
        b_spec = pl.BlockSpec((tn, tk), lambda i, j, k: (j, k))
    else:
        a_spec = pl.BlockSpec((tk, tm), lambda i, j, k: (k, i))
        b_spec = pl.BlockSpec((tk, tn), lambda i, j, k: (k, j))
    out_spec = pl.BlockSpec((tm, tn), lambda i, j, k: (i, j))
    own_in = [a_spec, b_spec] + ([out_spec] if add is not None else [])
    operands = (a, b) if add is None else (a, b, add)
    scratch = [pltpu.VMEM((tm, tn), F32)] if nk > 1 else []
    if ride is None:
        return pl.pallas_call(
            body, name=name, grid=grid, in_specs=own_in, out_specs=out_spec,
            out_shape=jax.ShapeDtypeStruct((M, N), out_dtype), scratch_shapes=scratch,
            compiler_params=_cparams(("parallel", "parallel", "arbitrary")),
        )(*operands)
    outs = pl.pallas_call(
        body, name=name, grid=grid, in_specs=own_in + ride.in_specs, out_specs=[out_spec] + ride.out_specs,
        out_shape=[jax.ShapeDtypeStruct((M, N), out_dtype)] + ride.out_shape,
        scratch_shapes=scratch + ride.scratch,
        compiler_params=_cparams(("arbitrary", "arbitrary", "arbitrary"), has_side_effects=True),
    )(*operands, *ride.arrays)
    return outs[0], outs[1:]


class Riders:
    def __init__(self, gather=(), scatter=(), own=None):
        self.gather_names = [n for n, _ in gather]
        self.shards = tuple(s for _, s in gather)
        self.scatter_names = [n for n, _ in scatter]
        self.slots = tuple(s for _, s in scatter)
        self.own = own

    def gather_ride(self, shards):
        return gather_ride(self.gather_names, list(shards)) if shards else None

    def scatter_ride(self, grads, own_grad=None):
        names, grads = list(self.scatter_names), list(grads)
        if self.own is not None:
            names.append(self.own)
            grads.append(own_grad)
        if not grads:
            return None
        return scatter_ride(names, [_cut_for_chips(n, g) for n, g in zip(names, grads)])

    def whole_slots(self, slots):
        return tuple(jnp.zeros(_whole_shape(n, s.shape[1:]), BF16) for n, s in zip(self.scatter_names, slots))


NO_RIDERS = Riders()


def linear(x, w, slot, name, residual=None, riders=NO_RIDERS):
    @jax.custom_vjp
    def op(x, w, slot, residual, shards, slots):
        ride = riders.gather_ride(shards)
        y = mm(x, w, 'nn', name + '_fwd', add=residual, ride=ride)
        y, gathered = y if ride is not None else (y, ())
        return y, tuple(gathered), riders.whole_slots(slots)

    def op_fwd(x, w, slot, residual, shards, slots):
        return op(x, w, slot, residual, shards, slots), (x, w, shards)

    def op_bwd(res, cts):
        x, w, shards = res
        g, _, slot_grads = cts
        ride = riders.scatter_ride(slot_grads)
        dx = mm(g, w, 'nt', name + '_dx', ride=ride)
        dx, received = dx if ride is not None else (dx, ())
        return (dx, jnp.zeros_like(w), mm(x, g, 'tn', name + '_dw', BF16), None if residual is None else g,
                tuple(jnp.zeros_like(s) for s in shards), tuple(received))

    op.defvjp(op_fwd, op_bwd)
    return op(x, w, slot, residual, riders.shards, riders.slots)


def _rowwise_calls(f, rows, params, consts, out_widths, name, need_row_grad=None, block_rows=None, carry=(),
                   out_dtype=F32):
    nr, npar, nc, nout = len(rows), len(params), len(consts), len(out_widths)
    carry = tuple(carry)
    L = rows[0].shape[0]
    widths = [r.shape[1] for r in rows]
    T = block_rows or _rows_for(L, max(widths + list(out_widths)))
    n = L // T
    need = list(need_row_grad) if need_row_grad is not None else [True] * nr
    pshapes = [p.shape for p in params]
    cshapes = [c.shape for c in consts]

    row_specs = [pl.BlockSpec((T, w), lambda i: (i, 0)) for w in widths]
    par_specs = [pl.BlockSpec(s, lambda i: (0, 0)) for s in pshapes]
    con_specs = [pl.BlockSpec(s, lambda i: (0, 0)) for s in cshapes]
    out_specs = [pl.BlockSpec((T, w), lambda i: (i, 0)) for w in out_widths]

    def fwd_call(rows, params, consts):
        def body(*refs):
            ins = [r[...] for r in refs[:nr + npar + nc]]
            outs = f(*ins)
            for o_ref, val in zip(refs[nr + npar + nc:], outs):
                o_ref[...] = val.astype(out_dtype)

        return pl.pallas_call(
            body, name=name + '_fwd', grid=(n,),
            in_specs=row_specs + par_specs + con_specs, out_specs=out_specs,
            out_shape=[jax.ShapeDtypeStruct((L, w), out_dtype) for w in out_widths],
            compiler_params=_cparams(("parallel",)),
        )(*rows, *params, *consts)

    def bwd_call(rows, params, consts, cts, carried):
        grad_rows = [k for k in range(nr) if need[k]]
        n_in = nr + npar + nc + nout

        def body(*refs):
            i = pl.program_id(0)
            rv = [r[...] for r in refs[:nr]]
            pv = [r[...] for r in refs[nr:nr + npar]]
            cv = [r[...] for r in refs[nr + npar:nr + npar + nc]]
            ctv = tuple(r[...] for r in refs[nr + npar + nc:n_in])
            carried_refs = dict(zip(carry, refs[n_in:n_in + len(carry)]))
            orefs = refs[n_in + len(carry):]
            _, vjp = jax.vjp(lambda *rp: tuple(f(*rp, *cv)), *rv, *pv)
            g = vjp(ctv)
            for slot, k in enumerate(grad_rows):
                orefs[slot][...] = g[k] + carried_refs[k][...] if k in carried_refs else g[k]

            @pl.when(i == 0)
            def _():
                for k in range(npar):
                    orefs[len(grad_rows) + k][...] = jnp.zeros(pshapes[k], F32)

            for k in range(npar):
                orefs[len(grad_rows) + k][...] += g[nr + k]

        outs = pl.pallas_call(
            body, name=name + '_bwd', grid=(n,),
            in_specs=row_specs + par_specs + con_specs + out_specs + [row_specs[k] for k in carry],
            out_specs=[row_specs[k] for k in grad_rows] + par_specs,
            out_shape=[jax.ShapeDtypeStruct((L, widths[k]), F32) for k in grad_rows]
            + [jax.ShapeDtypeStruct(s, F32) for s in pshapes],
            compiler_params=_cparams(("arbitrary",)),
        )(*rows, *params, *consts, *cts, *carried)
        drows = []
        slot = 0
        for k in range(nr):
            if need[k]:
                drows.append(outs[slot])
                slot += 1
            else:
                drows.append(jnp.zeros_like(rows[k]))
        return tuple(drows), tuple(outs[len(grad_rows):])

    return fwd_call, bwd_call


def rowwise(f, rows, params, consts, out_widths, name, need_row_grad=None, block_rows=None, carry=()):
    fwd_call, bwd_call = _rowwise_calls(f, rows, params, consts, out_widths, name, need_row_grad, block_rows, carry)
    nout = len(out_widths)

    @jax.custom_vjp
    def op(rows, params, consts):
        return tuple(fwd_call(rows, params, consts)) + tuple(rows[k] for k in carry)

    def op_fwd(rows, params, consts):
        return op(rows, params, consts), (rows, params, consts)

    def op_bwd(res, cts):
        rows, params, consts = res
        drows, dparams = bwd_call(rows, params, consts, cts[:nout], cts[nout:])
        return drows, dparams, tuple(jnp.zeros_like(c) for c in consts)

    op.defvjp(op_fwd, op_bwd)
    return op(tuple(rows), tuple(params), tuple(consts))


def rowwise_linear(f, rows, params, consts, w, slot, name, residual=None, riders=NO_RIDERS, carry=(),
                   block_rows=None):
    width = w.shape[0]
    fwd_call, bwd_call = _rowwise_calls(lambda *a: (f(*a),), rows, params, consts, [width], name, None,
                                        block_rows, carry, BF16)

    @jax.custom_vjp
    def op(rows, params, consts, w, slot, residual, shards, slots):
        h, = fwd_call(rows, params, consts)
        ride = riders.gather_ride(shards)
        y = mm(h, w, 'nn', name + '_mm', add=residual, ride=ride)
        y, gathered = y if ride is not None else (y, ())
        return y, tuple(rows[k] for k in carry), tuple(gathered), riders.whole_slots(slots)

    def op_fwd(rows, params, consts, w, slot, residual, shards, slots):
        h, = fwd_call(rows, params, consts)
        ride = riders.gather_ride(shards)
        y = mm(h, w, 'nn', name + '_mm', add=residual, ride=ride)
        y, gathered = y if ride is not None else (y, ())
        out = (y, tuple(rows[k] for k in carry), tuple(gathered), riders.whole_slots(slots))
        return out, (rows, params, consts, h, w, shards)

    def op_bwd(res, cts):
        rows, params, consts, h, w, shards = res
        g, carried, _, slot_grads = cts
        dw = mm(h, g, 'tn', name + '_dw', BF16)
        ride = riders.scatter_ride(slot_grads, dw)
        dh = mm(g, w, 'nt', name + '_dx', ride=ride)
        dh, received = dh if ride is not None else (dh, ())
        if riders.own is not None:
            received, dw = received[:-1], received[-1]
        drows, dparams = bwd_call(rows, params, consts, (dh,), carried)
        return (drows, dparams, tuple(jnp.zeros_like(c) for c in consts), jnp.zeros_like(w), dw,
                None if residual is None else g, tuple(jnp.zeros_like(s) for s in shards), tuple(received))

    op.defvjp(op_fwd, op_bwd)
    return op(tuple(rows), tuple(params), tuple(consts), w, slot, residual, riders.shards, riders.slots)


def _rms(x, g):
    return x * lax.rsqrt(jnp.mean(x * x, axis=-1, keepdims=True) + EPS) * g


def rmsnorm(x, g, name, carry=False):
    out = rowwise(lambda x, g: (_rms(x, g),), [x], [g.reshape(1, -1)], [], [x.shape[1]], name,
                  carry=(0,) if carry else ())
    return out if carry else out[0]


def _split2(x):
    hi = x.astype(BF16)
    return hi, (x - hi.astype(F32)).astype(BF16)


@jax.custom_vjp
def select_mm(x, sel):
    return sum(_dg(t, sel, 1, 0) for t in _split2(x))


def _select_mm_fwd(x, sel):
    return select_mm(x, sel), sel


def _select_mm_bwd(sel, g):
    return sum(_dg(t, sel, 1, 1) for t in _split2(g)), jnp.zeros_like(sel)


select_mm.defvjp(_select_mm_fwd, _select_mm_bwd)


def groupnorm(x, g, group, name):
    width = x.shape[1]
    g_full = jnp.tile(g.reshape(1, group), (1, width // group))
    if group % LANES == 0:
        def f(x, g_full):
            outs = []
            for lo in range(0, width, group):
                xs = x[:, lo:lo + group]
                outs.append(_rms(xs, g_full[:, lo:lo + group]))
            return (jnp.concatenate(outs, axis=-1),)

        return rowwise(f, [x], [g_full], [], [width], name)[0]

    gid = jnp.arange(width) // group
    sel = (gid[:, None] == jnp.arange(LANES)[None, :]).astype(BF16)

    def f(x, g_full, sel, sel_t):
        ms = select_mm(x * x, sel) * (1.0 / group)
        inv = select_mm(lax.rsqrt(ms + EPS), sel_t)
        return (x * inv * g_full,)

    return rowwise(f, [x], [g_full], [sel, sel.T], [width], name)[0]


def glu(x, name):
    half = x.shape[1] // 2

    def f(x):
        return (x[:, :half] * jax.nn.sigmoid(x[:, half:]),)

    return rowwise(f, [x], [], [], [half], name)[0]


def swiglu_block(x):
    half = x.shape[1] // 2
    gate = x[:, :half]
    return gate * jax.nn.sigmoid(gate) * x[:, half:]


def ln_silu_block(x, g, b):
    mu = jnp.mean(x, axis=-1, keepdims=True)
    xc = x - mu
    var = jnp.mean(xc * xc, axis=-1, keepdims=True)
    y = xc * lax.rsqrt(var + EPS) * g + b
    return y * jax.nn.sigmoid(y)


def branch_norms_block(a, b, c, g):
    w1, w2 = a.shape[1], b.shape[1]
    return jnp.concatenate([_rms(a, g[:, :w1]), _rms(b, g[:, w1:w1 + w2]), _rms(c, g[:, w1 + w2:])], axis=-1)


def xa_core_block(q, k, v):
    scale = XA_HEAD_DIM ** -0.5
    outs = []
    for h in range(XA_HEADS):
        sl = slice(h * XA_HEAD_DIM, (h + 1) * XA_HEAD_DIM)
        s = bdot_nt(q[:, sl], k[:, sl]) * scale
        m = lax.stop_gradient(jnp.max(s, axis=-1, keepdims=True))
        e = jnp.exp(s - m)
        p = e / jnp.sum(e, axis=-1, keepdims=True)
        outs.append(bdot_nn(p, v[:, sl]))
    return jnp.concatenate(outs, axis=-1)


def loss_rows(y, target, name):
    def f(y, t):
        d = y - t
        return (0.5 * jnp.mean(d * d, axis=-1, keepdims=True),)

    return rowwise(f, [y, target], [], [], [1], name, need_row_grad=[True, False])[0]


def _swiglu(gate, up):
    return gate * jax.nn.sigmoid(gate) * up


def mm_swiglu(h, w, name, ride=None):
    M, K = h.shape
    H = w.shape[1] // 2
    tm, tn = _pick(M, 1024), _pick(H, 1536)
    nj = H // tn
    grid = (M // tm, 2 * nj)

    def body(*refs):
        (a_ref, b_ref), (gu_ref, act_ref), (kept,), riding = _split_refs(ride, 2, 2, refs)
        ride_done = _ride_ends(ride, riding, grid)
        j = pl.program_id(1)
        prod = _dg(a_ref[...], b_ref[...], 1, 0)
        gu_ref[...] = prod

        @pl.when(j < nj)
        def _():
            kept[j] = prod

        @pl.when(j >= nj)
        def _():
            act_ref[...] = _swiglu(kept[j - nj], prod).astype(BF16)

        ride_done()

    return _ride_call(
        body, name, grid,
        [pl.BlockSpec((tm, K), lambda i, j: (i, 0)), pl.BlockSpec((K, tn), lambda i, j: (0, j))],
        [pl.BlockSpec((tm, tn), lambda i, j: (i, j)),
         pl.BlockSpec((tm, tn), lambda i, j: (i, jnp.maximum(j - nj, 0)))],
        [jax.ShapeDtypeStruct((M, 2 * H), F32), jax.ShapeDtypeStruct((M, H), BF16)],
        (h, w), ride, ("parallel", "arbitrary"), scratch=[pltpu.VMEM((nj, tm, tn), F32)])


def mm_swiglu_bwd(dy, w_out, gu, name):
    M, D = dy.shape
    H = gu.shape[1] // 2
    tm, tn = _pick(M, 512), _pick(H, 1536)
    nj = H // tn

    def body(dy_ref, w_ref, gate_ref, up_ref, o_ref, kept):
        j = pl.program_id(1)

        @pl.when(j < nj)
        def _():
            dact = _dg(dy_ref[...], w_ref[...], 1, 1)
            _, vjp = jax.vjp(_swiglu, gate_ref[...], up_ref[...])
            dgate, dup = vjp(dact)
            o_ref[...] = dgate.astype(BF16)
            kept[j] = dup.astype(BF16)

        @pl.when(j >= nj)
        def _():
            o_ref[...] = kept[j - nj]

    def tile(j):
        return jnp.minimum(j, nj - 1)

    return pl.pallas_call(
        body, name=name, grid=(M // tm, 2 * nj),
        in_specs=[pl.BlockSpec((tm, D), lambda i, j: (i, 0)),
                  pl.BlockSpec((tn, D), lambda i, j: (tile(j), 0)),
                  pl.BlockSpec((tm, tn), lambda i, j: (i, tile(j))),
                  pl.BlockSpec((tm, tn), lambda i, j: (i, nj + tile(j)))],
        out_specs=pl.BlockSpec((tm, tn), lambda i, j: (i, j)),
        out_shape=jax.ShapeDtypeStruct((M, 2 * H), BF16),
        scratch_shapes=[pltpu.VMEM((nj, tm, tn), BF16)],
        compiler_params=_cparams(("parallel", "arbitrary")),
    )(dy, w_out, gu, gu)


def ffn(x, gain, w_in, slot_in, w_out, slot_out, name, riders=NO_RIDERS):
    norm_fwd, norm_bwd = _rowwise_calls(lambda x, g: (_rms(x, g),), [x], [gain], [], [x.shape[1]],
                                        name + '_norm', None, None, (0,), BF16)

    def forward(x, gain, w_in, w_out, shards):
        h, = norm_fwd((x,), (gain,), ())
        (gu, act), gathered = mm_swiglu(h, w_in, name + '_in', riders.gather_ride(shards))
        return mm(act, w_out, 'nn', name + '_out', add=x), tuple(gathered), (h, gu, act)

    @jax.custom_vjp
    def op(x, gain, w_in, slot_in, w_out, slot_out, shards, slots):
        y, gathered, _ = forward(x, gain, w_in, w_out, shards)
        return y, gathered, riders.whole_slots(slots)

    def op_fwd(x, gain, w_in, slot_in, w_out, slot_out, shards, slots):
        y, gathered, (h, gu, act) = forward(x, gain, w_in, w_out, shards)
        return (y, gathered, riders.whole_slots(slots)), (x, gain, h, gu, act, w_in, w_out, shards)

    def op_bwd(res, cts):
        x, gain, h, gu, act, w_in, w_out, shards = res
        g, _, slot_grads = cts
        dgu = mm_swiglu_bwd(g, w_out, gu, name + '_dact')
        dw_out = mm(act, g, 'tn', name + '_out_dw', BF16)
        dw_in = mm(h, dgu, 'tn', name + '_in_dw', BF16)
        ride = riders.scatter_ride(slot_grads)
        dh = mm(dgu, w_in, 'nt', name + '_in_dx', ride=ride)
        dh, received = dh if ride is not None else (dh, ())
        (dx,), (dgain,) = norm_bwd((x,), (gain,), (), (dh,), (g,))
        return (dx, dgain, jnp.zeros_like(w_in), dw_in, jnp.zeros_like(w_out), dw_out,
                tuple(jnp.zeros_like(s) for s in shards), tuple(received))

    op.defvjp(op_fwd, op_bwd)
    return op(x, gain, w_in, slot_in, w_out, slot_out, riders.shards, riders.slots)


def _hilo(x, ones_bf16):
    hi = x.astype(BF16)
    lo = (x - hi.astype(F32)).astype(BF16)
    return _dg(hi, ones_bf16, 1, 0) + _dg(lo, ones_bf16, 1, 0)


def _sb_block(qh, kb, c, valid, strict_upper):
    z = _dg(qh, kb, 1, 1)
    a = jnp.minimum(z, 0.0) - jnp.log(1.0 + jnp.exp(-jnp.abs(z)))
    b = jnp.where(valid, a - z, 0.0)
    s = _hilo(b, strict_upper) + c
    w = jnp.where(valid, jnp.exp(a + s), 0.0)
    return a, b, w


def _sb_masks(T):
    row = lax.broadcasted_iota(jnp.int32, (T, T), 0)
    col = lax.broadcasted_iota(jnp.int32, (T, T), 1)
    return col < row, (row > col).astype(BF16), (row >= col).astype(BF16)


def _sb_key_blocks(i, j, T, causal):
    second = jnp.maximum(j - 1, 0)
    return [(pl.multiple_of(j * T, T), jnp.logical_or(causal, j != i)),
            (pl.multiple_of(second * T, T), jnp.logical_and(jnp.logical_or(causal, True), j >= 1))]


HEADS_PER_BLOCK = LANES // SB_HEAD_DIM


def _head_mask(h):
    lane = lax.broadcasted_iota(jnp.int32, (1, LANES), 1)
    return (lane // SB_HEAD_DIM == h).astype(F32)


def _max_all(columns):
    m = columns[0]
    for c in columns[1:]:
        m = jnp.maximum(m, c)
    return jnp.max(m)


def _ride_call(body, name, grid, in_specs, out_specs, out_shape, operands, ride, semantics, scratch=()):
    if ride is None:
        outs = pl.pallas_call(body, name=name, grid=grid, in_specs=in_specs, out_specs=out_specs,
                              out_shape=out_shape, scratch_shapes=list(scratch),
                              compiler_params=_cparams(semantics))(*operands)
        return outs, ()
    outs = pl.pallas_call(
        body, name=name, grid=grid, in_specs=in_specs + ride.in_specs, out_specs=out_specs + ride.out_specs,
        out_shape=out_shape + ride.out_shape, scratch_shapes=list(scratch) + ride.scratch,
        compiler_params=_cparams(("arbitrary",) * len(grid), has_side_effects=True),
    )(*operands, *ride.arrays)
    return outs[:len(out_shape)], outs[len(out_shape):]


def _ride_ends(ride, riding, grid):
    if riding is None:
        return lambda: None
    first, last = None, None
    for axis, size in enumerate(grid):
        at0, at1 = pl.program_id(axis) == 0, pl.program_id(axis) == size - 1
        first = at0 if first is None else jnp.logical_and(first, at0)
        last = at1 if last is None else jnp.logical_and(last, at1)
    return ride.at_ends(riding, first, last)


def _sb_fwd_call(q, k, v, T, name, ride=None):
    L, W = q.shape
    scale = SB_HEAD_DIM ** -0.5
    grid = (W // LANES, L // T)

    def body(*refs):
        (q_ref, k_ref, v_ref), (o_ref,), _, riding = _split_refs(ride, 3, 1, refs)
        ride_done = _ride_ends(ride, riding, grid)
        i = pl.program_id(1)
        causal, strict_upper, _ = _sb_masks(T)
        q2 = q_ref[...] * scale
        masks = [_head_mask(h) for h in range(HEADS_PER_BLOCK)]
        qs = [(q2 * hm).astype(BF16) for hm in masks]
        zero = jnp.zeros((T, 1), F32)

        def cond(state):
            j, cs, _ = state
            return jnp.logical_and(j >= 0, _max_all(cs) > -SB_CUT)

        def step(state):
            j, cs, acc = state
            blocks = _sb_key_blocks(i, j, T, causal)
            ks = [k_ref[pl.ds(r0, T), :].astype(BF16) for r0, _ in blocks]
            vs = [v_ref[pl.ds(r0, T), :] for r0, _ in blocks]
            new_cs = []
            for hm, qh, c in zip(masks, qs, cs):
                for (_, valid), kb, vb in zip(blocks, ks, vs):
                    _, b, w = _sb_block(qh, kb, c, valid, strict_upper)
                    vh = (vb * hm).astype(BF16)
                    w_hi = w.astype(BF16)
                    w_lo = (w - w_hi.astype(F32)).astype(BF16)
                    acc = acc + _dg(w_hi, vh, 1, 0) + _dg(w_lo, vh, 1, 0)
                    c = c + jnp.sum(b, axis=1, keepdims=True)
                new_cs.append(c)
            return j - len(blocks), tuple(new_cs), acc

        _, _, acc = lax.while_loop(cond, step, (i, (zero,) * HEADS_PER_BLOCK, jnp.zeros((T, LANES), F32)))
        o_ref[...] = acc
        ride_done()

    (o,), rode = _ride_call(
        body, name, grid,
        [pl.BlockSpec((T, LANES), lambda p, i: (i, p)),
         pl.BlockSpec((L, LANES), lambda p, i: (0, p)),
         pl.BlockSpec((L, LANES), lambda p, i: (0, p))],
        [pl.BlockSpec((T, LANES), lambda p, i: (i, p))], [jax.ShapeDtypeStruct((L, W), F32)],
        (q, k, v), ride, ("parallel", "parallel"))
    return o, rode


def _sb_bwd_call(q, k, v, o, do, T, name, ride=None):
    L, W = q.shape
    scale = SB_HEAD_DIM ** -0.5
    grid = (W // LANES, L // T)

    def body(*refs):
        (q_ref, k_ref, v_ref, o_ref, do_ref), (dq_ref, dk_ref, dv_ref), _, riding = _split_refs(ride, 5, 3, refs)
        ride_done = _ride_ends(ride, riding, grid)
        i = pl.program_id(1)

        @pl.when(i == 0)
        def _():
            dk_ref[...] = jnp.zeros_like(dk_ref)
            dv_ref[...] = jnp.zeros_like(dv_ref)

        causal, strict_upper, upper = _sb_masks(T)
        q2 = q_ref[...] * scale
        do2 = do_ref[...]
        o2 = o_ref[...]
        masks = [_head_mask(h) for h in range(HEADS_PER_BLOCK)]
        qs = [(q2 * hm).astype(BF16) for hm in masks]
        dos = [(do2 * hm).astype(BF16) for hm in masks]
        totals = [jnp.sum(doh.astype(F32) * o2, axis=1, keepdims=True) for doh in dos]
        zero = jnp.zeros((T, 1), F32)

        def cond(state):
            j, cs, _, _ = state
            return jnp.logical_and(j >= 0, _max_all(cs) > -SB_CUT)

        def step(state):
            j, cs, rs, dq = state
            blocks = _sb_key_blocks(i, j, T, causal)
            kfs = [k_ref[pl.ds(r0, T), :] for r0, _ in blocks]
            ks = [kf.astype(BF16) for kf in kfs]
            vs = [v_ref[pl.ds(r0, T), :].astype(BF16) for r0, _ in blocks]
            dks = [jnp.zeros((T, LANES), F32) for _ in blocks]
            dvs = [jnp.zeros((T, LANES), F32) for _ in blocks]
            new_cs, new_rs = [], []
            for hm, qh, doh, total, c, r in zip(masks, qs, dos, totals, cs, rs):
                for n, ((_, valid), kf, kb, vb) in enumerate(zip(blocks, kfs, ks, vs)):
                    a, b, w = _sb_block(qh, kb, c, valid, strict_upper)
                    e = _dg(doh, vb, 1, 1) * w
                    before = total - (_hilo(e, upper) + r)
                    dz = jnp.where(valid, e * jnp.exp(b) - before * jnp.exp(a), 0.0).astype(BF16)
                    dq = dq + _dg(dz, kf * hm, 1, 0)
                    dks[n] = dks[n] + _dg(dz, qh, 0, 0)
                    dvs[n] = dvs[n] + _dg(w, doh, 0, 0)
                    c = c + jnp.sum(b, axis=1, keepdims=True)
                    r = r + jnp.sum(e, axis=1, keepdims=True)
                new_cs.append(c)
                new_rs.append(r)
            for (r0, _), dk, dv in zip(blocks, dks, dvs):
                dk_ref[pl.ds(r0, T), :] += dk
                dv_ref[pl.ds(r0, T), :] += dv
            return j - len(blocks), tuple(new_cs), tuple(new_rs), dq

        init = (i, (zero,) * HEADS_PER_BLOCK, (zero,) * HEADS_PER_BLOCK, jnp.zeros((T, LANES), F32))
        dq = lax.while_loop(cond, step, init)[3]
        dq_ref[...] = dq * scale
        ride_done()

    blk = pl.BlockSpec((T, LANES), lambda p, i: (i, p))
    full = pl.BlockSpec((L, LANES), lambda p, i: (0, p))
    return _ride_call(body, name, grid, [blk, full, full, blk, blk], [blk, full, full],
                      [jax.ShapeDtypeStruct((L, W), F32)] * 3, (q, k, v, o, do), ride, ("parallel", "arbitrary"))


def sb_attention(q, k, v, name, riders=NO_RIDERS):
    T = min(256, q.shape[0])

    @jax.custom_vjp
    def op(q, k, v, shards, slots):
        o, gathered = _sb_fwd_call(q, k, v, T, name + '_fwd', riders.gather_ride(shards))
        return o, tuple(gathered), riders.whole_slots(slots)

    def op_fwd(q, k, v, shards, slots):
        out = op(q, k, v, shards, slots)
        return out, (q, k, v, out[0], shards)

    def op_bwd(res, cts):
        q, k, v, o, shards = res
        do, _, slot_grads = cts
        grads, received = _sb_bwd_call(q, k, v, o, do, T, name + '_bwd', riders.scatter_ride(slot_grads))
        return (*grads, tuple(jnp.zeros_like(s) for s in shards), tuple(received))

    op.defvjp(op_fwd, op_bwd)
    return op(q, k, v, riders.shards, riders.slots)


def _dwconv_fwd_call(x, w, b, T, name):
    L, C = x.shape
    per = T // CONV_HALO
    lead = CONV_HALO - (CONV_WIDTH - 1)

    def body(x_ref, halo_ref, w_ref, b_ref, o_ref, buf):
        i = pl.program_id(0)
        buf[0:CONV_HALO, :] = jnp.where(i > 0, halo_ref[...], 0.0)
        buf[CONV_HALO:CONV_HALO + T, :] = x_ref[...]
        acc = jnp.zeros((T, C), F32) + b_ref[...]
        for j in range(CONV_WIDTH):
            acc = acc + w_ref[j:j + 1, :] * buf[lead + j:lead + j + T, :]
        o_ref[...] = acc

    return pl.pallas_call(
        body, name=name, grid=(L // T,),
        in_specs=[pl.BlockSpec((T, C), lambda i: (i, 0)),
                  pl.BlockSpec((CONV_HALO, C), lambda i: (jnp.maximum(i * per - 1, 0), 0)),
                  pl.BlockSpec(w.shape, lambda i: (0, 0)),
                  pl.BlockSpec(b.shape, lambda i: (0, 0))],
        out_specs=pl.BlockSpec((T, C), lambda i: (i, 0)),
        out_shape=jax.ShapeDtypeStruct((L, C), F32),
        scratch_shapes=[pltpu.VMEM((T + CONV_HALO, C), F32)],
        compiler_params=_cparams(("parallel",)),
    )(x, x, w, b)


def _dwconv_bwd_call(x, w, g, T, name):
    L, C = x.shape
    per = T // CONV_HALO
    n = L // T
    last_halo = L // CONV_HALO - 1
    lead = CONV_HALO - (CONV_WIDTH - 1)

    def body(x_ref, xh_ref, g_ref, gh_ref, w_ref, dx_ref, dw_ref, db_ref, bufx, bufg):
        i = pl.program_id(0)
        bufx[0:CONV_HALO, :] = jnp.where(i > 0, xh_ref[...], 0.0)
        bufx[CONV_HALO:CONV_HALO + T, :] = x_ref[...]
        gm = g_ref[...]
        bufg[0:T, :] = gm
        bufg[T:T + CONV_HALO, :] = jnp.where(i < n - 1, gh_ref[...], 0.0)
        acc = jnp.zeros((T, C), F32)
        for j in range(CONV_WIDTH):
            off = CONV_WIDTH - 1 - j
            acc = acc + w_ref[j:j + 1, :] * bufg[off:off + T, :]
        dx_ref[...] = acc

        @pl.when(i == 0)
        def _():
            dw_ref[...] = jnp.zeros_like(dw_ref)
            db_ref[...] = jnp.zeros_like(db_ref)

        for j in range(CONV_WIDTH):
            dw_ref[j:j + 1, :] += jnp.sum(gm * bufx[lead + j:lead + j + T, :], axis=0, keepdims=True)
        db_ref[...] += jnp.sum(gm, axis=0, keepdims=True)

    return pl.pallas_call(
        body, name=name, grid=(n,),
        in_specs=[pl.BlockSpec((T, C), lambda i: (i, 0)),
                  pl.BlockSpec((CONV_HALO, C), lambda i: (jnp.maximum(i * per - 1, 0), 0)),
                  pl.BlockSpec((T, C), lambda i: (i, 0)),
                  pl.BlockSpec((CONV_HALO, C), lambda i: (jnp.minimum((i + 1) * per, last_halo), 0)),
                  pl.BlockSpec(w.shape, lambda i: (0, 0))],
        out_specs=[pl.BlockSpec((T, C), lambda i: (i, 0)),
                   pl.BlockSpec(w.shape, lambda i: (0, 0)),
                   pl.BlockSpec((1, C), lambda i: (0, 0))],
        out_shape=[jax.ShapeDtypeStruct((L, C), F32), jax.ShapeDtypeStruct(w.shape, F32),
                   jax.ShapeDtypeStruct((1, C), F32)],
        scratch_shapes=[pltpu.VMEM((T + CONV_HALO, C), F32), pltpu.VMEM((T + CONV_HALO, C), F32)],
        compiler_params=_cparams(("arbitrary",)),
    )(x, x, g, g, w)


def dwconv(x, w, b, name):
    T = min(512, x.shape[0])

    @jax.custom_vjp
    def op(x, w, b):
        return _dwconv_fwd_call(x, w, b, T, name + '_fwd')

    def op_fwd(x, w, b):
        return _dwconv_fwd_call(x, w, b, T, name + '_fwd'), (x, w)

    def op_bwd(res, g):
        x, w = res
        return tuple(_dwconv_bwd_call(x, w, g, T, name + '_bwd'))

    op.defvjp(op_fwd, op_bwd)
    return op(x, w, b)


def _ssm_fwd_call(u, ar, ai, bbr, bbi, cr, ci, d, T, name):
    L, C = u.shape
    S = SSM_LANES

    def body(u_ref, ar_ref, ai_ref, bbr_ref, bbi_ref, cr_ref, ci_ref, d_ref,
             y_ref, xr_ref, xi_ref, st_r, st_i, in_r, in_i, out_r, out_i):
        i = pl.program_id(0)

        @pl.when(i == 0)
        def _():
            st_r[...] = jnp.zeros_like(st_r)
            st_i[...] = jnp.zeros_like(st_i)

        u_blk = u_ref[...]
        xr_ref[...] = _dg(u_blk, bbr_ref[...], 1, 0)
        xi_ref[...] = _dg(u_blk, bbi_ref[...], 1, 0)
        a_r, a_i = ar_ref[...], ai_ref[...]

        def tile(t, carry):
            sr, si = carry
            r0 = pl.multiple_of(t * SUBLANES, SUBLANES)
            in_r[...] = xr_ref[pl.ds(r0, SUBLANES), :]
            in_i[...] = xi_ref[pl.ds(r0, SUBLANES), :]
            for r in range(SUBLANES):
                nr = a_r * sr - a_i * si + in_r[r:r + 1, :]
                ni = a_r * si + a_i * sr + in_i[r:r + 1, :]
                sr, si = nr, ni
                out_r[r:r + 1, :] = sr
                out_i[r:r + 1, :] = si
            xr_ref[pl.ds(r0, SUBLANES), :] = out_r[...]
            xi_ref[pl.ds(r0, SUBLANES), :] = out_i[...]
            return sr, si

        sr, si = lax.fori_loop(0, T // SUBLANES, tile, (st_r[0:1, :], st_i[0:1, :]))
        st_r[0:1, :] = sr
        st_i[0:1, :] = si
        y_ref[...] = (_dg(xr_ref[...], cr_ref[...], 1, 0) - _dg(xi_ref[...], ci_ref[...], 1, 0)
                      + d_ref[...] * u_blk)

    full = lambda a: pl.BlockSpec(a.shape, lambda i: (0, 0))
    return pl.pallas_call(
        body, name=name, grid=(L // T,),
        in_specs=[pl.BlockSpec((T, C), lambda i: (i, 0))] + [full(a) for a in (ar, ai, bbr, bbi, cr, ci, d)],
        out_specs=[pl.BlockSpec((T, C), lambda i: (i, 0)), pl.BlockSpec((T, S), lambda i: (i, 0)),
                   pl.BlockSpec((T, S), lambda i: (i, 0))],
        out_shape=[jax.ShapeDtypeStruct((L, C), F32), jax.ShapeDtypeStruct((L, S), F32),
                   jax.ShapeDtypeStruct((L, S), F32)],
        scratch_shapes=[pltpu.VMEM((SUBLANES, S), F32)] * 6,
        compiler_params=_cparams(("arbitrary",)),
    )(u, ar, ai, bbr, bbi, cr, ci, d)


def _ssm_bwd_call(u, xr, xi, dy, ar, ai, bbr, bbi, cr, ci, d, T, name):
    L, C = u.shape
    S = SSM_LANES
    n = L // T
    per = T // SUBLANES

    def body(u_ref, xr_ref, xi_ref, hr_ref, hi_ref, dy_ref, ar_ref, ai_ref, bbr_ref, bbi_ref, cr_ref, ci_ref,
             d_ref, du_ref, dar_ref, dai_ref, dbr_ref, dbi_ref, dcr_ref, dci_ref, dd_ref,
             lam_r, lam_i, prev_r, prev_i, st_r, st_i, in_r, in_i, out_r, out_i):
        i = pl.program_id(0)
        chunk = n - 1 - i

        @pl.when(i == 0)
        def _():
            st_r[...] = jnp.zeros_like(st_r)
            st_i[...] = jnp.zeros_like(st_i)
            for ref in (dar_ref, dai_ref, dbr_ref, dbi_ref, dcr_ref, dci_ref, dd_ref):
                ref[...] = jnp.zeros_like(ref)

        dy_blk = dy_ref[...]
        u_blk = u_ref[...]
        lam_r[...] = _dg(dy_blk, cr_ref[...], 1, 1)
        lam_i[...] = -_dg(dy_blk, ci_ref[...], 1, 1)
        dcr_ref[...] += _dg(xr_ref[...], dy_blk, 0, 0)
        dci_ref[...] -= _dg(xi_ref[...], dy_blk, 0, 0)
        a_r, a_i = ar_ref[...], ai_ref[...]

        def tile(k, carry):
            lr, li = carry
            r0 = pl.multiple_of((per - 1 - k) * SUBLANES, SUBLANES)
            in_r[...] = lam_r[pl.ds(r0, SUBLANES), :]
            in_i[...] = lam_i[pl.ds(r0, SUBLANES), :]
            for r in range(SUBLANES - 1, -1, -1):
                nr = in_r[r:r + 1, :] + a_r * lr + a_i * li
                ni = in_i[r:r + 1, :] + a_r * li - a_i * lr
                lr, li = nr, ni
                out_r[r:r + 1, :] = lr
                out_i[r:r + 1, :] = li
            lam_r[pl.ds(r0, SUBLANES), :] = out_r[...]
            lam_i[pl.ds(r0, SUBLANES), :] = out_i[...]
            return lr, li

        lr, li = lax.fori_loop(0, per, tile, (st_r[0:1, :], st_i[0:1, :]))
        st_r[0:1, :] = lr
        st_i[0:1, :] = li

        l_r, l_i = lam_r[...], lam_i[...]
        du_ref[...] = _dg(l_r, bbr_ref[...], 1, 1) + _dg(l_i, bbi_ref[...], 1, 1) + d_ref[...] * dy_blk
        dbr_ref[...] += _dg(u_blk, l_r, 0, 0)
        dbi_ref[...] += _dg(u_blk, l_i, 0, 0)
        dd_ref[...] += jnp.sum(dy_blk * u_blk, axis=0, keepdims=True)

        prev_r[0:SUBLANES, :] = jnp.where(chunk > 0, hr_ref[...], 0.0)
        prev_i[0:SUBLANES, :] = jnp.where(chunk > 0, hi_ref[...], 0.0)
        prev_r[SUBLANES:SUBLANES + T, :] = xr_ref[...]
        prev_i[SUBLANES:SUBLANES + T, :] = xi_ref[...]
        p_r = prev_r[SUBLANES - 1:SUBLANES - 1 + T, :]
        p_i = prev_i[SUBLANES - 1:SUBLANES - 1 + T, :]
        dar_ref[...] += jnp.sum(l_r * p_r + l_i * p_i, axis=0, keepdims=True)
        dai_ref[...] += jnp.sum(l_i * p_r - l_r * p_i, axis=0, keepdims=True)

    rev = lambda w: pl.BlockSpec((T, w), lambda i: (n - 1 - i, 0))
    halo = pl.BlockSpec((SUBLANES, S), lambda i: (jnp.maximum((n - 1 - i) * per - 1, 0), 0))
    full = lambda a: pl.BlockSpec(a.shape, lambda i: (0, 0))
    params = (ar, ai, bbr, bbi, cr, ci, d)
    return pl.pallas_call(
        body, name=name, grid=(n,),
        in_specs=[rev(C), rev(S), rev(S), halo, halo, rev(C)] + [full(a) for a in params],
        out_specs=[rev(C)] + [full(a) for a in params],
        out_shape=[jax.ShapeDtypeStruct((L, C), F32)] + [jax.ShapeDtypeStruct(a.shape, F32) for a in params],
        scratch_shapes=[pltpu.VMEM((T, S), F32), pltpu.VMEM((T, S), F32),
                        pltpu.VMEM((T + SUBLANES, S), F32), pltpu.VMEM((T + SUBLANES, S), F32)]
        + [pltpu.VMEM((SUBLANES, S), F32)] * 6,
        compiler_params=_cparams(("arbitrary",)),
    )(u, xr, xi, xr, xi, dy, *params)


def ssm_core(u, ar, ai, bbr, bbi, cr, ci, d, name):
    T = min(256, u.shape[0])

    @jax.custom_vjp
    def op(u, ar, ai, bbr, bbi, cr, ci, d):
        return _ssm_fwd_call(u, ar, ai, bbr, bbi, cr, ci, d, T, name + '_fwd')[0]

    def op_fwd(u, ar, ai, bbr, bbi, cr, ci, d):
        y, xr, xi = _ssm_fwd_call(u, ar, ai, bbr, bbi, cr, ci, d, T, name + '_fwd')
        return y, (u, xr, xi, ar, ai, bbr, bbi, cr, ci, d)

    def op_bwd(res, dy):
        u, xr, xi, ar, ai, bbr, bbi, cr, ci, d = res
        return tuple(_ssm_bwd_call(u, xr, xi, dy, ar, ai, bbr, bbi, cr, ci, d, T, name + '_bwd'))

    op.defvjp(op_fwd, op_bwd)
    return op(u, ar, ai, bbr, bbi, cr, ci, d)


@jax.custom_vjp
def _block_diag(blocks):
    G, R, Cc = blocks.shape
    eye = jnp.eye(G, dtype=blocks.dtype)
    return (blocks[:, :, None, :] * eye[:, None, :, None]).reshape(G * R, G * Cc)


def _block_diag_fwd(blocks):
    return _block_diag(blocks), blocks.shape


def _block_diag_bwd(shape, g):
    G, R, Cc = shape
    on_diagonal = jnp.eye(G, dtype=bool)[:, None, :, None]
    return (jnp.sum(jnp.where(on_diagonal, g.reshape(G, R, G, Cc), 0.0), axis=2),)


_block_diag.defvjp(_block_diag_fwd, _block_diag_bwd)


def ssm_discretise(lam_re, lam_im, log_dt, b_re, b_im, c_re, c_im):
    dt = jnp.exp(log_dt)[:, None]
    mag = jnp.exp(lam_re * dt)
    ar, ai = mag * jnp.cos(lam_im * dt), mag * jnp.sin(lam_im * dt)
    den = lam_re * lam_re + lam_im * lam_im
    fr = ((ar - 1.0) * lam_re + ai * lam_im) / den
    fi = (ai * lam_re - (ar - 1.0) * lam_im) / den
    bbr = fr[..., None] * b_re - fi[..., None] * b_im
    bbi = fr[..., None] * b_im + fi[..., None] * b_re
    return (ar.reshape(1, SSM_LANES), ai.reshape(1, SSM_LANES),
            _block_diag(bbr.transpose(0, 2, 1)), _block_diag(bbi.transpose(0, 2, 1)),
            _block_diag(c_re.transpose(0, 2, 1)), _block_diag(c_im.transpose(0, 2, 1)))


def split_columns(p, bounds):
    @jax.custom_vjp
    def op(p):
        return tuple(p[:, lo:hi] for lo, hi in zip(bounds[:-1], bounds[1:]))

    def op_fwd(p):
        return op(p), None

    def op_bwd(_, gs):
        return (jnp.concatenate(gs, axis=1),)

    op.defvjp(op_fwd, op_bwd)
    return op(p)


DEPTH = 2
EARLY = ['w_in', 'conv_pw2_w', 'ssm_glu_w']
LATE = [n for n in MATRICES if n not in EARLY]
FFN = ['ffn_w_in', 'ffn_w_out']
GATHER_AT = {
    'start': [('w_in', 0)],
    'sb0': [(n, 0) for n in MATRICES if n != 'w_in'],
    'ffn0': [(n, 1) for n in MATRICES if n not in FFN],
    'sb1': [(n, 1) for n in FFN],
}
SCATTER_AT = {
    'sb1': [(n, 1) for n in LATE],
    'ffn0': [(n, 1) for n in EARLY],
    'sb0': [(n, 0) for n in LATE],
    'win0': [(n, 0) for n in EARLY if n != 'w_in'],
}
OWN_AT = {'win0': ('w_in', 0)}


def _assemble(name, gathered):
    if name not in SLOTTED:
        return gathered
    return jnp.concatenate([gathered[j] for j in range(N_CHIPS)], axis=SHARD_AXIS[name] - 1)


def local_loss(slots, w, mats, shards, x, mem, target):
    mats = dict(mats)
    slot = {key: slots[key] for key in OWN_AT.values()}
    s1, s2, s3 = SB_WIDTH, 2 * SB_WIDTH, 3 * SB_WIDTH
    s4 = s3 + 2 * CONV_CH

    def riders_at(host):
        return Riders(gather=[(n, shards[(n, l)]) for n, l in GATHER_AT.get(host, [])],
                      scatter=[(n, slots[(n, l)]) for n, l in SCATTER_AT[host]],
                      own=OWN_AT[host][0] if host in OWN_AT else None)

    def take(host, gathered, handed):
        for (n, l), g in zip(GATHER_AT.get(host, []), gathered):
            mats[(n, l)] = _assemble(n, g)
        for key, s in zip(SCATTER_AT[host], handed):
            slot[key] = s

    def linear_(x, n, l, name, residual=None, host=None):
        y, gathered, handed = linear(x, mats[(n, l)], slot[(n, l)], name, residual,
                                     riders_at(host) if host else NO_RIDERS)
        if host:
            take(host, gathered, handed)
        return y

    def fused_(f, rows, params, n, l, name, residual=None, host=None, carry=(), block_rows=None):
        y, carried, gathered, handed = rowwise_linear(
            f, rows, params, [], mats[(n, l)], slot[(n, l)], name, residual,
            riders_at(host) if host else NO_RIDERS, carry, block_rows)
        if host:
            take(host, gathered, handed)
        return (y,) + tuple(carried)

    def gain(n, l):
        return w[n][l].reshape(1, -1)

    for l in range(DEPTH):
        tag = 'l%d_' % l
        p, x = fused_(_rms, [x], [gain('norm_mix_g', l)], 'w_in', l, tag + 'w_in', carry=(0,),
                      host='win0' if l == 0 else None)
        q, k, v, u_conv, u_ssm = split_columns(p, (0, s1, s2, s3, s4, p.shape[1]))
        q = groupnorm(q, w['sb_q_norm_g'][l], SB_HEAD_DIM, tag + 'q_norm')
        k = groupnorm(k, w['sb_k_norm_g'][l], SB_HEAD_DIM, tag + 'k_norm')
        o_sb, gathered, handed = sb_attention(q, k, v, tag + 'sb', riders_at('sb%d' % l))
        take('sb%d' % l, gathered, handed)

        dw_w = jnp.pad(w['conv_dw_w'][l], ((0, CONV_HALO - CONV_WIDTH), (0, 0)))
        hc = dwconv(glu(u_conv, tag + 'conv_glu'), dw_w, w['conv_dw_b'][l].reshape(1, -1), tag + 'dwconv')
        o_conv, = fused_(ln_silu_block, [hc], [gain('conv_ln_g', l), gain('conv_ln_b', l)], 'conv_pw2_w', l,
                         tag + 'pw2')

        ar, ai, bbr, bbi, cr, ci = ssm_discretise(
            w['ssm_lam_re'][l], w['ssm_lam_im'][l], w['ssm_log_dt'][l], w['ssm_b_re'][l], w['ssm_b_im'][l],
            w['ssm_c_re'][l], w['ssm_c_im'][l])
        y = ssm_core(u_ssm, ar, ai, bbr, bbi, cr, ci, w['ssm_d'][l].reshape(1, -1), tag + 'ssm')
        o_ssm = glu(linear_(y, 'ssm_glu_w', l, tag + 'ssm_glu_w'), tag + 'ssm_glu')

        x, = fused_(branch_norms_block, [o_sb, o_conv, o_ssm], [gain('branch_norm_g', l)], 'w_out', l,
                    tag + 'w_out', residual=x)

        q_raw, x = fused_(_rms, [x], [gain('norm_xa_g', l)], 'xa_wq', l, tag + 'xa_wq', carry=(0,))
        hm = rmsnorm(mem, w['norm_mem_g'][l], tag + 'norm_mem')
        qx = groupnorm(q_raw, w['xa_q_norm_g'][l], XA_HEAD_DIM, tag + 'xa_qn')
        kx = groupnorm(linear_(hm, 'xa_wk', l, tag + 'xa_wk'), w['xa_k_norm_g'][l], XA_HEAD_DIM, tag + 'xa_kn')
        vx = linear_(hm, 'xa_wv', l, tag + 'xa_wv')
        x, = fused_(xa_core_block, [qx], [kx, vx], 'xa_wo', l, tag + 'xa_wo', residual=x,
                    block_rows=min(256, qx.shape[0]))

        host = 'ffn0' if l == 0 else None
        x, gathered, handed = ffn(x, gain('norm_ffn_g', l), mats[('ffn_w_in', l)], slot[('ffn_w_in', l)],
                                  mats[('ffn_w_out', l)], slot[('ffn_w_out', l)], tag + 'ffn',
                                  riders_at(host) if host else NO_RIDERS)
        if host:
            take(host, gathered, handed)
    return jnp.sum(loss_rows(x, target, 'loss'))


def local_step(w, mats, shards, x, mem, target):
    def shard_shape(key):
        return shards[key].shape if key in shards else _shard_shape(key[0], mats[key].shape)

    keys = [key for keys in SCATTER_AT.values() for key in keys] + list(OWN_AT.values())
    slots = {key: jnp.zeros((N_CHIPS,) + shard_shape(key), BF16) for key in keys}
    loss, (g_mats, g_w, gx) = jax.value_and_grad(local_loss, argnums=(0, 1, 4))(
        slots, w, mats, shards, x, mem, target)
    return loss, gx, g_w, g_mats


PACK_ROWS = 2048


PIECE_ROWS = 16


def _piece_rows(size):
    rows = -(-size // LANES)
    return rows, -(-rows // PIECE_ROWS) * PIECE_ROWS


def pack(arrays, dtype):
    parts, total = [], 0
    for a in arrays:
        rows, padded = _piece_rows(a.size)
        a = a.astype(dtype)
        if a.size % LANES:
            a = jnp.pad(a.reshape(-1), (0, rows * LANES - a.size))
        a = a.reshape(rows, LANES)
        if padded != rows:
            a = jnp.pad(a, ((0, padded - rows), (0, 0)))
        parts.append(a)
        total += padded
    tail = -total % PACK_ROWS
    if tail:
        parts.append(jnp.zeros((tail, LANES), dtype))
    return jnp.concatenate(parts, axis=0)


def unpack(packed, shapes):
    out, off = [], 0
    for s in shapes:
        size = math.prod(s)
        rows, padded = _piece_rows(size)
        piece = packed[off:off + rows]
        if size % LANES:
            piece = piece.reshape(-1)[:size]
        out.append(piece.reshape(s))
        off += padded
    return out


def _mesh_pos():
    return lax.axis_index("x"), lax.axis_index("y"), lax.axis_index("c")


def _exchange_xy(n_arrays, src_of, dst_of, sems, wait):
    send_sems, recv_sems, local_sems = sems
    x, y, c = _mesh_pos()
    me = 2 * x + y
    peers = [(1 - x, y), (x, 1 - y), (1 - x, 1 - y)]
    for k in range(n_arrays):
        own = pltpu.make_async_copy(src_of(k, me), dst_of(k, me), local_sems.at[k])
        if wait:
            own.wait()
        else:
            own.start()
        for p, (px, py) in enumerate(peers):
            out = pltpu.make_async_remote_copy(
                src_ref=src_of(k, 2 * px + py), dst_ref=dst_of(k, me), send_sem=send_sems.at[3 * k + p],
                recv_sem=recv_sems.at[3 * k + p], device_id=(px, py, c), device_id_type=MESH)
            if wait:
                pltpu.make_async_remote_copy(
                    src_ref=src_of(k, me), dst_ref=dst_of(k, 2 * px + py), send_sem=send_sems.at[3 * k + p],
                    recv_sem=recv_sems.at[3 * k + p], device_id=(px, py, c), device_id_type=MESH).wait_recv()
                out.wait_send()
            else:
                out.start()


class Ride:
    def __init__(self, arrays, out_shapes, src_of, dst_of):
        self.arrays, self.out_shapes = list(arrays), list(out_shapes)
        self._src_of, self._dst_of = src_of, dst_of
        n = len(self.arrays)
        self.in_specs = [pl.BlockSpec(memory_space=pl.ANY)] * n
        self.out_specs = [pl.BlockSpec(memory_space=pl.ANY)] * len(self.out_shapes)
        self.out_shape = [jax.ShapeDtypeStruct(s, BF16) for s in self.out_shapes]
        self.scratch = [pltpu.SemaphoreType.DMA((3 * n,)), pltpu.SemaphoreType.DMA((3 * n,)),
                        pltpu.SemaphoreType.DMA((n,))]

    def run(self, parts, wait):
        ins, outs, sems = parts
        _exchange_xy(len(self.arrays), lambda k, chip: self._src_of(ins, k, chip),
                     lambda k, chip: self._dst_of(outs, k, chip), sems, wait)

    def at_ends(self, parts, first, last):
        pl.when(first)(lambda: self.run(parts, False))

        def finish():
            pl.when(last)(lambda: self.run(parts, True))
        return finish


def _split_refs(ride, n_in, n_out, refs):
    if ride is None:
        return refs[:n_in], refs[n_in:n_in + n_out], refs[n_in + n_out:], None
    ni, no = len(ride.arrays), len(ride.out_shapes)
    b = n_in + ni
    c = b + n_out
    d = c + no
    return refs[:n_in], refs[b:c], refs[d:len(refs) - 3], (refs[n_in:b], refs[c:d], refs[len(refs) - 3:])


def run_ride(ride, name):
    def body(*refs):
        parts = _split_refs(ride, 0, 0, refs)[3]
        ride.run(parts, False)
        ride.run(parts, True)

    return pl.pallas_call(
        body, name=name, in_specs=ride.in_specs, out_specs=ride.out_specs, out_shape=ride.out_shape,
        scratch_shapes=ride.scratch, compiler_params=pltpu.CompilerParams(has_side_effects=True),
    )(*ride.arrays)


SLOTTED = ('w_in', 'taps')


def _part(ref, axis, chip, size):
    start = pl.multiple_of(chip * size, size)
    index = [slice(None)] * len(ref.shape)
    index[axis] = pl.ds(start, size)
    return ref.at[tuple(index)]


def _whole_shape(name, shard_shape):
    s = list(shard_shape)
    s[SHARD_AXIS[name] - 1] *= N_CHIPS
    return tuple(s)


def _shard_shape(name, whole_shape):
    s = list(whole_shape)
    s[SHARD_AXIS[name] - 1] //= N_CHIPS
    return tuple(s)


def gather_ride(names, shards):
    def out_shape(k):
        return (N_CHIPS,) + shards[k].shape if names[k] in SLOTTED else _whole_shape(names[k], shards[k].shape)

    def dst_of(outs, k, chip):
        if names[k] in SLOTTED:
            return outs[k].at[chip]
        axis = SHARD_AXIS[names[k]] - 1
        return _part(outs[k], axis, chip, shards[k].shape[axis])

    return Ride(shards, [out_shape(k) for k in range(len(shards))], lambda ins, k, chip: ins[k], dst_of)


def scatter_ride(names, grads):
    def shard_shape(k):
        return grads[k].shape[1:] if names[k] in SLOTTED else _shard_shape(names[k], grads[k].shape)

    def src_of(ins, k, chip):
        if names[k] in SLOTTED:
            return ins[k].at[chip]
        axis = SHARD_AXIS[names[k]] - 1
        return _part(ins[k], axis, chip, shard_shape(k)[axis])

    return Ride(grads, [(N_CHIPS,) + shard_shape(k) for k in range(len(grads))], src_of,
                lambda outs, k, chip: outs[k].at[chip])


def _cut_for_chips(name, g):
    if name not in SLOTTED:
        return g
    axis = SHARD_AXIS[name] - 1
    cut = g.shape[:axis] + (N_CHIPS, g.shape[axis] // N_CHIPS) + g.shape[axis + 1:]
    return jnp.moveaxis(g.reshape(cut), axis, 0)


class SwapRide(Ride):
    def __init__(self, arrays):
        self.arrays = list(arrays)
        self.out_shapes = [a.shape for a in self.arrays]
        n = len(self.arrays)
        self.in_specs = [pl.BlockSpec(memory_space=pl.ANY)] * n
        self.out_specs = [pl.BlockSpec(memory_space=pl.ANY)] * n
        self.out_shape = [jax.ShapeDtypeStruct(a.shape, a.dtype) for a in self.arrays]
        self.scratch = [pltpu.SemaphoreType.DMA((n,)), pltpu.SemaphoreType.DMA((n,)), pltpu.SemaphoreType.DMA((1,))]

    def run(self, parts, wait):
        ins, outs, (send_sems, recv_sems, _) = parts
        x, y, c = _mesh_pos()
        for k in range(len(self.arrays)):
            cp = pltpu.make_async_remote_copy(
                src_ref=ins[k], dst_ref=outs[k], send_sem=send_sems.at[k], recv_sem=recv_sems.at[k],
                device_id=(x, y, 1 - c), device_id_type=MESH)
            if wait:
                cp.wait()
            else:
                cp.start()


def allreduce_small(buf, name):
    R = buf.shape[0]
    half = R // 2

    def body(in_ref, sum_ref, other_ref, chip_ref, got_ref, send_sems, recv_sems):
        x, y, c = _mesh_pos()
        sibling = (x, y, 1 - c)
        peers = [(1 - x, y, c), (x, 1 - y, c), (1 - x, 1 - y, c)]

        def copy(k, src, dst, to):
            return pltpu.make_async_remote_copy(src_ref=src, dst_ref=dst, send_sem=send_sems.at[k],
                                                recv_sem=recv_sems.at[k], device_id=to, device_id_type=MESH)

        swap = copy(0, in_ref, other_ref, sibling)
        swap.start()
        swap.wait()
        chip_ref[...] = in_ref[...] + other_ref[...]

        mine = chip_ref.at[pl.ds(pl.multiple_of(c * half, SUBLANES), half), :]
        sends = [copy(1 + k, mine, got_ref.at[k], to) for k, to in enumerate(peers)]
        for cp in sends:
            cp.start()
        for cp in sends:
            cp.wait()
        total = (mine[...] + got_ref[0]) + (got_ref[1] + got_ref[2])

        done = sum_ref.at[pl.ds(pl.multiple_of(c * half, SUBLANES), half), :]
        done[...] = total
        back = copy(4, done, done, sibling)
        back.start()
        back.wait()

    vmem = pl.BlockSpec(memory_space=pltpu.VMEM)
    return pl.pallas_call(
        body, name=name, in_specs=[vmem], out_specs=vmem,
        out_shape=jax.ShapeDtypeStruct((R, LANES), F32),
        scratch_shapes=[pltpu.VMEM((R, LANES), F32), pltpu.VMEM((R, LANES), F32),
                        pltpu.VMEM((3, half, LANES), F32),
                        pltpu.SemaphoreType.DMA((5,)), pltpu.SemaphoreType.DMA((5,))],
        compiler_params=pltpu.CompilerParams(has_side_effects=True, vmem_limit_bytes=VMEM_LIMIT),
    )(buf)


def _adamw_update(g, w, m, v):
    m2 = ADAM_B1 * m + (1.0 - ADAM_B1) * g
    v2 = ADAM_B2 * v + (1.0 - ADAM_B2) * (g * g)
    m_hat = m2 / (1.0 - ADAM_B1 ** ADAM_STEP)
    v_hat = v2 / (1.0 - ADAM_B2 ** ADAM_STEP)
    return -ADAM_LR * (m_hat / (jnp.sqrt(v_hat) + ADAM_EPS) + ADAM_WD * w), m2, v2


def adamw_matrix(layer, mine, other, w, m, v, so_far, name):
    _, rows, cols = w.shape
    T = 16
    while rows % (2 * T) == 0 and 2 * T * cols <= 128 * 1024:
        T *= 2

    def body(mine_ref, other_ref, w_ref, m_ref, v_ref, *rest):
        g_out, d_out, m_out, v_out = rest[-4:]

        def total(ref):
            acc = ref[0].astype(F32)
            for k in range(1, N_CHIPS):
                acc = acc + ref[k].astype(F32)
            return acc

        g = total(mine_ref) + total(other_ref)
        g_out[...] = g
        d_out[...], m_out[...], v_out[...] = _adamw_update(g, w_ref[...], m_ref[...], v_ref[...])

    slots = pl.BlockSpec((N_CHIPS, T, cols), lambda i: (0, i, 0))
    spec = pl.BlockSpec((None, T, cols), lambda i: (layer, i, 0))
    filled = [] if so_far is None else list(so_far)
    return pl.pallas_call(
        body, name=name, grid=(rows // T,),
        in_specs=[slots, slots, spec, spec, spec] + [pl.BlockSpec(memory_space=pl.ANY)] * len(filled),
        out_specs=[spec] * 4,
        out_shape=[jax.ShapeDtypeStruct(w.shape, F32)] * 4,
        input_output_aliases={5 + j: j for j in range(len(filled))},
        compiler_params=_cparams(("parallel",)),
    )(mine, other, w, m, v, *filled)


def adamw_small(gs, ws, ms, vs, name):
    n = len(gs)

    def body(*refs):
        for k in range(n):
            g, w, m, v = (refs[j * n + k][...] for j in range(4))
            d_out, m_out, v_out = (refs[(4 + j) * n + k] for j in range(3))
            d_out[...], m_out[...], v_out[...] = _adamw_update(g, w, m, v)

    vmem = pl.BlockSpec(memory_space=pltpu.VMEM)
    outs = pl.pallas_call(
        body, name=name,
        in_specs=[vmem] * (4 * n), out_specs=[vmem] * (3 * n),
        out_shape=[jax.ShapeDtypeStruct(w.shape, F32) for w in ws] * 3,
        compiler_params=_cparams(),
    )(*gs, *ws, *ms, *vs)
    return outs[:n], outs[n:2 * n], outs[2 * n:]


def _shard_of(full, axis, chip):
    size = full.shape[axis] // N_CHIPS
    return lax.slice_in_dim(full, chip * size, (chip + 1) * size, axis=axis)


def kernel(x, mem, norm_mix_g, w_in, sb_q_norm_g, sb_k_norm_g, conv_dw_w, conv_dw_b, conv_ln_g, conv_ln_b, conv_pw2_w, ssm_lam_re, ssm_lam_im, ssm_log_dt, ssm_b_re, ssm_b_im, ssm_c_re, ssm_c_im, ssm_d, ssm_glu_w, branch_norm_g, w_out, norm_xa_g, norm_mem_g, xa_wq, xa_wk, xa_wv, xa_q_norm_g, xa_k_norm_g, xa_wo, norm_ffn_g, ffn_w_in, ffn_w_out, loss_target, m_norm_mix_g, m_w_in, m_sb_q_norm_g, m_sb_k_norm_g, m_conv_dw_w, m_conv_dw_b, m_conv_ln_g, m_conv_ln_b, m_conv_pw2_w, m_ssm_lam_re, m_ssm_lam_im, m_ssm_log_dt, m_ssm_b_re, m_ssm_b_im, m_ssm_c_re, m_ssm_c_im, m_ssm_d, m_ssm_glu_w, m_branch_norm_g, m_w_out, m_norm_xa_g, m_norm_mem_g, m_xa_wq, m_xa_wk, m_xa_wv, m_xa_q_norm_g, m_xa_k_norm_g, m_xa_wo, m_norm_ffn_g, m_ffn_w_in, m_ffn_w_out, v_norm_mix_g, v_w_in, v_sb_q_norm_g, v_sb_k_norm_g, v_conv_dw_w, v_conv_dw_b, v_conv_ln_g, v_conv_ln_b, v_conv_pw2_w, v_ssm_lam_re, v_ssm_lam_im, v_ssm_log_dt, v_ssm_b_re, v_ssm_b_im, v_ssm_c_re, v_ssm_c_im, v_ssm_d, v_ssm_glu_w, v_branch_norm_g, v_w_out, v_norm_xa_g, v_norm_mem_g, v_xa_wq, v_xa_wk, v_xa_wv, v_xa_q_norm_g, v_xa_k_norm_g, v_xa_wo, v_norm_ffn_g, v_ffn_w_in, v_ffn_w_out):
    given = dict(locals())
    w = {n: given[n] for n in WEIGHTS}
    m = {n: given['m_' + n] for n in WEIGHTS}
    v = {n: given['v_' + n] for n in WEIGHTS}

    depth = w_in.shape[0]
    chip = 2 * lax.axis_index("x") + lax.axis_index("y")

    assert depth == DEPTH
    taps_bits = lax.bitcast_convert_type(conv_dw_w, BF16)
    shards = {(n, l): w[n][l].astype(BF16) for n in MATRICES for l in range(depth)}
    first = [shards.pop(key) for key in GATHER_AT['start']]
    gathered = run_ride(gather_ride([n for n, _ in GATHER_AT['start']] + ['taps'], first + [taps_bits]),
                        'gather_first')
    mats = {key: _assemble(key[0], g) for key, g in zip(GATHER_AT['start'], gathered)}
    taps = jnp.concatenate([lax.bitcast_convert_type(gathered[-1][j], F32) for j in range(N_CHIPS)], axis=2)
    local_w = {n: w[n] for n in REPLICATED}
    local_w['conv_dw_w'] = taps

    loss, gx, g_w, mine = local_step(local_w, mats, shards, x[0], mem[0], loss_target[0])

    keys = [(n, l) for n in MATRICES for l in range(depth)]
    other = dict(zip(keys, run_ride(SwapRide([mine[key] for key in keys]), 'swap_cores')))
    outs = {}
    for n in MATRICES:
        res = None
        for l in range(depth):
            res = adamw_matrix(l, mine[(n, l)], other[(n, l)], w[n], m[n], v[n], res, 'adamw_%s_%d' % (n, l))
        outs['grad_' + n], outs['delta_' + n], outs['new_m_' + n], outs['new_v_' + n] = res

    reduced = allreduce_small(pack([g_w[n] for n in REPLICATED] + [g_w['conv_dw_w'], loss.reshape(1)], F32),
                              'allreduce_small')
    reduced = unpack(reduced, [w[n].shape for n in REPLICATED] + [taps.shape, (1,)])
    total_loss = reduced[-1].reshape(())
    tap_cols = conv_dw_w.shape[2]
    reduced[-2] = lax.dynamic_slice_in_dim(reduced[-2], chip * tap_cols, tap_cols, axis=2)
    small_names = REPLICATED + ['conv_dw_w']
    deltas, new_ms, new_vs = adamw_small(reduced[:-1], [w[n] for n in small_names], [m[n] for n in small_names],
                                         [v[n] for n in small_names], 'adamw_small')
    for k, n in enumerate(small_names):
        outs['grad_' + n], outs['delta_' + n] = reduced[k], deltas[k]
        outs['new_m_' + n], outs['new_v_' + n] = new_ms[k], new_vs[k]
    return (total_loss, gx[None], *[outs['grad_' + n] for n in WEIGHTS], *[outs['delta_' + n] for n in WEIGHTS],
            *[outs['new_m_' + n] for n in WEIGHTS], *[outs['new_v_' + n] for n in WEIGHTS])
```

```python
import functools
import math

import jax
import jax.numpy as jnp
from jax import lax
from jax.experimental import pallas as pl
from jax.experimental.pallas import tpu as pltpu

F32 = jnp.float32
BF16 = jnp.bfloat16
MESH = pl.DeviceIdType.MESH
HIGHEST = lax.Precision.HIGHEST

EPS = 1e-6
LANES = 128
SUBLANES = 8
VMEM_LIMIT = 56 * 1024 * 1024

SB_HEAD_DIM = 64
SB_WIDTH = 512
CONV_CH = 256
CONV_WIDTH = 31
CONV_HALO = 32
SSM_CH = 256
SSM_GROUPS = 16
SSM_GROUP = 16
SSM_STATE = 64
SSM_LANES = SSM_GROUPS * SSM_STATE
XA_HEADS = 4
XA_HEAD_DIM = 256
SB_CUT = 110.0

ADAM_LR = 0.001
ADAM_B1 = 0.9
ADAM_B2 = 0.999
ADAM_EPS = 1e-08
ADAM_WD = 0.01
ADAM_STEP = 10

WEIGHTS = ['norm_mix_g', 'w_in', 'sb_q_norm_g', 'sb_k_norm_g', 'conv_dw_w', 'conv_dw_b', 'conv_ln_g',
           'conv_ln_b', 'conv_pw2_w', 'ssm_lam_re', 'ssm_lam_im', 'ssm_log_dt', 'ssm_b_re', 'ssm_b_im',
           'ssm_c_re', 'ssm_c_im', 'ssm_d', 'ssm_glu_w', 'branch_norm_g', 'w_out', 'norm_xa_g',
           'norm_mem_g', 'xa_wq', 'xa_wk', 'xa_wv', 'xa_q_norm_g', 'xa_k_norm_g', 'xa_wo', 'norm_ffn_g',
           'ffn_w_in', 'ffn_w_out']
SHARD_AXIS = {'w_in': 2, 'conv_dw_w': 2, 'conv_pw2_w': 1, 'ssm_glu_w': 2, 'w_out': 1, 'xa_wq': 1,
              'xa_wk': 1, 'xa_wv': 1, 'xa_wo': 1, 'ffn_w_in': 2, 'ffn_w_out': 1}
MATRICES = [n for n in WEIGHTS if n in SHARD_AXIS and n != 'conv_dw_w']
REPLICATED = [n for n in WEIGHTS if n not in SHARD_AXIS]
N_CHIPS = 4
N_DEV = 8


def _cparams(sem=None, **kw):
    if sem is not None:
        kw['dimension_semantics'] = sem
    return pltpu.CompilerParams(vmem_limit_bytes=VMEM_LIMIT, **kw)


def _pick(n, target):
    best = None
    d = LANES
    while d <= min(n, target):
        if n % d == 0:
            best = d
        d += LANES
    return best if best is not None else n


def _rows_for(n_rows, width):
    t = 512
    while t > 8 and t * width > 768 * 1024:
        t //= 2
    return min(t, n_rows)


def _dg(a, b, ca, cb):
    return lax.dot_general(a.astype(BF16), b.astype(BF16), (((ca,), (cb,)), ((), ())),
                           preferred_element_type=F32)


@jax.custom_vjp
def bdot_nn(a, b):
    return _dg(a, b, 1, 0)


def _bdot_nn_fwd(a, b):
    return _dg(a, b, 1, 0), (a, b)


def _bdot_nn_bwd(res, g):
    a, b = res
    return _dg(g, b, 1, 1), _dg(a, g, 0, 0)


bdot_nn.defvjp(_bdot_nn_fwd, _bdot_nn_bwd)


@jax.custom_vjp
def bdot_nt(a, b):
    return _dg(a, b, 1, 1)


def _bdot_nt_fwd(a, b):
    return _dg(a, b, 1, 1), (a, b)


def _bdot_nt_bwd(res, g):
    a, b = res
    return _dg(g, b, 1, 0), _dg(g, a, 0, 0)


bdot_nt.defvjp(_bdot_nt_fwd, _bdot_nt_bwd)


def mm(a, b, mode, name, out_dtype=F32, add=None, ride=None):
    if mode == 'nn':
        M, K = a.shape
        N = b.shape[1]
    elif mode == 'nt':
        M, K = a.shape
        N = b.shape[0]
    else:
        K, M = a.shape
        N = b.shape[1]
    if mode == 'tn':
        tm, tn, tk = _pick(M, 1536), _pick(N, 2816), _pick(K, 512)
    else:
        tm, tn = _pick(M, 1024), _pick(N, 1536)
        tk = K if K <= 2816 else _pick(K, 1536)
    nk = K // tk
    ca, cb = {'nn': (1, 0), 'nt': (1, 1), 'tn': (0, 0)}[mode]

    grid = (M // tm, N // tn, nk)

    def body(*refs):
        ins, (o_ref,), scratch, riding = _split_refs(ride, 3 if add is not None else 2, 1, refs)
        a_ref, b_ref = ins[:2]
        add_ref = ins[2] if add is not None else None
        i, j, k = pl.program_id(0), pl.program_id(1), pl.program_id(2)
        ride_done = None
        if riding is not None:
            first = jnp.logical_and(i == 0, jnp.logical_and(j == 0, k == 0))
            last = jnp.logical_and(i == grid[0] - 1, jnp.logical_and(j == grid[1] - 1, k == nk - 1))
            ride_done = ride.at_ends(riding, first, last)

        def finish(acc):
            if add_ref is not None:
                acc = acc + add_ref[...]
            o_ref[...] = acc.astype(out_dtype)

        if nk == 1:
            finish(_dg(a_ref[...], b_ref[...], ca, cb))
        else:
            acc_ref, = scratch

            @pl.when(k == 0)
            def _():
                acc_ref[...] = jnp.zeros_like(acc_ref)

            acc_ref[...] += _dg(a_ref[...], b_ref[...], ca, cb)

            @pl.when(k == nk - 1)
            def _():
                finish(acc_ref[...])

        if ride_done is not None:
            ride_done()

    if mode == 'nn':
        a_spec = pl.BlockSpec((tm, tk), lambda i, j, k: (i, k))
        b_spec = pl.BlockSpec((tk, tn), lambda i, j, k: (k, j))
    elif mode == 'nt':
        a_spec = pl.BlockSpec((tm, tk), lambda i, j, k: (i, k))
        b_spec = pl.BlockSpec((tn, tk), lambda i, j, k: (j, k))
    else:
        a_spec = pl.BlockSpec((tk, tm), lambda i, j, k: (k, i))
        b_spec = pl.BlockSpec((tk, tn), lambda i, j, k: (k, j))
    out_spec = pl.BlockSpec((tm, tn), lambda i, j, k: (i, j))
    own_in = [a_spec, b_spec] + ([out_spec] if add is not None else [])
    operands = (a, b) if add is None else (a, b, add)
    scratch = [pltpu.VMEM((tm, tn), F32)] if nk > 1 else []
    if ride is None:
        return pl.pallas_call(
            body, name=name, grid=grid, in_specs=own_in, out_specs=out_spec,
            out_shape=jax.ShapeDtypeStruct((M, N), out_dtype), scratch_shapes=scratch,
            compiler_params=_cparams(("parallel", "parallel", "arbitrary")),
        )(*operands)
    outs = pl.pallas_call(
        body, name=name, grid=grid, in_specs=own_in + ride.in_specs, out_specs=[out_spec] + ride.out_specs,
        out_shape=[jax.ShapeDtypeStruct((M, N), out_dtype)] + ride.out_shape,
        scratch_shapes=scratch + ride.scratch,
        compiler_params=_cparams(("arbitrary", "arbitrary", "arbitrary"), has_side_effects=True),
    )(*operands, *ride.arrays)
    return outs[0], outs[1:]


class Riders:
    def __init__(self, gather=(), scatter=(), own=None):
        self.gather_names = [n for n, _ in gather]
        self.shards = tuple(s for _, s in gather)
        self.scatter_names = [n for n, _ in scatter]
        self.slots = tuple(s for _, s in scatter)
        self.own = own

    def gather_ride(self, shards):
        return gather_ride(self.gather_names, list(shards)) if shards else None

    def scatter_ride(self, grads, own_grad=None):
        names, grads = list(self.scatter_names), list(grads)
        if self.own is not None:
            names.append(self.own)
            grads.append(own_grad)
        if not grads:
            return None
        return scatter_ride(names, [_cut_for_chips(n, g) for n, g in zip(names, grads)])

    def whole_slots(self, slots):
        return tuple(jnp.zeros(_whole_shape(n, s.shape[1:]), BF16) for n, s in zip(self.scatter_names, slots))


NO_RIDERS = Riders()


def linear(x, w, slot, name, residual=None, riders=NO_RIDERS):
    @jax.custom_vjp
    def op(x, w, slot, residual, shards, slots):
        ride = riders.gather_ride(shards)
        y = mm(x, w, 'nn', name + '_fwd', add=residual, ride=ride)
        y, gathered = y if ride is not None else (y, ())
        return y, tuple(gathered), riders.whole_slots(slots)

    def op_fwd(x, w, slot, residual, shards, slots):
        return op(x, w, slot, residual, shards, slots), (x, w, shards)

    def op_bwd(res, cts):
        x, w, shards = res
        g, _, slot_grads = cts
        ride = riders.scatter_ride(slot_grads)
        dx = mm(g, w, 'nt', name + '_dx', ride=ride)
        dx, received = dx if ride is not None else (dx, ())
        return (dx, jnp.zeros_like(w), mm(x, g, 'tn', name + '_dw', BF16), None if residual is None else g,
                tuple(jnp.zeros_like(s) for s in shards), tuple(received))

    op.defvjp(op_fwd, op_bwd)
    return op(x, w, slot, residual, riders.shards, riders.slots)


def _rowwise_calls(f, rows, params, consts, out_widths, name, need_row_grad=None, block_rows=None, carry=(),
                   out_dtype=F32):
    nr, npar, nc, nout = len(rows), len(params), len(consts), len(out_widths)
    carry = tuple(carry)
    L = rows[0].shape[0]
    widths = [r.shape[1] for r in rows]
    T = block_rows or _rows_for(L, max(widths + list(out_widths)))
    n = L // T
    need = list(need_row_grad) if need_row_grad is not None else [True] * nr
    pshapes = [p.shape for p in params]
    cshapes = [c.shape for c in consts]

    row_specs = [pl.BlockSpec((T, w), lambda i: (i, 0)) for w in widths]
    par_specs = [pl.BlockSpec(s, lambda i: (0, 0)) for s in pshapes]
    con_specs = [pl.BlockSpec(s, lambda i: (0, 0)) for s in cshapes]
    out_specs = [pl.BlockSpec((T, w), lambda i: (i, 0)) for w in out_widths]

    def fwd_call(rows, params, consts):
        def body(*refs):
            ins = [r[...] for r in refs[:nr + npar + nc]]
            outs = f(*ins)
            for o_ref, val in zip(refs[nr + npar + nc:], outs):
                o_ref[...] = val.astype(out_dtype)

        return pl.pallas_call(
            body, name=name + '_fwd', grid=(n,),
            in_specs=row_specs + par_specs + con_specs, out_specs=out_specs,
            out_shape=[jax.ShapeDtypeStruct((L, w), out_dtype) for w in out_widths],
            compiler_params=_cparams(("parallel",)),
        )(*rows, *params, *consts)

    def bwd_call(rows, params, consts, cts, carried):
        grad_rows = [k for k in range(nr) if need[k]]
        n_in = nr + npar + nc + nout

        def body(*refs):
            i = pl.program_id(0)
            rv = [r[...] for r in refs[:nr]]
            pv = [r[...] for r in refs[nr:nr + npar]]
            cv = [r[...] for r in refs[nr + npar:nr + npar + nc]]
            ctv = tuple(r[...] for r in refs[nr + npar + nc:n_in])
            carried_refs = dict(zip(carry, refs[n_in:n_in + len(carry)]))
            orefs = refs[n_in + len(carry):]
            _, vjp = jax.vjp(lambda *rp: tuple(f(*rp, *cv)), *rv, *pv)
            g = vjp(ctv)
            for slot, k in enumerate(grad_rows):
                orefs[slot][...] = g[k] + carried_refs[k][...] if k in carried_refs else g[k]

            @pl.when(i == 0)
            def _():
                for k in range(npar):
                    orefs[len(grad_rows) + k][...] = jnp.zeros(pshapes[k], F32)

            for k in range(npar):
                orefs[len(grad_rows) + k][...] += g[nr + k]

        outs = pl.pallas_call(
            body, name=name + '_bwd', grid=(n,),
            in_specs=row_specs + par_specs + con_specs + out_specs + [row_specs[k] for k in carry],
            out_specs=[row_specs[k] for k in grad_rows] + par_specs,
            out_shape=[jax.ShapeDtypeStruct((L, widths[k]), F32) for k in grad_rows]
            + [jax.ShapeDtypeStruct(s, F32) for s in pshapes],
            compiler_params=_cparams(("arbitrary",)),
        )(*rows, *params, *consts, *cts, *carried)
        drows = []
        slot = 0
        for k in range(nr):
            if need[k]:
                drows.append(outs[slot])
                slot += 1
            else:
                drows.append(jnp.zeros_like(rows[k]))
        return tuple(drows), tuple(outs[len(grad_rows):])

    return fwd_call, bwd_call


def rowwise(f, rows, params, consts, out_widths, name, need_row_grad=None, block_rows=None, carry=()):
    fwd_call, bwd_call = _rowwise_calls(f, rows, params, consts, out_widths, name, need_row_grad, block_rows, carry)
    nout = len(out_widths)

    @jax.custom_vjp
    def op(rows, params, consts):
        return tuple(fwd_call(rows, params, consts)) + tuple(rows[k] for k in carry)

    def op_fwd(rows, params, consts):
        return op(rows, params, consts), (rows, params, consts)

    def op_bwd(res, cts):
        rows, params, consts = res
        drows, dparams = bwd_call(rows, params, consts, cts[:nout], cts[nout:])
        return drows, dparams, tuple(jnp.zeros_like(c) for c in consts)

    op.defvjp(op_fwd, op_bwd)
    return op(tuple(rows), tuple(params), tuple(consts))


def rowwise_linear(f, rows, params, consts, w, slot, name, residual=None, riders=NO_RIDERS, carry=(),
                   block_rows=None):
    width = w.shape[0]
    fwd_call, bwd_call = _rowwise_calls(lambda *a: (f(*a),), rows, params, consts, [width], name, None,
                                        block_rows, carry, BF16)

    @jax.custom_vjp
    def op(rows, params, consts, w, slot, residual, shards, slots):
        h, = fwd_call(rows, params, consts)
        ride = riders.gather_ride(shards)
        y = mm(h, w, 'nn', name + '_mm', add=residual, ride=ride)
        y, gathered = y if ride is not None else (y, ())
        return y, tuple(rows[k] for k in carry), tuple(gathered), riders.whole_slots(slots)

    def op_fwd(rows, params, consts, w, slot, residual, shards, slots):
        h, = fwd_call(rows, params, consts)
        ride = riders.gather_ride(shards)
        y = mm(h, w, 'nn', name + '_mm', add=residual, ride=ride)
        y, gathered = y if ride is not None else (y, ())
        out = (y, tuple(rows[k] for k in carry), tuple(gathered), riders.whole_slots(slots))
        return out, (rows, params, consts, h, w, shards)

    def op_bwd(res, cts):
        rows, params, consts, h, w, shards = res
        g, carried, _, slot_grads = cts
        dw = mm(h, g, 'tn', name + '_dw', BF16)
        ride = riders.scatter_ride(slot_grads, dw)
        dh = mm(g, w, 'nt', name + '_dx', ride=ride)
        dh, received = dh if ride is not None else (dh, ())
        if riders.own is not None:
            received, dw = received[:-1], received[-1]
        drows, dparams = bwd_call(rows, params, consts, (dh,), carried)
        return (drows, dparams, tuple(jnp.zeros_like(c) for c in consts), jnp.zeros_like(w), dw,
                None if residual is None else g, tuple(jnp.zeros_like(s) for s in shards), tuple(received))

    op.defvjp(op_fwd, op_bwd)
    return op(tuple(rows), tuple(params), tuple(consts), w, slot, residual, riders.shards, riders.slots)


def _rms(x, g):
    return x * lax.rsqrt(jnp.mean(x * x, axis=-1, keepdims=True) + EPS) * g


def rmsnorm(x, g, name, carry=False):
    out = rowwise(lambda x, g: (_rms(x, g),), [x], [g.reshape(1, -1)], [], [x.shape[1]], name,
                  carry=(0,) if carry else ())
    return out if carry else out[0]


def _split2(x):
    hi = x.astype(BF16)
    return hi, (x - hi.astype(F32)).astype(BF16)


@jax.custom_vjp
def select_mm(x, sel):
    return sum(_dg(t, sel, 1, 0) for t in _split2(x))


def _select_mm_fwd(x, sel):
    return select_mm(x, sel), sel


def _select_mm_bwd(sel, g):
    return sum(_dg(t, sel, 1, 1) for t in _split2(g)), jnp.zeros_like(sel)


select_mm.defvjp(_select_mm_fwd, _select_mm_bwd)


def groupnorm(x, g, group, name):
    width = x.shape[1]
    g_full = jnp.tile(g.reshape(1, group), (1, width // group))
    if group % LANES == 0:
        def f(x, g_full):
            outs = []
            for lo in range(0, width, group):
                xs = x[:, lo:lo + group]
                outs.append(_rms(xs, g_full[:, lo:lo + group]))
            return (jnp.concatenate(outs, axis=-1),)

        return rowwise(f, [x], [g_full], [], [width], name)[0]

    gid = jnp.arange(width) // group
    sel = (gid[:, None] == jnp.arange(LANES)[None, :]).astype(BF16)

    def f(x, g_full, sel, sel_t):
        ms = select_mm(x * x, sel) * (1.0 / group)
        inv = select_mm(lax.rsqrt(ms + EPS), sel_t)
        return (x * inv * g_full,)

    return rowwise(f, [x], [g_full], [sel, sel.T], [width], name)[0]


def glu(x, name):
    half = x.shape[1] // 2

    def f(x):
        return (x[:, :half] * jax.nn.sigmoid(x[:, half:]),)

    return rowwise(f, [x], [], [], [half], name)[0]


def swiglu_block(x):
    half = x.shape[1] // 2
    gate = x[:, :half]
    return gate * jax.nn.sigmoid(gate) * x[:, half:]


def ln_silu_block(x, g, b):
    mu = jnp.mean(x, axis=-1, keepdims=True)
    xc = x - mu
    var = jnp.mean(xc * xc, axis=-1, keepdims=True)
    y = xc * lax.rsqrt(var + EPS) * g + b
    return y * jax.nn.sigmoid(y)


def branch_norms_block(a, b, c, g):
    w1, w2 = a.shape[1], b.shape[1]
    return jnp.concatenate([_rms(a, g[:, :w1]), _rms(b, g[:, w1:w1 + w2]), _rms(c, g[:, w1 + w2:])], axis=-1)


def xa_core_block(q, k, v):
    scale = XA_HEAD_DIM ** -0.5
    outs = []
    for h in range(XA_HEADS):
        sl = slice(h * XA_HEAD_DIM, (h + 1) * XA_HEAD_DIM)
        s = bdot_nt(q[:, sl], k[:, sl]) * scale
        m = lax.stop_gradient(jnp.max(s, axis=-1, keepdims=True))
        e = jnp.exp(s - m)
        p = e / jnp.sum(e, axis=-1, keepdims=True)
        outs.append(bdot_nn(p, v[:, sl]))
    return jnp.concatenate(outs, axis=-1)


def loss_rows(y, target, name):
    def f(y, t):
        d = y - t
        return (0.5 * jnp.mean(d * d, axis=-1, keepdims=True),)

    return rowwise(f, [y, target], [], [], [1], name, need_row_grad=[True, False])[0]


def _swiglu(gate, up):
    return gate * jax.nn.sigmoid(gate) * up


def mm_swiglu(h, w, name, ride=None):
    M, K = h.shape
    H = w.shape[1] // 2
    tm, tn = _pick(M, 1024), _pick(H, 1536)
    nj = H // tn
    grid = (M // tm, 2 * nj)

    def body(*refs):
        (a_ref, b_ref), (gu_ref, act_ref), (kept,), riding = _split_refs(ride, 2, 2, refs)
        ride_done = _ride_ends(ride, riding, grid)
        j = pl.program_id(1)
        prod = _dg(a_ref[...], b_ref[...], 1, 0)
        gu_ref[...] = prod

        @pl.when(j < nj)
        def _():
            kept[j] = prod

        @pl.when(j >= nj)
        def _():
            act_ref[...] = _swiglu(kept[j - nj], prod).astype(BF16)

        ride_done()

    return _ride_call(
        body, name, grid,
        [pl.BlockSpec((tm, K), lambda i, j: (i, 0)), pl.BlockSpec((K, tn), lambda i, j: (0, j))],
        [pl.BlockSpec((tm, tn), lambda i, j: (i, j)),
         pl.BlockSpec((tm, tn), lambda i, j: (i, jnp.maximum(j - nj, 0)))],
        [jax.ShapeDtypeStruct((M, 2 * H), F32), jax.ShapeDtypeStruct((M, H), BF16)],
        (h, w), ride, ("parallel", "arbitrary"), scratch=[pltpu.VMEM((nj, tm, tn), F32)])


def mm_swiglu_bwd(dy, w_out, gu, name):
    M, D = dy.shape
    H = gu.shape[1] // 2
    tm, tn = _pick(M, 512), _pick(H, 1536)
    nj = H // tn

    def body(dy_ref, w_ref, gate_ref, up_ref, o_ref, kept):
        j = pl.program_id(1)

        @pl.when(j < nj)
        def _():
            dact = _dg(dy_ref[...], w_ref[...], 1, 1)
            _, vjp = jax.vjp(_swiglu, gate_ref[...], up_ref[...])
            dgate, dup = vjp(dact)
            o_ref[...] = dgate.astype(BF16)
            kept[j] = dup.astype(BF16)

        @pl.when(j >= nj)
        def _():
            o_ref[...] = kept[j - nj]

    def tile(j):
        return jnp.minimum(j, nj - 1)

    return pl.pallas_call(
        body, name=name, grid=(M // tm, 2 * nj),
        in_specs=[pl.BlockSpec((tm, D), lambda i, j: (i, 0)),
                  pl.BlockSpec((tn, D), lambda i, j: (tile(j), 0)),
                  pl.BlockSpec((tm, tn), lambda i, j: (i, tile(j))),
                  pl.BlockSpec((tm, tn), lambda i, j: (i, nj + tile(j)))],
        out_specs=pl.BlockSpec((tm, tn), lambda i, j: (i, j)),
        out_shape=jax.ShapeDtypeStruct((M, 2 * H), BF16),
        scratch_shapes=[pltpu.VMEM((nj, tm, tn), BF16)],
        compiler_params=_cparams(("parallel", "arbitrary")),
    )(dy, w_out, gu, gu)


def ffn(x, gain, w_in, slot_in, w_out, slot_out, name, riders=NO_RIDERS):
    norm_fwd, norm_bwd = _rowwise_calls(lambda x, g: (_rms(x, g),), [x], [gain], [], [x.shape[1]],
                                        name + '_norm', None, None, (0,), BF16)

    def forward(x, gain, w_in, w_out, shards):
        h, = norm_fwd((x,), (gain,), ())
        (gu, act), gathered = mm_swiglu(h, w_in, name + '_in', riders.gather_ride(shards))
        return mm(act, w_out, 'nn', name + '_out', add=x), tuple(gathered), (h, gu, act)

    @jax.custom_vjp
    def op(x, gain, w_in, slot_in, w_out, slot_out, shards, slots):
        y, gathered, _ = forward(x, gain, w_in, w_out, shards)
        return y, gathered, riders.whole_slots(slots)

    def op_fwd(x, gain, w_in, slot_in, w_out, slot_out, shards, slots):
        y, gathered, (h, gu, act) = forward(x, gain, w_in, w_out, shards)
        return (y, gathered, riders.whole_slots(slots)), (x, gain, h, gu, act, w_in, w_out, shards)

    def op_bwd(res, cts):
        x, gain, h, gu, act, w_in, w_out, shards = res
        g, _, slot_grads = cts
        dgu = mm_swiglu_bwd(g, w_out, gu, name + '_dact')
        dw_out = mm(act, g, 'tn', name + '_out_dw', BF16)
        dw_in = mm(h, dgu, 'tn', name + '_in_dw', BF16)
        ride = riders.scatter_ride(slot_grads)
        dh = mm(dgu, w_in, 'nt', name + '_in_dx', ride=ride)
        dh, received = dh if ride is not None else (dh, ())
        (dx,), (dgain,) = norm_bwd((x,), (gain,), (), (dh,), (g,))
        return (dx, dgain, jnp.zeros_like(w_in), dw_in, jnp.zeros_like(w_out), dw_out,
                tuple(jnp.zeros_like(s) for s in shards), tuple(received))

    op.defvjp(op_fwd, op_bwd)
    return op(x, gain, w_in, slot_in, w_out, slot_out, riders.shards, riders.slots)


def _hilo(x, ones_bf16):
    hi = x.astype(BF16)
    lo = (x - hi.astype(F32)).astype(BF16)
    return _dg(hi, ones_bf16, 1, 0) + _dg(lo, ones_bf16, 1, 0)


def _sb_block(qh, kb, c, valid, strict_upper):
    z = _dg(qh, kb, 1, 1)
    a = jnp.minimum(z, 0.0) - jnp.log(1.0 + jnp.exp(-jnp.abs(z)))
    b = jnp.where(valid, a - z, 0.0)
    s = _hilo(b, strict_upper) + c
    w = jnp.where(valid, jnp.exp(a + s), 0.0)
    return a, b, w


def _sb_masks(T):
    row = lax.broadcasted_iota(jnp.int32, (T, T), 0)
    col = lax.broadcasted_iota(jnp.int32, (T, T), 1)
    return col < row, (row > col).astype(BF16), (row >= col).astype(BF16)


def _sb_key_blocks(i, j, T, causal):
    second = jnp.maximum(j - 1, 0)
    return [(pl.multiple_of(j * T, T), jnp.logical_or(causal, j != i)),
            (pl.multiple_of(second * T, T), jnp.logical_and(jnp.logical_or(causal, True), j >= 1))]


HEADS_PER_BLOCK = LANES // SB_HEAD_DIM


def _head_mask(h):
    lane = lax.broadcasted_iota(jnp.int32, (1, LANES), 1)
    return (lane // SB_HEAD_DIM == h).astype(F32)


def _max_all(columns):
    m = columns[0]
    for c in columns[1:]:
        m = jnp.maximum(m, c)
    return jnp.max(m)


def _ride_call(body, name, grid, in_specs, out_specs, out_shape, operands, ride, semantics, scratch=()):
    if ride is None:
        outs = pl.pallas_call(body, name=name, grid=grid, in_specs=in_specs, out_specs=out_specs,
                              out_shape=out_shape, scratch_shapes=list(scratch),
                              compiler_params=_cparams(semantics))(*operands)
        return outs, ()
    outs = pl.pallas_call(
        body, name=name, grid=grid, in_specs=in_specs + ride.in_specs, out_specs=out_specs + ride.out_specs,
        out_shape=out_shape + ride.out_shape, scratch_shapes=list(scratch) + ride.scratch,
        compiler_params=_cparams(("arbitrary",) * len(grid), has_side_effects=True),
    )(*operands, *ride.arrays)
    return outs[:len(out_shape)], outs[len(out_shape):]


def _ride_ends(ride, riding, grid):
    if riding is None:
        return lambda: None
    first, last = None, None
    for axis, size in enumerate(grid):
        at0, at1 = pl.program_id(axis) == 0, pl.program_id(axis) == size - 1
        first = at0 if first is None else jnp.logical_and(first, at0)
        last = at1 if last is None else jnp.logical_and(last, at1)
    return ride.at_ends(riding, first, last)


def _sb_fwd_call(q, k, v, T, name, ride=None):
    L, W = q.shape
    scale = SB_HEAD_DIM ** -0.5
    grid = (W // LANES, L // T)

    def body(*refs):
        (q_ref, k_ref, v_ref), (o_ref,), _, riding = _split_refs(ride, 3, 1, refs)
        ride_done = _ride_ends(ride, riding, grid)
        i = pl.program_id(1)
        causal, strict_upper, _ = _sb_masks(T)
        q2 = q_ref[...] * scale
        masks = [_head_mask(h) for h in range(HEADS_PER_BLOCK)]
        qs = [(q2 * hm).astype(BF16) for hm in masks]
        zero = jnp.zeros((T, 1), F32)

        def cond(state):
            j, cs, _ = state
            return jnp.logical_and(j >= 0, _max_all(cs) > -SB_CUT)

        def step(state):
            j, cs, acc = state
            blocks = _sb_key_blocks(i, j, T, causal)
            ks = [k_ref[pl.ds(r0, T), :].astype(BF16) for r0, _ in blocks]
            vs = [v_ref[pl.ds(r0, T), :] for r0, _ in blocks]
            new_cs = []
            for hm, qh, c in zip(masks, qs, cs):
                for (_, valid), kb, vb in zip(blocks, ks, vs):
                    _, b, w = _sb_block(qh, kb, c, valid, strict_upper)
                    vh = (vb * hm).astype(BF16)
                    w_hi = w.astype(BF16)
                    w_lo = (w - w_hi.astype(F32)).astype(BF16)
                    acc = acc + _dg(w_hi, vh, 1, 0) + _dg(w_lo, vh, 1, 0)
                    c = c + jnp.sum(b, axis=1, keepdims=True)
                new_cs.append(c)
            return j - len(blocks), tuple(new_cs), acc

        _, _, acc = lax.while_loop(cond, step, (i, (zero,) * HEADS_PER_BLOCK, jnp.zeros((T, LANES), F32)))
        o_ref[...] = acc
        ride_done()

    (o,), rode = _ride_call(
        body, name, grid,
        [pl.BlockSpec((T, LANES), lambda p, i: (i, p)),
         pl.BlockSpec((L, LANES), lambda p, i: (0, p)),
         pl.BlockSpec((L, LANES), lambda p, i: (0, p))],
        [pl.BlockSpec((T, LANES), lambda p, i: (i, p))], [jax.ShapeDtypeStruct((L, W), F32)],
        (q, k, v), ride, ("parallel", "parallel"))
    return o, rode


def _sb_bwd_call(q, k, v, o, do, T, name, ride=None):
    L, W = q.shape
    scale = SB_HEAD_DIM ** -0.5
    grid = (W // LANES, L // T)

    def body(*refs):
        (q_ref, k_ref, v_ref, o_ref, do_ref), (dq_ref, dk_ref, dv_ref), _, riding = _split_refs(ride, 5, 3, refs)
        ride_done = _ride_ends(ride, riding, grid)
        i = pl.program_id(1)

        @pl.when(i == 0)
        def _():
            dk_ref[...] = jnp.zeros_like(dk_ref)
            dv_ref[...] = jnp.zeros_like(dv_ref)

        causal, strict_upper, upper = _sb_masks(T)
        q2 = q_ref[...] * scale
        do2 = do_ref[...]
        o2 = o_ref[...]
        masks = [_head_mask(h) for h in range(HEADS_PER_BLOCK)]
        qs = [(q2 * hm).astype(BF16) for hm in masks]
        dos = [(do2 * hm).astype(BF16) for hm in masks]
        totals = [jnp.sum(doh.astype(F32) * o2, axis=1, keepdims=True) for doh in dos]
        zero = jnp.zeros((T, 1), F32)

        def cond(state):
            j, cs, _, _ = state
            return jnp.logical_and(j >= 0, _max_all(cs) > -SB_CUT)

        def step(state):
            j, cs, rs, dq = state
            blocks = _sb_key_blocks(i, j, T, causal)
            kfs = [k_ref[pl.ds(r0, T), :] for r0, _ in blocks]
            ks = [kf.astype(BF16) for kf in kfs]
            vs = [v_ref[pl.ds(r0, T), :].astype(BF16) for r0, _ in blocks]
            dks = [jnp.zeros((T, LANES), F32) for _ in blocks]
            dvs = [jnp.zeros((T, LANES), F32) for _ in blocks]
            new_cs, new_rs = [], []
            for hm, qh, doh, total, c, r in zip(masks, qs, dos, totals, cs, rs):
                for n, ((_, valid), kf, kb, vb) in enumerate(zip(blocks, kfs, ks, vs)):
                    a, b, w = _sb_block(qh, kb, c, valid, strict_upper)
                    e = _dg(doh, vb, 1, 1) * w
                    before = total - (_hilo(e, upper) + r)
                    dz = jnp.where(valid, e * jnp.exp(b) - before * jnp.exp(a), 0.0).astype(BF16)
                    dq = dq + _dg(dz, kf * hm, 1, 0)
                    dks[n] = dks[n] + _dg(dz, qh, 0, 0)
                    dvs[n] = dvs[n] + _dg(w, doh, 0, 0)
                    c = c + jnp.sum(b, axis=1, keepdims=True)
                    r = r + jnp.sum(e, axis=1, keepdims=True)
                new_cs.append(c)
                new_rs.append(r)
            for (r0, _), dk, dv in zip(blocks, dks, dvs):
                dk_ref[pl.ds(r0, T), :] += dk
                dv_ref[pl.ds(r0, T), :] += dv
            return j - len(blocks), tuple(new_cs), tuple(new_rs), dq

        init = (i, (zero,) * HEADS_PER_BLOCK, (zero,) * HEADS_PER_BLOCK, jnp.zeros((T, LANES), F32))
        dq = lax.while_loop(cond, step, init)[3]
        dq_ref[...] = dq * scale
        ride_done()

    blk = pl.BlockSpec((T, LANES), lambda p, i: (i, p))
    full = pl.BlockSpec((L, LANES), lambda p, i: (0, p))
    return _ride_call(body, name, grid, [blk, full, full, blk, blk], [blk, full, full],
                      [jax.ShapeDtypeStruct((L, W), F32)] * 3, (q, k, v, o, do), ride, ("parallel", "arbitrary"))


def sb_attention(q, k, v, name, riders=NO_RIDERS):
    T = min(256, q.shape[0])

    @jax.custom_vjp
    def op(q, k, v, shards, slots):
        o, gathered = _sb_fwd_call(q, k, v, T, name + '_fwd', riders.gather_ride(shards))
        return o, tuple(gathered), riders.whole_slots(slots)

    def op_fwd(q, k, v, shards, slots):
        out = op(q, k, v, shards, slots)
        return out, (q, k, v, out[0], shards)

    def op_bwd(res, cts):
        q, k, v, o, shards = res
        do, _, slot_grads = cts
        grads, received = _sb_bwd_call(q, k, v, o, do, T, name + '_bwd', riders.scatter_ride(slot_grads))
        return (*grads, tuple(jnp.zeros_like(s) for s in shards), tuple(received))

    op.defvjp(op_fwd, op_bwd)
    return op(q, k, v, riders.shards, riders.slots)


def _dwconv_fwd_call(x, w, b, T, name):
    L, C = x.shape
    per = T // CONV_HALO
    lead = CONV_HALO - (CONV_WIDTH - 1)

    def body(x_ref, halo_ref, w_ref, b_ref, o_ref, buf):
        i = pl.program_id(0)
        buf[0:CONV_HALO, :] = jnp.where(i > 0, halo_ref[...], 0.0)
        buf[CONV_HALO:CONV_HALO + T, :] = x_ref[...]
        acc = jnp.zeros((T, C), F32) + b_ref[...]
        for j in range(CONV_WIDTH):
            acc = acc + w_ref[j:j + 1, :] * buf[lead + j:lead + j + T, :]
        o_ref[...] = acc

    return pl.pallas_call(
        body, name=name, grid=(L // T,),
        in_specs=[pl.BlockSpec((T, C), lambda i: (i, 0)),
                  pl.BlockSpec((CONV_HALO, C), lambda i: (jnp.maximum(i * per - 1, 0), 0)),
                  pl.BlockSpec(w.shape, lambda i: (0, 0)),
                  pl.BlockSpec(b.shape, lambda i: (0, 0))],
        out_specs=pl.BlockSpec((T, C), lambda i: (i, 0)),
        out_shape=jax.ShapeDtypeStruct((L, C), F32),
        scratch_shapes=[pltpu.VMEM((T + CONV_HALO, C), F32)],
        compiler_params=_cparams(("parallel",)),
    )(x, x, w, b)


def _dwconv_bwd_call(x, w, g, T, name):
    L, C = x.shape
    per = T // CONV_HALO
    n = L // T
    last_halo = L // CONV_HALO - 1
    lead = CONV_HALO - (CONV_WIDTH - 1)

    def body(x_ref, xh_ref, g_ref, gh_ref, w_ref, dx_ref, dw_ref, db_ref, bufx, bufg):
        i = pl.program_id(0)
        bufx[0:CONV_HALO, :] = jnp.where(i > 0, xh_ref[...], 0.0)
        bufx[CONV_HALO:CONV_HALO + T, :] = x_ref[...]
        gm = g_ref[...]
        bufg[0:T, :] = gm
        bufg[T:T + CONV_HALO, :] = jnp.where(i < n - 1, gh_ref[...], 0.0)
        acc = jnp.zeros((T, C), F32)
        for j in range(CONV_WIDTH):
            off = CONV_WIDTH - 1 - j
            acc = acc + w_ref[j:j + 1, :] * bufg[off:off + T, :]
        dx_ref[...] = acc

        @pl.when(i == 0)
        def _():
            dw_ref[...] = jnp.zeros_like(dw_ref)
            db_ref[...] = jnp.zeros_like(db_ref)

        for j in range(CONV_WIDTH):
            dw_ref[j:j + 1, :] += jnp.sum(gm * bufx[lead + j:lead + j + T, :], axis=0, keepdims=True)
        db_ref[...] += jnp.sum(gm, axis=0, keepdims=True)

    return pl.pallas_call(
        body, name=name, grid=(n,),
        in_specs=[pl.BlockSpec((T, C), lambda i: (i, 0)),
                  pl.BlockSpec((CONV_HALO, C), lambda i: (jnp.maximum(i * per - 1, 0), 0)),
                  pl.BlockSpec((T, C), lambda i: (i, 0)),
                  pl.BlockSpec((CONV_HALO, C), lambda i: (jnp.minimum((i + 1) * per, last_halo), 0)),
                  pl.BlockSpec(w.shape, lambda i: (0, 0))],
        out_specs=[pl.BlockSpec((T, C), lambda i: (i, 0)),
                   pl.BlockSpec(w.shape, lambda i: (0, 0)),
                   pl.BlockSpec((1, C), lambda i: (0, 0))],
        out_shape=[jax.ShapeDtypeStruct((L, C), F32), jax.ShapeDtypeStruct(w.shape, F32),
                   jax.ShapeDtypeStruct((1, C), F32)],
        scratch_shapes=[pltpu.VMEM((T + CONV_HALO, C), F32), pltpu.VMEM((T + CONV_HALO, C), F32)],
        compiler_params=_cparams(("arbitrary",)),
    )(x, x, g, g, w)


def dwconv(x, w, b, name):
    T = min(512, x.shape[0])

    @jax.custom_vjp
    def op(x, w, b):
        return _dwconv_fwd_call(x, w, b, T, name + '_fwd')

    def op_fwd(x, w, b):
        return _dwconv_fwd_call(x, w, b, T, name + '_fwd'), (x, w)

    def op_bwd(res, g):
        x, w = res
        return tuple(_dwconv_bwd_call(x, w, g, T, name + '_bwd'))

    op.defvjp(op_fwd, op_bwd)
    return op(x, w, b)


def _ssm_fwd_call(u, ar, ai, bbr, bbi, cr, ci, d, T, name):
    L, C = u.shape
    S = SSM_LANES

    def body(u_ref, ar_ref, ai_ref, bbr_ref, bbi_ref, cr_ref, ci_ref, d_ref,
             y_ref, xr_ref, xi_ref, st_r, st_i, in_r, in_i, out_r, out_i):
        i = pl.program_id(0)

        @pl.when(i == 0)
        def _():
            st_r[...] = jnp.zeros_like(st_r)
            st_i[...] = jnp.zeros_like(st_i)

        u_blk = u_ref[...]
        xr_ref[...] = _dg(u_blk, bbr_ref[...], 1, 0)
        xi_ref[...] = _dg(u_blk, bbi_ref[...], 1, 0)
        a_r, a_i = ar_ref[...], ai_ref[...]

        def tile(t, carry):
            sr, si = carry
            r0 = pl.multiple_of(t * SUBLANES, SUBLANES)
            in_r[...] = xr_ref[pl.ds(r0, SUBLANES), :]
            in_i[...] = xi_ref[pl.ds(r0, SUBLANES), :]
            for r in range(SUBLANES):
                nr = a_r * sr - a_i * si + in_r[r:r + 1, :]
                ni = a_r * si + a_i * sr + in_i[r:r + 1, :]
                sr, si = nr, ni
                out_r[r:r + 1, :] = sr
                out_i[r:r + 1, :] = si
            xr_ref[pl.ds(r0, SUBLANES), :] = out_r[...]
            xi_ref[pl.ds(r0, SUBLANES), :] = out_i[...]
            return sr, si

        sr, si = lax.fori_loop(0, T // SUBLANES, tile, (st_r[0:1, :], st_i[0:1, :]))
        st_r[0:1, :] = sr
        st_i[0:1, :] = si
        y_ref[...] = (_dg(xr_ref[...], cr_ref[...], 1, 0) - _dg(xi_ref[...], ci_ref[...], 1, 0)
                      + d_ref[...] * u_blk)

    full = lambda a: pl.BlockSpec(a.shape, lambda i: (0, 0))
    return pl.pallas_call(
        body, name=name, grid=(L // T,),
        in_specs=[pl.BlockSpec((T, C), lambda i: (i, 0))] + [full(a) for a in (ar, ai, bbr, bbi, cr, ci, d)],
        out_specs=[pl.BlockSpec((T, C), lambda i: (i, 0)), pl.BlockSpec((T, S), lambda i: (i, 0)),
                   pl.BlockSpec((T, S), lambda i: (i, 0))],
        out_shape=[jax.ShapeDtypeStruct((L, C), F32), jax.ShapeDtypeStruct((L, S), F32),
                   jax.ShapeDtypeStruct((L, S), F32)],
        scratch_shapes=[pltpu.VMEM((SUBLANES, S), F32)] * 6,
        compiler_params=_cparams(("arbitrary",)),
    )(u, ar, ai, bbr, bbi, cr, ci, d)


def _ssm_bwd_call(u, xr, xi, dy, ar, ai, bbr, bbi, cr, ci, d, T, name):
    L, C = u.shape
    S = SSM_LANES
    n = L // T
    per = T // SUBLANES

    def body(u_ref, xr_ref, xi_ref, hr_ref, hi_ref, dy_ref, ar_ref, ai_ref, bbr_ref, bbi_ref, cr_ref, ci_ref,
             d_ref, du_ref, dar_ref, dai_ref, dbr_ref, dbi_ref, dcr_ref, dci_ref, dd_ref,
             lam_r, lam_i, prev_r, prev_i, st_r, st_i, in_r, in_i, out_r, out_i):
        i = pl.program_id(0)
        chunk = n - 1 - i

        @pl.when(i == 0)
        def _():
            st_r[...] = jnp.zeros_like(st_r)
            st_i[...] = jnp.zeros_like(st_i)
            for ref in (dar_ref, dai_ref, dbr_ref, dbi_ref, dcr_ref, dci_ref, dd_ref):
                ref[...] = jnp.zeros_like(ref)

        dy_blk = dy_ref[...]
        u_blk = u_ref[...]
        lam_r[...] = _dg(dy_blk, cr_ref[...], 1, 1)
        lam_i[...] = -_dg(dy_blk, ci_ref[...], 1, 1)
        dcr_ref[...] += _dg(xr_ref[...], dy_blk, 0, 0)
        dci_ref[...] -= _dg(xi_ref[...], dy_blk, 0, 0)
        a_r, a_i = ar_ref[...], ai_ref[...]

        def tile(k, carry):
            lr, li = carry
            r0 = pl.multiple_of((per - 1 - k) * SUBLANES, SUBLANES)
            in_r[...] = lam_r[pl.ds(r0, SUBLANES), :]
            in_i[...] = lam_i[pl.ds(r0, SUBLANES), :]
            for r in range(SUBLANES - 1, -1, -1):
                nr = in_r[r:r + 1, :] + a_r * lr + a_i * li
                ni = in_i[r:r + 1, :] + a_r * li - a_i * lr
                lr, li = nr, ni
                out_r[r:r + 1, :] = lr
                out_i[r:r + 1, :] = li
            lam_r[pl.ds(r0, SUBLANES), :] = out_r[...]
            lam_i[pl.ds(r0, SUBLANES), :] = out_i[...]
            return lr, li

        lr, li = lax.fori_loop(0, per, tile, (st_r[0:1, :], st_i[0:1, :]))
        st_r[0:1, :] = lr
        st_i[0:1, :] = li

        l_r, l_i = lam_r[...], lam_i[...]
        du_ref[...] = _dg(l_r, bbr_ref[...], 1, 1) + _dg(l_i, bbi_ref[...], 1, 1) + d_ref[...] * dy_blk
        dbr_ref[...] += _dg(u_blk, l_r, 0, 0)
        dbi_ref[...] += _dg(u_blk, l_i, 0, 0)
        dd_ref[...] += jnp.sum(dy_blk * u_blk, axis=0, keepdims=True)

        prev_r[0:SUBLANES, :] = jnp.where(chunk > 0, hr_ref[...], 0.0)
        prev_i[0:SUBLANES, :] = jnp.where(chunk > 0, hi_ref[...], 0.0)
        prev_r[SUBLANES:SUBLANES + T, :] = xr_ref[...]
        prev_i[SUBLANES:SUBLANES + T, :] = xi_ref[...]
        p_r = prev_r[SUBLANES - 1:SUBLANES - 1 + T, :]
        p_i = prev_i[SUBLANES - 1:SUBLANES - 1 + T, :]
        dar_ref[...] += jnp.sum(l_r * p_r + l_i * p_i, axis=0, keepdims=True)
        dai_ref[...] += jnp.sum(l_i * p_r - l_r * p_i, axis=0, keepdims=True)

    rev = lambda w: pl.BlockSpec((T, w), lambda i: (n - 1 - i, 0))
    halo = pl.BlockSpec((SUBLANES, S), lambda i: (jnp.maximum((n - 1 - i) * per - 1, 0), 0))
    full = lambda a: pl.BlockSpec(a.shape, lambda i: (0, 0))
    params = (ar, ai, bbr, bbi, cr, ci, d)
    return pl.pallas_call(
        body, name=name, grid=(n,),
        in_specs=[rev(C), rev(S), rev(S), halo, halo, rev(C)] + [full(a) for a in params],
        out_specs=[rev(C)] + [full(a) for a in params],
        out_shape=[jax.ShapeDtypeStruct((L, C), F32)] + [jax.ShapeDtypeStruct(a.shape, F32) for a in params],
        scratch_shapes=[pltpu.VMEM((T, S), F32), pltpu.VMEM((T, S), F32),
                        pltpu.VMEM((T + SUBLANES, S), F32), pltpu.VMEM((T + SUBLANES, S), F32)]
        + [pltpu.VMEM((SUBLANES, S), F32)] * 6,
        compiler_params=_cparams(("arbitrary",)),
    )(u, xr, xi, xr, xi, dy, *params)


def ssm_core(u, ar, ai, bbr, bbi, cr, ci, d, name):
    T = min(256, u.shape[0])

    @jax.custom_vjp
    def op(u, ar, ai, bbr, bbi, cr, ci, d):
        return _ssm_fwd_call(u, ar, ai, bbr, bbi, cr, ci, d, T, name + '_fwd')[0]

    def op_fwd(u, ar, ai, bbr, bbi, cr, ci, d):
        y, xr, xi = _ssm_fwd_call(u, ar, ai, bbr, bbi, cr, ci, d, T, name + '_fwd')
        return y, (u, xr, xi, ar, ai, bbr, bbi, cr, ci, d)

    def op_bwd(res, dy):
        u, xr, xi, ar, ai, bbr, bbi, cr, ci, d = res
        return tuple(_ssm_bwd_call(u, xr, xi, dy, ar, ai, bbr, bbi, cr, ci, d, T, name + '_bwd'))

    op.defvjp(op_fwd, op_bwd)
    return op(u, ar, ai, bbr, bbi, cr, ci, d)


@jax.custom_vjp
def _block_diag(blocks):
    G, R, Cc = blocks.shape
    eye = jnp.eye(G, dtype=blocks.dtype)
    return (blocks[:, :, None, :] * eye[:, None, :, None]).reshape(G * R, G * Cc)


def _block_diag_fwd(blocks):
    return _block_diag(blocks), blocks.shape


def _block_diag_bwd(shape, g):
    G, R, Cc = shape
    on_diagonal = jnp.eye(G, dtype=bool)[:, None, :, None]
    return (jnp.sum(jnp.where(on_diagonal, g.reshape(G, R, G, Cc), 0.0), axis=2),)


_block_diag.defvjp(_block_diag_fwd, _block_diag_bwd)


def ssm_discretise(lam_re, lam_im, log_dt, b_re, b_im, c_re, c_im):
    dt = jnp.exp(log_dt)[:, None]
    mag = jnp.exp(lam_re * dt)
    ar, ai = mag * jnp.cos(lam_im * dt), mag * jnp.sin(lam_im * dt)
    den = lam_re * lam_re + lam_im * lam_im
    fr = ((ar - 1.0) * lam_re + ai * lam_im) / den
    fi = (ai * lam_re - (ar - 1.0) * lam_im) / den
    bbr = fr[..., None] * b_re - fi[..., None] * b_im
    bbi = fr[..., None] * b_im + fi[..., None] * b_re
    return (ar.reshape(1, SSM_LANES), ai.reshape(1, SSM_LANES),
            _block_diag(bbr.transpose(0, 2, 1)), _block_diag(bbi.transpose(0, 2, 1)),
            _block_diag(c_re.transpose(0, 2, 1)), _block_diag(c_im.transpose(0, 2, 1)))


def split_columns(p, bounds):
    @jax.custom_vjp
    def op(p):
        return tuple(p[:, lo:hi] for lo, hi in zip(bounds[:-1], bounds[1:]))

    def op_fwd(p):
        return op(p), None

    def op_bwd(_, gs):
        return (jnp.concatenate(gs, axis=1),)

    op.defvjp(op_fwd, op_bwd)
    return op(p)


DEPTH = 2
EARLY = ['w_in', 'conv_pw2_w', 'ssm_glu_w']
LATE = [n for n in MATRICES if n not in EARLY]
FFN = ['ffn_w_in', 'ffn_w_out']
AFTER_XA_Q = ['xa_wk', 'xa_wv', 'xa_wo']
GATHER_AT = {
    'start': [('w_in', 0)],
    'win0': [('ffn_w_out', 0)],
    'sb0': [(n, 0) for n in MATRICES if n not in ('w_in', 'ffn_w_out')],
    'ffn0': [(n, 1) for n in MATRICES if n not in FFN + AFTER_XA_Q],
    'sb1': [(n, 1) for n in FFN + AFTER_XA_Q],
}
SCATTER_AT = {
    'sb1': [(n, 1) for n in LATE],
    'ffn0': [(n, 1) for n in EARLY],
    'sb0': [(n, 0) for n in LATE],
    'win0': [(n, 0) for n in EARLY if n != 'w_in'],
}
OWN_AT = {'win0': ('w_in', 0)}


def _assemble(name, gathered):
    if name not in SLOTTED:
        return gathered
    return jnp.concatenate([gathered[j] for j in range(N_CHIPS)], axis=SHARD_AXIS[name] - 1)


def local_loss(slots, w, mats, shards, x, mem, target):
    mats = dict(mats)
    slot = {key: slots[key] for key in OWN_AT.values()}
    s1, s2, s3 = SB_WIDTH, 2 * SB_WIDTH, 3 * SB_WIDTH
    s4 = s3 + 2 * CONV_CH

    def riders_at(host):
        return Riders(gather=[(n, shards[(n, l)]) for n, l in GATHER_AT.get(host, [])],
                      scatter=[(n, slots[(n, l)]) for n, l in SCATTER_AT[host]],
                      own=OWN_AT[host][0] if host in OWN_AT else None)

    def take(host, gathered, handed):
        for (n, l), g in zip(GATHER_AT.get(host, []), gathered):
            mats[(n, l)] = _assemble(n, g)
        for key, s in zip(SCATTER_AT[host], handed):
            slot[key] = s

    def linear_(x, n, l, name, residual=None, host=None):
        y, gathered, handed = linear(x, mats[(n, l)], slot[(n, l)], name, residual,
                                     riders_at(host) if host else NO_RIDERS)
        if host:
            take(host, gathered, handed)
        return y

    def fused_(f, rows, params, n, l, name, residual=None, host=None, carry=(), block_rows=None):
        y, carried, gathered, handed = rowwise_linear(
            f, rows, params, [], mats[(n, l)], slot[(n, l)], name, residual,
            riders_at(host) if host else NO_RIDERS, carry, block_rows)
        if host:
            take(host, gathered, handed)
        return (y,) + tuple(carried)

    def gain(n, l):
        return w[n][l].reshape(1, -1)

    for l in range(DEPTH):
        tag = 'l%d_' % l
        p, x = fused_(_rms, [x], [gain('norm_mix_g', l)], 'w_in', l, tag + 'w_in', carry=(0,),
                      host='win0' if l == 0 else None)
        q, k, v, u_conv, u_ssm = split_columns(p, (0, s1, s2, s3, s4, p.shape[1]))
        q = groupnorm(q, w['sb_q_norm_g'][l], SB_HEAD_DIM, tag + 'q_norm')
        k = groupnorm(k, w['sb_k_norm_g'][l], SB_HEAD_DIM, tag + 'k_norm')
        o_sb, gathered, handed = sb_attention(q, k, v, tag + 'sb', riders_at('sb%d' % l))
        take('sb%d' % l, gathered, handed)

        dw_w = jnp.pad(w['conv_dw_w'][l], ((0, CONV_HALO - CONV_WIDTH), (0, 0)))
        hc = dwconv(glu(u_conv, tag + 'conv_glu'), dw_w, w['conv_dw_b'][l].reshape(1, -1), tag + 'dwconv')
        o_conv, = fused_(ln_silu_block, [hc], [gain('conv_ln_g', l), gain('conv_ln_b', l)], 'conv_pw2_w', l,
                         tag + 'pw2')

        ar, ai, bbr, bbi, cr, ci = ssm_discretise(
            w['ssm_lam_re'][l], w['ssm_lam_im'][l], w['ssm_log_dt'][l], w['ssm_b_re'][l], w['ssm_b_im'][l],
            w['ssm_c_re'][l], w['ssm_c_im'][l])
        y = ssm_core(u_ssm, ar, ai, bbr, bbi, cr, ci, w['ssm_d'][l].reshape(1, -1), tag + 'ssm')
        o_ssm = glu(linear_(y, 'ssm_glu_w', l, tag + 'ssm_glu_w'), tag + 'ssm_glu')

        x, = fused_(branch_norms_block, [o_sb, o_conv, o_ssm], [gain('branch_norm_g', l)], 'w_out', l,
                    tag + 'w_out', residual=x)

        q_raw, x = fused_(_rms, [x], [gain('norm_xa_g', l)], 'xa_wq', l, tag + 'xa_wq', carry=(0,))
        hm = rmsnorm(mem, w['norm_mem_g'][l], tag + 'norm_mem')
        qx = groupnorm(q_raw, w['xa_q_norm_g'][l], XA_HEAD_DIM, tag + 'xa_qn')
        kx = groupnorm(linear_(hm, 'xa_wk', l, tag + 'xa_wk'), w['xa_k_norm_g'][l], XA_HEAD_DIM, tag + 'xa_kn')
        vx = linear_(hm, 'xa_wv', l, tag + 'xa_wv')
        x, = fused_(xa_core_block, [qx], [kx, vx], 'xa_wo', l, tag + 'xa_wo', residual=x,
                    block_rows=min(256, qx.shape[0]))

        host = 'ffn0' if l == 0 else None
        x, gathered, handed = ffn(x, gain('norm_ffn_g', l), mats[('ffn_w_in', l)], slot[('ffn_w_in', l)],
                                  mats[('ffn_w_out', l)], slot[('ffn_w_out', l)], tag + 'ffn',
                                  riders_at(host) if host else NO_RIDERS)
        if host:
            take(host, gathered, handed)
    return jnp.sum(loss_rows(x, target, 'loss'))


def local_step(w, mats, shards, x, mem, target):
    def shard_shape(key):
        return shards[key].shape if key in shards else _shard_shape(key[0], mats[key].shape)

    keys = [key for keys in SCATTER_AT.values() for key in keys] + list(OWN_AT.values())
    slots = {key: jnp.zeros((N_CHIPS,) + shard_shape(key), BF16) for key in keys}
    loss, (g_mats, g_w, gx) = jax.value_and_grad(local_loss, argnums=(0, 1, 4))(
        slots, w, mats, shards, x, mem, target)
    return loss, gx, g_w, g_mats


PACK_ROWS = 2048


PIECE_ROWS = 16


def _piece_rows(size):
    rows = -(-size // LANES)
    return rows, -(-rows // PIECE_ROWS) * PIECE_ROWS


def pack(arrays, dtype):
    parts, total = [], 0
    for a in arrays:
        rows, padded = _piece_rows(a.size)
        a = a.astype(dtype)
        if a.size % LANES:
            a = jnp.pad(a.reshape(-1), (0, rows * LANES - a.size))
        a = a.reshape(rows, LANES)
        if padded != rows:
            a = jnp.pad(a, ((0, padded - rows), (0, 0)))
        parts.append(a)
        total += padded
    tail = -total % PACK_ROWS
    if tail:
        parts.append(jnp.zeros((tail, LANES), dtype))
    return jnp.concatenate(parts, axis=0)


def unpack(packed, shapes):
    out, off = [], 0
    for s in shapes:
        size = math.prod(s)
        rows, padded = _piece_rows(size)
        piece = packed[off:off + rows]
        if size % LANES:
            piece = piece.reshape(-1)[:size]
        out.append(piece.reshape(s))
        off += padded
    return out


def _mesh_pos():
    return lax.axis_index("x"), lax.axis_index("y"), lax.axis_index("c")


def _exchange_xy(n_arrays, src_of, dst_of, sems, wait):
    send_sems, recv_sems, local_sems = sems
    x, y, c = _mesh_pos()
    me = 2 * x + y
    peers = [(1 - x, y), (x, 1 - y), (1 - x, 1 - y)]
    for k in range(n_arrays):
        own = pltpu.make_async_copy(src_of(k, me), dst_of(k, me), local_sems.at[k])
        if wait:
            own.wait()
        else:
            own.start()
        for p, (px, py) in enumerate(peers):
            out = pltpu.make_async_remote_copy(
                src_ref=src_of(k, 2 * px + py), dst_ref=dst_of(k, me), send_sem=send_sems.at[3 * k + p],
                recv_sem=recv_sems.at[3 * k + p], device_id=(px, py, c), device_id_type=MESH)
            if wait:
                pltpu.make_async_remote_copy(
                    src_ref=src_of(k, me), dst_ref=dst_of(k, 2 * px + py), send_sem=send_sems.at[3 * k + p],
                    recv_sem=recv_sems.at[3 * k + p], device_id=(px, py, c), device_id_type=MESH).wait_recv()
                out.wait_send()
            else:
                out.start()


class Ride:
    def __init__(self, arrays, out_shapes, src_of, dst_of):
        self.arrays, self.out_shapes = list(arrays), list(out_shapes)
        self._src_of, self._dst_of = src_of, dst_of
        n = len(self.arrays)
        self.in_specs = [pl.BlockSpec(memory_space=pl.ANY)] * n
        self.out_specs = [pl.BlockSpec(memory_space=pl.ANY)] * len(self.out_shapes)
        self.out_shape = [jax.ShapeDtypeStruct(s, BF16) for s in self.out_shapes]
        self.scratch = [pltpu.SemaphoreType.DMA((3 * n,)), pltpu.SemaphoreType.DMA((3 * n,)),
                        pltpu.SemaphoreType.DMA((n,))]

    def run(self, parts, wait):
        ins, outs, sems = parts
        _exchange_xy(len(self.arrays), lambda k, chip: self._src_of(ins, k, chip),
                     lambda k, chip: self._dst_of(outs, k, chip), sems, wait)

    def at_ends(self, parts, first, last):
        pl.when(first)(lambda: self.run(parts, False))

        def finish():
            pl.when(last)(lambda: self.run(parts, True))
        return finish


def _split_refs(ride, n_in, n_out, refs):
    if ride is None:
        return refs[:n_in], refs[n_in:n_in + n_out], refs[n_in + n_out:], None
    ni, no = len(ride.arrays), len(ride.out_shapes)
    b = n_in + ni
    c = b + n_out
    d = c + no
    return refs[:n_in], refs[b:c], refs[d:len(refs) - 3], (refs[n_in:b], refs[c:d], refs[len(refs) - 3:])


def run_ride(ride, name):
    def body(*refs):
        parts = _split_refs(ride, 0, 0, refs)[3]
        ride.run(parts, False)
        ride.run(parts, True)

    return pl.pallas_call(
        body, name=name, in_specs=ride.in_specs, out_specs=ride.out_specs, out_shape=ride.out_shape,
        scratch_shapes=ride.scratch, compiler_params=pltpu.CompilerParams(has_side_effects=True),
    )(*ride.arrays)


SLOTTED = ('w_in', 'taps')


def _part(ref, axis, chip, size):
    start = pl.multiple_of(chip * size, size)
    index = [slice(None)] * len(ref.shape)
    index[axis] = pl.ds(start, size)
    return ref.at[tuple(index)]


def _whole_shape(name, shard_shape):
    s = list(shard_shape)
    s[SHARD_AXIS[name] - 1] *= N_CHIPS
    return tuple(s)


def _shard_shape(name, whole_shape):
    s = list(whole_shape)
    s[SHARD_AXIS[name] - 1] //= N_CHIPS
    return tuple(s)


def gather_ride(names, shards):
    def out_shape(k):
        return (N_CHIPS,) + shards[k].shape if names[k] in SLOTTED else _whole_shape(names[k], shards[k].shape)

    def dst_of(outs, k, chip):
        if names[k] in SLOTTED:
            return outs[k].at[chip]
        axis = SHARD_AXIS[names[k]] - 1
        return _part(outs[k], axis, chip, shards[k].shape[axis])

    return Ride(shards, [out_shape(k) for k in range(len(shards))], lambda ins, k, chip: ins[k], dst_of)


def scatter_ride(names, grads):
    def shard_shape(k):
        return grads[k].shape[1:] if names[k] in SLOTTED else _shard_shape(names[k], grads[k].shape)

    def src_of(ins, k, chip):
        if names[k] in SLOTTED:
            return ins[k].at[chip]
        axis = SHARD_AXIS[names[k]] - 1
        return _part(ins[k], axis, chip, shard_shape(k)[axis])

    return Ride(grads, [(N_CHIPS,) + shard_shape(k) for k in range(len(grads))], src_of,
                lambda outs, k, chip: outs[k].at[chip])


def _cut_for_chips(name, g):
    if name not in SLOTTED:
        return g
    axis = SHARD_AXIS[name] - 1
    cut = g.shape[:axis] + (N_CHIPS, g.shape[axis] // N_CHIPS) + g.shape[axis + 1:]
    return jnp.moveaxis(g.reshape(cut), axis, 0)


class SwapRide(Ride):
    def __init__(self, arrays):
        self.arrays = list(arrays)
        self.out_shapes = [a.shape for a in self.arrays]
        n = len(self.arrays)
        self.in_specs = [pl.BlockSpec(memory_space=pl.ANY)] * n
        self.out_specs = [pl.BlockSpec(memory_space=pl.ANY)] * n
        self.out_shape = [jax.ShapeDtypeStruct(a.shape, a.dtype) for a in self.arrays]
        self.scratch = [pltpu.SemaphoreType.DMA((n,)), pltpu.SemaphoreType.DMA((n,)), pltpu.SemaphoreType.DMA((1,))]

    def run(self, parts, wait):
        ins, outs, (send_sems, recv_sems, _) = parts
        x, y, c = _mesh_pos()
        for k in range(len(self.arrays)):
            cp = pltpu.make_async_remote_copy(
                src_ref=ins[k], dst_ref=outs[k], send_sem=send_sems.at[k], recv_sem=recv_sems.at[k],
                device_id=(x, y, 1 - c), device_id_type=MESH)
            if wait:
                cp.wait()
            else:
                cp.start()


def allreduce_small(buf, name):
    R = buf.shape[0]
    half = R // 2

    def body(in_ref, sum_ref, other_ref, chip_ref, got_ref, send_sems, recv_sems):
        x, y, c = _mesh_pos()
        sibling = (x, y, 1 - c)
        peers = [(1 - x, y, c), (x, 1 - y, c), (1 - x, 1 - y, c)]

        def copy(k, src, dst, to):
            return pltpu.make_async_remote_copy(src_ref=src, dst_ref=dst, send_sem=send_sems.at[k],
                                                recv_sem=recv_sems.at[k], device_id=to, device_id_type=MESH)

        swap = copy(0, in_ref, other_ref, sibling)
        swap.start()
        swap.wait()
        chip_ref[...] = in_ref[...] + other_ref[...]

        mine = chip_ref.at[pl.ds(pl.multiple_of(c * half, SUBLANES), half), :]
        sends = [copy(1 + k, mine, got_ref.at[k], to) for k, to in enumerate(peers)]
        for cp in sends:
            cp.start()
        for cp in sends:
            cp.wait()
        total = (mine[...] + got_ref[0]) + (got_ref[1] + got_ref[2])

        done = sum_ref.at[pl.ds(pl.multiple_of(c * half, SUBLANES), half), :]
        done[...] = total
        back = copy(4, done, done, sibling)
        back.start()
        back.wait()

    vmem = pl.BlockSpec(memory_space=pltpu.VMEM)
    return pl.pallas_call(
        body, name=name, in_specs=[vmem], out_specs=vmem,
        out_shape=jax.ShapeDtypeStruct((R, LANES), F32),
        scratch_shapes=[pltpu.VMEM((R, LANES), F32), pltpu.VMEM((R, LANES), F32),
                        pltpu.VMEM((3, half, LANES), F32),
                        pltpu.SemaphoreType.DMA((5,)), pltpu.SemaphoreType.DMA((5,))],
        compiler_params=pltpu.CompilerParams(has_side_effects=True, vmem_limit_bytes=VMEM_LIMIT),
    )(buf)


def _adamw_update(g, w, m, v):
    m2 = ADAM_B1 * m + (1.0 - ADAM_B1) * g
    v2 = ADAM_B2 * v + (1.0 - ADAM_B2) * (g * g)
    m_hat = m2 / (1.0 - ADAM_B1 ** ADAM_STEP)
    v_hat = v2 / (1.0 - ADAM_B2 ** ADAM_STEP)
    return -ADAM_LR * (m_hat / (jnp.sqrt(v_hat) + ADAM_EPS) + ADAM_WD * w), m2, v2


def adamw_matrix(layer, mine, other, w, m, v, so_far, name):
    _, rows, cols = w.shape
    T = 16
    while rows % (2 * T) == 0 and 2 * T * cols <= 128 * 1024:
        T *= 2

    def body(mine_ref, other_ref, w_ref, m_ref, v_ref, *rest):
        g_out, d_out, m_out, v_out = rest[-4:]

        def total(ref):
            acc = ref[0].astype(F32)
            for k in range(1, N_CHIPS):
                acc = acc + ref[k].astype(F32)
            return acc

        g = total(mine_ref) + total(other_ref)
        g_out[...] = g
        d_out[...], m_out[...], v_out[...] = _adamw_update(g, w_ref[...], m_ref[...], v_ref[...])

    slots = pl.BlockSpec((N_CHIPS, T, cols), lambda i: (0, i, 0))
    spec = pl.BlockSpec((None, T, cols), lambda i: (layer, i, 0))
    filled = [] if so_far is None else list(so_far)
    return pl.pallas_call(
        body, name=name, grid=(rows // T,),
        in_specs=[slots, slots, spec, spec, spec] + [pl.BlockSpec(memory_space=pl.ANY)] * len(filled),
        out_specs=[spec] * 4,
        out_shape=[jax.ShapeDtypeStruct(w.shape, F32)] * 4,
        input_output_aliases={5 + j: j for j in range(len(filled))},
        compiler_params=_cparams(("parallel",)),
    )(mine, other, w, m, v, *filled)


def adamw_small(gs, ws, ms, vs, name):
    n = len(gs)

    def body(*refs):
        for k in range(n):
            g, w, m, v = (refs[j * n + k][...] for j in range(4))
            d_out, m_out, v_out = (refs[(4 + j) * n + k] for j in range(3))
            d_out[...], m_out[...], v_out[...] = _adamw_update(g, w, m, v)

    vmem = pl.BlockSpec(memory_space=pltpu.VMEM)
    outs = pl.pallas_call(
        body, name=name,
        in_specs=[vmem] * (4 * n), out_specs=[vmem] * (3 * n),
        out_shape=[jax.ShapeDtypeStruct(w.shape, F32) for w in ws] * 3,
        compiler_params=_cparams(),
    )(*gs, *ws, *ms, *vs)
    return outs[:n], outs[n:2 * n], outs[2 * n:]


def _shard_of(full, axis, chip):
    size = full.shape[axis] // N_CHIPS
    return lax.slice_in_dim(full, chip * size, (chip + 1) * size, axis=axis)


def kernel(x, mem, norm_mix_g, w_in, sb_q_norm_g, sb_k_norm_g, conv_dw_w, conv_dw_b, conv_ln_g, conv_ln_b, conv_pw2_w, ssm_lam_re, ssm_lam_im, ssm_log_dt, ssm_b_re, ssm_b_im, ssm_c_re, ssm_c_im, ssm_d, ssm_glu_w, branch_norm_g, w_out, norm_xa_g, norm_mem_g, xa_wq, xa_wk, xa_wv, xa_q_norm_g, xa_k_norm_g, xa_wo, norm_ffn_g, ffn_w_in, ffn_w_out, loss_target, m_norm_mix_g, m_w_in, m_sb_q_norm_g, m_sb_k_norm_g, m_conv_dw_w, m_conv_dw_b, m_conv_ln_g, m_conv_ln_b, m_conv_pw2_w, m_ssm_lam_re, m_ssm_lam_im, m_ssm_log_dt, m_ssm_b_re, m_ssm_b_im, m_ssm_c_re, m_ssm_c_im, m_ssm_d, m_ssm_glu_w, m_branch_norm_g, m_w_out, m_norm_xa_g, m_norm_mem_g, m_xa_wq, m_xa_wk, m_xa_wv, m_xa_q_norm_g, m_xa_k_norm_g, m_xa_wo, m_norm_ffn_g, m_ffn_w_in, m_ffn_w_out, v_norm_mix_g, v_w_in, v_sb_q_norm_g, v_sb_k_norm_g, v_conv_dw_w, v_conv_dw_b, v_conv_ln_g, v_conv_ln_b, v_conv_pw2_w, v_ssm_lam_re, v_ssm_lam_im, v_ssm_log_dt, v_ssm_b_re, v_ssm_b_im, v_ssm_c_re, v_ssm_c_im, v_ssm_d, v_ssm_glu_w, v_branch_norm_g, v_w_out, v_norm_xa_g, v_norm_mem_g, v_xa_wq, v_xa_wk, v_xa_wv, v_xa_q_norm_g, v_xa_k_norm_g, v_xa_wo, v_norm_ffn_g, v_ffn_w_in, v_ffn_w_out):
    given = dict(locals())
    w = {n: given[n] for n in WEIGHTS}
    m = {n: given['m_' + n] for n in WEIGHTS}
    v = {n: given['v_' + n] for n in WEIGHTS}

    depth = w_in.shape[0]
    chip = 2 * lax.axis_index("x") + lax.axis_index("y")

    assert depth == DEPTH
    taps_bits = lax.bitcast_convert_type(conv_dw_w, BF16)
    shards = {(n, l): w[n][l].astype(BF16) for n in MATRICES for l in range(depth)}
    first = [shards.pop(key) for key in GATHER_AT['start']]
    gathered = run_ride(gather_ride([n for n, _ in GATHER_AT['start']] + ['taps'], first + [taps_bits]),
                        'gather_first')
    mats = {key: _assemble(key[0], g) for key, g in zip(GATHER_AT['start'], gathered)}
    taps = jnp.concatenate([lax.bitcast_convert_type(gathered[-1][j], F32) for j in range(N_CHIPS)], axis=2)
    local_w = {n: w[n] for n in REPLICATED}
    local_w['conv_dw_w'] = taps

    loss, gx, g_w, mine = local_step(local_w, mats, shards, x[0], mem[0], loss_target[0])

    keys = [(n, l) for n in MATRICES for l in range(depth)]
    other = dict(zip(keys, run_ride(SwapRide([mine[key] for key in keys]), 'swap_cores')))
    outs = {}
    for n in MATRICES:
        res = None
        for l in range(depth):
            res = adamw_matrix(l, mine[(n, l)], other[(n, l)], w[n], m[n], v[n], res, 'adamw_%s_%d' % (n, l))
        outs['grad_' + n], outs['delta_' + n], outs['new_m_' + n], outs['new_v_' + n] = res

    reduced = allreduce_small(pack([g_w[n] for n in REPLICATED] + [g_w['conv_dw_w'], loss.reshape(1)], F32),
                              'allreduce_small')
    reduced = unpack(reduced, [w[n].shape for n in REPLICATED] + [taps.shape, (1,)])
    total_loss = reduced[-1].reshape(())
    tap_cols = conv_dw_w.shape[2]
    reduced[-2] = lax.dynamic_slice_in_dim(reduced[-2], chip * tap_cols, tap_cols, axis=2)
    small_names = REPLICATED + ['conv_dw_w']
    deltas, new_ms, new_vs = adamw_small(reduced[:-1], [w[n] for n in small_names], [m[n] for n in small_names],
                                         [v[n] for n in small_names], 'adamw_small')
    for k, n in enumerate(small_names):
        outs['grad_' + n], outs['delta_' + n] = reduced[k], deltas[k]
        outs['new_m_' + n], outs['new_v_' + n] = new_ms[k], new_vs[k]
    return (total_loss, gx[None], *[outs['grad_' + n] for n in WEIGHTS], *[outs['delta_' + n] for n in WEIGHTS],
            *[outs['new_m_' + n] for n in WEIGHTS], *[outs['new_v_' + n] for n in WEIGHTS])
```

```python
import math

import jax
import jax.numpy as jnp
from jax import lax
from jax.experimental import pallas as pl
from jax.experimental.pallas import tpu as pltpu

F32 = jnp.float32
BF16 = jnp.bfloat16
MESH = pl.DeviceIdType.MESH

EPS = 1e-6
LANES = 128
SUBLANES = 8
VMEM_LIMIT = 56 * 1024 * 1024

SB_HEAD_DIM = 64
SB_WIDTH = 512
CONV_CH = 256
CONV_WIDTH = 31
CONV_HALO = 32
SSM_CH = 256
SSM_GROUPS = 16
SSM_GROUP = 16
SSM_STATE = 64
SSM_LANES = SSM_GROUPS * SSM_STATE
XA_HEADS = 4
XA_HEAD_DIM = 256
SB_CUT = 110.0

ADAM_LR = 0.001
ADAM_B1 = 0.9
ADAM_B2 = 0.999
ADAM_EPS = 1e-08
ADAM_WD = 0.01
ADAM_STEP = 10

WEIGHTS = ['norm_mix_g', 'w_in', 'sb_q_norm_g', 'sb_k_norm_g', 'conv_dw_w', 'conv_dw_b', 'conv_ln_g',
           'conv_ln_b', 'conv_pw2_w', 'ssm_lam_re', 'ssm_lam_im', 'ssm_log_dt', 'ssm_b_re', 'ssm_b_im',
           'ssm_c_re', 'ssm_c_im', 'ssm_d', 'ssm_glu_w', 'branch_norm_g', 'w_out', 'norm_xa_g',
           'norm_mem_g', 'xa_wq', 'xa_wk', 'xa_wv', 'xa_q_norm_g', 'xa_k_norm_g', 'xa_wo', 'norm_ffn_g',
           'ffn_w_in', 'ffn_w_out']
SHARD_AXIS = {'w_in': 2, 'conv_dw_w': 2, 'conv_pw2_w': 1, 'ssm_glu_w': 2, 'w_out': 1, 'xa_wq': 1,
              'xa_wk': 1, 'xa_wv': 1, 'xa_wo': 1, 'ffn_w_in': 2, 'ffn_w_out': 1}
MATRICES = [n for n in WEIGHTS if n in SHARD_AXIS and n != 'conv_dw_w']
REPLICATED = [n for n in WEIGHTS if n not in SHARD_AXIS]
N_CHIPS = 4
N_DEV = 8


def _cparams(sem=None, **kw):
    if sem is not None:
        kw['dimension_semantics'] = sem
    return pltpu.CompilerParams(vmem_limit_bytes=VMEM_LIMIT, **kw)


def _pick(n, target):
    best = None
    d = LANES
    while d <= min(n, target):
        if n % d == 0:
            best = d
        d += LANES
    return best if best is not None else n


def _rows_for(n_rows, width):
    t = 512
    while t > 8 and t * width > 768 * 1024:
        t //= 2
    return min(t, n_rows)


def _dg(a, b, ca, cb):
    return lax.dot_general(a.astype(BF16), b.astype(BF16), (((ca,), (cb,)), ((), ())),
                           preferred_element_type=F32)


@jax.custom_vjp
def bdot_nn(a, b):
    return _dg(a, b, 1, 0)


def _bdot_nn_fwd(a, b):
    return _dg(a, b, 1, 0), (a, b)


def _bdot_nn_bwd(res, g):
    a, b = res
    return _dg(g, b, 1, 1), _dg(a, g, 0, 0)


bdot_nn.defvjp(_bdot_nn_fwd, _bdot_nn_bwd)


@jax.custom_vjp
def bdot_nt(a, b):
    return _dg(a, b, 1, 1)


def _bdot_nt_fwd(a, b):
    return _dg(a, b, 1, 1), (a, b)


def _bdot_nt_bwd(res, g):
    a, b = res
    return _dg(g, b, 1, 0), _dg(g, a, 0, 0)


bdot_nt.defvjp(_bdot_nt_fwd, _bdot_nt_bwd)


def mm(a, b, mode, name, out_dtype=F32, add=None, ride=None):
    if mode == 'nn':
        M, K = a.shape
        N = b.shape[1]
    elif mode == 'nt':
        M, K = a.shape
        N = b.shape[0]
    else:
        K, M = a.shape
        N = b.shape[1]
    if mode == 'tn':
        tm, tn, tk = _pick(M, 1536), _pick(N, 2816), _pick(K, 512)
    else:
        tm, tn = _pick(M, 1024), _pick(N, 1536)
        tk = K if K <= 2816 else _pick(K, 1536)
    nk = K // tk
    ca, cb = {'nn': (1, 0), 'nt': (1, 1), 'tn': (0, 0)}[mode]

    grid = (M // tm, N // tn, nk)

    def body(*refs):
        ins, (o_ref,), scratch, riding = _split_refs(ride, 3 if add is not None else 2, 1, refs)
        a_ref, b_ref = ins[:2]
        add_ref = ins[2] if add is not None else None
        i, j, k = pl.program_id(0), pl.program_id(1), pl.program_id(2)
        ride_done = None
        if riding is not None:
            first = jnp.logical_and(i == 0, jnp.logical_and(j == 0, k == 0))
            last = jnp.logical_and(i == grid[0] - 1, jnp.logical_and(j == grid[1] - 1, k == nk - 1))
            ride_done = ride.at_ends(riding, first, last)

        def finish(acc):
            if add_ref is not None:
                acc = acc + add_ref[...]
            o_ref[...] = acc.astype(out_dtype)

        if nk == 1:
            finish(_dg(a_ref[...], b_ref[...], ca, cb))
        else:
            acc_ref, = scratch

            @pl.when(k == 0)
            def _():
                acc_ref[...] = jnp.zeros_like(acc_ref)

            acc_ref[...] += _dg(a_ref[...], b_ref[...], ca, cb)

            @pl.when(k == nk - 1)
            def _():
                finish(acc_ref[...])

        if ride_done is not None:
            ride_done()

    if mode == 'nn':
        a_spec = pl.BlockSpec((tm, tk), lambda i, j, k: (i, k))
        b_spec = pl.BlockSpec((tk, tn), lambda i, j, k: (k, j))
    elif mode == 'nt':
        a_spec = pl.BlockSpec((tm, tk), lambda i, j, k: (i, k))
        b_spec = pl.BlockSpec((tn, tk), lambda i, j, k: (j, k))
    else:
        a_spec = pl.BlockSpec((tk, tm), lambda i, j, k: (k, i))
        b_spec = pl.BlockSpec((tk, tn), lambda i, j, k: (k, j))
    out_spec = pl.BlockSpec((tm, tn), lambda i, j, k: (i, j))
    own_in = [a_spec, b_spec] + ([out_spec] if add is not None else [])
    operands = (a, b) if add is None else (a, b, add)
    scratch = [pltpu.VMEM((tm, tn), F32)] if nk > 1 else []
    if ride is None:
        return pl.pallas_call(
            body, name=name, grid=grid, in_specs=own_in, out_specs=out_spec,
            out_shape=jax.ShapeDtypeStruct((M, N), out_dtype), scratch_shapes=scratch,
            compiler_params=_cparams(("parallel", "parallel", "arbitrary")),
        )(*operands)
    outs = pl.pallas_call(
        body, name=name, grid=grid, in_specs=own_in + ride.in_specs, out_specs=[out_spec] + ride.out_specs,
        out_shape=[jax.ShapeDtypeStruct((M, N), out_dtype)] + ride.out_shape,
        scratch_shapes=scratch + ride.scratch,
        compiler_params=_cparams(("arbitrary", "arbitrary", "arbitrary"), has_side_effects=True),
    )(*operands, *ride.arrays)
    return outs[0], outs[1:]


class Riders:
    def __init__(self, gather=(), scatter=(), own=None):
        self.gather_names = [n for n, _ in gather]
        self.shards = tuple(s for _, s in gather)
        self.scatter_names = [n for n, _ in scatter]
        self.slots = tuple(s for _, s in scatter)
        self.own = own

    def gather_ride(self, shards):
        return gather_ride(self.gather_names, list(shards)) if shards else None

    def scatter_ride(self, grads, own_grad=None):
        names, grads = list(self.scatter_names), list(grads)
        if self.own is not None:
            names.append(self.own)
            grads.append(own_grad)
        if not grads:
            return None
        return scatter_ride(names, [_cut_for_chips(n, g) for n, g in zip(names, grads)])

    def whole_slots(self, slots):
        return tuple(jnp.zeros(_whole_shape(n, s.shape[1:]), BF16) for n, s in zip(self.scatter_names, slots))


NO_RIDERS = Riders()


def linear(x, w, slot, name):
    @jax.custom_vjp
    def op(x, w, slot):
        return mm(x, w, 'nn', name + '_fwd')

    def op_fwd(x, w, slot):
        return op(x, w, slot), (x, w)

    def op_bwd(res, g):
        x, w = res
        return mm(g, w, 'nt', name + '_dx'), jnp.zeros_like(w), mm(x, g, 'tn', name + '_dw', BF16)

    op.defvjp(op_fwd, op_bwd)
    return op(x, w, slot)


def _rowwise_calls(f, rows, params, consts, out_widths, name, need_row_grad=None, block_rows=None, carry=(),
                   out_dtype=F32):
    nr, npar, nc, nout = len(rows), len(params), len(consts), len(out_widths)
    carry = tuple(carry)
    L = rows[0].shape[0]
    widths = [r.shape[1] for r in rows]
    T = block_rows or _rows_for(L, max(widths + list(out_widths)))
    n = L // T
    need = list(need_row_grad) if need_row_grad is not None else [True] * nr
    pshapes = [p.shape for p in params]
    cshapes = [c.shape for c in consts]

    row_specs = [pl.BlockSpec((T, w), lambda i: (i, 0)) for w in widths]
    par_specs = [pl.BlockSpec(s, lambda i: (0, 0)) for s in pshapes]
    con_specs = [pl.BlockSpec(s, lambda i: (0, 0)) for s in cshapes]
    out_specs = [pl.BlockSpec((T, w), lambda i: (i, 0)) for w in out_widths]

    def fwd_call(rows, params, consts):
        def body(*refs):
            ins = [r[...] for r in refs[:nr + npar + nc]]
            outs = f(*ins)
            for o_ref, val in zip(refs[nr + npar + nc:], outs):
                o_ref[...] = val.astype(out_dtype)

        return pl.pallas_call(
            body, name=name + '_fwd', grid=(n,),
            in_specs=row_specs + par_specs + con_specs, out_specs=out_specs,
            out_shape=[jax.ShapeDtypeStruct((L, w), out_dtype) for w in out_widths],
            compiler_params=_cparams(("parallel",)),
        )(*rows, *params, *consts)

    def bwd_call(rows, params, consts, cts, carried):
        grad_rows = [k for k in range(nr) if need[k]]
        n_in = nr + npar + nc + nout

        def body(*refs):
            i = pl.program_id(0)
            rv = [r[...] for r in refs[:nr]]
            pv = [r[...] for r in refs[nr:nr + npar]]
            cv = [r[...] for r in refs[nr + npar:nr + npar + nc]]
            ctv = tuple(r[...] for r in refs[nr + npar + nc:n_in])
            carried_refs = dict(zip(carry, refs[n_in:n_in + len(carry)]))
            orefs = refs[n_in + len(carry):]
            _, vjp = jax.vjp(lambda *rp: tuple(f(*rp, *cv)), *rv, *pv)
            g = vjp(ctv)
            for slot, k in enumerate(grad_rows):
                orefs[slot][...] = g[k] + carried_refs[k][...] if k in carried_refs else g[k]

            @pl.when(i == 0)
            def _():
                for k in range(npar):
                    orefs[len(grad_rows) + k][...] = jnp.zeros(pshapes[k], F32)

            for k in range(npar):
                orefs[len(grad_rows) + k][...] += g[nr + k]

        outs = pl.pallas_call(
            body, name=name + '_bwd', grid=(n,),
            in_specs=row_specs + par_specs + con_specs + out_specs + [row_specs[k] for k in carry],
            out_specs=[row_specs[k] for k in grad_rows] + par_specs,
            out_shape=[jax.ShapeDtypeStruct((L, widths[k]), F32) for k in grad_rows]
            + [jax.ShapeDtypeStruct(s, F32) for s in pshapes],
            compiler_params=_cparams(("arbitrary",)),
        )(*rows, *params, *consts, *cts, *carried)
        drows = []
        slot = 0
        for k in range(nr):
            if need[k]:
                drows.append(outs[slot])
                slot += 1
            else:
                drows.append(jnp.zeros_like(rows[k]))
        return tuple(drows), tuple(outs[len(grad_rows):])

    return fwd_call, bwd_call


def rowwise(f, rows, params, consts, out_widths, name, need_row_grad=None, block_rows=None, carry=()):
    fwd_call, bwd_call = _rowwise_calls(f, rows, params, consts, out_widths, name, need_row_grad, block_rows, carry)
    nout = len(out_widths)

    @jax.custom_vjp
    def op(rows, params, consts):
        return tuple(fwd_call(rows, params, consts)) + tuple(rows[k] for k in carry)

    def op_fwd(rows, params, consts):
        return op(rows, params, consts), (rows, params, consts)

    def op_bwd(res, cts):
        rows, params, consts = res
        drows, dparams = bwd_call(rows, params, consts, cts[:nout], cts[nout:])
        return drows, dparams, tuple(jnp.zeros_like(c) for c in consts)

    op.defvjp(op_fwd, op_bwd)
    return op(tuple(rows), tuple(params), tuple(consts))


def rowwise_linear(f, rows, params, consts, w, slot, name, residual=None, riders=NO_RIDERS, carry=(),
                   block_rows=None):
    width = w.shape[0]
    fwd_call, bwd_call = _rowwise_calls(lambda *a: (f(*a),), rows, params, consts, [width], name, None,
                                        block_rows, carry, BF16)

    @jax.custom_vjp
    def op(rows, params, consts, w, slot, residual, shards, slots):
        h, = fwd_call(rows, params, consts)
        ride = riders.gather_ride(shards)
        y = mm(h, w, 'nn', name + '_mm', add=residual, ride=ride)
        y, gathered = y if ride is not None else (y, ())
        return y, tuple(rows[k] for k in carry), tuple(gathered), riders.whole_slots(slots)

    def op_fwd(rows, params, consts, w, slot, residual, shards, slots):
        h, = fwd_call(rows, params, consts)
        ride = riders.gather_ride(shards)
        y = mm(h, w, 'nn', name + '_mm', add=residual, ride=ride)
        y, gathered = y if ride is not None else (y, ())
        out = (y, tuple(rows[k] for k in carry), tuple(gathered), riders.whole_slots(slots))
        return out, (rows, params, consts, h, w, shards)

    def op_bwd(res, cts):
        rows, params, consts, h, w, shards = res
        g, carried, _, slot_grads = cts
        dw = mm(h, g, 'tn', name + '_dw', BF16)
        ride = riders.scatter_ride(slot_grads, dw)
        dh = mm(g, w, 'nt', name + '_dx', ride=ride)
        dh, received = dh if ride is not None else (dh, ())
        if riders.own is not None:
            received, dw = received[:-1], received[-1]
        drows, dparams = bwd_call(rows, params, consts, (dh,), carried)
        return (drows, dparams, tuple(jnp.zeros_like(c) for c in consts), jnp.zeros_like(w), dw,
                None if residual is None else g, tuple(jnp.zeros_like(s) for s in shards), tuple(received))

    op.defvjp(op_fwd, op_bwd)
    return op(tuple(rows), tuple(params), tuple(consts), w, slot, residual, riders.shards, riders.slots)


def _rms(x, g):
    return x * lax.rsqrt(jnp.mean(x * x, axis=-1, keepdims=True) + EPS) * g


def rmsnorm(x, g, name, carry=False):
    out = rowwise(lambda x, g: (_rms(x, g),), [x], [g.reshape(1, -1)], [], [x.shape[1]], name,
                  carry=(0,) if carry else ())
    return out if carry else out[0]


def _split2(x):
    hi = x.astype(BF16)
    return hi, (x - hi.astype(F32)).astype(BF16)


@jax.custom_vjp
def select_mm(x, sel):
    return sum(_dg(t, sel, 1, 0) for t in _split2(x))


def _select_mm_fwd(x, sel):
    return select_mm(x, sel), sel


def _select_mm_bwd(sel, g):
    return sum(_dg(t, sel, 1, 1) for t in _split2(g)), jnp.zeros_like(sel)


select_mm.defvjp(_select_mm_fwd, _select_mm_bwd)


def groupnorm(x, g, group, name):
    width = x.shape[1]
    g_full = jnp.tile(g.reshape(1, group), (1, width // group))
    if group % LANES == 0:
        def f(x, g_full):
            outs = []
            for lo in range(0, width, group):
                xs = x[:, lo:lo + group]
                outs.append(_rms(xs, g_full[:, lo:lo + group]))
            return (jnp.concatenate(outs, axis=-1),)

        return rowwise(f, [x], [g_full], [], [width], name)[0]

    gid = jnp.arange(width) // group
    sel = (gid[:, None] == jnp.arange(LANES)[None, :]).astype(BF16)

    def f(x, g_full, sel, sel_t):
        ms = select_mm(x * x, sel) * (1.0 / group)
        inv = select_mm(lax.rsqrt(ms + EPS), sel_t)
        return (x * inv * g_full,)

    return rowwise(f, [x], [g_full], [sel, sel.T], [width], name)[0]


def glu(x, name):
    half = x.shape[1] // 2

    def f(x):
        return (x[:, :half] * jax.nn.sigmoid(x[:, half:]),)

    return rowwise(f, [x], [], [], [half], name)[0]


def ln_silu_block(x, g, b):
    mu = jnp.mean(x, axis=-1, keepdims=True)
    xc = x - mu
    var = jnp.mean(xc * xc, axis=-1, keepdims=True)
    y = xc * lax.rsqrt(var + EPS) * g + b
    return y * jax.nn.sigmoid(y)


def branch_norms_block(a, b, c, g):
    w1, w2 = a.shape[1], b.shape[1]
    return jnp.concatenate([_rms(a, g[:, :w1]), _rms(b, g[:, w1:w1 + w2]), _rms(c, g[:, w1 + w2:])], axis=-1)


def xa_core_block(q, k, v):
    scale = XA_HEAD_DIM ** -0.5
    outs = []
    for h in range(XA_HEADS):
        sl = slice(h * XA_HEAD_DIM, (h + 1) * XA_HEAD_DIM)
        s = bdot_nt(q[:, sl], k[:, sl]) * scale
        m = lax.stop_gradient(jnp.max(s, axis=-1, keepdims=True))
        e = jnp.exp(s - m)
        p = e / jnp.sum(e, axis=-1, keepdims=True)
        outs.append(bdot_nn(p, v[:, sl]))
    return jnp.concatenate(outs, axis=-1)


def loss_rows(y, target, name):
    def f(y, t):
        d = y - t
        return (0.5 * jnp.mean(d * d, axis=-1, keepdims=True),)

    return rowwise(f, [y, target], [], [], [1], name, need_row_grad=[True, False])[0]


def _swiglu(gate, up):
    return gate * jax.nn.sigmoid(gate) * up


def mm_swiglu(h, w, name, ride=None):
    M, K = h.shape
    H = w.shape[1] // 2
    tm, tn = _pick(M, 1024), _pick(H, 1536)
    nj = H // tn
    grid = (M // tm, 2 * nj)

    def body(*refs):
        (a_ref, b_ref), (gu_ref, act_ref), (kept,), riding = _split_refs(ride, 2, 2, refs)
        ride_done = _ride_ends(ride, riding, grid)
        j = pl.program_id(1)
        prod = _dg(a_ref[...], b_ref[...], 1, 0)
        gu_ref[...] = prod

        @pl.when(j < nj)
        def _():
            kept[j] = prod

        @pl.when(j >= nj)
        def _():
            act_ref[...] = _swiglu(kept[j - nj], prod).astype(BF16)

        ride_done()

    return _ride_call(
        body, name, grid,
        [pl.BlockSpec((tm, K), lambda i, j: (i, 0)), pl.BlockSpec((K, tn), lambda i, j: (0, j))],
        [pl.BlockSpec((tm, tn), lambda i, j: (i, j)),
         pl.BlockSpec((tm, tn), lambda i, j: (i, jnp.maximum(j - nj, 0)))],
        [jax.ShapeDtypeStruct((M, 2 * H), F32), jax.ShapeDtypeStruct((M, H), BF16)],
        (h, w), ride, ("parallel", "arbitrary"), scratch=[pltpu.VMEM((nj, tm, tn), F32)])


def mm_swiglu_bwd(dy, w_out, gu, name):
    M, D = dy.shape
    H = gu.shape[1] // 2
    tm, tn = _pick(M, 512), _pick(H, 1536)
    nj = H // tn

    def body(dy_ref, w_ref, gate_ref, up_ref, o_ref, kept):
        j = pl.program_id(1)

        @pl.when(j < nj)
        def _():
            dact = _dg(dy_ref[...], w_ref[...], 1, 1)
            _, vjp = jax.vjp(_swiglu, gate_ref[...], up_ref[...])
            dgate, dup = vjp(dact)
            o_ref[...] = dgate.astype(BF16)
            kept[j] = dup.astype(BF16)

        @pl.when(j >= nj)
        def _():
            o_ref[...] = kept[j - nj]

    def tile(j):
        return jnp.minimum(j, nj - 1)

    return pl.pallas_call(
        body, name=name, grid=(M // tm, 2 * nj),
        in_specs=[pl.BlockSpec((tm, D), lambda i, j: (i, 0)),
                  pl.BlockSpec((tn, D), lambda i, j: (tile(j), 0)),
                  pl.BlockSpec((tm, tn), lambda i, j: (i, tile(j))),
                  pl.BlockSpec((tm, tn), lambda i, j: (i, nj + tile(j)))],
        out_specs=pl.BlockSpec((tm, tn), lambda i, j: (i, j)),
        out_shape=jax.ShapeDtypeStruct((M, 2 * H), BF16),
        scratch_shapes=[pltpu.VMEM((nj, tm, tn), BF16)],
        compiler_params=_cparams(("parallel", "arbitrary")),
    )(dy, w_out, gu, gu)


def ffn(x, gain, w_in, slot_in, w_out, slot_out, name, riders=NO_RIDERS):
    norm_fwd, norm_bwd = _rowwise_calls(lambda x, g: (_rms(x, g),), [x], [gain], [], [x.shape[1]],
                                        name + '_norm', None, None, (0,), BF16)

    def forward(x, gain, w_in, w_out, shards):
        h, = norm_fwd((x,), (gain,), ())
        (gu, act), gathered = mm_swiglu(h, w_in, name + '_in', riders.gather_ride(shards))
        return mm(act, w_out, 'nn', name + '_out', add=x), tuple(gathered), (h, gu, act)

    @jax.custom_vjp
    def op(x, gain, w_in, slot_in, w_out, slot_out, shards, slots):
        y, gathered, _ = forward(x, gain, w_in, w_out, shards)
        return y, gathered, riders.whole_slots(slots)

    def op_fwd(x, gain, w_in, slot_in, w_out, slot_out, shards, slots):
        y, gathered, (h, gu, act) = forward(x, gain, w_in, w_out, shards)
        return (y, gathered, riders.whole_slots(slots)), (x, gain, h, gu, act, w_in, w_out, shards)

    def op_bwd(res, cts):
        x, gain, h, gu, act, w_in, w_out, shards = res
        g, _, slot_grads = cts
        dgu = mm_swiglu_bwd(g, w_out, gu, name + '_dact')
        dw_out = mm(act, g, 'tn', name + '_out_dw', BF16)
        dw_in = mm(h, dgu, 'tn', name + '_in_dw', BF16)
        ride = riders.scatter_ride(slot_grads)
        dh = mm(dgu, w_in, 'nt', name + '_in_dx', ride=ride)
        dh, received = dh if ride is not None else (dh, ())
        (dx,), (dgain,) = norm_bwd((x,), (gain,), (), (dh,), (g,))
        return (dx, dgain, jnp.zeros_like(w_in), dw_in, jnp.zeros_like(w_out), dw_out,
                tuple(jnp.zeros_like(s) for s in shards), tuple(received))

    op.defvjp(op_fwd, op_bwd)
    return op(x, gain, w_in, slot_in, w_out, slot_out, riders.shards, riders.slots)


def _hilo(x, ones_bf16):
    hi = x.astype(BF16)
    lo = (x - hi.astype(F32)).astype(BF16)
    return _dg(hi, ones_bf16, 1, 0) + _dg(lo, ones_bf16, 1, 0)


def _sb_block(qh, kb, c, valid, strict_upper):
    z = _dg(qh, kb, 1, 1)
    a = jnp.minimum(z, 0.0) - jnp.log(1.0 + jnp.exp(-jnp.abs(z)))
    b = jnp.where(valid, a - z, 0.0)
    s = _hilo(b, strict_upper) + c
    w = jnp.where(valid, jnp.exp(a + s), 0.0)
    return a, b, w


def _sb_masks(T):
    row = lax.broadcasted_iota(jnp.int32, (T, T), 0)
    col = lax.broadcasted_iota(jnp.int32, (T, T), 1)
    return col < row, (row > col).astype(BF16), (row >= col).astype(BF16)


def _sb_key_blocks(i, j, T, causal):
    second = jnp.maximum(j - 1, 0)
    return [(pl.multiple_of(j * T, T), jnp.logical_or(causal, j != i)),
            (pl.multiple_of(second * T, T), jnp.logical_and(jnp.logical_or(causal, True), j >= 1))]


HEADS_PER_BLOCK = LANES // SB_HEAD_DIM


def _head_mask(h):
    lane = lax.broadcasted_iota(jnp.int32, (1, LANES), 1)
    return (lane // SB_HEAD_DIM == h).astype(F32)


def _max_all(columns):
    m = columns[0]
    for c in columns[1:]:
        m = jnp.maximum(m, c)
    return jnp.max(m)


def _ride_call(body, name, grid, in_specs, out_specs, out_shape, operands, ride, semantics, scratch=()):
    if ride is None:
        outs = pl.pallas_call(body, name=name, grid=grid, in_specs=in_specs, out_specs=out_specs,
                              out_shape=out_shape, scratch_shapes=list(scratch),
                              compiler_params=_cparams(semantics))(*operands)
        return outs, ()
    outs = pl.pallas_call(
        body, name=name, grid=grid, in_specs=in_specs + ride.in_specs, out_specs=out_specs + ride.out_specs,
        out_shape=out_shape + ride.out_shape, scratch_shapes=list(scratch) + ride.scratch,
        compiler_params=_cparams(("arbitrary",) * len(grid), has_side_effects=True),
    )(*operands, *ride.arrays)
    return outs[:len(out_shape)], outs[len(out_shape):]


def _ride_ends(ride, riding, grid):
    if riding is None:
        return lambda: None
    first, last = None, None
    for axis, size in enumerate(grid):
        at0, at1 = pl.program_id(axis) == 0, pl.program_id(axis) == size - 1
        first = at0 if first is None else jnp.logical_and(first, at0)
        last = at1 if last is None else jnp.logical_and(last, at1)
    return ride.at_ends(riding, first, last)


def _sb_fwd_call(q, k, v, T, name, ride=None):
    L, W = q.shape
    scale = SB_HEAD_DIM ** -0.5
    grid = (W // LANES, L // T)

    def body(*refs):
        (q_ref, k_ref, v_ref), (o_ref,), _, riding = _split_refs(ride, 3, 1, refs)
        ride_done = _ride_ends(ride, riding, grid)
        i = pl.program_id(1)
        causal, strict_upper, _ = _sb_masks(T)
        q2 = q_ref[...] * scale
        masks = [_head_mask(h) for h in range(HEADS_PER_BLOCK)]
        qs = [(q2 * hm).astype(BF16) for hm in masks]
        zero = jnp.zeros((T, 1), F32)

        def cond(state):
            j, cs, _ = state
            return jnp.logical_and(j >= 0, _max_all(cs) > -SB_CUT)

        def step(state):
            j, cs, acc = state
            blocks = _sb_key_blocks(i, j, T, causal)
            ks = [k_ref[pl.ds(r0, T), :].astype(BF16) for r0, _ in blocks]
            vs = [v_ref[pl.ds(r0, T), :] for r0, _ in blocks]
            new_cs = []
            for hm, qh, c in zip(masks, qs, cs):
                for (_, valid), kb, vb in zip(blocks, ks, vs):
                    _, b, w = _sb_block(qh, kb, c, valid, strict_upper)
                    vh = (vb * hm).astype(BF16)
                    w_hi = w.astype(BF16)
                    w_lo = (w - w_hi.astype(F32)).astype(BF16)
                    acc = acc + _dg(w_hi, vh, 1, 0) + _dg(w_lo, vh, 1, 0)
                    c = c + jnp.sum(b, axis=1, keepdims=True)
                new_cs.append(c)
            return j - len(blocks), tuple(new_cs), acc

        _, _, acc = lax.while_loop(cond, step, (i, (zero,) * HEADS_PER_BLOCK, jnp.zeros((T, LANES), F32)))
        o_ref[...] = acc
        ride_done()

    (o,), rode = _ride_call(
        body, name, grid,
        [pl.BlockSpec((T, LANES), lambda p, i: (i, p)),
         pl.BlockSpec((L, LANES), lambda p, i: (0, p)),
         pl.BlockSpec((L, LANES), lambda p, i: (0, p))],
        [pl.BlockSpec((T, LANES), lambda p, i: (i, p))], [jax.ShapeDtypeStruct((L, W), F32)],
        (q, k, v), ride, ("parallel", "parallel"))
    return o, rode


def _sb_bwd_call(q, k, v, o, do, T, name, ride=None):
    L, W = q.shape
    scale = SB_HEAD_DIM ** -0.5
    grid = (W // LANES, L // T)

    def body(*refs):
        (q_ref, k_ref, v_ref, o_ref, do_ref), (dq_ref, dk_ref, dv_ref), _, riding = _split_refs(ride, 5, 3, refs)
        ride_done = _ride_ends(ride, riding, grid)
        i = pl.program_id(1)

        @pl.when(i == 0)
        def _():
            dk_ref[...] = jnp.zeros_like(dk_ref)
            dv_ref[...] = jnp.zeros_like(dv_ref)

        causal, strict_upper, upper = _sb_masks(T)
        q2 = q_ref[...] * scale
        do2 = do_ref[...]
        o2 = o_ref[...]
        masks = [_head_mask(h) for h in range(HEADS_PER_BLOCK)]
        qs = [(q2 * hm).astype(BF16) for hm in masks]
        dos = [(do2 * hm).astype(BF16) for hm in masks]
        totals = [jnp.sum(doh.astype(F32) * o2, axis=1, keepdims=True) for doh in dos]
        zero = jnp.zeros((T, 1), F32)

        def cond(state):
            j, cs, _, _ = state
            return jnp.logical_and(j >= 0, _max_all(cs) > -SB_CUT)

        def step(state):
            j, cs, rs, dq = state
            blocks = _sb_key_blocks(i, j, T, causal)
            kfs = [k_ref[pl.ds(r0, T), :] for r0, _ in blocks]
            ks = [kf.astype(BF16) for kf in kfs]
            vs = [v_ref[pl.ds(r0, T), :].astype(BF16) for r0, _ in blocks]
            dks = [jnp.zeros((T, LANES), F32) for _ in blocks]
            dvs = [jnp.zeros((T, LANES), F32) for _ in blocks]
            new_cs, new_rs = [], []
            for hm, qh, doh, total, c, r in zip(masks, qs, dos, totals, cs, rs):
                for n, ((_, valid), kf, kb, vb) in enumerate(zip(blocks, kfs, ks, vs)):
                    a, b, w = _sb_block(qh, kb, c, valid, strict_upper)
                    e = _dg(doh, vb, 1, 1) * w
                    before = total - (_hilo(e, upper) + r)
                    dz = jnp.where(valid, e * jnp.exp(b) - before * jnp.exp(a), 0.0).astype(BF16)
                    dq = dq + _dg(dz, kf * hm, 1, 0)
                    dks[n] = dks[n] + _dg(dz, qh, 0, 0)
                    dvs[n] = dvs[n] + _dg(w, doh, 0, 0)
                    c = c + jnp.sum(b, axis=1, keepdims=True)
                    r = r + jnp.sum(e, axis=1, keepdims=True)
                new_cs.append(c)
                new_rs.append(r)
            for (r0, _), dk, dv in zip(blocks, dks, dvs):
                dk_ref[pl.ds(r0, T), :] += dk
                dv_ref[pl.ds(r0, T), :] += dv
            return j - len(blocks), tuple(new_cs), tuple(new_rs), dq

        init = (i, (zero,) * HEADS_PER_BLOCK, (zero,) * HEADS_PER_BLOCK, jnp.zeros((T, LANES), F32))
        dq = lax.while_loop(cond, step, init)[3]
        dq_ref[...] = dq * scale
        ride_done()

    blk = pl.BlockSpec((T, LANES), lambda p, i: (i, p))
    full = pl.BlockSpec((L, LANES), lambda p, i: (0, p))
    return _ride_call(body, name, grid, [blk, full, full, blk, blk], [blk, full, full],
                      [jax.ShapeDtypeStruct((L, W), F32)] * 3, (q, k, v, o, do), ride, ("parallel", "arbitrary"))


def sb_attention(q, k, v, name, riders=NO_RIDERS):
    T = min(256, q.shape[0])

    @jax.custom_vjp
    def op(q, k, v, shards, slots):
        o, gathered = _sb_fwd_call(q, k, v, T, name + '_fwd', riders.gather_ride(shards))
        return o, tuple(gathered), riders.whole_slots(slots)

    def op_fwd(q, k, v, shards, slots):
        out = op(q, k, v, shards, slots)
        return out, (q, k, v, out[0], shards)

    def op_bwd(res, cts):
        q, k, v, o, shards = res
        do, _, slot_grads = cts
        grads, received = _sb_bwd_call(q, k, v, o, do, T, name + '_bwd', riders.scatter_ride(slot_grads))
        return (*grads, tuple(jnp.zeros_like(s) for s in shards), tuple(received))

    op.defvjp(op_fwd, op_bwd)
    return op(q, k, v, riders.shards, riders.slots)


def _dwconv_fwd_call(x, w, b, T, name):
    L, C = x.shape
    per = T // CONV_HALO
    lead = CONV_HALO - (CONV_WIDTH - 1)

    def body(x_ref, halo_ref, w_ref, b_ref, o_ref, buf):
        i = pl.program_id(0)
        buf[0:CONV_HALO, :] = jnp.where(i > 0, halo_ref[...], 0.0)
        buf[CONV_HALO:CONV_HALO + T, :] = x_ref[...]
        acc = jnp.zeros((T, C), F32) + b_ref[...]
        for j in range(CONV_WIDTH):
            acc = acc + w_ref[j:j + 1, :] * buf[lead + j:lead + j + T, :]
        o_ref[...] = acc

    return pl.pallas_call(
        body, name=name, grid=(L // T,),
        in_specs=[pl.BlockSpec((T, C), lambda i: (i, 0)),
                  pl.BlockSpec((CONV_HALO, C), lambda i: (jnp.maximum(i * per - 1, 0), 0)),
                  pl.BlockSpec(w.shape, lambda i: (0, 0)),
                  pl.BlockSpec(b.shape, lambda i: (0, 0))],
        out_specs=pl.BlockSpec((T, C), lambda i: (i, 0)),
        out_shape=jax.ShapeDtypeStruct((L, C), F32),
        scratch_shapes=[pltpu.VMEM((T + CONV_HALO, C), F32)],
        compiler_params=_cparams(("parallel",)),
    )(x, x, w, b)


def _dwconv_bwd_call(x, w, g, T, name):
    L, C = x.shape
    per = T // CONV_HALO
    n = L // T
    last_halo = L // CONV_HALO - 1
    lead = CONV_HALO - (CONV_WIDTH - 1)

    def body(x_ref, xh_ref, g_ref, gh_ref, w_ref, dx_ref, dw_ref, db_ref, bufx, bufg):
        i = pl.program_id(0)
        bufx[0:CONV_HALO, :] = jnp.where(i > 0, xh_ref[...], 0.0)
        bufx[CONV_HALO:CONV_HALO + T, :] = x_ref[...]
        gm = g_ref[...]
        bufg[0:T, :] = gm
        bufg[T:T + CONV_HALO, :] = jnp.where(i < n - 1, gh_ref[...], 0.0)
        acc = jnp.zeros((T, C), F32)
        for j in range(CONV_WIDTH):
            off = CONV_WIDTH - 1 - j
            acc = acc + w_ref[j:j + 1, :] * bufg[off:off + T, :]
        dx_ref[...] = acc

        @pl.when(i == 0)
        def _():
            dw_ref[...] = jnp.zeros_like(dw_ref)
            db_ref[...] = jnp.zeros_like(db_ref)

        for j in range(CONV_WIDTH):
            dw_ref[j:j + 1, :] += jnp.sum(gm * bufx[lead + j:lead + j + T, :], axis=0, keepdims=True)
        db_ref[...] += jnp.sum(gm, axis=0, keepdims=True)

    return pl.pallas_call(
        body, name=name, grid=(n,),
        in_specs=[pl.BlockSpec((T, C), lambda i: (i, 0)),
                  pl.BlockSpec((CONV_HALO, C), lambda i: (jnp.maximum(i * per - 1, 0), 0)),
                  pl.BlockSpec((T, C), lambda i: (i, 0)),
                  pl.BlockSpec((CONV_HALO, C), lambda i: (jnp.minimum((i + 1) * per, last_halo), 0)),
                  pl.BlockSpec(w.shape, lambda i: (0, 0))],
        out_specs=[pl.BlockSpec((T, C), lambda i: (i, 0)),
                   pl.BlockSpec(w.shape, lambda i: (0, 0)),
                   pl.BlockSpec((1, C), lambda i: (0, 0))],
        out_shape=[jax.ShapeDtypeStruct((L, C), F32), jax.ShapeDtypeStruct(w.shape, F32),
                   jax.ShapeDtypeStruct((1, C), F32)],
        scratch_shapes=[pltpu.VMEM((T + CONV_HALO, C), F32), pltpu.VMEM((T + CONV_HALO, C), F32)],
        compiler_params=_cparams(("arbitrary",)),
    )(x, x, g, g, w)


def dwconv(x, w, b, name):
    T = min(1024, x.shape[0])

    @jax.custom_vjp
    def op(x, w, b):
        return _dwconv_fwd_call(x, w, b, T, name + '_fwd')

    def op_fwd(x, w, b):
        return _dwconv_fwd_call(x, w, b, T, name + '_fwd'), (x, w)

    def op_bwd(res, g):
        x, w = res
        return tuple(_dwconv_bwd_call(x, w, g, T, name + '_bwd'))

    op.defvjp(op_fwd, op_bwd)
    return op(x, w, b)


def _ssm_fwd_call(u, ar, ai, bbr, bbi, cr, ci, d, T, name):
    L, C = u.shape
    S = SSM_LANES

    def body(u_ref, ar_ref, ai_ref, bbr_ref, bbi_ref, cr_ref, ci_ref, d_ref,
             y_ref, xr_ref, xi_ref, st_r, st_i, in_r, in_i, out_r, out_i):
        i = pl.program_id(0)

        @pl.when(i == 0)
        def _():
            st_r[...] = jnp.zeros_like(st_r)
            st_i[...] = jnp.zeros_like(st_i)

        u_blk = u_ref[...]
        xr_ref[...] = _dg(u_blk, bbr_ref[...], 1, 0)
        xi_ref[...] = _dg(u_blk, bbi_ref[...], 1, 0)
        a_r, a_i = ar_ref[...], ai_ref[...]

        def tile(t, carry):
            sr, si = carry
            r0 = pl.multiple_of(t * SUBLANES, SUBLANES)
            in_r[...] = xr_ref[pl.ds(r0, SUBLANES), :]
            in_i[...] = xi_ref[pl.ds(r0, SUBLANES), :]
            for r in range(SUBLANES):
                nr = a_r * sr - a_i * si + in_r[r:r + 1, :]
                ni = a_r * si + a_i * sr + in_i[r:r + 1, :]
                sr, si = nr, ni
                out_r[r:r + 1, :] = sr
                out_i[r:r + 1, :] = si
            xr_ref[pl.ds(r0, SUBLANES), :] = out_r[...]
            xi_ref[pl.ds(r0, SUBLANES), :] = out_i[...]
            return sr, si

        sr, si = lax.fori_loop(0, T // SUBLANES, tile, (st_r[0:1, :], st_i[0:1, :]))
        st_r[0:1, :] = sr
        st_i[0:1, :] = si
        y_ref[...] = (_dg(xr_ref[...], cr_ref[...], 1, 0) - _dg(xi_ref[...], ci_ref[...], 1, 0)
                      + d_ref[...] * u_blk)

    full = lambda a: pl.BlockSpec(a.shape, lambda i: (0, 0))
    return pl.pallas_call(
        body, name=name, grid=(L // T,),
        in_specs=[pl.BlockSpec((T, C), lambda i: (i, 0))] + [full(a) for a in (ar, ai, bbr, bbi, cr, ci, d)],
        out_specs=[pl.BlockSpec((T, C), lambda i: (i, 0)), pl.BlockSpec((T, S), lambda i: (i, 0)),
                   pl.BlockSpec((T, S), lambda i: (i, 0))],
        out_shape=[jax.ShapeDtypeStruct((L, C), F32), jax.ShapeDtypeStruct((L, S), F32),
                   jax.ShapeDtypeStruct((L, S), F32)],
        scratch_shapes=[pltpu.VMEM((SUBLANES, S), F32)] * 6,
        compiler_params=_cparams(("arbitrary",)),
    )(u, ar, ai, bbr, bbi, cr, ci, d)


def _ssm_bwd_call(u, xr, xi, dy, ar, ai, bbr, bbi, cr, ci, d, T, name):
    L, C = u.shape
    S = SSM_LANES
    n = L // T
    per = T // SUBLANES

    def body(u_ref, xr_ref, xi_ref, hr_ref, hi_ref, dy_ref, ar_ref, ai_ref, bbr_ref, bbi_ref, cr_ref, ci_ref,
             d_ref, du_ref, dar_ref, dai_ref, dbr_ref, dbi_ref, dcr_ref, dci_ref, dd_ref,
             lam_r, lam_i, prev_r, prev_i, st_r, st_i, in_r, in_i, out_r, out_i):
        i = pl.program_id(0)
        chunk = n - 1 - i

        @pl.when(i == 0)
        def _():
            st_r[...] = jnp.zeros_like(st_r)
            st_i[...] = jnp.zeros_like(st_i)
            for ref in (dar_ref, dai_ref, dbr_ref, dbi_ref, dcr_ref, dci_ref, dd_ref):
                ref[...] = jnp.zeros_like(ref)

        dy_blk = dy_ref[...]
        u_blk = u_ref[...]
        lam_r[...] = _dg(dy_blk, cr_ref[...], 1, 1)
        lam_i[...] = -_dg(dy_blk, ci_ref[...], 1, 1)
        dcr_ref[...] += _dg(xr_ref[...], dy_blk, 0, 0)
        dci_ref[...] -= _dg(xi_ref[...], dy_blk, 0, 0)
        a_r, a_i = ar_ref[...], ai_ref[...]

        def tile(k, carry):
            lr, li = carry
            r0 = pl.multiple_of((per - 1 - k) * SUBLANES, SUBLANES)
            in_r[...] = lam_r[pl.ds(r0, SUBLANES), :]
            in_i[...] = lam_i[pl.ds(r0, SUBLANES), :]
            for r in range(SUBLANES - 1, -1, -1):
                nr = in_r[r:r + 1, :] + a_r * lr + a_i * li
                ni = in_i[r:r + 1, :] + a_r * li - a_i * lr
                lr, li = nr, ni
                out_r[r:r + 1, :] = lr
                out_i[r:r + 1, :] = li
            lam_r[pl.ds(r0, SUBLANES), :] = out_r[...]
            lam_i[pl.ds(r0, SUBLANES), :] = out_i[...]
            return lr, li

        lr, li = lax.fori_loop(0, per, tile, (st_r[0:1, :], st_i[0:1, :]))
        st_r[0:1, :] = lr
        st_i[0:1, :] = li

        l_r, l_i = lam_r[...], lam_i[...]
        du_ref[...] = _dg(l_r, bbr_ref[...], 1, 1) + _dg(l_i, bbi_ref[...], 1, 1) + d_ref[...] * dy_blk
        dbr_ref[...] += _dg(u_blk, l_r, 0, 0)
        dbi_ref[...] += _dg(u_blk, l_i, 0, 0)
        dd_ref[...] += jnp.sum(dy_blk * u_blk, axis=0, keepdims=True)

        prev_r[0:SUBLANES, :] = jnp.where(chunk > 0, hr_ref[...], 0.0)
        prev_i[0:SUBLANES, :] = jnp.where(chunk > 0, hi_ref[...], 0.0)
        prev_r[SUBLANES:SUBLANES + T, :] = xr_ref[...]
        prev_i[SUBLANES:SUBLANES + T, :] = xi_ref[...]
        p_r = prev_r[SUBLANES - 1:SUBLANES - 1 + T, :]
        p_i = prev_i[SUBLANES - 1:SUBLANES - 1 + T, :]
        dar_ref[...] += jnp.sum(l_r * p_r + l_i * p_i, axis=0, keepdims=True)
        dai_ref[...] += jnp.sum(l_i * p_r - l_r * p_i, axis=0, keepdims=True)

    rev = lambda w: pl.BlockSpec((T, w), lambda i: (n - 1 - i, 0))
    halo = pl.BlockSpec((SUBLANES, S), lambda i: (jnp.maximum((n - 1 - i) * per - 1, 0), 0))
    full = lambda a: pl.BlockSpec(a.shape, lambda i: (0, 0))
    params = (ar, ai, bbr, bbi, cr, ci, d)
    return pl.pallas_call(
        body, name=name, grid=(n,),
        in_specs=[rev(C), rev(S), rev(S), halo, halo, rev(C)] + [full(a) for a in params],
        out_specs=[rev(C)] + [full(a) for a in params],
        out_shape=[jax.ShapeDtypeStruct((L, C), F32)] + [jax.ShapeDtypeStruct(a.shape, F32) for a in params],
        scratch_shapes=[pltpu.VMEM((T, S), F32), pltpu.VMEM((T, S), F32),
                        pltpu.VMEM((T + SUBLANES, S), F32), pltpu.VMEM((T + SUBLANES, S), F32)]
        + [pltpu.VMEM((SUBLANES, S), F32)] * 6,
        compiler_params=_cparams(("arbitrary",)),
    )(u, xr, xi, xr, xi, dy, *params)


def ssm_core(u, ar, ai, bbr, bbi, cr, ci, d, name):
    T = min(256, u.shape[0])

    @jax.custom_vjp
    def op(u, ar, ai, bbr, bbi, cr, ci, d):
        return _ssm_fwd_call(u, ar, ai, bbr, bbi, cr, ci, d, T, name + '_fwd')[0]

    def op_fwd(u, ar, ai, bbr, bbi, cr, ci, d):
        y, xr, xi = _ssm_fwd_call(u, ar, ai, bbr, bbi, cr, ci, d, T, name + '_fwd')
        return y, (u, xr, xi, ar, ai, bbr, bbi, cr, ci, d)

    def op_bwd(res, dy):
        u, xr, xi, ar, ai, bbr, bbi, cr, ci, d = res
        return tuple(_ssm_bwd_call(u, xr, xi, dy, ar, ai, bbr, bbi, cr, ci, d, T, name + '_bwd'))

    op.defvjp(op_fwd, op_bwd)
    return op(u, ar, ai, bbr, bbi, cr, ci, d)


@jax.custom_vjp
def _block_diag(blocks):
    G, R, Cc = blocks.shape
    eye = jnp.eye(G, dtype=blocks.dtype)
    return (blocks[:, :, None, :] * eye[:, None, :, None]).reshape(G * R, G * Cc)


def _block_diag_fwd(blocks):
    return _block_diag(blocks), blocks.shape


def _block_diag_bwd(shape, g):
    G, R, Cc = shape
    on_diagonal = jnp.eye(G, dtype=bool)[:, None, :, None]
    return (jnp.sum(jnp.where(on_diagonal, g.reshape(G, R, G, Cc), 0.0), axis=2),)


_block_diag.defvjp(_block_diag_fwd, _block_diag_bwd)


def ssm_discretise(lam_re, lam_im, log_dt, b_re, b_im, c_re, c_im):
    dt = jnp.exp(log_dt)[:, None]
    mag = jnp.exp(lam_re * dt)
    ar, ai = mag * jnp.cos(lam_im * dt), mag * jnp.sin(lam_im * dt)
    den = lam_re * lam_re + lam_im * lam_im
    fr = ((ar - 1.0) * lam_re + ai * lam_im) / den
    fi = (ai * lam_re - (ar - 1.0) * lam_im) / den
    bbr = fr[..., None] * b_re - fi[..., None] * b_im
    bbi = fr[..., None] * b_im + fi[..., None] * b_re
    return (ar.reshape(1, SSM_LANES), ai.reshape(1, SSM_LANES),
            _block_diag(bbr.transpose(0, 2, 1)), _block_diag(bbi.transpose(0, 2, 1)),
            _block_diag(c_re.transpose(0, 2, 1)), _block_diag(c_im.transpose(0, 2, 1)))


def split_columns(p, bounds):
    @jax.custom_vjp
    def op(p):
        return tuple(p[:, lo:hi] for lo, hi in zip(bounds[:-1], bounds[1:]))

    def op_fwd(p):
        return op(p), None

    def op_bwd(_, gs):
        return (jnp.concatenate(gs, axis=1),)

    op.defvjp(op_fwd, op_bwd)
    return op(p)


DEPTH = 2
EARLY = ['w_in', 'conv_pw2_w', 'ssm_glu_w']
LATE = [n for n in MATRICES if n not in EARLY]
FFN = ['ffn_w_in', 'ffn_w_out']
AFTER_XA_Q = ['xa_wk', 'xa_wv', 'xa_wo']
GATHER_AT = {
    'start': [('w_in', 0)],
    'win0': [('ffn_w_out', 0)],
    'sb0': [(n, 0) for n in MATRICES if n not in ('w_in', 'ffn_w_out')],
    'ffn0': [(n, 1) for n in MATRICES if n not in FFN + AFTER_XA_Q],
    'sb1': [(n, 1) for n in FFN + AFTER_XA_Q],
}
SCATTER_AT = {
    'sb1': [(n, 1) for n in LATE],
    'ffn0': [(n, 1) for n in EARLY],
    'sb0': [(n, 0) for n in LATE],
    'win0': [(n, 0) for n in EARLY if n != 'w_in'],
}
OWN_AT = {'win0': ('w_in', 0)}


def _assemble(name, gathered):
    if name not in SLOTTED:
        return gathered
    return jnp.concatenate([gathered[j] for j in range(N_CHIPS)], axis=SHARD_AXIS[name] - 1)


def local_loss(slots, w, mats, shards, x, mem, target):
    mats = dict(mats)
    slot = {key: slots[key] for key in OWN_AT.values()}
    s1, s2, s3 = SB_WIDTH, 2 * SB_WIDTH, 3 * SB_WIDTH
    s4 = s3 + 2 * CONV_CH

    def riders_at(host):
        return Riders(gather=[(n, shards[(n, l)]) for n, l in GATHER_AT.get(host, [])],
                      scatter=[(n, slots[(n, l)]) for n, l in SCATTER_AT[host]],
                      own=OWN_AT[host][0] if host in OWN_AT else None)

    def take(host, gathered, handed):
        for (n, l), g in zip(GATHER_AT.get(host, []), gathered):
            mats[(n, l)] = _assemble(n, g)
        for key, s in zip(SCATTER_AT[host], handed):
            slot[key] = s

    def linear_(x, n, l, name):
        return linear(x, mats[(n, l)], slot[(n, l)], name)

    def fused_(f, rows, params, n, l, name, residual=None, host=None, carry=(), block_rows=None):
        y, carried, gathered, handed = rowwise_linear(
            f, rows, params, [], mats[(n, l)], slot[(n, l)], name, residual,
            riders_at(host) if host else NO_RIDERS, carry, block_rows)
        if host:
            take(host, gathered, handed)
        return (y,) + tuple(carried)

    def gain(n, l):
        return w[n][l].reshape(1, -1)

    for l in range(DEPTH):
        tag = 'l%d_' % l
        p, x = fused_(_rms, [x], [gain('norm_mix_g', l)], 'w_in', l, tag + 'w_in', carry=(0,),
                      host='win0' if l == 0 else None)
        q, k, v, u_conv, u_ssm = split_columns(p, (0, s1, s2, s3, s4, p.shape[1]))
        q = groupnorm(q, w['sb_q_norm_g'][l], SB_HEAD_DIM, tag + 'q_norm')
        k = groupnorm(k, w['sb_k_norm_g'][l], SB_HEAD_DIM, tag + 'k_norm')
        o_sb, gathered, handed = sb_attention(q, k, v, tag + 'sb', riders_at('sb%d' % l))
        take('sb%d' % l, gathered, handed)

        dw_w = jnp.pad(w['conv_dw_w'][l], ((0, CONV_HALO - CONV_WIDTH), (0, 0)))
        hc = dwconv(glu(u_conv, tag + 'conv_glu'), dw_w, w['conv_dw_b'][l].reshape(1, -1), tag + 'dwconv')
        o_conv, = fused_(ln_silu_block, [hc], [gain('conv_ln_g', l), gain('conv_ln_b', l)], 'conv_pw2_w', l,
                         tag + 'pw2')

        ar, ai, bbr, bbi, cr, ci = ssm_discretise(
            w['ssm_lam_re'][l], w['ssm_lam_im'][l], w['ssm_log_dt'][l], w['ssm_b_re'][l], w['ssm_b_im'][l],
            w['ssm_c_re'][l], w['ssm_c_im'][l])
        y = ssm_core(u_ssm, ar, ai, bbr, bbi, cr, ci, w['ssm_d'][l].reshape(1, -1), tag + 'ssm')
        o_ssm = glu(linear_(y, 'ssm_glu_w', l, tag + 'ssm_glu_w'), tag + 'ssm_glu')

        x, = fused_(branch_norms_block, [o_sb, o_conv, o_ssm], [gain('branch_norm_g', l)], 'w_out', l,
                    tag + 'w_out', residual=x)

        q_raw, x = fused_(_rms, [x], [gain('norm_xa_g', l)], 'xa_wq', l, tag + 'xa_wq', carry=(0,))
        hm = rmsnorm(mem, w['norm_mem_g'][l], tag + 'norm_mem')
        qx = groupnorm(q_raw, w['xa_q_norm_g'][l], XA_HEAD_DIM, tag + 'xa_qn')
        kx = groupnorm(linear_(hm, 'xa_wk', l, tag + 'xa_wk'), w['xa_k_norm_g'][l], XA_HEAD_DIM, tag + 'xa_kn')
        vx = linear_(hm, 'xa_wv', l, tag + 'xa_wv')
        x, = fused_(xa_core_block, [qx], [kx, vx], 'xa_wo', l, tag + 'xa_wo', residual=x,
                    block_rows=min(256, qx.shape[0]))

        host = 'ffn0' if l == 0 else None
        x, gathered, handed = ffn(x, gain('norm_ffn_g', l), mats[('ffn_w_in', l)], slot[('ffn_w_in', l)],
                                  mats[('ffn_w_out', l)], slot[('ffn_w_out', l)], tag + 'ffn',
                                  riders_at(host) if host else NO_RIDERS)
        if host:
            take(host, gathered, handed)
    return jnp.sum(loss_rows(x, target, 'loss'))


def local_step(w, mats, shards, x, mem, target):
    def shard_shape(key):
        return shards[key].shape if key in shards else _shard_shape(key[0], mats[key].shape)

    keys = [key for keys in SCATTER_AT.values() for key in keys] + list(OWN_AT.values())
    slots = {key: jnp.zeros((N_CHIPS,) + shard_shape(key), BF16) for key in keys}
    loss, (g_mats, g_w, gx) = jax.value_and_grad(local_loss, argnums=(0, 1, 4))(
        slots, w, mats, shards, x, mem, target)
    return loss, gx, g_w, g_mats


PACK_ROWS = 2048


PIECE_ROWS = 16


def _piece_rows(size):
    rows = -(-size // LANES)
    return rows, -(-rows // PIECE_ROWS) * PIECE_ROWS


def pack(arrays, dtype):
    parts, total = [], 0
    for a in arrays:
        rows, padded = _piece_rows(a.size)
        a = a.astype(dtype)
        if a.size % LANES:
            a = jnp.pad(a.reshape(-1), (0, rows * LANES - a.size))
        a = a.reshape(rows, LANES)
        if padded != rows:
            a = jnp.pad(a, ((0, padded - rows), (0, 0)))
        parts.append(a)
        total += padded
    tail = -total % PACK_ROWS
    if tail:
        parts.append(jnp.zeros((tail, LANES), dtype))
    return jnp.concatenate(parts, axis=0)


def unpack(packed, shapes):
    out, off = [], 0
    for s in shapes:
        size = math.prod(s)
        rows, padded = _piece_rows(size)
        piece = packed[off:off + rows]
        if size % LANES:
            piece = piece.reshape(-1)[:size]
        out.append(piece.reshape(s))
        off += padded
    return out


def _mesh_pos():
    return lax.axis_index("x"), lax.axis_index("y"), lax.axis_index("c")


def _exchange_xy(n_arrays, src_of, dst_of, sems, wait):
    send_sems, recv_sems, local_sems = sems
    x, y, c = _mesh_pos()
    me = 2 * x + y
    peers = [(1 - x, y), (x, 1 - y), (1 - x, 1 - y)]
    for k in range(n_arrays):
        own = pltpu.make_async_copy(src_of(k, me), dst_of(k, me), local_sems.at[k])
        if wait:
            own.wait()
        else:
            own.start()
        for p, (px, py) in enumerate(peers):
            out = pltpu.make_async_remote_copy(
                src_ref=src_of(k, 2 * px + py), dst_ref=dst_of(k, me), send_sem=send_sems.at[3 * k + p],
                recv_sem=recv_sems.at[3 * k + p], device_id=(px, py, c), device_id_type=MESH)
            if wait:
                pltpu.make_async_remote_copy(
                    src_ref=src_of(k, me), dst_ref=dst_of(k, 2 * px + py), send_sem=send_sems.at[3 * k + p],
                    recv_sem=recv_sems.at[3 * k + p], device_id=(px, py, c), device_id_type=MESH).wait_recv()
                out.wait_send()
            else:
                out.start()


class Ride:
    def __init__(self, arrays, out_shapes, src_of, dst_of):
        self.arrays, self.out_shapes = list(arrays), list(out_shapes)
        self._src_of, self._dst_of = src_of, dst_of
        n = len(self.arrays)
        self.in_specs = [pl.BlockSpec(memory_space=pl.ANY)] * n
        self.out_specs = [pl.BlockSpec(memory_space=pl.ANY)] * len(self.out_shapes)
        self.out_shape = [jax.ShapeDtypeStruct(s, BF16) for s in self.out_shapes]
        self.scratch = [pltpu.SemaphoreType.DMA((3 * n,)), pltpu.SemaphoreType.DMA((3 * n,)),
                        pltpu.SemaphoreType.DMA((n,))]

    def run(self, parts, wait):
        ins, outs, sems = parts
        _exchange_xy(len(self.arrays), lambda k, chip: self._src_of(ins, k, chip),
                     lambda k, chip: self._dst_of(outs, k, chip), sems, wait)

    def at_ends(self, parts, first, last):
        pl.when(first)(lambda: self.run(parts, False))

        def finish():
            pl.when(last)(lambda: self.run(parts, True))
        return finish


def _split_refs(ride, n_in, n_out, refs):
    if ride is None:
        return refs[:n_in], refs[n_in:n_in + n_out], refs[n_in + n_out:], None
    ni, no = len(ride.arrays), len(ride.out_shapes)
    b = n_in + ni
    c = b + n_out
    d = c + no
    return refs[:n_in], refs[b:c], refs[d:len(refs) - 3], (refs[n_in:b], refs[c:d], refs[len(refs) - 3:])


def run_ride(ride, name):
    def body(*refs):
        parts = _split_refs(ride, 0, 0, refs)[3]
        ride.run(parts, False)
        ride.run(parts, True)

    return pl.pallas_call(
        body, name=name, in_specs=ride.in_specs, out_specs=ride.out_specs, out_shape=ride.out_shape,
        scratch_shapes=ride.scratch, compiler_params=pltpu.CompilerParams(has_side_effects=True),
    )(*ride.arrays)


SLOTTED = ('w_in', 'taps')


def _part(ref, axis, chip, size):
    start = pl.multiple_of(chip * size, size)
    index = [slice(None)] * len(ref.shape)
    index[axis] = pl.ds(start, size)
    return ref.at[tuple(index)]


def _whole_shape(name, shard_shape):
    s = list(shard_shape)
    s[SHARD_AXIS[name] - 1] *= N_CHIPS
    return tuple(s)


def _shard_shape(name, whole_shape):
    s = list(whole_shape)
    s[SHARD_AXIS[name] - 1] //= N_CHIPS
    return tuple(s)


def gather_ride(names, shards):
    def out_shape(k):
        return (N_CHIPS,) + shards[k].shape if names[k] in SLOTTED else _whole_shape(names[k], shards[k].shape)

    def dst_of(outs, k, chip):
        if names[k] in SLOTTED:
            return outs[k].at[chip]
        axis = SHARD_AXIS[names[k]] - 1
        return _part(outs[k], axis, chip, shards[k].shape[axis])

    return Ride(shards, [out_shape(k) for k in range(len(shards))], lambda ins, k, chip: ins[k], dst_of)


def scatter_ride(names, grads):
    def shard_shape(k):
        return grads[k].shape[1:] if names[k] in SLOTTED else _shard_shape(names[k], grads[k].shape)

    def src_of(ins, k, chip):
        if names[k] in SLOTTED:
            return ins[k].at[chip]
        axis = SHARD_AXIS[names[k]] - 1
        return _part(ins[k], axis, chip, shard_shape(k)[axis])

    return Ride(grads, [(N_CHIPS,) + shard_shape(k) for k in range(len(grads))], src_of,
                lambda outs, k, chip: outs[k].at[chip])


def _cut_for_chips(name, g):
    if name not in SLOTTED:
        return g
    axis = SHARD_AXIS[name] - 1
    cut = g.shape[:axis] + (N_CHIPS, g.shape[axis] // N_CHIPS) + g.shape[axis + 1:]
    return jnp.moveaxis(g.reshape(cut), axis, 0)


class SwapRide(Ride):
    def __init__(self, arrays):
        self.arrays = list(arrays)
        self.out_shapes = [a.shape for a in self.arrays]
        n = len(self.arrays)
        self.in_specs = [pl.BlockSpec(memory_space=pl.ANY)] * n
        self.out_specs = [pl.BlockSpec(memory_space=pl.ANY)] * n
        self.out_shape = [jax.ShapeDtypeStruct(a.shape, a.dtype) for a in self.arrays]
        self.scratch = [pltpu.SemaphoreType.DMA((n,)), pltpu.SemaphoreType.DMA((n,)), pltpu.SemaphoreType.DMA((1,))]

    def run(self, parts, wait):
        ins, outs, (send_sems, recv_sems, _) = parts
        x, y, c = _mesh_pos()
        for k in range(len(self.arrays)):
            cp = pltpu.make_async_remote_copy(
                src_ref=ins[k], dst_ref=outs[k], send_sem=send_sems.at[k], recv_sem=recv_sems.at[k],
                device_id=(x, y, 1 - c), device_id_type=MESH)
            if wait:
                cp.wait()
            else:
                cp.start()


def allreduce_small(buf, name):
    R = buf.shape[0]
    half = R // 2

    def body(in_ref, sum_ref, other_ref, chip_ref, got_ref, send_sems, recv_sems):
        x, y, c = _mesh_pos()
        sibling = (x, y, 1 - c)
        peers = [(1 - x, y, c), (x, 1 - y, c), (1 - x, 1 - y, c)]

        def copy(k, src, dst, to):
            return pltpu.make_async_remote_copy(src_ref=src, dst_ref=dst, send_sem=send_sems.at[k],
                                                recv_sem=recv_sems.at[k], device_id=to, device_id_type=MESH)

        swap = copy(0, in_ref, other_ref, sibling)
        swap.start()
        swap.wait()
        chip_ref[...] = in_ref[...] + other_ref[...]

        mine = chip_ref.at[pl.ds(pl.multiple_of(c * half, SUBLANES), half), :]
        sends = [copy(1 + k, mine, got_ref.at[k], to) for k, to in enumerate(peers)]
        for cp in sends:
            cp.start()
        for cp in sends:
            cp.wait()
        total = (mine[...] + got_ref[0]) + (got_ref[1] + got_ref[2])

        done = sum_ref.at[pl.ds(pl.multiple_of(c * half, SUBLANES), half), :]
        done[...] = total
        back = copy(4, done, done, sibling)
        back.start()
        back.wait()

    vmem = pl.BlockSpec(memory_space=pltpu.VMEM)
    return pl.pallas_call(
        body, name=name, in_specs=[vmem], out_specs=vmem,
        out_shape=jax.ShapeDtypeStruct((R, LANES), F32),
        scratch_shapes=[pltpu.VMEM((R, LANES), F32), pltpu.VMEM((R, LANES), F32),
                        pltpu.VMEM((3, half, LANES), F32),
                        pltpu.SemaphoreType.DMA((5,)), pltpu.SemaphoreType.DMA((5,))],
        compiler_params=pltpu.CompilerParams(has_side_effects=True, vmem_limit_bytes=VMEM_LIMIT),
    )(buf)


def _adamw_update(g, w, m, v):
    m2 = ADAM_B1 * m + (1.0 - ADAM_B1) * g
    v2 = ADAM_B2 * v + (1.0 - ADAM_B2) * (g * g)
    m_hat = m2 / (1.0 - ADAM_B1 ** ADAM_STEP)
    v_hat = v2 / (1.0 - ADAM_B2 ** ADAM_STEP)
    return -ADAM_LR * (m_hat / (jnp.sqrt(v_hat) + ADAM_EPS) + ADAM_WD * w), m2, v2


def adamw_matrix(layer, mine, other, w, m, v, so_far, name):
    _, rows, cols = w.shape
    T = 16
    while rows % (2 * T) == 0 and 2 * T * cols <= 128 * 1024:
        T *= 2

    def body(mine_ref, other_ref, w_ref, m_ref, v_ref, *rest):
        g_out, d_out, m_out, v_out = rest[-4:]

        def total(ref):
            acc = ref[0].astype(F32)
            for k in range(1, N_CHIPS):
                acc = acc + ref[k].astype(F32)
            return acc

        g = total(mine_ref) + total(other_ref)
        g_out[...] = g
        d_out[...], m_out[...], v_out[...] = _adamw_update(g, w_ref[...], m_ref[...], v_ref[...])

    slots = pl.BlockSpec((N_CHIPS, T, cols), lambda i: (0, i, 0))
    spec = pl.BlockSpec((None, T, cols), lambda i: (layer, i, 0))
    filled = [] if so_far is None else list(so_far)
    return pl.pallas_call(
        body, name=name, grid=(rows // T,),
        in_specs=[slots, slots, spec, spec, spec] + [pl.BlockSpec(memory_space=pl.ANY)] * len(filled),
        out_specs=[spec] * 4,
        out_shape=[jax.ShapeDtypeStruct(w.shape, F32)] * 4,
        input_output_aliases={5 + j: j for j in range(len(filled))},
        compiler_params=_cparams(("parallel",)),
    )(mine, other, w, m, v, *filled)


def adamw_small(gs, ws, ms, vs, name):
    n = len(gs)

    def body(*refs):
        for k in range(n):
            g, w, m, v = (refs[j * n + k][...] for j in range(4))
            d_out, m_out, v_out = (refs[(4 + j) * n + k] for j in range(3))
            d_out[...], m_out[...], v_out[...] = _adamw_update(g, w, m, v)

    vmem = pl.BlockSpec(memory_space=pltpu.VMEM)
    outs = pl.pallas_call(
        body, name=name,
        in_specs=[vmem] * (4 * n), out_specs=[vmem] * (3 * n),
        out_shape=[jax.ShapeDtypeStruct(w.shape, F32) for w in ws] * 3,
        compiler_params=_cparams(),
    )(*gs, *ws, *ms, *vs)
    return outs[:n], outs[n:2 * n], outs[2 * n:]


def _shard_of(full, axis, chip):
    size = full.shape[axis] // N_CHIPS
    return lax.slice_in_dim(full, chip * size, (chip + 1) * size, axis=axis)


def kernel(x, mem, norm_mix_g, w_in, sb_q_norm_g, sb_k_norm_g, conv_dw_w, conv_dw_b, conv_ln_g, conv_ln_b, conv_pw2_w, ssm_lam_re, ssm_lam_im, ssm_log_dt, ssm_b_re, ssm_b_im, ssm_c_re, ssm_c_im, ssm_d, ssm_glu_w, branch_norm_g, w_out, norm_xa_g, norm_mem_g, xa_wq, xa_wk, xa_wv, xa_q_norm_g, xa_k_norm_g, xa_wo, norm_ffn_g, ffn_w_in, ffn_w_out, loss_target, m_norm_mix_g, m_w_in, m_sb_q_norm_g, m_sb_k_norm_g, m_conv_dw_w, m_conv_dw_b, m_conv_ln_g, m_conv_ln_b, m_conv_pw2_w, m_ssm_lam_re, m_ssm_lam_im, m_ssm_log_dt, m_ssm_b_re, m_ssm_b_im, m_ssm_c_re, m_ssm_c_im, m_ssm_d, m_ssm_glu_w, m_branch_norm_g, m_w_out, m_norm_xa_g, m_norm_mem_g, m_xa_wq, m_xa_wk, m_xa_wv, m_xa_q_norm_g, m_xa_k_norm_g, m_xa_wo, m_norm_ffn_g, m_ffn_w_in, m_ffn_w_out, v_norm_mix_g, v_w_in, v_sb_q_norm_g, v_sb_k_norm_g, v_conv_dw_w, v_conv_dw_b, v_conv_ln_g, v_conv_ln_b, v_conv_pw2_w, v_ssm_lam_re, v_ssm_lam_im, v_ssm_log_dt, v_ssm_b_re, v_ssm_b_im, v_ssm_c_re, v_ssm_c_im, v_ssm_d, v_ssm_glu_w, v_branch_norm_g, v_w_out, v_norm_xa_g, v_norm_mem_g, v_xa_wq, v_xa_wk, v_xa_wv, v_xa_q_norm_g, v_xa_k_norm_g, v_xa_wo, v_norm_ffn_g, v_ffn_w_in, v_ffn_w_out):
    given = dict(locals())
    w = {n: given[n] for n in WEIGHTS}
    m = {n: given['m_' + n] for n in WEIGHTS}
    v = {n: given['v_' + n] for n in WEIGHTS}

    depth = w_in.shape[0]
    chip = 2 * lax.axis_index("x") + lax.axis_index("y")

    assert depth == DEPTH
    taps_bits = lax.bitcast_convert_type(conv_dw_w, BF16)
    shards = {(n, l): w[n][l].astype(BF16) for n in MATRICES for l in range(depth)}
    first = [shards.pop(key) for key in GATHER_AT['start']]
    gathered = run_ride(gather_ride([n for n, _ in GATHER_AT['start']] + ['taps'], first + [taps_bits]),
                        'gather_first')
    mats = {key: _assemble(key[0], g) for key, g in zip(GATHER_AT['start'], gathered)}
    taps = jnp.concatenate([lax.bitcast_convert_type(gathered[-1][j], F32) for j in range(N_CHIPS)], axis=2)
    local_w = {n: w[n] for n in REPLICATED}
    local_w['conv_dw_w'] = taps

    loss, gx, g_w, mine = local_step(local_w, mats, shards, x[0], mem[0], loss_target[0])

    keys = [(n, l) for n in MATRICES for l in range(depth)]
    other = dict(zip(keys, run_ride(SwapRide([mine[key] for key in keys]), 'swap_cores')))
    outs = {}
    for n in MATRICES:
        res = None
        for l in range(depth):
            res = adamw_matrix(l, mine[(n, l)], other[(n, l)], w[n], m[n], v[n], res, 'adamw_%s_%d' % (n, l))
        outs['grad_' + n], outs['delta_' + n], outs['new_m_' + n], outs['new_v_' + n] = res

    reduced = allreduce_small(pack([g_w[n] for n in REPLICATED] + [g_w['conv_dw_w'], loss.reshape(1)], F32),
                              'allreduce_small')
    reduced = unpack(reduced, [w[n].shape for n in REPLICATED] + [taps.shape, (1,)])
    total_loss = reduced[-1].reshape(())
    tap_cols = conv_dw_w.shape[2]
    reduced[-2] = lax.dynamic_slice_in_dim(reduced[-2], chip * tap_cols, tap_cols, axis=2)
    small_names = REPLICATED + ['conv_dw_w']
    deltas, new_ms, new_vs = adamw_small(reduced[:-1], [w[n] for n in small_names], [m[n] for n in small_names],
                                         [v[n] for n in small_names], 'adamw_small')
    for k, n in enumerate(small_names):
        outs['grad_' + n], outs['delta_' + n] = reduced[k], deltas[k]
        outs['new_m_' + n], outs['new_v_' + n] = new_ms[k], new_vs[k]
    return (total_loss, gx[None], *[outs['grad_' + n] for n in WEIGHTS], *[outs['delta_' + n] for n in WEIGHTS],
            *[outs['new_m_' + n] for n in WEIGHTS], *[outs['new_v_' + n] for n in WEIGHTS])
```

```python
import math

import jax
import jax.numpy as jnp
from jax import lax
from jax.experimental import pallas as pl
from jax.experimental.pallas import tpu as pltpu

F32 = jnp.float32
BF16 = jnp.bfloat16
MESH = pl.DeviceIdType.MESH

EPS = 1e-6
LANES = 128
SUBLANES = 8
VMEM_LIMIT = 56 * 1024 * 1024

SB_HEAD_DIM = 64
SB_WIDTH = 512
CONV_CH = 256
CONV_WIDTH = 31
CONV_HALO = 32
SSM_CH = 256
SSM_GROUPS = 16
SSM_GROUP = 16
SSM_STATE = 64
SSM_LANES = SSM_GROUPS * SSM_STATE
XA_HEADS = 4
XA_HEAD_DIM = 256
SB_CUT = 110.0

ADAM_LR = 0.001
ADAM_B1 = 0.9
ADAM_B2 = 0.999
ADAM_EPS = 1e-08
ADAM_WD = 0.01
ADAM_STEP = 10

WEIGHTS = ['norm_mix_g', 'w_in', 'sb_q_norm_g', 'sb_k_norm_g', 'conv_dw_w', 'conv_dw_b', 'conv_ln_g',
           'conv_ln_b', 'conv_pw2_w', 'ssm_lam_re', 'ssm_lam_im', 'ssm_log_dt', 'ssm_b_re', 'ssm_b_im',
           'ssm_c_re', 'ssm_c_im', 'ssm_d', 'ssm_glu_w', 'branch_norm_g', 'w_out', 'norm_xa_g',
           'norm_mem_g', 'xa_wq', 'xa_wk', 'xa_wv', 'xa_q_norm_g', 'xa_k_norm_g', 'xa_wo', 'norm_ffn_g',
           'ffn_w_in', 'ffn_w_out']
SHARD_AXIS = {'w_in': 2, 'conv_dw_w': 2, 'conv_pw2_w': 1, 'ssm_glu_w': 2, 'w_out': 1, 'xa_wq': 1,
              'xa_wk': 1, 'xa_wv': 1, 'xa_wo': 1, 'ffn_w_in': 2, 'ffn_w_out': 1}
MATRICES = [n for n in WEIGHTS if n in SHARD_AXIS and n != 'conv_dw_w']
REPLICATED = [n for n in WEIGHTS if n not in SHARD_AXIS]
N_CHIPS = 4
N_DEV = 8


def _cparams(sem=None, **kw):
    if sem is not None:
        kw['dimension_semantics'] = sem
    return pltpu.CompilerParams(vmem_limit_bytes=VMEM_LIMIT, **kw)


def _pick(n, target):
    best = None
    d = LANES
    while d <= min(n, target):
        if n % d == 0:
            best = d
        d += LANES
    return best if best is not None else n


def _rows_for(n_rows, width):
    t = 512
    while t > 8 and t * width > 768 * 1024:
        t //= 2
    return min(t, n_rows)


def _dg(a, b, ca, cb):
    return lax.dot_general(a.astype(BF16), b.astype(BF16), (((ca,), (cb,)), ((), ())),
                           preferred_element_type=F32)


@jax.custom_vjp
def bdot_nn(a, b):
    return _dg(a, b, 1, 0)


def _bdot_nn_fwd(a, b):
    return _dg(a, b, 1, 0), (a, b)


def _bdot_nn_bwd(res, g):
    a, b = res
    return _dg(g, b, 1, 1), _dg(a, g, 0, 0)


bdot_nn.defvjp(_bdot_nn_fwd, _bdot_nn_bwd)


@jax.custom_vjp
def bdot_nt(a, b):
    return _dg(a, b, 1, 1)


def _bdot_nt_fwd(a, b):
    return _dg(a, b, 1, 1), (a, b)


def _bdot_nt_bwd(res, g):
    a, b = res
    return _dg(g, b, 1, 0), _dg(g, a, 0, 0)


bdot_nt.defvjp(_bdot_nt_fwd, _bdot_nt_bwd)


def mm(a, b, mode, name, out_dtype=F32, add=None, ride=None):
    if mode == 'nn':
        M, K = a.shape
        N = b.shape[1]
    elif mode == 'nt':
        M, K = a.shape
        N = b.shape[0]
    else:
        K, M = a.shape
        N = b.shape[1]
    if mode == 'tn':
        tm, tn, tk = _pick(M, 1536), _pick(N, 2816), _pick(K, 512)
    else:
        tm, tn = _pick(M, 1024), _pick(N, 1536)
        tk = K if K <= 2816 else _pick(K, 1536)
    nk = K // tk
    ca, cb = {'nn': (1, 0), 'nt': (1, 1), 'tn': (0, 0)}[mode]

    grid = (M // tm, N // tn, nk)

    def body(*refs):
        ins, (o_ref,), scratch, riding = _split_refs(ride, 3 if add is not None else 2, 1, refs)
        a_ref, b_ref = ins[:2]
        add_ref = ins[2] if add is not None else None
        i, j, k = pl.program_id(0), pl.program_id(1), pl.program_id(2)
        ride_done = None
        if riding is not None:
            first = jnp.logical_and(i == 0, jnp.logical_and(j == 0, k == 0))
            last = jnp.logical_and(i == grid[0] - 1, jnp.logical_and(j == grid[1] - 1, k == nk - 1))
            ride_done = ride.at_ends(riding, first, last)

        def finish(acc):
            if add_ref is not None:
                acc = acc + add_ref[...]
            o_ref[...] = acc.astype(out_dtype)

        if nk == 1:
            finish(_dg(a_ref[...], b_ref[...], ca, cb))
        else:
            acc_ref, = scratch

            @pl.when(k == 0)
            def _():
                acc_ref[...] = jnp.zeros_like(acc_ref)

            acc_ref[...] += _dg(a_ref[...], b_ref[...], ca, cb)

            @pl.when(k == nk - 1)
            def _():
                finish(acc_ref[...])

        if ride_done is not None:
            ride_done()

    if mode == 'nn':
        a_spec = pl.BlockSpec((tm, tk), lambda i, j, k: (i, k))
        b_spec = pl.BlockSpec((tk, tn), lambda i, j, k: (k, j))
    elif mode == 'nt':
        a_spec = pl.BlockSpec((tm, tk), lambda i, j, k: (i, k))
        b_spec = pl.BlockSpec((tn, tk), lambda i, j, k: (j, k))
    else:
        a_spec = pl.BlockSpec((tk, tm), lambda i, j, k: (k, i))
        b_spec = pl.BlockSpec((tk, tn), lambda i, j, k: (k, j))
    out_spec = pl.BlockSpec((tm, tn), lambda i, j, k: (i, j))
    own_in = [a_spec, b_spec] + ([out_spec] if add is not None else [])
    operands = (a, b) if add is None else (a, b, add)
    scratch = [pltpu.VMEM((tm, tn), F32)] if nk > 1 else []
    if ride is None:
        return pl.pallas_call(
            body, name=name, grid=grid, in_specs=own_in, out_specs=out_spec,
            out_shape=jax.ShapeDtypeStruct((M, N), out_dtype), scratch_shapes=scratch,
            compiler_params=_cparams(("parallel", "parallel", "arbitrary")),
        )(*operands)
    outs = pl.pallas_call(
        body, name=name, grid=grid, in_specs=own_in + ride.in_specs, out_specs=[out_spec] + ride.out_specs,
        out_shape=[jax.ShapeDtypeStruct((M, N), out_dtype)] + ride.out_shape,
        scratch_shapes=scratch + ride.scratch,
        compiler_params=_cparams(("arbitrary", "arbitrary", "arbitrary"), has_side_effects=True),
    )(*operands, *ride.arrays)
    return outs[0], outs[1:]


class Riders:
    def __init__(self, gather=(), scatter=(), own=None):
        self.gather_names = [n for n, _ in gather]
        self.shards = tuple(s for _, s in gather)
        self.scatter_names = [n for n, _ in scatter]
        self.slots = tuple(s for _, s in scatter)
        self.own = own

    def gather_ride(self, shards):
        return gather_ride(self.gather_names, list(shards)) if shards else None

    def scatter_ride(self, grads, own_grad=None):
        names, grads = list(self.scatter_names), list(grads)
        if self.own is not None:
            names.append(self.own)
            grads.append(own_grad)
        if not grads:
            return None
        return scatter_ride(names, [_cut_for_chips(n, g) for n, g in zip(names, grads)])

    def whole_slots(self, slots):
        return tuple(jnp.zeros(_whole_shape(n, s.shape[1:]), BF16) for n, s in zip(self.scatter_names, slots))


NO_RIDERS = Riders()


def linear(x, w, slot, name):
    @jax.custom_vjp
    def op(x, w, slot):
        return mm(x, w, 'nn', name + '_fwd')

    def op_fwd(x, w, slot):
        return op(x, w, slot), (x, w)

    def op_bwd(res, g):
        x, w = res
        return mm(g, w, 'nt', name + '_dx'), jnp.zeros_like(w), mm(x, g, 'tn', name + '_dw', BF16)

    op.defvjp(op_fwd, op_bwd)
    return op(x, w, slot)


def _rowwise_calls(f, rows, params, consts, out_widths, name, need_row_grad=None, block_rows=None, carry=(),
                   out_dtype=F32):
    nr, npar, nc, nout = len(rows), len(params), len(consts), len(out_widths)
    carry = tuple(carry)
    L = rows[0].shape[0]
    widths = [r.shape[1] for r in rows]
    T = block_rows or _rows_for(L, max(widths + list(out_widths)))
    n = L // T
    need = list(need_row_grad) if need_row_grad is not None else [True] * nr
    pshapes = [p.shape for p in params]
    cshapes = [c.shape for c in consts]

    row_specs = [pl.BlockSpec((T, w), lambda i: (i, 0)) for w in widths]
    par_specs = [pl.BlockSpec(s, lambda i: (0, 0)) for s in pshapes]
    con_specs = [pl.BlockSpec(s, lambda i: (0, 0)) for s in cshapes]
    out_specs = [pl.BlockSpec((T, w), lambda i: (i, 0)) for w in out_widths]

    def fwd_call(rows, params, consts):
        def body(*refs):
            ins = [r[...] for r in refs[:nr + npar + nc]]
            outs = f(*ins)
            for o_ref, val in zip(refs[nr + npar + nc:], outs):
                o_ref[...] = val.astype(out_dtype)

        return pl.pallas_call(
            body, name=name + '_fwd', grid=(n,),
            in_specs=row_specs + par_specs + con_specs, out_specs=out_specs,
            out_shape=[jax.ShapeDtypeStruct((L, w), out_dtype) for w in out_widths],
            compiler_params=_cparams(("parallel",)),
        )(*rows, *params, *consts)

    def bwd_call(rows, params, consts, cts, carried):
        grad_rows = [k for k in range(nr) if need[k]]
        n_in = nr + npar + nc + nout

        def body(*refs):
            i = pl.program_id(0)
            rv = [r[...] for r in refs[:nr]]
            pv = [r[...] for r in refs[nr:nr + npar]]
            cv = [r[...] for r in refs[nr + npar:nr + npar + nc]]
            ctv = tuple(r[...] for r in refs[nr + npar + nc:n_in])
            carried_refs = dict(zip(carry, refs[n_in:n_in + len(carry)]))
            orefs = refs[n_in + len(carry):]
            _, vjp = jax.vjp(lambda *rp: tuple(f(*rp, *cv)), *rv, *pv)
            g = vjp(ctv)
            for slot, k in enumerate(grad_rows):
                orefs[slot][...] = g[k] + carried_refs[k][...] if k in carried_refs else g[k]

            @pl.when(i == 0)
            def _():
                for k in range(npar):
                    orefs[len(grad_rows) + k][...] = jnp.zeros(pshapes[k], F32)

            for k in range(npar):
                orefs[len(grad_rows) + k][...] += g[nr + k]

        outs = pl.pallas_call(
            body, name=name + '_bwd', grid=(n,),
            in_specs=row_specs + par_specs + con_specs + out_specs + [row_specs[k] for k in carry],
            out_specs=[row_specs[k] for k in grad_rows] + par_specs,
            out_shape=[jax.ShapeDtypeStruct((L, widths[k]), F32) for k in grad_rows]
            + [jax.ShapeDtypeStruct(s, F32) for s in pshapes],
            compiler_params=_cparams(("arbitrary",)),
        )(*rows, *params, *consts, *cts, *carried)
        drows = []
        slot = 0
        for k in range(nr):
            if need[k]:
                drows.append(outs[slot])
                slot += 1
            else:
                drows.append(jnp.zeros_like(rows[k]))
        return tuple(drows), tuple(outs[len(grad_rows):])

    return fwd_call, bwd_call


def rowwise(f, rows, params, consts, out_widths, name, need_row_grad=None, block_rows=None, carry=()):
    fwd_call, bwd_call = _rowwise_calls(f, rows, params, consts, out_widths, name, need_row_grad, block_rows, carry)
    nout = len(out_widths)

    @jax.custom_vjp
    def op(rows, params, consts):
        return tuple(fwd_call(rows, params, consts)) + tuple(rows[k] for k in carry)

    def op_fwd(rows, params, consts):
        return op(rows, params, consts), (rows, params, consts)

    def op_bwd(res, cts):
        rows, params, consts = res
        drows, dparams = bwd_call(rows, params, consts, cts[:nout], cts[nout:])
        return drows, dparams, tuple(jnp.zeros_like(c) for c in consts)

    op.defvjp(op_fwd, op_bwd)
    return op(tuple(rows), tuple(params), tuple(consts))


def rowwise_linear(f, rows, params, consts, w, slot, name, residual=None, riders=NO_RIDERS, carry=(),
                   block_rows=None):
    width = w.shape[0]
    fwd_call, bwd_call = _rowwise_calls(lambda *a: (f(*a),), rows, params, consts, [width], name, None,
                                        block_rows, carry, BF16)

    @jax.custom_vjp
    def op(rows, params, consts, w, slot, residual, shards, slots):
        h, = fwd_call(rows, params, consts)
        ride = riders.gather_ride(shards)
        y = mm(h, w, 'nn', name + '_mm', add=residual, ride=ride)
        y, gathered = y if ride is not None else (y, ())
        return y, tuple(rows[k] for k in carry), tuple(gathered), riders.whole_slots(slots)

    def op_fwd(rows, params, consts, w, slot, residual, shards, slots):
        h, = fwd_call(rows, params, consts)
        ride = riders.gather_ride(shards)
        y = mm(h, w, 'nn', name + '_mm', add=residual, ride=ride)
        y, gathered = y if ride is not None else (y, ())
        out = (y, tuple(rows[k] for k in carry), tuple(gathered), riders.whole_slots(slots))
        return out, (rows, params, consts, h, w, shards)

    def op_bwd(res, cts):
        rows, params, consts, h, w, shards = res
        g, carried, _, slot_grads = cts
        dw = mm(h, g, 'tn', name + '_dw', BF16)
        ride = riders.scatter_ride(slot_grads, dw)
        dh = mm(g, w, 'nt', name + '_dx', ride=ride)
        dh, received = dh if ride is not None else (dh, ())
        if riders.own is not None:
            received, dw = received[:-1], received[-1]
        drows, dparams = bwd_call(rows, params, consts, (dh,), carried)
        return (drows, dparams, tuple(jnp.zeros_like(c) for c in consts), jnp.zeros_like(w), dw,
                None if residual is None else g, tuple(jnp.zeros_like(s) for s in shards), tuple(received))

    op.defvjp(op_fwd, op_bwd)
    return op(tuple(rows), tuple(params), tuple(consts), w, slot, residual, riders.shards, riders.slots)


def _rms(x, g):
    return x * lax.rsqrt(jnp.mean(x * x, axis=-1, keepdims=True) + EPS) * g


def rmsnorm(x, g, name, carry=False):
    out = rowwise(lambda x, g: (_rms(x, g),), [x], [g.reshape(1, -1)], [], [x.shape[1]], name,
                  carry=(0,) if carry else ())
    return out if carry else out[0]


def _split2(x):
    hi = x.astype(BF16)
    return hi, (x - hi.astype(F32)).astype(BF16)


@jax.custom_vjp
def select_mm(x, sel):
    return sum(_dg(t, sel, 1, 0) for t in _split2(x))


def _select_mm_fwd(x, sel):
    return select_mm(x, sel), sel


def _select_mm_bwd(sel, g):
    return sum(_dg(t, sel, 1, 1) for t in _split2(g)), jnp.zeros_like(sel)


select_mm.defvjp(_select_mm_fwd, _select_mm_bwd)


def groupnorm(x, g, group, name):
    width = x.shape[1]
    g_full = jnp.tile(g.reshape(1, group), (1, width // group))
    if group % LANES == 0:
        def f(x, g_full):
            outs = []
            for lo in range(0, width, group):
                xs = x[:, lo:lo + group]
                outs.append(_rms(xs, g_full[:, lo:lo + group]))
            return (jnp.concatenate(outs, axis=-1),)

        return rowwise(f, [x], [g_full], [], [width], name)[0]

    gid = jnp.arange(width) // group
    sel = (gid[:, None] == jnp.arange(LANES)[None, :]).astype(BF16)

    def f(x, g_full, sel, sel_t):
        ms = select_mm(x * x, sel) * (1.0 / group)
        inv = select_mm(lax.rsqrt(ms + EPS), sel_t)
        return (x * inv * g_full,)

    return rowwise(f, [x], [g_full], [sel, sel.T], [width], name)[0]


def glu(x, name):
    half = x.shape[1] // 2

    def f(x):
        return (x[:, :half] * jax.nn.sigmoid(x[:, half:]),)

    return rowwise(f, [x], [], [], [half], name)[0]


def ln_silu_block(x, g, b):
    mu = jnp.mean(x, axis=-1, keepdims=True)
    xc = x - mu
    var = jnp.mean(xc * xc, axis=-1, keepdims=True)
    y = xc * lax.rsqrt(var + EPS) * g + b
    return y * jax.nn.sigmoid(y)


def branch_norms_block(a, b, c, g):
    w1, w2 = a.shape[1], b.shape[1]
    return jnp.concatenate([_rms(a, g[:, :w1]), _rms(b, g[:, w1:w1 + w2]), _rms(c, g[:, w1 + w2:])], axis=-1)


def xa_core_block(q, k, v):
    scale = XA_HEAD_DIM ** -0.5
    outs = []
    for h in range(XA_HEADS):
        sl = slice(h * XA_HEAD_DIM, (h + 1) * XA_HEAD_DIM)
        s = bdot_nt(q[:, sl], k[:, sl]) * scale
        m = lax.stop_gradient(jnp.max(s, axis=-1, keepdims=True))
        e = jnp.exp(s - m)
        p = e / jnp.sum(e, axis=-1, keepdims=True)
        outs.append(bdot_nn(p, v[:, sl]))
    return jnp.concatenate(outs, axis=-1)


def loss_rows(y, target, name):
    def f(y, t):
        d = y - t
        return (0.5 * jnp.mean(d * d, axis=-1, keepdims=True),)

    return rowwise(f, [y, target], [], [], [1], name, need_row_grad=[True, False])[0]


def _swiglu(gate, up):
    return gate * jax.nn.sigmoid(gate) * up


def mm_swiglu(h, w, name, ride=None):
    M, K = h.shape
    H = w.shape[1] // 2
    tm, tn = _pick(M, 1024), _pick(H, 1536)
    nj = H // tn
    grid = (M // tm, 2 * nj)

    def body(*refs):
        (a_ref, b_ref), (gu_ref, act_ref), (kept,), riding = _split_refs(ride, 2, 2, refs)
        ride_done = _ride_ends(ride, riding, grid)
        j = pl.program_id(1)
        prod = _dg(a_ref[...], b_ref[...], 1, 0)
        gu_ref[...] = prod

        @pl.when(j < nj)
        def _():
            kept[j] = prod

        @pl.when(j >= nj)
        def _():
            act_ref[...] = _swiglu(kept[j - nj], prod).astype(BF16)

        ride_done()

    return _ride_call(
        body, name, grid,
        [pl.BlockSpec((tm, K), lambda i, j: (i, 0)), pl.BlockSpec((K, tn), lambda i, j: (0, j))],
        [pl.BlockSpec((tm, tn), lambda i, j: (i, j)),
         pl.BlockSpec((tm, tn), lambda i, j: (i, jnp.maximum(j - nj, 0)))],
        [jax.ShapeDtypeStruct((M, 2 * H), F32), jax.ShapeDtypeStruct((M, H), BF16)],
        (h, w), ride, ("parallel", "arbitrary"), scratch=[pltpu.VMEM((nj, tm, tn), F32)])


def mm_swiglu_bwd(dy, w_out, gu, name):
    M, D = dy.shape
    H = gu.shape[1] // 2
    tm, tn = _pick(M, 512), _pick(H, 1536)
    nj = H // tn

    def body(dy_ref, w_ref, gate_ref, up_ref, o_ref, kept):
        j = pl.program_id(1)

        @pl.when(j < nj)
        def _():
            dact = _dg(dy_ref[...], w_ref[...], 1, 1)
            _, vjp = jax.vjp(_swiglu, gate_ref[...], up_ref[...])
            dgate, dup = vjp(dact)
            o_ref[...] = dgate.astype(BF16)
            kept[j] = dup.astype(BF16)

        @pl.when(j >= nj)
        def _():
            o_ref[...] = kept[j - nj]

    def tile(j):
        return jnp.minimum(j, nj - 1)

    return pl.pallas_call(
        body, name=name, grid=(M // tm, 2 * nj),
        in_specs=[pl.BlockSpec((tm, D), lambda i, j: (i, 0)),
                  pl.BlockSpec((tn, D), lambda i, j: (tile(j), 0)),
                  pl.BlockSpec((tm, tn), lambda i, j: (i, tile(j))),
                  pl.BlockSpec((tm, tn), lambda i, j: (i, nj + tile(j)))],
        out_specs=pl.BlockSpec((tm, tn), lambda i, j: (i, j)),
        out_shape=jax.ShapeDtypeStruct((M, 2 * H), BF16),
        scratch_shapes=[pltpu.VMEM((nj, tm, tn), BF16)],
        compiler_params=_cparams(("parallel", "arbitrary")),
    )(dy, w_out, gu, gu)


def ffn(x, gain, w_in, slot_in, w_out, slot_out, name, riders=NO_RIDERS):
    norm_fwd, norm_bwd = _rowwise_calls(lambda x, g: (_rms(x, g),), [x], [gain], [], [x.shape[1]],
                                        name + '_norm', None, None, (0,), BF16)

    def forward(x, gain, w_in, w_out, shards):
        h, = norm_fwd((x,), (gain,), ())
        (gu, act), gathered = mm_swiglu(h, w_in, name + '_in', riders.gather_ride(shards))
        return mm(act, w_out, 'nn', name + '_out', add=x), tuple(gathered), (h, gu, act)

    @jax.custom_vjp
    def op(x, gain, w_in, slot_in, w_out, slot_out, shards, slots):
        y, gathered, _ = forward(x, gain, w_in, w_out, shards)
        return y, gathered, riders.whole_slots(slots)

    def op_fwd(x, gain, w_in, slot_in, w_out, slot_out, shards, slots):
        y, gathered, (h, gu, act) = forward(x, gain, w_in, w_out, shards)
        return (y, gathered, riders.whole_slots(slots)), (x, gain, h, gu, act, w_in, w_out, shards)

    def op_bwd(res, cts):
        x, gain, h, gu, act, w_in, w_out, shards = res
        g, _, slot_grads = cts
        dgu = mm_swiglu_bwd(g, w_out, gu, name + '_dact')
        dw_out = mm(act, g, 'tn', name + '_out_dw', BF16)
        dw_in = mm(h, dgu, 'tn', name + '_in_dw', BF16)
        ride = riders.scatter_ride(slot_grads)
        dh = mm(dgu, w_in, 'nt', name + '_in_dx', ride=ride)
        dh, received = dh if ride is not None else (dh, ())
        (dx,), (dgain,) = norm_bwd((x,), (gain,), (), (dh,), (g,))
        return (dx, dgain, jnp.zeros_like(w_in), dw_in, jnp.zeros_like(w_out), dw_out,
                tuple(jnp.zeros_like(s) for s in shards), tuple(received))

    op.defvjp(op_fwd, op_bwd)
    return op(x, gain, w_in, slot_in, w_out, slot_out, riders.shards, riders.slots)


def _hilo(x, ones_bf16):
    hi = x.astype(BF16)
    lo = (x - hi.astype(F32)).astype(BF16)
    return _dg(hi, ones_bf16, 1, 0) + _dg(lo, ones_bf16, 1, 0)


def _sb_block(qh, kb, c, valid, strict_upper):
    z = _dg(qh, kb, 1, 1)
    a = jnp.minimum(z, 0.0) - jnp.log(1.0 + jnp.exp(-jnp.abs(z)))
    b = jnp.where(valid, a - z, 0.0)
    s = _hilo(b, strict_upper) + c
    w = jnp.where(valid, jnp.exp(a + s), 0.0)
    return a, b, w


def _sb_masks(T):
    row = lax.broadcasted_iota(jnp.int32, (T, T), 0)
    col = lax.broadcasted_iota(jnp.int32, (T, T), 1)
    return col < row, (row > col).astype(BF16), (row >= col).astype(BF16)


def _sb_key_blocks(i, j, T, causal):
    second = jnp.maximum(j - 1, 0)
    return [(pl.multiple_of(j * T, T), jnp.logical_or(causal, j != i)),
            (pl.multiple_of(second * T, T), jnp.logical_and(jnp.logical_or(causal, True), j >= 1))]


HEADS_PER_BLOCK = LANES // SB_HEAD_DIM


def _head_mask(h):
    lane = lax.broadcasted_iota(jnp.int32, (1, LANES), 1)
    return (lane // SB_HEAD_DIM == h).astype(F32)


def _max_all(columns):
    m = columns[0]
    for c in columns[1:]:
        m = jnp.maximum(m, c)
    return jnp.max(m)


def _ride_call(body, name, grid, in_specs, out_specs, out_shape, operands, ride, semantics, scratch=()):
    if ride is None:
        outs = pl.pallas_call(body, name=name, grid=grid, in_specs=in_specs, out_specs=out_specs,
                              out_shape=out_shape, scratch_shapes=list(scratch),
                              compiler_params=_cparams(semantics))(*operands)
        return outs, ()
    outs = pl.pallas_call(
        body, name=name, grid=grid, in_specs=in_specs + ride.in_specs, out_specs=out_specs + ride.out_specs,
        out_shape=out_shape + ride.out_shape, scratch_shapes=list(scratch) + ride.scratch,
        compiler_params=_cparams(("arbitrary",) * len(grid), has_side_effects=True),
    )(*operands, *ride.arrays)
    return outs[:len(out_shape)], outs[len(out_shape):]


def _ride_ends(ride, riding, grid):
    if riding is None:
        return lambda: None
    first, last = None, None
    for axis, size in enumerate(grid):
        at0, at1 = pl.program_id(axis) == 0, pl.program_id(axis) == size - 1
        first = at0 if first is None else jnp.logical_and(first, at0)
        last = at1 if last is None else jnp.logical_and(last, at1)
    return ride.at_ends(riding, first, last)


def _sb_fwd_call(q, k, v, T, name, ride=None):
    L, W = q.shape
    scale = SB_HEAD_DIM ** -0.5
    grid = (W // LANES, L // T)

    def body(*refs):
        (q_ref, k_ref, v_ref), (o_ref,), _, riding = _split_refs(ride, 3, 1, refs)
        ride_done = _ride_ends(ride, riding, grid)
        i = pl.program_id(1)
        causal, strict_upper, _ = _sb_masks(T)
        q2 = q_ref[...] * scale
        masks = [_head_mask(h) for h in range(HEADS_PER_BLOCK)]
        qs = [(q2 * hm).astype(BF16) for hm in masks]
        zero = jnp.zeros((T, 1), F32)

        def cond(state):
            j, cs, _ = state
            return jnp.logical_and(j >= 0, _max_all(cs) > -SB_CUT)

        def step(state):
            j, cs, acc = state
            blocks = _sb_key_blocks(i, j, T, causal)
            ks = [k_ref[pl.ds(r0, T), :].astype(BF16) for r0, _ in blocks]
            vs = [v_ref[pl.ds(r0, T), :] for r0, _ in blocks]
            new_cs = []
            for hm, qh, c in zip(masks, qs, cs):
                for (_, valid), kb, vb in zip(blocks, ks, vs):
                    _, b, w = _sb_block(qh, kb, c, valid, strict_upper)
                    vh = vb * hm.astype(BF16)
                    w_hi = w.astype(BF16)
                    w_lo = (w - w_hi.astype(F32)).astype(BF16)
                    acc = acc + _dg(w_hi, vh, 1, 0) + _dg(w_lo, vh, 1, 0)
                    c = c + jnp.sum(b, axis=1, keepdims=True)
                new_cs.append(c)
            return j - len(blocks), tuple(new_cs), acc

        _, _, acc = lax.while_loop(cond, step, (i, (zero,) * HEADS_PER_BLOCK, jnp.zeros((T, LANES), F32)))
        o_ref[...] = acc
        ride_done()

    (o,), rode = _ride_call(
        body, name, grid,
        [pl.BlockSpec((T, LANES), lambda p, i: (i, p)),
         pl.BlockSpec((L, LANES), lambda p, i: (0, p)),
         pl.BlockSpec((L, LANES), lambda p, i: (0, p))],
        [pl.BlockSpec((T, LANES), lambda p, i: (i, p))], [jax.ShapeDtypeStruct((L, W), F32)],
        (q, k, v), ride, ("parallel", "parallel"))
    return o, rode


def _sb_bwd_call(q, k, v, o, do, T, name, ride=None):
    L, W = q.shape
    scale = SB_HEAD_DIM ** -0.5
    grid = (W // LANES, L // T)

    def body(*refs):
        (q_ref, k_ref, v_ref, o_ref, do_ref), (dq_ref, dk_ref, dv_ref), _, riding = _split_refs(ride, 5, 3, refs)
        ride_done = _ride_ends(ride, riding, grid)
        i = pl.program_id(1)

        @pl.when(i == 0)
        def _():
            dk_ref[...] = jnp.zeros_like(dk_ref)
            dv_ref[...] = jnp.zeros_like(dv_ref)

        causal, strict_upper, upper = _sb_masks(T)
        q2 = q_ref[...] * scale
        do2 = do_ref[...]
        o2 = o_ref[...]
        masks = [_head_mask(h) for h in range(HEADS_PER_BLOCK)]
        qs = [(q2 * hm).astype(BF16) for hm in masks]
        dos = [(do2 * hm).astype(BF16) for hm in masks]
        totals = [jnp.sum(doh.astype(F32) * o2, axis=1, keepdims=True) for doh in dos]
        zero = jnp.zeros((T, 1), F32)

        def cond(state):
            j, cs, _, _ = state
            return jnp.logical_and(j >= 0, _max_all(cs) > -SB_CUT)

        def step(state):
            j, cs, rs, dq = state
            blocks = _sb_key_blocks(i, j, T, causal)
            kfs = [k_ref[pl.ds(r0, T), :] for r0, _ in blocks]
            ks = [kf.astype(BF16) for kf in kfs]
            vs = [v_ref[pl.ds(r0, T), :].astype(BF16) for r0, _ in blocks]
            dks = [jnp.zeros((T, LANES), F32) for _ in blocks]
            dvs = [jnp.zeros((T, LANES), F32) for _ in blocks]
            new_cs, new_rs = [], []
            for hm, qh, doh, total, c, r in zip(masks, qs, dos, totals, cs, rs):
                for n, ((_, valid), kf, kb, vb) in enumerate(zip(blocks, kfs, ks, vs)):
                    a, b, w = _sb_block(qh, kb, c, valid, strict_upper)
                    e = _dg(doh, vb, 1, 1) * w
                    before = total - (_hilo(e, upper) + r)
                    dz = jnp.where(valid, e * jnp.exp(b) - before * jnp.exp(a), 0.0).astype(BF16)
                    dq = dq + _dg(dz, kf * hm.astype(BF16), 1, 0)
                    dks[n] = dks[n] + _dg(dz, qh, 0, 0)
                    dvs[n] = dvs[n] + _dg(w, doh, 0, 0)
                    c = c + jnp.sum(b, axis=1, keepdims=True)
                    r = r + jnp.sum(e, axis=1, keepdims=True)
                new_cs.append(c)
                new_rs.append(r)
            for (r0, _), dk, dv in zip(blocks, dks, dvs):
                dk_ref[pl.ds(r0, T), :] += dk
                dv_ref[pl.ds(r0, T), :] += dv
            return j - len(blocks), tuple(new_cs), tuple(new_rs), dq

        init = (i, (zero,) * HEADS_PER_BLOCK, (zero,) * HEADS_PER_BLOCK, jnp.zeros((T, LANES), F32))
        dq = lax.while_loop(cond, step, init)[3]
        dq_ref[...] = dq * scale
        ride_done()

    blk = pl.BlockSpec((T, LANES), lambda p, i: (i, p))
    full = pl.BlockSpec((L, LANES), lambda p, i: (0, p))
    return _ride_call(body, name, grid, [blk, full, full, blk, blk], [blk, full, full],
                      [jax.ShapeDtypeStruct((L, W), F32)] * 3, (q, k, v, o, do), ride, ("parallel", "arbitrary"))


def sb_attention(q, k, v, name, riders=NO_RIDERS):
    T = min(256, q.shape[0])

    @jax.custom_vjp
    def op(q, k, v, shards, slots):
        o, gathered = _sb_fwd_call(q, k.astype(BF16), v.astype(BF16), T, name + '_fwd', riders.gather_ride(shards))
        return o, tuple(gathered), riders.whole_slots(slots)

    def op_fwd(q, k, v, shards, slots):
        out = op(q, k, v, shards, slots)
        return out, (q, k.astype(BF16), v.astype(BF16), out[0], shards)

    def op_bwd(res, cts):
        q, k, v, o, shards = res
        do, _, slot_grads = cts
        grads, received = _sb_bwd_call(q, k, v, o, do, T, name + '_bwd', riders.scatter_ride(slot_grads))
        return (*grads, tuple(jnp.zeros_like(s) for s in shards), tuple(received))

    op.defvjp(op_fwd, op_bwd)
    return op(q, k, v, riders.shards, riders.slots)


def _dwconv_fwd_call(x, w, b, T, name):
    L, C = x.shape
    per = T // CONV_HALO
    lead = CONV_HALO - (CONV_WIDTH - 1)

    def body(x_ref, halo_ref, w_ref, b_ref, o_ref, buf):
        i = pl.program_id(0)
        buf[0:CONV_HALO, :] = jnp.where(i > 0, halo_ref[...], 0.0)
        buf[CONV_HALO:CONV_HALO + T, :] = x_ref[...]
        acc = jnp.zeros((T, C), F32) + b_ref[...]
        for j in range(CONV_WIDTH):
            acc = acc + w_ref[j:j + 1, :] * buf[lead + j:lead + j + T, :]
        o_ref[...] = acc

    return pl.pallas_call(
        body, name=name, grid=(L // T,),
        in_specs=[pl.BlockSpec((T, C), lambda i: (i, 0)),
                  pl.BlockSpec((CONV_HALO, C), lambda i: (jnp.maximum(i * per - 1, 0), 0)),
                  pl.BlockSpec(w.shape, lambda i: (0, 0)),
                  pl.BlockSpec(b.shape, lambda i: (0, 0))],
        out_specs=pl.BlockSpec((T, C), lambda i: (i, 0)),
        out_shape=jax.ShapeDtypeStruct((L, C), F32),
        scratch_shapes=[pltpu.VMEM((T + CONV_HALO, C), F32)],
        compiler_params=_cparams(("parallel",)),
    )(x, x, w, b)


def _dwconv_bwd_call(x, w, g, T, name):
    L, C = x.shape
    per = T // CONV_HALO
    n = L // T
    last_halo = L // CONV_HALO - 1
    lead = CONV_HALO - (CONV_WIDTH - 1)

    def body(x_ref, xh_ref, g_ref, gh_ref, w_ref, dx_ref, dw_ref, db_ref, bufx, bufg):
        i = pl.program_id(0)
        bufx[0:CONV_HALO, :] = jnp.where(i > 0, xh_ref[...], 0.0)
        bufx[CONV_HALO:CONV_HALO + T, :] = x_ref[...]
        gm = g_ref[...]
        bufg[0:T, :] = gm
        bufg[T:T + CONV_HALO, :] = jnp.where(i < n - 1, gh_ref[...], 0.0)
        acc = jnp.zeros((T, C), F32)
        for j in range(CONV_WIDTH):
            off = CONV_WIDTH - 1 - j
            acc = acc + w_ref[j:j + 1, :] * bufg[off:off + T, :]
        dx_ref[...] = acc

        @pl.when(i == 0)
        def _():
            dw_ref[...] = jnp.zeros_like(dw_ref)
            db_ref[...] = jnp.zeros_like(db_ref)

        for j in range(CONV_WIDTH):
            dw_ref[j:j + 1, :] += jnp.sum(gm * bufx[lead + j:lead + j + T, :], axis=0, keepdims=True)
        db_ref[...] += jnp.sum(gm, axis=0, keepdims=True)

    return pl.pallas_call(
        body, name=name, grid=(n,),
        in_specs=[pl.BlockSpec((T, C), lambda i: (i, 0)),
                  pl.BlockSpec((CONV_HALO, C), lambda i: (jnp.maximum(i * per - 1, 0), 0)),
                  pl.BlockSpec((T, C), lambda i: (i, 0)),
                  pl.BlockSpec((CONV_HALO, C), lambda i: (jnp.minimum((i + 1) * per, last_halo), 0)),
                  pl.BlockSpec(w.shape, lambda i: (0, 0))],
        out_specs=[pl.BlockSpec((T, C), lambda i: (i, 0)),
                   pl.BlockSpec(w.shape, lambda i: (0, 0)),
                   pl.BlockSpec((1, C), lambda i: (0, 0))],
        out_shape=[jax.ShapeDtypeStruct((L, C), F32), jax.ShapeDtypeStruct(w.shape, F32),
                   jax.ShapeDtypeStruct((1, C), F32)],
        scratch_shapes=[pltpu.VMEM((T + CONV_HALO, C), F32), pltpu.VMEM((T + CONV_HALO, C), F32)],
        compiler_params=_cparams(("arbitrary",)),
    )(x, x, g, g, w)


def dwconv(x, w, b, name):
    T = min(512, x.shape[0])

    @jax.custom_vjp
    def op(x, w, b):
        return _dwconv_fwd_call(x, w, b, T, name + '_fwd')

    def op_fwd(x, w, b):
        return _dwconv_fwd_call(x, w, b, T, name + '_fwd'), (x, w)

    def op_bwd(res, g):
        x, w = res
        return tuple(_dwconv_bwd_call(x, w, g, T, name + '_bwd'))

    op.defvjp(op_fwd, op_bwd)
    return op(x, w, b)


def _ssm_fwd_call(u, ar, ai, bbr, bbi, cr, ci, d, T, name):
    L, C = u.shape
    S = SSM_LANES

    def body(u_ref, ar_ref, ai_ref, bbr_ref, bbi_ref, cr_ref, ci_ref, d_ref,
             y_ref, xr_ref, xi_ref, st_r, st_i, in_r, in_i, out_r, out_i):
        i = pl.program_id(0)

        @pl.when(i == 0)
        def _():
            st_r[...] = jnp.zeros_like(st_r)
            st_i[...] = jnp.zeros_like(st_i)

        u_blk = u_ref[...]
        xr_ref[...] = _dg(u_blk, bbr_ref[...], 1, 0)
        xi_ref[...] = _dg(u_blk, bbi_ref[...], 1, 0)
        a_r, a_i = ar_ref[...], ai_ref[...]

        def tile(t, carry):
            sr, si = carry
            r0 = pl.multiple_of(t * SUBLANES, SUBLANES)
            in_r[...] = xr_ref[pl.ds(r0, SUBLANES), :]
            in_i[...] = xi_ref[pl.ds(r0, SUBLANES), :]
            for r in range(SUBLANES):
                nr = a_r * sr - a_i * si + in_r[r:r + 1, :]
                ni = a_r * si + a_i * sr + in_i[r:r + 1, :]
                sr, si = nr, ni
                out_r[r:r + 1, :] = sr
                out_i[r:r + 1, :] = si
            xr_ref[pl.ds(r0, SUBLANES), :] = out_r[...]
            xi_ref[pl.ds(r0, SUBLANES), :] = out_i[...]
            return sr, si

        sr, si = lax.fori_loop(0, T // SUBLANES, tile, (st_r[0:1, :], st_i[0:1, :]))
        st_r[0:1, :] = sr
        st_i[0:1, :] = si
        y_ref[...] = (_dg(xr_ref[...], cr_ref[...], 1, 0) - _dg(xi_ref[...], ci_ref[...], 1, 0)
                      + d_ref[...] * u_blk)

    full = lambda a: pl.BlockSpec(a.shape, lambda i: (0, 0))
    return pl.pallas_call(
        body, name=name, grid=(L // T,),
        in_specs=[pl.BlockSpec((T, C), lambda i: (i, 0))] + [full(a) for a in (ar, ai, bbr, bbi, cr, ci, d)],
        out_specs=[pl.BlockSpec((T, C), lambda i: (i, 0)), pl.BlockSpec((T, S), lambda i: (i, 0)),
                   pl.BlockSpec((T, S), lambda i: (i, 0))],
        out_shape=[jax.ShapeDtypeStruct((L, C), F32), jax.ShapeDtypeStruct((L, S), F32),
                   jax.ShapeDtypeStruct((L, S), F32)],
        scratch_shapes=[pltpu.VMEM((SUBLANES, S), F32)] * 6,
        compiler_params=_cparams(("arbitrary",)),
    )(u, ar, ai, bbr, bbi, cr, ci, d)


def _ssm_bwd_call(u, xr, xi, dy, ar, ai, bbr, bbi, cr, ci, d, T, name):
    L, C = u.shape
    S = SSM_LANES
    n = L // T
    per = T // SUBLANES

    def body(u_ref, xr_ref, xi_ref, hr_ref, hi_ref, dy_ref, ar_ref, ai_ref, bbr_ref, bbi_ref, cr_ref, ci_ref,
             d_ref, du_ref, dar_ref, dai_ref, dbr_ref, dbi_ref, dcr_ref, dci_ref, dd_ref,
             lam_r, lam_i, prev_r, prev_i, st_r, st_i, in_r, in_i, out_r, out_i):
        i = pl.program_id(0)
        chunk = n - 1 - i

        @pl.when(i == 0)
        def _():
            st_r[...] = jnp.zeros_like(st_r)
            st_i[...] = jnp.zeros_like(st_i)
            for ref in (dar_ref, dai_ref, dbr_ref, dbi_ref, dcr_ref, dci_ref, dd_ref):
                ref[...] = jnp.zeros_like(ref)

        dy_blk = dy_ref[...]
        u_blk = u_ref[...]
        lam_r[...] = _dg(dy_blk, cr_ref[...], 1, 1)
        lam_i[...] = -_dg(dy_blk, ci_ref[...], 1, 1)
        dcr_ref[...] += _dg(xr_ref[...], dy_blk, 0, 0)
        dci_ref[...] -= _dg(xi_ref[...], dy_blk, 0, 0)
        a_r, a_i = ar_ref[...], ai_ref[...]

        def tile(k, carry):
            lr, li = carry
            r0 = pl.multiple_of((per - 1 - k) * SUBLANES, SUBLANES)
            in_r[...] = lam_r[pl.ds(r0, SUBLANES), :]
            in_i[...] = lam_i[pl.ds(r0, SUBLANES), :]
            for r in range(SUBLANES - 1, -1, -1):
                nr = in_r[r:r + 1, :] + a_r * lr + a_i * li
                ni = in_i[r:r + 1, :] + a_r * li - a_i * lr
                lr, li = nr, ni
                out_r[r:r + 1, :] = lr
                out_i[r:r + 1, :] = li
            lam_r[pl.ds(r0, SUBLANES), :] = out_r[...]
            lam_i[pl.ds(r0, SUBLANES), :] = out_i[...]
            return lr, li

        lr, li = lax.fori_loop(0, per, tile, (st_r[0:1, :], st_i[0:1, :]))
        st_r[0:1, :] = lr
        st_i[0:1, :] = li

        l_r, l_i = lam_r[...], lam_i[...]
        du_ref[...] = _dg(l_r, bbr_ref[...], 1, 1) + _dg(l_i, bbi_ref[...], 1, 1) + d_ref[...] * dy_blk
        dbr_ref[...] += _dg(u_blk, l_r, 0, 0)
        dbi_ref[...] += _dg(u_blk, l_i, 0, 0)
        dd_ref[...] += jnp.sum(dy_blk * u_blk, axis=0, keepdims=True)

        prev_r[0:SUBLANES, :] = jnp.where(chunk > 0, hr_ref[...], 0.0)
        prev_i[0:SUBLANES, :] = jnp.where(chunk > 0, hi_ref[...], 0.0)
        prev_r[SUBLANES:SUBLANES + T, :] = xr_ref[...]
        prev_i[SUBLANES:SUBLANES + T, :] = xi_ref[...]
        p_r = prev_r[SUBLANES - 1:SUBLANES - 1 + T, :]
        p_i = prev_i[SUBLANES - 1:SUBLANES - 1 + T, :]
        dar_ref[...] += jnp.sum(l_r * p_r + l_i * p_i, axis=0, keepdims=True)
        dai_ref[...] += jnp.sum(l_i * p_r - l_r * p_i, axis=0, keepdims=True)

    rev = lambda w: pl.BlockSpec((T, w), lambda i: (n - 1 - i, 0))
    halo = pl.BlockSpec((SUBLANES, S), lambda i: (jnp.maximum((n - 1 - i) * per - 1, 0), 0))
    full = lambda a: pl.BlockSpec(a.shape, lambda i: (0, 0))
    params = (ar, ai, bbr, bbi, cr, ci, d)
    return pl.pallas_call(
        body, name=name, grid=(n,),
        in_specs=[rev(C), rev(S), rev(S), halo, halo, rev(C)] + [full(a) for a in params],
        out_specs=[rev(C)] + [full(a) for a in params],
        out_shape=[jax.ShapeDtypeStruct((L, C), F32)] + [jax.ShapeDtypeStruct(a.shape, F32) for a in params],
        scratch_shapes=[pltpu.VMEM((T, S), F32), pltpu.VMEM((T, S), F32),
                        pltpu.VMEM((T + SUBLANES, S), F32), pltpu.VMEM((T + SUBLANES, S), F32)]
        + [pltpu.VMEM((SUBLANES, S), F32)] * 6,
        compiler_params=_cparams(("arbitrary",)),
    )(u, xr, xi, xr, xi, dy, *params)


def ssm_core(u, ar, ai, bbr, bbi, cr, ci, d, name):
    T = min(256, u.shape[0])

    @jax.custom_vjp
    def op(u, ar, ai, bbr, bbi, cr, ci, d):
        return _ssm_fwd_call(u, ar, ai, bbr, bbi, cr, ci, d, T, name + '_fwd')[0]

    def op_fwd(u, ar, ai, bbr, bbi, cr, ci, d):
        y, xr, xi = _ssm_fwd_call(u, ar, ai, bbr, bbi, cr, ci, d, T, name + '_fwd')
        return y, (u, xr, xi, ar, ai, bbr, bbi, cr, ci, d)

    def op_bwd(res, dy):
        u, xr, xi, ar, ai, bbr, bbi, cr, ci, d = res
        return tuple(_ssm_bwd_call(u, xr, xi, dy, ar, ai, bbr, bbi, cr, ci, d, T, name + '_bwd'))

    op.defvjp(op_fwd, op_bwd)
    return op(u, ar, ai, bbr, bbi, cr, ci, d)


@jax.custom_vjp
def _block_diag(blocks):
    G, R, Cc = blocks.shape
    eye = jnp.eye(G, dtype=blocks.dtype)
    return (blocks[:, :, None, :] * eye[:, None, :, None]).reshape(G * R, G * Cc)


def _block_diag_fwd(blocks):
    return _block_diag(blocks), blocks.shape


def _block_diag_bwd(shape, g):
    G, R, Cc = shape
    on_diagonal = jnp.eye(G, dtype=bool)[:, None, :, None]
    return (jnp.sum(jnp.where(on_diagonal, g.reshape(G, R, G, Cc), 0.0), axis=2),)


_block_diag.defvjp(_block_diag_fwd, _block_diag_bwd)


def ssm_discretise(lam_re, lam_im, log_dt, b_re, b_im, c_re, c_im):
    dt = jnp.exp(log_dt)[:, None]
    mag = jnp.exp(lam_re * dt)
    ar, ai = mag * jnp.cos(lam_im * dt), mag * jnp.sin(lam_im * dt)
    den = lam_re * lam_re + lam_im * lam_im
    fr = ((ar - 1.0) * lam_re + ai * lam_im) / den
    fi = (ai * lam_re - (ar - 1.0) * lam_im) / den
    bbr = fr[..., None] * b_re - fi[..., None] * b_im
    bbi = fr[..., None] * b_im + fi[..., None] * b_re
    return (ar.reshape(1, SSM_LANES), ai.reshape(1, SSM_LANES),
            _block_diag(bbr.transpose(0, 2, 1)), _block_diag(bbi.transpose(0, 2, 1)),
            _block_diag(c_re.transpose(0, 2, 1)), _block_diag(c_im.transpose(0, 2, 1)))


def split_columns(p, bounds):
    @jax.custom_vjp
    def op(p):
        return tuple(p[:, lo:hi] for lo, hi in zip(bounds[:-1], bounds[1:]))

    def op_fwd(p):
        return op(p), None

    def op_bwd(_, gs):
        return (jnp.concatenate(gs, axis=1),)

    op.defvjp(op_fwd, op_bwd)
    return op(p)


DEPTH = 2
EARLY = ['w_in', 'conv_pw2_w', 'ssm_glu_w']
LATE = [n for n in MATRICES if n not in EARLY]
FFN = ['ffn_w_in', 'ffn_w_out']
AFTER_XA_Q = ['xa_wk', 'xa_wv', 'xa_wo']
GATHER_AT = {
    'start': [('w_in', 0)],
    'win0': [('ffn_w_out', 0)],
    'sb0': [(n, 0) for n in MATRICES if n not in ('w_in', 'ffn_w_out')],
    'ffn0': [(n, 1) for n in MATRICES if n not in FFN + AFTER_XA_Q],
    'sb1': [(n, 1) for n in FFN + AFTER_XA_Q],
}
SCATTER_AT = {
    'sb1': [(n, 1) for n in LATE],
    'ffn0': [(n, 1) for n in EARLY],
    'sb0': [(n, 0) for n in LATE],
    'win0': [(n, 0) for n in EARLY if n != 'w_in'],
}
OWN_AT = {'win0': ('w_in', 0)}


def _assemble(name, gathered):
    if name not in SLOTTED:
        return gathered
    return jnp.concatenate([gathered[j] for j in range(N_CHIPS)], axis=SHARD_AXIS[name] - 1)


def local_loss(slots, w, mats, shards, x, mem, target):
    mats = dict(mats)
    slot = {key: slots[key] for key in OWN_AT.values()}
    s1, s2, s3 = SB_WIDTH, 2 * SB_WIDTH, 3 * SB_WIDTH
    s4 = s3 + 2 * CONV_CH

    def riders_at(host):
        return Riders(gather=[(n, shards[(n, l)]) for n, l in GATHER_AT.get(host, [])],
                      scatter=[(n, slots[(n, l)]) for n, l in SCATTER_AT[host]],
                      own=OWN_AT[host][0] if host in OWN_AT else None)

    def take(host, gathered, handed):
        for (n, l), g in zip(GATHER_AT.get(host, []), gathered):
            mats[(n, l)] = _assemble(n, g)
        for key, s in zip(SCATTER_AT[host], handed):
            slot[key] = s

    def linear_(x, n, l, name):
        return linear(x, mats[(n, l)], slot[(n, l)], name)

    def fused_(f, rows, params, n, l, name, residual=None, host=None, carry=(), block_rows=None):
        y, carried, gathered, handed = rowwise_linear(
            f, rows, params, [], mats[(n, l)], slot[(n, l)], name, residual,
            riders_at(host) if host else NO_RIDERS, carry, block_rows)
        if host:
            take(host, gathered, handed)
        return (y,) + tuple(carried)

    def gain(n, l):
        return w[n][l].reshape(1, -1)

    for l in range(DEPTH):
        tag = 'l%d_' % l
        p, x = fused_(_rms, [x], [gain('norm_mix_g', l)], 'w_in', l, tag + 'w_in', carry=(0,),
                      host='win0' if l == 0 else None)
        q, k, v, u_conv, u_ssm = split_columns(p, (0, s1, s2, s3, s4, p.shape[1]))
        q = groupnorm(q, w['sb_q_norm_g'][l], SB_HEAD_DIM, tag + 'q_norm')
        k = groupnorm(k, w['sb_k_norm_g'][l], SB_HEAD_DIM, tag + 'k_norm')
        o_sb, gathered, handed = sb_attention(q, k, v, tag + 'sb', riders_at('sb%d' % l))
        take('sb%d' % l, gathered, handed)

        dw_w = jnp.pad(w['conv_dw_w'][l], ((0, CONV_HALO - CONV_WIDTH), (0, 0)))
        hc = dwconv(glu(u_conv, tag + 'conv_glu'), dw_w, w['conv_dw_b'][l].reshape(1, -1), tag + 'dwconv')
        o_conv, = fused_(ln_silu_block, [hc], [gain('conv_ln_g', l), gain('conv_ln_b', l)], 'conv_pw2_w', l,
                         tag + 'pw2')

        ar, ai, bbr, bbi, cr, ci = ssm_discretise(
            w['ssm_lam_re'][l], w['ssm_lam_im'][l], w['ssm_log_dt'][l], w['ssm_b_re'][l], w['ssm_b_im'][l],
            w['ssm_c_re'][l], w['ssm_c_im'][l])
        y = ssm_core(u_ssm, ar, ai, bbr, bbi, cr, ci, w['ssm_d'][l].reshape(1, -1), tag + 'ssm')
        o_ssm = glu(linear_(y, 'ssm_glu_w', l, tag + 'ssm_glu_w'), tag + 'ssm_glu')

        x, = fused_(branch_norms_block, [o_sb, o_conv, o_ssm], [gain('branch_norm_g', l)], 'w_out', l,
                    tag + 'w_out', residual=x)

        q_raw, x = fused_(_rms, [x], [gain('norm_xa_g', l)], 'xa_wq', l, tag + 'xa_wq', carry=(0,))
        hm = rmsnorm(mem, w['norm_mem_g'][l], tag + 'norm_mem')
        qx = groupnorm(q_raw, w['xa_q_norm_g'][l], XA_HEAD_DIM, tag + 'xa_qn')
        kx = groupnorm(linear_(hm, 'xa_wk', l, tag + 'xa_wk'), w['xa_k_norm_g'][l], XA_HEAD_DIM, tag + 'xa_kn')
        vx = linear_(hm, 'xa_wv', l, tag + 'xa_wv')
        x, = fused_(xa_core_block, [qx], [kx, vx], 'xa_wo', l, tag + 'xa_wo', residual=x,
                    block_rows=min(256, qx.shape[0]))

        host = 'ffn0' if l == 0 else None
        x, gathered, handed = ffn(x, gain('norm_ffn_g', l), mats[('ffn_w_in', l)], slot[('ffn_w_in', l)],
                                  mats[('ffn_w_out', l)], slot[('ffn_w_out', l)], tag + 'ffn',
                                  riders_at(host) if host else NO_RIDERS)
        if host:
            take(host, gathered, handed)
    return jnp.sum(loss_rows(x, target, 'loss'))


def local_step(w, mats, shards, x, mem, target):
    def shard_shape(key):
        return shards[key].shape if key in shards else _shard_shape(key[0], mats[key].shape)

    keys = [key for keys in SCATTER_AT.values() for key in keys] + list(OWN_AT.values())
    slots = {key: jnp.zeros((N_CHIPS,) + shard_shape(key), BF16) for key in keys}
    loss, (g_mats, g_w, gx) = jax.value_and_grad(local_loss, argnums=(0, 1, 4))(
        slots, w, mats, shards, x, mem, target)
    return loss, gx, g_w, g_mats


PACK_ROWS = 2048


PIECE_ROWS = 16


def _piece_rows(size):
    rows = -(-size // LANES)
    return rows, -(-rows // PIECE_ROWS) * PIECE_ROWS


def pack(arrays, dtype):
    parts, total = [], 0
    for a in arrays:
        rows, padded = _piece_rows(a.size)
        a = a.astype(dtype)
        if a.size % LANES:
            a = jnp.pad(a.reshape(-1), (0, rows * LANES - a.size))
        a = a.reshape(rows, LANES)
        if padded != rows:
            a = jnp.pad(a, ((0, padded - rows), (0, 0)))
        parts.append(a)
        total += padded
    tail = -total % PACK_ROWS
    if tail:
        parts.append(jnp.zeros((tail, LANES), dtype))
    return jnp.concatenate(parts, axis=0)


def unpack(packed, shapes):
    out, off = [], 0
    for s in shapes:
        size = math.prod(s)
        rows, padded = _piece_rows(size)
        piece = packed[off:off + rows]
        if size % LANES:
            piece = piece.reshape(-1)[:size]
        out.append(piece.reshape(s))
        off += padded
    return out


def _mesh_pos():
    return lax.axis_index("x"), lax.axis_index("y"), lax.axis_index("c")


def _exchange_xy(n_arrays, src_of, dst_of, sems, wait):
    send_sems, recv_sems, local_sems = sems
    x, y, c = _mesh_pos()
    me = 2 * x + y
    peers = [(1 - x, y), (x, 1 - y), (1 - x, 1 - y)]
    for k in range(n_arrays):
        own = pltpu.make_async_copy(src_of(k, me), dst_of(k, me), local_sems.at[k])
        if wait:
            own.wait()
        else:
            own.start()
        for p, (px, py) in enumerate(peers):
            out = pltpu.make_async_remote_copy(
                src_ref=src_of(k, 2 * px + py), dst_ref=dst_of(k, me), send_sem=send_sems.at[3 * k + p],
                recv_sem=recv_sems.at[3 * k + p], device_id=(px, py, c), device_id_type=MESH)
            if wait:
                pltpu.make_async_remote_copy(
                    src_ref=src_of(k, me), dst_ref=dst_of(k, 2 * px + py), send_sem=send_sems.at[3 * k + p],
                    recv_sem=recv_sems.at[3 * k + p], device_id=(px, py, c), device_id_type=MESH).wait_recv()
                out.wait_send()
            else:
                out.start()


class Ride:
    def __init__(self, arrays, out_shapes, src_of, dst_of):
        self.arrays, self.out_shapes = list(arrays), list(out_shapes)
        self._src_of, self._dst_of = src_of, dst_of
        n = len(self.arrays)
        self.in_specs = [pl.BlockSpec(memory_space=pl.ANY)] * n
        self.out_specs = [pl.BlockSpec(memory_space=pl.ANY)] * len(self.out_shapes)
        self.out_shape = [jax.ShapeDtypeStruct(s, BF16) for s in self.out_shapes]
        self.scratch = [pltpu.SemaphoreType.DMA((3 * n,)), pltpu.SemaphoreType.DMA((3 * n,)),
                        pltpu.SemaphoreType.DMA((n,))]

    def run(self, parts, wait):
        ins, outs, sems = parts
        _exchange_xy(len(self.arrays), lambda k, chip: self._src_of(ins, k, chip),
                     lambda k, chip: self._dst_of(outs, k, chip), sems, wait)

    def at_ends(self, parts, first, last):
        pl.when(first)(lambda: self.run(parts, False))

        def finish():
            pl.when(last)(lambda: self.run(parts, True))
        return finish


def _split_refs(ride, n_in, n_out, refs):
    if ride is None:
        return refs[:n_in], refs[n_in:n_in + n_out], refs[n_in + n_out:], None
    ni, no = len(ride.arrays), len(ride.out_shapes)
    b = n_in + ni
    c = b + n_out
    d = c + no
    return refs[:n_in], refs[b:c], refs[d:len(refs) - 3], (refs[n_in:b], refs[c:d], refs[len(refs) - 3:])


def run_ride(ride, name):
    def body(*refs):
        parts = _split_refs(ride, 0, 0, refs)[3]
        ride.run(parts, False)
        ride.run(parts, True)

    return pl.pallas_call(
        body, name=name, in_specs=ride.in_specs, out_specs=ride.out_specs, out_shape=ride.out_shape,
        scratch_shapes=ride.scratch, compiler_params=pltpu.CompilerParams(has_side_effects=True),
    )(*ride.arrays)


SLOTTED = ('w_in', 'taps')


def _part(ref, axis, chip, size):
    start = pl.multiple_of(chip * size, size)
    index = [slice(None)] * len(ref.shape)
    index[axis] = pl.ds(start, size)
    return ref.at[tuple(index)]


def _whole_shape(name, shard_shape):
    s = list(shard_shape)
    s[SHARD_AXIS[name] - 1] *= N_CHIPS
    return tuple(s)


def _shard_shape(name, whole_shape):
    s = list(whole_shape)
    s[SHARD_AXIS[name] - 1] //= N_CHIPS
    return tuple(s)


def gather_ride(names, shards):
    def out_shape(k):
        return (N_CHIPS,) + shards[k].shape if names[k] in SLOTTED else _whole_shape(names[k], shards[k].shape)

    def dst_of(outs, k, chip):
        if names[k] in SLOTTED:
            return outs[k].at[chip]
        axis = SHARD_AXIS[names[k]] - 1
        return _part(outs[k], axis, chip, shards[k].shape[axis])

    return Ride(shards, [out_shape(k) for k in range(len(shards))], lambda ins, k, chip: ins[k], dst_of)


def scatter_ride(names, grads):
    def shard_shape(k):
        return grads[k].shape[1:] if names[k] in SLOTTED else _shard_shape(names[k], grads[k].shape)

    def src_of(ins, k, chip):
        if names[k] in SLOTTED:
            return ins[k].at[chip]
        axis = SHARD_AXIS[names[k]] - 1
        return _part(ins[k], axis, chip, shard_shape(k)[axis])

    return Ride(grads, [(N_CHIPS,) + shard_shape(k) for k in range(len(grads))], src_of,
                lambda outs, k, chip: outs[k].at[chip])


def _cut_for_chips(name, g):
    if name not in SLOTTED:
        return g
    axis = SHARD_AXIS[name] - 1
    cut = g.shape[:axis] + (N_CHIPS, g.shape[axis] // N_CHIPS) + g.shape[axis + 1:]
    return jnp.moveaxis(g.reshape(cut), axis, 0)


class SwapRide(Ride):
    def __init__(self, arrays):
        self.arrays = list(arrays)
        self.out_shapes = [a.shape for a in self.arrays]
        n = len(self.arrays)
        self.in_specs = [pl.BlockSpec(memory_space=pl.ANY)] * n
        self.out_specs = [pl.BlockSpec(memory_space=pl.ANY)] * n
        self.out_shape = [jax.ShapeDtypeStruct(a.shape, a.dtype) for a in self.arrays]
        self.scratch = [pltpu.SemaphoreType.DMA((n,)), pltpu.SemaphoreType.DMA((n,)), pltpu.SemaphoreType.DMA((1,))]

    def run(self, parts, wait):
        ins, outs, (send_sems, recv_sems, _) = parts
        x, y, c = _mesh_pos()
        for k in range(len(self.arrays)):
            cp = pltpu.make_async_remote_copy(
                src_ref=ins[k], dst_ref=outs[k], send_sem=send_sems.at[k], recv_sem=recv_sems.at[k],
                device_id=(x, y, 1 - c), device_id_type=MESH)
            if wait:
                cp.wait()
            else:
                cp.start()


def allreduce_small(buf, name):
    R = buf.shape[0]
    half = R // 2

    def body(in_ref, sum_ref, other_ref, chip_ref, got_ref, send_sems, recv_sems):
        x, y, c = _mesh_pos()
        sibling = (x, y, 1 - c)
        peers = [(1 - x, y, c), (x, 1 - y, c), (1 - x, 1 - y, c)]

        def copy(k, src, dst, to):
            return pltpu.make_async_remote_copy(src_ref=src, dst_ref=dst, send_sem=send_sems.at[k],
                                                recv_sem=recv_sems.at[k], device_id=to, device_id_type=MESH)

        swap = copy(0, in_ref, other_ref, sibling)
        swap.start()
        swap.wait()
        chip_ref[...] = in_ref[...] + other_ref[...]

        mine = chip_ref.at[pl.ds(pl.multiple_of(c * half, SUBLANES), half), :]
        sends = [copy(1 + k, mine, got_ref.at[k], to) for k, to in enumerate(peers)]
        for cp in sends:
            cp.start()
        for cp in sends:
            cp.wait()
        total = (mine[...] + got_ref[0]) + (got_ref[1] + got_ref[2])

        done = sum_ref.at[pl.ds(pl.multiple_of(c * half, SUBLANES), half), :]
        done[...] = total
        back = copy(4, done, done, sibling)
        back.start()
        back.wait()

    vmem = pl.BlockSpec(memory_space=pltpu.VMEM)
    return pl.pallas_call(
        body, name=name, in_specs=[vmem], out_specs=vmem,
        out_shape=jax.ShapeDtypeStruct((R, LANES), F32),
        scratch_shapes=[pltpu.VMEM((R, LANES), F32), pltpu.VMEM((R, LANES), F32),
                        pltpu.VMEM((3, half, LANES), F32),
                        pltpu.SemaphoreType.DMA((5,)), pltpu.SemaphoreType.DMA((5,))],
        compiler_params=pltpu.CompilerParams(has_side_effects=True, vmem_limit_bytes=VMEM_LIMIT),
    )(buf)


def _adamw_update(g, w, m, v):
    m2 = ADAM_B1 * m + (1.0 - ADAM_B1) * g
    v2 = ADAM_B2 * v + (1.0 - ADAM_B2) * (g * g)
    m_hat = m2 / (1.0 - ADAM_B1 ** ADAM_STEP)
    v_hat = v2 / (1.0 - ADAM_B2 ** ADAM_STEP)
    return -ADAM_LR * (m_hat / (jnp.sqrt(v_hat) + ADAM_EPS) + ADAM_WD * w), m2, v2


def adamw_matrix(layer, mine, other, w, m, v, so_far, name):
    _, rows, cols = w.shape
    T = 16
    while rows % (2 * T) == 0 and 2 * T * cols <= 128 * 1024:
        T *= 2

    def body(mine_ref, other_ref, w_ref, m_ref, v_ref, *rest):
        g_out, d_out, m_out, v_out = rest[-4:]

        def total(ref):
            acc = ref[0].astype(F32)
            for k in range(1, N_CHIPS):
                acc = acc + ref[k].astype(F32)
            return acc

        g = total(mine_ref) + total(other_ref)
        g_out[...] = g
        d_out[...], m_out[...], v_out[...] = _adamw_update(g, w_ref[...], m_ref[...], v_ref[...])

    slots = pl.BlockSpec((N_CHIPS, T, cols), lambda i: (0, i, 0))
    spec = pl.BlockSpec((None, T, cols), lambda i: (layer, i, 0))
    filled = [] if so_far is None else list(so_far)
    return pl.pallas_call(
        body, name=name, grid=(rows // T,),
        in_specs=[slots, slots, spec, spec, spec] + [pl.BlockSpec(memory_space=pl.ANY)] * len(filled),
        out_specs=[spec] * 4,
        out_shape=[jax.ShapeDtypeStruct(w.shape, F32)] * 4,
        input_output_aliases={5 + j: j for j in range(len(filled))},
        compiler_params=_cparams(("parallel",)),
    )(mine, other, w, m, v, *filled)


def adamw_small(gs, ws, ms, vs, name):
    n = len(gs)

    def body(*refs):
        for k in range(n):
            g, w, m, v = (refs[j * n + k][...] for j in range(4))
            d_out, m_out, v_out = (refs[(4 + j) * n + k] for j in range(3))
            d_out[...], m_out[...], v_out[...] = _adamw_update(g, w, m, v)

    vmem = pl.BlockSpec(memory_space=pltpu.VMEM)
    outs = pl.pallas_call(
        body, name=name,
        in_specs=[vmem] * (4 * n), out_specs=[vmem] * (3 * n),
        out_shape=[jax.ShapeDtypeStruct(w.shape, F32) for w in ws] * 3,
        compiler_params=_cparams(),
    )(*gs, *ws, *ms, *vs)
    return outs[:n], outs[n:2 * n], outs[2 * n:]


def _shard_of(full, axis, chip):
    size = full.shape[axis] // N_CHIPS
    return lax.slice_in_dim(full, chip * size, (chip + 1) * size, axis=axis)


def kernel(x, mem, norm_mix_g, w_in, sb_q_norm_g, sb_k_norm_g, conv_dw_w, conv_dw_b, conv_ln_g, conv_ln_b, conv_pw2_w, ssm_lam_re, ssm_lam_im, ssm_log_dt, ssm_b_re, ssm_b_im, ssm_c_re, ssm_c_im, ssm_d, ssm_glu_w, branch_norm_g, w_out, norm_xa_g, norm_mem_g, xa_wq, xa_wk, xa_wv, xa_q_norm_g, xa_k_norm_g, xa_wo, norm_ffn_g, ffn_w_in, ffn_w_out, loss_target, m_norm_mix_g, m_w_in, m_sb_q_norm_g, m_sb_k_norm_g, m_conv_dw_w, m_conv_dw_b, m_conv_ln_g, m_conv_ln_b, m_conv_pw2_w, m_ssm_lam_re, m_ssm_lam_im, m_ssm_log_dt, m_ssm_b_re, m_ssm_b_im, m_ssm_c_re, m_ssm_c_im, m_ssm_d, m_ssm_glu_w, m_branch_norm_g, m_w_out, m_norm_xa_g, m_norm_mem_g, m_xa_wq, m_xa_wk, m_xa_wv, m_xa_q_norm_g, m_xa_k_norm_g, m_xa_wo, m_norm_ffn_g, m_ffn_w_in, m_ffn_w_out, v_norm_mix_g, v_w_in, v_sb_q_norm_g, v_sb_k_norm_g, v_conv_dw_w, v_conv_dw_b, v_conv_ln_g, v_conv_ln_b, v_conv_pw2_w, v_ssm_lam_re, v_ssm_lam_im, v_ssm_log_dt, v_ssm_b_re, v_ssm_b_im, v_ssm_c_re, v_ssm_c_im, v_ssm_d, v_ssm_glu_w, v_branch_norm_g, v_w_out, v_norm_xa_g, v_norm_mem_g, v_xa_wq, v_xa_wk, v_xa_wv, v_xa_q_norm_g, v_xa_k_norm_g, v_xa_wo, v_norm_ffn_g, v_ffn_w_in, v_ffn_w_out):
    given = dict(locals())
    w = {n: given[n] for n in WEIGHTS}
    m = {n: given['m_' + n] for n in WEIGHTS}
    v = {n: given['v_' + n] for n in WEIGHTS}

    depth = w_in.shape[0]
    chip = 2 * lax.axis_index("x") + lax.axis_index("y")

    assert depth == DEPTH
    taps_bits = lax.bitcast_convert_type(conv_dw_w, BF16)
    shards = {(n, l): w[n][l].astype(BF16) for n in MATRICES for l in range(depth)}
    first = [shards.pop(key) for key in GATHER_AT['start']]
    gathered = run_ride(gather_ride([n for n, _ in GATHER_AT['start']] + ['taps'], first + [taps_bits]),
                        'gather_first')
    mats = {key: _assemble(key[0], g) for key, g in zip(GATHER_AT['start'], gathered)}
    taps = jnp.concatenate([lax.bitcast_convert_type(gathered[-1][j], F32) for j in range(N_CHIPS)], axis=2)
    local_w = {n: w[n] for n in REPLICATED}
    local_w['conv_dw_w'] = taps

    loss, gx, g_w, mine = local_step(local_w, mats, shards, x[0], mem[0], loss_target[0])

    keys = [(n, l) for n in MATRICES for l in range(depth)]
    other = dict(zip(keys, run_ride(SwapRide([mine[key] for key in keys]), 'swap_cores')))
    outs = {}
    for n in MATRICES:
        res = None
        for l in range(depth):
            res = adamw_matrix(l, mine[(n, l)], other[(n, l)], w[n], m[n], v[n], res, 'adamw_%s_%d' % (n, l))
        outs['grad_' + n], outs['delta_' + n], outs['new_m_' + n], outs['new_v_' + n] = res

    reduced = allreduce_small(pack([g_w[n] for n in REPLICATED] + [g_w['conv_dw_w'], loss.reshape(1)], F32),
                              'allreduce_small')
    reduced = unpack(reduced, [w[n].shape for n in REPLICATED] + [taps.shape, (1,)])
    total_loss = reduced[-1].reshape(())
    tap_cols = conv_dw_w.shape[2]
    reduced[-2] = lax.dynamic_slice_in_dim(reduced[-2], chip * tap_cols, tap_cols, axis=2)
    small_names = REPLICATED + ['conv_dw_w']
    deltas, new_ms, new_vs = adamw_small(reduced[:-1], [w[n] for n in small_names], [m[n] for n in small_names],
                                         [v[n] for n in small_names], 'adamw_small')
    for k, n in enumerate(small_names):
        outs['grad_' + n], outs['delta_' + n] = reduced[k], deltas[k]
        outs['new_m_' + n], outs['new_v_' + n] = new_ms[k], new_vs[k]
    return (total_loss, gx[None], *[outs['grad_' + n] for n in WEIGHTS], *[outs['delta_' + n] for n in WEIGHTS],
            *[outs['new_m_' + n] for n in WEIGHTS], *[outs['new_v_' + n] for n in WEIGHTS])
```
